```python
import math
import jax, jax.numpy as jnp
from jax import lax
import numpy as np

D_MODEL = 1024
BATCH = 8
SEQ = 16384
DEPTH = 4

N_A = DEPTH // 2
N_B = DEPTH - N_A
MEM_TOKENS = 256
MEM_HEADS = 4
MEM_DH = D_MODEL // 16
MEM_W = MEM_HEADS * MEM_DH
DN_DK = 128
DN_DV = 128
DN_HEADS = (3 * D_MODEL) // (4 * DN_DV)
DN_QK_W = DN_HEADS * DN_DK
DN_V_W = DN_HEADS * DN_DV
CONV_WIDTH = 4
CHUNK = 64
SWA_DH = 64
SWA_HEADS = (3 * D_MODEL) // (4 * SWA_DH)
SWA_KV_HEADS = 2
SWA_GROUP = SWA_HEADS // SWA_KV_HEADS
SWA_Q_W = SWA_HEADS * SWA_DH
SWA_KV_W = SWA_KV_HEADS * SWA_DH
WINDOW = 128
ROPE_THETA = 10000.0
MLP_HIDDEN = 4 * D_MODEL
LN_EPS = 1e-5
NORM_EPS = 1e-6
DN_ALPHA = (2.0 * DEPTH) ** 0.25
DN_BETA = (8.0 * DEPTH) ** -0.25
A_IN = 2 * DN_QK_W + 2 * DN_V_W + 2 * DN_HEADS + MEM_W
B_IN = SWA_Q_W + MEM_W
MIX_W = DN_V_W + MEM_W

kernel_name = "yoco_deltanet_swa_sink_memory_trunk"


def layer_norm(x, g, b):
    xf = x.astype(jnp.float32)
    mu = jnp.mean(xf, axis=-1, keepdims=True)
    var = jnp.mean(jnp.square(xf - mu), axis=-1, keepdims=True)
    y = (xf - mu) * lax.rsqrt(var + LN_EPS) * g.astype(jnp.float32) + b.astype(jnp.float32)
    return y.astype(x.dtype)


def l2_normalize(x):
    xf = x.astype(jnp.float32)
    return xf * lax.rsqrt(jnp.sum(xf * xf, axis=-1, keepdims=True) + NORM_EPS)


def rope_tables(positions, dh):
    inv_freq = ROPE_THETA ** (-jnp.arange(0, dh, 2, dtype=jnp.float32) / dh)
    ang = positions.astype(jnp.float32)[..., None] * inv_freq
    return jnp.cos(ang)[:, :, None, :], jnp.sin(ang)[:, :, None, :]


def apply_rope(x, cos, sin):
    xf = x.astype(jnp.float32)
    x1, x2 = jnp.split(xf, 2, axis=-1)
    out = jnp.concatenate([x1 * cos - x2 * sin, x2 * cos + x1 * sin], axis=-1)
    return out.astype(x.dtype)


def causal_depthwise_conv(x, w):
    C = x.shape[-1]
    return lax.conv_general_dilated(
        x, w[:, None, :].astype(x.dtype), window_strides=(1,),
        padding=[(CONV_WIDTH - 1, 0)], dimension_numbers=('NWC', 'WIO', 'NWC'),
        feature_group_count=C)


def gated_delta_rule(q, k, v, g, beta):
    B, S, H, DK = q.shape
    DV = v.shape[-1]
    N = S // CHUNK
    f32 = jnp.float32

    def to_chunks(t):
        t = t.astype(f32).reshape((B, N, CHUNK, H) + t.shape[3:])
        return jnp.moveaxis(t, 3, 1)

    q = to_chunks(q) * (DK ** -0.5)
    k = to_chunks(k)
    v = to_chunks(v)
    beta = to_chunks(beta)
    g = jnp.cumsum(to_chunks(g), axis=-1)
    incl = jnp.tril(jnp.ones((CHUNK, CHUNK), bool))
    strict = jnp.tril(jnp.ones((CHUNK, CHUNK), bool), -1)
    decay = jnp.exp(jnp.where(incl, g[..., :, None] - g[..., None, :], -jnp.inf))
    k_beta = k * beta[..., None]
    L = jnp.where(strict, jnp.einsum('bhnid,bhnjd->bhnij', k_beta, k) * decay, 0.0)
    rhs = jnp.concatenate([v * beta[..., None], k_beta * jnp.exp(g)[..., None]], axis=-1)
    sol = lax.linalg.triangular_solve(L, rhs, left_side=True, lower=True, unit_diagonal=True)
    u, w = sol[..., :DV], sol[..., DV:]
    intra = jnp.where(incl, jnp.einsum('bhnid,bhnjd->bhnij', q, k) * decay, 0.0)
    q_dec = q * jnp.exp(g)[..., None]
    k_dec = k * jnp.exp(g[..., -1:] - g)[..., None]
    chunk_decay = jnp.exp(g[..., -1])

    def step(state, inp):
        qd, kd, uc, wc, ac, cd = inp
        v_new = uc - jnp.einsum('bhik,bhkv->bhiv', wc, state)
        out = jnp.einsum('bhik,bhkv->bhiv', qd, state) + jnp.einsum('bhij,bhjv->bhiv', ac, v_new)
        state = state * cd[..., None, None] + jnp.einsum('bhik,bhiv->bhkv', kd, v_new)
        return state, out

    xs = (jnp.moveaxis(q_dec, 2, 0), jnp.moveaxis(k_dec, 2, 0), jnp.moveaxis(u, 2, 0),
          jnp.moveaxis(w, 2, 0), jnp.moveaxis(intra, 2, 0), jnp.moveaxis(chunk_decay, 2, 0))
    s0 = jnp.zeros((B, H, DK, DV), f32)
    _, out = lax.scan(step, s0, xs)
    return jnp.transpose(out, (1, 0, 3, 2, 4)).reshape(B, S, H, DV)


def sliding_window_sink_attention(q, k, v, sinks):
    B, S, HKV, G, dh = q.shape
    NB = S // WINDOW
    qb = q.reshape(B, NB, WINDOW, HKV, G, dh)

    def band_keys(t):
        tb = t.reshape(B, NB, WINDOW, HKV, dh)
        prev = jnp.pad(tb, ((0, 0), (1, 0), (0, 0), (0, 0), (0, 0)))[:, :-1]
        return jnp.concatenate([prev, tb], axis=2)

    kk = band_keys(k)
    vv = band_keys(v)
    s = jnp.einsum('bnqhgd,bnkhd->bnhgqk', qb, kk).astype(jnp.float32) * (dh ** -0.5)
    qi = jnp.arange(WINDOW)[:, None]
    kj = jnp.arange(2 * WINDOW)[None, :]
    diff = qi + WINDOW - kj
    band = (diff >= 0) & (diff < WINDOW)
    key_pos = jnp.arange(NB)[:, None] * WINDOW - WINDOW + kj
    valid = band[None] & (key_pos >= 0)[:, None, :]
    s = jnp.where(valid[None, :, None, None], s, -jnp.inf)
    sink = sinks.astype(jnp.float32).reshape(HKV, G)[None, None, :, :, None, None]
    m = jnp.maximum(jnp.max(s, axis=-1, keepdims=True), sink)
    p = jnp.exp(s - m)
    denom = jnp.sum(p, axis=-1, keepdims=True) + jnp.exp(sink - m)
    probs = (p / denom).astype(v.dtype)
    o = jnp.einsum('bnhgqk,bnkhd->bnqhgd', probs, vv)
    return o.reshape(B, S, HKV * G * dh)


def memory_cross_attention(qm, mem, w_kv):
    B, S, _ = qm.shape
    M = mem.shape[1]
    kv = mem @ w_kv
    k = kv[..., :MEM_W].reshape(B, M, MEM_HEADS, MEM_DH)
    v = kv[..., MEM_W:].reshape(B, M, MEM_HEADS, MEM_DH)
    q = qm.reshape(B, S, MEM_HEADS, MEM_DH)
    s = jnp.einsum('bshd,bmhd->bhsm', q, k).astype(jnp.float32) * (MEM_DH ** -0.5)
    p = jax.nn.softmax(s, axis=-1).astype(v.dtype)
    return jnp.einsum('bhsm,bmhd->bshd', p, v).reshape(B, S, MEM_W)


def mixer_a(h, mem, w_in, conv_w, A_log, dt_bias, norm_w, mem_w_kv, w_o):
    B, S, _ = h.shape
    proj = h @ w_in
    c1 = 2 * DN_QK_W + DN_V_W
    qkv = proj[..., :c1]
    z = proj[..., c1:c1 + DN_V_W]
    a = proj[..., c1 + DN_V_W:c1 + DN_V_W + DN_HEADS]
    b = proj[..., c1 + DN_V_W + DN_HEADS:c1 + DN_V_W + 2 * DN_HEADS]
    qm = proj[..., c1 + DN_V_W + 2 * DN_HEADS:]
    qkv = jax.nn.silu(causal_depthwise_conv(qkv, conv_w))
    q = l2_normalize(qkv[..., :DN_QK_W].reshape(B, S, DN_HEADS, DN_DK))
    k = l2_normalize(qkv[..., DN_QK_W:2 * DN_QK_W].reshape(B, S, DN_HEADS, DN_DK))
    v = qkv[..., 2 * DN_QK_W:].reshape(B, S, DN_HEADS, DN_DV)
    beta = jax.nn.sigmoid(b.astype(jnp.float32))
    g = -jnp.exp(A_log.astype(jnp.float32)) * jax.nn.softplus(a.astype(jnp.float32) + dt_bias.astype(jnp.float32))
    o = gated_delta_rule(q, k, v, g, beta)
    o = o * lax.rsqrt(jnp.mean(o * o, axis=-1, keepdims=True) + NORM_EPS) * norm_w.astype(jnp.float32)
    o = o * jax.nn.silu(z.astype(jnp.float32).reshape(B, S, DN_HEADS, DN_DV))
    o = o.astype(h.dtype).reshape(B, S, DN_V_W)
    mo = memory_cross_attention(qm, mem, mem_w_kv)
    return jnp.concatenate([o, mo], axis=-1) @ w_o


def mixer_b(h, mem, k_sh, v_sh, cos, sin, w_in, sinks, mem_w_kv, w_o):
    B, S, _ = h.shape
    proj = h @ w_in
    q = apply_rope(proj[..., :SWA_Q_W].reshape(B, S, SWA_HEADS, SWA_DH), cos, sin)
    q = q.reshape(B, S, SWA_KV_HEADS, SWA_GROUP, SWA_DH)
    o = sliding_window_sink_attention(q, k_sh, v_sh, sinks)
    mo = memory_cross_attention(proj[..., SWA_Q_W:], mem, mem_w_kv)
    return jnp.concatenate([o, mo], axis=-1) @ w_o


def shared_kv(h, w_kv, cos, sin):
    B, S, _ = h.shape
    kv = h @ w_kv
    k = apply_rope(kv[..., :SWA_KV_W].reshape(B, S, SWA_KV_HEADS, SWA_DH), cos, sin)
    v = kv[..., SWA_KV_W:].reshape(B, S, SWA_KV_HEADS, SWA_DH)
    return k, v


def sq_relu_mlp(h, w_up, w_down):
    return jnp.square(jax.nn.relu(h @ w_up)) @ w_down


def _fwd_setup_inputs(seed: int = 0) -> dict:
    key = jax.random.key(seed)
    ks = jax.random.split(key, 20)
    f32 = jnp.float32

    def dense(k, shape, fan_in, scale=1.0):
        return jax.random.normal(k, shape, f32) * (fan_in ** -0.5) * scale

    x = jax.random.normal(ks[0], (BATCH, SEQ, D_MODEL), f32)
    mem = jax.random.normal(ks[1], (BATCH, MEM_TOKENS, D_MODEL), f32)
    positions = (jax.random.randint(ks[2], (BATCH, 1), 0, 4096, jnp.int32)
                 + jnp.arange(SEQ, dtype=jnp.int32)[None, :])
    a_w_in = dense(ks[3], (N_A, D_MODEL, A_IN), D_MODEL)
    a_conv_w = jax.random.normal(ks[4], (N_A, CONV_WIDTH, 2 * DN_QK_W + DN_V_W), f32) * (CONV_WIDTH ** -0.5)
    a_A_log = jnp.log(jax.random.uniform(ks[5], (N_A, DN_HEADS), f32, 1.0, 16.0))
    dt = jnp.exp(jax.random.uniform(ks[6], (N_A, DN_HEADS), f32, math.log(1e-3), math.log(1e-1)))
    a_dt_bias = dt + jnp.log(-jnp.expm1(-dt))
    a_norm_w = 1.0 + 0.02 * jax.random.normal(ks[7], (N_A, DN_DV), f32)
    b_w_in = dense(ks[8], (N_B, D_MODEL, B_IN), D_MODEL)
    b_sinks = 0.5 * jax.random.normal(ks[9], (N_B, SWA_HEADS), f32)
    w_kv_shared = dense(ks[10], (D_MODEL, 2 * SWA_KV_W), D_MODEL)
    mem_w_kv = dense(ks[11], (DEPTH, D_MODEL, 2 * MEM_W), D_MODEL)
    w_o = dense(ks[12], (DEPTH, MIX_W, D_MODEL), MIX_W, DN_BETA)
    mlp_w_up = dense(ks[13], (DEPTH, D_MODEL, MLP_HIDDEN), D_MODEL)
    mlp_w_down = dense(ks[14], (DEPTH, MLP_HIDDEN, D_MODEL), MLP_HIDDEN, DN_BETA)
    ln_g = 1.0 + 0.02 * jax.random.normal(ks[15], (DEPTH, 2, D_MODEL), f32)
    ln_b = 0.02 * jax.random.normal(ks[16], (DEPTH, 2, D_MODEL), f32)
    return {"x": x, "mem": mem, "positions": positions, "a_w_in": a_w_in, "a_conv_w": a_conv_w,
            "a_A_log": a_A_log, "a_dt_bias": a_dt_bias, "a_norm_w": a_norm_w, "b_w_in": b_w_in,
            "b_sinks": b_sinks, "w_kv_shared": w_kv_shared, "mem_w_kv": mem_w_kv, "w_o": w_o,
            "mlp_w_up": mlp_w_up, "mlp_w_down": mlp_w_down, "ln_g": ln_g, "ln_b": ln_b}


def _fwd_reference(x, mem, positions, a_w_in, a_conv_w, a_A_log, a_dt_bias, a_norm_w, b_w_in,
              b_sinks, w_kv_shared, mem_w_kv, w_o, mlp_w_up, mlp_w_down, ln_g, ln_b):
    cos, sin = rope_tables(positions, SWA_DH)
    h = x
    k_sh = None
    v_sh = None
    for layer in range(DEPTH):
        if layer < N_A:
            mix = mixer_a(h, mem, a_w_in[layer], a_conv_w[layer], a_A_log[layer], a_dt_bias[layer],
                          a_norm_w[layer], mem_w_kv[layer], w_o[layer])
        else:
            j = layer - N_A
            mix = mixer_b(h, mem, k_sh, v_sh, cos, sin, b_w_in[j], b_sinks[j], mem_w_kv[layer], w_o[layer])
        h = layer_norm(DN_ALPHA * h + mix, ln_g[layer, 0], ln_b[layer, 0])
        h = layer_norm(DN_ALPHA * h + sq_relu_mlp(h, mlp_w_up[layer], mlp_w_down[layer]),
                       ln_g[layer, 1], ln_b[layer, 1])
        if layer == N_A - 1:
            k_sh, v_sh = shared_kv(h, w_kv_shared, cos, sin)
    return h


import jax as _jax
import jax.numpy as _jnp

TWIN_FORMAT = 'train_step'
FWD_PARAMS = ['x', 'mem', 'positions', 'a_w_in', 'a_conv_w', 'a_A_log', 'a_dt_bias', 'a_norm_w', 'b_w_in', 'b_sinks', 'w_kv_shared', 'mem_w_kv', 'w_o', 'mlp_w_up', 'mlp_w_down', 'ln_g', 'ln_b']
TWIN_WEIGHTS = ['a_w_in', 'a_conv_w', 'a_A_log', 'a_dt_bias', 'a_norm_w', 'b_w_in', 'b_sinks', 'w_kv_shared', 'mem_w_kv', 'w_o', 'mlp_w_up', 'mlp_w_down', 'ln_g', 'ln_b']
TWIN_DIFF_INPUT = 'x'
TWIN_INPUTS = ['x', 'mem', 'positions', 'a_w_in', 'a_conv_w', 'a_A_log', 'a_dt_bias', 'a_norm_w', 'b_w_in', 'b_sinks', 'w_kv_shared', 'mem_w_kv', 'w_o', 'mlp_w_up', 'mlp_w_down', 'ln_g', 'ln_b', 'loss_target', 'm_a_w_in', 'm_a_conv_w', 'm_a_A_log', 'm_a_dt_bias', 'm_a_norm_w', 'm_b_w_in', 'm_b_sinks', 'm_w_kv_shared', 'm_mem_w_kv', 'm_w_o', 'm_mlp_w_up', 'm_mlp_w_down', 'm_ln_g', 'm_ln_b', 'v_a_w_in', 'v_a_conv_w', 'v_a_A_log', 'v_a_dt_bias', 'v_a_norm_w', 'v_b_w_in', 'v_b_sinks', 'v_w_kv_shared', 'v_mem_w_kv', 'v_w_o', 'v_mlp_w_up', 'v_mlp_w_down', 'v_ln_g', 'v_ln_b']
TWIN_OUTPUTS = ['loss', 'grad_x', 'grad_a_w_in', 'grad_a_conv_w', 'grad_a_A_log', 'grad_a_dt_bias', 'grad_a_norm_w', 'grad_b_w_in', 'grad_b_sinks', 'grad_w_kv_shared', 'grad_mem_w_kv', 'grad_w_o', 'grad_mlp_w_up', 'grad_mlp_w_down', 'grad_ln_g', 'grad_ln_b', 'delta_a_w_in', 'delta_a_conv_w', 'delta_a_A_log', 'delta_a_dt_bias', 'delta_a_norm_w', 'delta_b_w_in', 'delta_b_sinks', 'delta_w_kv_shared', 'delta_mem_w_kv', 'delta_w_o', 'delta_mlp_w_up', 'delta_mlp_w_down', 'delta_ln_g', 'delta_ln_b', 'new_m_a_w_in', 'new_m_a_conv_w', 'new_m_a_A_log', 'new_m_a_dt_bias', 'new_m_a_norm_w', 'new_m_b_w_in', 'new_m_b_sinks', 'new_m_w_kv_shared', 'new_m_mem_w_kv', 'new_m_w_o', 'new_m_mlp_w_up', 'new_m_mlp_w_down', 'new_m_ln_g', 'new_m_ln_b', 'new_v_a_w_in', 'new_v_a_conv_w', 'new_v_a_A_log', 'new_v_a_dt_bias', 'new_v_a_norm_w', 'new_v_b_w_in', 'new_v_b_sinks', 'new_v_w_kv_shared', 'new_v_mem_w_kv', 'new_v_w_o', 'new_v_mlp_w_up', 'new_v_mlp_w_down', 'new_v_ln_g', 'new_v_ln_b']
TWIN_LEAF_KINDS = {'loss': 'loss', 'grad_x': 'grad_x', 'grad_a_w_in': 'grad_w', 'grad_a_conv_w': 'grad_w', 'grad_a_A_log': 'grad_w', 'grad_a_dt_bias': 'grad_w', 'grad_a_norm_w': 'grad_w', 'grad_b_w_in': 'grad_w', 'grad_b_sinks': 'grad_w', 'grad_w_kv_shared': 'grad_w', 'grad_mem_w_kv': 'grad_w', 'grad_w_o': 'grad_w', 'grad_mlp_w_up': 'grad_w', 'grad_mlp_w_down': 'grad_w', 'grad_ln_g': 'grad_w', 'grad_ln_b': 'grad_w', 'delta_a_w_in': 'delta_w', 'delta_a_conv_w': 'delta_w', 'delta_a_A_log': 'delta_w', 'delta_a_dt_bias': 'delta_w', 'delta_a_norm_w': 'delta_w', 'delta_b_w_in': 'delta_w', 'delta_b_sinks': 'delta_w', 'delta_w_kv_shared': 'delta_w', 'delta_mem_w_kv': 'delta_w', 'delta_w_o': 'delta_w', 'delta_mlp_w_up': 'delta_w', 'delta_mlp_w_down': 'delta_w', 'delta_ln_g': 'delta_w', 'delta_ln_b': 'delta_w', 'new_m_a_w_in': 'new_m', 'new_m_a_conv_w': 'new_m', 'new_m_a_A_log': 'new_m', 'new_m_a_dt_bias': 'new_m', 'new_m_a_norm_w': 'new_m', 'new_m_b_w_in': 'new_m', 'new_m_b_sinks': 'new_m', 'new_m_w_kv_shared': 'new_m', 'new_m_mem_w_kv': 'new_m', 'new_m_w_o': 'new_m', 'new_m_mlp_w_up': 'new_m', 'new_m_mlp_w_down': 'new_m', 'new_m_ln_g': 'new_m', 'new_m_ln_b': 'new_m', 'new_v_a_w_in': 'new_v', 'new_v_a_conv_w': 'new_v', 'new_v_a_A_log': 'new_v', 'new_v_a_dt_bias': 'new_v', 'new_v_a_norm_w': 'new_v', 'new_v_b_w_in': 'new_v', 'new_v_b_sinks': 'new_v', 'new_v_w_kv_shared': 'new_v', 'new_v_mem_w_kv': 'new_v', 'new_v_w_o': 'new_v', 'new_v_mlp_w_up': 'new_v', 'new_v_mlp_w_down': 'new_v', 'new_v_ln_g': 'new_v', 'new_v_ln_b': 'new_v'}


def _forward(args):
    return _fwd_reference(*[args[k] for k in FWD_PARAMS])


def _output_shape():
    def fwd():
        inp = _fwd_setup_inputs(0)
        return _fwd_reference(*[inp[k] for k in FWD_PARAMS])
    out = _jax.eval_shape(fwd)
    return out.shape, out.dtype

N_MICROBATCH = 1
ADAM_LR = 0.001
ADAM_B1 = 0.9
ADAM_B2 = 0.999
ADAM_EPS = 1e-08
ADAM_WD = 0.01
ADAM_STEP = 10
PER_EXAMPLE_BATCH_AXIS = {'x': 0, 'mem': 0, 'positions': 0, 'loss_target': 0}
SHARED_INPUTS = []
_WEIGHT_DTYPES = {'a_w_in': _jnp.float32, 'a_conv_w': _jnp.float32, 'a_A_log': _jnp.float32, 'a_dt_bias': _jnp.float32, 'a_norm_w': _jnp.float32, 'b_w_in': _jnp.float32, 'b_sinks': _jnp.float32, 'w_kv_shared': _jnp.float32, 'mem_w_kv': _jnp.float32, 'w_o': _jnp.float32, 'mlp_w_up': _jnp.float32, 'mlp_w_down': _jnp.float32, 'ln_g': _jnp.float32, 'ln_b': _jnp.float32}
MOMENT_SCALE = {'a_w_in': 4.408754e-02, 'a_conv_w': 4.403797e-02, 'a_A_log': 2.601674e-01, 'a_dt_bias': 2.574390e-01, 'a_norm_w': 1.819687e-01, 'b_w_in': 1.446616e-02, 'b_sinks': 1.698478e-02, 'w_kv_shared': 1.433926e-01, 'mem_w_kv': 1.080991e-02, 'w_o': 1.202239e-01, 'mlp_w_up': 6.411116e-02, 'mlp_w_down': 4.766476e-01, 'ln_g': 4.561801e+01, 'ln_b': 1.061137e+01}


def _to_microbatches(a, axis):
    t = _jnp.moveaxis(a, axis, 0)
    t = t.reshape((N_MICROBATCH, t.shape[0] // N_MICROBATCH) + t.shape[1:])
    return _jnp.moveaxis(t, 1, axis + 1)


def setup_inputs(seed: int = 0) -> dict:
    inp = _fwd_setup_inputs(seed)
    key = _jax.random.fold_in(_jax.random.key(seed), 7919)
    shape, _ = _output_shape()
    out = dict(inp)
    out["loss_target"] = _jax.random.normal(_jax.random.fold_in(key, 0), shape, _jnp.float32)
    for i, name in enumerate(TWIN_WEIGHTS):
        w = inp[name].astype(_jnp.float32)
        if MOMENT_SCALE is None:
            s = _jnp.sqrt(_jnp.mean(_jnp.square(w)) + 1e-30)
        else:
            s = MOMENT_SCALE[name]
        km, kv = _jax.random.split(_jax.random.fold_in(key, i + 1))
        out[name] = w
        out["m_" + name] = s * _jax.random.normal(km, w.shape, _jnp.float32)
        out["v_" + name] = (s * s) * _jax.random.uniform(kv, w.shape, _jnp.float32, 0.5, 1.5)
    if N_MICROBATCH > 1:
        for name, axis in PER_EXAMPLE_BATCH_AXIS.items():
            out[name] = _to_microbatches(out[name], axis)
    return {'x': out['x'], 'mem': out['mem'], 'positions': out['positions'], 'a_w_in': out['a_w_in'], 'a_conv_w': out['a_conv_w'], 'a_A_log': out['a_A_log'], 'a_dt_bias': out['a_dt_bias'], 'a_norm_w': out['a_norm_w'], 'b_w_in': out['b_w_in'], 'b_sinks': out['b_sinks'], 'w_kv_shared': out['w_kv_shared'], 'mem_w_kv': out['mem_w_kv'], 'w_o': out['w_o'], 'mlp_w_up': out['mlp_w_up'], 'mlp_w_down': out['mlp_w_down'], 'ln_g': out['ln_g'], 'ln_b': out['ln_b'], 'loss_target': out['loss_target'], 'm_a_w_in': out['m_a_w_in'], 'm_a_conv_w': out['m_a_conv_w'], 'm_a_A_log': out['m_a_A_log'], 'm_a_dt_bias': out['m_a_dt_bias'], 'm_a_norm_w': out['m_a_norm_w'], 'm_b_w_in': out['m_b_w_in'], 'm_b_sinks': out['m_b_sinks'], 'm_w_kv_shared': out['m_w_kv_shared'], 'm_mem_w_kv': out['m_mem_w_kv'], 'm_w_o': out['m_w_o'], 'm_mlp_w_up': out['m_mlp_w_up'], 'm_mlp_w_down': out['m_mlp_w_down'], 'm_ln_g': out['m_ln_g'], 'm_ln_b': out['m_ln_b'], 'v_a_w_in': out['v_a_w_in'], 'v_a_conv_w': out['v_a_conv_w'], 'v_a_A_log': out['v_a_A_log'], 'v_a_dt_bias': out['v_a_dt_bias'], 'v_a_norm_w': out['v_a_norm_w'], 'v_b_w_in': out['v_b_w_in'], 'v_b_sinks': out['v_b_sinks'], 'v_w_kv_shared': out['v_w_kv_shared'], 'v_mem_w_kv': out['v_mem_w_kv'], 'v_w_o': out['v_w_o'], 'v_mlp_w_up': out['v_mlp_w_up'], 'v_mlp_w_down': out['v_mlp_w_down'], 'v_ln_g': out['v_ln_g'], 'v_ln_b': out['v_ln_b']}


def _loss(weights, diff, rest, loss_target):
    with _jax.named_scope("forward"):
        args = {**rest, TWIN_DIFF_INPUT: diff, **{k: w.astype(_WEIGHT_DTYPES[k]) for k, w in weights.items()}}
        y = _forward(args)
    with _jax.named_scope("loss_head"):
        err = _jnp.square(y.astype(_jnp.float32) - loss_target)
        return 0.5 * _jnp.sum(_jnp.mean(err, axis=-1)) if err.ndim else 0.5 * err


def _adamw(w, g, m, v):
    m = ADAM_B1 * m + (1.0 - ADAM_B1) * g
    v = ADAM_B2 * v + (1.0 - ADAM_B2) * _jnp.square(g)
    m_hat = m / (1.0 - ADAM_B1 ** ADAM_STEP)
    v_hat = v / (1.0 - ADAM_B2 ** ADAM_STEP)
    delta = -ADAM_LR * (m_hat / (_jnp.sqrt(v_hat) + ADAM_EPS) + ADAM_WD * w)
    return delta, m, v


def reference(x, mem, positions, a_w_in, a_conv_w, a_A_log, a_dt_bias, a_norm_w, b_w_in, b_sinks, w_kv_shared, mem_w_kv, w_o, mlp_w_up, mlp_w_down, ln_g, ln_b, loss_target, m_a_w_in, m_a_conv_w, m_a_A_log, m_a_dt_bias, m_a_norm_w, m_b_w_in, m_b_sinks, m_w_kv_shared, m_mem_w_kv, m_w_o, m_mlp_w_up, m_mlp_w_down, m_ln_g, m_ln_b, v_a_w_in, v_a_conv_w, v_a_A_log, v_a_dt_bias, v_a_norm_w, v_b_w_in, v_b_sinks, v_w_kv_shared, v_mem_w_kv, v_w_o, v_mlp_w_up, v_mlp_w_down, v_ln_g, v_ln_b):
    given = dict(x=x, mem=mem, positions=positions, a_w_in=a_w_in, a_conv_w=a_conv_w, a_A_log=a_A_log, a_dt_bias=a_dt_bias, a_norm_w=a_norm_w, b_w_in=b_w_in, b_sinks=b_sinks, w_kv_shared=w_kv_shared, mem_w_kv=mem_w_kv, w_o=w_o, mlp_w_up=mlp_w_up, mlp_w_down=mlp_w_down, ln_g=ln_g, ln_b=ln_b, loss_target=loss_target, m_a_w_in=m_a_w_in, m_a_conv_w=m_a_conv_w, m_a_A_log=m_a_A_log, m_a_dt_bias=m_a_dt_bias, m_a_norm_w=m_a_norm_w, m_b_w_in=m_b_w_in, m_b_sinks=m_b_sinks, m_w_kv_shared=m_w_kv_shared, m_mem_w_kv=m_mem_w_kv, m_w_o=m_w_o, m_mlp_w_up=m_mlp_w_up, m_mlp_w_down=m_mlp_w_down, m_ln_g=m_ln_g, m_ln_b=m_ln_b, v_a_w_in=v_a_w_in, v_a_conv_w=v_a_conv_w, v_a_A_log=v_a_A_log, v_a_dt_bias=v_a_dt_bias, v_a_norm_w=v_a_norm_w, v_b_w_in=v_b_w_in, v_b_sinks=v_b_sinks, v_w_kv_shared=v_w_kv_shared, v_mem_w_kv=v_mem_w_kv, v_w_o=v_w_o, v_mlp_w_up=v_mlp_w_up, v_mlp_w_down=v_mlp_w_down, v_ln_g=v_ln_g, v_ln_b=v_ln_b)
    weights = {n: given[n] for n in TWIN_WEIGHTS}
    shared = {n: given[n] for n in SHARED_INPUTS}
    per_example = {n: given[n] for n in ['x', 'mem', 'positions']}
    grad_fn = _jax.value_and_grad(_loss, argnums=(0, 1))

    def one_microbatch(ex, loss_target):
        ex = dict(ex)
        diff = ex.pop(TWIN_DIFF_INPUT)
        return grad_fn(weights, diff, {**shared, **ex}, loss_target)

    if N_MICROBATCH == 1:
        loss, (grad_w, grad_x) = one_microbatch(per_example, given["loss_target"])
    else:
        def body(carry, xs):
            loss_sum, grad_sum = carry
            l_k, (gw_k, gx_k) = one_microbatch(xs[0], xs[1])
            with _jax.named_scope("update"):
                return (loss_sum + l_k, _jax.tree.map(_jnp.add, grad_sum, gw_k)), gx_k

        init = (_jnp.zeros((), _jnp.float32), _jax.tree.map(_jnp.zeros_like, weights))
        (loss, grad_w), grad_x = _jax.lax.scan(body, init, (per_example, given["loss_target"]))
    with _jax.named_scope("update"):
        delta_w, new_m, new_v = {}, {}, {}
        for n in TWIN_WEIGHTS:
            delta_w[n], new_m[n], new_v[n] = _adamw(weights[n], grad_w[n], given["m_" + n], given["v_" + n])
    return (loss, grad_x, *[grad_w[n] for n in TWIN_WEIGHTS], *[delta_w[n] for n in TWIN_WEIGHTS],
            *[new_m[n] for n in TWIN_WEIGHTS], *[new_v[n] for n in TWIN_WEIGHTS])
```

```python
import functools
import math

import jax
import jax.numpy as jnp
from jax import lax
from jax.experimental import pallas as pl
from jax.experimental.pallas import tpu as pltpu

D_MODEL = 1024
DEPTH = 4
N_A = DEPTH // 2
N_B = DEPTH - N_A
MEM_HEADS = 4
MEM_DH = D_MODEL // 16
MEM_W = MEM_HEADS * MEM_DH
DN_DK = 128
DN_DV = 128
DN_HEADS = (3 * D_MODEL) // (4 * DN_DV)
DN_QK_W = DN_HEADS * DN_DK
DN_V_W = DN_HEADS * DN_DV
CONV_WIDTH = 4
CHUNK = 64
SWA_DH = 64
SWA_HEADS = (3 * D_MODEL) // (4 * SWA_DH)
SWA_KV_HEADS = 2
SWA_GROUP = SWA_HEADS // SWA_KV_HEADS
SWA_Q_W = SWA_HEADS * SWA_DH
SWA_KV_W = SWA_KV_HEADS * SWA_DH
WINDOW = 128
ROPE_THETA = 10000.0
MLP_HIDDEN = 4 * D_MODEL
LN_EPS = 1e-5
NORM_EPS = 1e-6
DN_ALPHA = (2.0 * DEPTH) ** 0.25
A_IN = 2 * DN_QK_W + 2 * DN_V_W + 2 * DN_HEADS + MEM_W
A_IN_PAD = 3456
QKVZ_W = 2 * DN_QK_W + 2 * DN_V_W

ADAM_LR = 0.001
ADAM_B1 = 0.9
ADAM_B2 = 0.999
ADAM_EPS = 1e-08
ADAM_WD = 0.01
ADAM_STEP = 10

N_CHIPS = 4
FLAT_W = 1024
BIG = ("a_w_in", "b_w_in", "w_kv_shared", "mem_w_kv", "w_o", "mlp_w_up", "mlp_w_down")
SMALL = ("a_conv_w", "ln_g", "ln_b", "a_A_log", "a_dt_bias", "a_norm_w", "b_sinks")
REPLICATED = ("a_A_log", "a_dt_bias", "a_norm_w", "b_sinks")
WEIGHTS = ("a_w_in", "a_conv_w", "a_A_log", "a_dt_bias", "a_norm_w", "b_w_in", "b_sinks", "w_kv_shared",
           "mem_w_kv", "w_o", "mlp_w_up", "mlp_w_down", "ln_g", "ln_b")
SHARD_AXIS = {"a_w_in": 2, "a_conv_w": 2, "b_w_in": 1, "w_kv_shared": 0, "mem_w_kv": 1, "w_o": 1,
              "mlp_w_up": 2, "mlp_w_down": 1, "ln_g": 2, "ln_b": 2}
SMALL_ROWS = 16
ROW_ALIGN = 256

MESH = pl.DeviceIdType.MESH
HBM_SPEC = pl.BlockSpec(memory_space=pltpu.HBM)
VMEM_LIMIT = 48 * 1024 * 1024


def _rows_for(shards):
    n_big = sum(math.prod(shards[n].shape) for n in BIG)
    n_small = sum(math.prod(shards[n].shape) for n in SMALL)
    assert n_small <= SMALL_ROWS * FLAT_W
    total = -(-n_big // FLAT_W) + SMALL_ROWS
    total = -(-total // ROW_ALIGN) * ROW_ALIGN
    return total - SMALL_ROWS, total


def _pack(shards, rb, dtype_big=jnp.float32):
    big = jnp.concatenate([shards[n].reshape(-1).astype(dtype_big) for n in BIG])
    big = jnp.pad(big, (0, rb * FLAT_W - big.shape[0])).reshape(rb, FLAT_W)
    small = jnp.concatenate([shards[n].reshape(-1).astype(jnp.float32) for n in SMALL])
    small = jnp.pad(small, (0, SMALL_ROWS * FLAT_W - small.shape[0])).reshape(SMALL_ROWS, FLAT_W)
    return big, small


def _unpack(big, small, shapes):
    out = {}
    for flat, names in ((big.reshape(-1), BIG), (small.reshape(-1), SMALL)):
        off = 0
        for n in names:
            size = math.prod(shapes[n])
            out[n] = flat[off:off + size].reshape(shapes[n])
            off += size
    return out


def _other_chips(x, y):
    return [(1 - x, y), (x, 1 - y), (1 - x, 1 - y)]


def _gather_weights(big, small):
    def body(big_ref, small_ref, obig_ref, osmall_ref, send_sems, recv_sems, local_sems):
        x, y, c = lax.axis_index("x"), lax.axis_index("y"), lax.axis_index("c")
        me = 2 * x + y
        pairs = ((big_ref, obig_ref), (small_ref, osmall_ref))
        local = [pltpu.make_async_copy(src, dst.at[me], local_sems.at[i]) for i, (src, dst) in enumerate(pairs)]
        for cp in local:
            cp.start()
        sends = []
        for j, (px, py) in enumerate(_other_chips(x, y)):
            for i, (src, dst) in enumerate(pairs):
                sends.append(pltpu.make_async_remote_copy(
                    src_ref=src, dst_ref=dst.at[me], send_sem=send_sems.at[2 * j + i], recv_sem=recv_sems.at[2 * j + i],
                    device_id=(px, py, c), device_id_type=MESH))
        for cp in sends:
            cp.start()
        for j, (px, py) in enumerate(_other_chips(x, y)):
            for i, (src, dst) in enumerate(pairs):
                pltpu.make_async_remote_copy(
                    src_ref=src, dst_ref=dst.at[2 * px + py], send_sem=send_sems.at[2 * j + i],
                    recv_sem=recv_sems.at[2 * j + i], device_id=(px, py, c), device_id_type=MESH).wait_recv()
        for cp in sends:
            cp.wait_send()
        for cp in local:
            cp.wait()

    return pl.pallas_call(
        body, name="gather_weights",
        out_shape=(jax.ShapeDtypeStruct((N_CHIPS,) + big.shape, big.dtype),
                   jax.ShapeDtypeStruct((N_CHIPS,) + small.shape, small.dtype)),
        in_specs=[HBM_SPEC, HBM_SPEC], out_specs=(HBM_SPEC, HBM_SPEC),
        scratch_shapes=[pltpu.SemaphoreType.DMA((6,)), pltpu.SemaphoreType.DMA((6,)), pltpu.SemaphoreType.DMA((2,))],
    )(big, small)


def _scatter_grads(g):
    def body(g_ref, o_ref, send_sems, recv_sems, local_sem):
        x, y, c = lax.axis_index("x"), lax.axis_index("y"), lax.axis_index("c")
        me = 2 * x + y
        local = pltpu.make_async_copy(g_ref.at[me], o_ref.at[me], local_sem)
        local.start()
        sends = []
        for j, (px, py) in enumerate(_other_chips(x, y)):
            sends.append(pltpu.make_async_remote_copy(
                src_ref=g_ref.at[2 * px + py], dst_ref=o_ref.at[me], send_sem=send_sems.at[j], recv_sem=recv_sems.at[j],
                device_id=(px, py, c), device_id_type=MESH))
        for cp in sends:
            cp.start()
        for j, (px, py) in enumerate(_other_chips(x, y)):
            pltpu.make_async_remote_copy(
                src_ref=g_ref.at[me], dst_ref=o_ref.at[2 * px + py], send_sem=send_sems.at[j], recv_sem=recv_sems.at[j],
                device_id=(px, py, c), device_id_type=MESH).wait_recv()
        for cp in sends:
            cp.wait_send()
        local.wait()

    return pl.pallas_call(
        body, name="scatter_grads",
        out_shape=jax.ShapeDtypeStruct(g.shape, g.dtype),
        in_specs=[HBM_SPEC], out_specs=HBM_SPEC,
        scratch_shapes=[pltpu.SemaphoreType.DMA((3,)), pltpu.SemaphoreType.DMA((3,)), pltpu.SemaphoreType.DMA],
    )(g)


def _swap_with_sibling(v):
    def body(v_ref, o_ref, send_sem, recv_sem):
        x, y, c = lax.axis_index("x"), lax.axis_index("y"), lax.axis_index("c")
        cp = pltpu.make_async_remote_copy(src_ref=v_ref, dst_ref=o_ref, send_sem=send_sem, recv_sem=recv_sem,
                                          device_id=(x, y, 1 - c), device_id_type=MESH)
        cp.start()
        cp.wait()

    return pl.pallas_call(
        body, name="swap_with_sibling",
        out_shape=jax.ShapeDtypeStruct(v.shape, v.dtype),
        in_specs=[HBM_SPEC], out_specs=HBM_SPEC,
        scratch_shapes=[pltpu.SemaphoreType.DMA, pltpu.SemaphoreType.DMA],
    )(v)


def _sum_chips(parts):
    n, rows, width = parts.shape

    def body(p_ref, o_ref):
        p = [p_ref[q].astype(jnp.float32) for q in range(n)]
        o_ref[...] = (p[0] + p[1]) + (p[2] + p[3])

    return pl.pallas_call(
        body, name="sum_chips", grid=(rows // ROW_ALIGN,),
        out_shape=jax.ShapeDtypeStruct((rows, width), jnp.float32),
        in_specs=[pl.BlockSpec((n, ROW_ALIGN, width), lambda i: (0, i, 0))],
        out_specs=pl.BlockSpec((ROW_ALIGN, width), lambda i: (i, 0)),
        compiler_params=pltpu.CompilerParams(dimension_semantics=("parallel",), vmem_limit_bytes=VMEM_LIMIT),
    )(parts)


def _adamw(mine, other, w, m, v):
    rows, width = w.shape
    blk = ROW_ALIGN // 2

    def body(a_ref, b_ref, w_ref, m_ref, v_ref, g_out, d_out, m_out, v_out):
        g = a_ref[...] + b_ref[...]
        m_new = ADAM_B1 * m_ref[...] + (1.0 - ADAM_B1) * g
        v_new = ADAM_B2 * v_ref[...] + (1.0 - ADAM_B2) * jnp.square(g)
        m_hat = m_new / (1.0 - ADAM_B1 ** ADAM_STEP)
        v_hat = v_new / (1.0 - ADAM_B2 ** ADAM_STEP)
        g_out[...] = g
        d_out[...] = -ADAM_LR * (m_hat / (jnp.sqrt(v_hat) + ADAM_EPS) + ADAM_WD * w_ref[...])
        m_out[...] = m_new
        v_out[...] = v_new

    spec = pl.BlockSpec((blk, width), lambda i: (i, 0))
    shape = jax.ShapeDtypeStruct((rows, width), jnp.float32)
    return pl.pallas_call(
        body, name="adamw", grid=(rows // blk,),
        out_shape=(shape,) * 4, in_specs=[spec] * 5, out_specs=(spec,) * 4,
        compiler_params=pltpu.CompilerParams(dimension_semantics=("parallel",), vmem_limit_bytes=VMEM_LIMIT),
    )(mine, other, w, m, v)


def _tile(dim, pref):
    if dim <= pref:
        return dim
    for t in range(pref - pref % 128, 0, -128):
        if dim % t == 0:
            return t
    raise ValueError(f"no 128-aligned tile for {dim}")


def _matmul(a, b, *, ta=False, tb=False, name, epilogue=None, extra=None, out_dtype=jnp.float32):
    (k_a, m) = a.shape if ta else a.shape[::-1]
    (k_b, n) = b.shape[::-1] if tb else b.shape
    assert k_a == k_b, (a.shape, b.shape, ta, tb)
    k = k_a
    tm, tn, tk = _tile(m, 1024), _tile(n, 512), _tile(k, 1024)
    nk = k // tk
    a_spec = pl.BlockSpec((tk, tm), lambda i, j, l: (l, i)) if ta else pl.BlockSpec((tm, tk), lambda i, j, l: (i, l))
    b_spec = pl.BlockSpec((tn, tk), lambda i, j, l: (j, l)) if tb else pl.BlockSpec((tk, tn), lambda i, j, l: (l, j))
    o_spec = pl.BlockSpec((tm, tn), lambda i, j, l: (i, j))
    dims = (((0 if ta else 1,), (1 if tb else 0,)), ((), ()))
    has_extra = epilogue == "relu2_grad"
    assert has_extra == (extra is not None)

    def body(*refs):
        a_ref, b_ref = refs[:2]
        outs, acc_ref = refs[2 + has_extra:-1], refs[-1]
        l = pl.program_id(2)

        @pl.when(l == 0)
        def _():
            acc_ref[...] = jnp.zeros_like(acc_ref)

        acc_ref[...] += lax.dot_general(a_ref[...].astype(jnp.bfloat16), b_ref[...].astype(jnp.bfloat16), dims,
                                        preferred_element_type=jnp.float32)

        @pl.when(l == nk - 1)
        def _():
            acc = acc_ref[...]
            if epilogue is None:
                outs[0][...] = acc.astype(out_dtype)
            elif epilogue == "relu2":
                outs[0][...] = acc.astype(jnp.bfloat16)
                outs[1][...] = jnp.square(jnp.maximum(acc, 0.0)).astype(jnp.bfloat16)
            else:
                outs[0][...] = (acc * (2.0 * jnp.maximum(refs[2][...].astype(jnp.float32), 0.0))).astype(out_dtype)

    if epilogue == "relu2":
        out_shape = (jax.ShapeDtypeStruct((m, n), jnp.bfloat16),) * 2
        out_specs = (o_spec, o_spec)
    else:
        out_shape = jax.ShapeDtypeStruct((m, n), out_dtype)
        out_specs = o_spec
    return pl.pallas_call(
        body, name=name, grid=(m // tm, n // tn, nk), out_shape=out_shape,
        in_specs=[a_spec, b_spec] + ([o_spec] if has_extra else []), out_specs=out_specs,
        scratch_shapes=[pltpu.VMEM((tm, tn), jnp.float32)],
        compiler_params=pltpu.CompilerParams(dimension_semantics=("parallel", "parallel", "arbitrary"),
                                             vmem_limit_bytes=VMEM_LIMIT),
    )(*((a, b) + ((extra,) if has_extra else ())))


@jax.custom_vjp
def _linear(x, w, slot):
    del slot
    return _matmul(x, w, name="linear_fwd")


def _linear_fwd(x, w, slot):
    del slot
    return _matmul(x, w, name="linear_fwd"), (x, w)


def _linear_bwd(res, dy):
    x, w = res
    dx = _matmul(dy, w, tb=True, name="linear_dx")
    dw = _matmul(x, dy, ta=True, name="linear_dw")
    return dx, jnp.zeros_like(w), dw


_linear.defvjp(_linear_fwd, _linear_bwd)


@jax.custom_vjp
def _mlp(h, w_up, w_down, slot_up, slot_down):
    return _mlp_fwd(h, w_up, w_down, slot_up, slot_down)[0]


def _mlp_fwd(h, w_up, w_down, slot_up, slot_down):
    del slot_up, slot_down
    up, act = _matmul(h, w_up, name="mlp_up", epilogue="relu2")
    return _matmul(act, w_down, name="mlp_down"), (h, up, act, w_up, w_down)


def _mlp_bwd(res, dy):
    h, up, act, w_up, w_down = res
    d_up = _matmul(dy, w_down, tb=True, name="mlp_d_up", epilogue="relu2_grad", extra=up, out_dtype=jnp.bfloat16)
    dw_down = _matmul(act, dy, ta=True, name="mlp_dw_down")
    dw_up = _matmul(h, d_up, ta=True, name="mlp_dw_up")
    dh = _matmul(d_up, w_up, tb=True, name="mlp_dh")
    return dh, jnp.zeros_like(w_up), jnp.zeros_like(w_down), dw_up, dw_down


_mlp.defvjp(_mlp_fwd, _mlp_bwd)


LN_ROWS = 256


def _ln_call(h, mix, g, b):
    s, d = h.shape
    tok = pl.BlockSpec((LN_ROWS, d), lambda i: (i, 0))
    vec = pl.BlockSpec((1, d), lambda i: (0, 0))
    stat = pl.BlockSpec((LN_ROWS, 1), lambda i: (i, 0))

    def body(h_ref, mix_ref, g_ref, b_ref, y_ref, xhat_ref, rstd_ref):
        z = DN_ALPHA * h_ref[...] + mix_ref[...]
        mu = jnp.mean(z, axis=-1, keepdims=True)
        zc = z - mu
        rstd = lax.rsqrt(jnp.mean(jnp.square(zc), axis=-1, keepdims=True) + LN_EPS)
        xhat = zc * rstd
        y_ref[...] = xhat * g_ref[...] + b_ref[...]
        xhat_ref[...] = xhat
        rstd_ref[...] = rstd

    sd = jax.ShapeDtypeStruct
    return pl.pallas_call(
        body, name="ln_fwd", grid=(s // LN_ROWS,),
        out_shape=(sd((s, d), jnp.float32), sd((s, d), jnp.float32), sd((s, 1), jnp.float32)),
        in_specs=[tok, tok, vec, vec], out_specs=(tok, tok, stat),
        compiler_params=pltpu.CompilerParams(dimension_semantics=("parallel",)),
    )(h, mix, g, b)


def _ln_grad_call(dy, xhat, rstd, g):
    s, d = dy.shape
    tok = pl.BlockSpec((LN_ROWS, d), lambda i: (i, 0))
    vec = pl.BlockSpec((1, d), lambda i: (0, 0))
    stat = pl.BlockSpec((LN_ROWS, 1), lambda i: (i, 0))

    def body(dy_ref, xhat_ref, rstd_ref, g_ref, dz_ref, dg_ref, db_ref):
        @pl.when(pl.program_id(0) == 0)
        def _():
            dg_ref[...] = jnp.zeros_like(dg_ref)
            db_ref[...] = jnp.zeros_like(db_ref)

        dy, xhat = dy_ref[...], xhat_ref[...]
        dyg = dy * g_ref[...]
        m1 = jnp.mean(dyg, axis=-1, keepdims=True)
        m2 = jnp.mean(dyg * xhat, axis=-1, keepdims=True)
        dz_ref[...] = rstd_ref[...] * (dyg - m1 - xhat * m2)
        dg_ref[...] += jnp.sum(dy * xhat, axis=0, keepdims=True)
        db_ref[...] += jnp.sum(dy, axis=0, keepdims=True)

    sd = jax.ShapeDtypeStruct
    return pl.pallas_call(
        body, name="ln_bwd", grid=(s // LN_ROWS,),
        out_shape=(sd((s, d), jnp.float32), sd((1, d), jnp.float32), sd((1, d), jnp.float32)),
        in_specs=[tok, tok, stat, vec], out_specs=(tok, vec, vec),
        compiler_params=pltpu.CompilerParams(dimension_semantics=("arbitrary",)),
    )(dy, xhat, rstd, g)


@jax.custom_vjp
def _ln_res(h, mix, g, b):
    return _ln_call(h, mix, g, b)[0]


def _ln_res_fwd(h, mix, g, b):
    y, xhat, rstd = _ln_call(h, mix, g, b)
    return y, (xhat, rstd, g)


def _ln_res_bwd(res, dy):
    xhat, rstd, g = res
    dz, dg, db = _ln_grad_call(dy, xhat, rstd, g)
    return DN_ALPHA * dz, dz, dg, db


_ln_res.defvjp(_ln_res_fwd, _ln_res_bwd)


MXU_DTYPE = jnp.bfloat16
DN_CB = 4
DN_SCALE = DN_DK ** -0.5
_HI = lax.Precision.HIGHEST


def _dot(a, b, ca=1, cb=0, precision=None):
    if precision is None:
        a, b = a.astype(MXU_DTYPE), b.astype(MXU_DTYPE)
    return lax.dot_general(a, b, (((ca,), (cb,)), ((), ())), preferred_element_type=jnp.float32, precision=precision)


def _chunk_masks():
    row = lax.broadcasted_iota(jnp.int32, (CHUNK, CHUNK), 0)
    col = lax.broadcasted_iota(jnp.int32, (CHUNK, CHUNK), 1)
    return row >= col, row > col, row == col


def _last_row(col_vec):
    last = lax.broadcasted_iota(jnp.int32, (CHUNK, 1), 0) == CHUNK - 1
    return jnp.sum(jnp.where(last, col_vec, 0.0), axis=0, keepdims=True), last


def _chunk_terms(q, k, beta, gcc, gcr):
    incl, strict, _ = _chunk_masks()
    decay = jnp.where(incl, jnp.exp(jnp.minimum(gcc - gcr, 0.0)), 0.0)
    kb = k * beta
    lmat = jnp.where(strict, _dot(kb, k, 1, 1) * decay, 0.0)
    intra = jnp.where(incl, _dot(q, k, 1, 1) * decay, 0.0)
    return decay, kb, lmat, intra


def _unit_lower_inverse(lmat):
    _, _, eye = _chunk_masks()
    x = -lmat
    t = jnp.where(eye, 1.0, 0.0) + x
    p = _dot(x, x, precision=_HI)
    for _ in range(4):
        t = t + _dot(t, p, precision=_HI)
        p = _dot(p, p, precision=_HI)
    return t + _dot(t, p, precision=_HI)


def _dn_specs(n_chunks):
    tok = pl.BlockSpec((DN_CB * CHUNK, DN_DK), lambda h, n: (n, h))
    colv = pl.BlockSpec((None, DN_CB, CHUNK, 1), lambda h, n: (h, n, 0, 0))
    rowv = pl.BlockSpec((None, DN_CB, 1, CHUNK), lambda h, n: (h, n, 0, 0))
    sq = pl.BlockSpec((None, DN_CB, CHUNK, CHUNK), lambda h, n: (h, n, 0, 0))
    lane = pl.BlockSpec((None, DN_CB, 1, DN_DV), lambda h, n: (h, n, 0, 0))
    return tok, colv, rowv, sq, lane


def _dn_prep(q, k, v, beta, gcc, gcr):
    s = q.shape[0]
    n_chunks = s // CHUNK
    tok, colv, rowv, sq, lane = _dn_specs(n_chunks)

    def body(q_ref, k_ref, v_ref, beta_ref, gcc_ref, gcr_ref, u_ref, w_ref, qd_ref, kd_ref, intra_ref, t_ref, cd_ref):
        for c in range(DN_CB):
            rows = pl.ds(c * CHUNK, CHUNK)
            q_c, k_c, v_c = q_ref[rows, :] * DN_SCALE, k_ref[rows, :], v_ref[rows, :]
            beta_c, gcc_c, gcr_c = beta_ref[c], gcc_ref[c], gcr_ref[c]
            _, kb, lmat, intra = _chunk_terms(q_c, k_c, beta_c, gcc_c, gcr_c)
            t = _unit_lower_inverse(lmat)
            eg = jnp.exp(gcc_c)
            g_last, _ = _last_row(gcc_c)
            u_ref[rows, :] = _dot(t, v_c * beta_c, precision=_HI)
            w_ref[rows, :] = _dot(t, kb * eg, precision=_HI).astype(w_ref.dtype)
            qd_ref[rows, :] = (q_c * eg).astype(qd_ref.dtype)
            kd_ref[rows, :] = (k_c * jnp.exp(g_last - gcc_c)).astype(kd_ref.dtype)
            intra_ref[c] = intra.astype(intra_ref.dtype)
            t_ref[c] = t
            cd_ref[c] = jnp.broadcast_to(jnp.exp(g_last), (1, DN_DV))

    f32, mx = jnp.float32, MXU_DTYPE
    sd = jax.ShapeDtypeStruct
    return pl.pallas_call(
        body, name="dn_prep", grid=(DN_HEADS, n_chunks // DN_CB),
        out_shape=(sd(q.shape, f32), sd(q.shape, mx), sd(q.shape, mx), sd(q.shape, mx),
                   sd((DN_HEADS, n_chunks, CHUNK, CHUNK), mx), sd((DN_HEADS, n_chunks, CHUNK, CHUNK), f32),
                   sd((DN_HEADS, n_chunks, 1, DN_DV), f32)),
        in_specs=[tok, tok, tok, colv, colv, rowv], out_specs=(tok, tok, tok, tok, sq, sq, lane),
        compiler_params=pltpu.CompilerParams(dimension_semantics=("parallel", "parallel")),
    )(q, k, v, beta, gcc, gcr)


def _dn_scan(u, w, qd, kd, intra, cd):
    s, width = u.shape
    n_chunks = s // CHUNK
    tok = pl.BlockSpec((CHUNK, width), lambda n: (n, 0))
    sq = pl.BlockSpec((DN_HEADS, None, CHUNK, CHUNK), lambda n: (0, n, 0, 0))
    lane = pl.BlockSpec((DN_HEADS, None, 1, DN_DV), lambda n: (0, n, 0, 0))
    st = pl.BlockSpec((DN_HEADS, None, DN_DK, DN_DV), lambda n: (0, n, 0, 0))

    def body(u_ref, w_ref, qd_ref, kd_ref, intra_ref, cd_ref, o_ref, vn_ref, st_ref, state):
        @pl.when(pl.program_id(0) == 0)
        def _():
            state[...] = jnp.zeros_like(state)

        for h in range(DN_HEADS):
            cols = pl.ds(h * DN_DK, DN_DK)
            s_h = state[h]
            s_mx = s_h.astype(MXU_DTYPE)
            st_ref[h] = s_mx.astype(st_ref.dtype)
            v_new = u_ref[:, cols] - _dot(w_ref[:, cols], s_mx)
            vn_ref[:, cols] = v_new.astype(vn_ref.dtype)
            o_ref[:, cols] = _dot(qd_ref[:, cols], s_mx) + _dot(intra_ref[h], v_new)
            state[h] = s_h * cd_ref[h] + _dot(kd_ref[:, cols], v_new, 0, 0)

    sd = jax.ShapeDtypeStruct
    return pl.pallas_call(
        body, name="dn_scan", grid=(n_chunks,),
        out_shape=(sd(u.shape, jnp.float32), sd(u.shape, MXU_DTYPE),
                   sd((DN_HEADS, n_chunks, DN_DK, DN_DV), MXU_DTYPE)),
        in_specs=[tok, tok, tok, tok, sq, lane], out_specs=(tok, tok, st),
        scratch_shapes=[pltpu.VMEM((DN_HEADS, DN_DK, DN_DV), jnp.float32)],
        compiler_params=pltpu.CompilerParams(dimension_semantics=("arbitrary",)),
    )(u, w, qd, kd, intra, cd)


def _dn_bwd_scan(do, w, qd, kd, intra, cd, vn, st):
    s, width = do.shape
    n_chunks = s // CHUNK
    last = n_chunks - 1
    tok = pl.BlockSpec((CHUNK, width), lambda n: (last - n, 0))
    sq = pl.BlockSpec((DN_HEADS, None, CHUNK, CHUNK), lambda n: (0, last - n, 0, 0))
    lane = pl.BlockSpec((DN_HEADS, None, 1, DN_DV), lambda n: (0, last - n, 0, 0))
    stt = pl.BlockSpec((DN_HEADS, None, DN_DK, DN_DV), lambda n: (0, last - n, 0, 0))

    def body(do_ref, w_ref, qd_ref, kd_ref, intra_ref, cd_ref, vn_ref, st_ref,
             du_ref, dw_ref, dqd_ref, dkd_ref, dintra_ref, dgl_ref, dstate):
        @pl.when(pl.program_id(0) == 0)
        def _():
            dstate[...] = jnp.zeros_like(dstate)

        for h in range(DN_HEADS):
            cols = pl.ds(h * DN_DK, DN_DK)
            ds_h = dstate[h]
            ds_mx = ds_h.astype(MXU_DTYPE)
            s_mx = st_ref[h]
            do_h = do_ref[:, cols].astype(MXU_DTYPE)
            dv_new = _dot(intra_ref[h], do_h, 0, 0) + _dot(kd_ref[:, cols], ds_mx)
            du_ref[:, cols] = dv_new
            dintra_ref[h] = _dot(do_h, vn_ref[:, cols], 1, 1)
            dqd_ref[:, cols] = _dot(do_h, s_mx, 1, 1)
            dkd_ref[:, cols] = _dot(vn_ref[:, cols], ds_mx, 1, 1)
            dw_ref[:, cols] = -_dot(dv_new, s_mx, 1, 1)
            cd_h = cd_ref[h]
            dcd = jnp.sum(jnp.sum(s_mx.astype(jnp.float32) * ds_h, axis=1, keepdims=True), axis=0, keepdims=True)
            dgl_ref[h] = dcd * cd_h
            dstate[h] = _dot(qd_ref[:, cols], do_h, 0, 0) + ds_h * cd_h - _dot(w_ref[:, cols], dv_new, 0, 0)

    sd = jax.ShapeDtypeStruct
    f32 = jnp.float32
    return pl.pallas_call(
        body, name="dn_bwd_scan", grid=(n_chunks,),
        out_shape=(sd(do.shape, f32), sd(do.shape, f32), sd(do.shape, f32), sd(do.shape, f32),
                   sd((DN_HEADS, n_chunks, CHUNK, CHUNK), f32), sd((DN_HEADS, n_chunks, 1, DN_DV), f32)),
        in_specs=[tok, tok, tok, tok, sq, lane, tok, stt], out_specs=(tok, tok, tok, tok, sq, lane),
        scratch_shapes=[pltpu.VMEM((DN_HEADS, DN_DK, DN_DV), f32)],
        compiler_params=pltpu.CompilerParams(dimension_semantics=("arbitrary",)),
    )(do, w, qd, kd, intra, cd, vn, st)


def _dn_bwd_chunks(q, k, v, beta, gcc, gcr, t, u, w, du, dw, dqd, dkd, dintra, dgl):
    s = q.shape[0]
    n_chunks = s // CHUNK
    tok, colv, rowv, sq, lane = _dn_specs(n_chunks)

    def body(q_ref, k_ref, v_ref, beta_ref, gcc_ref, gcr_ref, t_ref, u_ref, w_ref, du_ref, dw_ref, dqd_ref, dkd_ref,
             dintra_ref, dgl_ref, dq_ref, dk_ref, dv_ref, dbeta_ref, dgcc_ref, dgcr_ref):
        incl, strict, _ = _chunk_masks()
        for c in range(DN_CB):
            rows = pl.ds(c * CHUNK, CHUNK)
            q_c, k_c, v_c = q_ref[rows, :] * DN_SCALE, k_ref[rows, :], v_ref[rows, :]
            beta_c, gcc_c, gcr_c = beta_ref[c], gcc_ref[c], gcr_ref[c]
            decay, kb, lmat, intra = _chunk_terms(q_c, k_c, beta_c, gcc_c, gcr_c)
            eg = jnp.exp(gcc_c)
            g_last, is_last = _last_row(gcc_c)
            e_rev = jnp.exp(g_last - gcc_c)
            d_sol = jnp.concatenate([du_ref[rows, :], dw_ref[rows, :]], axis=1)
            d_rhs = _dot(t_ref[c], d_sol, 0, 0, precision=_HI)
            sol = jnp.concatenate([u_ref[rows, :], w_ref[rows, :].astype(jnp.float32)], axis=1)
            d_l = jnp.where(strict, -_dot(d_rhs, sol, 1, 1), 0.0)
            d_rhs_u, d_rhs_w = d_rhs[:, :DN_DV], d_rhs[:, DN_DV:]
            dv_ref[rows, :] = d_rhs_u * beta_c
            dbeta = jnp.sum(d_rhs_u * v_c, axis=1, keepdims=True)
            dkb = d_rhs_w * eg
            dgc = jnp.sum(d_rhs_w * kb * eg, axis=1, keepdims=True)
            d_a = d_l * decay
            dkb = dkb + _dot(d_a, k_c)
            dk = _dot(d_a, kb, 0, 0)
            m1 = d_l * lmat
            dgc = dgc + jnp.sum(m1, axis=1, keepdims=True)
            dgr = -jnp.sum(m1, axis=0, keepdims=True)
            d_intra = jnp.where(incl, dintra_ref[c], 0.0)
            d_qk = d_intra * decay
            dq = _dot(d_qk, k_c)
            dk = dk + _dot(d_qk, q_c, 0, 0)
            m2 = d_intra * intra
            dgc = dgc + jnp.sum(m2, axis=1, keepdims=True)
            dgr = dgr - jnp.sum(m2, axis=0, keepdims=True)
            dqd = dqd_ref[rows, :]
            dq = dq + dqd * eg
            dgc = dgc + jnp.sum(dqd * q_c * eg, axis=1, keepdims=True)
            dkd = dkd_ref[rows, :]
            dk = dk + dkd * e_rev
            tk = jnp.sum(dkd * k_c * e_rev, axis=1, keepdims=True)
            dgc = dgc - tk
            d_last = dgl_ref[c][:, :1] + jnp.sum(tk, axis=0, keepdims=True)
            dgc = dgc + jnp.where(is_last, d_last, 0.0)
            dk = dk + dkb * beta_c
            dbeta = dbeta + jnp.sum(dkb * k_c, axis=1, keepdims=True)
            dq_ref[rows, :] = dq * DN_SCALE
            dk_ref[rows, :] = dk
            dbeta_ref[c] = dbeta
            dgcc_ref[c] = dgc
            dgcr_ref[c] = dgr

    sd = jax.ShapeDtypeStruct
    f32 = jnp.float32
    return pl.pallas_call(
        body, name="dn_bwd_chunks", grid=(DN_HEADS, n_chunks // DN_CB),
        out_shape=(sd(q.shape, f32), sd(q.shape, f32), sd(q.shape, f32), sd(beta.shape, f32), sd(gcc.shape, f32),
                   sd(gcr.shape, f32)),
        in_specs=[tok, tok, tok, colv, colv, rowv, sq, tok, tok, tok, tok, tok, tok, sq, lane],
        out_specs=(tok, tok, tok, colv, colv, rowv),
        compiler_params=pltpu.CompilerParams(dimension_semantics=("parallel", "parallel")),
    )(q, k, v, beta, gcc, gcr, t, u, w, du, dw, dqd, dkd, dintra, dgl)


@jax.custom_vjp
def _delta_rule_op(q, k, v, beta, gcc, gcr):
    return _delta_rule_fwd(q, k, v, beta, gcc, gcr)[0]


def _delta_rule_fwd(q, k, v, beta, gcc, gcr):
    u, w, qd, kd, intra, t, cd = _dn_prep(q, k, v, beta, gcc, gcr)
    out, vn, st = _dn_scan(u, w, qd, kd, intra, cd)
    return out, (q, k, v, beta, gcc, gcr, u, w, qd, kd, intra, t, cd, vn, st)


def _delta_rule_bwd(res, do):
    q, k, v, beta, gcc, gcr, u, w, qd, kd, intra, t, cd, vn, st = res
    du, dw, dqd, dkd, dintra, dgl = _dn_bwd_scan(do, w, qd, kd, intra, cd, vn, st)
    return _dn_bwd_chunks(q, k, v, beta, gcc, gcr, t, u, w, du, dw, dqd, dkd, dintra, dgl)


_delta_rule_op.defvjp(_delta_rule_fwd, _delta_rule_bwd)


def _gated_delta_rule(q, k, v, g, beta):
    _, s, h, _ = q.shape
    n_chunks = s // CHUNK
    gc = jnp.cumsum(g[0].T.reshape(h, n_chunks, CHUNK), axis=-1)
    beta_c = beta[0].T.reshape(h, n_chunks, CHUNK, 1)
    out = _delta_rule_op(q.reshape(s, h * DN_DK), k.reshape(s, h * DN_DK), v.reshape(s, h * DN_DV), beta_c,
                         gc[..., None], gc[:, :, None, :])
    return out.reshape(1, s, h, DN_DV)


def _project(h, w, slot):
    b, s, d = h.shape
    return _linear(h.reshape(b * s, d), w, slot).reshape(b, s, w.shape[1])


def _l2_normalize(x):
    return x * lax.rsqrt(jnp.sum(x * x, axis=-1, keepdims=True) + NORM_EPS)


def _rope_tables(positions, dh):
    inv_freq = ROPE_THETA ** (-jnp.arange(0, dh, 2, dtype=jnp.float32) / dh)
    ang = positions.astype(jnp.float32)[..., None] * inv_freq
    return jnp.cos(ang)[:, :, None, :], jnp.sin(ang)[:, :, None, :]


def _apply_rope(x, cos, sin):
    x1, x2 = jnp.split(x, 2, axis=-1)
    return jnp.concatenate([x1 * cos - x2 * sin, x2 * cos + x1 * sin], axis=-1)


def _causal_conv(x, w):
    s = x.shape[1]
    xp = jnp.pad(x, ((0, 0), (CONV_WIDTH - 1, 0), (0, 0)))
    return sum(w[j] * xp[:, j:j + s] for j in range(CONV_WIDTH))


def _swa_sink_attention(q, k, v, sinks):
    B, S, HKV, G, dh = q.shape
    NB = S // WINDOW
    qb = q.reshape(B, NB, WINDOW, HKV, G, dh)

    def band_keys(t):
        tb = t.reshape(B, NB, WINDOW, HKV, dh)
        prev = jnp.pad(tb, ((0, 0), (1, 0), (0, 0), (0, 0), (0, 0)))[:, :-1]
        return jnp.concatenate([prev, tb], axis=2)

    kk = band_keys(k)
    vv = band_keys(v)
    s = jnp.einsum('bnqhgd,bnkhd->bnhgqk', qb, kk) * (dh ** -0.5)
    qi = jnp.arange(WINDOW)[:, None]
    kj = jnp.arange(2 * WINDOW)[None, :]
    diff = qi + WINDOW - kj
    band = (diff >= 0) & (diff < WINDOW)
    key_pos = jnp.arange(NB)[:, None] * WINDOW - WINDOW + kj
    valid = band[None] & (key_pos >= 0)[:, None, :]
    s = jnp.where(valid[None, :, None, None], s, -jnp.inf)
    sink = sinks.reshape(HKV, G)[None, None, :, :, None, None]
    m = jnp.maximum(jnp.max(s, axis=-1, keepdims=True), sink)
    p = jnp.exp(s - m)
    denom = jnp.sum(p, axis=-1, keepdims=True) + jnp.exp(sink - m)
    probs = p / denom
    o = jnp.einsum('bnhgqk,bnkhd->bnqhgd', probs, vv)
    return o.reshape(B, S, HKV * G * dh)


def _memory_attention(qm, kv):
    B, S, _ = qm.shape
    M = kv.shape[1]
    k = kv[..., :MEM_W].reshape(B, M, MEM_HEADS, MEM_DH)
    v = kv[..., MEM_W:].reshape(B, M, MEM_HEADS, MEM_DH)
    q = qm.reshape(B, S, MEM_HEADS, MEM_DH)
    s = jnp.einsum('bshd,bmhd->bhsm', q, k) * (MEM_DH ** -0.5)
    p = jax.nn.softmax(s, axis=-1)
    return jnp.einsum('bhsm,bmhd->bshd', p, v).reshape(B, S, MEM_W)


def _mixer_a(h, mem, p, s, layer):
    B, S, _ = h.shape
    proj = _project(h, p["a_w_in"][layer], s["a_w_in"][layer])
    c1 = 2 * DN_QK_W + DN_V_W
    qkv = proj[..., :c1]
    z = proj[..., c1:QKVZ_W]
    qm = proj[..., QKVZ_W:QKVZ_W + MEM_W]
    a = proj[..., QKVZ_W + MEM_W:QKVZ_W + MEM_W + DN_HEADS]
    b = proj[..., QKVZ_W + MEM_W + DN_HEADS:QKVZ_W + MEM_W + 2 * DN_HEADS]
    qkv = jax.nn.silu(_causal_conv(qkv, p["a_conv_w"][layer]))
    q = _l2_normalize(qkv[..., :DN_QK_W].reshape(B, S, DN_HEADS, DN_DK))
    k = _l2_normalize(qkv[..., DN_QK_W:2 * DN_QK_W].reshape(B, S, DN_HEADS, DN_DK))
    v = qkv[..., 2 * DN_QK_W:].reshape(B, S, DN_HEADS, DN_DV)
    beta = jax.nn.sigmoid(b)
    g = -jnp.exp(p["a_A_log"][layer]) * jax.nn.softplus(a + p["a_dt_bias"][layer])
    o = _gated_delta_rule(q, k, v, g, beta)
    o = o * lax.rsqrt(jnp.mean(o * o, axis=-1, keepdims=True) + NORM_EPS) * p["a_norm_w"][layer]
    o = o * jax.nn.silu(z.reshape(B, S, DN_HEADS, DN_DV))
    o = o.reshape(B, S, DN_V_W)
    kv = _project(mem, p["mem_w_kv"][layer], s["mem_w_kv"][layer])
    mo = _memory_attention(qm, kv)
    return _project(jnp.concatenate([o, mo], axis=-1), p["w_o"][layer], s["w_o"][layer])


def _mixer_b(h, mem, k_sh, v_sh, cos, sin, p, s, layer):
    B, S, _ = h.shape
    j = layer - N_A
    proj = _project(h, p["b_w_in"][j], s["b_w_in"][j])
    q = _apply_rope(proj[..., :SWA_Q_W].reshape(B, S, SWA_HEADS, SWA_DH), cos, sin)
    q = q.reshape(B, S, SWA_KV_HEADS, SWA_GROUP, SWA_DH)
    o = _swa_sink_attention(q, k_sh, v_sh, p["b_sinks"][j])
    kv = _project(mem, p["mem_w_kv"][layer], s["mem_w_kv"][layer])
    mo = _memory_attention(proj[..., SWA_Q_W:], kv)
    return _project(jnp.concatenate([o, mo], axis=-1), p["w_o"][layer], s["w_o"][layer])


def _forward(p, s, x, mem, positions):
    cos, sin = _rope_tables(positions, SWA_DH)
    h = x
    k_sh = v_sh = None
    for layer in range(DEPTH):
        if layer < N_A:
            mix = _mixer_a(h, mem, p, s, layer)
        else:
            mix = _mixer_b(h, mem, k_sh, v_sh, cos, sin, p, s, layer)
        seq = h.shape[1]
        h2 = _ln_res(h[0], mix[0], p["ln_g"][layer, 0][None], p["ln_b"][layer, 0][None])
        down = _mlp(h2, p["mlp_w_up"][layer], p["mlp_w_down"][layer], s["mlp_w_up"][layer], s["mlp_w_down"][layer])
        h = _ln_res(h2, down, p["ln_g"][layer, 1][None], p["ln_b"][layer, 1][None]).reshape(1, seq, D_MODEL)
        if layer == N_A - 1:
            B, S, _ = h.shape
            kv = _project(h, p["w_kv_shared"], s["w_kv_shared"])
            k_sh = _apply_rope(kv[..., :SWA_KV_W].reshape(B, S, SWA_KV_HEADS, SWA_DH), cos, sin)
            v_sh = kv[..., SWA_KV_W:].reshape(B, S, SWA_KV_HEADS, SWA_DH)
    return h


def _loss(diff, s, p, mem, positions, target):
    y = _forward({**p, **diff["small"]}, s, diff["x"], mem, positions)
    return 0.5 * jnp.sum(jnp.mean(jnp.square(y - target), axis=-1))


def _reorder_a_w_in(w):
    pad = jnp.zeros(w.shape[:-1] + (A_IN_PAD - A_IN,), w.dtype)
    return jnp.concatenate([w[..., :QKVZ_W], w[..., QKVZ_W + 2 * DN_HEADS:], w[..., QKVZ_W:QKVZ_W + 2 * DN_HEADS], pad],
                           axis=-1)


def _restore_a_w_in(w):
    return jnp.concatenate([w[..., :QKVZ_W], w[..., QKVZ_W + MEM_W:QKVZ_W + MEM_W + 2 * DN_HEADS],
                            w[..., QKVZ_W:QKVZ_W + MEM_W]], axis=-1)


def kernel(x, mem, positions, a_w_in, a_conv_w, a_A_log, a_dt_bias, a_norm_w, b_w_in, b_sinks, w_kv_shared, mem_w_kv, w_o, mlp_w_up, mlp_w_down, ln_g, ln_b, loss_target, m_a_w_in, m_a_conv_w, m_a_A_log, m_a_dt_bias, m_a_norm_w, m_b_w_in, m_b_sinks, m_w_kv_shared, m_mem_w_kv, m_w_o, m_mlp_w_up, m_mlp_w_down, m_ln_g, m_ln_b, v_a_w_in, v_a_conv_w, v_a_A_log, v_a_dt_bias, v_a_norm_w, v_b_w_in, v_b_sinks, v_w_kv_shared, v_mem_w_kv, v_w_o, v_mlp_w_up, v_mlp_w_down, v_ln_g, v_ln_b):
    w_sh = dict(a_w_in=a_w_in, a_conv_w=a_conv_w, a_A_log=a_A_log, a_dt_bias=a_dt_bias, a_norm_w=a_norm_w,
                b_w_in=b_w_in, b_sinks=b_sinks, w_kv_shared=w_kv_shared, mem_w_kv=mem_w_kv, w_o=w_o,
                mlp_w_up=mlp_w_up, mlp_w_down=mlp_w_down, ln_g=ln_g, ln_b=ln_b)
    m_sh = dict(a_w_in=m_a_w_in, a_conv_w=m_a_conv_w, a_A_log=m_a_A_log, a_dt_bias=m_a_dt_bias, a_norm_w=m_a_norm_w,
                b_w_in=m_b_w_in, b_sinks=m_b_sinks, w_kv_shared=m_w_kv_shared, mem_w_kv=m_mem_w_kv, w_o=m_w_o,
                mlp_w_up=m_mlp_w_up, mlp_w_down=m_mlp_w_down, ln_g=m_ln_g, ln_b=m_ln_b)
    v_sh = dict(a_w_in=v_a_w_in, a_conv_w=v_a_conv_w, a_A_log=v_a_A_log, a_dt_bias=v_a_dt_bias, a_norm_w=v_a_norm_w,
                b_w_in=v_b_w_in, b_sinks=v_b_sinks, w_kv_shared=v_w_kv_shared, mem_w_kv=v_mem_w_kv, w_o=v_w_o,
                mlp_w_up=v_mlp_w_up, mlp_w_down=v_mlp_w_down, ln_g=v_ln_g, ln_b=v_ln_b)
    shard_shapes = {n: w_sh[n].shape for n in WEIGHTS}
    rb, rows = _rows_for(w_sh)

    big, small = _pack(w_sh, rb, jnp.bfloat16)
    gbig, gsmall = _gather_weights(big, small)
    pieces = [_unpack(gbig[q], gsmall[q], shard_shapes) for q in range(N_CHIPS)]
    full = {n: jnp.concatenate([pieces[q][n] for q in range(N_CHIPS)], axis=SHARD_AXIS[n]) for n in SHARD_AXIS}
    for n in REPLICATED:
        full[n] = w_sh[n]
    big_w = {n: full[n] for n in BIG}
    big_w["a_w_in"] = _reorder_a_w_in(big_w["a_w_in"])
    small_w = {n: full[n] for n in SMALL}
    slots = {n: jnp.zeros(big_w[n].shape, jnp.float32) for n in BIG}

    loss, (grads, g_slots) = jax.value_and_grad(_loss, argnums=(0, 1))(
        {"x": x, "small": small_w}, slots, big_w, mem, positions, loss_target)
    loss = lax.psum(loss, ("x", "y", "c"))
    g_full = {**g_slots, **grads["small"]}
    g_full["a_w_in"] = _restore_a_w_in(g_full["a_w_in"])

    def shard_of(n, q):
        if n in REPLICATED:
            return g_full[n]
        size = shard_shapes[n][SHARD_AXIS[n]]
        return lax.slice_in_dim(g_full[n], q * size, (q + 1) * size, axis=SHARD_AXIS[n])

    parts = []
    for q in range(N_CHIPS):
        pb, ps = _pack({n: shard_of(n, q) for n in WEIGHTS}, rb, jnp.bfloat16)
        parts.append(jnp.concatenate([pb, ps.astype(jnp.bfloat16)], axis=0))
    received = _scatter_grads(jnp.stack(parts))
    mine = _sum_chips(received)
    other = _swap_with_sibling(mine)

    flat = [jnp.concatenate(_pack(d, rb), axis=0) for d in (w_sh, m_sh, v_sh)]
    outs = _adamw(mine, other, *flat)
    g_o, d_o, m_o, v_o = [_unpack(o[:rb], o[rb:], shard_shapes) for o in outs]
    return (loss, grads["x"], *[g_o[n] for n in WEIGHTS], *[d_o[n] for n in WEIGHTS],
            *[m_o[n] for n in WEIGHTS], *[v_o[n] for n in WEIGHTS])
```

```python
import functools
import math

import jax
import jax.numpy as jnp
from jax import lax
from jax.experimental import pallas as pl
from jax.experimental.pallas import tpu as pltpu

D_MODEL = 1024
DEPTH = 4
N_A = DEPTH // 2
N_B = DEPTH - N_A
MEM_HEADS = 4
MEM_DH = D_MODEL // 16
MEM_W = MEM_HEADS * MEM_DH
DN_DK = 128
DN_DV = 128
DN_HEADS = (3 * D_MODEL) // (4 * DN_DV)
DN_QK_W = DN_HEADS * DN_DK
DN_V_W = DN_HEADS * DN_DV
CONV_WIDTH = 4
CHUNK = 64
SWA_DH = 64
SWA_HEADS = (3 * D_MODEL) // (4 * SWA_DH)
SWA_KV_HEADS = 2
SWA_GROUP = SWA_HEADS // SWA_KV_HEADS
SWA_Q_W = SWA_HEADS * SWA_DH
SWA_KV_W = SWA_KV_HEADS * SWA_DH
WINDOW = 128
ROPE_THETA = 10000.0
MLP_HIDDEN = 4 * D_MODEL
LN_EPS = 1e-5
NORM_EPS = 1e-6
DN_ALPHA = (2.0 * DEPTH) ** 0.25
A_IN = 2 * DN_QK_W + 2 * DN_V_W + 2 * DN_HEADS + MEM_W
A_IN_PAD = 3456
QKVZ_W = 2 * DN_QK_W + 2 * DN_V_W

ADAM_LR = 0.001
ADAM_B1 = 0.9
ADAM_B2 = 0.999
ADAM_EPS = 1e-08
ADAM_WD = 0.01
ADAM_STEP = 10

N_CHIPS = 4
FLAT_W = 1024
BIG = ("a_w_in", "b_w_in", "w_kv_shared", "mem_w_kv", "w_o", "mlp_w_up", "mlp_w_down")
SMALL = ("a_conv_w", "ln_g", "ln_b", "a_A_log", "a_dt_bias", "a_norm_w", "b_sinks")
REPLICATED = ("a_A_log", "a_dt_bias", "a_norm_w", "b_sinks")
WEIGHTS = ("a_w_in", "a_conv_w", "a_A_log", "a_dt_bias", "a_norm_w", "b_w_in", "b_sinks", "w_kv_shared",
           "mem_w_kv", "w_o", "mlp_w_up", "mlp_w_down", "ln_g", "ln_b")
SHARD_AXIS = {"a_w_in": 2, "a_conv_w": 2, "b_w_in": 1, "w_kv_shared": 0, "mem_w_kv": 1, "w_o": 1,
              "mlp_w_up": 2, "mlp_w_down": 1, "ln_g": 2, "ln_b": 2}
SMALL_ROWS = 16
ROW_ALIGN = 256

MESH = pl.DeviceIdType.MESH
HBM_SPEC = pl.BlockSpec(memory_space=pltpu.HBM)
VMEM_LIMIT = 48 * 1024 * 1024


def _rows_for(shards):
    n_big = sum(math.prod(shards[n].shape) for n in BIG)
    n_small = sum(math.prod(shards[n].shape) for n in SMALL)
    assert n_small <= SMALL_ROWS * FLAT_W
    total = -(-n_big // FLAT_W) + SMALL_ROWS
    total = -(-total // ROW_ALIGN) * ROW_ALIGN
    return total - SMALL_ROWS, total


def _pack(shards, rb, dtype_big=jnp.float32):
    big = jnp.concatenate([shards[n].reshape(-1).astype(dtype_big) for n in BIG])
    big = jnp.pad(big, (0, rb * FLAT_W - big.shape[0])).reshape(rb, FLAT_W)
    small = jnp.concatenate([shards[n].reshape(-1).astype(jnp.float32) for n in SMALL])
    small = jnp.pad(small, (0, SMALL_ROWS * FLAT_W - small.shape[0])).reshape(SMALL_ROWS, FLAT_W)
    return big, small


def _unpack(big, small, shapes):
    out = {}
    for flat, names in ((big.reshape(-1), BIG), (small.reshape(-1), SMALL)):
        off = 0
        for n in names:
            size = math.prod(shapes[n])
            out[n] = flat[off:off + size].reshape(shapes[n])
            off += size
    return out


def _other_chips(x, y):
    return [(1 - x, y), (x, 1 - y), (1 - x, 1 - y)]


def _gather_weights(big, small):
    def body(big_ref, small_ref, obig_ref, osmall_ref, send_sems, recv_sems, local_sems):
        x, y, c = lax.axis_index("x"), lax.axis_index("y"), lax.axis_index("c")
        me = 2 * x + y
        pairs = ((big_ref, obig_ref), (small_ref, osmall_ref))
        local = [pltpu.make_async_copy(src, dst.at[me], local_sems.at[i]) for i, (src, dst) in enumerate(pairs)]
        for cp in local:
            cp.start()
        sends = []
        for j, (px, py) in enumerate(_other_chips(x, y)):
            for i, (src, dst) in enumerate(pairs):
                sends.append(pltpu.make_async_remote_copy(
                    src_ref=src, dst_ref=dst.at[me], send_sem=send_sems.at[2 * j + i], recv_sem=recv_sems.at[2 * j + i],
                    device_id=(px, py, c), device_id_type=MESH))
        for cp in sends:
            cp.start()
        for j, (px, py) in enumerate(_other_chips(x, y)):
            for i, (src, dst) in enumerate(pairs):
                pltpu.make_async_remote_copy(
                    src_ref=src, dst_ref=dst.at[2 * px + py], send_sem=send_sems.at[2 * j + i],
                    recv_sem=recv_sems.at[2 * j + i], device_id=(px, py, c), device_id_type=MESH).wait_recv()
        for cp in sends:
            cp.wait_send()
        for cp in local:
            cp.wait()

    return pl.pallas_call(
        body, name="gather_weights",
        out_shape=(jax.ShapeDtypeStruct((N_CHIPS,) + big.shape, big.dtype),
                   jax.ShapeDtypeStruct((N_CHIPS,) + small.shape, small.dtype)),
        in_specs=[HBM_SPEC, HBM_SPEC], out_specs=(HBM_SPEC, HBM_SPEC),
        scratch_shapes=[pltpu.SemaphoreType.DMA((6,)), pltpu.SemaphoreType.DMA((6,)), pltpu.SemaphoreType.DMA((2,))],
    )(big, small)


def _scatter_grads(g):
    def body(g_ref, o_ref, send_sems, recv_sems, local_sem):
        x, y, c = lax.axis_index("x"), lax.axis_index("y"), lax.axis_index("c")
        me = 2 * x + y
        local = pltpu.make_async_copy(g_ref.at[me], o_ref.at[me], local_sem)
        local.start()
        sends = []
        for j, (px, py) in enumerate(_other_chips(x, y)):
            sends.append(pltpu.make_async_remote_copy(
                src_ref=g_ref.at[2 * px + py], dst_ref=o_ref.at[me], send_sem=send_sems.at[j], recv_sem=recv_sems.at[j],
                device_id=(px, py, c), device_id_type=MESH))
        for cp in sends:
            cp.start()
        for j, (px, py) in enumerate(_other_chips(x, y)):
            pltpu.make_async_remote_copy(
                src_ref=g_ref.at[me], dst_ref=o_ref.at[2 * px + py], send_sem=send_sems.at[j], recv_sem=recv_sems.at[j],
                device_id=(px, py, c), device_id_type=MESH).wait_recv()
        for cp in sends:
            cp.wait_send()
        local.wait()

    return pl.pallas_call(
        body, name="scatter_grads",
        out_shape=jax.ShapeDtypeStruct(g.shape, g.dtype),
        in_specs=[HBM_SPEC], out_specs=HBM_SPEC,
        scratch_shapes=[pltpu.SemaphoreType.DMA((3,)), pltpu.SemaphoreType.DMA((3,)), pltpu.SemaphoreType.DMA],
    )(g)


def _swap_with_sibling(v):
    def body(v_ref, o_ref, send_sem, recv_sem):
        x, y, c = lax.axis_index("x"), lax.axis_index("y"), lax.axis_index("c")
        cp = pltpu.make_async_remote_copy(src_ref=v_ref, dst_ref=o_ref, send_sem=send_sem, recv_sem=recv_sem,
                                          device_id=(x, y, 1 - c), device_id_type=MESH)
        cp.start()
        cp.wait()

    return pl.pallas_call(
        body, name="swap_with_sibling",
        out_shape=jax.ShapeDtypeStruct(v.shape, v.dtype),
        in_specs=[HBM_SPEC], out_specs=HBM_SPEC,
        scratch_shapes=[pltpu.SemaphoreType.DMA, pltpu.SemaphoreType.DMA],
    )(v)


def _sum_chips(parts):
    n, rows, width = parts.shape

    def body(p_ref, o_ref):
        p = [p_ref[q].astype(jnp.float32) for q in range(n)]
        o_ref[...] = (p[0] + p[1]) + (p[2] + p[3])

    return pl.pallas_call(
        body, name="sum_chips", grid=(rows // ROW_ALIGN,),
        out_shape=jax.ShapeDtypeStruct((rows, width), jnp.float32),
        in_specs=[pl.BlockSpec((n, ROW_ALIGN, width), lambda i: (0, i, 0))],
        out_specs=pl.BlockSpec((ROW_ALIGN, width), lambda i: (i, 0)),
        compiler_params=pltpu.CompilerParams(dimension_semantics=("parallel",), vmem_limit_bytes=VMEM_LIMIT),
    )(parts)


def _adamw(mine, other, w, m, v):
    rows, width = w.shape
    blk = ROW_ALIGN // 2

    def body(a_ref, b_ref, w_ref, m_ref, v_ref, g_out, d_out, m_out, v_out):
        g = a_ref[...] + b_ref[...]
        m_new = ADAM_B1 * m_ref[...] + (1.0 - ADAM_B1) * g
        v_new = ADAM_B2 * v_ref[...] + (1.0 - ADAM_B2) * jnp.square(g)
        m_hat = m_new / (1.0 - ADAM_B1 ** ADAM_STEP)
        v_hat = v_new / (1.0 - ADAM_B2 ** ADAM_STEP)
        g_out[...] = g
        d_out[...] = -ADAM_LR * (m_hat / (jnp.sqrt(v_hat) + ADAM_EPS) + ADAM_WD * w_ref[...])
        m_out[...] = m_new
        v_out[...] = v_new

    spec = pl.BlockSpec((blk, width), lambda i: (i, 0))
    shape = jax.ShapeDtypeStruct((rows, width), jnp.float32)
    return pl.pallas_call(
        body, name="adamw", grid=(rows // blk,),
        out_shape=(shape,) * 4, in_specs=[spec] * 5, out_specs=(spec,) * 4,
        compiler_params=pltpu.CompilerParams(dimension_semantics=("parallel",), vmem_limit_bytes=VMEM_LIMIT),
    )(mine, other, w, m, v)


def _tile(dim, pref):
    if dim <= pref:
        return dim
    for t in range(pref - pref % 128, 0, -128):
        if dim % t == 0:
            return t
    raise ValueError(f"no 128-aligned tile for {dim}")


def _matmul(a, b, *, ta=False, tb=False, name, epilogue=None, extra=None, out_dtype=jnp.float32):
    (k_a, m) = a.shape if ta else a.shape[::-1]
    (k_b, n) = b.shape[::-1] if tb else b.shape
    assert k_a == k_b, (a.shape, b.shape, ta, tb)
    k = k_a
    tm, tn, tk = _tile(m, 1024), _tile(n, 512), _tile(k, 1024)
    nk = k // tk
    a_spec = pl.BlockSpec((tk, tm), lambda i, j, l: (l, i)) if ta else pl.BlockSpec((tm, tk), lambda i, j, l: (i, l))
    b_spec = pl.BlockSpec((tn, tk), lambda i, j, l: (j, l)) if tb else pl.BlockSpec((tk, tn), lambda i, j, l: (l, j))
    o_spec = pl.BlockSpec((tm, tn), lambda i, j, l: (i, j))
    dims = (((0 if ta else 1,), (1 if tb else 0,)), ((), ()))
    has_extra = epilogue == "relu2_grad"
    assert has_extra == (extra is not None)

    def body(*refs):
        a_ref, b_ref = refs[:2]
        outs, acc_ref = refs[2 + has_extra:-1], refs[-1]
        l = pl.program_id(2)

        @pl.when(l == 0)
        def _():
            acc_ref[...] = jnp.zeros_like(acc_ref)

        acc_ref[...] += lax.dot_general(a_ref[...].astype(jnp.bfloat16), b_ref[...].astype(jnp.bfloat16), dims,
                                        preferred_element_type=jnp.float32)

        @pl.when(l == nk - 1)
        def _():
            acc = acc_ref[...]
            if epilogue is None:
                outs[0][...] = acc.astype(out_dtype)
            elif epilogue == "relu2":
                outs[0][...] = acc.astype(jnp.bfloat16)
                outs[1][...] = jnp.square(jnp.maximum(acc, 0.0)).astype(jnp.bfloat16)
            else:
                outs[0][...] = (acc * (2.0 * jnp.maximum(refs[2][...].astype(jnp.float32), 0.0))).astype(out_dtype)

    if epilogue == "relu2":
        out_shape = (jax.ShapeDtypeStruct((m, n), jnp.bfloat16),) * 2
        out_specs = (o_spec, o_spec)
    else:
        out_shape = jax.ShapeDtypeStruct((m, n), out_dtype)
        out_specs = o_spec
    return pl.pallas_call(
        body, name=name, grid=(m // tm, n // tn, nk), out_shape=out_shape,
        in_specs=[a_spec, b_spec] + ([o_spec] if has_extra else []), out_specs=out_specs,
        scratch_shapes=[pltpu.VMEM((tm, tn), jnp.float32)],
        compiler_params=pltpu.CompilerParams(dimension_semantics=("parallel", "parallel", "arbitrary"),
                                             vmem_limit_bytes=VMEM_LIMIT),
    )(*((a, b) + ((extra,) if has_extra else ())))


def _lo(x):
    return lax.stop_gradient(x.astype(jnp.bfloat16))


@jax.custom_vjp
def _linear(x, x_lo, w, slot):
    del x, slot
    return _matmul(x_lo, w, name="linear_fwd")


def _linear_fwd(x, x_lo, w, slot):
    del x, slot
    return _matmul(x_lo, w, name="linear_fwd"), (x_lo, w)


def _linear_bwd(res, dy):
    x_lo, w = res
    dx = _matmul(dy, w, tb=True, name="linear_dx")
    dw = _matmul(x_lo, dy, ta=True, name="linear_dw")
    return dx, jnp.zeros_like(x_lo), jnp.zeros_like(w), dw


_linear.defvjp(_linear_fwd, _linear_bwd)


@jax.custom_vjp
def _mlp(h, h_lo, w_up, w_down, slot_up, slot_down):
    return _mlp_fwd(h, h_lo, w_up, w_down, slot_up, slot_down)[0]


def _mlp_fwd(h, h_lo, w_up, w_down, slot_up, slot_down):
    del h, slot_up, slot_down
    up, act = _matmul(h_lo, w_up, name="mlp_up", epilogue="relu2")
    return _matmul(act, w_down, name="mlp_down"), (h_lo, up, act, w_up, w_down)


def _mlp_bwd(res, dy):
    h_lo, up, act, w_up, w_down = res
    d_up = _matmul(dy, w_down, tb=True, name="mlp_d_up", epilogue="relu2_grad", extra=up, out_dtype=jnp.bfloat16)
    dw_down = _matmul(act, dy, ta=True, name="mlp_dw_down")
    dw_up = _matmul(h_lo, d_up, ta=True, name="mlp_dw_up")
    dh = _matmul(d_up, w_up, tb=True, name="mlp_dh")
    return dh, jnp.zeros_like(h_lo), jnp.zeros_like(w_up), jnp.zeros_like(w_down), dw_up, dw_down


_mlp.defvjp(_mlp_fwd, _mlp_bwd)


LN_ROWS = 256


def _ln_call(h, mix, g, b):
    s, d = h.shape
    tok = pl.BlockSpec((LN_ROWS, d), lambda i: (i, 0))
    vec = pl.BlockSpec((1, d), lambda i: (0, 0))
    stat = pl.BlockSpec((LN_ROWS, 1), lambda i: (i, 0))

    def body(h_ref, mix_ref, g_ref, b_ref, y_ref, ylo_ref, xhat_ref, rstd_ref):
        z = DN_ALPHA * h_ref[...] + mix_ref[...]
        mu = jnp.mean(z, axis=-1, keepdims=True)
        zc = z - mu
        rstd = lax.rsqrt(jnp.mean(jnp.square(zc), axis=-1, keepdims=True) + LN_EPS)
        xhat = zc * rstd
        y = xhat * g_ref[...] + b_ref[...]
        y_ref[...] = y
        ylo_ref[...] = y.astype(ylo_ref.dtype)
        xhat_ref[...] = xhat
        rstd_ref[...] = rstd

    sd = jax.ShapeDtypeStruct
    return pl.pallas_call(
        body, name="ln_fwd", grid=(s // LN_ROWS,),
        out_shape=(sd((s, d), jnp.float32), sd((s, d), jnp.bfloat16), sd((s, d), jnp.float32), sd((s, 1), jnp.float32)),
        in_specs=[tok, tok, vec, vec], out_specs=(tok, tok, tok, stat),
        compiler_params=pltpu.CompilerParams(dimension_semantics=("parallel",)),
    )(h, mix, g, b)


def _ln_grad_call(dy, xhat, rstd, g):
    s, d = dy.shape
    tok = pl.BlockSpec((LN_ROWS, d), lambda i: (i, 0))
    vec = pl.BlockSpec((1, d), lambda i: (0, 0))
    stat = pl.BlockSpec((LN_ROWS, 1), lambda i: (i, 0))

    def body(dy_ref, xhat_ref, rstd_ref, g_ref, dz_ref, dg_ref, db_ref):
        @pl.when(pl.program_id(0) == 0)
        def _():
            dg_ref[...] = jnp.zeros_like(dg_ref)
            db_ref[...] = jnp.zeros_like(db_ref)

        dy, xhat = dy_ref[...], xhat_ref[...]
        dyg = dy * g_ref[...]
        m1 = jnp.mean(dyg, axis=-1, keepdims=True)
        m2 = jnp.mean(dyg * xhat, axis=-1, keepdims=True)
        dz_ref[...] = rstd_ref[...] * (dyg - m1 - xhat * m2)
        dg_ref[...] += jnp.sum(dy * xhat, axis=0, keepdims=True)
        db_ref[...] += jnp.sum(dy, axis=0, keepdims=True)

    sd = jax.ShapeDtypeStruct
    return pl.pallas_call(
        body, name="ln_bwd", grid=(s // LN_ROWS,),
        out_shape=(sd((s, d), jnp.float32), sd((1, d), jnp.float32), sd((1, d), jnp.float32)),
        in_specs=[tok, tok, stat, vec], out_specs=(tok, vec, vec),
        compiler_params=pltpu.CompilerParams(dimension_semantics=("arbitrary",)),
    )(dy, xhat, rstd, g)


@jax.custom_vjp
def _ln_res(h, mix, g, b):
    return _ln_call(h, mix, g, b)[:2]


def _ln_res_fwd(h, mix, g, b):
    y, y_lo, xhat, rstd = _ln_call(h, mix, g, b)
    return (y, y_lo), (xhat, rstd, g)


def _ln_res_bwd(res, cts):
    xhat, rstd, g = res
    dz, dg, db = _ln_grad_call(cts[0], xhat, rstd, g)
    return DN_ALPHA * dz, dz, dg, db


_ln_res.defvjp(_ln_res_fwd, _ln_res_bwd)


MXU_DTYPE = jnp.bfloat16
DN_CB = 8
DN_SCALE = DN_DK ** -0.5


def _dot(a, b, ca=1, cb=0):
    return lax.dot_general(a.astype(MXU_DTYPE), b.astype(MXU_DTYPE), (((ca,), (cb,)), ((), ())),
                           preferred_element_type=jnp.float32)


def _chunk_masks():
    row = lax.broadcasted_iota(jnp.int32, (CHUNK, CHUNK), 0)
    col = lax.broadcasted_iota(jnp.int32, (CHUNK, CHUNK), 1)
    return row >= col, row > col, row == col


def _to_col(row_vec):
    _, _, eye = _chunk_masks()
    return jnp.sum(jnp.where(eye, jnp.broadcast_to(row_vec, (CHUNK, CHUNK)), 0.0), axis=1, keepdims=True)


def _to_row(col_vec):
    _, _, eye = _chunk_masks()
    return jnp.sum(jnp.where(eye, jnp.broadcast_to(col_vec, (CHUNK, CHUNK)), 0.0), axis=0, keepdims=True)


def _last_row(col_vec):
    last = lax.broadcasted_iota(jnp.int32, (CHUNK, 1), 0) == CHUNK - 1
    return jnp.sum(jnp.where(last, col_vec, 0.0), axis=0, keepdims=True), last


def _chunk_terms(q, k, beta, gcc, gcr):
    incl, strict, _ = _chunk_masks()
    decay = jnp.where(incl, jnp.exp(jnp.minimum(gcc - gcr, 0.0)), 0.0)
    kb = k * beta
    lmat = jnp.where(strict, _dot(kb, k, 1, 1) * decay, 0.0)
    intra = jnp.where(incl, _dot(q, k, 1, 1) * decay, 0.0)
    return decay, kb, lmat, intra


def _dot3(a, b, ca=1, cb=0):
    if MXU_DTYPE == jnp.float32:
        return _dot(a, b, ca, cb)
    a_hi, b_hi = a.astype(MXU_DTYPE), b.astype(MXU_DTYPE)
    a_lo = (a - a_hi.astype(jnp.float32)).astype(MXU_DTYPE)
    b_lo = (b - b_hi.astype(jnp.float32)).astype(MXU_DTYPE)
    return _dot(a_hi, b_hi, ca, cb) + (_dot(a_hi, b_lo, ca, cb) + _dot(a_lo, b_hi, ca, cb))


def _unit_lower_inverse(lmat):
    _, _, eye = _chunk_masks()
    ident = jnp.where(eye, 1.0, 0.0)
    x = -lmat
    t = ident + x
    p = _dot(x, x)
    for _ in range(4):
        t = t + _dot(t, p)
        p = _dot(p, p)
    t = t + _dot(t, p)
    resid = (t - ident) + _dot3(lmat, t)
    return t - _dot(t, resid)


def _dn_specs(n_chunks):
    tok = pl.BlockSpec((DN_CB * CHUNK, DN_DK), lambda h, n: (n, h))
    rowv = pl.BlockSpec((None, DN_CB, CHUNK), lambda h, n: (h, n, 0))
    sq = pl.BlockSpec((None, DN_CB, CHUNK, CHUNK), lambda h, n: (h, n, 0, 0))
    lane = pl.BlockSpec((None, DN_CB, 1, DN_DV), lambda h, n: (h, n, 0, 0))
    return tok, rowv, sq, lane


def _dn_prep(q, k, v, beta, gc):
    s = q.shape[0]
    n_chunks = s // CHUNK
    tok, rowv, sq, lane = _dn_specs(n_chunks)

    def body(q_ref, k_ref, v_ref, beta_ref, gc_ref, u_ref, w_ref, qd_ref, kd_ref, intra_ref, t_ref, cd_ref):
        for c in range(DN_CB):
            rows = pl.ds(c * CHUNK, CHUNK)
            q_c, k_c, v_c = q_ref[rows, :] * DN_SCALE, k_ref[rows, :], v_ref[rows, :]
            gcr_c = gc_ref[pl.ds(c, 1), :]
            beta_c, gcc_c = _to_col(beta_ref[pl.ds(c, 1), :]), _to_col(gcr_c)
            _, kb, lmat, intra = _chunk_terms(q_c, k_c, beta_c, gcc_c, gcr_c)
            t = _unit_lower_inverse(lmat)
            eg = jnp.exp(gcc_c)
            g_last, _ = _last_row(gcc_c)
            sol = _dot3(t, jnp.concatenate([v_c * beta_c, kb * eg], axis=1))
            u_ref[rows, :] = sol[:, :DN_DV]
            w_ref[rows, :] = sol[:, DN_DV:].astype(w_ref.dtype)
            qd_ref[rows, :] = (q_c * eg).astype(qd_ref.dtype)
            kd_ref[rows, :] = (k_c * jnp.exp(g_last - gcc_c)).astype(kd_ref.dtype)
            intra_ref[c] = intra.astype(intra_ref.dtype)
            t_ref[c] = t
            cd_ref[c] = jnp.broadcast_to(jnp.exp(g_last), (1, DN_DV))

    f32, mx = jnp.float32, MXU_DTYPE
    sd = jax.ShapeDtypeStruct
    return pl.pallas_call(
        body, name="dn_prep", grid=(DN_HEADS, n_chunks // DN_CB),
        out_shape=(sd(q.shape, f32), sd(q.shape, mx), sd(q.shape, mx), sd(q.shape, mx),
                   sd((DN_HEADS, n_chunks, CHUNK, CHUNK), mx), sd((DN_HEADS, n_chunks, CHUNK, CHUNK), f32),
                   sd((DN_HEADS, n_chunks, 1, DN_DV), f32)),
        in_specs=[tok, tok, tok, rowv, rowv], out_specs=(tok, tok, tok, tok, sq, sq, lane),
        compiler_params=pltpu.CompilerParams(dimension_semantics=("parallel", "parallel")),
    )(q, k, v, beta, gc)


def _dn_scan(u, w, qd, kd, intra, cd):
    s, width = u.shape
    n_chunks = s // CHUNK
    tok = pl.BlockSpec((CHUNK, width), lambda n: (n, 0))
    sq = pl.BlockSpec((DN_HEADS, None, CHUNK, CHUNK), lambda n: (0, n, 0, 0))
    lane = pl.BlockSpec((DN_HEADS, None, 1, DN_DV), lambda n: (0, n, 0, 0))
    st = pl.BlockSpec((DN_HEADS, None, DN_DK, DN_DV), lambda n: (0, n, 0, 0))

    def body(u_ref, w_ref, qd_ref, kd_ref, intra_ref, cd_ref, o_ref, vn_ref, st_ref, state):
        @pl.when(pl.program_id(0) == 0)
        def _():
            state[...] = jnp.zeros_like(state)

        for h in range(DN_HEADS):
            cols = pl.ds(h * DN_DK, DN_DK)
            s_h = state[h]
            s_mx = s_h.astype(MXU_DTYPE)
            st_ref[h] = s_mx.astype(st_ref.dtype)
            v_new = u_ref[:, cols] - _dot(w_ref[:, cols], s_mx)
            vn_ref[:, cols] = v_new.astype(vn_ref.dtype)
            o_ref[:, cols] = _dot(qd_ref[:, cols], s_mx) + _dot(intra_ref[h], v_new)
            state[h] = s_h * cd_ref[h] + _dot(kd_ref[:, cols], v_new, 0, 0)

    sd = jax.ShapeDtypeStruct
    return pl.pallas_call(
        body, name="dn_scan", grid=(n_chunks,),
        out_shape=(sd(u.shape, jnp.float32), sd(u.shape, MXU_DTYPE),
                   sd((DN_HEADS, n_chunks, DN_DK, DN_DV), MXU_DTYPE)),
        in_specs=[tok, tok, tok, tok, sq, lane], out_specs=(tok, tok, st),
        scratch_shapes=[pltpu.VMEM((DN_HEADS, DN_DK, DN_DV), jnp.float32)],
        compiler_params=pltpu.CompilerParams(dimension_semantics=("arbitrary",)),
    )(u, w, qd, kd, intra, cd)


def _dn_bwd_scan(do, w, qd, kd, intra, cd, vn, st):
    s, width = do.shape
    n_chunks = s // CHUNK
    last = n_chunks - 1
    tok = pl.BlockSpec((CHUNK, width), lambda n: (last - n, 0))
    sq = pl.BlockSpec((DN_HEADS, None, CHUNK, CHUNK), lambda n: (0, last - n, 0, 0))
    lane = pl.BlockSpec((DN_HEADS, None, 1, DN_DV), lambda n: (0, last - n, 0, 0))
    stt = pl.BlockSpec((DN_HEADS, None, DN_DK, DN_DV), lambda n: (0, last - n, 0, 0))

    def body(do_ref, w_ref, qd_ref, kd_ref, intra_ref, cd_ref, vn_ref, st_ref,
             du_ref, dw_ref, dqd_ref, dkd_ref, dintra_ref, dgl_ref, dstate):
        @pl.when(pl.program_id(0) == 0)
        def _():
            dstate[...] = jnp.zeros_like(dstate)

        for h in range(DN_HEADS):
            cols = pl.ds(h * DN_DK, DN_DK)
            ds_h = dstate[h]
            ds_mx = ds_h.astype(MXU_DTYPE)
            s_mx = st_ref[h]
            do_h = do_ref[:, cols].astype(MXU_DTYPE)
            dv_new = _dot(intra_ref[h], do_h, 0, 0) + _dot(kd_ref[:, cols], ds_mx)
            du_ref[:, cols] = dv_new
            dintra_ref[h] = _dot(do_h, vn_ref[:, cols], 1, 1)
            dqd_ref[:, cols] = _dot(do_h, s_mx, 1, 1)
            dkd_ref[:, cols] = _dot(vn_ref[:, cols], ds_mx, 1, 1)
            dw_ref[:, cols] = -_dot(dv_new, s_mx, 1, 1)
            cd_h = cd_ref[h]
            dcd = jnp.sum(jnp.sum(s_mx.astype(jnp.float32) * ds_h, axis=1, keepdims=True), axis=0, keepdims=True)
            dgl_ref[h] = dcd * cd_h
            dstate[h] = _dot(qd_ref[:, cols], do_h, 0, 0) + ds_h * cd_h - _dot(w_ref[:, cols], dv_new, 0, 0)

    sd = jax.ShapeDtypeStruct
    f32 = jnp.float32
    return pl.pallas_call(
        body, name="dn_bwd_scan", grid=(n_chunks,),
        out_shape=(sd(do.shape, f32), sd(do.shape, f32), sd(do.shape, f32), sd(do.shape, f32),
                   sd((DN_HEADS, n_chunks, CHUNK, CHUNK), f32), sd((DN_HEADS, n_chunks, 1, DN_DV), f32)),
        in_specs=[tok, tok, tok, tok, sq, lane, tok, stt], out_specs=(tok, tok, tok, tok, sq, lane),
        scratch_shapes=[pltpu.VMEM((DN_HEADS, DN_DK, DN_DV), f32)],
        compiler_params=pltpu.CompilerParams(dimension_semantics=("arbitrary",)),
    )(do, w, qd, kd, intra, cd, vn, st)


def _dn_bwd_chunks(q, k, v, beta, gc, t, u, w, du, dw, dqd, dkd, dintra, dgl):
    s = q.shape[0]
    n_chunks = s // CHUNK
    tok, rowv, sq, lane = _dn_specs(n_chunks)

    def body(q_ref, k_ref, v_ref, beta_ref, gc_ref, t_ref, u_ref, w_ref, du_ref, dw_ref, dqd_ref, dkd_ref,
             dintra_ref, dgl_ref, dq_ref, dk_ref, dv_ref, dbeta_ref, dgc_ref):
        incl, strict, _ = _chunk_masks()
        for c in range(DN_CB):
            rows = pl.ds(c * CHUNK, CHUNK)
            q_c, k_c, v_c = q_ref[rows, :] * DN_SCALE, k_ref[rows, :], v_ref[rows, :]
            gcr_c = gc_ref[pl.ds(c, 1), :]
            beta_c, gcc_c = _to_col(beta_ref[pl.ds(c, 1), :]), _to_col(gcr_c)
            decay, kb, lmat, intra = _chunk_terms(q_c, k_c, beta_c, gcc_c, gcr_c)
            eg = jnp.exp(gcc_c)
            g_last, is_last = _last_row(gcc_c)
            e_rev = jnp.exp(g_last - gcc_c)
            d_sol = jnp.concatenate([du_ref[rows, :], dw_ref[rows, :]], axis=1)
            d_rhs = _dot3(t_ref[c], d_sol, 0, 0)
            sol = jnp.concatenate([u_ref[rows, :], w_ref[rows, :].astype(jnp.float32)], axis=1)
            d_l = jnp.where(strict, -_dot(d_rhs, sol, 1, 1), 0.0)
            d_rhs_u, d_rhs_w = d_rhs[:, :DN_DV], d_rhs[:, DN_DV:]
            dv_ref[rows, :] = d_rhs_u * beta_c
            dbeta = jnp.sum(d_rhs_u * v_c, axis=1, keepdims=True)
            dkb = d_rhs_w * eg
            dgc = jnp.sum(d_rhs_w * kb * eg, axis=1, keepdims=True)
            d_a = d_l * decay
            dkb = dkb + _dot(d_a, k_c)
            dk = _dot(d_a, kb, 0, 0)
            m1 = d_l * lmat
            dgc = dgc + jnp.sum(m1, axis=1, keepdims=True)
            dgr = -jnp.sum(m1, axis=0, keepdims=True)
            d_intra = jnp.where(incl, dintra_ref[c], 0.0)
            d_qk = d_intra * decay
            dq = _dot(d_qk, k_c)
            dk = dk + _dot(d_qk, q_c, 0, 0)
            m2 = d_intra * intra
            dgc = dgc + jnp.sum(m2, axis=1, keepdims=True)
            dgr = dgr - jnp.sum(m2, axis=0, keepdims=True)
            dqd = dqd_ref[rows, :]
            dq = dq + dqd * eg
            dgc = dgc + jnp.sum(dqd * q_c * eg, axis=1, keepdims=True)
            dkd = dkd_ref[rows, :]
            dk = dk + dkd * e_rev
            tk = jnp.sum(dkd * k_c * e_rev, axis=1, keepdims=True)
            dgc = dgc - tk
            d_last = dgl_ref[c][:, :1] + jnp.sum(tk, axis=0, keepdims=True)
            dgc = dgc + jnp.where(is_last, d_last, 0.0)
            dk = dk + dkb * beta_c
            dbeta = dbeta + jnp.sum(dkb * k_c, axis=1, keepdims=True)
            dq_ref[rows, :] = dq * DN_SCALE
            dk_ref[rows, :] = dk
            dbeta_ref[pl.ds(c, 1), :] = _to_row(dbeta)
            dgc_ref[pl.ds(c, 1), :] = _to_row(dgc) + dgr

    sd = jax.ShapeDtypeStruct
    f32 = jnp.float32
    return pl.pallas_call(
        body, name="dn_bwd_chunks", grid=(DN_HEADS, n_chunks // DN_CB),
        out_shape=(sd(q.shape, f32), sd(q.shape, f32), sd(q.shape, f32), sd(beta.shape, f32), sd(gc.shape, f32)),
        in_specs=[tok, tok, tok, rowv, rowv, sq, tok, tok, tok, tok, tok, tok, sq, lane],
        out_specs=(tok, tok, tok, rowv, rowv),
        compiler_params=pltpu.CompilerParams(dimension_semantics=("parallel", "parallel")),
    )(q, k, v, beta, gc, t, u, w, du, dw, dqd, dkd, dintra, dgl)


@jax.custom_vjp
def _delta_rule_op(q, k, v, beta, gc):
    return _delta_rule_fwd(q, k, v, beta, gc)[0]


def _delta_rule_fwd(q, k, v, beta, gc):
    u, w, qd, kd, intra, t, cd = _dn_prep(q, k, v, beta, gc)
    out, vn, st = _dn_scan(u, w, qd, kd, intra, cd)
    return out, (q, k, v, beta, gc, u, w, qd, kd, intra, t, cd, vn, st)


def _delta_rule_bwd(res, do):
    q, k, v, beta, gc, u, w, qd, kd, intra, t, cd, vn, st = res
    du, dw, dqd, dkd, dintra, dgl = _dn_bwd_scan(do, w, qd, kd, intra, cd, vn, st)
    return _dn_bwd_chunks(q, k, v, beta, gc, t, u, w, du, dw, dqd, dkd, dintra, dgl)


_delta_rule_op.defvjp(_delta_rule_fwd, _delta_rule_bwd)


def _gated_delta_rule(q, k, v, g, beta):
    _, s, h, _ = q.shape
    n_chunks = s // CHUNK
    gc = jnp.cumsum(g[0].T.reshape(h, n_chunks, CHUNK), axis=-1)
    beta_c = beta[0].T.reshape(h, n_chunks, CHUNK)
    out = _delta_rule_op(q.reshape(s, h * DN_DK), k.reshape(s, h * DN_DK), v.reshape(s, h * DN_DV), beta_c, gc)
    return out.reshape(1, s, h, DN_DV)


def _project(h, h_lo, w, slot):
    b, s, d = h.shape
    return _linear(h.reshape(b * s, d), h_lo.reshape(b * s, d), w, slot).reshape(b, s, w.shape[1])


def _l2_normalize(x):
    return x * lax.rsqrt(jnp.sum(x * x, axis=-1, keepdims=True) + NORM_EPS)


def _rope_tables(positions, dh):
    inv_freq = ROPE_THETA ** (-jnp.arange(0, dh, 2, dtype=jnp.float32) / dh)
    ang = positions.astype(jnp.float32)[..., None] * inv_freq
    return jnp.cos(ang)[:, :, None, :], jnp.sin(ang)[:, :, None, :]


def _apply_rope(x, cos, sin):
    x1, x2 = jnp.split(x, 2, axis=-1)
    return jnp.concatenate([x1 * cos - x2 * sin, x2 * cos + x1 * sin], axis=-1)


def _causal_conv(x, w):
    s = x.shape[1]
    xp = jnp.pad(x, ((0, 0), (CONV_WIDTH - 1, 0), (0, 0)))
    return sum(w[j] * xp[:, j:j + s] for j in range(CONV_WIDTH))


_MASKED = -1e30


def _swa_probs(q_h, k_h, sink, valid):
    s = jnp.where(valid, _dot(q_h, k_h, 1, 1) * (SWA_DH ** -0.5), _MASKED)
    m = jnp.maximum(jnp.max(s, axis=-1, keepdims=True), sink)
    p = jnp.exp(s - m)
    e_sink = jnp.exp(sink - m)
    inv = 1.0 / (jnp.sum(p, axis=-1, keepdims=True) + e_sink)
    return p * inv, e_sink * inv


def _swa_valid(n):
    qi = lax.broadcasted_iota(jnp.int32, (WINDOW, 2 * WINDOW), 0)
    kj = lax.broadcasted_iota(jnp.int32, (WINDOW, 2 * WINDOW), 1)
    diff = qi + WINDOW - kj
    return (diff >= 0) & (diff < WINDOW) & ((kj >= WINDOW) | (n > 0))


def _swa_specs():
    qs = pl.BlockSpec((WINDOW, SWA_Q_W), lambda n: (n, 0))
    cur = pl.BlockSpec((WINDOW, SWA_KV_W), lambda n: (n, 0))
    prev = pl.BlockSpec((WINDOW, SWA_KV_W), lambda n: (jnp.maximum(n - 1, 0), 0))
    sk = pl.BlockSpec((SWA_HEADS, 1, 128), lambda n: (0, 0, 0))
    return qs, cur, prev, sk


def _swa_fwd_call(q, k, v, sinks):
    qs, cur, prev, sk = _swa_specs()

    def body(q_ref, kp_ref, kc_ref, vp_ref, vc_ref, sink_ref, o_ref):
        valid = _swa_valid(pl.program_id(0))
        kk = jnp.concatenate([kp_ref[...], kc_ref[...]], axis=0)
        vv = jnp.concatenate([vp_ref[...], vc_ref[...]], axis=0)
        for h in range(SWA_HEADS):
            kv_cols = slice((h // SWA_GROUP) * SWA_DH, (h // SWA_GROUP + 1) * SWA_DH)
            cols = pl.ds(h * SWA_DH, SWA_DH)
            probs, _ = _swa_probs(q_ref[:, cols], kk[:, kv_cols], sink_ref[h][:, :1], valid)
            o_ref[:, cols] = _dot(probs, vv[:, kv_cols])

    return pl.pallas_call(
        body, name="swa_fwd", grid=(q.shape[0] // WINDOW,),
        out_shape=jax.ShapeDtypeStruct(q.shape, jnp.float32),
        in_specs=[qs, prev, cur, prev, cur, sk], out_specs=qs,
        compiler_params=pltpu.CompilerParams(dimension_semantics=("parallel",)),
    )(q, k, k, v, v, sinks)


def _swa_bwd_call(q, k, v, sinks, do):
    qs, cur, prev, sk = _swa_specs()

    def body(q_ref, kp_ref, kc_ref, vp_ref, vc_ref, sink_ref, do_ref, dq_ref, dkc_ref, dkp_ref, dvc_ref, dvp_ref, ds_ref):
        @pl.when(pl.program_id(0) == 0)
        def _():
            ds_ref[...] = jnp.zeros_like(ds_ref)

        valid = _swa_valid(pl.program_id(0))
        kk = jnp.concatenate([kp_ref[...], kc_ref[...]], axis=0)
        vv = jnp.concatenate([vp_ref[...], vc_ref[...]], axis=0)
        lane0 = lax.broadcasted_iota(jnp.int32, (1, 128), 1) == 0
        dk_heads, dv_heads = [], []
        for hkv in range(SWA_KV_HEADS):
            kv_cols = slice(hkv * SWA_DH, (hkv + 1) * SWA_DH)
            k_h, v_h = kk[:, kv_cols], vv[:, kv_cols]
            dk_h = jnp.zeros((2 * WINDOW, SWA_DH), jnp.float32)
            dv_h = jnp.zeros((2 * WINDOW, SWA_DH), jnp.float32)
            for g in range(SWA_GROUP):
                h = hkv * SWA_GROUP + g
                cols = pl.ds(h * SWA_DH, SWA_DH)
                q_h, do_h = q_ref[:, cols], do_ref[:, cols]
                probs, p_sink = _swa_probs(q_h, k_h, sink_ref[h][:, :1], valid)
                dp = _dot(do_h, v_h, 1, 1)
                rs = jnp.sum(probs * dp, axis=-1, keepdims=True)
                d_s = probs * (dp - rs)
                dq_ref[:, cols] = _dot(d_s, k_h) * (SWA_DH ** -0.5)
                dk_h = dk_h + _dot(d_s, q_h, 0, 0) * (SWA_DH ** -0.5)
                dv_h = dv_h + _dot(probs, do_h, 0, 0)
                d_sink = -jnp.sum(p_sink * rs, axis=0, keepdims=True)
                ds_ref[h] += jnp.where(lane0, d_sink, 0.0)
            dk_heads.append(dk_h)
            dv_heads.append(dv_h)
        dk = jnp.concatenate(dk_heads, axis=1)
        dv = jnp.concatenate(dv_heads, axis=1)
        dkp_ref[...] = dk[:WINDOW]
        dkc_ref[...] = dk[WINDOW:]
        dvp_ref[...] = dv[:WINDOW]
        dvc_ref[...] = dv[WINDOW:]

    sd = jax.ShapeDtypeStruct
    f32 = jnp.float32
    return pl.pallas_call(
        body, name="swa_bwd", grid=(q.shape[0] // WINDOW,),
        out_shape=(sd(q.shape, f32), sd(k.shape, f32), sd(k.shape, f32), sd(k.shape, f32), sd(k.shape, f32),
                   sd(sinks.shape, f32)),
        in_specs=[qs, prev, cur, prev, cur, sk, qs], out_specs=(qs, cur, cur, cur, cur, sk),
        compiler_params=pltpu.CompilerParams(dimension_semantics=("arbitrary",)),
    )(q, k, k, v, v, sinks, do)


@jax.custom_vjp
def _swa_op(q, k, v, sinks):
    return _swa_fwd_call(q, k, v, sinks)


def _swa_op_fwd(q, k, v, sinks):
    return _swa_fwd_call(q, k, v, sinks), (q, k, v, sinks)


def _swa_op_bwd(res, do):
    q, k, v, sinks = res
    dq, dkc, dkp, dvc, dvp, dsinks = _swa_bwd_call(q, k, v, sinks, do)

    def fold(cur, prev):
        return cur + jnp.concatenate([prev[WINDOW:], jnp.zeros_like(prev[:WINDOW])], axis=0)

    return dq, fold(dkc, dkp), fold(dvc, dvp), dsinks


_swa_op.defvjp(_swa_op_fwd, _swa_op_bwd)


def _swa_sink_attention(q, k, v, sinks):
    s = q.shape[1]
    sinks_b = jnp.broadcast_to(sinks[:, None, None], (SWA_HEADS, 1, 128))
    return _swa_op(q.reshape(s, SWA_Q_W), k.reshape(s, SWA_KV_W), v.reshape(s, SWA_KV_W), sinks_b)[None]


MEM_ROWS = 512


def _mem_probs(q_h, k_h):
    s = _dot(q_h, k_h, 1, 1) * (MEM_DH ** -0.5)
    p = jnp.exp(s - jnp.max(s, axis=-1, keepdims=True))
    return p / jnp.sum(p, axis=-1, keepdims=True)


def _mem_fwd_call(qm, kv):
    qs = pl.BlockSpec((MEM_ROWS, MEM_W), lambda i: (i, 0))
    kvs = pl.BlockSpec(kv.shape, lambda i: (0, 0))

    def body(q_ref, kv_ref, o_ref):
        for h in range(MEM_HEADS):
            cols = pl.ds(h * MEM_DH, MEM_DH)
            probs = _mem_probs(q_ref[:, cols], kv_ref[:, cols])
            o_ref[:, cols] = _dot(probs, kv_ref[:, pl.ds(MEM_W + h * MEM_DH, MEM_DH)])

    return pl.pallas_call(
        body, name="mem_fwd", grid=(qm.shape[0] // MEM_ROWS,),
        out_shape=jax.ShapeDtypeStruct(qm.shape, jnp.float32), in_specs=[qs, kvs], out_specs=qs,
        compiler_params=pltpu.CompilerParams(dimension_semantics=("parallel",)),
    )(qm, kv)


def _mem_bwd_call(qm, kv, do):
    qs = pl.BlockSpec((MEM_ROWS, MEM_W), lambda i: (i, 0))
    kvs = pl.BlockSpec(kv.shape, lambda i: (0, 0))

    def body(q_ref, kv_ref, do_ref, dq_ref, dkv_ref):
        @pl.when(pl.program_id(0) == 0)
        def _():
            dkv_ref[...] = jnp.zeros_like(dkv_ref)

        for h in range(MEM_HEADS):
            cols = pl.ds(h * MEM_DH, MEM_DH)
            v_cols = pl.ds(MEM_W + h * MEM_DH, MEM_DH)
            q_h, k_h, do_h = q_ref[:, cols], kv_ref[:, cols], do_ref[:, cols]
            probs = _mem_probs(q_h, k_h)
            dp = _dot(do_h, kv_ref[:, v_cols], 1, 1)
            d_s = probs * (dp - jnp.sum(probs * dp, axis=-1, keepdims=True))
            dq_ref[:, cols] = _dot(d_s, k_h) * (MEM_DH ** -0.5)
            dkv_ref[:, cols] += _dot(d_s, q_h, 0, 0) * (MEM_DH ** -0.5)
            dkv_ref[:, v_cols] += _dot(probs, do_h, 0, 0)

    sd = jax.ShapeDtypeStruct
    return pl.pallas_call(
        body, name="mem_bwd", grid=(qm.shape[0] // MEM_ROWS,),
        out_shape=(sd(qm.shape, jnp.float32), sd(kv.shape, jnp.float32)),
        in_specs=[qs, kvs, qs], out_specs=(qs, kvs),
        compiler_params=pltpu.CompilerParams(dimension_semantics=("arbitrary",)),
    )(qm, kv, do)


@jax.custom_vjp
def _mem_op(qm, kv):
    return _mem_fwd_call(qm, kv)


def _mem_op_fwd(qm, kv):
    return _mem_fwd_call(qm, kv), (qm, kv)


def _mem_op_bwd(res, do):
    return _mem_bwd_call(*res, do)


_mem_op.defvjp(_mem_op_fwd, _mem_op_bwd)


def _memory_attention(qm, kv):
    return _mem_op(qm[0], kv[0])[None]


def _mixer_a(h, h_lo, mem, mem_lo, p, s, layer):
    B, S, _ = h.shape
    proj = _project(h, h_lo, p["a_w_in"][layer], s["a_w_in"][layer])
    c1 = 2 * DN_QK_W + DN_V_W
    qkv = proj[..., :c1]
    z = proj[..., c1:QKVZ_W]
    qm = proj[..., QKVZ_W:QKVZ_W + MEM_W]
    a = proj[..., QKVZ_W + MEM_W:QKVZ_W + MEM_W + DN_HEADS]
    b = proj[..., QKVZ_W + MEM_W + DN_HEADS:QKVZ_W + MEM_W + 2 * DN_HEADS]
    qkv = jax.nn.silu(_causal_conv(qkv, p["a_conv_w"][layer]))
    q = _l2_normalize(qkv[..., :DN_QK_W].reshape(B, S, DN_HEADS, DN_DK))
    k = _l2_normalize(qkv[..., DN_QK_W:2 * DN_QK_W].reshape(B, S, DN_HEADS, DN_DK))
    v = qkv[..., 2 * DN_QK_W:].reshape(B, S, DN_HEADS, DN_DV)
    beta = jax.nn.sigmoid(b)
    g = -jnp.exp(p["a_A_log"][layer]) * jax.nn.softplus(a + p["a_dt_bias"][layer])
    o = _gated_delta_rule(q, k, v, g, beta)
    o = o * lax.rsqrt(jnp.mean(o * o, axis=-1, keepdims=True) + NORM_EPS) * p["a_norm_w"][layer]
    o = o * jax.nn.silu(z.reshape(B, S, DN_HEADS, DN_DV))
    o = o.reshape(B, S, DN_V_W)
    kv = _project(mem, mem_lo, p["mem_w_kv"][layer], s["mem_w_kv"][layer])
    mo = _memory_attention(qm, kv)
    cat = jnp.concatenate([o, mo], axis=-1)
    return _project(cat, _lo(cat), p["w_o"][layer], s["w_o"][layer])


def _mixer_b(h, h_lo, mem, mem_lo, k_sh, v_sh, cos, sin, p, s, layer):
    B, S, _ = h.shape
    j = layer - N_A
    proj = _project(h, h_lo, p["b_w_in"][j], s["b_w_in"][j])
    q = _apply_rope(proj[..., :SWA_Q_W].reshape(B, S, SWA_HEADS, SWA_DH), cos, sin)
    q = q.reshape(B, S, SWA_KV_HEADS, SWA_GROUP, SWA_DH)
    o = _swa_sink_attention(q, k_sh, v_sh, p["b_sinks"][j])
    kv = _project(mem, mem_lo, p["mem_w_kv"][layer], s["mem_w_kv"][layer])
    mo = _memory_attention(proj[..., SWA_Q_W:], kv)
    cat = jnp.concatenate([o, mo], axis=-1)
    return _project(cat, _lo(cat), p["w_o"][layer], s["w_o"][layer])


def _forward(p, s, x, mem, positions):
    cos, sin = _rope_tables(positions, SWA_DH)
    h, h_lo, mem_lo = x, _lo(x), _lo(mem)
    k_sh = v_sh = None
    for layer in range(DEPTH):
        if layer < N_A:
            mix = _mixer_a(h, h_lo, mem, mem_lo, p, s, layer)
        else:
            mix = _mixer_b(h, h_lo, mem, mem_lo, k_sh, v_sh, cos, sin, p, s, layer)
        seq = h.shape[1]
        h2, h2_lo = _ln_res(h[0], mix[0], p["ln_g"][layer, 0][None], p["ln_b"][layer, 0][None])
        down = _mlp(h2, h2_lo, p["mlp_w_up"][layer], p["mlp_w_down"][layer], s["mlp_w_up"][layer],
                    s["mlp_w_down"][layer])
        h, h_lo = _ln_res(h2, down, p["ln_g"][layer, 1][None], p["ln_b"][layer, 1][None])
        h, h_lo = h.reshape(1, seq, D_MODEL), h_lo.reshape(1, seq, D_MODEL)
        if layer == N_A - 1:
            B, S, _ = h.shape
            kv = _project(h, h_lo, p["w_kv_shared"], s["w_kv_shared"])
            k_sh = _apply_rope(kv[..., :SWA_KV_W].reshape(B, S, SWA_KV_HEADS, SWA_DH), cos, sin)
            v_sh = kv[..., SWA_KV_W:].reshape(B, S, SWA_KV_HEADS, SWA_DH)
    return h


def _loss(diff, s, p, mem, positions, target):
    y = _forward({**p, **diff["small"]}, s, diff["x"], mem, positions)
    return 0.5 * jnp.sum(jnp.mean(jnp.square(y - target), axis=-1))


def _reorder_a_w_in(w):
    pad = jnp.zeros(w.shape[:-1] + (A_IN_PAD - A_IN,), w.dtype)
    return jnp.concatenate([w[..., :QKVZ_W], w[..., QKVZ_W + 2 * DN_HEADS:], w[..., QKVZ_W:QKVZ_W + 2 * DN_HEADS], pad],
                           axis=-1)


def _restore_a_w_in(w):
    return jnp.concatenate([w[..., :QKVZ_W], w[..., QKVZ_W + MEM_W:QKVZ_W + MEM_W + 2 * DN_HEADS],
                            w[..., QKVZ_W:QKVZ_W + MEM_W]], axis=-1)


def kernel(x, mem, positions, a_w_in, a_conv_w, a_A_log, a_dt_bias, a_norm_w, b_w_in, b_sinks, w_kv_shared, mem_w_kv, w_o, mlp_w_up, mlp_w_down, ln_g, ln_b, loss_target, m_a_w_in, m_a_conv_w, m_a_A_log, m_a_dt_bias, m_a_norm_w, m_b_w_in, m_b_sinks, m_w_kv_shared, m_mem_w_kv, m_w_o, m_mlp_w_up, m_mlp_w_down, m_ln_g, m_ln_b, v_a_w_in, v_a_conv_w, v_a_A_log, v_a_dt_bias, v_a_norm_w, v_b_w_in, v_b_sinks, v_w_kv_shared, v_mem_w_kv, v_w_o, v_mlp_w_up, v_mlp_w_down, v_ln_g, v_ln_b):
    w_sh = dict(a_w_in=a_w_in, a_conv_w=a_conv_w, a_A_log=a_A_log, a_dt_bias=a_dt_bias, a_norm_w=a_norm_w,
                b_w_in=b_w_in, b_sinks=b_sinks, w_kv_shared=w_kv_shared, mem_w_kv=mem_w_kv, w_o=w_o,
                mlp_w_up=mlp_w_up, mlp_w_down=mlp_w_down, ln_g=ln_g, ln_b=ln_b)
    m_sh = dict(a_w_in=m_a_w_in, a_conv_w=m_a_conv_w, a_A_log=m_a_A_log, a_dt_bias=m_a_dt_bias, a_norm_w=m_a_norm_w,
                b_w_in=m_b_w_in, b_sinks=m_b_sinks, w_kv_shared=m_w_kv_shared, mem_w_kv=m_mem_w_kv, w_o=m_w_o,
                mlp_w_up=m_mlp_w_up, mlp_w_down=m_mlp_w_down, ln_g=m_ln_g, ln_b=m_ln_b)
    v_sh = dict(a_w_in=v_a_w_in, a_conv_w=v_a_conv_w, a_A_log=v_a_A_log, a_dt_bias=v_a_dt_bias, a_norm_w=v_a_norm_w,
                b_w_in=v_b_w_in, b_sinks=v_b_sinks, w_kv_shared=v_w_kv_shared, mem_w_kv=v_mem_w_kv, w_o=v_w_o,
                mlp_w_up=v_mlp_w_up, mlp_w_down=v_mlp_w_down, ln_g=v_ln_g, ln_b=v_ln_b)
    shard_shapes = {n: w_sh[n].shape for n in WEIGHTS}
    rb, rows = _rows_for(w_sh)

    big, small = _pack(w_sh, rb, jnp.bfloat16)
    gbig, gsmall = _gather_weights(big, small)
    pieces = [_unpack(gbig[q], gsmall[q], shard_shapes) for q in range(N_CHIPS)]
    full = {n: jnp.concatenate([pieces[q][n] for q in range(N_CHIPS)], axis=SHARD_AXIS[n]) for n in SHARD_AXIS}
    for n in REPLICATED:
        full[n] = w_sh[n]
    big_w = {n: full[n] for n in BIG}
    big_w["a_w_in"] = _reorder_a_w_in(big_w["a_w_in"])
    small_w = {n: full[n] for n in SMALL}
    slots = {n: jnp.zeros(big_w[n].shape, jnp.float32) for n in BIG}

    loss, (grads, g_slots) = jax.value_and_grad(_loss, argnums=(0, 1))(
        {"x": x, "small": small_w}, slots, big_w, mem, positions, loss_target)
    loss = lax.psum(loss, ("x", "y", "c"))
    g_full = {**g_slots, **grads["small"]}
    g_full["a_w_in"] = _restore_a_w_in(g_full["a_w_in"])

    def shard_of(n, q):
        if n in REPLICATED:
            return g_full[n]
        size = shard_shapes[n][SHARD_AXIS[n]]
        return lax.slice_in_dim(g_full[n], q * size, (q + 1) * size, axis=SHARD_AXIS[n])

    parts = []
    for q in range(N_CHIPS):
        pb, ps = _pack({n: shard_of(n, q) for n in WEIGHTS}, rb, jnp.bfloat16)
        parts.append(jnp.concatenate([pb, ps.astype(jnp.bfloat16)], axis=0))
    received = _scatter_grads(jnp.stack(parts))
    mine = _sum_chips(received)
    other = _swap_with_sibling(mine)

    flat = [jnp.concatenate(_pack(d, rb), axis=0) for d in (w_sh, m_sh, v_sh)]
    outs = _adamw(mine, other, *flat)
    g_o, d_o, m_o, v_o = [_unpack(o[:rb], o[rb:], shard_shapes) for o in outs]
    return (loss, grads["x"], *[g_o[n] for n in WEIGHTS], *[d_o[n] for n in WEIGHTS],
            *[m_o[n] for n in WEIGHTS], *[v_o[n] for n in WEIGHTS])
```

```python
import functools
import math

import jax
import jax.numpy as jnp
from jax import lax
from jax.experimental import pallas as pl
from jax.experimental.pallas import tpu as pltpu

D_MODEL = 1024
DEPTH = 4
N_A = DEPTH // 2
N_B = DEPTH - N_A
MEM_HEADS = 4
MEM_DH = D_MODEL // 16
MEM_W = MEM_HEADS * MEM_DH
DN_DK = 128
DN_DV = 128
DN_HEADS = (3 * D_MODEL) // (4 * DN_DV)
DN_QK_W = DN_HEADS * DN_DK
DN_V_W = DN_HEADS * DN_DV
CONV_WIDTH = 4
CHUNK = 64
SWA_DH = 64
SWA_HEADS = (3 * D_MODEL) // (4 * SWA_DH)
SWA_KV_HEADS = 2
SWA_GROUP = SWA_HEADS // SWA_KV_HEADS
SWA_Q_W = SWA_HEADS * SWA_DH
SWA_KV_W = SWA_KV_HEADS * SWA_DH
WINDOW = 128
ROPE_THETA = 10000.0
MLP_HIDDEN = 4 * D_MODEL
LN_EPS = 1e-5
NORM_EPS = 1e-6
DN_ALPHA = (2.0 * DEPTH) ** 0.25
A_IN = 2 * DN_QK_W + 2 * DN_V_W + 2 * DN_HEADS + MEM_W
A_IN_PAD = 3456
QKVZ_W = 2 * DN_QK_W + 2 * DN_V_W

ADAM_LR = 0.001
ADAM_B1 = 0.9
ADAM_B2 = 0.999
ADAM_EPS = 1e-08
ADAM_WD = 0.01
ADAM_STEP = 10

N_CHIPS = 4
FLAT_W = 1024
BIG = ("a_w_in", "b_w_in", "w_kv_shared", "mem_w_kv", "w_o", "mlp_w_up", "mlp_w_down")
SMALL = ("a_conv_w", "ln_g", "ln_b", "a_A_log", "a_dt_bias", "a_norm_w", "b_sinks")
REPLICATED = ("a_A_log", "a_dt_bias", "a_norm_w", "b_sinks")
WEIGHTS = ("a_w_in", "a_conv_w", "a_A_log", "a_dt_bias", "a_norm_w", "b_w_in", "b_sinks", "w_kv_shared",
           "mem_w_kv", "w_o", "mlp_w_up", "mlp_w_down", "ln_g", "ln_b")
SHARD_AXIS = {"a_w_in": 2, "a_conv_w": 2, "b_w_in": 1, "w_kv_shared": 0, "mem_w_kv": 1, "w_o": 1,
              "mlp_w_up": 2, "mlp_w_down": 1, "ln_g": 2, "ln_b": 2}
SMALL_ROWS = 16
ROW_ALIGN = 256

MESH = pl.DeviceIdType.MESH
HBM_SPEC = pl.BlockSpec(memory_space=pltpu.HBM)
VMEM_LIMIT = 48 * 1024 * 1024


def _rows_for(shards):
    n_big = sum(math.prod(shards[n].shape) for n in BIG)
    n_small = sum(math.prod(shards[n].shape) for n in SMALL)
    assert n_small <= SMALL_ROWS * FLAT_W
    total = -(-n_big // FLAT_W) + SMALL_ROWS
    total = -(-total // ROW_ALIGN) * ROW_ALIGN
    return total - SMALL_ROWS, total


def _pack(shards, rb, dtype_big=jnp.float32):
    big = jnp.concatenate([shards[n].reshape(-1).astype(dtype_big) for n in BIG])
    big = jnp.pad(big, (0, rb * FLAT_W - big.shape[0])).reshape(rb, FLAT_W)
    small = jnp.concatenate([shards[n].reshape(-1).astype(jnp.float32) for n in SMALL])
    small = jnp.pad(small, (0, SMALL_ROWS * FLAT_W - small.shape[0])).reshape(SMALL_ROWS, FLAT_W)
    return big, small


def _unpack(big, small, shapes):
    out = {}
    for flat, names in ((big.reshape(-1), BIG), (small.reshape(-1), SMALL)):
        off = 0
        for n in names:
            size = math.prod(shapes[n])
            out[n] = flat[off:off + size].reshape(shapes[n])
            off += size
    return out


def _other_chips(x, y):
    return [(1 - x, y), (x, 1 - y), (1 - x, 1 - y)]


def _gather_weights(big, small):
    def body(big_ref, small_ref, obig_ref, osmall_ref, send_sems, recv_sems, local_sems):
        x, y, c = lax.axis_index("x"), lax.axis_index("y"), lax.axis_index("c")
        me = 2 * x + y
        pairs = ((big_ref, obig_ref), (small_ref, osmall_ref))
        local = [pltpu.make_async_copy(src, dst.at[me], local_sems.at[i]) for i, (src, dst) in enumerate(pairs)]
        for cp in local:
            cp.start()
        sends = []
        for j, (px, py) in enumerate(_other_chips(x, y)):
            for i, (src, dst) in enumerate(pairs):
                sends.append(pltpu.make_async_remote_copy(
                    src_ref=src, dst_ref=dst.at[me], send_sem=send_sems.at[2 * j + i], recv_sem=recv_sems.at[2 * j + i],
                    device_id=(px, py, c), device_id_type=MESH))
        for cp in sends:
            cp.start()
        for j, (px, py) in enumerate(_other_chips(x, y)):
            for i, (src, dst) in enumerate(pairs):
                pltpu.make_async_remote_copy(
                    src_ref=src, dst_ref=dst.at[2 * px + py], send_sem=send_sems.at[2 * j + i],
                    recv_sem=recv_sems.at[2 * j + i], device_id=(px, py, c), device_id_type=MESH).wait_recv()
        for cp in sends:
            cp.wait_send()
        for cp in local:
            cp.wait()

    return pl.pallas_call(
        body, name="gather_weights",
        out_shape=(jax.ShapeDtypeStruct((N_CHIPS,) + big.shape, big.dtype),
                   jax.ShapeDtypeStruct((N_CHIPS,) + small.shape, small.dtype)),
        in_specs=[HBM_SPEC, HBM_SPEC], out_specs=(HBM_SPEC, HBM_SPEC),
        scratch_shapes=[pltpu.SemaphoreType.DMA((6,)), pltpu.SemaphoreType.DMA((6,)), pltpu.SemaphoreType.DMA((2,))],
    )(big, small)


def _scatter_grads(g):
    def body(g_ref, o_ref, send_sems, recv_sems, local_sem):
        x, y, c = lax.axis_index("x"), lax.axis_index("y"), lax.axis_index("c")
        me = 2 * x + y
        local = pltpu.make_async_copy(g_ref.at[me], o_ref.at[me], local_sem)
        local.start()
        sends = []
        for j, (px, py) in enumerate(_other_chips(x, y)):
            sends.append(pltpu.make_async_remote_copy(
                src_ref=g_ref.at[2 * px + py], dst_ref=o_ref.at[me], send_sem=send_sems.at[j], recv_sem=recv_sems.at[j],
                device_id=(px, py, c), device_id_type=MESH))
        for cp in sends:
            cp.start()
        for j, (px, py) in enumerate(_other_chips(x, y)):
            pltpu.make_async_remote_copy(
                src_ref=g_ref.at[me], dst_ref=o_ref.at[2 * px + py], send_sem=send_sems.at[j], recv_sem=recv_sems.at[j],
                device_id=(px, py, c), device_id_type=MESH).wait_recv()
        for cp in sends:
            cp.wait_send()
        local.wait()

    return pl.pallas_call(
        body, name="scatter_grads",
        out_shape=jax.ShapeDtypeStruct(g.shape, g.dtype),
        in_specs=[HBM_SPEC], out_specs=HBM_SPEC,
        scratch_shapes=[pltpu.SemaphoreType.DMA((3,)), pltpu.SemaphoreType.DMA((3,)), pltpu.SemaphoreType.DMA],
    )(g)


def _swap_with_sibling(v):
    def body(v_ref, o_ref, send_sem, recv_sem):
        x, y, c = lax.axis_index("x"), lax.axis_index("y"), lax.axis_index("c")
        cp = pltpu.make_async_remote_copy(src_ref=v_ref, dst_ref=o_ref, send_sem=send_sem, recv_sem=recv_sem,
                                          device_id=(x, y, 1 - c), device_id_type=MESH)
        cp.start()
        cp.wait()

    return pl.pallas_call(
        body, name="swap_with_sibling",
        out_shape=jax.ShapeDtypeStruct(v.shape, v.dtype),
        in_specs=[HBM_SPEC], out_specs=HBM_SPEC,
        scratch_shapes=[pltpu.SemaphoreType.DMA, pltpu.SemaphoreType.DMA],
    )(v)


def _sum_chips(parts):
    n, rows, width = parts.shape

    def body(p_ref, o_ref):
        p = [p_ref[q].astype(jnp.float32) for q in range(n)]
        o_ref[...] = (p[0] + p[1]) + (p[2] + p[3])

    return pl.pallas_call(
        body, name="sum_chips", grid=(rows // ROW_ALIGN,),
        out_shape=jax.ShapeDtypeStruct((rows, width), jnp.float32),
        in_specs=[pl.BlockSpec((n, ROW_ALIGN, width), lambda i: (0, i, 0))],
        out_specs=pl.BlockSpec((ROW_ALIGN, width), lambda i: (i, 0)),
        compiler_params=pltpu.CompilerParams(dimension_semantics=("parallel",), vmem_limit_bytes=VMEM_LIMIT),
    )(parts)


def _adamw(mine, other, w, m, v):
    rows, width = w.shape
    blk = ROW_ALIGN // 2

    def body(a_ref, b_ref, w_ref, m_ref, v_ref, g_out, d_out, m_out, v_out):
        g = a_ref[...] + b_ref[...]
        m_new = ADAM_B1 * m_ref[...] + (1.0 - ADAM_B1) * g
        v_new = ADAM_B2 * v_ref[...] + (1.0 - ADAM_B2) * jnp.square(g)
        m_hat = m_new / (1.0 - ADAM_B1 ** ADAM_STEP)
        v_hat = v_new / (1.0 - ADAM_B2 ** ADAM_STEP)
        g_out[...] = g
        d_out[...] = -ADAM_LR * (m_hat / (jnp.sqrt(v_hat) + ADAM_EPS) + ADAM_WD * w_ref[...])
        m_out[...] = m_new
        v_out[...] = v_new

    spec = pl.BlockSpec((blk, width), lambda i: (i, 0))
    shape = jax.ShapeDtypeStruct((rows, width), jnp.float32)
    return pl.pallas_call(
        body, name="adamw", grid=(rows // blk,),
        out_shape=(shape,) * 4, in_specs=[spec] * 5, out_specs=(spec,) * 4,
        compiler_params=pltpu.CompilerParams(dimension_semantics=("parallel",), vmem_limit_bytes=VMEM_LIMIT),
    )(mine, other, w, m, v)


def _tile(dim, pref):
    if dim <= pref:
        return dim
    for t in range(pref - pref % 128, 0, -128):
        if dim % t == 0:
            return t
    raise ValueError(f"no 128-aligned tile for {dim}")


def _matmul(a, b, *, ta=False, tb=False, name, epilogue=None, extra=None, out_dtype=jnp.float32):
    (k_a, m) = a.shape if ta else a.shape[::-1]
    (k_b, n) = b.shape[::-1] if tb else b.shape
    assert k_a == k_b, (a.shape, b.shape, ta, tb)
    k = k_a
    tm, tn, tk = _tile(m, 1024), _tile(n, 512), _tile(k, 1024)
    nk = k // tk
    a_spec = pl.BlockSpec((tk, tm), lambda i, j, l: (l, i)) if ta else pl.BlockSpec((tm, tk), lambda i, j, l: (i, l))
    b_spec = pl.BlockSpec((tn, tk), lambda i, j, l: (j, l)) if tb else pl.BlockSpec((tk, tn), lambda i, j, l: (l, j))
    o_spec = pl.BlockSpec((tm, tn), lambda i, j, l: (i, j))
    dims = (((0 if ta else 1,), (1 if tb else 0,)), ((), ()))
    has_extra = epilogue == "relu2_grad"
    assert has_extra == (extra is not None)

    def body(*refs):
        a_ref, b_ref = refs[:2]
        outs = refs[2 + has_extra:2 + has_extra + (2 if epilogue == "relu2" else 1)]
        l = pl.program_id(2)
        part = lax.dot_general(a_ref[...].astype(jnp.bfloat16), b_ref[...].astype(jnp.bfloat16), dims,
                               preferred_element_type=jnp.float32)

        def finish(acc):
            if epilogue is None:
                outs[0][...] = acc.astype(out_dtype)
            elif epilogue == "relu2":
                outs[0][...] = acc.astype(jnp.bfloat16)
                outs[1][...] = jnp.square(jnp.maximum(acc, 0.0)).astype(jnp.bfloat16)
            else:
                outs[0][...] = (acc * (2.0 * jnp.maximum(refs[2][...].astype(jnp.float32), 0.0))).astype(out_dtype)

        if nk == 1:
            finish(part)
            return
        acc_ref = refs[-1]

        @pl.when(l == 0)
        def _():
            acc_ref[...] = part

        @pl.when((l > 0) & (l < nk - 1))
        def _():
            acc_ref[...] += part

        @pl.when(l == nk - 1)
        def _():
            finish(acc_ref[...] + part)

    if epilogue == "relu2":
        out_shape = (jax.ShapeDtypeStruct((m, n), jnp.bfloat16),) * 2
        out_specs = (o_spec, o_spec)
    else:
        out_shape = jax.ShapeDtypeStruct((m, n), out_dtype)
        out_specs = o_spec
    return pl.pallas_call(
        body, name=name, grid=(m // tm, n // tn, nk), out_shape=out_shape,
        in_specs=[a_spec, b_spec] + ([o_spec] if has_extra else []), out_specs=out_specs,
        scratch_shapes=[pltpu.VMEM((tm, tn), jnp.float32)] if nk > 1 else [],
        compiler_params=pltpu.CompilerParams(dimension_semantics=("parallel", "parallel", "arbitrary"),
                                             vmem_limit_bytes=VMEM_LIMIT),
    )(*((a, b) + ((extra,) if has_extra else ())))


def _lo(x):
    return lax.stop_gradient(x.astype(jnp.bfloat16))


@jax.custom_vjp
def _linear(x, x_lo, w, slot):
    del x, slot
    return _matmul(x_lo, w, name="linear_fwd")


def _linear_fwd(x, x_lo, w, slot):
    del x, slot
    return _matmul(x_lo, w, name="linear_fwd"), (x_lo, w)


def _linear_bwd(res, dy):
    x_lo, w = res
    dx = _matmul(dy, w, tb=True, name="linear_dx")
    dw = _matmul(x_lo, dy, ta=True, name="linear_dw")
    return dx, jnp.zeros_like(x_lo), jnp.zeros_like(w), dw


_linear.defvjp(_linear_fwd, _linear_bwd)


@jax.custom_vjp
def _mlp(h, h_lo, w_up, w_down, slot_up, slot_down):
    return _mlp_fwd(h, h_lo, w_up, w_down, slot_up, slot_down)[0]


def _mlp_fwd(h, h_lo, w_up, w_down, slot_up, slot_down):
    del h, slot_up, slot_down
    up, act = _matmul(h_lo, w_up, name="mlp_up", epilogue="relu2")
    return _matmul(act, w_down, name="mlp_down"), (h_lo, up, act, w_up, w_down)


def _mlp_bwd(res, dy):
    h_lo, up, act, w_up, w_down = res
    d_up = _matmul(dy, w_down, tb=True, name="mlp_d_up", epilogue="relu2_grad", extra=up, out_dtype=jnp.bfloat16)
    dw_down = _matmul(act, dy, ta=True, name="mlp_dw_down")
    dw_up = _matmul(h_lo, d_up, ta=True, name="mlp_dw_up")
    dh = _matmul(d_up, w_up, tb=True, name="mlp_dh")
    return dh, jnp.zeros_like(h_lo), jnp.zeros_like(w_up), jnp.zeros_like(w_down), dw_up, dw_down


_mlp.defvjp(_mlp_fwd, _mlp_bwd)


LN_ROWS = 256


def _ln_call(h, mix, g, b):
    s, d = h.shape
    tok = pl.BlockSpec((LN_ROWS, d), lambda i: (i, 0))
    vec = pl.BlockSpec((1, d), lambda i: (0, 0))
    stat = pl.BlockSpec((LN_ROWS, 1), lambda i: (i, 0))

    def body(h_ref, mix_ref, g_ref, b_ref, y_ref, ylo_ref, xhat_ref, rstd_ref):
        z = DN_ALPHA * h_ref[...] + mix_ref[...]
        mu = jnp.mean(z, axis=-1, keepdims=True)
        zc = z - mu
        rstd = lax.rsqrt(jnp.mean(jnp.square(zc), axis=-1, keepdims=True) + LN_EPS)
        xhat = zc * rstd
        y = xhat * g_ref[...] + b_ref[...]
        y_ref[...] = y
        ylo_ref[...] = y.astype(ylo_ref.dtype)
        xhat_ref[...] = xhat
        rstd_ref[...] = rstd

    sd = jax.ShapeDtypeStruct
    return pl.pallas_call(
        body, name="ln_fwd", grid=(s // LN_ROWS,),
        out_shape=(sd((s, d), jnp.float32), sd((s, d), jnp.bfloat16), sd((s, d), jnp.float32), sd((s, 1), jnp.float32)),
        in_specs=[tok, tok, vec, vec], out_specs=(tok, tok, tok, stat),
        compiler_params=pltpu.CompilerParams(dimension_semantics=("parallel",)),
    )(h, mix, g, b)


def _ln_grad_call(dy, xhat, rstd, g):
    s, d = dy.shape
    tok = pl.BlockSpec((LN_ROWS, d), lambda i: (i, 0))
    vec = pl.BlockSpec((1, d), lambda i: (0, 0))
    stat = pl.BlockSpec((LN_ROWS, 1), lambda i: (i, 0))

    def body(dy_ref, xhat_ref, rstd_ref, g_ref, dz_ref, dg_ref, db_ref):
        @pl.when(pl.program_id(0) == 0)
        def _():
            dg_ref[...] = jnp.zeros_like(dg_ref)
            db_ref[...] = jnp.zeros_like(db_ref)

        dy, xhat = dy_ref[...], xhat_ref[...]
        dyg = dy * g_ref[...]
        m1 = jnp.mean(dyg, axis=-1, keepdims=True)
        m2 = jnp.mean(dyg * xhat, axis=-1, keepdims=True)
        dz_ref[...] = rstd_ref[...] * (dyg - m1 - xhat * m2)
        dg_ref[...] += jnp.sum(dy * xhat, axis=0, keepdims=True)
        db_ref[...] += jnp.sum(dy, axis=0, keepdims=True)

    sd = jax.ShapeDtypeStruct
    return pl.pallas_call(
        body, name="ln_bwd", grid=(s // LN_ROWS,),
        out_shape=(sd((s, d), jnp.float32), sd((1, d), jnp.float32), sd((1, d), jnp.float32)),
        in_specs=[tok, tok, stat, vec], out_specs=(tok, vec, vec),
        compiler_params=pltpu.CompilerParams(dimension_semantics=("arbitrary",)),
    )(dy, xhat, rstd, g)


@jax.custom_vjp
def _ln_res(h, mix, g, b):
    return _ln_call(h, mix, g, b)[:2]


def _ln_res_fwd(h, mix, g, b):
    y, y_lo, xhat, rstd = _ln_call(h, mix, g, b)
    return (y, y_lo), (xhat, rstd, g)


def _ln_res_bwd(res, cts):
    xhat, rstd, g = res
    dz, dg, db = _ln_grad_call(cts[0], xhat, rstd, g)
    return DN_ALPHA * dz, dz, dg, db


_ln_res.defvjp(_ln_res_fwd, _ln_res_bwd)


MXU_DTYPE = jnp.bfloat16
DN_CB = 8
DN_GROUP = 4
DN_SCALE = DN_DK ** -0.5


def _dot(a, b, ca=1, cb=0):
    return lax.dot_general(a.astype(MXU_DTYPE), b.astype(MXU_DTYPE), (((ca,), (cb,)), ((), ())),
                           preferred_element_type=jnp.float32)


def _chunk_masks():
    row = lax.broadcasted_iota(jnp.int32, (CHUNK, CHUNK), 0)
    col = lax.broadcasted_iota(jnp.int32, (CHUNK, CHUNK), 1)
    return row >= col, row > col, row == col


def _to_col(row_vec):
    _, _, eye = _chunk_masks()
    return jnp.sum(jnp.where(eye, jnp.broadcast_to(row_vec, (CHUNK, CHUNK)), 0.0), axis=1, keepdims=True)


def _to_row(col_vec):
    _, _, eye = _chunk_masks()
    return jnp.sum(jnp.where(eye, jnp.broadcast_to(col_vec, (CHUNK, CHUNK)), 0.0), axis=0, keepdims=True)


def _last_row(col_vec):
    last = lax.broadcasted_iota(jnp.int32, (CHUNK, 1), 0) == CHUNK - 1
    return jnp.sum(jnp.where(last, col_vec, 0.0), axis=0, keepdims=True), last


def _chunk_terms(q, k, beta, gcc, gcr):
    incl, strict, _ = _chunk_masks()
    decay = jnp.where(incl, jnp.exp(jnp.minimum(gcc - gcr, 0.0)), 0.0)
    kb = k * beta
    lmat = jnp.where(strict, _dot(kb, k, 1, 1) * decay, 0.0)
    intra = jnp.where(incl, _dot(q, k, 1, 1) * decay, 0.0)
    return decay, kb, lmat, intra


def _dot3(a, b, ca=1, cb=0):
    if MXU_DTYPE == jnp.float32:
        return _dot(a, b, ca, cb)
    a_hi, b_hi = a.astype(MXU_DTYPE), b.astype(MXU_DTYPE)
    a_lo = (a - a_hi.astype(jnp.float32)).astype(MXU_DTYPE)
    b_lo = (b - b_hi.astype(jnp.float32)).astype(MXU_DTYPE)
    return _dot(a_hi, b_hi, ca, cb) + (_dot(a_hi, b_lo, ca, cb) + _dot(a_lo, b_hi, ca, cb))


def _unit_lower_inverse(lmats):
    _, _, eye = _chunk_masks()
    ident = jnp.where(eye, 1.0, 0.0)
    ts = [ident - m for m in lmats]
    ps = [_dot(m, m) for m in lmats]
    for _ in range(4):
        ts = [t + _dot(t, p) for t, p in zip(ts, ps)]
        ps = [_dot(p, p) for p in ps]
    ts = [t + _dot(t, p) for t, p in zip(ts, ps)]
    resids = [(t - ident) + _dot3(m, t) for m, t in zip(lmats, ts)]
    return [t - _dot(t, r) for t, r in zip(ts, resids)]


def _dn_specs(n_chunks):
    tok = pl.BlockSpec((DN_CB * CHUNK, DN_DK), lambda h, n: (n, h))
    rowv = pl.BlockSpec((None, DN_CB, CHUNK), lambda h, n: (h, n, 0))
    sq = pl.BlockSpec((None, DN_CB, CHUNK, CHUNK), lambda h, n: (h, n, 0, 0))
    lane = pl.BlockSpec((None, DN_CB, 1, DN_DV), lambda h, n: (h, n, 0, 0))
    return tok, rowv, sq, lane


def _dn_prep(q, k, v, beta, gc):
    s = q.shape[0]
    n_chunks = s // CHUNK
    tok, rowv, sq, lane = _dn_specs(n_chunks)

    def body(q_ref, k_ref, v_ref, beta_ref, gc_ref, u_ref, w_ref, qd_ref, kd_ref, intra_ref, t_ref, cd_ref):
        for c0 in range(0, DN_CB, DN_GROUP):
            chunks = range(c0, c0 + DN_GROUP)
            rhs, lmats = [], []
            for c in chunks:
                rows = pl.ds(c * CHUNK, CHUNK)
                q_c, k_c, v_c = q_ref[rows, :] * DN_SCALE, k_ref[rows, :], v_ref[rows, :]
                gcr_c = gc_ref[pl.ds(c, 1), :]
                beta_c, gcc_c = _to_col(beta_ref[pl.ds(c, 1), :]), _to_col(gcr_c)
                _, kb, lmat, intra = _chunk_terms(q_c, k_c, beta_c, gcc_c, gcr_c)
                eg = jnp.exp(gcc_c)
                g_last, _ = _last_row(gcc_c)
                qd_ref[rows, :] = (q_c * eg).astype(qd_ref.dtype)
                kd_ref[rows, :] = (k_c * jnp.exp(g_last - gcc_c)).astype(kd_ref.dtype)
                intra_ref[c] = intra.astype(intra_ref.dtype)
                cd_ref[c] = jnp.broadcast_to(jnp.exp(g_last), (1, DN_DV))
                rhs.append(jnp.concatenate([v_c * beta_c, kb * eg], axis=1))
                lmats.append(lmat)
            ts = _unit_lower_inverse(lmats)
            sols = [_dot3(t, r) for t, r in zip(ts, rhs)]
            for c, t, sol in zip(chunks, ts, sols):
                rows = pl.ds(c * CHUNK, CHUNK)
                t_ref[c] = t
                u_ref[rows, :] = sol[:, :DN_DV]
                w_ref[rows, :] = sol[:, DN_DV:].astype(w_ref.dtype)

    f32, mx = jnp.float32, MXU_DTYPE
    sd = jax.ShapeDtypeStruct
    return pl.pallas_call(
        body, name="dn_prep", grid=(DN_HEADS, n_chunks // DN_CB),
        out_shape=(sd(q.shape, f32), sd(q.shape, mx), sd(q.shape, mx), sd(q.shape, mx),
                   sd((DN_HEADS, n_chunks, CHUNK, CHUNK), mx), sd((DN_HEADS, n_chunks, CHUNK, CHUNK), f32),
                   sd((DN_HEADS, n_chunks, 1, DN_DV), f32)),
        in_specs=[tok, tok, tok, rowv, rowv], out_specs=(tok, tok, tok, tok, sq, sq, lane),
        compiler_params=pltpu.CompilerParams(dimension_semantics=("parallel", "parallel")),
    )(q, k, v, beta, gc)


def _dn_scan(u, w, qd, kd, intra, cd):
    s, width = u.shape
    n_chunks = s // CHUNK
    tok = pl.BlockSpec((CHUNK, width), lambda n: (n, 0))
    sq = pl.BlockSpec((DN_HEADS, None, CHUNK, CHUNK), lambda n: (0, n, 0, 0))
    lane = pl.BlockSpec((DN_HEADS, None, 1, DN_DV), lambda n: (0, n, 0, 0))
    st = pl.BlockSpec((DN_HEADS, None, DN_DK, DN_DV), lambda n: (0, n, 0, 0))

    def body(u_ref, w_ref, qd_ref, kd_ref, intra_ref, cd_ref, o_ref, vn_ref, st_ref, state):
        @pl.when(pl.program_id(0) == 0)
        def _():
            state[...] = jnp.zeros_like(state)

        heads = range(DN_HEADS)
        cols = [pl.ds(h * DN_DK, DN_DK) for h in heads]
        s_f = [state[h] for h in heads]
        s_mx = [s.astype(MXU_DTYPE) for s in s_f]
        for h in heads:
            st_ref[h] = s_mx[h]
        ws = [_dot(w_ref[:, cols[h]], s_mx[h]) for h in heads]
        qs = [_dot(qd_ref[:, cols[h]], s_mx[h]) for h in heads]
        v_new = [(u_ref[:, cols[h]] - ws[h]).astype(MXU_DTYPE) for h in heads]
        inner = [_dot(intra_ref[h], v_new[h]) for h in heads]
        outer = [_dot(kd_ref[:, cols[h]], v_new[h], 0, 0) for h in heads]
        for h in heads:
            vn_ref[:, cols[h]] = v_new[h]
            o_ref[:, cols[h]] = qs[h] + inner[h]
            state[h] = s_f[h] * cd_ref[h] + outer[h]

    sd = jax.ShapeDtypeStruct
    return pl.pallas_call(
        body, name="dn_scan", grid=(n_chunks,),
        out_shape=(sd(u.shape, jnp.float32), sd(u.shape, MXU_DTYPE),
                   sd((DN_HEADS, n_chunks, DN_DK, DN_DV), MXU_DTYPE)),
        in_specs=[tok, tok, tok, tok, sq, lane], out_specs=(tok, tok, st),
        scratch_shapes=[pltpu.VMEM((DN_HEADS, DN_DK, DN_DV), jnp.float32)],
        compiler_params=pltpu.CompilerParams(dimension_semantics=("arbitrary",)),
    )(u, w, qd, kd, intra, cd)


def _dn_bwd_scan(do, w, qd, kd, intra, cd, vn, st):
    s, width = do.shape
    n_chunks = s // CHUNK
    last = n_chunks - 1
    tok = pl.BlockSpec((CHUNK, width), lambda n: (last - n, 0))
    sq = pl.BlockSpec((DN_HEADS, None, CHUNK, CHUNK), lambda n: (0, last - n, 0, 0))
    lane = pl.BlockSpec((DN_HEADS, None, 1, DN_DV), lambda n: (0, last - n, 0, 0))
    stt = pl.BlockSpec((DN_HEADS, None, DN_DK, DN_DV), lambda n: (0, last - n, 0, 0))

    def body(do_ref, w_ref, qd_ref, kd_ref, intra_ref, cd_ref, vn_ref, st_ref,
             du_ref, dw_ref, dqd_ref, dkd_ref, dintra_ref, dgl_ref, dstate):
        @pl.when(pl.program_id(0) == 0)
        def _():
            dstate[...] = jnp.zeros_like(dstate)

        heads = range(DN_HEADS)
        cols = [pl.ds(h * DN_DK, DN_DK) for h in heads]
        ds_f = [dstate[h] for h in heads]
        ds_mx = [d.astype(MXU_DTYPE) for d in ds_f]
        do_h = [do_ref[:, cols[h]].astype(MXU_DTYPE) for h in heads]
        dv_a = [_dot(intra_ref[h], do_h[h], 0, 0) for h in heads]
        dv_b = [_dot(kd_ref[:, cols[h]], ds_mx[h]) for h in heads]
        d_intra = [_dot(do_h[h], vn_ref[:, cols[h]], 1, 1) for h in heads]
        d_qd = [_dot(do_h[h], st_ref[h], 1, 1) for h in heads]
        d_kd = [_dot(vn_ref[:, cols[h]], ds_mx[h], 1, 1) for h in heads]
        ds_q = [_dot(qd_ref[:, cols[h]], do_h[h], 0, 0) for h in heads]
        dv_new = [dv_a[h] + dv_b[h] for h in heads]
        dv_mx = [d.astype(MXU_DTYPE) for d in dv_new]
        d_w = [_dot(dv_mx[h], st_ref[h], 1, 1) for h in heads]
        ds_w = [_dot(w_ref[:, cols[h]], dv_mx[h], 0, 0) for h in heads]
        for h in heads:
            du_ref[:, cols[h]] = dv_new[h]
            dintra_ref[h] = d_intra[h]
            dqd_ref[:, cols[h]] = d_qd[h]
            dkd_ref[:, cols[h]] = d_kd[h]
            dw_ref[:, cols[h]] = -d_w[h]
            cd_h = cd_ref[h]
            dcd = jnp.sum(jnp.sum(st_ref[h].astype(jnp.float32) * ds_f[h], axis=1, keepdims=True), axis=0,
                          keepdims=True)
            dgl_ref[h] = dcd * cd_h
            dstate[h] = ds_q[h] + ds_f[h] * cd_h - ds_w[h]

    sd = jax.ShapeDtypeStruct
    f32 = jnp.float32
    return pl.pallas_call(
        body, name="dn_bwd_scan", grid=(n_chunks,),
        out_shape=(sd(do.shape, f32), sd(do.shape, f32), sd(do.shape, f32), sd(do.shape, f32),
                   sd((DN_HEADS, n_chunks, CHUNK, CHUNK), f32), sd((DN_HEADS, n_chunks, 1, DN_DV), f32)),
        in_specs=[tok, tok, tok, tok, sq, lane, tok, stt], out_specs=(tok, tok, tok, tok, sq, lane),
        scratch_shapes=[pltpu.VMEM((DN_HEADS, DN_DK, DN_DV), f32)],
        compiler_params=pltpu.CompilerParams(dimension_semantics=("arbitrary",)),
    )(do, w, qd, kd, intra, cd, vn, st)


def _dn_bwd_chunks(q, k, v, beta, gc, t, u, w, du, dw, dqd, dkd, dintra, dgl):
    s = q.shape[0]
    n_chunks = s // CHUNK
    tok, rowv, sq, lane = _dn_specs(n_chunks)

    def body(q_ref, k_ref, v_ref, beta_ref, gc_ref, t_ref, u_ref, w_ref, du_ref, dw_ref, dqd_ref, dkd_ref,
             dintra_ref, dgl_ref, dq_ref, dk_ref, dv_ref, dbeta_ref, dgc_ref):
        incl, strict, _ = _chunk_masks()

        def first(c):
            rows = pl.ds(c * CHUNK, CHUNK)
            q_c, k_c = q_ref[rows, :] * DN_SCALE, k_ref[rows, :]
            gcr_c = gc_ref[pl.ds(c, 1), :]
            beta_c, gcc_c = _to_col(beta_ref[pl.ds(c, 1), :]), _to_col(gcr_c)
            decay, kb, lmat, intra = _chunk_terms(q_c, k_c, beta_c, gcc_c, gcr_c)
            d_sol = jnp.concatenate([du_ref[rows, :], dw_ref[rows, :]], axis=1)
            d_rhs = _dot3(t_ref[c], d_sol, 0, 0)
            return dict(rows=rows, q=q_c, k=k_c, beta=beta_c, gcc=gcc_c, decay=decay, kb=kb, lmat=lmat, intra=intra,
                        d_rhs=d_rhs)

        def second(c, e):
            sol = jnp.concatenate([u_ref[e["rows"], :], w_ref[e["rows"], :].astype(jnp.float32)], axis=1)
            e["d_l"] = jnp.where(strict, -_dot(e["d_rhs"], sol, 1, 1), 0.0)
            e["d_intra"] = jnp.where(incl, dintra_ref[c], 0.0)
            d_qk = e["d_intra"] * e["decay"]
            e["dq"] = _dot(d_qk, e["k"])
            e["dk"] = _dot(d_qk, e["q"], 0, 0)

        def third(e):
            d_a = e["d_l"] * e["decay"]
            e["dkb"] = _dot(d_a, e["k"])
            e["dk"] = e["dk"] + _dot(d_a, e["kb"], 0, 0)

        def last(c, e):
            rows, q_c, k_c, beta_c, gcc_c = e["rows"], e["q"], e["k"], e["beta"], e["gcc"]
            v_c = v_ref[rows, :]
            eg = jnp.exp(gcc_c)
            g_last, is_last = _last_row(gcc_c)
            e_rev = jnp.exp(g_last - gcc_c)
            d_rhs_u, d_rhs_w = e["d_rhs"][:, :DN_DV], e["d_rhs"][:, DN_DV:]
            dv_ref[rows, :] = d_rhs_u * beta_c
            dbeta = jnp.sum(d_rhs_u * v_c, axis=1, keepdims=True)
            dkb = e["dkb"] + d_rhs_w * eg
            dgc = jnp.sum(d_rhs_w * e["kb"] * eg, axis=1, keepdims=True)
            m1 = e["d_l"] * e["lmat"]
            dgc = dgc + jnp.sum(m1, axis=1, keepdims=True)
            dgr = -jnp.sum(m1, axis=0, keepdims=True)
            m2 = e["d_intra"] * e["intra"]
            dgc = dgc + jnp.sum(m2, axis=1, keepdims=True)
            dgr = dgr - jnp.sum(m2, axis=0, keepdims=True)
            dqd = dqd_ref[rows, :]
            dq = e["dq"] + dqd * eg
            dgc = dgc + jnp.sum(dqd * q_c * eg, axis=1, keepdims=True)
            dkd = dkd_ref[rows, :]
            dk = e["dk"] + dkd * e_rev
            tk = jnp.sum(dkd * k_c * e_rev, axis=1, keepdims=True)
            dgc = dgc - tk
            d_last = dgl_ref[c][:, :1] + jnp.sum(tk, axis=0, keepdims=True)
            dgc = dgc + jnp.where(is_last, d_last, 0.0)
            dk = dk + dkb * beta_c
            dbeta = dbeta + jnp.sum(dkb * k_c, axis=1, keepdims=True)
            dq_ref[rows, :] = dq * DN_SCALE
            dk_ref[rows, :] = dk
            dbeta_ref[pl.ds(c, 1), :] = _to_row(dbeta)
            dgc_ref[pl.ds(c, 1), :] = _to_row(dgc) + dgr

        for c0 in range(0, DN_CB, DN_GROUP):
            chunks = range(c0, c0 + DN_GROUP)
            env = [first(c) for c in chunks]
            for c, e in zip(chunks, env):
                second(c, e)
            for e in env:
                third(e)
            for c, e in zip(chunks, env):
                last(c, e)

    sd = jax.ShapeDtypeStruct
    f32 = jnp.float32
    return pl.pallas_call(
        body, name="dn_bwd_chunks", grid=(DN_HEADS, n_chunks // DN_CB),
        out_shape=(sd(q.shape, f32), sd(q.shape, f32), sd(q.shape, f32), sd(beta.shape, f32), sd(gc.shape, f32)),
        in_specs=[tok, tok, tok, rowv, rowv, sq, tok, tok, tok, tok, tok, tok, sq, lane],
        out_specs=(tok, tok, tok, rowv, rowv),
        compiler_params=pltpu.CompilerParams(dimension_semantics=("parallel", "parallel")),
    )(q, k, v, beta, gc, t, u, w, du, dw, dqd, dkd, dintra, dgl)


@jax.custom_vjp
def _delta_rule_op(q, k, v, beta, gc):
    return _delta_rule_fwd(q, k, v, beta, gc)[0]


def _delta_rule_fwd(q, k, v, beta, gc):
    u, w, qd, kd, intra, t, cd = _dn_prep(q, k, v, beta, gc)
    out, vn, st = _dn_scan(u, w, qd, kd, intra, cd)
    return out, (q, k, v, beta, gc, u, w, qd, kd, intra, t, cd, vn, st)


def _delta_rule_bwd(res, do):
    q, k, v, beta, gc, u, w, qd, kd, intra, t, cd, vn, st = res
    du, dw, dqd, dkd, dintra, dgl = _dn_bwd_scan(do, w, qd, kd, intra, cd, vn, st)
    return _dn_bwd_chunks(q, k, v, beta, gc, t, u, w, du, dw, dqd, dkd, dintra, dgl)


_delta_rule_op.defvjp(_delta_rule_fwd, _delta_rule_bwd)


def _gated_delta_rule(q, k, v, g, beta):
    _, s, h, _ = q.shape
    n_chunks = s // CHUNK
    gc = jnp.cumsum(g[0].T.reshape(h, n_chunks, CHUNK), axis=-1)
    beta_c = beta[0].T.reshape(h, n_chunks, CHUNK)
    out = _delta_rule_op(q.reshape(s, h * DN_DK), k.reshape(s, h * DN_DK), v.reshape(s, h * DN_DV), beta_c, gc)
    return out.reshape(1, s, h, DN_DV)


def _project(h, h_lo, w, slot):
    b, s, d = h.shape
    return _linear(h.reshape(b * s, d), h_lo.reshape(b * s, d), w, slot).reshape(b, s, w.shape[1])


def _l2_normalize(x):
    return x * lax.rsqrt(jnp.sum(x * x, axis=-1, keepdims=True) + NORM_EPS)


def _rope_tables(positions, dh):
    inv_freq = ROPE_THETA ** (-jnp.arange(0, dh, 2, dtype=jnp.float32) / dh)
    ang = positions.astype(jnp.float32)[..., None] * inv_freq
    return jnp.cos(ang)[:, :, None, :], jnp.sin(ang)[:, :, None, :]


def _apply_rope(x, cos, sin):
    x1, x2 = jnp.split(x, 2, axis=-1)
    return jnp.concatenate([x1 * cos - x2 * sin, x2 * cos + x1 * sin], axis=-1)


def _causal_conv(x, w):
    s = x.shape[1]
    xp = jnp.pad(x, ((0, 0), (CONV_WIDTH - 1, 0), (0, 0)))
    return sum(w[j] * xp[:, j:j + s] for j in range(CONV_WIDTH))


_MASKED = -1e30


def _swa_probs(qs, k_h, sinks, valid):
    ss = [jnp.where(valid, _dot(q_h, k_h, 1, 1) * (SWA_DH ** -0.5), _MASKED) for q_h in qs]
    ms = [jnp.maximum(jnp.max(s, axis=-1, keepdims=True), sink) for s, sink in zip(ss, sinks)]
    ps = [jnp.exp(s - m) for s, m in zip(ss, ms)]
    es = [jnp.exp(sink - m) for sink, m in zip(sinks, ms)]
    invs = [1.0 / (jnp.sum(p, axis=-1, keepdims=True) + e) for p, e in zip(ps, es)]
    return [p * inv for p, inv in zip(ps, invs)], [e * inv for e, inv in zip(es, invs)]


def _swa_valid(n):
    qi = lax.broadcasted_iota(jnp.int32, (WINDOW, 2 * WINDOW), 0)
    kj = lax.broadcasted_iota(jnp.int32, (WINDOW, 2 * WINDOW), 1)
    diff = qi + WINDOW - kj
    return (diff >= 0) & (diff < WINDOW) & ((kj >= WINDOW) | (n > 0))


def _swa_specs():
    qs = pl.BlockSpec((WINDOW, SWA_Q_W), lambda n: (n, 0))
    cur = pl.BlockSpec((WINDOW, SWA_KV_W), lambda n: (n, 0))
    prev = pl.BlockSpec((WINDOW, SWA_KV_W), lambda n: (jnp.maximum(n - 1, 0), 0))
    sk = pl.BlockSpec((SWA_HEADS, 1, 128), lambda n: (0, 0, 0))
    return qs, cur, prev, sk


def _swa_fwd_call(q, k, v, sinks):
    qs, cur, prev, sk = _swa_specs()

    def body(q_ref, kp_ref, kc_ref, vp_ref, vc_ref, sink_ref, o_ref):
        valid = _swa_valid(pl.program_id(0))
        kk = jnp.concatenate([kp_ref[...], kc_ref[...]], axis=0)
        vv = jnp.concatenate([vp_ref[...], vc_ref[...]], axis=0)
        for hkv in range(SWA_KV_HEADS):
            kv_cols = slice(hkv * SWA_DH, (hkv + 1) * SWA_DH)
            heads = range(hkv * SWA_GROUP, (hkv + 1) * SWA_GROUP)
            probs, _ = _swa_probs([q_ref[:, pl.ds(h * SWA_DH, SWA_DH)] for h in heads], kk[:, kv_cols],
                                  [sink_ref[h][:, :1] for h in heads], valid)
            outs = [_dot(p, vv[:, kv_cols]) for p in probs]
            for h, o in zip(heads, outs):
                o_ref[:, pl.ds(h * SWA_DH, SWA_DH)] = o

    return pl.pallas_call(
        body, name="swa_fwd", grid=(q.shape[0] // WINDOW,),
        out_shape=jax.ShapeDtypeStruct(q.shape, jnp.float32),
        in_specs=[qs, prev, cur, prev, cur, sk], out_specs=qs,
        compiler_params=pltpu.CompilerParams(dimension_semantics=("parallel",)),
    )(q, k, k, v, v, sinks)


def _swa_bwd_call(q, k, v, sinks, do):
    qs, cur, prev, sk = _swa_specs()

    def body(q_ref, kp_ref, kc_ref, vp_ref, vc_ref, sink_ref, do_ref, dq_ref, dkc_ref, dkp_ref, dvc_ref, dvp_ref, ds_ref):
        @pl.when(pl.program_id(0) == 0)
        def _():
            ds_ref[...] = jnp.zeros_like(ds_ref)

        valid = _swa_valid(pl.program_id(0))
        kk = jnp.concatenate([kp_ref[...], kc_ref[...]], axis=0)
        vv = jnp.concatenate([vp_ref[...], vc_ref[...]], axis=0)
        lane0 = lax.broadcasted_iota(jnp.int32, (1, 128), 1) == 0
        dk_heads, dv_heads = [], []
        for hkv in range(SWA_KV_HEADS):
            kv_cols = slice(hkv * SWA_DH, (hkv + 1) * SWA_DH)
            k_h, v_h = kk[:, kv_cols], vv[:, kv_cols]
            heads = range(hkv * SWA_GROUP, (hkv + 1) * SWA_GROUP)
            qs = [q_ref[:, pl.ds(h * SWA_DH, SWA_DH)] for h in heads]
            dos = [do_ref[:, pl.ds(h * SWA_DH, SWA_DH)] for h in heads]
            probs, p_sinks = _swa_probs(qs, k_h, [sink_ref[h][:, :1] for h in heads], valid)
            dps = [_dot(do_h, v_h, 1, 1) for do_h in dos]
            rss = [jnp.sum(p * dp, axis=-1, keepdims=True) for p, dp in zip(probs, dps)]
            d_ss = [p * (dp - rs) for p, dp, rs in zip(probs, dps, rss)]
            dqs = [_dot(d_s, k_h) * (SWA_DH ** -0.5) for d_s in d_ss]
            dks = [_dot(d_s, q_h, 0, 0) for d_s, q_h in zip(d_ss, qs)]
            dvs = [_dot(p, do_h, 0, 0) for p, do_h in zip(probs, dos)]
            for h, dq, p_sink, rs in zip(heads, dqs, p_sinks, rss):
                dq_ref[:, pl.ds(h * SWA_DH, SWA_DH)] = dq
                d_sink = -jnp.sum(p_sink * rs, axis=0, keepdims=True)
                ds_ref[h] += jnp.where(lane0, d_sink, 0.0)
            dk_heads.append(sum(dks[1:], dks[0]) * (SWA_DH ** -0.5))
            dv_heads.append(sum(dvs[1:], dvs[0]))
        dk = jnp.concatenate(dk_heads, axis=1)
        dv = jnp.concatenate(dv_heads, axis=1)
        dkp_ref[...] = dk[:WINDOW]
        dkc_ref[...] = dk[WINDOW:]
        dvp_ref[...] = dv[:WINDOW]
        dvc_ref[...] = dv[WINDOW:]

    sd = jax.ShapeDtypeStruct
    f32 = jnp.float32
    return pl.pallas_call(
        body, name="swa_bwd", grid=(q.shape[0] // WINDOW,),
        out_shape=(sd(q.shape, f32), sd(k.shape, f32), sd(k.shape, f32), sd(k.shape, f32), sd(k.shape, f32),
                   sd(sinks.shape, f32)),
        in_specs=[qs, prev, cur, prev, cur, sk, qs], out_specs=(qs, cur, cur, cur, cur, sk),
        compiler_params=pltpu.CompilerParams(dimension_semantics=("arbitrary",)),
    )(q, k, k, v, v, sinks, do)


@jax.custom_vjp
def _swa_op(q, k, v, sinks):
    return _swa_fwd_call(q, k, v, sinks)


def _swa_op_fwd(q, k, v, sinks):
    return _swa_fwd_call(q, k, v, sinks), (q, k, v, sinks)


def _swa_op_bwd(res, do):
    q, k, v, sinks = res
    dq, dkc, dkp, dvc, dvp, dsinks = _swa_bwd_call(q, k, v, sinks, do)

    def fold(cur, prev):
        return cur + jnp.concatenate([prev[WINDOW:], jnp.zeros_like(prev[:WINDOW])], axis=0)

    return dq, fold(dkc, dkp), fold(dvc, dvp), dsinks


_swa_op.defvjp(_swa_op_fwd, _swa_op_bwd)


def _swa_sink_attention(q, k, v, sinks):
    s = q.shape[1]
    sinks_b = jnp.broadcast_to(sinks[:, None, None], (SWA_HEADS, 1, 128))
    return _swa_op(q.reshape(s, SWA_Q_W), k.reshape(s, SWA_KV_W), v.reshape(s, SWA_KV_W), sinks_b)[None]


MEM_ROWS = 512


def _mem_probs(q_h, k_h):
    s = _dot(q_h, k_h, 1, 1) * (MEM_DH ** -0.5)
    p = jnp.exp(s - jnp.max(s, axis=-1, keepdims=True))
    return p / jnp.sum(p, axis=-1, keepdims=True)


def _mem_fwd_call(qm, kv):
    qs = pl.BlockSpec((MEM_ROWS, MEM_W), lambda i: (i, 0))
    kvs = pl.BlockSpec(kv.shape, lambda i: (0, 0))

    def body(q_ref, kv_ref, o_ref):
        for h in range(MEM_HEADS):
            cols = pl.ds(h * MEM_DH, MEM_DH)
            probs = _mem_probs(q_ref[:, cols], kv_ref[:, cols])
            o_ref[:, cols] = _dot(probs, kv_ref[:, pl.ds(MEM_W + h * MEM_DH, MEM_DH)])

    return pl.pallas_call(
        body, name="mem_fwd", grid=(qm.shape[0] // MEM_ROWS,),
        out_shape=jax.ShapeDtypeStruct(qm.shape, jnp.float32), in_specs=[qs, kvs], out_specs=qs,
        compiler_params=pltpu.CompilerParams(dimension_semantics=("parallel",)),
    )(qm, kv)


def _mem_bwd_call(qm, kv, do):
    qs = pl.BlockSpec((MEM_ROWS, MEM_W), lambda i: (i, 0))
    kvs = pl.BlockSpec(kv.shape, lambda i: (0, 0))

    def body(q_ref, kv_ref, do_ref, dq_ref, dkv_ref):
        @pl.when(pl.program_id(0) == 0)
        def _():
            dkv_ref[...] = jnp.zeros_like(dkv_ref)

        for h in range(MEM_HEADS):
            cols = pl.ds(h * MEM_DH, MEM_DH)
            v_cols = pl.ds(MEM_W + h * MEM_DH, MEM_DH)
            q_h, k_h, do_h = q_ref[:, cols], kv_ref[:, cols], do_ref[:, cols]
            probs = _mem_probs(q_h, k_h)
            dp = _dot(do_h, kv_ref[:, v_cols], 1, 1)
            d_s = probs * (dp - jnp.sum(probs * dp, axis=-1, keepdims=True))
            dq_ref[:, cols] = _dot(d_s, k_h) * (MEM_DH ** -0.5)
            dkv_ref[:, cols] += _dot(d_s, q_h, 0, 0) * (MEM_DH ** -0.5)
            dkv_ref[:, v_cols] += _dot(probs, do_h, 0, 0)

    sd = jax.ShapeDtypeStruct
    return pl.pallas_call(
        body, name="mem_bwd", grid=(qm.shape[0] // MEM_ROWS,),
        out_shape=(sd(qm.shape, jnp.float32), sd(kv.shape, jnp.float32)),
        in_specs=[qs, kvs, qs], out_specs=(qs, kvs),
        compiler_params=pltpu.CompilerParams(dimension_semantics=("arbitrary",)),
    )(qm, kv, do)


@jax.custom_vjp
def _mem_op(qm, kv):
    return _mem_fwd_call(qm, kv)


def _mem_op_fwd(qm, kv):
    return _mem_fwd_call(qm, kv), (qm, kv)


def _mem_op_bwd(res, do):
    return _mem_bwd_call(*res, do)


_mem_op.defvjp(_mem_op_fwd, _mem_op_bwd)


def _memory_attention(qm, kv):
    return _mem_op(qm[0], kv[0])[None]


def _mixer_a(h, h_lo, mem, mem_lo, p, s, layer):
    B, S, _ = h.shape
    proj = _project(h, h_lo, p["a_w_in"][layer], s["a_w_in"][layer])
    c1 = 2 * DN_QK_W + DN_V_W
    qkv = proj[..., :c1]
    z = proj[..., c1:QKVZ_W]
    qm = proj[..., QKVZ_W:QKVZ_W + MEM_W]
    a = proj[..., QKVZ_W + MEM_W:QKVZ_W + MEM_W + DN_HEADS]
    b = proj[..., QKVZ_W + MEM_W + DN_HEADS:QKVZ_W + MEM_W + 2 * DN_HEADS]
    qkv = jax.nn.silu(_causal_conv(qkv, p["a_conv_w"][layer]))
    q = _l2_normalize(qkv[..., :DN_QK_W].reshape(B, S, DN_HEADS, DN_DK))
    k = _l2_normalize(qkv[..., DN_QK_W:2 * DN_QK_W].reshape(B, S, DN_HEADS, DN_DK))
    v = qkv[..., 2 * DN_QK_W:].reshape(B, S, DN_HEADS, DN_DV)
    beta = jax.nn.sigmoid(b)
    g = -jnp.exp(p["a_A_log"][layer]) * jax.nn.softplus(a + p["a_dt_bias"][layer])
    o = _gated_delta_rule(q, k, v, g, beta)
    o = o * lax.rsqrt(jnp.mean(o * o, axis=-1, keepdims=True) + NORM_EPS) * p["a_norm_w"][layer]
    o = o * jax.nn.silu(z.reshape(B, S, DN_HEADS, DN_DV))
    o = o.reshape(B, S, DN_V_W)
    kv = _project(mem, mem_lo, p["mem_w_kv"][layer], s["mem_w_kv"][layer])
    mo = _memory_attention(qm, kv)
    cat = jnp.concatenate([o, mo], axis=-1)
    return _project(cat, _lo(cat), p["w_o"][layer], s["w_o"][layer])


def _mixer_b(h, h_lo, mem, mem_lo, k_sh, v_sh, cos, sin, p, s, layer):
    B, S, _ = h.shape
    j = layer - N_A
    proj = _project(h, h_lo, p["b_w_in"][j], s["b_w_in"][j])
    q = _apply_rope(proj[..., :SWA_Q_W].reshape(B, S, SWA_HEADS, SWA_DH), cos, sin)
    q = q.reshape(B, S, SWA_KV_HEADS, SWA_GROUP, SWA_DH)
    o = _swa_sink_attention(q, k_sh, v_sh, p["b_sinks"][j])
    kv = _project(mem, mem_lo, p["mem_w_kv"][layer], s["mem_w_kv"][layer])
    mo = _memory_attention(proj[..., SWA_Q_W:], kv)
    cat = jnp.concatenate([o, mo], axis=-1)
    return _project(cat, _lo(cat), p["w_o"][layer], s["w_o"][layer])


def _forward(p, s, x, mem, positions):
    cos, sin = _rope_tables(positions, SWA_DH)
    h, h_lo, mem_lo = x, _lo(x), _lo(mem)
    k_sh = v_sh = None
    for layer in range(DEPTH):
        if layer < N_A:
            mix = _mixer_a(h, h_lo, mem, mem_lo, p, s, layer)
        else:
            mix = _mixer_b(h, h_lo, mem, mem_lo, k_sh, v_sh, cos, sin, p, s, layer)
        seq = h.shape[1]
        h2, h2_lo = _ln_res(h[0], mix[0], p["ln_g"][layer, 0][None], p["ln_b"][layer, 0][None])
        down = _mlp(h2, h2_lo, p["mlp_w_up"][layer], p["mlp_w_down"][layer], s["mlp_w_up"][layer],
                    s["mlp_w_down"][layer])
        h, h_lo = _ln_res(h2, down, p["ln_g"][layer, 1][None], p["ln_b"][layer, 1][None])
        h, h_lo = h.reshape(1, seq, D_MODEL), h_lo.reshape(1, seq, D_MODEL)
        if layer == N_A - 1:
            B, S, _ = h.shape
            kv = _project(h, h_lo, p["w_kv_shared"], s["w_kv_shared"])
            k_sh = _apply_rope(kv[..., :SWA_KV_W].reshape(B, S, SWA_KV_HEADS, SWA_DH), cos, sin)
            v_sh = kv[..., SWA_KV_W:].reshape(B, S, SWA_KV_HEADS, SWA_DH)
    return h


def _loss(diff, s, p, mem, positions, target):
    y = _forward({**p, **diff["small"]}, s, diff["x"], mem, positions)
    return 0.5 * jnp.sum(jnp.mean(jnp.square(y - target), axis=-1))


def _reorder_a_w_in(w):
    pad = jnp.zeros(w.shape[:-1] + (A_IN_PAD - A_IN,), w.dtype)
    return jnp.concatenate([w[..., :QKVZ_W], w[..., QKVZ_W + 2 * DN_HEADS:], w[..., QKVZ_W:QKVZ_W + 2 * DN_HEADS], pad],
                           axis=-1)


def _restore_a_w_in(w):
    return jnp.concatenate([w[..., :QKVZ_W], w[..., QKVZ_W + MEM_W:QKVZ_W + MEM_W + 2 * DN_HEADS],
                            w[..., QKVZ_W:QKVZ_W + MEM_W]], axis=-1)


def kernel(x, mem, positions, a_w_in, a_conv_w, a_A_log, a_dt_bias, a_norm_w, b_w_in, b_sinks, w_kv_shared, mem_w_kv, w_o, mlp_w_up, mlp_w_down, ln_g, ln_b, loss_target, m_a_w_in, m_a_conv_w, m_a_A_log, m_a_dt_bias, m_a_norm_w, m_b_w_in, m_b_sinks, m_w_kv_shared, m_mem_w_kv, m_w_o, m_mlp_w_up, m_mlp_w_down, m_ln_g, m_ln_b, v_a_w_in, v_a_conv_w, v_a_A_log, v_a_dt_bias, v_a_norm_w, v_b_w_in, v_b_sinks, v_w_kv_shared, v_mem_w_kv, v_w_o, v_mlp_w_up, v_mlp_w_down, v_ln_g, v_ln_b):
    w_sh = dict(a_w_in=a_w_in, a_conv_w=a_conv_w, a_A_log=a_A_log, a_dt_bias=a_dt_bias, a_norm_w=a_norm_w,
                b_w_in=b_w_in, b_sinks=b_sinks, w_kv_shared=w_kv_shared, mem_w_kv=mem_w_kv, w_o=w_o,
                mlp_w_up=mlp_w_up, mlp_w_down=mlp_w_down, ln_g=ln_g, ln_b=ln_b)
    m_sh = dict(a_w_in=m_a_w_in, a_conv_w=m_a_conv_w, a_A_log=m_a_A_log, a_dt_bias=m_a_dt_bias, a_norm_w=m_a_norm_w,
                b_w_in=m_b_w_in, b_sinks=m_b_sinks, w_kv_shared=m_w_kv_shared, mem_w_kv=m_mem_w_kv, w_o=m_w_o,
                mlp_w_up=m_mlp_w_up, mlp_w_down=m_mlp_w_down, ln_g=m_ln_g, ln_b=m_ln_b)
    v_sh = dict(a_w_in=v_a_w_in, a_conv_w=v_a_conv_w, a_A_log=v_a_A_log, a_dt_bias=v_a_dt_bias, a_norm_w=v_a_norm_w,
                b_w_in=v_b_w_in, b_sinks=v_b_sinks, w_kv_shared=v_w_kv_shared, mem_w_kv=v_mem_w_kv, w_o=v_w_o,
                mlp_w_up=v_mlp_w_up, mlp_w_down=v_mlp_w_down, ln_g=v_ln_g, ln_b=v_ln_b)
    shard_shapes = {n: w_sh[n].shape for n in WEIGHTS}
    rb, rows = _rows_for(w_sh)

    big, small = _pack(w_sh, rb, jnp.bfloat16)
    gbig, gsmall = _gather_weights(big, small)
    pieces = [_unpack(gbig[q], gsmall[q], shard_shapes) for q in range(N_CHIPS)]
    full = {n: jnp.concatenate([pieces[q][n] for q in range(N_CHIPS)], axis=SHARD_AXIS[n]) for n in SHARD_AXIS}
    for n in REPLICATED:
        full[n] = w_sh[n]
    big_w = {n: full[n] for n in BIG}
    big_w["a_w_in"] = _reorder_a_w_in(big_w["a_w_in"])
    small_w = {n: full[n] for n in SMALL}
    slots = {n: jnp.zeros(big_w[n].shape, jnp.float32) for n in BIG}

    loss, (grads, g_slots) = jax.value_and_grad(_loss, argnums=(0, 1))(
        {"x": x, "small": small_w}, slots, big_w, mem, positions, loss_target)
    loss = lax.psum(loss, ("x", "y", "c"))
    g_full = {**g_slots, **grads["small"]}
    g_full["a_w_in"] = _restore_a_w_in(g_full["a_w_in"])

    def shard_of(n, q):
        if n in REPLICATED:
            return g_full[n]
        size = shard_shapes[n][SHARD_AXIS[n]]
        return lax.slice_in_dim(g_full[n], q * size, (q + 1) * size, axis=SHARD_AXIS[n])

    parts = []
    for q in range(N_CHIPS):
        pb, ps = _pack({n: shard_of(n, q) for n in WEIGHTS}, rb, jnp.bfloat16)
        parts.append(jnp.concatenate([pb, ps.astype(jnp.bfloat16)], axis=0))
    received = _scatter_grads(jnp.stack(parts))
    mine = _sum_chips(received)
    other = _swap_with_sibling(mine)

    flat = [jnp.concatenate(_pack(d, rb), axis=0) for d in (w_sh, m_sh, v_sh)]
    outs = _adamw(mine, other, *flat)
    g_o, d_o, m_o, v_o = [_unpack(o[:rb], o[rb:], shard_shapes) for o in outs]
    return (loss, grads["x"], *[g_o[n] for n in WEIGHTS], *[d_o[n] for n in WEIGHTS],
            *[m_o[n] for n in WEIGHTS], *[v_o[n] for n in WEIGHTS])
```

```python
import functools
import math

import jax
import jax.numpy as jnp
from jax import lax
from jax.experimental import pallas as pl
from jax.experimental.pallas import tpu as pltpu

D_MODEL = 1024
DEPTH = 4
N_A = DEPTH // 2
N_B = DEPTH - N_A
MEM_HEADS = 4
MEM_DH = D_MODEL // 16
MEM_W = MEM_HEADS * MEM_DH
DN_DK = 128
DN_DV = 128
DN_HEADS = (3 * D_MODEL) // (4 * DN_DV)
DN_QK_W = DN_HEADS * DN_DK
DN_V_W = DN_HEADS * DN_DV
CONV_WIDTH = 4
CHUNK = 64
SWA_DH = 64
SWA_HEADS = (3 * D_MODEL) // (4 * SWA_DH)
SWA_KV_HEADS = 2
SWA_GROUP = SWA_HEADS // SWA_KV_HEADS
SWA_Q_W = SWA_HEADS * SWA_DH
SWA_KV_W = SWA_KV_HEADS * SWA_DH
WINDOW = 128
ROPE_THETA = 10000.0
MLP_HIDDEN = 4 * D_MODEL
LN_EPS = 1e-5
NORM_EPS = 1e-6
DN_ALPHA = (2.0 * DEPTH) ** 0.25
A_IN = 2 * DN_QK_W + 2 * DN_V_W + 2 * DN_HEADS + MEM_W
A_IN_PAD = 3456
QKVZ_W = 2 * DN_QK_W + 2 * DN_V_W

ADAM_LR = 0.001
ADAM_B1 = 0.9
ADAM_B2 = 0.999
ADAM_EPS = 1e-08
ADAM_WD = 0.01
ADAM_STEP = 10

N_CHIPS = 4
FLAT_W = 1024
BIG = ("a_w_in", "b_w_in", "w_kv_shared", "mem_w_kv", "w_o", "mlp_w_up", "mlp_w_down")
SMALL = ("a_conv_w", "ln_g", "ln_b", "a_A_log", "a_dt_bias", "a_norm_w", "b_sinks")
REPLICATED = ("a_A_log", "a_dt_bias", "a_norm_w", "b_sinks")
WEIGHTS = ("a_w_in", "a_conv_w", "a_A_log", "a_dt_bias", "a_norm_w", "b_w_in", "b_sinks", "w_kv_shared",
           "mem_w_kv", "w_o", "mlp_w_up", "mlp_w_down", "ln_g", "ln_b")
SHARD_AXIS = {"a_w_in": 2, "a_conv_w": 2, "b_w_in": 1, "w_kv_shared": 0, "mem_w_kv": 1, "w_o": 1,
              "mlp_w_up": 2, "mlp_w_down": 1, "ln_g": 2, "ln_b": 2}
SMALL_ROWS = 32
ROW_ALIGN = 256

MESH = pl.DeviceIdType.MESH
HBM_SPEC = pl.BlockSpec(memory_space=pltpu.HBM)
VMEM_LIMIT = 48 * 1024 * 1024


def _rows_for(shards):
    n_big = sum(math.prod(shards[n].shape) for n in BIG)
    n_small = sum(math.prod(shards[n].shape) for n in SMALL)
    assert n_small <= SMALL_ROWS * FLAT_W
    total = -(-n_big // FLAT_W) + SMALL_ROWS
    total = -(-total // (2 * ROW_ALIGN)) * (2 * ROW_ALIGN)
    return total - SMALL_ROWS, total


def _pack(shards, rb, dtype_big=jnp.float32):
    big = jnp.concatenate([shards[n].reshape(-1).astype(dtype_big) for n in BIG])
    big = jnp.pad(big, (0, rb * FLAT_W - big.shape[0])).reshape(rb, FLAT_W)
    small = jnp.concatenate([shards[n].reshape(-1).astype(jnp.float32) for n in SMALL])
    small = jnp.pad(small, (0, SMALL_ROWS * FLAT_W - small.shape[0])).reshape(SMALL_ROWS, FLAT_W)
    return big, small


def _unpack(big, small, shapes):
    out = {}
    for flat, names in ((big.reshape(-1), BIG), (small.reshape(-1), SMALL)):
        off = 0
        for n in names:
            size = math.prod(shapes[n])
            out[n] = flat[off:off + size].reshape(shapes[n])
            off += size
    return out


def _other_chips(x, y):
    return [(1 - x, y), (x, 1 - y), (1 - x, 1 - y)]


def _gather_weights(big, small):
    def body(big_ref, small_ref, obig_ref, osmall_ref, send_sems, recv_sems, pass_send_sems, pass_recv_sems, local_sems):
        x, y, c = lax.axis_index("x"), lax.axis_index("y"), lax.axis_index("c")
        me = 2 * x + y
        sibling = (x, y, 1 - c)
        pairs = ((big_ref, obig_ref), (small_ref, osmall_ref))
        local = [pltpu.make_async_copy(src, dst.at[me], local_sems.at[i]) for i, (src, dst) in enumerate(pairs)]
        for cp in local:
            cp.start()
        sends = []
        for j, (px, py) in enumerate(_other_chips(x, y)):
            for i, (src, dst) in enumerate(pairs):
                sends.append(pltpu.make_async_remote_copy(
                    src_ref=src.at[c], dst_ref=dst.at[me, c], send_sem=send_sems.at[2 * j + i],
                    recv_sem=recv_sems.at[2 * j + i], device_id=(px, py, c), device_id_type=MESH))
        for cp in sends:
            cp.start()
        passed = []
        for j, (px, py) in enumerate(_other_chips(x, y)):
            for i, (src, dst) in enumerate(pairs):
                landed = dst.at[2 * px + py, c]
                pltpu.make_async_remote_copy(
                    src_ref=src.at[c], dst_ref=landed, send_sem=send_sems.at[2 * j + i],
                    recv_sem=recv_sems.at[2 * j + i], device_id=(px, py, c), device_id_type=MESH).wait_recv()
                passed.append(pltpu.make_async_remote_copy(
                    src_ref=landed, dst_ref=landed, send_sem=pass_send_sems.at[2 * j + i],
                    recv_sem=pass_recv_sems.at[2 * j + i], device_id=sibling, device_id_type=MESH))
                passed[-1].start()
        for j, (px, py) in enumerate(_other_chips(x, y)):
            for i, (src, dst) in enumerate(pairs):
                other_half = dst.at[2 * px + py, 1 - c]
                pltpu.make_async_remote_copy(
                    src_ref=other_half, dst_ref=other_half, send_sem=pass_send_sems.at[2 * j + i],
                    recv_sem=pass_recv_sems.at[2 * j + i], device_id=sibling, device_id_type=MESH).wait_recv()
        for cp in sends + passed:
            cp.wait_send()
        for cp in local:
            cp.wait()

    dma6 = pltpu.SemaphoreType.DMA((6,))
    return pl.pallas_call(
        body, name="gather_weights",
        out_shape=(jax.ShapeDtypeStruct((N_CHIPS,) + big.shape, big.dtype),
                   jax.ShapeDtypeStruct((N_CHIPS,) + small.shape, small.dtype)),
        in_specs=[HBM_SPEC, HBM_SPEC], out_specs=(HBM_SPEC, HBM_SPEC),
        scratch_shapes=[dma6, dma6, dma6, dma6, pltpu.SemaphoreType.DMA((2,))],
    )(big, small)


def _scatter_grads(g):
    def body(g_ref, o_ref, send_sems, recv_sems, local_sem):
        x, y, c = lax.axis_index("x"), lax.axis_index("y"), lax.axis_index("c")
        me = 2 * x + y
        local = pltpu.make_async_copy(g_ref.at[me], o_ref.at[me], local_sem)
        local.start()
        sends = []
        for j, (px, py) in enumerate(_other_chips(x, y)):
            sends.append(pltpu.make_async_remote_copy(
                src_ref=g_ref.at[2 * px + py], dst_ref=o_ref.at[me], send_sem=send_sems.at[j], recv_sem=recv_sems.at[j],
                device_id=(px, py, c), device_id_type=MESH))
        for cp in sends:
            cp.start()
        for j, (px, py) in enumerate(_other_chips(x, y)):
            pltpu.make_async_remote_copy(
                src_ref=g_ref.at[me], dst_ref=o_ref.at[2 * px + py], send_sem=send_sems.at[j], recv_sem=recv_sems.at[j],
                device_id=(px, py, c), device_id_type=MESH).wait_recv()
        for cp in sends:
            cp.wait_send()
        local.wait()

    return pl.pallas_call(
        body, name="scatter_grads",
        out_shape=jax.ShapeDtypeStruct(g.shape, g.dtype),
        in_specs=[HBM_SPEC], out_specs=HBM_SPEC,
        scratch_shapes=[pltpu.SemaphoreType.DMA((3,)), pltpu.SemaphoreType.DMA((3,)), pltpu.SemaphoreType.DMA],
    )(g)


def _swap_halves(g):
    def body(g_ref, o_ref, send_sem, recv_sem):
        x, y, c = lax.axis_index("x"), lax.axis_index("y"), lax.axis_index("c")
        cp = pltpu.make_async_remote_copy(src_ref=g_ref.at[1 - c], dst_ref=o_ref, send_sem=send_sem, recv_sem=recv_sem,
                                          device_id=(x, y, 1 - c), device_id_type=MESH)
        cp.start()
        cp.wait()

    return pl.pallas_call(
        body, name="swap_halves",
        out_shape=jax.ShapeDtypeStruct(g.shape[1:], g.dtype),
        in_specs=[HBM_SPEC], out_specs=HBM_SPEC,
        scratch_shapes=[pltpu.SemaphoreType.DMA, pltpu.SemaphoreType.DMA],
    )(g)


def _join_halves(v):
    def body(v_ref, o_ref, send_sem, recv_sem, local_sem):
        x, y, c = lax.axis_index("x"), lax.axis_index("y"), lax.axis_index("c")
        local = pltpu.make_async_copy(v_ref, o_ref.at[c], local_sem)
        local.start()
        cp = pltpu.make_async_remote_copy(src_ref=v_ref, dst_ref=o_ref.at[c], send_sem=send_sem, recv_sem=recv_sem,
                                          device_id=(x, y, 1 - c), device_id_type=MESH)
        cp.start()
        cp.wait_send()
        pltpu.make_async_remote_copy(src_ref=v_ref, dst_ref=o_ref.at[1 - c], send_sem=send_sem, recv_sem=recv_sem,
                                     device_id=(x, y, 1 - c), device_id_type=MESH).wait_recv()
        local.wait()

    return pl.pallas_call(
        body, name="join_halves",
        out_shape=jax.ShapeDtypeStruct((2,) + v.shape, v.dtype),
        in_specs=[HBM_SPEC], out_specs=HBM_SPEC,
        scratch_shapes=[pltpu.SemaphoreType.DMA, pltpu.SemaphoreType.DMA, pltpu.SemaphoreType.DMA],
    )(v)


def _add_pairs(g, theirs, half):
    _, n, rows, width = g.shape
    assert rows % ROW_ALIGN == 0, rows

    def body(half_ref, g_ref, t_ref, o_ref):
        o_ref[...] = (g_ref[...].astype(jnp.float32) + t_ref[...].astype(jnp.float32)).astype(o_ref.dtype)

    blk = pl.BlockSpec((None, ROW_ALIGN, width), lambda p, i, h: (p, i, 0))
    grid_spec = pltpu.PrefetchScalarGridSpec(
        num_scalar_prefetch=1, grid=(n, rows // ROW_ALIGN),
        in_specs=[pl.BlockSpec((None, None, ROW_ALIGN, width), lambda p, i, h: (h[0], p, i, 0)), blk], out_specs=blk)
    return pl.pallas_call(
        body, name="add_pairs", grid_spec=grid_spec, out_shape=jax.ShapeDtypeStruct(theirs.shape, g.dtype),
        compiler_params=pltpu.CompilerParams(dimension_semantics=("parallel", "parallel")),
    )(half, g, theirs)


def _sum_chips(parts):
    n, rows, width = parts.shape
    assert rows % ROW_ALIGN == 0, rows

    def body(p_ref, o_ref):
        p = [p_ref[q].astype(jnp.float32) for q in range(n)]
        o_ref[...] = (p[0] + p[1]) + (p[2] + p[3])

    return pl.pallas_call(
        body, name="sum_chips", grid=(rows // ROW_ALIGN,),
        out_shape=jax.ShapeDtypeStruct((rows, width), jnp.float32),
        in_specs=[pl.BlockSpec((n, ROW_ALIGN, width), lambda i: (0, i, 0))],
        out_specs=pl.BlockSpec((ROW_ALIGN, width), lambda i: (i, 0)),
        compiler_params=pltpu.CompilerParams(dimension_semantics=("parallel",), vmem_limit_bytes=VMEM_LIMIT),
    )(parts)


def _adamw(g, w, m, v):
    rows, width = w.shape
    blk = ROW_ALIGN // 2

    def body(g_ref, w_ref, m_ref, v_ref, d_out, m_out, v_out):
        g = g_ref[...]
        m_new = ADAM_B1 * m_ref[...] + (1.0 - ADAM_B1) * g
        v_new = ADAM_B2 * v_ref[...] + (1.0 - ADAM_B2) * jnp.square(g)
        m_hat = m_new / (1.0 - ADAM_B1 ** ADAM_STEP)
        v_hat = v_new / (1.0 - ADAM_B2 ** ADAM_STEP)
        d_out[...] = -ADAM_LR * (m_hat / (jnp.sqrt(v_hat) + ADAM_EPS) + ADAM_WD * w_ref[...])
        m_out[...] = m_new
        v_out[...] = v_new

    spec = pl.BlockSpec((blk, width), lambda i: (i, 0))
    shape = jax.ShapeDtypeStruct((rows, width), jnp.float32)
    return pl.pallas_call(
        body, name="adamw", grid=(rows // blk,),
        out_shape=(shape,) * 3, in_specs=[spec] * 4, out_specs=(spec,) * 3,
        compiler_params=pltpu.CompilerParams(dimension_semantics=("parallel",), vmem_limit_bytes=VMEM_LIMIT),
    )(g, w, m, v)


def _tile(dim, pref):
    if dim <= pref:
        return dim
    for t in range(pref - pref % 128, 0, -128):
        if dim % t == 0:
            return t
    raise ValueError(f"no 128-aligned tile for {dim}")


def _matmul(a, b, *, ta=False, tb=False, name, epilogue=None, extra=None, out_dtype=jnp.float32):
    (k_a, m) = a.shape if ta else a.shape[::-1]
    (k_b, n) = b.shape[::-1] if tb else b.shape
    assert k_a == k_b, (a.shape, b.shape, ta, tb)
    k = k_a
    tm, tn, tk = _tile(m, 1024), _tile(n, 512), _tile(k, 1024)
    nk = k // tk
    a_spec = pl.BlockSpec((tk, tm), lambda i, j, l: (l, i)) if ta else pl.BlockSpec((tm, tk), lambda i, j, l: (i, l))
    b_spec = pl.BlockSpec((tn, tk), lambda i, j, l: (j, l)) if tb else pl.BlockSpec((tk, tn), lambda i, j, l: (l, j))
    o_spec = pl.BlockSpec((tm, tn), lambda i, j, l: (i, j))
    dims = (((0 if ta else 1,), (1 if tb else 0,)), ((), ()))
    has_extra = epilogue == "relu2_grad"
    assert has_extra == (extra is not None)

    def body(*refs):
        a_ref, b_ref = refs[:2]
        outs = refs[2 + has_extra:2 + has_extra + (2 if epilogue == "relu2" else 1)]
        l = pl.program_id(2)
        part = lax.dot_general(a_ref[...].astype(jnp.bfloat16), b_ref[...].astype(jnp.bfloat16), dims,
                               preferred_element_type=jnp.float32)

        def finish(acc):
            if epilogue is None:
                outs[0][...] = acc.astype(out_dtype)
            elif epilogue == "relu2":
                outs[0][...] = acc.astype(jnp.bfloat16)
                outs[1][...] = jnp.square(jnp.maximum(acc, 0.0)).astype(jnp.bfloat16)
            else:
                outs[0][...] = (acc * (2.0 * jnp.maximum(refs[2][...].astype(jnp.float32), 0.0))).astype(out_dtype)

        if nk == 1:
            finish(part)
            return
        acc_ref = refs[-1]

        @pl.when(l == 0)
        def _():
            acc_ref[...] = part

        @pl.when((l > 0) & (l < nk - 1))
        def _():
            acc_ref[...] += part

        @pl.when(l == nk - 1)
        def _():
            finish(acc_ref[...] + part)

    if epilogue == "relu2":
        out_shape = (jax.ShapeDtypeStruct((m, n), jnp.bfloat16),) * 2
        out_specs = (o_spec, o_spec)
    else:
        out_shape = jax.ShapeDtypeStruct((m, n), out_dtype)
        out_specs = o_spec
    return pl.pallas_call(
        body, name=name, grid=(m // tm, n // tn, nk), out_shape=out_shape,
        in_specs=[a_spec, b_spec] + ([o_spec] if has_extra else []), out_specs=out_specs,
        scratch_shapes=[pltpu.VMEM((tm, tn), jnp.float32)] if nk > 1 else [],
        compiler_params=pltpu.CompilerParams(dimension_semantics=("parallel", "parallel", "arbitrary"),
                                             vmem_limit_bytes=VMEM_LIMIT),
    )(*((a, b) + ((extra,) if has_extra else ())))


def _lo(x):
    return lax.stop_gradient(x.astype(jnp.bfloat16))


@jax.custom_vjp
def _linear(x, x_lo, w, slot):
    del x, slot
    return _matmul(x_lo, w, name="linear_fwd")


def _linear_fwd(x, x_lo, w, slot):
    del x, slot
    return _matmul(x_lo, w, name="linear_fwd"), (x_lo, w)


def _linear_bwd(res, dy):
    x_lo, w = res
    dx = _matmul(dy, w, tb=True, name="linear_dx")
    dw = _matmul(x_lo, dy, ta=True, name="linear_dw")
    return dx, jnp.zeros_like(x_lo), jnp.zeros_like(w), dw


_linear.defvjp(_linear_fwd, _linear_bwd)


@jax.custom_vjp
def _mlp(h, h_lo, w_up, w_down, slot_up, slot_down):
    return _mlp_fwd(h, h_lo, w_up, w_down, slot_up, slot_down)[0]


def _mlp_fwd(h, h_lo, w_up, w_down, slot_up, slot_down):
    del h, slot_up, slot_down
    up, act = _matmul(h_lo, w_up, name="mlp_up", epilogue="relu2")
    return _matmul(act, w_down, name="mlp_down"), (h_lo, up, act, w_up, w_down)


def _mlp_bwd(res, dy):
    h_lo, up, act, w_up, w_down = res
    d_up = _matmul(dy, w_down, tb=True, name="mlp_d_up", epilogue="relu2_grad", extra=up, out_dtype=jnp.bfloat16)
    dw_down = _matmul(act, dy, ta=True, name="mlp_dw_down")
    dw_up = _matmul(h_lo, d_up, ta=True, name="mlp_dw_up")
    dh = _matmul(d_up, w_up, tb=True, name="mlp_dh")
    return dh, jnp.zeros_like(h_lo), jnp.zeros_like(w_up), jnp.zeros_like(w_down), dw_up, dw_down


_mlp.defvjp(_mlp_fwd, _mlp_bwd)


LN_ROWS = 256


def _ln_call(h, mix, g, b):
    s, d = h.shape
    tok = pl.BlockSpec((LN_ROWS, d), lambda i: (i, 0))
    vec = pl.BlockSpec((1, d), lambda i: (0, 0))
    stat = pl.BlockSpec((LN_ROWS, 1), lambda i: (i, 0))

    def body(h_ref, mix_ref, g_ref, b_ref, y_ref, ylo_ref, xhat_ref, rstd_ref):
        z = DN_ALPHA * h_ref[...] + mix_ref[...]
        mu = jnp.mean(z, axis=-1, keepdims=True)
        zc = z - mu
        rstd = lax.rsqrt(jnp.mean(jnp.square(zc), axis=-1, keepdims=True) + LN_EPS)
        xhat = zc * rstd
        y = xhat * g_ref[...] + b_ref[...]
        y_ref[...] = y
        ylo_ref[...] = y.astype(ylo_ref.dtype)
        xhat_ref[...] = xhat
        rstd_ref[...] = rstd

    sd = jax.ShapeDtypeStruct
    return pl.pallas_call(
        body, name="ln_fwd", grid=(s // LN_ROWS,),
        out_shape=(sd((s, d), jnp.float32), sd((s, d), jnp.bfloat16), sd((s, d), jnp.float32), sd((s, 1), jnp.float32)),
        in_specs=[tok, tok, vec, vec], out_specs=(tok, tok, tok, stat),
        compiler_params=pltpu.CompilerParams(dimension_semantics=("parallel",)),
    )(h, mix, g, b)


def _ln_grad_call(dy, xhat, rstd, g):
    s, d = dy.shape
    tok = pl.BlockSpec((LN_ROWS, d), lambda i: (i, 0))
    vec = pl.BlockSpec((1, d), lambda i: (0, 0))
    stat = pl.BlockSpec((LN_ROWS, 1), lambda i: (i, 0))

    def body(dy_ref, xhat_ref, rstd_ref, g_ref, dz_ref, dg_ref, db_ref):
        @pl.when(pl.program_id(0) == 0)
        def _():
            dg_ref[...] = jnp.zeros_like(dg_ref)
            db_ref[...] = jnp.zeros_like(db_ref)

        dy, xhat = dy_ref[...], xhat_ref[...]
        dyg = dy * g_ref[...]
        m1 = jnp.mean(dyg, axis=-1, keepdims=True)
        m2 = jnp.mean(dyg * xhat, axis=-1, keepdims=True)
        dz_ref[...] = rstd_ref[...] * (dyg - m1 - xhat * m2)
        dg_ref[...] += jnp.sum(dy * xhat, axis=0, keepdims=True)
        db_ref[...] += jnp.sum(dy, axis=0, keepdims=True)

    sd = jax.ShapeDtypeStruct
    return pl.pallas_call(
        body, name="ln_bwd", grid=(s // LN_ROWS,),
        out_shape=(sd((s, d), jnp.float32), sd((1, d), jnp.float32), sd((1, d), jnp.float32)),
        in_specs=[tok, tok, stat, vec], out_specs=(tok, vec, vec),
        compiler_params=pltpu.CompilerParams(dimension_semantics=("arbitrary",)),
    )(dy, xhat, rstd, g)


@jax.custom_vjp
def _ln_res(h, mix, g, b):
    return _ln_call(h, mix, g, b)[:2]


def _ln_res_fwd(h, mix, g, b):
    y, y_lo, xhat, rstd = _ln_call(h, mix, g, b)
    return (y, y_lo), (xhat, rstd, g)


def _ln_res_bwd(res, cts):
    xhat, rstd, g = res
    dz, dg, db = _ln_grad_call(cts[0], xhat, rstd, g)
    return DN_ALPHA * dz, dz, dg, db


_ln_res.defvjp(_ln_res_fwd, _ln_res_bwd)


MXU_DTYPE = jnp.bfloat16
DN_CB = 8
DN_GROUP = 4
DN_SCALE = DN_DK ** -0.5


def _dot(a, b, ca=1, cb=0):
    return lax.dot_general(a.astype(MXU_DTYPE), b.astype(MXU_DTYPE), (((ca,), (cb,)), ((), ())),
                           preferred_element_type=jnp.float32)


def _chunk_masks():
    row = lax.broadcasted_iota(jnp.int32, (CHUNK, CHUNK), 0)
    col = lax.broadcasted_iota(jnp.int32, (CHUNK, CHUNK), 1)
    return row >= col, row > col, row == col


def _to_col(row_vec):
    _, _, eye = _chunk_masks()
    return jnp.sum(jnp.where(eye, jnp.broadcast_to(row_vec, (CHUNK, CHUNK)), 0.0), axis=1, keepdims=True)


def _to_row(col_vec):
    _, _, eye = _chunk_masks()
    return jnp.sum(jnp.where(eye, jnp.broadcast_to(col_vec, (CHUNK, CHUNK)), 0.0), axis=0, keepdims=True)


def _last_row(col_vec):
    last = lax.broadcasted_iota(jnp.int32, (CHUNK, 1), 0) == CHUNK - 1
    return jnp.sum(jnp.where(last, col_vec, 0.0), axis=0, keepdims=True), last


def _chunk_terms(q, k, beta, gcc, gcr):
    incl, strict, _ = _chunk_masks()
    decay = jnp.where(incl, jnp.exp(jnp.minimum(gcc - gcr, 0.0)), 0.0)
    kb = k * beta
    lmat = jnp.where(strict, _dot(kb, k, 1, 1) * decay, 0.0)
    intra = jnp.where(incl, _dot(q, k, 1, 1) * decay, 0.0)
    return decay, kb, lmat, intra


def _dot3(a, b, ca=1, cb=0):
    if MXU_DTYPE == jnp.float32:
        return _dot(a, b, ca, cb)
    a_hi, b_hi = a.astype(MXU_DTYPE), b.astype(MXU_DTYPE)
    a_lo = (a - a_hi.astype(jnp.float32)).astype(MXU_DTYPE)
    b_lo = (b - b_hi.astype(jnp.float32)).astype(MXU_DTYPE)
    return _dot(a_hi, b_hi, ca, cb) + (_dot(a_hi, b_lo, ca, cb) + _dot(a_lo, b_hi, ca, cb))


def _unit_lower_inverse(lmats):
    _, _, eye = _chunk_masks()
    ident = jnp.where(eye, 1.0, 0.0)
    ts = [ident - m for m in lmats]
    ps = [_dot(m, m) for m in lmats]
    for _ in range(4):
        ts = [t + _dot(t, p) for t, p in zip(ts, ps)]
        ps = [_dot(p, p) for p in ps]
    ts = [t + _dot(t, p) for t, p in zip(ts, ps)]
    resids = [(t - ident) + _dot3(m, t) for m, t in zip(lmats, ts)]
    return [t - _dot(t, r) for t, r in zip(ts, resids)]


def _dn_specs(n_chunks):
    tok = pl.BlockSpec((DN_CB * CHUNK, DN_DK), lambda h, n: (n, h))
    rowv = pl.BlockSpec((None, DN_CB, CHUNK), lambda h, n: (h, n, 0))
    sq = pl.BlockSpec((None, DN_CB, CHUNK, CHUNK), lambda h, n: (h, n, 0, 0))
    lane = pl.BlockSpec((None, DN_CB, 1, DN_DV), lambda h, n: (h, n, 0, 0))
    planes = [pl.BlockSpec((None, DN_CB * CHUNK, DN_DK), functools.partial(lambda h, n, p: (p, n, h), p=p))
              for p in range(3)]
    return tok, rowv, sq, lane, planes


def _dn_prep(qkv, beta, gc):
    s = qkv.shape[1]
    n_chunks = s // CHUNK
    tok, rowv, sq, lane, planes = _dn_specs(n_chunks)
    tok_shape = qkv.shape[1:]

    def body(q_ref, k_ref, v_ref, beta_ref, gc_ref, u_ref, w_ref, qd_ref, kd_ref, intra_ref, t_ref, cd_ref):
        for c0 in range(0, DN_CB, DN_GROUP):
            chunks = range(c0, c0 + DN_GROUP)
            rhs, lmats = [], []
            for c in chunks:
                rows = pl.ds(c * CHUNK, CHUNK)
                q_c, k_c, v_c = q_ref[rows, :] * DN_SCALE, k_ref[rows, :], v_ref[rows, :]
                gcr_c = gc_ref[pl.ds(c, 1), :]
                beta_c, gcc_c = _to_col(beta_ref[pl.ds(c, 1), :]), _to_col(gcr_c)
                _, kb, lmat, intra = _chunk_terms(q_c, k_c, beta_c, gcc_c, gcr_c)
                eg = jnp.exp(gcc_c)
                g_last, _ = _last_row(gcc_c)
                qd_ref[rows, :] = (q_c * eg).astype(qd_ref.dtype)
                kd_ref[rows, :] = (k_c * jnp.exp(g_last - gcc_c)).astype(kd_ref.dtype)
                intra_ref[c] = intra.astype(intra_ref.dtype)
                cd_ref[c] = jnp.broadcast_to(jnp.exp(g_last), (1, DN_DV))
                rhs.append(jnp.concatenate([v_c * beta_c, kb * eg], axis=1))
                lmats.append(lmat)
            ts = _unit_lower_inverse(lmats)
            sols = [_dot3(t, r) for t, r in zip(ts, rhs)]
            for c, t, sol in zip(chunks, ts, sols):
                rows = pl.ds(c * CHUNK, CHUNK)
                t_ref[c] = t
                u_ref[rows, :] = sol[:, :DN_DV]
                w_ref[rows, :] = sol[:, DN_DV:].astype(w_ref.dtype)

    f32, mx = jnp.float32, MXU_DTYPE
    sd = jax.ShapeDtypeStruct
    return pl.pallas_call(
        body, name="dn_prep", grid=(DN_HEADS, n_chunks // DN_CB),
        out_shape=(sd(tok_shape, f32), sd(tok_shape, mx), sd(tok_shape, mx), sd(tok_shape, mx),
                   sd((DN_HEADS, n_chunks, CHUNK, CHUNK), mx), sd((DN_HEADS, n_chunks, CHUNK, CHUNK), f32),
                   sd((DN_HEADS, n_chunks, 1, DN_DV), f32)),
        in_specs=planes + [rowv, rowv], out_specs=(tok, tok, tok, tok, sq, sq, lane),
        compiler_params=pltpu.CompilerParams(dimension_semantics=("parallel", "parallel")),
    )(qkv, qkv, qkv, beta, gc)


def _dn_scan(u, w, qd, kd, intra, cd):
    s, width = u.shape
    n_chunks = s // CHUNK
    tok = pl.BlockSpec((CHUNK, width), lambda n: (n, 0))
    sq = pl.BlockSpec((DN_HEADS, None, CHUNK, CHUNK), lambda n: (0, n, 0, 0))
    lane = pl.BlockSpec((DN_HEADS, None, 1, DN_DV), lambda n: (0, n, 0, 0))
    st = pl.BlockSpec((DN_HEADS, None, DN_DK, DN_DV), lambda n: (0, n, 0, 0))

    def body(u_ref, w_ref, qd_ref, kd_ref, intra_ref, cd_ref, o_ref, vn_ref, st_ref, state):
        @pl.when(pl.program_id(0) == 0)
        def _():
            state[...] = jnp.zeros_like(state)

        heads = range(DN_HEADS)
        cols = [pl.ds(h * DN_DK, DN_DK) for h in heads]
        s_f = [state[h] for h in heads]
        s_mx = [s.astype(MXU_DTYPE) for s in s_f]
        for h in heads:
            st_ref[h] = s_mx[h]
        ws = [_dot(w_ref[:, cols[h]], s_mx[h]) for h in heads]
        qs = [_dot(qd_ref[:, cols[h]], s_mx[h]) for h in heads]
        v_new = [(u_ref[:, cols[h]] - ws[h]).astype(MXU_DTYPE) for h in heads]
        inner = [_dot(intra_ref[h], v_new[h]) for h in heads]
        outer = [_dot(kd_ref[:, cols[h]], v_new[h], 0, 0) for h in heads]
        for h in heads:
            vn_ref[:, cols[h]] = v_new[h]
            o_ref[:, cols[h]] = qs[h] + inner[h]
            state[h] = s_f[h] * cd_ref[h] + outer[h]

    sd = jax.ShapeDtypeStruct
    return pl.pallas_call(
        body, name="dn_scan", grid=(n_chunks,),
        out_shape=(sd(u.shape, jnp.float32), sd(u.shape, MXU_DTYPE),
                   sd((DN_HEADS, n_chunks, DN_DK, DN_DV), MXU_DTYPE)),
        in_specs=[tok, tok, tok, tok, sq, lane], out_specs=(tok, tok, st),
        scratch_shapes=[pltpu.VMEM((DN_HEADS, DN_DK, DN_DV), jnp.float32)],
        compiler_params=pltpu.CompilerParams(dimension_semantics=("arbitrary",)),
    )(u, w, qd, kd, intra, cd)


def _dn_bwd_scan(do, w, qd, kd, intra, cd, vn, st):
    s, width = do.shape
    n_chunks = s // CHUNK
    last = n_chunks - 1
    tok = pl.BlockSpec((CHUNK, width), lambda n: (last - n, 0))
    sq = pl.BlockSpec((DN_HEADS, None, CHUNK, CHUNK), lambda n: (0, last - n, 0, 0))
    lane = pl.BlockSpec((DN_HEADS, None, 1, DN_DV), lambda n: (0, last - n, 0, 0))
    stt = pl.BlockSpec((DN_HEADS, None, DN_DK, DN_DV), lambda n: (0, last - n, 0, 0))

    def body(do_ref, w_ref, qd_ref, kd_ref, intra_ref, cd_ref, vn_ref, st_ref,
             du_ref, dw_ref, dqd_ref, dkd_ref, dintra_ref, dgl_ref, dstate):
        @pl.when(pl.program_id(0) == 0)
        def _():
            dstate[...] = jnp.zeros_like(dstate)

        heads = range(DN_HEADS)
        cols = [pl.ds(h * DN_DK, DN_DK) for h in heads]
        ds_f = [dstate[h] for h in heads]
        ds_mx = [d.astype(MXU_DTYPE) for d in ds_f]
        do_h = [do_ref[:, cols[h]].astype(MXU_DTYPE) for h in heads]
        dv_a = [_dot(intra_ref[h], do_h[h], 0, 0) for h in heads]
        dv_b = [_dot(kd_ref[:, cols[h]], ds_mx[h]) for h in heads]
        d_intra = [_dot(do_h[h], vn_ref[:, cols[h]], 1, 1) for h in heads]
        d_qd = [_dot(do_h[h], st_ref[h], 1, 1) for h in heads]
        d_kd = [_dot(vn_ref[:, cols[h]], ds_mx[h], 1, 1) for h in heads]
        ds_q = [_dot(qd_ref[:, cols[h]], do_h[h], 0, 0) for h in heads]
        dv_new = [dv_a[h] + dv_b[h] for h in heads]
        dv_mx = [d.astype(MXU_DTYPE) for d in dv_new]
        d_w = [_dot(dv_mx[h], st_ref[h], 1, 1) for h in heads]
        ds_w = [_dot(w_ref[:, cols[h]], dv_mx[h], 0, 0) for h in heads]
        for h in heads:
            du_ref[:, cols[h]] = dv_new[h]
            dintra_ref[h] = d_intra[h]
            dqd_ref[:, cols[h]] = d_qd[h]
            dkd_ref[:, cols[h]] = d_kd[h]
            dw_ref[:, cols[h]] = -d_w[h]
            cd_h = cd_ref[h]
            dcd = jnp.sum(jnp.sum(st_ref[h].astype(jnp.float32) * ds_f[h], axis=1, keepdims=True), axis=0,
                          keepdims=True)
            dgl_ref[h] = dcd * cd_h
            dstate[h] = ds_q[h] + ds_f[h] * cd_h - ds_w[h]

    sd = jax.ShapeDtypeStruct
    f32 = jnp.float32
    return pl.pallas_call(
        body, name="dn_bwd_scan", grid=(n_chunks,),
        out_shape=(sd(do.shape, f32), sd(do.shape, f32), sd(do.shape, f32), sd(do.shape, f32),
                   sd((DN_HEADS, n_chunks, CHUNK, CHUNK), f32), sd((DN_HEADS, n_chunks, 1, DN_DV), f32)),
        in_specs=[tok, tok, tok, tok, sq, lane, tok, stt], out_specs=(tok, tok, tok, tok, sq, lane),
        scratch_shapes=[pltpu.VMEM((DN_HEADS, DN_DK, DN_DV), f32)],
        compiler_params=pltpu.CompilerParams(dimension_semantics=("arbitrary",)),
    )(do, w, qd, kd, intra, cd, vn, st)


def _dn_bwd_chunks(qkv, beta, gc, t, u, w, du, dw, dqd, dkd, dintra, dgl):
    s = qkv.shape[1]
    n_chunks = s // CHUNK
    tok, rowv, sq, lane, planes = _dn_specs(n_chunks)
    all_planes = pl.BlockSpec((3, DN_CB * CHUNK, DN_DK), lambda h, n: (0, n, h))

    def body(q_ref, k_ref, v_ref, beta_ref, gc_ref, t_ref, u_ref, w_ref, du_ref, dw_ref, dqd_ref, dkd_ref,
             dintra_ref, dgl_ref, dqkv_ref, dbeta_ref, dgc_ref):
        incl, strict, _ = _chunk_masks()

        def first(c):
            rows = pl.ds(c * CHUNK, CHUNK)
            q_c, k_c = q_ref[rows, :] * DN_SCALE, k_ref[rows, :]
            gcr_c = gc_ref[pl.ds(c, 1), :]
            beta_c, gcc_c = _to_col(beta_ref[pl.ds(c, 1), :]), _to_col(gcr_c)
            decay, kb, lmat, intra = _chunk_terms(q_c, k_c, beta_c, gcc_c, gcr_c)
            d_sol = jnp.concatenate([du_ref[rows, :], dw_ref[rows, :]], axis=1)
            d_rhs = _dot3(t_ref[c], d_sol, 0, 0)
            return dict(rows=rows, q=q_c, k=k_c, beta=beta_c, gcc=gcc_c, decay=decay, kb=kb, lmat=lmat, intra=intra,
                        d_rhs=d_rhs)

        def second(c, e):
            sol = jnp.concatenate([u_ref[e["rows"], :], w_ref[e["rows"], :].astype(jnp.float32)], axis=1)
            e["d_l"] = jnp.where(strict, -_dot(e["d_rhs"], sol, 1, 1), 0.0)
            e["d_intra"] = jnp.where(incl, dintra_ref[c], 0.0)
            d_qk = e["d_intra"] * e["decay"]
            e["dq"] = _dot(d_qk, e["k"])
            e["dk"] = _dot(d_qk, e["q"], 0, 0)

        def third(e):
            d_a = e["d_l"] * e["decay"]
            e["dkb"] = _dot(d_a, e["k"])
            e["dk"] = e["dk"] + _dot(d_a, e["kb"], 0, 0)

        def last(c, e):
            rows, q_c, k_c, beta_c, gcc_c = e["rows"], e["q"], e["k"], e["beta"], e["gcc"]
            v_c = v_ref[rows, :]
            eg = jnp.exp(gcc_c)
            g_last, is_last = _last_row(gcc_c)
            e_rev = jnp.exp(g_last - gcc_c)
            d_rhs_u, d_rhs_w = e["d_rhs"][:, :DN_DV], e["d_rhs"][:, DN_DV:]
            dqkv_ref[2, rows, :] = d_rhs_u * beta_c
            dbeta = jnp.sum(d_rhs_u * v_c, axis=1, keepdims=True)
            dkb = e["dkb"] + d_rhs_w * eg
            dgc = jnp.sum(d_rhs_w * e["kb"] * eg, axis=1, keepdims=True)
            m1 = e["d_l"] * e["lmat"]
            dgc = dgc + jnp.sum(m1, axis=1, keepdims=True)
            dgr = -jnp.sum(m1, axis=0, keepdims=True)
            m2 = e["d_intra"] * e["intra"]
            dgc = dgc + jnp.sum(m2, axis=1, keepdims=True)
            dgr = dgr - jnp.sum(m2, axis=0, keepdims=True)
            dqd = dqd_ref[rows, :]
            dq = e["dq"] + dqd * eg
            dgc = dgc + jnp.sum(dqd * q_c * eg, axis=1, keepdims=True)
            dkd = dkd_ref[rows, :]
            dk = e["dk"] + dkd * e_rev
            tk = jnp.sum(dkd * k_c * e_rev, axis=1, keepdims=True)
            dgc = dgc - tk
            d_last = dgl_ref[c][:, :1] + jnp.sum(tk, axis=0, keepdims=True)
            dgc = dgc + jnp.where(is_last, d_last, 0.0)
            dk = dk + dkb * beta_c
            dbeta = dbeta + jnp.sum(dkb * k_c, axis=1, keepdims=True)
            dqkv_ref[0, rows, :] = dq * DN_SCALE
            dqkv_ref[1, rows, :] = dk
            dbeta_ref[pl.ds(c, 1), :] = _to_row(dbeta)
            dgc_ref[pl.ds(c, 1), :] = _to_row(dgc) + dgr

        for c0 in range(0, DN_CB, DN_GROUP):
            chunks = range(c0, c0 + DN_GROUP)
            env = [first(c) for c in chunks]
            for c, e in zip(chunks, env):
                second(c, e)
            for e in env:
                third(e)
            for c, e in zip(chunks, env):
                last(c, e)

    sd = jax.ShapeDtypeStruct
    f32 = jnp.float32
    return pl.pallas_call(
        body, name="dn_bwd_chunks", grid=(DN_HEADS, n_chunks // DN_CB),
        out_shape=(sd(qkv.shape, f32), sd(beta.shape, f32), sd(gc.shape, f32)),
        in_specs=planes + [rowv, rowv, sq, tok, tok, tok, tok, tok, tok, sq, lane],
        out_specs=(all_planes, rowv, rowv),
        compiler_params=pltpu.CompilerParams(dimension_semantics=("parallel", "parallel")),
    )(qkv, qkv, qkv, beta, gc, t, u, w, du, dw, dqd, dkd, dintra, dgl)


@jax.custom_vjp
def _delta_rule_op(qkv, beta, gc):
    return _delta_rule_fwd(qkv, beta, gc)[0]


def _delta_rule_fwd(qkv, beta, gc):
    u, w, qd, kd, intra, t, cd = _dn_prep(qkv, beta, gc)
    out, vn, st = _dn_scan(u, w, qd, kd, intra, cd)
    return out, (qkv, beta, gc, u, w, qd, kd, intra, t, cd, vn, st)


def _delta_rule_bwd(res, do):
    qkv, beta, gc, u, w, qd, kd, intra, t, cd, vn, st = res
    du, dw, dqd, dkd, dintra, dgl = _dn_bwd_scan(do, w, qd, kd, intra, cd, vn, st)
    return _dn_bwd_chunks(qkv, beta, gc, t, u, w, du, dw, dqd, dkd, dintra, dgl)


_delta_rule_op.defvjp(_delta_rule_fwd, _delta_rule_bwd)


def _gated_delta_rule(qkv, g, beta):
    s, h = g.shape
    n_chunks = s // CHUNK
    gc = jnp.cumsum(g.T.reshape(h, n_chunks, CHUNK), axis=-1)
    return _delta_rule_op(qkv, beta.T.reshape(h, n_chunks, CHUNK), gc)


PRE_ROWS = 256
HALO = 8
PRE_W = DN_QK_W


def _shift_rows(xs, k):
    return pltpu.roll(xs, k, 0)[HALO:]


def _conv_silu(x_ref, halo_ref, w_ref, first_block):
    halo = jnp.where(first_block, 0.0, halo_ref[...])
    xs = jnp.concatenate([halo, x_ref[...]], axis=0)
    taps = [_shift_rows(xs, CONV_WIDTH - 1 - j) for j in range(CONV_WIDTH - 1)] + [x_ref[...]]
    conv = sum(w_ref[pl.ds(j, 1), :] * taps[j] for j in range(CONV_WIDTH))
    return conv, jax.nn.sigmoid(conv), taps


def _pre_specs():
    blk = pl.BlockSpec((PRE_ROWS, PRE_W), lambda j, i: (i, j))
    prev = pl.BlockSpec((HALO, PRE_W), lambda j, i: (jnp.maximum(i * (PRE_ROWS // HALO) - 1, 0), j))
    wts = pl.BlockSpec((CONV_WIDTH, PRE_W), lambda j, i: (0, j))
    plane = pl.BlockSpec((None, PRE_ROWS, PRE_W), lambda j, i: (j, i, 0))
    return blk, prev, wts, plane


def _pre_fwd_call(x, conv_w):
    s = x.shape[0]
    blk, prev, wts, plane = _pre_specs()

    def body(x_ref, halo_ref, w_ref, o_ref):
        conv, sig, _ = _conv_silu(x_ref, halo_ref, w_ref, pl.program_id(1) == 0)
        act = conv * sig
        is_v = pl.program_id(0) == 2
        for h in range(DN_HEADS):
            cols = slice(h * DN_DK, (h + 1) * DN_DK)
            a_h = act[:, cols]
            r = lax.rsqrt(jnp.sum(a_h * a_h, axis=-1, keepdims=True) + NORM_EPS)
            o_ref[:, cols] = a_h * jnp.where(is_v, 1.0, r)

    return pl.pallas_call(
        body, name="pre_fwd", grid=(3, s // PRE_ROWS),
        out_shape=jax.ShapeDtypeStruct((3, s, PRE_W), jnp.float32),
        in_specs=[blk, prev, wts], out_specs=plane,
        compiler_params=pltpu.CompilerParams(dimension_semantics=("parallel", "parallel")),
    )(x, x, conv_w)


def _pre_bwd_act_call(x, conv_w, d_out):
    s = x.shape[0]
    blk, prev, wts, plane = _pre_specs()

    def body(x_ref, halo_ref, w_ref, do_ref, dc_ref):
        conv, sig, _ = _conv_silu(x_ref, halo_ref, w_ref, pl.program_id(1) == 0)
        act = conv * sig
        d_silu = sig * (1.0 + conv * (1.0 - sig))
        is_v = pl.program_id(0) == 2
        for h in range(DN_HEADS):
            cols = slice(h * DN_DK, (h + 1) * DN_DK)
            a_h, do_h = act[:, cols], do_ref[:, cols]
            r = lax.rsqrt(jnp.sum(a_h * a_h, axis=-1, keepdims=True) + NORM_EPS)
            n_h = a_h * r
            d_norm = r * (do_h - n_h * jnp.sum(do_h * n_h, axis=-1, keepdims=True))
            dc_ref[:, cols] = jnp.where(is_v, do_h, d_norm) * d_silu[:, cols]

    return pl.pallas_call(
        body, name="pre_bwd_act", grid=(3, s // PRE_ROWS),
        out_shape=jax.ShapeDtypeStruct(x.shape, jnp.float32),
        in_specs=[blk, prev, wts, plane], out_specs=blk,
        compiler_params=pltpu.CompilerParams(dimension_semantics=("parallel", "parallel")),
    )(x, x, conv_w, d_out)


def _pre_bwd_conv_call(x, conv_w, dc):
    s = x.shape[0]
    n_blocks = s // PRE_ROWS
    blk, prev, wts, _ = _pre_specs()
    nxt = pl.BlockSpec((HALO, PRE_W), lambda j, i: (jnp.minimum((i + 1) * (PRE_ROWS // HALO), s // HALO - 1), j))

    def body(x_ref, halo_ref, w_ref, dc_ref, dcn_ref, dx_ref, dw_ref):
        i = pl.program_id(1)

        @pl.when(i == 0)
        def _():
            dw_ref[...] = jnp.zeros_like(dw_ref)

        dcv = dc_ref[...]
        ahead = jnp.concatenate([dcv, jnp.where(i == n_blocks - 1, 0.0, dcn_ref[...])], axis=0)
        dx = w_ref[pl.ds(CONV_WIDTH - 1, 1), :] * dcv
        for j in range(CONV_WIDTH - 1):
            k = CONV_WIDTH - 1 - j
            dx = dx + w_ref[pl.ds(j, 1), :] * pltpu.roll(ahead, PRE_ROWS + HALO - k, 0)[:PRE_ROWS]
        dx_ref[...] = dx
        halo = jnp.where(i == 0, 0.0, halo_ref[...])
        xs = jnp.concatenate([halo, x_ref[...]], axis=0)
        for j in range(CONV_WIDTH):
            tap = x_ref[...] if j == CONV_WIDTH - 1 else _shift_rows(xs, CONV_WIDTH - 1 - j)
            dw_ref[pl.ds(j, 1), :] += jnp.sum(dcv * tap, axis=0, keepdims=True)

    sd = jax.ShapeDtypeStruct
    return pl.pallas_call(
        body, name="pre_bwd_conv", grid=(3, n_blocks),
        out_shape=(sd(x.shape, jnp.float32), sd(conv_w.shape, jnp.float32)),
        in_specs=[blk, prev, wts, blk, nxt], out_specs=(blk, wts),
        compiler_params=pltpu.CompilerParams(dimension_semantics=("parallel", "arbitrary")),
    )(x, x, conv_w, dc, dc)


@jax.custom_vjp
def _pre_op(x, conv_w):
    return _pre_fwd_call(x, conv_w)


def _pre_op_fwd(x, conv_w):
    return _pre_fwd_call(x, conv_w), (x, conv_w)


def _pre_op_bwd(res, d_out):
    x, conv_w = res
    return _pre_bwd_conv_call(x, conv_w, _pre_bwd_act_call(x, conv_w, d_out))


_pre_op.defvjp(_pre_op_fwd, _pre_op_bwd)


def _project(h, h_lo, w, slot):
    b, s, d = h.shape
    return _linear(h.reshape(b * s, d), h_lo.reshape(b * s, d), w, slot).reshape(b, s, w.shape[1])


def _rope_tables(positions, dh):
    inv_freq = ROPE_THETA ** (-jnp.arange(0, dh, 2, dtype=jnp.float32) / dh)
    ang = positions.astype(jnp.float32)[..., None] * inv_freq
    return jnp.cos(ang)[:, :, None, :], jnp.sin(ang)[:, :, None, :]


def _apply_rope(x, cos, sin):
    x1, x2 = jnp.split(x, 2, axis=-1)
    return jnp.concatenate([x1 * cos - x2 * sin, x2 * cos + x1 * sin], axis=-1)


_MASKED = -1e30


def _swa_probs(qs, k_h, sinks, valid):
    ss = [jnp.where(valid, _dot(q_h, k_h, 1, 1) * (SWA_DH ** -0.5), _MASKED) for q_h in qs]
    ms = [jnp.maximum(jnp.max(s, axis=-1, keepdims=True), sink) for s, sink in zip(ss, sinks)]
    ps = [jnp.exp(s - m) for s, m in zip(ss, ms)]
    es = [jnp.exp(sink - m) for sink, m in zip(sinks, ms)]
    invs = [1.0 / (jnp.sum(p, axis=-1, keepdims=True) + e) for p, e in zip(ps, es)]
    return [p * inv for p, inv in zip(ps, invs)], [e * inv for e, inv in zip(es, invs)]


def _swa_valid(n):
    qi = lax.broadcasted_iota(jnp.int32, (WINDOW, 2 * WINDOW), 0)
    kj = lax.broadcasted_iota(jnp.int32, (WINDOW, 2 * WINDOW), 1)
    diff = qi + WINDOW - kj
    return (diff >= 0) & (diff < WINDOW) & ((kj >= WINDOW) | (n > 0))


def _swa_specs():
    qs = pl.BlockSpec((WINDOW, SWA_Q_W), lambda n: (n, 0))
    cur = pl.BlockSpec((WINDOW, SWA_KV_W), lambda n: (n, 0))
    prev = pl.BlockSpec((WINDOW, SWA_KV_W), lambda n: (jnp.maximum(n - 1, 0), 0))
    sk = pl.BlockSpec((SWA_HEADS, 1, 128), lambda n: (0, 0, 0))
    return qs, cur, prev, sk


def _swa_fwd_call(q, k, v, sinks):
    qs, cur, prev, sk = _swa_specs()

    def body(q_ref, kp_ref, kc_ref, vp_ref, vc_ref, sink_ref, o_ref):
        valid = _swa_valid(pl.program_id(0))
        kk = jnp.concatenate([kp_ref[...], kc_ref[...]], axis=0)
        vv = jnp.concatenate([vp_ref[...], vc_ref[...]], axis=0)
        for hkv in range(SWA_KV_HEADS):
            kv_cols = slice(hkv * SWA_DH, (hkv + 1) * SWA_DH)
            heads = range(hkv * SWA_GROUP, (hkv + 1) * SWA_GROUP)
            probs, _ = _swa_probs([q_ref[:, pl.ds(h * SWA_DH, SWA_DH)] for h in heads], kk[:, kv_cols],
                                  [sink_ref[h][:, :1] for h in heads], valid)
            outs = [_dot(p, vv[:, kv_cols]) for p in probs]
            for h, o in zip(heads, outs):
                o_ref[:, pl.ds(h * SWA_DH, SWA_DH)] = o

    return pl.pallas_call(
        body, name="swa_fwd", grid=(q.shape[0] // WINDOW,),
        out_shape=jax.ShapeDtypeStruct(q.shape, jnp.float32),
        in_specs=[qs, prev, cur, prev, cur, sk], out_specs=qs,
        compiler_params=pltpu.CompilerParams(dimension_semantics=("parallel",)),
    )(q, k, k, v, v, sinks)


def _swa_bwd_call(q, k, v, sinks, do):
    qs, cur, prev, sk = _swa_specs()

    def body(q_ref, kp_ref, kc_ref, vp_ref, vc_ref, sink_ref, do_ref, dq_ref, dkc_ref, dkp_ref, dvc_ref, dvp_ref, ds_ref):
        @pl.when(pl.program_id(0) == 0)
        def _():
            ds_ref[...] = jnp.zeros_like(ds_ref)

        valid = _swa_valid(pl.program_id(0))
        kk = jnp.concatenate([kp_ref[...], kc_ref[...]], axis=0)
        vv = jnp.concatenate([vp_ref[...], vc_ref[...]], axis=0)
        lane0 = lax.broadcasted_iota(jnp.int32, (1, 128), 1) == 0
        dk_heads, dv_heads = [], []
        for hkv in range(SWA_KV_HEADS):
            kv_cols = slice(hkv * SWA_DH, (hkv + 1) * SWA_DH)
            k_h, v_h = kk[:, kv_cols], vv[:, kv_cols]
            heads = range(hkv * SWA_GROUP, (hkv + 1) * SWA_GROUP)
            qs = [q_ref[:, pl.ds(h * SWA_DH, SWA_DH)] for h in heads]
            dos = [do_ref[:, pl.ds(h * SWA_DH, SWA_DH)] for h in heads]
            probs, p_sinks = _swa_probs(qs, k_h, [sink_ref[h][:, :1] for h in heads], valid)
            dps = [_dot(do_h, v_h, 1, 1) for do_h in dos]
            rss = [jnp.sum(p * dp, axis=-1, keepdims=True) for p, dp in zip(probs, dps)]
            d_ss = [p * (dp - rs) for p, dp, rs in zip(probs, dps, rss)]
            dqs = [_dot(d_s, k_h) * (SWA_DH ** -0.5) for d_s in d_ss]
            dks = [_dot(d_s, q_h, 0, 0) for d_s, q_h in zip(d_ss, qs)]
            dvs = [_dot(p, do_h, 0, 0) for p, do_h in zip(probs, dos)]
            for h, dq, p_sink, rs in zip(heads, dqs, p_sinks, rss):
                dq_ref[:, pl.ds(h * SWA_DH, SWA_DH)] = dq
                d_sink = -jnp.sum(p_sink * rs, axis=0, keepdims=True)
                ds_ref[h] += jnp.where(lane0, d_sink, 0.0)
            dk_heads.append(sum(dks[1:], dks[0]) * (SWA_DH ** -0.5))
            dv_heads.append(sum(dvs[1:], dvs[0]))
        dk = jnp.concatenate(dk_heads, axis=1)
        dv = jnp.concatenate(dv_heads, axis=1)
        dkp_ref[...] = dk[:WINDOW]
        dkc_ref[...] = dk[WINDOW:]
        dvp_ref[...] = dv[:WINDOW]
        dvc_ref[...] = dv[WINDOW:]

    sd = jax.ShapeDtypeStruct
    f32 = jnp.float32
    return pl.pallas_call(
        body, name="swa_bwd", grid=(q.shape[0] // WINDOW,),
        out_shape=(sd(q.shape, f32), sd(k.shape, f32), sd(k.shape, f32), sd(k.shape, f32), sd(k.shape, f32),
                   sd(sinks.shape, f32)),
        in_specs=[qs, prev, cur, prev, cur, sk, qs], out_specs=(qs, cur, cur, cur, cur, sk),
        compiler_params=pltpu.CompilerParams(dimension_semantics=("arbitrary",)),
    )(q, k, k, v, v, sinks, do)


@jax.custom_vjp
def _swa_op(q, k, v, sinks):
    return _swa_fwd_call(q, k, v, sinks)


def _swa_op_fwd(q, k, v, sinks):
    return _swa_fwd_call(q, k, v, sinks), (q, k, v, sinks)


def _swa_op_bwd(res, do):
    q, k, v, sinks = res
    dq, dkc, dkp, dvc, dvp, dsinks = _swa_bwd_call(q, k, v, sinks, do)

    def fold(cur, prev):
        return cur + jnp.concatenate([prev[WINDOW:], jnp.zeros_like(prev[:WINDOW])], axis=0)

    return dq, fold(dkc, dkp), fold(dvc, dvp), dsinks


_swa_op.defvjp(_swa_op_fwd, _swa_op_bwd)


def _swa_sink_attention(q, k, v, sinks):
    s = q.shape[1]
    sinks_b = jnp.broadcast_to(sinks[:, None, None], (SWA_HEADS, 1, 128))
    return _swa_op(q.reshape(s, SWA_Q_W), k.reshape(s, SWA_KV_W), v.reshape(s, SWA_KV_W), sinks_b)[None]


MEM_ROWS = 512


def _mem_probs(q_h, k_h):
    s = _dot(q_h, k_h, 1, 1) * (MEM_DH ** -0.5)
    p = jnp.exp(s - jnp.max(s, axis=-1, keepdims=True))
    return p / jnp.sum(p, axis=-1, keepdims=True)


def _mem_fwd_call(qm, kv):
    qs = pl.BlockSpec((MEM_ROWS, MEM_W), lambda i: (i, 0))
    kvs = pl.BlockSpec(kv.shape, lambda i: (0, 0))

    def body(q_ref, kv_ref, o_ref):
        for h in range(MEM_HEADS):
            cols = pl.ds(h * MEM_DH, MEM_DH)
            probs = _mem_probs(q_ref[:, cols], kv_ref[:, cols])
            o_ref[:, cols] = _dot(probs, kv_ref[:, pl.ds(MEM_W + h * MEM_DH, MEM_DH)])

    return pl.pallas_call(
        body, name="mem_fwd", grid=(qm.shape[0] // MEM_ROWS,),
        out_shape=jax.ShapeDtypeStruct(qm.shape, jnp.float32), in_specs=[qs, kvs], out_specs=qs,
        compiler_params=pltpu.CompilerParams(dimension_semantics=("parallel",)),
    )(qm, kv)


def _mem_bwd_call(qm, kv, do):
    qs = pl.BlockSpec((MEM_ROWS, MEM_W), lambda i: (i, 0))
    kvs = pl.BlockSpec(kv.shape, lambda i: (0, 0))

    def body(q_ref, kv_ref, do_ref, dq_ref, dkv_ref):
        @pl.when(pl.program_id(0) == 0)
        def _():
            dkv_ref[...] = jnp.zeros_like(dkv_ref)

        for h in range(MEM_HEADS):
            cols = pl.ds(h * MEM_DH, MEM_DH)
            v_cols = pl.ds(MEM_W + h * MEM_DH, MEM_DH)
            q_h, k_h, do_h = q_ref[:, cols], kv_ref[:, cols], do_ref[:, cols]
            probs = _mem_probs(q_h, k_h)
            dp = _dot(do_h, kv_ref[:, v_cols], 1, 1)
            d_s = probs * (dp - jnp.sum(probs * dp, axis=-1, keepdims=True))
            dq_ref[:, cols] = _dot(d_s, k_h) * (MEM_DH ** -0.5)
            dkv_ref[:, cols] += _dot(d_s, q_h, 0, 0) * (MEM_DH ** -0.5)
            dkv_ref[:, v_cols] += _dot(probs, do_h, 0, 0)

    sd = jax.ShapeDtypeStruct
    return pl.pallas_call(
        body, name="mem_bwd", grid=(qm.shape[0] // MEM_ROWS,),
        out_shape=(sd(qm.shape, jnp.float32), sd(kv.shape, jnp.float32)),
        in_specs=[qs, kvs, qs], out_specs=(qs, kvs),
        compiler_params=pltpu.CompilerParams(dimension_semantics=("arbitrary",)),
    )(qm, kv, do)


@jax.custom_vjp
def _mem_op(qm, kv):
    return _mem_fwd_call(qm, kv)


def _mem_op_fwd(qm, kv):
    return _mem_fwd_call(qm, kv), (qm, kv)


def _mem_op_bwd(res, do):
    return _mem_bwd_call(*res, do)


_mem_op.defvjp(_mem_op_fwd, _mem_op_bwd)


def _memory_attention(qm, kv):
    return _mem_op(qm[0], kv[0])[None]


def _mixer_a(h, h_lo, mem, mem_lo, p, s, layer):
    B, S, _ = h.shape
    proj = _project(h, h_lo, p["a_w_in"][layer], s["a_w_in"][layer])
    c1 = 2 * DN_QK_W + DN_V_W
    qkv = proj[..., :c1]
    z = proj[..., c1:QKVZ_W]
    qm = proj[..., QKVZ_W:QKVZ_W + MEM_W]
    a = proj[..., QKVZ_W + MEM_W:QKVZ_W + MEM_W + DN_HEADS]
    b = proj[..., QKVZ_W + MEM_W + DN_HEADS:QKVZ_W + MEM_W + 2 * DN_HEADS]
    planes = _pre_op(qkv[0], p["a_conv_w"][layer])
    beta = jax.nn.sigmoid(b[0])
    g = -jnp.exp(p["a_A_log"][layer]) * jax.nn.softplus(a[0] + p["a_dt_bias"][layer])
    o = _gated_delta_rule(planes, g, beta).reshape(B, S, DN_HEADS, DN_DV)
    o = o * lax.rsqrt(jnp.mean(o * o, axis=-1, keepdims=True) + NORM_EPS) * p["a_norm_w"][layer]
    o = o * jax.nn.silu(z.reshape(B, S, DN_HEADS, DN_DV))
    o = o.reshape(B, S, DN_V_W)
    kv = _project(mem, mem_lo, p["mem_w_kv"][layer], s["mem_w_kv"][layer])
    mo = _memory_attention(qm, kv)
    cat = jnp.concatenate([o, mo], axis=-1)
    return _project(cat, _lo(cat), p["w_o"][layer], s["w_o"][layer])


def _mixer_b(h, h_lo, mem, mem_lo, k_sh, v_sh, cos, sin, p, s, layer):
    B, S, _ = h.shape
    j = layer - N_A
    proj = _project(h, h_lo, p["b_w_in"][j], s["b_w_in"][j])
    q = _apply_rope(proj[..., :SWA_Q_W].reshape(B, S, SWA_HEADS, SWA_DH), cos, sin)
    q = q.reshape(B, S, SWA_KV_HEADS, SWA_GROUP, SWA_DH)
    o = _swa_sink_attention(q, k_sh, v_sh, p["b_sinks"][j])
    kv = _project(mem, mem_lo, p["mem_w_kv"][layer], s["mem_w_kv"][layer])
    mo = _memory_attention(proj[..., SWA_Q_W:], kv)
    cat = jnp.concatenate([o, mo], axis=-1)
    return _project(cat, _lo(cat), p["w_o"][layer], s["w_o"][layer])


def _forward(p, s, x, mem, positions):
    cos, sin = _rope_tables(positions, SWA_DH)
    h, h_lo, mem_lo = x, _lo(x), _lo(mem)
    k_sh = v_sh = None
    for layer in range(DEPTH):
        if layer < N_A:
            mix = _mixer_a(h, h_lo, mem, mem_lo, p, s, layer)
        else:
            mix = _mixer_b(h, h_lo, mem, mem_lo, k_sh, v_sh, cos, sin, p, s, layer)
        seq = h.shape[1]
        h2, h2_lo = _ln_res(h[0], mix[0], p["ln_g"][layer, 0][None], p["ln_b"][layer, 0][None])
        down = _mlp(h2, h2_lo, p["mlp_w_up"][layer], p["mlp_w_down"][layer], s["mlp_w_up"][layer],
                    s["mlp_w_down"][layer])
        h, h_lo = _ln_res(h2, down, p["ln_g"][layer, 1][None], p["ln_b"][layer, 1][None])
        h, h_lo = h.reshape(1, seq, D_MODEL), h_lo.reshape(1, seq, D_MODEL)
        if layer == N_A - 1:
            B, S, _ = h.shape
            kv = _project(h, h_lo, p["w_kv_shared"], s["w_kv_shared"])
            k_sh = _apply_rope(kv[..., :SWA_KV_W].reshape(B, S, SWA_KV_HEADS, SWA_DH), cos, sin)
            v_sh = kv[..., SWA_KV_W:].reshape(B, S, SWA_KV_HEADS, SWA_DH)
    return h


def _loss(diff, s, p, mem, positions, target):
    y = _forward({**p, **diff["small"]}, s, diff["x"], mem, positions)
    return 0.5 * jnp.sum(jnp.mean(jnp.square(y - target), axis=-1))


def _reorder_a_w_in(w):
    pad = jnp.zeros(w.shape[:-1] + (A_IN_PAD - A_IN,), w.dtype)
    return jnp.concatenate([w[..., :QKVZ_W], w[..., QKVZ_W + 2 * DN_HEADS:], w[..., QKVZ_W:QKVZ_W + 2 * DN_HEADS], pad],
                           axis=-1)


def _restore_a_w_in(w):
    return jnp.concatenate([w[..., :QKVZ_W], w[..., QKVZ_W + MEM_W:QKVZ_W + MEM_W + 2 * DN_HEADS],
                            w[..., QKVZ_W:QKVZ_W + MEM_W]], axis=-1)


def kernel(x, mem, positions, a_w_in, a_conv_w, a_A_log, a_dt_bias, a_norm_w, b_w_in, b_sinks, w_kv_shared, mem_w_kv, w_o, mlp_w_up, mlp_w_down, ln_g, ln_b, loss_target, m_a_w_in, m_a_conv_w, m_a_A_log, m_a_dt_bias, m_a_norm_w, m_b_w_in, m_b_sinks, m_w_kv_shared, m_mem_w_kv, m_w_o, m_mlp_w_up, m_mlp_w_down, m_ln_g, m_ln_b, v_a_w_in, v_a_conv_w, v_a_A_log, v_a_dt_bias, v_a_norm_w, v_b_w_in, v_b_sinks, v_w_kv_shared, v_mem_w_kv, v_w_o, v_mlp_w_up, v_mlp_w_down, v_ln_g, v_ln_b):
    w_sh = dict(a_w_in=a_w_in, a_conv_w=a_conv_w, a_A_log=a_A_log, a_dt_bias=a_dt_bias, a_norm_w=a_norm_w,
                b_w_in=b_w_in, b_sinks=b_sinks, w_kv_shared=w_kv_shared, mem_w_kv=mem_w_kv, w_o=w_o,
                mlp_w_up=mlp_w_up, mlp_w_down=mlp_w_down, ln_g=ln_g, ln_b=ln_b)
    m_sh = dict(a_w_in=m_a_w_in, a_conv_w=m_a_conv_w, a_A_log=m_a_A_log, a_dt_bias=m_a_dt_bias, a_norm_w=m_a_norm_w,
                b_w_in=m_b_w_in, b_sinks=m_b_sinks, w_kv_shared=m_w_kv_shared, mem_w_kv=m_mem_w_kv, w_o=m_w_o,
                mlp_w_up=m_mlp_w_up, mlp_w_down=m_mlp_w_down, ln_g=m_ln_g, ln_b=m_ln_b)
    v_sh = dict(a_w_in=v_a_w_in, a_conv_w=v_a_conv_w, a_A_log=v_a_A_log, a_dt_bias=v_a_dt_bias, a_norm_w=v_a_norm_w,
                b_w_in=v_b_w_in, b_sinks=v_b_sinks, w_kv_shared=v_w_kv_shared, mem_w_kv=v_mem_w_kv, w_o=v_w_o,
                mlp_w_up=v_mlp_w_up, mlp_w_down=v_mlp_w_down, ln_g=v_ln_g, ln_b=v_ln_b)
    shard_shapes = {n: w_sh[n].shape for n in WEIGHTS}
    rb, rows = _rows_for(w_sh)

    big, small = _pack(w_sh, rb, jnp.bfloat16)
    gbig, gsmall = _gather_weights(big.reshape(2, rb // 2, FLAT_W), small.reshape(2, SMALL_ROWS // 2, FLAT_W))
    gbig, gsmall = gbig.reshape(N_CHIPS, rb, FLAT_W), gsmall.reshape(N_CHIPS, SMALL_ROWS, FLAT_W)
    pieces = [_unpack(gbig[q], gsmall[q], shard_shapes) for q in range(N_CHIPS)]
    full = {n: jnp.concatenate([pieces[q][n] for q in range(N_CHIPS)], axis=SHARD_AXIS[n]) for n in SHARD_AXIS}
    for n in REPLICATED:
        full[n] = w_sh[n]
    big_w = {n: full[n] for n in BIG}
    big_w["a_w_in"] = _reorder_a_w_in(big_w["a_w_in"])
    small_w = {n: full[n] for n in SMALL}
    slots = {n: jnp.zeros(big_w[n].shape, jnp.float32) for n in BIG}

    loss, (grads, g_slots) = jax.value_and_grad(_loss, argnums=(0, 1))(
        {"x": x, "small": small_w}, slots, big_w, mem, positions, loss_target)
    loss = lax.psum(loss, ("x", "y", "c"))
    g_full = {**g_slots, **grads["small"]}
    g_full["a_w_in"] = _restore_a_w_in(g_full["a_w_in"])

    def shard_of(n, q):
        if n in REPLICATED:
            return g_full[n]
        size = shard_shapes[n][SHARD_AXIS[n]]
        return lax.slice_in_dim(g_full[n], q * size, (q + 1) * size, axis=SHARD_AXIS[n])

    parts = []
    for q in range(N_CHIPS):
        pb, ps = _pack({n: shard_of(n, q) for n in WEIGHTS}, rb, jnp.bfloat16)
        parts.append(jnp.concatenate([pb, ps.astype(jnp.bfloat16)], axis=0).reshape(2, rows // 2, FLAT_W))
    partials = jnp.stack(parts, axis=1)
    half = lax.axis_index("c").astype(jnp.int32).reshape(1)
    chip_partials = _add_pairs(partials, _swap_halves(partials), half)
    g_flat = _join_halves(_sum_chips(_scatter_grads(chip_partials))).reshape(rows, FLAT_W)

    flat = [jnp.concatenate(_pack(d, rb), axis=0) for d in (w_sh, m_sh, v_sh)]
    outs = (g_flat,) + tuple(_adamw(g_flat, *flat))
    g_o, d_o, m_o, v_o = [_unpack(o[:rb], o[rb:], shard_shapes) for o in outs]
    return (loss, grads["x"], *[g_o[n] for n in WEIGHTS], *[d_o[n] for n in WEIGHTS],
            *[m_o[n] for n in WEIGHTS], *[v_o[n] for n in WEIGHTS])
```

```python
import functools
import math

import jax
import jax.numpy as jnp
from jax import lax
from jax.experimental import pallas as pl
from jax.experimental.pallas import tpu as pltpu

D_MODEL = 1024
DEPTH = 4
N_A = DEPTH // 2
N_B = DEPTH - N_A
MEM_HEADS = 4
MEM_DH = D_MODEL // 16
MEM_W = MEM_HEADS * MEM_DH
DN_DK = 128
DN_DV = 128
DN_HEADS = (3 * D_MODEL) // (4 * DN_DV)
DN_QK_W = DN_HEADS * DN_DK
DN_V_W = DN_HEADS * DN_DV
CONV_WIDTH = 4
CHUNK = 64
SWA_DH = 64
SWA_HEADS = (3 * D_MODEL) // (4 * SWA_DH)
SWA_KV_HEADS = 2
SWA_GROUP = SWA_HEADS // SWA_KV_HEADS
SWA_Q_W = SWA_HEADS * SWA_DH
SWA_KV_W = SWA_KV_HEADS * SWA_DH
WINDOW = 128
ROPE_THETA = 10000.0
MLP_HIDDEN = 4 * D_MODEL
LN_EPS = 1e-5
NORM_EPS = 1e-6
DN_ALPHA = (2.0 * DEPTH) ** 0.25
A_IN = 2 * DN_QK_W + 2 * DN_V_W + 2 * DN_HEADS + MEM_W
A_IN_PAD = 3456
QKVZ_W = 2 * DN_QK_W + 2 * DN_V_W

ADAM_LR = 0.001
ADAM_B1 = 0.9
ADAM_B2 = 0.999
ADAM_EPS = 1e-08
ADAM_WD = 0.01
ADAM_STEP = 10

N_CHIPS = 4
FLAT_W = 1024
BIG = ("a_w_in", "b_w_in", "w_kv_shared", "mem_w_kv", "w_o", "mlp_w_up", "mlp_w_down")
SMALL = ("a_conv_w", "ln_g", "ln_b", "a_A_log", "a_dt_bias", "a_norm_w", "b_sinks")
REPLICATED = ("a_A_log", "a_dt_bias", "a_norm_w", "b_sinks")
WEIGHTS = ("a_w_in", "a_conv_w", "a_A_log", "a_dt_bias", "a_norm_w", "b_w_in", "b_sinks", "w_kv_shared",
           "mem_w_kv", "w_o", "mlp_w_up", "mlp_w_down", "ln_g", "ln_b")
SHARD_AXIS = {"a_w_in": 2, "a_conv_w": 2, "b_w_in": 1, "w_kv_shared": 0, "mem_w_kv": 1, "w_o": 1,
              "mlp_w_up": 2, "mlp_w_down": 1, "ln_g": 2, "ln_b": 2}
SMALL_ROWS = 32
ROW_ALIGN = 256

MESH = pl.DeviceIdType.MESH
HBM_SPEC = pl.BlockSpec(memory_space=pltpu.HBM)
VMEM_LIMIT = 48 * 1024 * 1024


def _rows_for(shards):
    n_big = sum(math.prod(shards[n].shape) for n in BIG)
    n_small = sum(math.prod(shards[n].shape) for n in SMALL)
    assert n_small <= SMALL_ROWS * FLAT_W
    total = -(-n_big // FLAT_W) + SMALL_ROWS
    total = -(-total // (2 * ROW_ALIGN)) * (2 * ROW_ALIGN)
    return total - SMALL_ROWS, total


def _pack(shards, rb, dtype_big=jnp.float32):
    big = jnp.concatenate([shards[n].reshape(-1).astype(dtype_big) for n in BIG])
    big = jnp.pad(big, (0, rb * FLAT_W - big.shape[0])).reshape(rb, FLAT_W)
    small = jnp.concatenate([shards[n].reshape(-1).astype(jnp.float32) for n in SMALL])
    small = jnp.pad(small, (0, SMALL_ROWS * FLAT_W - small.shape[0])).reshape(SMALL_ROWS, FLAT_W)
    return big, small


def _unpack(big, small, shapes):
    out = {}
    for flat, names in ((big.reshape(-1), BIG), (small.reshape(-1), SMALL)):
        off = 0
        for n in names:
            size = math.prod(shapes[n])
            out[n] = flat[off:off + size].reshape(shapes[n])
            off += size
    return out


def _other_chips(x, y):
    return [(1 - x, y), (x, 1 - y), (1 - x, 1 - y)]


def _gather_weights(big, small):
    def body(big_ref, small_ref, obig_ref, osmall_ref, send_sems, recv_sems, pass_send_sems, pass_recv_sems, local_sems):
        x, y, c = lax.axis_index("x"), lax.axis_index("y"), lax.axis_index("c")
        me = 2 * x + y
        sibling = (x, y, 1 - c)
        pairs = ((big_ref, obig_ref), (small_ref, osmall_ref))
        local = [pltpu.make_async_copy(src, dst.at[me], local_sems.at[i]) for i, (src, dst) in enumerate(pairs)]
        for cp in local:
            cp.start()
        sends = []
        for j, (px, py) in enumerate(_other_chips(x, y)):
            for i, (src, dst) in enumerate(pairs):
                sends.append(pltpu.make_async_remote_copy(
                    src_ref=src.at[c], dst_ref=dst.at[me, c], send_sem=send_sems.at[2 * j + i],
                    recv_sem=recv_sems.at[2 * j + i], device_id=(px, py, c), device_id_type=MESH))
        for cp in sends:
            cp.start()
        passed = []
        for j, (px, py) in enumerate(_other_chips(x, y)):
            for i, (src, dst) in enumerate(pairs):
                landed = dst.at[2 * px + py, c]
                pltpu.make_async_remote_copy(
                    src_ref=src.at[c], dst_ref=landed, send_sem=send_sems.at[2 * j + i],
                    recv_sem=recv_sems.at[2 * j + i], device_id=(px, py, c), device_id_type=MESH).wait_recv()
                passed.append(pltpu.make_async_remote_copy(
                    src_ref=landed, dst_ref=landed, send_sem=pass_send_sems.at[2 * j + i],
                    recv_sem=pass_recv_sems.at[2 * j + i], device_id=sibling, device_id_type=MESH))
                passed[-1].start()
        for j, (px, py) in enumerate(_other_chips(x, y)):
            for i, (src, dst) in enumerate(pairs):
                other_half = dst.at[2 * px + py, 1 - c]
                pltpu.make_async_remote_copy(
                    src_ref=other_half, dst_ref=other_half, send_sem=pass_send_sems.at[2 * j + i],
                    recv_sem=pass_recv_sems.at[2 * j + i], device_id=sibling, device_id_type=MESH).wait_recv()
        for cp in sends + passed:
            cp.wait_send()
        for cp in local:
            cp.wait()

    dma6 = pltpu.SemaphoreType.DMA((6,))
    return pl.pallas_call(
        body, name="gather_weights",
        out_shape=(jax.ShapeDtypeStruct((N_CHIPS,) + big.shape, big.dtype),
                   jax.ShapeDtypeStruct((N_CHIPS,) + small.shape, small.dtype)),
        in_specs=[HBM_SPEC, HBM_SPEC], out_specs=(HBM_SPEC, HBM_SPEC),
        scratch_shapes=[dma6, dma6, dma6, dma6, pltpu.SemaphoreType.DMA((2,))],
    )(big, small)


def _scatter_grads(g):
    def body(g_ref, o_ref, send_sems, recv_sems, local_sem):
        x, y, c = lax.axis_index("x"), lax.axis_index("y"), lax.axis_index("c")
        me = 2 * x + y
        local = pltpu.make_async_copy(g_ref.at[me], o_ref.at[me], local_sem)
        local.start()
        sends = []
        for j, (px, py) in enumerate(_other_chips(x, y)):
            sends.append(pltpu.make_async_remote_copy(
                src_ref=g_ref.at[2 * px + py], dst_ref=o_ref.at[me], send_sem=send_sems.at[j], recv_sem=recv_sems.at[j],
                device_id=(px, py, c), device_id_type=MESH))
        for cp in sends:
            cp.start()
        for j, (px, py) in enumerate(_other_chips(x, y)):
            pltpu.make_async_remote_copy(
                src_ref=g_ref.at[me], dst_ref=o_ref.at[2 * px + py], send_sem=send_sems.at[j], recv_sem=recv_sems.at[j],
                device_id=(px, py, c), device_id_type=MESH).wait_recv()
        for cp in sends:
            cp.wait_send()
        local.wait()

    return pl.pallas_call(
        body, name="scatter_grads",
        out_shape=jax.ShapeDtypeStruct(g.shape, g.dtype),
        in_specs=[HBM_SPEC], out_specs=HBM_SPEC,
        scratch_shapes=[pltpu.SemaphoreType.DMA((3,)), pltpu.SemaphoreType.DMA((3,)), pltpu.SemaphoreType.DMA],
    )(g)


def _swap_halves(g):
    def body(g_ref, o_ref, send_sem, recv_sem):
        x, y, c = lax.axis_index("x"), lax.axis_index("y"), lax.axis_index("c")
        cp = pltpu.make_async_remote_copy(src_ref=g_ref.at[1 - c], dst_ref=o_ref, send_sem=send_sem, recv_sem=recv_sem,
                                          device_id=(x, y, 1 - c), device_id_type=MESH)
        cp.start()
        cp.wait()

    return pl.pallas_call(
        body, name="swap_halves",
        out_shape=jax.ShapeDtypeStruct(g.shape[1:], g.dtype),
        in_specs=[HBM_SPEC], out_specs=HBM_SPEC,
        scratch_shapes=[pltpu.SemaphoreType.DMA, pltpu.SemaphoreType.DMA],
    )(g)


def _join_halves(v):
    def body(v_ref, o_ref, send_sem, recv_sem, local_sem):
        x, y, c = lax.axis_index("x"), lax.axis_index("y"), lax.axis_index("c")
        local = pltpu.make_async_copy(v_ref, o_ref.at[c], local_sem)
        local.start()
        cp = pltpu.make_async_remote_copy(src_ref=v_ref, dst_ref=o_ref.at[c], send_sem=send_sem, recv_sem=recv_sem,
                                          device_id=(x, y, 1 - c), device_id_type=MESH)
        cp.start()
        cp.wait_send()
        pltpu.make_async_remote_copy(src_ref=v_ref, dst_ref=o_ref.at[1 - c], send_sem=send_sem, recv_sem=recv_sem,
                                     device_id=(x, y, 1 - c), device_id_type=MESH).wait_recv()
        local.wait()

    return pl.pallas_call(
        body, name="join_halves",
        out_shape=jax.ShapeDtypeStruct((2,) + v.shape, v.dtype),
        in_specs=[HBM_SPEC], out_specs=HBM_SPEC,
        scratch_shapes=[pltpu.SemaphoreType.DMA, pltpu.SemaphoreType.DMA, pltpu.SemaphoreType.DMA],
    )(v)


def _add_pairs(g, theirs, half):
    _, n, rows, width = g.shape
    assert rows % ROW_ALIGN == 0, rows

    def body(half_ref, g_ref, t_ref, o_ref):
        o_ref[...] = (g_ref[...].astype(jnp.float32) + t_ref[...].astype(jnp.float32)).astype(o_ref.dtype)

    blk = pl.BlockSpec((None, ROW_ALIGN, width), lambda p, i, h: (p, i, 0))
    grid_spec = pltpu.PrefetchScalarGridSpec(
        num_scalar_prefetch=1, grid=(n, rows // ROW_ALIGN),
        in_specs=[pl.BlockSpec((None, None, ROW_ALIGN, width), lambda p, i, h: (h[0], p, i, 0)), blk], out_specs=blk)
    return pl.pallas_call(
        body, name="add_pairs", grid_spec=grid_spec, out_shape=jax.ShapeDtypeStruct(theirs.shape, g.dtype),
        compiler_params=pltpu.CompilerParams(dimension_semantics=("parallel", "parallel")),
    )(half, g, theirs)


def _sum_chips(parts):
    n, rows, width = parts.shape
    assert rows % ROW_ALIGN == 0, rows

    def body(p_ref, o_ref):
        p = [p_ref[q].astype(jnp.float32) for q in range(n)]
        o_ref[...] = (p[0] + p[1]) + (p[2] + p[3])

    return pl.pallas_call(
        body, name="sum_chips", grid=(rows // ROW_ALIGN,),
        out_shape=jax.ShapeDtypeStruct((rows, width), jnp.float32),
        in_specs=[pl.BlockSpec((n, ROW_ALIGN, width), lambda i: (0, i, 0))],
        out_specs=pl.BlockSpec((ROW_ALIGN, width), lambda i: (i, 0)),
        compiler_params=pltpu.CompilerParams(dimension_semantics=("parallel",), vmem_limit_bytes=VMEM_LIMIT),
    )(parts)


def _adamw(g, w, m, v):
    rows, width = w.shape
    blk = ROW_ALIGN // 2

    def body(g_ref, w_ref, m_ref, v_ref, d_out, m_out, v_out):
        g = g_ref[...]
        m_new = ADAM_B1 * m_ref[...] + (1.0 - ADAM_B1) * g
        v_new = ADAM_B2 * v_ref[...] + (1.0 - ADAM_B2) * jnp.square(g)
        m_hat = m_new / (1.0 - ADAM_B1 ** ADAM_STEP)
        v_hat = v_new / (1.0 - ADAM_B2 ** ADAM_STEP)
        d_out[...] = -ADAM_LR * (m_hat / (jnp.sqrt(v_hat) + ADAM_EPS) + ADAM_WD * w_ref[...])
        m_out[...] = m_new
        v_out[...] = v_new

    spec = pl.BlockSpec((blk, width), lambda i: (i, 0))
    shape = jax.ShapeDtypeStruct((rows, width), jnp.float32)
    return pl.pallas_call(
        body, name="adamw", grid=(rows // blk,),
        out_shape=(shape,) * 3, in_specs=[spec] * 4, out_specs=(spec,) * 3,
        compiler_params=pltpu.CompilerParams(dimension_semantics=("parallel",), vmem_limit_bytes=VMEM_LIMIT),
    )(g, w, m, v)


def _tile(dim, pref):
    if dim <= pref:
        return dim
    for t in range(pref - pref % 128, 0, -128):
        if dim % t == 0:
            return t
    raise ValueError(f"no 128-aligned tile for {dim}")


def _matmul(a, b, *, ta=False, tb=False, name, epilogue=None, extra=None, out_dtype=jnp.float32):
    (k_a, m) = a.shape if ta else a.shape[::-1]
    (k_b, n) = b.shape[::-1] if tb else b.shape
    assert k_a == k_b, (a.shape, b.shape, ta, tb)
    k = k_a
    tm, tn, tk = _tile(m, 1024), _tile(n, 512), _tile(k, 1024)
    nk = k // tk
    a_spec = pl.BlockSpec((tk, tm), lambda i, j, l: (l, i)) if ta else pl.BlockSpec((tm, tk), lambda i, j, l: (i, l))
    b_spec = pl.BlockSpec((tn, tk), lambda i, j, l: (j, l)) if tb else pl.BlockSpec((tk, tn), lambda i, j, l: (l, j))
    o_spec = pl.BlockSpec((tm, tn), lambda i, j, l: (i, j))
    dims = (((0 if ta else 1,), (1 if tb else 0,)), ((), ()))
    has_extra = epilogue == "relu2_grad"
    assert has_extra == (extra is not None)

    def body(*refs):
        a_ref, b_ref = refs[:2]
        outs = refs[2 + has_extra:2 + has_extra + (2 if epilogue == "relu2" else 1)]
        l = pl.program_id(2)
        part = lax.dot_general(a_ref[...].astype(jnp.bfloat16), b_ref[...].astype(jnp.bfloat16), dims,
                               preferred_element_type=jnp.float32)

        def finish(acc):
            if epilogue is None:
                outs[0][...] = acc.astype(out_dtype)
            elif epilogue == "relu2":
                outs[0][...] = acc.astype(jnp.bfloat16)
                outs[1][...] = jnp.square(jnp.maximum(acc, 0.0)).astype(jnp.bfloat16)
            else:
                outs[0][...] = (acc * (2.0 * jnp.maximum(refs[2][...].astype(jnp.float32), 0.0))).astype(out_dtype)

        if nk == 1:
            finish(part)
            return
        acc_ref = refs[-1]

        @pl.when(l == 0)
        def _():
            acc_ref[...] = part

        @pl.when((l > 0) & (l < nk - 1))
        def _():
            acc_ref[...] += part

        @pl.when(l == nk - 1)
        def _():
            finish(acc_ref[...] + part)

    if epilogue == "relu2":
        out_shape = (jax.ShapeDtypeStruct((m, n), jnp.bfloat16),) * 2
        out_specs = (o_spec, o_spec)
    else:
        out_shape = jax.ShapeDtypeStruct((m, n), out_dtype)
        out_specs = o_spec
    return pl.pallas_call(
        body, name=name, grid=(m // tm, n // tn, nk), out_shape=out_shape,
        in_specs=[a_spec, b_spec] + ([o_spec] if has_extra else []), out_specs=out_specs,
        scratch_shapes=[pltpu.VMEM((tm, tn), jnp.float32)] if nk > 1 else [],
        compiler_params=pltpu.CompilerParams(dimension_semantics=("parallel", "parallel", "arbitrary"),
                                             vmem_limit_bytes=VMEM_LIMIT),
    )(*((a, b) + ((extra,) if has_extra else ())))


def _lo(x):
    return lax.stop_gradient(x.astype(jnp.bfloat16))


@jax.custom_vjp
def _linear(x, x_lo, w, slot):
    del x, slot
    return _matmul(x_lo, w, name="linear_fwd")


def _linear_fwd(x, x_lo, w, slot):
    del x, slot
    return _matmul(x_lo, w, name="linear_fwd"), (x_lo, w)


def _linear_bwd(res, dy):
    x_lo, w = res
    dx = _matmul(dy, w, tb=True, name="linear_dx")
    dw = _matmul(x_lo, dy, ta=True, name="linear_dw")
    return dx, jnp.zeros_like(x_lo), jnp.zeros_like(w), dw


_linear.defvjp(_linear_fwd, _linear_bwd)


@jax.custom_vjp
def _mlp(h, h_lo, w_up, w_down, slot_up, slot_down):
    return _mlp_fwd(h, h_lo, w_up, w_down, slot_up, slot_down)[0]


def _mlp_fwd(h, h_lo, w_up, w_down, slot_up, slot_down):
    del h, slot_up, slot_down
    up, act = _matmul(h_lo, w_up, name="mlp_up", epilogue="relu2")
    return _matmul(act, w_down, name="mlp_down"), (h_lo, up, act, w_up, w_down)


def _mlp_bwd(res, dy):
    h_lo, up, act, w_up, w_down = res
    d_up = _matmul(dy, w_down, tb=True, name="mlp_d_up", epilogue="relu2_grad", extra=up, out_dtype=jnp.bfloat16)
    dw_down = _matmul(act, dy, ta=True, name="mlp_dw_down")
    dw_up = _matmul(h_lo, d_up, ta=True, name="mlp_dw_up")
    dh = _matmul(d_up, w_up, tb=True, name="mlp_dh")
    return dh, jnp.zeros_like(h_lo), jnp.zeros_like(w_up), jnp.zeros_like(w_down), dw_up, dw_down


_mlp.defvjp(_mlp_fwd, _mlp_bwd)


LN_ROWS = 256


def _ln_call(h, mix, g, b):
    s, d = h.shape
    tok = pl.BlockSpec((LN_ROWS, d), lambda i: (i, 0))
    vec = pl.BlockSpec((1, d), lambda i: (0, 0))
    stat = pl.BlockSpec((LN_ROWS, 1), lambda i: (i, 0))

    def body(h_ref, mix_ref, g_ref, b_ref, y_ref, ylo_ref, xhat_ref, rstd_ref):
        z = DN_ALPHA * h_ref[...] + mix_ref[...]
        mu = jnp.mean(z, axis=-1, keepdims=True)
        zc = z - mu
        rstd = lax.rsqrt(jnp.mean(jnp.square(zc), axis=-1, keepdims=True) + LN_EPS)
        xhat = zc * rstd
        y = xhat * g_ref[...] + b_ref[...]
        y_ref[...] = y
        ylo_ref[...] = y.astype(ylo_ref.dtype)
        xhat_ref[...] = xhat
        rstd_ref[...] = rstd

    sd = jax.ShapeDtypeStruct
    return pl.pallas_call(
        body, name="ln_fwd", grid=(s // LN_ROWS,),
        out_shape=(sd((s, d), jnp.float32), sd((s, d), jnp.bfloat16), sd((s, d), jnp.float32), sd((s, 1), jnp.float32)),
        in_specs=[tok, tok, vec, vec], out_specs=(tok, tok, tok, stat),
        compiler_params=pltpu.CompilerParams(dimension_semantics=("parallel",)),
    )(h, mix, g, b)


def _ln_grad_call(dy, xhat, rstd, g):
    s, d = dy.shape
    tok = pl.BlockSpec((LN_ROWS, d), lambda i: (i, 0))
    vec = pl.BlockSpec((1, d), lambda i: (0, 0))
    stat = pl.BlockSpec((LN_ROWS, 1), lambda i: (i, 0))

    def body(dy_ref, xhat_ref, rstd_ref, g_ref, dz_ref, dg_ref, db_ref):
        @pl.when(pl.program_id(0) == 0)
        def _():
            dg_ref[...] = jnp.zeros_like(dg_ref)
            db_ref[...] = jnp.zeros_like(db_ref)

        dy, xhat = dy_ref[...], xhat_ref[...]
        dyg = dy * g_ref[...]
        m1 = jnp.mean(dyg, axis=-1, keepdims=True)
        m2 = jnp.mean(dyg * xhat, axis=-1, keepdims=True)
        dz_ref[...] = rstd_ref[...] * (dyg - m1 - xhat * m2)
        dg_ref[...] += jnp.sum(dy * xhat, axis=0, keepdims=True)
        db_ref[...] += jnp.sum(dy, axis=0, keepdims=True)

    sd = jax.ShapeDtypeStruct
    return pl.pallas_call(
        body, name="ln_bwd", grid=(s // LN_ROWS,),
        out_shape=(sd((s, d), jnp.float32), sd((1, d), jnp.float32), sd((1, d), jnp.float32)),
        in_specs=[tok, tok, stat, vec], out_specs=(tok, vec, vec),
        compiler_params=pltpu.CompilerParams(dimension_semantics=("arbitrary",)),
    )(dy, xhat, rstd, g)


@jax.custom_vjp
def _ln_res(h, mix, g, b):
    return _ln_call(h, mix, g, b)[:2]


def _ln_res_fwd(h, mix, g, b):
    y, y_lo, xhat, rstd = _ln_call(h, mix, g, b)
    return (y, y_lo), (xhat, rstd, g)


def _ln_res_bwd(res, cts):
    xhat, rstd, g = res
    dz, dg, db = _ln_grad_call(cts[0], xhat, rstd, g)
    return DN_ALPHA * dz, dz, dg, db


_ln_res.defvjp(_ln_res_fwd, _ln_res_bwd)


MXU_DTYPE = jnp.bfloat16
DN_CB = 8
DN_GROUP = 4
DN_SCALE = DN_DK ** -0.5


def _dot(a, b, ca=1, cb=0):
    return lax.dot_general(a.astype(MXU_DTYPE), b.astype(MXU_DTYPE), (((ca,), (cb,)), ((), ())),
                           preferred_element_type=jnp.float32)


def _chunk_masks():
    row = lax.broadcasted_iota(jnp.int32, (CHUNK, CHUNK), 0)
    col = lax.broadcasted_iota(jnp.int32, (CHUNK, CHUNK), 1)
    return row >= col, row > col, row == col


def _to_col(row_vec):
    _, _, eye = _chunk_masks()
    return jnp.sum(jnp.where(eye, jnp.broadcast_to(row_vec, (CHUNK, CHUNK)), 0.0), axis=1, keepdims=True)


def _to_row(col_vec):
    _, _, eye = _chunk_masks()
    return jnp.sum(jnp.where(eye, jnp.broadcast_to(col_vec, (CHUNK, CHUNK)), 0.0), axis=0, keepdims=True)


def _last_row(col_vec):
    last = lax.broadcasted_iota(jnp.int32, (CHUNK, 1), 0) == CHUNK - 1
    return jnp.sum(jnp.where(last, col_vec, 0.0), axis=0, keepdims=True), last


def _chunk_terms(q, k, beta, gcc, gcr):
    incl, strict, _ = _chunk_masks()
    decay = jnp.where(incl, jnp.exp(jnp.minimum(gcc - gcr, 0.0)), 0.0)
    kb = k * beta
    lmat = jnp.where(strict, _dot(kb, k, 1, 1) * decay, 0.0)
    intra = jnp.where(incl, _dot(q, k, 1, 1) * decay, 0.0)
    return decay, kb, lmat, intra


def _dot3(a, b, ca=1, cb=0):
    if MXU_DTYPE == jnp.float32:
        return _dot(a, b, ca, cb)
    a_hi, b_hi = a.astype(MXU_DTYPE), b.astype(MXU_DTYPE)
    a_lo = (a - a_hi.astype(jnp.float32)).astype(MXU_DTYPE)
    b_lo = (b - b_hi.astype(jnp.float32)).astype(MXU_DTYPE)
    return _dot(a_hi, b_hi, ca, cb) + (_dot(a_hi, b_lo, ca, cb) + _dot(a_lo, b_hi, ca, cb))


def _unit_lower_inverse(lmats):
    _, _, eye = _chunk_masks()
    ident = jnp.where(eye, 1.0, 0.0)
    ts = [ident - m for m in lmats]
    ps = [_dot(m, m) for m in lmats]
    for _ in range(4):
        ts = [t + _dot(t, p) for t, p in zip(ts, ps)]
        ps = [_dot(p, p) for p in ps]
    ts = [t + _dot(t, p) for t, p in zip(ts, ps)]
    resids = [(t - ident) + _dot3(m, t) for m, t in zip(lmats, ts)]
    return [t - _dot(t, r) for t, r in zip(ts, resids)]


def _dn_specs(n_chunks):
    tok = pl.BlockSpec((DN_CB * CHUNK, DN_DK), lambda h, n: (n, h))
    rowv = pl.BlockSpec((None, DN_CB, CHUNK), lambda h, n: (h, n, 0))
    sq = pl.BlockSpec((None, DN_CB, CHUNK, CHUNK), lambda h, n: (h, n, 0, 0))
    lane = pl.BlockSpec((None, DN_CB, 1, DN_DV), lambda h, n: (h, n, 0, 0))
    planes = [pl.BlockSpec((None, DN_CB * CHUNK, DN_DK), functools.partial(lambda h, n, p: (p, n, h), p=p))
              for p in range(3)]
    return tok, rowv, sq, lane, planes


def _dn_prep(qkv, beta, gc):
    s = qkv.shape[1]
    n_chunks = s // CHUNK
    tok, rowv, sq, lane, planes = _dn_specs(n_chunks)
    tok_shape = qkv.shape[1:]

    def body(q_ref, k_ref, v_ref, beta_ref, gc_ref, u_ref, w_ref, qd_ref, kd_ref, intra_ref, t_ref, cd_ref):
        for c0 in range(0, DN_CB, DN_GROUP):
            chunks = range(c0, c0 + DN_GROUP)
            rhs, lmats = [], []
            for c in chunks:
                rows = pl.ds(c * CHUNK, CHUNK)
                q_c, k_c, v_c = q_ref[rows, :] * DN_SCALE, k_ref[rows, :], v_ref[rows, :]
                gcr_c = gc_ref[pl.ds(c, 1), :]
                beta_c, gcc_c = _to_col(beta_ref[pl.ds(c, 1), :]), _to_col(gcr_c)
                _, kb, lmat, intra = _chunk_terms(q_c, k_c, beta_c, gcc_c, gcr_c)
                eg = jnp.exp(gcc_c)
                g_last, _ = _last_row(gcc_c)
                qd_ref[rows, :] = (q_c * eg).astype(qd_ref.dtype)
                kd_ref[rows, :] = (k_c * jnp.exp(g_last - gcc_c)).astype(kd_ref.dtype)
                intra_ref[c] = intra.astype(intra_ref.dtype)
                cd_ref[c] = jnp.broadcast_to(jnp.exp(g_last), (1, DN_DV))
                rhs.append(jnp.concatenate([v_c * beta_c, kb * eg], axis=1))
                lmats.append(lmat)
            ts = _unit_lower_inverse(lmats)
            sols = [_dot3(t, r) for t, r in zip(ts, rhs)]
            for c, t, sol in zip(chunks, ts, sols):
                rows = pl.ds(c * CHUNK, CHUNK)
                t_ref[c] = t
                u_ref[rows, :] = sol[:, :DN_DV]
                w_ref[rows, :] = sol[:, DN_DV:].astype(w_ref.dtype)

    f32, mx = jnp.float32, MXU_DTYPE
    sd = jax.ShapeDtypeStruct
    return pl.pallas_call(
        body, name="dn_prep", grid=(DN_HEADS, n_chunks // DN_CB),
        out_shape=(sd(tok_shape, f32), sd(tok_shape, mx), sd(tok_shape, mx), sd(tok_shape, mx),
                   sd((DN_HEADS, n_chunks, CHUNK, CHUNK), mx), sd((DN_HEADS, n_chunks, CHUNK, CHUNK), f32),
                   sd((DN_HEADS, n_chunks, 1, DN_DV), f32)),
        in_specs=planes + [rowv, rowv], out_specs=(tok, tok, tok, tok, sq, sq, lane),
        compiler_params=pltpu.CompilerParams(dimension_semantics=("parallel", "parallel")),
    )(qkv, qkv, qkv, beta, gc)


def _dn_scan(u, w, qd, kd, intra, cd):
    s, width = u.shape
    n_chunks = s // CHUNK
    tok = pl.BlockSpec((CHUNK, width), lambda n: (n, 0))
    sq = pl.BlockSpec((DN_HEADS, None, CHUNK, CHUNK), lambda n: (0, n, 0, 0))
    lane = pl.BlockSpec((DN_HEADS, None, 1, DN_DV), lambda n: (0, n, 0, 0))
    st = pl.BlockSpec((DN_HEADS, None, DN_DK, DN_DV), lambda n: (0, n, 0, 0))

    def body(u_ref, w_ref, qd_ref, kd_ref, intra_ref, cd_ref, o_ref, vn_ref, st_ref, state):
        @pl.when(pl.program_id(0) == 0)
        def _():
            state[...] = jnp.zeros_like(state)

        heads = range(DN_HEADS)
        cols = [pl.ds(h * DN_DK, DN_DK) for h in heads]
        s_f = [state[h] for h in heads]
        s_mx = [s.astype(MXU_DTYPE) for s in s_f]
        for h in heads:
            st_ref[h] = s_mx[h]
        ws = [_dot(w_ref[:, cols[h]], s_mx[h]) for h in heads]
        qs = [_dot(qd_ref[:, cols[h]], s_mx[h]) for h in heads]
        v_new = [(u_ref[:, cols[h]] - ws[h]).astype(MXU_DTYPE) for h in heads]
        inner = [_dot(intra_ref[h], v_new[h]) for h in heads]
        outer = [_dot(kd_ref[:, cols[h]], v_new[h], 0, 0) for h in heads]
        for h in heads:
            vn_ref[:, cols[h]] = v_new[h]
            o_ref[:, cols[h]] = qs[h] + inner[h]
            state[h] = s_f[h] * cd_ref[h] + outer[h]

    sd = jax.ShapeDtypeStruct
    return pl.pallas_call(
        body, name="dn_scan", grid=(n_chunks,),
        out_shape=(sd(u.shape, jnp.float32), sd(u.shape, MXU_DTYPE),
                   sd((DN_HEADS, n_chunks, DN_DK, DN_DV), MXU_DTYPE)),
        in_specs=[tok, tok, tok, tok, sq, lane], out_specs=(tok, tok, st),
        scratch_shapes=[pltpu.VMEM((DN_HEADS, DN_DK, DN_DV), jnp.float32)],
        compiler_params=pltpu.CompilerParams(dimension_semantics=("arbitrary",)),
    )(u, w, qd, kd, intra, cd)


def _dn_bwd_scan(do, w, qd, kd, intra, cd, vn, st):
    s, width = do.shape
    n_chunks = s // CHUNK
    last = n_chunks - 1
    tok = pl.BlockSpec((CHUNK, width), lambda n: (last - n, 0))
    sq = pl.BlockSpec((DN_HEADS, None, CHUNK, CHUNK), lambda n: (0, last - n, 0, 0))
    lane = pl.BlockSpec((DN_HEADS, None, 1, DN_DV), lambda n: (0, last - n, 0, 0))
    stt = pl.BlockSpec((DN_HEADS, None, DN_DK, DN_DV), lambda n: (0, last - n, 0, 0))

    def body(do_ref, w_ref, qd_ref, kd_ref, intra_ref, cd_ref, vn_ref, st_ref,
             du_ref, dw_ref, dqd_ref, dkd_ref, dintra_ref, dgl_ref, dstate):
        @pl.when(pl.program_id(0) == 0)
        def _():
            dstate[...] = jnp.zeros_like(dstate)

        heads = range(DN_HEADS)
        cols = [pl.ds(h * DN_DK, DN_DK) for h in heads]
        ds_f = [dstate[h] for h in heads]
        ds_mx = [d.astype(MXU_DTYPE) for d in ds_f]
        do_h = [do_ref[:, cols[h]].astype(MXU_DTYPE) for h in heads]
        dv_a = [_dot(intra_ref[h], do_h[h], 0, 0) for h in heads]
        dv_b = [_dot(kd_ref[:, cols[h]], ds_mx[h]) for h in heads]
        d_intra = [_dot(do_h[h], vn_ref[:, cols[h]], 1, 1) for h in heads]
        d_qd = [_dot(do_h[h], st_ref[h], 1, 1) for h in heads]
        d_kd = [_dot(vn_ref[:, cols[h]], ds_mx[h], 1, 1) for h in heads]
        ds_q = [_dot(qd_ref[:, cols[h]], do_h[h], 0, 0) for h in heads]
        dv_new = [dv_a[h] + dv_b[h] for h in heads]
        dv_mx = [d.astype(MXU_DTYPE) for d in dv_new]
        d_w = [_dot(dv_mx[h], st_ref[h], 1, 1) for h in heads]
        ds_w = [_dot(w_ref[:, cols[h]], dv_mx[h], 0, 0) for h in heads]
        for h in heads:
            du_ref[:, cols[h]] = dv_new[h]
            dintra_ref[h] = d_intra[h]
            dqd_ref[:, cols[h]] = d_qd[h]
            dkd_ref[:, cols[h]] = d_kd[h]
            dw_ref[:, cols[h]] = -d_w[h]
            cd_h = cd_ref[h]
            dcd = jnp.sum(jnp.sum(st_ref[h].astype(jnp.float32) * ds_f[h], axis=1, keepdims=True), axis=0,
                          keepdims=True)
            dgl_ref[h] = dcd * cd_h
            dstate[h] = ds_q[h] + ds_f[h] * cd_h - ds_w[h]

    sd = jax.ShapeDtypeStruct
    f32 = jnp.float32
    return pl.pallas_call(
        body, name="dn_bwd_scan", grid=(n_chunks,),
        out_shape=(sd(do.shape, f32), sd(do.shape, f32), sd(do.shape, f32), sd(do.shape, f32),
                   sd((DN_HEADS, n_chunks, CHUNK, CHUNK), f32), sd((DN_HEADS, n_chunks, 1, DN_DV), f32)),
        in_specs=[tok, tok, tok, tok, sq, lane, tok, stt], out_specs=(tok, tok, tok, tok, sq, lane),
        scratch_shapes=[pltpu.VMEM((DN_HEADS, DN_DK, DN_DV), f32)],
        compiler_params=pltpu.CompilerParams(dimension_semantics=("arbitrary",)),
    )(do, w, qd, kd, intra, cd, vn, st)


def _dn_bwd_chunks(qkv, beta, gc, t, u, w, du, dw, dqd, dkd, dintra, dgl):
    s = qkv.shape[1]
    n_chunks = s // CHUNK
    tok, rowv, sq, lane, planes = _dn_specs(n_chunks)
    all_planes = pl.BlockSpec((3, DN_CB * CHUNK, DN_DK), lambda h, n: (0, n, h))

    def body(q_ref, k_ref, v_ref, beta_ref, gc_ref, t_ref, u_ref, w_ref, du_ref, dw_ref, dqd_ref, dkd_ref,
             dintra_ref, dgl_ref, dqkv_ref, dbeta_ref, dgc_ref):
        incl, strict, _ = _chunk_masks()

        def first(c):
            rows = pl.ds(c * CHUNK, CHUNK)
            q_c, k_c = q_ref[rows, :] * DN_SCALE, k_ref[rows, :]
            gcr_c = gc_ref[pl.ds(c, 1), :]
            beta_c, gcc_c = _to_col(beta_ref[pl.ds(c, 1), :]), _to_col(gcr_c)
            decay, kb, lmat, intra = _chunk_terms(q_c, k_c, beta_c, gcc_c, gcr_c)
            d_sol = jnp.concatenate([du_ref[rows, :], dw_ref[rows, :]], axis=1)
            d_rhs = _dot3(t_ref[c], d_sol, 0, 0)
            return dict(rows=rows, q=q_c, k=k_c, beta=beta_c, gcc=gcc_c, decay=decay, kb=kb, lmat=lmat, intra=intra,
                        d_rhs=d_rhs)

        def second(c, e):
            sol = jnp.concatenate([u_ref[e["rows"], :], w_ref[e["rows"], :].astype(jnp.float32)], axis=1)
            e["d_l"] = jnp.where(strict, -_dot(e["d_rhs"], sol, 1, 1), 0.0)
            e["d_intra"] = jnp.where(incl, dintra_ref[c], 0.0)
            d_qk = e["d_intra"] * e["decay"]
            e["dq"] = _dot(d_qk, e["k"])
            e["dk"] = _dot(d_qk, e["q"], 0, 0)

        def third(e):
            d_a = e["d_l"] * e["decay"]
            e["dkb"] = _dot(d_a, e["k"])
            e["dk"] = e["dk"] + _dot(d_a, e["kb"], 0, 0)

        def last(c, e):
            rows, q_c, k_c, beta_c, gcc_c = e["rows"], e["q"], e["k"], e["beta"], e["gcc"]
            v_c = v_ref[rows, :]
            eg = jnp.exp(gcc_c)
            g_last, is_last = _last_row(gcc_c)
            e_rev = jnp.exp(g_last - gcc_c)
            d_rhs_u, d_rhs_w = e["d_rhs"][:, :DN_DV], e["d_rhs"][:, DN_DV:]
            dqkv_ref[2, rows, :] = d_rhs_u * beta_c
            dbeta = jnp.sum(d_rhs_u * v_c, axis=1, keepdims=True)
            dkb = e["dkb"] + d_rhs_w * eg
            dgc = jnp.sum(d_rhs_w * e["kb"] * eg, axis=1, keepdims=True)
            m1 = e["d_l"] * e["lmat"]
            dgc = dgc + jnp.sum(m1, axis=1, keepdims=True)
            dgr = -jnp.sum(m1, axis=0, keepdims=True)
            m2 = e["d_intra"] * e["intra"]
            dgc = dgc + jnp.sum(m2, axis=1, keepdims=True)
            dgr = dgr - jnp.sum(m2, axis=0, keepdims=True)
            dqd = dqd_ref[rows, :]
            dq = e["dq"] + dqd * eg
            dgc = dgc + jnp.sum(dqd * q_c * eg, axis=1, keepdims=True)
            dkd = dkd_ref[rows, :]
            dk = e["dk"] + dkd * e_rev
            tk = jnp.sum(dkd * k_c * e_rev, axis=1, keepdims=True)
            dgc = dgc - tk
            d_last = dgl_ref[c][:, :1] + jnp.sum(tk, axis=0, keepdims=True)
            dgc = dgc + jnp.where(is_last, d_last, 0.0)
            dk = dk + dkb * beta_c
            dbeta = dbeta + jnp.sum(dkb * k_c, axis=1, keepdims=True)
            dqkv_ref[0, rows, :] = dq * DN_SCALE
            dqkv_ref[1, rows, :] = dk
            dbeta_ref[pl.ds(c, 1), :] = _to_row(dbeta)
            dgc_ref[pl.ds(c, 1), :] = _to_row(dgc) + dgr

        for c0 in range(0, DN_CB, DN_GROUP):
            chunks = range(c0, c0 + DN_GROUP)
            env = [first(c) for c in chunks]
            for c, e in zip(chunks, env):
                second(c, e)
            for e in env:
                third(e)
            for c, e in zip(chunks, env):
                last(c, e)

    sd = jax.ShapeDtypeStruct
    f32 = jnp.float32
    return pl.pallas_call(
        body, name="dn_bwd_chunks", grid=(DN_HEADS, n_chunks // DN_CB),
        out_shape=(sd(qkv.shape, f32), sd(beta.shape, f32), sd(gc.shape, f32)),
        in_specs=planes + [rowv, rowv, sq, tok, tok, tok, tok, tok, tok, sq, lane],
        out_specs=(all_planes, rowv, rowv),
        compiler_params=pltpu.CompilerParams(dimension_semantics=("parallel", "parallel")),
    )(qkv, qkv, qkv, beta, gc, t, u, w, du, dw, dqd, dkd, dintra, dgl)


@jax.custom_vjp
def _delta_rule_op(qkv, beta, gc):
    return _delta_rule_fwd(qkv, beta, gc)[0]


def _delta_rule_fwd(qkv, beta, gc):
    u, w, qd, kd, intra, t, cd = _dn_prep(qkv, beta, gc)
    out, vn, st = _dn_scan(u, w, qd, kd, intra, cd)
    return out, (qkv, beta, gc, u, w, qd, kd, intra, t, cd, vn, st)


def _delta_rule_bwd(res, do):
    qkv, beta, gc, u, w, qd, kd, intra, t, cd, vn, st = res
    du, dw, dqd, dkd, dintra, dgl = _dn_bwd_scan(do, w, qd, kd, intra, cd, vn, st)
    return _dn_bwd_chunks(qkv, beta, gc, t, u, w, du, dw, dqd, dkd, dintra, dgl)


_delta_rule_op.defvjp(_delta_rule_fwd, _delta_rule_bwd)


def _gated_delta_rule(qkv, g, beta):
    s, h = g.shape
    n_chunks = s // CHUNK
    gc = jnp.cumsum(g.T.reshape(h, n_chunks, CHUNK), axis=-1)
    return _delta_rule_op(qkv, beta.T.reshape(h, n_chunks, CHUNK), gc)


PRE_ROWS = 256
HALO = 8
PRE_W = DN_QK_W


def _shift_rows(xs, k):
    return pltpu.roll(xs, k, 0)[HALO:]


def _conv_silu(x_ref, halo_ref, w_ref, first_block):
    halo = jnp.where(first_block, 0.0, halo_ref[...])
    xs = jnp.concatenate([halo, x_ref[...]], axis=0)
    taps = [_shift_rows(xs, CONV_WIDTH - 1 - j) for j in range(CONV_WIDTH - 1)] + [x_ref[...]]
    conv = sum(w_ref[pl.ds(j, 1), :] * taps[j] for j in range(CONV_WIDTH))
    return conv, jax.nn.sigmoid(conv), taps


def _pre_specs():
    blk = pl.BlockSpec((PRE_ROWS, PRE_W), lambda j, i: (i, j))
    prev = pl.BlockSpec((HALO, PRE_W), lambda j, i: (jnp.maximum(i * (PRE_ROWS // HALO) - 1, 0), j))
    wts = pl.BlockSpec((CONV_WIDTH, PRE_W), lambda j, i: (0, j))
    plane = pl.BlockSpec((None, PRE_ROWS, PRE_W), lambda j, i: (j, i, 0))
    return blk, prev, wts, plane


def _pre_fwd_call(x, conv_w):
    s = x.shape[0]
    blk, prev, wts, plane = _pre_specs()

    def body(x_ref, halo_ref, w_ref, o_ref):
        conv, sig, _ = _conv_silu(x_ref, halo_ref, w_ref, pl.program_id(1) == 0)
        act = conv * sig
        is_v = pl.program_id(0) == 2
        for h in range(DN_HEADS):
            cols = slice(h * DN_DK, (h + 1) * DN_DK)
            a_h = act[:, cols]
            r = lax.rsqrt(jnp.sum(a_h * a_h, axis=-1, keepdims=True) + NORM_EPS)
            o_ref[:, cols] = a_h * jnp.where(is_v, 1.0, r)

    return pl.pallas_call(
        body, name="pre_fwd", grid=(3, s // PRE_ROWS),
        out_shape=jax.ShapeDtypeStruct((3, s, PRE_W), jnp.float32),
        in_specs=[blk, prev, wts], out_specs=plane,
        compiler_params=pltpu.CompilerParams(dimension_semantics=("parallel", "parallel")),
    )(x, x, conv_w)


def _pre_bwd_act_call(x, conv_w, d_out):
    s = x.shape[0]
    blk, prev, wts, plane = _pre_specs()

    def body(x_ref, halo_ref, w_ref, do_ref, dc_ref):
        conv, sig, _ = _conv_silu(x_ref, halo_ref, w_ref, pl.program_id(1) == 0)
        act = conv * sig
        d_silu = sig * (1.0 + conv * (1.0 - sig))
        is_v = pl.program_id(0) == 2
        for h in range(DN_HEADS):
            cols = slice(h * DN_DK, (h + 1) * DN_DK)
            a_h, do_h = act[:, cols], do_ref[:, cols]
            r = lax.rsqrt(jnp.sum(a_h * a_h, axis=-1, keepdims=True) + NORM_EPS)
            n_h = a_h * r
            d_norm = r * (do_h - n_h * jnp.sum(do_h * n_h, axis=-1, keepdims=True))
            dc_ref[:, cols] = jnp.where(is_v, do_h, d_norm) * d_silu[:, cols]

    return pl.pallas_call(
        body, name="pre_bwd_act", grid=(3, s // PRE_ROWS),
        out_shape=jax.ShapeDtypeStruct(x.shape, jnp.float32),
        in_specs=[blk, prev, wts, plane], out_specs=blk,
        compiler_params=pltpu.CompilerParams(dimension_semantics=("parallel", "parallel")),
    )(x, x, conv_w, d_out)


def _pre_bwd_conv_call(x, conv_w, dc):
    s = x.shape[0]
    n_blocks = s // PRE_ROWS
    blk, prev, wts, _ = _pre_specs()
    nxt = pl.BlockSpec((HALO, PRE_W), lambda j, i: (jnp.minimum((i + 1) * (PRE_ROWS // HALO), s // HALO - 1), j))

    def body(x_ref, halo_ref, w_ref, dc_ref, dcn_ref, dx_ref, dw_ref):
        i = pl.program_id(1)

        @pl.when(i == 0)
        def _():
            dw_ref[...] = jnp.zeros_like(dw_ref)

        dcv = dc_ref[...]
        ahead = jnp.concatenate([dcv, jnp.where(i == n_blocks - 1, 0.0, dcn_ref[...])], axis=0)
        dx = w_ref[pl.ds(CONV_WIDTH - 1, 1), :] * dcv
        for j in range(CONV_WIDTH - 1):
            k = CONV_WIDTH - 1 - j
            dx = dx + w_ref[pl.ds(j, 1), :] * pltpu.roll(ahead, PRE_ROWS + HALO - k, 0)[:PRE_ROWS]
        dx_ref[...] = dx
        halo = jnp.where(i == 0, 0.0, halo_ref[...])
        xs = jnp.concatenate([halo, x_ref[...]], axis=0)
        for j in range(CONV_WIDTH):
            tap = x_ref[...] if j == CONV_WIDTH - 1 else _shift_rows(xs, CONV_WIDTH - 1 - j)
            dw_ref[pl.ds(j, 1), :] += jnp.sum(dcv * tap, axis=0, keepdims=True)

    sd = jax.ShapeDtypeStruct
    return pl.pallas_call(
        body, name="pre_bwd_conv", grid=(3, n_blocks),
        out_shape=(sd(x.shape, jnp.float32), sd(conv_w.shape, jnp.float32)),
        in_specs=[blk, prev, wts, blk, nxt], out_specs=(blk, wts),
        compiler_params=pltpu.CompilerParams(dimension_semantics=("parallel", "arbitrary")),
    )(x, x, conv_w, dc, dc)


@jax.custom_vjp
def _pre_op(x, conv_w):
    return _pre_fwd_call(x, conv_w)


def _pre_op_fwd(x, conv_w):
    return _pre_fwd_call(x, conv_w), (x, conv_w)


def _pre_op_bwd(res, d_out):
    x, conv_w = res
    return _pre_bwd_conv_call(x, conv_w, _pre_bwd_act_call(x, conv_w, d_out))


_pre_op.defvjp(_pre_op_fwd, _pre_op_bwd)


def _project(h, h_lo, w, slot):
    b, s, d = h.shape
    return _linear(h.reshape(b * s, d), h_lo.reshape(b * s, d), w, slot).reshape(b, s, w.shape[1])


def _rope_table(positions, dh):
    inv_freq = ROPE_THETA ** (-jnp.arange(0, dh, 2, dtype=jnp.float32) / dh)
    ang = positions.astype(jnp.float32)[:, None] * inv_freq
    return jnp.concatenate([jnp.cos(ang), jnp.cos(ang), jnp.sin(ang), jnp.sin(ang)], axis=-1)


GATE_ROWS = 512


def _gate_terms(o_h, z_h):
    r = lax.rsqrt(jnp.mean(o_h * o_h, axis=-1, keepdims=True) + NORM_EPS)
    sig = jax.nn.sigmoid(z_h)
    return r, o_h * r, sig, z_h * sig


def _gate_fwd_call(o, z, nw):
    tok = pl.BlockSpec((GATE_ROWS, DN_V_W), lambda i: (i, 0))
    vec = pl.BlockSpec((1, DN_DV), lambda i: (0, 0))

    def body(o_ref, z_ref, nw_ref, y_ref):
        for h in range(DN_HEADS):
            cols = pl.ds(h * DN_DV, DN_DV)
            _, n_h, _, g_h = _gate_terms(o_ref[:, cols], z_ref[:, cols])
            y_ref[:, cols] = n_h * nw_ref[...] * g_h

    return pl.pallas_call(
        body, name="gate_fwd", grid=(o.shape[0] // GATE_ROWS,),
        out_shape=jax.ShapeDtypeStruct(o.shape, jnp.float32), in_specs=[tok, tok, vec], out_specs=tok,
        compiler_params=pltpu.CompilerParams(dimension_semantics=("parallel",)),
    )(o, z, nw)


def _gate_bwd_call(o, z, nw, dy):
    tok = pl.BlockSpec((GATE_ROWS, DN_V_W), lambda i: (i, 0))
    vec = pl.BlockSpec((1, DN_DV), lambda i: (0, 0))

    def body(o_ref, z_ref, nw_ref, dy_ref, do_ref, dz_ref, dnw_ref):
        @pl.when(pl.program_id(0) == 0)
        def _():
            dnw_ref[...] = jnp.zeros_like(dnw_ref)

        for h in range(DN_HEADS):
            cols = pl.ds(h * DN_DV, DN_DV)
            z_h, dy_h = z_ref[:, cols], dy_ref[:, cols]
            r, n_h, sig, g_h = _gate_terms(o_ref[:, cols], z_h)
            dz_ref[:, cols] = dy_h * n_h * nw_ref[...] * (sig * (1.0 + z_h * (1.0 - sig)))
            dn = dy_h * nw_ref[...] * g_h
            do_ref[:, cols] = r * (dn - n_h * jnp.mean(dn * n_h, axis=-1, keepdims=True))
            dnw_ref[...] += jnp.sum(dy_h * n_h * g_h, axis=0, keepdims=True)

    sd = jax.ShapeDtypeStruct
    return pl.pallas_call(
        body, name="gate_bwd", grid=(o.shape[0] // GATE_ROWS,),
        out_shape=(sd(o.shape, jnp.float32), sd(o.shape, jnp.float32), sd(nw.shape, jnp.float32)),
        in_specs=[tok, tok, vec, tok], out_specs=(tok, tok, vec),
        compiler_params=pltpu.CompilerParams(dimension_semantics=("arbitrary",)),
    )(o, z, nw, dy)


@jax.custom_vjp
def _gate_op(o, z, nw):
    return _gate_fwd_call(o, z, nw)


def _gate_op_fwd(o, z, nw):
    return _gate_fwd_call(o, z, nw), (o, z, nw)


def _gate_op_bwd(res, dy):
    return _gate_bwd_call(*res, dy)


_gate_op.defvjp(_gate_op_fwd, _gate_op_bwd)


_MASKED = -1e30


def _swa_probs(qs, k_h, sinks, valid):
    ss = [jnp.where(valid, _dot(q_h, k_h, 1, 1) * (SWA_DH ** -0.5), _MASKED) for q_h in qs]
    ms = [jnp.maximum(jnp.max(s, axis=-1, keepdims=True), sink) for s, sink in zip(ss, sinks)]
    ps = [jnp.exp(s - m) for s, m in zip(ss, ms)]
    es = [jnp.exp(sink - m) for sink, m in zip(sinks, ms)]
    invs = [1.0 / (jnp.sum(p, axis=-1, keepdims=True) + e) for p, e in zip(ps, es)]
    return [p * inv for p, inv in zip(ps, invs)], [e * inv for e, inv in zip(es, invs)]


def _swa_valid(n):
    qi = lax.broadcasted_iota(jnp.int32, (WINDOW, 2 * WINDOW), 0)
    kj = lax.broadcasted_iota(jnp.int32, (WINDOW, 2 * WINDOW), 1)
    diff = qi + WINDOW - kj
    return (diff >= 0) & (diff < WINDOW) & ((kj >= WINDOW) | (n > 0))


def _rope_matrix():
    i = lax.broadcasted_iota(jnp.int32, (SWA_DH, SWA_DH), 0)
    j = lax.broadcasted_iota(jnp.int32, (SWA_DH, SWA_DH), 1)
    return jnp.where(i == j + SWA_DH // 2, -1.0, jnp.where(i + SWA_DH // 2 == j, 1.0, 0.0))


def _times_exact(a, b, ca=1, cb=0):
    if MXU_DTYPE == jnp.float32:
        return _dot(a, b, ca, cb)
    a_hi = a.astype(MXU_DTYPE)
    return _dot(a_hi, b, ca, cb) + _dot(a - a_hi.astype(jnp.float32), b, ca, cb)


def _rope(x, table, rmat):
    return x * table[:, :SWA_DH] + _times_exact(x, rmat) * table[:, SWA_DH:]


def _unrope(dy, table, rmat):
    return dy * table[:, :SWA_DH] + _times_exact(dy * table[:, SWA_DH:], rmat, 1, 1)


def _swa_specs():
    qs = pl.BlockSpec((WINDOW, SWA_Q_W), lambda n: (n, 0))
    first = lambda n: jnp.maximum(n - 1, 0)
    kv = [pl.BlockSpec((WINDOW, SWA_KV_W), lambda n: (first(n), 0)), pl.BlockSpec((WINDOW, SWA_KV_W), lambda n: (n, 0)),
          pl.BlockSpec((WINDOW, SWA_KV_W), lambda n: (first(n), 1)), pl.BlockSpec((WINDOW, SWA_KV_W), lambda n: (n, 1))]
    tables = [pl.BlockSpec((WINDOW, 2 * SWA_DH), lambda n: (first(n), 0)),
              pl.BlockSpec((WINDOW, 2 * SWA_DH), lambda n: (n, 0))]
    cur = pl.BlockSpec((WINDOW, SWA_KV_W), lambda n: (n, 0))
    sk = pl.BlockSpec((SWA_HEADS, 1, 128), lambda n: (0, 0, 0))
    return qs, kv, tables, cur, sk


def _swa_load(kp_ref, kc_ref, vp_ref, vc_ref, tp_ref, tc_ref):
    rmat = _rope_matrix()
    table_kk = jnp.concatenate([tp_ref[...], tc_ref[...]], axis=0)
    kk = jnp.concatenate([kp_ref[...], kc_ref[...]], axis=0)
    kk = [_rope(kk[:, hkv * SWA_DH:(hkv + 1) * SWA_DH], table_kk, rmat) for hkv in range(SWA_KV_HEADS)]
    vv = jnp.concatenate([vp_ref[...], vc_ref[...]], axis=0)
    return kk, vv, table_kk, rmat


def _swa_fwd_call(q, kv, table, sinks):
    qs, kvs, tables, _, sk = _swa_specs()

    def body(q_ref, kp_ref, kc_ref, vp_ref, vc_ref, tp_ref, tc_ref, sink_ref, o_ref):
        valid = _swa_valid(pl.program_id(0))
        kk, vv, _, rmat = _swa_load(kp_ref, kc_ref, vp_ref, vc_ref, tp_ref, tc_ref)
        for hkv in range(SWA_KV_HEADS):
            heads = range(hkv * SWA_GROUP, (hkv + 1) * SWA_GROUP)
            q_rot = [_rope(q_ref[:, pl.ds(h * SWA_DH, SWA_DH)], tc_ref[...], rmat) for h in heads]
            probs, _ = _swa_probs(q_rot, kk[hkv], [sink_ref[h][:, :1] for h in heads], valid)
            outs = [_dot(p, vv[:, hkv * SWA_DH:(hkv + 1) * SWA_DH]) for p in probs]
            for h, o in zip(heads, outs):
                o_ref[:, pl.ds(h * SWA_DH, SWA_DH)] = o

    return pl.pallas_call(
        body, name="swa_fwd", grid=(q.shape[0] // WINDOW,),
        out_shape=jax.ShapeDtypeStruct(q.shape, jnp.float32),
        in_specs=[qs] + kvs + tables + [sk], out_specs=qs,
        compiler_params=pltpu.CompilerParams(dimension_semantics=("parallel",)),
    )(q, kv, kv, kv, kv, table, table, sinks)


def _swa_bwd_call(q, kv, table, sinks, do):
    qs, kvs, tables, cur, sk = _swa_specs()

    def body(q_ref, kp_ref, kc_ref, vp_ref, vc_ref, tp_ref, tc_ref, sink_ref, do_ref,
             dq_ref, dkc_ref, dkp_ref, dvc_ref, dvp_ref, ds_ref):
        @pl.when(pl.program_id(0) == 0)
        def _():
            ds_ref[...] = jnp.zeros_like(ds_ref)

        valid = _swa_valid(pl.program_id(0))
        kk, vv, table_kk, rmat = _swa_load(kp_ref, kc_ref, vp_ref, vc_ref, tp_ref, tc_ref)
        lane0 = lax.broadcasted_iota(jnp.int32, (1, 128), 1) == 0
        dk_heads, dv_heads = [], []
        for hkv in range(SWA_KV_HEADS):
            k_h, v_h = kk[hkv], vv[:, hkv * SWA_DH:(hkv + 1) * SWA_DH]
            heads = range(hkv * SWA_GROUP, (hkv + 1) * SWA_GROUP)
            q_rot = [_rope(q_ref[:, pl.ds(h * SWA_DH, SWA_DH)], tc_ref[...], rmat) for h in heads]
            dos = [do_ref[:, pl.ds(h * SWA_DH, SWA_DH)] for h in heads]
            probs, p_sinks = _swa_probs(q_rot, k_h, [sink_ref[h][:, :1] for h in heads], valid)
            dps = [_dot(do_h, v_h, 1, 1) for do_h in dos]
            rss = [jnp.sum(p * dp, axis=-1, keepdims=True) for p, dp in zip(probs, dps)]
            d_ss = [p * (dp - rs) for p, dp, rs in zip(probs, dps, rss)]
            dqs = [_dot(d_s, k_h) * (SWA_DH ** -0.5) for d_s in d_ss]
            dks = [_dot(d_s, q_h, 0, 0) for d_s, q_h in zip(d_ss, q_rot)]
            dvs = [_dot(p, do_h, 0, 0) for p, do_h in zip(probs, dos)]
            dqs = [_unrope(dq, tc_ref[...], rmat) for dq in dqs]
            for h, dq, p_sink, rs in zip(heads, dqs, p_sinks, rss):
                dq_ref[:, pl.ds(h * SWA_DH, SWA_DH)] = dq
                d_sink = -jnp.sum(p_sink * rs, axis=0, keepdims=True)
                ds_ref[h] += jnp.where(lane0, d_sink, 0.0)
            dk_heads.append(_unrope(sum(dks[1:], dks[0]) * (SWA_DH ** -0.5), table_kk, rmat))
            dv_heads.append(sum(dvs[1:], dvs[0]))
        dk = jnp.concatenate(dk_heads, axis=1)
        dv = jnp.concatenate(dv_heads, axis=1)
        dkp_ref[...] = dk[:WINDOW]
        dkc_ref[...] = dk[WINDOW:]
        dvp_ref[...] = dv[:WINDOW]
        dvc_ref[...] = dv[WINDOW:]

    sd = jax.ShapeDtypeStruct
    f32 = jnp.float32
    half = (q.shape[0], SWA_KV_W)
    return pl.pallas_call(
        body, name="swa_bwd", grid=(q.shape[0] // WINDOW,),
        out_shape=(sd(q.shape, f32), sd(half, f32), sd(half, f32), sd(half, f32), sd(half, f32), sd(sinks.shape, f32)),
        in_specs=[qs] + kvs + tables + [sk, qs], out_specs=(qs, cur, cur, cur, cur, sk),
        compiler_params=pltpu.CompilerParams(dimension_semantics=("arbitrary",)),
    )(q, kv, kv, kv, kv, table, table, sinks, do)


@jax.custom_vjp
def _swa_op(q, kv, table, sinks):
    return _swa_fwd_call(q, kv, table, sinks)


def _swa_op_fwd(q, kv, table, sinks):
    return _swa_fwd_call(q, kv, table, sinks), (q, kv, table, sinks)


def _swa_op_bwd(res, do):
    q, kv, table, sinks = res
    dq, dkc, dkp, dvc, dvp, dsinks = _swa_bwd_call(q, kv, table, sinks, do)

    def fold(cur, prev):
        return cur + jnp.concatenate([prev[WINDOW:], jnp.zeros_like(prev[:WINDOW])], axis=0)

    return dq, jnp.concatenate([fold(dkc, dkp), fold(dvc, dvp)], axis=1), jnp.zeros_like(table), dsinks


_swa_op.defvjp(_swa_op_fwd, _swa_op_bwd)


def _swa_sink_attention(q, kv, table, sinks):
    return _swa_op(q, kv, table, jnp.broadcast_to(sinks[:, None, None], (SWA_HEADS, 1, 128)))


MEM_ROWS = 512


def _mem_probs(q_h, k_h):
    s = _dot(q_h, k_h, 1, 1) * (MEM_DH ** -0.5)
    p = jnp.exp(s - jnp.max(s, axis=-1, keepdims=True))
    return p / jnp.sum(p, axis=-1, keepdims=True)


def _mem_fwd_call(qm, kv):
    qs = pl.BlockSpec((MEM_ROWS, MEM_W), lambda i: (i, 0))
    kvs = pl.BlockSpec(kv.shape, lambda i: (0, 0))

    def body(q_ref, kv_ref, o_ref):
        for h in range(MEM_HEADS):
            cols = pl.ds(h * MEM_DH, MEM_DH)
            probs = _mem_probs(q_ref[:, cols], kv_ref[:, cols])
            o_ref[:, cols] = _dot(probs, kv_ref[:, pl.ds(MEM_W + h * MEM_DH, MEM_DH)])

    return pl.pallas_call(
        body, name="mem_fwd", grid=(qm.shape[0] // MEM_ROWS,),
        out_shape=jax.ShapeDtypeStruct(qm.shape, jnp.float32), in_specs=[qs, kvs], out_specs=qs,
        compiler_params=pltpu.CompilerParams(dimension_semantics=("parallel",)),
    )(qm, kv)


def _mem_bwd_call(qm, kv, do):
    qs = pl.BlockSpec((MEM_ROWS, MEM_W), lambda i: (i, 0))
    kvs = pl.BlockSpec(kv.shape, lambda i: (0, 0))

    def body(q_ref, kv_ref, do_ref, dq_ref, dkv_ref):
        @pl.when(pl.program_id(0) == 0)
        def _():
            dkv_ref[...] = jnp.zeros_like(dkv_ref)

        for h in range(MEM_HEADS):
            cols = pl.ds(h * MEM_DH, MEM_DH)
            v_cols = pl.ds(MEM_W + h * MEM_DH, MEM_DH)
            q_h, k_h, do_h = q_ref[:, cols], kv_ref[:, cols], do_ref[:, cols]
            probs = _mem_probs(q_h, k_h)
            dp = _dot(do_h, kv_ref[:, v_cols], 1, 1)
            d_s = probs * (dp - jnp.sum(probs * dp, axis=-1, keepdims=True))
            dq_ref[:, cols] = _dot(d_s, k_h) * (MEM_DH ** -0.5)
            dkv_ref[:, cols] += _dot(d_s, q_h, 0, 0) * (MEM_DH ** -0.5)
            dkv_ref[:, v_cols] += _dot(probs, do_h, 0, 0)

    sd = jax.ShapeDtypeStruct
    return pl.pallas_call(
        body, name="mem_bwd", grid=(qm.shape[0] // MEM_ROWS,),
        out_shape=(sd(qm.shape, jnp.float32), sd(kv.shape, jnp.float32)),
        in_specs=[qs, kvs, qs], out_specs=(qs, kvs),
        compiler_params=pltpu.CompilerParams(dimension_semantics=("arbitrary",)),
    )(qm, kv, do)


@jax.custom_vjp
def _mem_op(qm, kv):
    return _mem_fwd_call(qm, kv)


def _mem_op_fwd(qm, kv):
    return _mem_fwd_call(qm, kv), (qm, kv)


def _mem_op_bwd(res, do):
    return _mem_bwd_call(*res, do)


_mem_op.defvjp(_mem_op_fwd, _mem_op_bwd)


def _memory_attention(qm, kv):
    return _mem_op(qm[0], kv[0])[None]


def _mixer_a(h, h_lo, mem, mem_lo, p, s, layer):
    B, S, _ = h.shape
    proj = _project(h, h_lo, p["a_w_in"][layer], s["a_w_in"][layer])
    c1 = 2 * DN_QK_W + DN_V_W
    qkv = proj[..., :c1]
    z = proj[..., c1:QKVZ_W]
    qm = proj[..., QKVZ_W:QKVZ_W + MEM_W]
    a = proj[..., QKVZ_W + MEM_W:QKVZ_W + MEM_W + DN_HEADS]
    b = proj[..., QKVZ_W + MEM_W + DN_HEADS:QKVZ_W + MEM_W + 2 * DN_HEADS]
    planes = _pre_op(qkv[0], p["a_conv_w"][layer])
    beta = jax.nn.sigmoid(b[0])
    g = -jnp.exp(p["a_A_log"][layer]) * jax.nn.softplus(a[0] + p["a_dt_bias"][layer])
    o = _gate_op(_gated_delta_rule(planes, g, beta), z[0], p["a_norm_w"][layer][None])[None]
    kv = _project(mem, mem_lo, p["mem_w_kv"][layer], s["mem_w_kv"][layer])
    mo = _memory_attention(qm, kv)
    cat = jnp.concatenate([o, mo], axis=-1)
    return _project(cat, _lo(cat), p["w_o"][layer], s["w_o"][layer])


def _mixer_b(h, h_lo, mem, mem_lo, kv_shared, table, p, s, layer):
    j = layer - N_A
    proj = _project(h, h_lo, p["b_w_in"][j], s["b_w_in"][j])
    o = _swa_sink_attention(proj[0, :, :SWA_Q_W], kv_shared, table, p["b_sinks"][j])[None]
    kv = _project(mem, mem_lo, p["mem_w_kv"][layer], s["mem_w_kv"][layer])
    mo = _memory_attention(proj[..., SWA_Q_W:], kv)
    cat = jnp.concatenate([o, mo], axis=-1)
    return _project(cat, _lo(cat), p["w_o"][layer], s["w_o"][layer])


def _forward(p, s, x, mem, positions):
    table = _rope_table(positions[0], SWA_DH)
    h, h_lo, mem_lo = x, _lo(x), _lo(mem)
    kv_shared = None
    for layer in range(DEPTH):
        if layer < N_A:
            mix = _mixer_a(h, h_lo, mem, mem_lo, p, s, layer)
        else:
            mix = _mixer_b(h, h_lo, mem, mem_lo, kv_shared, table, p, s, layer)
        seq = h.shape[1]
        h2, h2_lo = _ln_res(h[0], mix[0], p["ln_g"][layer, 0][None], p["ln_b"][layer, 0][None])
        down = _mlp(h2, h2_lo, p["mlp_w_up"][layer], p["mlp_w_down"][layer], s["mlp_w_up"][layer],
                    s["mlp_w_down"][layer])
        h, h_lo = _ln_res(h2, down, p["ln_g"][layer, 1][None], p["ln_b"][layer, 1][None])
        h, h_lo = h.reshape(1, seq, D_MODEL), h_lo.reshape(1, seq, D_MODEL)
        if layer == N_A - 1:
            kv_shared = _project(h, h_lo, p["w_kv_shared"], s["w_kv_shared"])[0]
    return h


def _loss(diff, s, p, mem, positions, target):
    y = _forward({**p, **diff["small"]}, s, diff["x"], mem, positions)
    return 0.5 * jnp.sum(jnp.mean(jnp.square(y - target), axis=-1))


def _reorder_a_w_in(w):
    pad = jnp.zeros(w.shape[:-1] + (A_IN_PAD - A_IN,), w.dtype)
    return jnp.concatenate([w[..., :QKVZ_W], w[..., QKVZ_W + 2 * DN_HEADS:], w[..., QKVZ_W:QKVZ_W + 2 * DN_HEADS], pad],
                           axis=-1)


def _restore_a_w_in(w):
    return jnp.concatenate([w[..., :QKVZ_W], w[..., QKVZ_W + MEM_W:QKVZ_W + MEM_W + 2 * DN_HEADS],
                            w[..., QKVZ_W:QKVZ_W + MEM_W]], axis=-1)


def kernel(x, mem, positions, a_w_in, a_conv_w, a_A_log, a_dt_bias, a_norm_w, b_w_in, b_sinks, w_kv_shared, mem_w_kv, w_o, mlp_w_up, mlp_w_down, ln_g, ln_b, loss_target, m_a_w_in, m_a_conv_w, m_a_A_log, m_a_dt_bias, m_a_norm_w, m_b_w_in, m_b_sinks, m_w_kv_shared, m_mem_w_kv, m_w_o, m_mlp_w_up, m_mlp_w_down, m_ln_g, m_ln_b, v_a_w_in, v_a_conv_w, v_a_A_log, v_a_dt_bias, v_a_norm_w, v_b_w_in, v_b_sinks, v_w_kv_shared, v_mem_w_kv, v_w_o, v_mlp_w_up, v_mlp_w_down, v_ln_g, v_ln_b):
    w_sh = dict(a_w_in=a_w_in, a_conv_w=a_conv_w, a_A_log=a_A_log, a_dt_bias=a_dt_bias, a_norm_w=a_norm_w,
                b_w_in=b_w_in, b_sinks=b_sinks, w_kv_shared=w_kv_shared, mem_w_kv=mem_w_kv, w_o=w_o,
                mlp_w_up=mlp_w_up, mlp_w_down=mlp_w_down, ln_g=ln_g, ln_b=ln_b)
    m_sh = dict(a_w_in=m_a_w_in, a_conv_w=m_a_conv_w, a_A_log=m_a_A_log, a_dt_bias=m_a_dt_bias, a_norm_w=m_a_norm_w,
                b_w_in=m_b_w_in, b_sinks=m_b_sinks, w_kv_shared=m_w_kv_shared, mem_w_kv=m_mem_w_kv, w_o=m_w_o,
                mlp_w_up=m_mlp_w_up, mlp_w_down=m_mlp_w_down, ln_g=m_ln_g, ln_b=m_ln_b)
    v_sh = dict(a_w_in=v_a_w_in, a_conv_w=v_a_conv_w, a_A_log=v_a_A_log, a_dt_bias=v_a_dt_bias, a_norm_w=v_a_norm_w,
                b_w_in=v_b_w_in, b_sinks=v_b_sinks, w_kv_shared=v_w_kv_shared, mem_w_kv=v_mem_w_kv, w_o=v_w_o,
                mlp_w_up=v_mlp_w_up, mlp_w_down=v_mlp_w_down, ln_g=v_ln_g, ln_b=v_ln_b)
    shard_shapes = {n: w_sh[n].shape for n in WEIGHTS}
    rb, rows = _rows_for(w_sh)

    big, small = _pack(w_sh, rb, jnp.bfloat16)
    gbig, gsmall = _gather_weights(big.reshape(2, rb // 2, FLAT_W), small.reshape(2, SMALL_ROWS // 2, FLAT_W))
    gbig, gsmall = gbig.reshape(N_CHIPS, rb, FLAT_W), gsmall.reshape(N_CHIPS, SMALL_ROWS, FLAT_W)
    pieces = [_unpack(gbig[q], gsmall[q], shard_shapes) for q in range(N_CHIPS)]
    full = {n: jnp.concatenate([pieces[q][n] for q in range(N_CHIPS)], axis=SHARD_AXIS[n]) for n in SHARD_AXIS}
    for n in REPLICATED:
        full[n] = w_sh[n]
    big_w = {n: full[n] for n in BIG}
    big_w["a_w_in"] = _reorder_a_w_in(big_w["a_w_in"])
    small_w = {n: full[n] for n in SMALL}
    slots = {n: jnp.zeros(big_w[n].shape, jnp.float32) for n in BIG}

    loss, (grads, g_slots) = jax.value_and_grad(_loss, argnums=(0, 1))(
        {"x": x, "small": small_w}, slots, big_w, mem, positions, loss_target)
    loss = lax.psum(loss, ("x", "y", "c"))
    g_full = {**g_slots, **grads["small"]}
    g_full["a_w_in"] = _restore_a_w_in(g_full["a_w_in"])

    def shard_of(n, q):
        if n in REPLICATED:
            return g_full[n]
        size = shard_shapes[n][SHARD_AXIS[n]]
        return lax.slice_in_dim(g_full[n], q * size, (q + 1) * size, axis=SHARD_AXIS[n])

    parts = []
    for q in range(N_CHIPS):
        pb, ps = _pack({n: shard_of(n, q) for n in WEIGHTS}, rb, jnp.bfloat16)
        parts.append(jnp.concatenate([pb, ps.astype(jnp.bfloat16)], axis=0).reshape(2, rows // 2, FLAT_W))
    partials = jnp.stack(parts, axis=1)
    half = lax.axis_index("c").astype(jnp.int32).reshape(1)
    chip_partials = _add_pairs(partials, _swap_halves(partials), half)
    g_flat = _join_halves(_sum_chips(_scatter_grads(chip_partials))).reshape(rows, FLAT_W)

    flat = [jnp.concatenate(_pack(d, rb), axis=0) for d in (w_sh, m_sh, v_sh)]
    outs = (g_flat,) + tuple(_adamw(g_flat, *flat))
    g_o, d_o, m_o, v_o = [_unpack(o[:rb], o[rb:], shard_shapes) for o in outs]
    return (loss, grads["x"], *[g_o[n] for n in WEIGHTS], *[d_o[n] for n in WEIGHTS],
            *[m_o[n] for n in WEIGHTS], *[v_o[n] for n in WEIGHTS])
```

```python
import functools
import math

import jax
import jax.numpy as jnp
from jax import lax
from jax.experimental import pallas as pl
from jax.experimental.pallas import tpu as pltpu

D_MODEL = 1024
DEPTH = 4
N_A = DEPTH // 2
N_B = DEPTH - N_A
MEM_HEADS = 4
MEM_DH = D_MODEL // 16
MEM_W = MEM_HEADS * MEM_DH
DN_DK = 128
DN_DV = 128
DN_HEADS = (3 * D_MODEL) // (4 * DN_DV)
DN_QK_W = DN_HEADS * DN_DK
DN_V_W = DN_HEADS * DN_DV
CONV_WIDTH = 4
CHUNK = 64
SWA_DH = 64
SWA_HEADS = (3 * D_MODEL) // (4 * SWA_DH)
SWA_KV_HEADS = 2
SWA_GROUP = SWA_HEADS // SWA_KV_HEADS
SWA_Q_W = SWA_HEADS * SWA_DH
SWA_KV_W = SWA_KV_HEADS * SWA_DH
WINDOW = 128
ROPE_THETA = 10000.0
MLP_HIDDEN = 4 * D_MODEL
LN_EPS = 1e-5
NORM_EPS = 1e-6
DN_ALPHA = (2.0 * DEPTH) ** 0.25
A_IN = 2 * DN_QK_W + 2 * DN_V_W + 2 * DN_HEADS + MEM_W
A_IN_PAD = 3456
QKVZ_W = 2 * DN_QK_W + 2 * DN_V_W

ADAM_LR = 0.001
ADAM_B1 = 0.9
ADAM_B2 = 0.999
ADAM_EPS = 1e-08
ADAM_WD = 0.01
ADAM_STEP = 10

N_CHIPS = 4
FLAT_W = 1024
BIG = ("a_w_in", "b_w_in", "w_kv_shared", "mem_w_kv", "w_o", "mlp_w_up", "mlp_w_down")
SMALL = ("a_conv_w", "ln_g", "ln_b", "a_A_log", "a_dt_bias", "a_norm_w", "b_sinks")
REPLICATED = ("a_A_log", "a_dt_bias", "a_norm_w", "b_sinks")
WEIGHTS = ("a_w_in", "a_conv_w", "a_A_log", "a_dt_bias", "a_norm_w", "b_w_in", "b_sinks", "w_kv_shared",
           "mem_w_kv", "w_o", "mlp_w_up", "mlp_w_down", "ln_g", "ln_b")
SHARD_AXIS = {"a_w_in": 2, "a_conv_w": 2, "b_w_in": 1, "w_kv_shared": 0, "mem_w_kv": 1, "w_o": 1,
              "mlp_w_up": 2, "mlp_w_down": 1, "ln_g": 2, "ln_b": 2}
SMALL_ROWS = 32
ROW_ALIGN = 256

MESH = pl.DeviceIdType.MESH
HBM_SPEC = pl.BlockSpec(memory_space=pltpu.HBM)
VMEM_LIMIT = 48 * 1024 * 1024


def _rows_for(shards):
    n_big = sum(math.prod(shards[n].shape) for n in BIG)
    n_small = sum(math.prod(shards[n].shape) for n in SMALL)
    assert n_small <= SMALL_ROWS * FLAT_W
    total = -(-n_big // FLAT_W) + SMALL_ROWS
    total = -(-total // (2 * ROW_ALIGN)) * (2 * ROW_ALIGN)
    return total - SMALL_ROWS, total


def _pack(shards, rb, dtype_big=jnp.float32):
    big = jnp.concatenate([shards[n].reshape(-1).astype(dtype_big) for n in BIG])
    big = jnp.pad(big, (0, rb * FLAT_W - big.shape[0])).reshape(rb, FLAT_W)
    small = jnp.concatenate([shards[n].reshape(-1).astype(jnp.float32) for n in SMALL])
    small = jnp.pad(small, (0, SMALL_ROWS * FLAT_W - small.shape[0])).reshape(SMALL_ROWS, FLAT_W)
    return big, small


def _unpack(big, small, shapes):
    out = {}
    for flat, names in ((big.reshape(-1), BIG), (small.reshape(-1), SMALL)):
        off = 0
        for n in names:
            size = math.prod(shapes[n])
            out[n] = flat[off:off + size].reshape(shapes[n])
            off += size
    return out


def _other_chips(x, y):
    return [(1 - x, y), (x, 1 - y), (1 - x, 1 - y)]


def _gather_weights(big, small):
    def body(big_ref, small_ref, obig_ref, osmall_ref, send_sems, recv_sems, pass_send_sems, pass_recv_sems, local_sems):
        x, y, c = lax.axis_index("x"), lax.axis_index("y"), lax.axis_index("c")
        me = 2 * x + y
        sibling = (x, y, 1 - c)
        pairs = ((big_ref, obig_ref), (small_ref, osmall_ref))
        local = [pltpu.make_async_copy(src, dst.at[me], local_sems.at[i]) for i, (src, dst) in enumerate(pairs)]
        for cp in local:
            cp.start()
        sends = []
        for j, (px, py) in enumerate(_other_chips(x, y)):
            for i, (src, dst) in enumerate(pairs):
                sends.append(pltpu.make_async_remote_copy(
                    src_ref=src.at[c], dst_ref=dst.at[me, c], send_sem=send_sems.at[2 * j + i],
                    recv_sem=recv_sems.at[2 * j + i], device_id=(px, py, c), device_id_type=MESH))
        for cp in sends:
            cp.start()
        passed = []
        for j, (px, py) in enumerate(_other_chips(x, y)):
            for i, (src, dst) in enumerate(pairs):
                landed = dst.at[2 * px + py, c]
                pltpu.make_async_remote_copy(
                    src_ref=src.at[c], dst_ref=landed, send_sem=send_sems.at[2 * j + i],
                    recv_sem=recv_sems.at[2 * j + i], device_id=(px, py, c), device_id_type=MESH).wait_recv()
                passed.append(pltpu.make_async_remote_copy(
                    src_ref=landed, dst_ref=landed, send_sem=pass_send_sems.at[2 * j + i],
                    recv_sem=pass_recv_sems.at[2 * j + i], device_id=sibling, device_id_type=MESH))
                passed[-1].start()
        for j, (px, py) in enumerate(_other_chips(x, y)):
            for i, (src, dst) in enumerate(pairs):
                other_half = dst.at[2 * px + py, 1 - c]
                pltpu.make_async_remote_copy(
                    src_ref=other_half, dst_ref=other_half, send_sem=pass_send_sems.at[2 * j + i],
                    recv_sem=pass_recv_sems.at[2 * j + i], device_id=sibling, device_id_type=MESH).wait_recv()
        for cp in sends + passed:
            cp.wait_send()
        for cp in local:
            cp.wait()

    dma6 = pltpu.SemaphoreType.DMA((6,))
    return pl.pallas_call(
        body, name="gather_weights",
        out_shape=(jax.ShapeDtypeStruct((N_CHIPS,) + big.shape, big.dtype),
                   jax.ShapeDtypeStruct((N_CHIPS,) + small.shape, small.dtype)),
        in_specs=[HBM_SPEC, HBM_SPEC], out_specs=(HBM_SPEC, HBM_SPEC),
        scratch_shapes=[dma6, dma6, dma6, dma6, pltpu.SemaphoreType.DMA((2,))],
    )(big, small)


def _scatter_grads(g):
    def body(g_ref, o_ref, send_sems, recv_sems, local_sem):
        x, y, c = lax.axis_index("x"), lax.axis_index("y"), lax.axis_index("c")
        me = 2 * x + y
        local = pltpu.make_async_copy(g_ref.at[me], o_ref.at[me], local_sem)
        local.start()
        sends = []
        for j, (px, py) in enumerate(_other_chips(x, y)):
            sends.append(pltpu.make_async_remote_copy(
                src_ref=g_ref.at[2 * px + py], dst_ref=o_ref.at[me], send_sem=send_sems.at[j], recv_sem=recv_sems.at[j],
                device_id=(px, py, c), device_id_type=MESH))
        for cp in sends:
            cp.start()
        for j, (px, py) in enumerate(_other_chips(x, y)):
            pltpu.make_async_remote_copy(
                src_ref=g_ref.at[me], dst_ref=o_ref.at[2 * px + py], send_sem=send_sems.at[j], recv_sem=recv_sems.at[j],
                device_id=(px, py, c), device_id_type=MESH).wait_recv()
        for cp in sends:
            cp.wait_send()
        local.wait()

    return pl.pallas_call(
        body, name="scatter_grads",
        out_shape=jax.ShapeDtypeStruct(g.shape, g.dtype),
        in_specs=[HBM_SPEC], out_specs=HBM_SPEC,
        scratch_shapes=[pltpu.SemaphoreType.DMA((3,)), pltpu.SemaphoreType.DMA((3,)), pltpu.SemaphoreType.DMA],
    )(g)


def _swap_halves(g):
    def body(g_ref, o_ref, send_sem, recv_sem):
        x, y, c = lax.axis_index("x"), lax.axis_index("y"), lax.axis_index("c")
        cp = pltpu.make_async_remote_copy(src_ref=g_ref.at[1 - c], dst_ref=o_ref, send_sem=send_sem, recv_sem=recv_sem,
                                          device_id=(x, y, 1 - c), device_id_type=MESH)
        cp.start()
        cp.wait()

    return pl.pallas_call(
        body, name="swap_halves",
        out_shape=jax.ShapeDtypeStruct(g.shape[1:], g.dtype),
        in_specs=[HBM_SPEC], out_specs=HBM_SPEC,
        scratch_shapes=[pltpu.SemaphoreType.DMA, pltpu.SemaphoreType.DMA],
    )(g)


def _join_halves(v):
    def body(v_ref, o_ref, send_sem, recv_sem, local_sem):
        x, y, c = lax.axis_index("x"), lax.axis_index("y"), lax.axis_index("c")
        local = pltpu.make_async_copy(v_ref, o_ref.at[c], local_sem)
        local.start()
        cp = pltpu.make_async_remote_copy(src_ref=v_ref, dst_ref=o_ref.at[c], send_sem=send_sem, recv_sem=recv_sem,
                                          device_id=(x, y, 1 - c), device_id_type=MESH)
        cp.start()
        cp.wait_send()
        pltpu.make_async_remote_copy(src_ref=v_ref, dst_ref=o_ref.at[1 - c], send_sem=send_sem, recv_sem=recv_sem,
                                     device_id=(x, y, 1 - c), device_id_type=MESH).wait_recv()
        local.wait()

    return pl.pallas_call(
        body, name="join_halves",
        out_shape=jax.ShapeDtypeStruct((2,) + v.shape, v.dtype),
        in_specs=[HBM_SPEC], out_specs=HBM_SPEC,
        scratch_shapes=[pltpu.SemaphoreType.DMA, pltpu.SemaphoreType.DMA, pltpu.SemaphoreType.DMA],
    )(v)


def _add_pairs(g, theirs, half):
    _, n, rows, width = g.shape
    assert rows % ROW_ALIGN == 0, rows

    def body(half_ref, g_ref, t_ref, o_ref):
        o_ref[...] = (g_ref[...].astype(jnp.float32) + t_ref[...].astype(jnp.float32)).astype(o_ref.dtype)

    blk = pl.BlockSpec((None, ROW_ALIGN, width), lambda p, i, h: (p, i, 0))
    grid_spec = pltpu.PrefetchScalarGridSpec(
        num_scalar_prefetch=1, grid=(n, rows // ROW_ALIGN),
        in_specs=[pl.BlockSpec((None, None, ROW_ALIGN, width), lambda p, i, h: (h[0], p, i, 0)), blk], out_specs=blk)
    return pl.pallas_call(
        body, name="add_pairs", grid_spec=grid_spec, out_shape=jax.ShapeDtypeStruct(theirs.shape, g.dtype),
        compiler_params=pltpu.CompilerParams(dimension_semantics=("parallel", "parallel")),
    )(half, g, theirs)


def _sum_chips(parts):
    n, rows, width = parts.shape
    assert rows % ROW_ALIGN == 0, rows

    def body(p_ref, o_ref):
        p = [p_ref[q].astype(jnp.float32) for q in range(n)]
        o_ref[...] = (p[0] + p[1]) + (p[2] + p[3])

    return pl.pallas_call(
        body, name="sum_chips", grid=(rows // ROW_ALIGN,),
        out_shape=jax.ShapeDtypeStruct((rows, width), jnp.float32),
        in_specs=[pl.BlockSpec((n, ROW_ALIGN, width), lambda i: (0, i, 0))],
        out_specs=pl.BlockSpec((ROW_ALIGN, width), lambda i: (i, 0)),
        compiler_params=pltpu.CompilerParams(dimension_semantics=("parallel",), vmem_limit_bytes=VMEM_LIMIT),
    )(parts)


def _adamw(g, w, m, v):
    rows, width = w.shape
    blk = ROW_ALIGN // 2

    def body(g_ref, w_ref, m_ref, v_ref, d_out, m_out, v_out):
        g = g_ref[...]
        m_new = ADAM_B1 * m_ref[...] + (1.0 - ADAM_B1) * g
        v_new = ADAM_B2 * v_ref[...] + (1.0 - ADAM_B2) * jnp.square(g)
        m_hat = m_new / (1.0 - ADAM_B1 ** ADAM_STEP)
        v_hat = v_new / (1.0 - ADAM_B2 ** ADAM_STEP)
        d_out[...] = -ADAM_LR * (m_hat / (jnp.sqrt(v_hat) + ADAM_EPS) + ADAM_WD * w_ref[...])
        m_out[...] = m_new
        v_out[...] = v_new

    spec = pl.BlockSpec((blk, width), lambda i: (i, 0))
    shape = jax.ShapeDtypeStruct((rows, width), jnp.float32)
    return pl.pallas_call(
        body, name="adamw", grid=(rows // blk,),
        out_shape=(shape,) * 3, in_specs=[spec] * 4, out_specs=(spec,) * 3,
        compiler_params=pltpu.CompilerParams(dimension_semantics=("parallel",), vmem_limit_bytes=VMEM_LIMIT),
    )(g, w, m, v)


def _tile(dim, pref):
    if dim <= pref:
        return dim
    for t in range(pref - pref % 128, 0, -128):
        if dim % t == 0:
            return t
    raise ValueError(f"no 128-aligned tile for {dim}")


def _matmul(a, b, *, ta=False, tb=False, name, epilogue=None, extra=None, out_dtype=jnp.float32):
    (k_a, m) = a.shape if ta else a.shape[::-1]
    (k_b, n) = b.shape[::-1] if tb else b.shape
    assert k_a == k_b, (a.shape, b.shape, ta, tb)
    k = k_a
    tm, tn, tk = _tile(m, 2048), _tile(n, 512), _tile(k, 1024)
    nk = k // tk
    a_spec = pl.BlockSpec((tk, tm), lambda i, j, l: (l, i)) if ta else pl.BlockSpec((tm, tk), lambda i, j, l: (i, l))
    b_spec = pl.BlockSpec((tn, tk), lambda i, j, l: (j, l)) if tb else pl.BlockSpec((tk, tn), lambda i, j, l: (l, j))
    o_spec = pl.BlockSpec((tm, tn), lambda i, j, l: (i, j))
    dims = (((0 if ta else 1,), (1 if tb else 0,)), ((), ()))
    has_extra = epilogue == "relu2_grad"
    assert has_extra == (extra is not None)

    def body(*refs):
        a_ref, b_ref = refs[:2]
        outs = refs[2 + has_extra:2 + has_extra + (2 if epilogue == "relu2" else 1)]
        l = pl.program_id(2)
        part = lax.dot_general(a_ref[...].astype(jnp.bfloat16), b_ref[...].astype(jnp.bfloat16), dims,
                               preferred_element_type=jnp.float32)

        def finish(acc):
            if epilogue is None:
                outs[0][...] = acc.astype(out_dtype)
            elif epilogue == "relu2":
                outs[0][...] = acc.astype(jnp.bfloat16)
                outs[1][...] = jnp.square(jnp.maximum(acc, 0.0)).astype(jnp.bfloat16)
            else:
                outs[0][...] = (acc * (2.0 * jnp.maximum(refs[2][...].astype(jnp.float32), 0.0))).astype(out_dtype)

        if nk == 1:
            finish(part)
            return
        acc_ref = refs[-1]

        @pl.when(l == 0)
        def _():
            acc_ref[...] = part

        @pl.when((l > 0) & (l < nk - 1))
        def _():
            acc_ref[...] += part

        @pl.when(l == nk - 1)
        def _():
            finish(acc_ref[...] + part)

    if epilogue == "relu2":
        out_shape = (jax.ShapeDtypeStruct((m, n), jnp.bfloat16),) * 2
        out_specs = (o_spec, o_spec)
    else:
        out_shape = jax.ShapeDtypeStruct((m, n), out_dtype)
        out_specs = o_spec
    return pl.pallas_call(
        body, name=name, grid=(m // tm, n // tn, nk), out_shape=out_shape,
        in_specs=[a_spec, b_spec] + ([o_spec] if has_extra else []), out_specs=out_specs,
        scratch_shapes=[pltpu.VMEM((tm, tn), jnp.float32)] if nk > 1 else [],
        compiler_params=pltpu.CompilerParams(dimension_semantics=("parallel", "parallel", "arbitrary"),
                                             vmem_limit_bytes=VMEM_LIMIT),
    )(*((a, b) + ((extra,) if has_extra else ())))


def _lo(x):
    return lax.stop_gradient(x.astype(jnp.bfloat16))


@jax.custom_vjp
def _linear(x, x_lo, w, slot):
    del x, slot
    return _matmul(x_lo, w, name="linear_fwd")


def _linear_fwd(x, x_lo, w, slot):
    del x, slot
    return _matmul(x_lo, w, name="linear_fwd"), (x_lo, w)


def _linear_bwd(res, dy):
    x_lo, w = res
    dy = dy.astype(jnp.bfloat16)
    dx = _matmul(dy, w, tb=True, name="linear_dx")
    dw = _matmul(x_lo, dy, ta=True, name="linear_dw")
    return dx, jnp.zeros_like(x_lo), jnp.zeros_like(w), dw


_linear.defvjp(_linear_fwd, _linear_bwd)


@jax.custom_vjp
def _mlp(h, h_lo, w_up, w_down, slot_up, slot_down):
    return _mlp_fwd(h, h_lo, w_up, w_down, slot_up, slot_down)[0]


def _mlp_fwd(h, h_lo, w_up, w_down, slot_up, slot_down):
    del h, slot_up, slot_down
    up, act = _matmul(h_lo, w_up, name="mlp_up", epilogue="relu2")
    return _matmul(act, w_down, name="mlp_down"), (h_lo, up, act, w_up, w_down)


def _mlp_bwd(res, dy):
    h_lo, up, act, w_up, w_down = res
    d_up = _matmul(dy, w_down, tb=True, name="mlp_d_up", epilogue="relu2_grad", extra=up, out_dtype=jnp.bfloat16)
    dw_down = _matmul(act, dy, ta=True, name="mlp_dw_down")
    dw_up = _matmul(h_lo, d_up, ta=True, name="mlp_dw_up")
    dh = _matmul(d_up, w_up, tb=True, name="mlp_dh")
    return dh, jnp.zeros_like(h_lo), jnp.zeros_like(w_up), jnp.zeros_like(w_down), dw_up, dw_down


_mlp.defvjp(_mlp_fwd, _mlp_bwd)


LN_ROWS = 256


def _ln_call(h, mix, g, b):
    s, d = h.shape
    tok = pl.BlockSpec((LN_ROWS, d), lambda i: (i, 0))
    vec = pl.BlockSpec((1, d), lambda i: (0, 0))
    stat = pl.BlockSpec((LN_ROWS, 1), lambda i: (i, 0))

    def body(h_ref, mix_ref, g_ref, b_ref, y_ref, ylo_ref, xhat_ref, rstd_ref):
        z = DN_ALPHA * h_ref[...] + mix_ref[...]
        mu = jnp.mean(z, axis=-1, keepdims=True)
        zc = z - mu
        rstd = lax.rsqrt(jnp.mean(jnp.square(zc), axis=-1, keepdims=True) + LN_EPS)
        xhat = zc * rstd
        y = xhat * g_ref[...] + b_ref[...]
        y_ref[...] = y
        ylo_ref[...] = y.astype(ylo_ref.dtype)
        xhat_ref[...] = xhat
        rstd_ref[...] = rstd

    sd = jax.ShapeDtypeStruct
    return pl.pallas_call(
        body, name="ln_fwd", grid=(s // LN_ROWS,),
        out_shape=(sd((s, d), jnp.float32), sd((s, d), jnp.bfloat16), sd((s, d), jnp.float32), sd((s, 1), jnp.float32)),
        in_specs=[tok, tok, vec, vec], out_specs=(tok, tok, tok, stat),
        compiler_params=pltpu.CompilerParams(dimension_semantics=("parallel",)),
    )(h, mix, g, b)


def _ln_grad_call(dy, xhat, rstd, g):
    s, d = dy.shape
    tok = pl.BlockSpec((LN_ROWS, d), lambda i: (i, 0))
    vec = pl.BlockSpec((1, d), lambda i: (0, 0))
    stat = pl.BlockSpec((LN_ROWS, 1), lambda i: (i, 0))

    def body(dy_ref, xhat_ref, rstd_ref, g_ref, dz_ref, dg_ref, db_ref):
        @pl.when(pl.program_id(0) == 0)
        def _():
            dg_ref[...] = jnp.zeros_like(dg_ref)
            db_ref[...] = jnp.zeros_like(db_ref)

        dy, xhat = dy_ref[...], xhat_ref[...]
        dyg = dy * g_ref[...]
        m1 = jnp.mean(dyg, axis=-1, keepdims=True)
        m2 = jnp.mean(dyg * xhat, axis=-1, keepdims=True)
        dz_ref[...] = rstd_ref[...] * (dyg - m1 - xhat * m2)
        dg_ref[...] += jnp.sum(dy * xhat, axis=0, keepdims=True)
        db_ref[...] += jnp.sum(dy, axis=0, keepdims=True)

    sd = jax.ShapeDtypeStruct
    return pl.pallas_call(
        body, name="ln_bwd", grid=(s // LN_ROWS,),
        out_shape=(sd((s, d), jnp.float32), sd((1, d), jnp.float32), sd((1, d), jnp.float32)),
        in_specs=[tok, tok, stat, vec], out_specs=(tok, vec, vec),
        compiler_params=pltpu.CompilerParams(dimension_semantics=("arbitrary",)),
    )(dy, xhat, rstd, g)


@jax.custom_vjp
def _ln_res(h, mix, g, b):
    return _ln_call(h, mix, g, b)[:2]


def _ln_res_fwd(h, mix, g, b):
    y, y_lo, xhat, rstd = _ln_call(h, mix, g, b)
    return (y, y_lo), (xhat, rstd, g)


def _ln_res_bwd(res, cts):
    xhat, rstd, g = res
    dz, dg, db = _ln_grad_call(cts[0], xhat, rstd, g)
    return DN_ALPHA * dz, dz, dg, db


_ln_res.defvjp(_ln_res_fwd, _ln_res_bwd)


MXU_DTYPE = jnp.bfloat16
DN_CB = 8
DN_GROUP = 8
DN_SCALE = DN_DK ** -0.5


def _dot(a, b, ca=1, cb=0):
    return lax.dot_general(a.astype(MXU_DTYPE), b.astype(MXU_DTYPE), (((ca,), (cb,)), ((), ())),
                           preferred_element_type=jnp.float32)


def _chunk_masks():
    row = lax.broadcasted_iota(jnp.int32, (CHUNK, CHUNK), 0)
    col = lax.broadcasted_iota(jnp.int32, (CHUNK, CHUNK), 1)
    return row >= col, row > col, row == col


def _to_col(row_vec):
    _, _, eye = _chunk_masks()
    return jnp.sum(jnp.where(eye, jnp.broadcast_to(row_vec, (CHUNK, CHUNK)), 0.0), axis=1, keepdims=True)


def _to_row(col_vec):
    _, _, eye = _chunk_masks()
    return jnp.sum(jnp.where(eye, jnp.broadcast_to(col_vec, (CHUNK, CHUNK)), 0.0), axis=0, keepdims=True)


def _last_row(col_vec):
    last = lax.broadcasted_iota(jnp.int32, (CHUNK, 1), 0) == CHUNK - 1
    return jnp.sum(jnp.where(last, col_vec, 0.0), axis=0, keepdims=True), last


def _chunk_terms(q, k, beta, gcc, gcr):
    incl, strict, _ = _chunk_masks()
    decay = jnp.where(incl, jnp.exp(jnp.minimum(gcc - gcr, 0.0)), 0.0)
    kb = k * beta
    lmat = jnp.where(strict, _dot(kb, k, 1, 1) * decay, 0.0)
    intra = jnp.where(incl, _dot(q, k, 1, 1) * decay, 0.0)
    return decay, kb, lmat, intra


def _dot3(a, b, ca=1, cb=0):
    if MXU_DTYPE == jnp.float32:
        return _dot(a, b, ca, cb)
    a_hi, b_hi = a.astype(MXU_DTYPE), b.astype(MXU_DTYPE)
    a_lo = (a - a_hi.astype(jnp.float32)).astype(MXU_DTYPE)
    b_lo = (b - b_hi.astype(jnp.float32)).astype(MXU_DTYPE)
    return _dot(a_hi, b_hi, ca, cb) + (_dot(a_hi, b_lo, ca, cb) + _dot(a_lo, b_hi, ca, cb))


def _unit_lower_inverse(lmats):
    _, _, eye = _chunk_masks()
    ident = jnp.where(eye, 1.0, 0.0)
    ts = [ident - m for m in lmats]
    ps = [_dot(m, m) for m in lmats]
    for _ in range(4):
        ts = [t + _dot(t, p) for t, p in zip(ts, ps)]
        ps = [_dot(p, p) for p in ps]
    ts = [t + _dot(t, p) for t, p in zip(ts, ps)]
    resids = [(t - ident) + _dot3(m, t) for m, t in zip(lmats, ts)]
    return [t - _dot(t, r) for t, r in zip(ts, resids)]


def _dn_specs(n_chunks):
    tok = pl.BlockSpec((DN_CB * CHUNK, DN_DK), lambda h, n: (n, h))
    rowv = pl.BlockSpec((None, DN_CB, CHUNK), lambda h, n: (h, n, 0))
    sq = pl.BlockSpec((None, DN_CB, CHUNK, CHUNK), lambda h, n: (h, n, 0, 0))
    lane = pl.BlockSpec((None, DN_CB, 1, DN_DV), lambda h, n: (h, n, 0, 0))
    planes = [pl.BlockSpec((None, DN_CB * CHUNK, DN_DK), functools.partial(lambda h, n, p: (p, n, h), p=p))
              for p in range(3)]
    return tok, rowv, sq, lane, planes


def _dn_prep(qkv, beta, gc):
    s = qkv.shape[1]
    n_chunks = s // CHUNK
    tok, rowv, sq, lane, planes = _dn_specs(n_chunks)
    tok_shape = qkv.shape[1:]

    def body(q_ref, k_ref, v_ref, beta_ref, gc_ref, u_ref, w_ref, qd_ref, kd_ref, intra_ref, t_ref, cd_ref):
        for c0 in range(0, DN_CB, DN_GROUP):
            chunks = range(c0, c0 + DN_GROUP)
            rhs, lmats = [], []
            for c in chunks:
                rows = pl.ds(c * CHUNK, CHUNK)
                q_c, k_c, v_c = q_ref[rows, :] * DN_SCALE, k_ref[rows, :], v_ref[rows, :]
                gcr_c = gc_ref[pl.ds(c, 1), :]
                beta_c, gcc_c = _to_col(beta_ref[pl.ds(c, 1), :]), _to_col(gcr_c)
                _, kb, lmat, intra = _chunk_terms(q_c, k_c, beta_c, gcc_c, gcr_c)
                eg = jnp.exp(gcc_c)
                g_last, _ = _last_row(gcc_c)
                qd_ref[rows, :] = (q_c * eg).astype(qd_ref.dtype)
                kd_ref[rows, :] = (k_c * jnp.exp(g_last - gcc_c)).astype(kd_ref.dtype)
                intra_ref[c] = intra.astype(intra_ref.dtype)
                cd_ref[c] = jnp.broadcast_to(jnp.exp(g_last), (1, DN_DV))
                rhs.append(jnp.concatenate([v_c * beta_c, kb * eg], axis=1))
                lmats.append(lmat)
            ts = _unit_lower_inverse(lmats)
            sols = [_dot3(t, r) for t, r in zip(ts, rhs)]
            for c, t, sol in zip(chunks, ts, sols):
                rows = pl.ds(c * CHUNK, CHUNK)
                t_ref[c] = t
                u_ref[rows, :] = sol[:, :DN_DV]
                w_ref[rows, :] = sol[:, DN_DV:].astype(w_ref.dtype)

    f32, mx = jnp.float32, MXU_DTYPE
    sd = jax.ShapeDtypeStruct
    return pl.pallas_call(
        body, name="dn_prep", grid=(DN_HEADS, n_chunks // DN_CB),
        out_shape=(sd(tok_shape, f32), sd(tok_shape, mx), sd(tok_shape, mx), sd(tok_shape, mx),
                   sd((DN_HEADS, n_chunks, CHUNK, CHUNK), mx), sd((DN_HEADS, n_chunks, CHUNK, CHUNK), f32),
                   sd((DN_HEADS, n_chunks, 1, DN_DV), f32)),
        in_specs=planes + [rowv, rowv], out_specs=(tok, tok, tok, tok, sq, sq, lane),
        compiler_params=pltpu.CompilerParams(dimension_semantics=("parallel", "parallel")),
    )(qkv, qkv, qkv, beta, gc)


def _dn_scan(u, w, qd, kd, intra, cd):
    s, width = u.shape
    n_chunks = s // CHUNK
    tok = pl.BlockSpec((CHUNK, width), lambda n: (n, 0))
    sq = pl.BlockSpec((DN_HEADS, None, CHUNK, CHUNK), lambda n: (0, n, 0, 0))
    lane = pl.BlockSpec((DN_HEADS, None, 1, DN_DV), lambda n: (0, n, 0, 0))
    st = pl.BlockSpec((DN_HEADS, None, DN_DK, DN_DV), lambda n: (0, n, 0, 0))

    def body(u_ref, w_ref, qd_ref, kd_ref, intra_ref, cd_ref, o_ref, vn_ref, st_ref, state):
        @pl.when(pl.program_id(0) == 0)
        def _():
            state[...] = jnp.zeros_like(state)

        heads = range(DN_HEADS)
        cols = [pl.ds(h * DN_DK, DN_DK) for h in heads]
        s_f = [state[h] for h in heads]
        s_mx = [s.astype(MXU_DTYPE) for s in s_f]
        for h in heads:
            st_ref[h] = s_mx[h]
        ws = [_dot(w_ref[:, cols[h]], s_mx[h]) for h in heads]
        qs = [_dot(qd_ref[:, cols[h]], s_mx[h]) for h in heads]
        v_new = [(u_ref[:, cols[h]] - ws[h]).astype(MXU_DTYPE) for h in heads]
        inner = [_dot(intra_ref[h], v_new[h]) for h in heads]
        outer = [_dot(kd_ref[:, cols[h]], v_new[h], 0, 0) for h in heads]
        for h in heads:
            vn_ref[:, cols[h]] = v_new[h]
            o_ref[:, cols[h]] = qs[h] + inner[h]
            state[h] = s_f[h] * cd_ref[h] + outer[h]

    sd = jax.ShapeDtypeStruct
    return pl.pallas_call(
        body, name="dn_scan", grid=(n_chunks,),
        out_shape=(sd(u.shape, jnp.float32), sd(u.shape, MXU_DTYPE),
                   sd((DN_HEADS, n_chunks, DN_DK, DN_DV), MXU_DTYPE)),
        in_specs=[tok, tok, tok, tok, sq, lane], out_specs=(tok, tok, st),
        scratch_shapes=[pltpu.VMEM((DN_HEADS, DN_DK, DN_DV), jnp.float32)],
        compiler_params=pltpu.CompilerParams(dimension_semantics=("arbitrary",)),
    )(u, w, qd, kd, intra, cd)


def _dn_bwd_scan(do, w, qd, kd, intra, cd, vn, st):
    s, width = do.shape
    n_chunks = s // CHUNK
    last = n_chunks - 1
    tok = pl.BlockSpec((CHUNK, width), lambda n: (last - n, 0))
    sq = pl.BlockSpec((DN_HEADS, None, CHUNK, CHUNK), lambda n: (0, last - n, 0, 0))
    lane = pl.BlockSpec((DN_HEADS, None, 1, DN_DV), lambda n: (0, last - n, 0, 0))
    stt = pl.BlockSpec((DN_HEADS, None, DN_DK, DN_DV), lambda n: (0, last - n, 0, 0))

    def body(do_ref, w_ref, qd_ref, kd_ref, intra_ref, cd_ref, vn_ref, st_ref,
             du_ref, dw_ref, dqd_ref, dkd_ref, dintra_ref, dgl_ref, dstate):
        @pl.when(pl.program_id(0) == 0)
        def _():
            dstate[...] = jnp.zeros_like(dstate)

        heads = range(DN_HEADS)
        cols = [pl.ds(h * DN_DK, DN_DK) for h in heads]
        ds_f = [dstate[h] for h in heads]
        ds_mx = [d.astype(MXU_DTYPE) for d in ds_f]
        do_h = [do_ref[:, cols[h]].astype(MXU_DTYPE) for h in heads]
        dv_a = [_dot(intra_ref[h], do_h[h], 0, 0) for h in heads]
        dv_b = [_dot(kd_ref[:, cols[h]], ds_mx[h]) for h in heads]
        d_intra = [_dot(do_h[h], vn_ref[:, cols[h]], 1, 1) for h in heads]
        d_qd = [_dot(do_h[h], st_ref[h], 1, 1) for h in heads]
        d_kd = [_dot(vn_ref[:, cols[h]], ds_mx[h], 1, 1) for h in heads]
        ds_q = [_dot(qd_ref[:, cols[h]], do_h[h], 0, 0) for h in heads]
        dv_new = [dv_a[h] + dv_b[h] for h in heads]
        dv_mx = [d.astype(MXU_DTYPE) for d in dv_new]
        d_w = [_dot(dv_mx[h], st_ref[h], 1, 1) for h in heads]
        ds_w = [_dot(w_ref[:, cols[h]], dv_mx[h], 0, 0) for h in heads]
        for h in heads:
            du_ref[:, cols[h]] = dv_new[h]
            dintra_ref[h] = d_intra[h]
            dqd_ref[:, cols[h]] = d_qd[h]
            dkd_ref[:, cols[h]] = d_kd[h]
            dw_ref[:, cols[h]] = -d_w[h]
            cd_h = cd_ref[h]
            dcd = jnp.sum(jnp.sum(st_ref[h].astype(jnp.float32) * ds_f[h], axis=1, keepdims=True), axis=0,
                          keepdims=True)
            dgl_ref[h] = dcd * cd_h
            dstate[h] = ds_q[h] + ds_f[h] * cd_h - ds_w[h]

    sd = jax.ShapeDtypeStruct
    f32 = jnp.float32
    return pl.pallas_call(
        body, name="dn_bwd_scan", grid=(n_chunks,),
        out_shape=(sd(do.shape, f32), sd(do.shape, f32), sd(do.shape, f32), sd(do.shape, f32),
                   sd((DN_HEADS, n_chunks, CHUNK, CHUNK), f32), sd((DN_HEADS, n_chunks, 1, DN_DV), f32)),
        in_specs=[tok, tok, tok, tok, sq, lane, tok, stt], out_specs=(tok, tok, tok, tok, sq, lane),
        scratch_shapes=[pltpu.VMEM((DN_HEADS, DN_DK, DN_DV), f32)],
        compiler_params=pltpu.CompilerParams(dimension_semantics=("arbitrary",)),
    )(do, w, qd, kd, intra, cd, vn, st)


def _dn_bwd_chunks(qkv, beta, gc, t, u, w, du, dw, dqd, dkd, dintra, dgl):
    s = qkv.shape[1]
    n_chunks = s // CHUNK
    tok, rowv, sq, lane, planes = _dn_specs(n_chunks)
    all_planes = pl.BlockSpec((3, DN_CB * CHUNK, DN_DK), lambda h, n: (0, n, h))

    def body(q_ref, k_ref, v_ref, beta_ref, gc_ref, t_ref, u_ref, w_ref, du_ref, dw_ref, dqd_ref, dkd_ref,
             dintra_ref, dgl_ref, dqkv_ref, dbeta_ref, dgc_ref):
        incl, strict, _ = _chunk_masks()

        def first(c):
            rows = pl.ds(c * CHUNK, CHUNK)
            q_c, k_c = q_ref[rows, :] * DN_SCALE, k_ref[rows, :]
            gcr_c = gc_ref[pl.ds(c, 1), :]
            beta_c, gcc_c = _to_col(beta_ref[pl.ds(c, 1), :]), _to_col(gcr_c)
            decay, kb, lmat, intra = _chunk_terms(q_c, k_c, beta_c, gcc_c, gcr_c)
            d_sol = jnp.concatenate([du_ref[rows, :], dw_ref[rows, :]], axis=1)
            d_rhs = _dot3(t_ref[c], d_sol, 0, 0)
            return dict(rows=rows, q=q_c, k=k_c, beta=beta_c, gcc=gcc_c, decay=decay, kb=kb, lmat=lmat, intra=intra,
                        d_rhs=d_rhs)

        def second(c, e):
            sol = jnp.concatenate([u_ref[e["rows"], :], w_ref[e["rows"], :].astype(jnp.float32)], axis=1)
            e["d_l"] = jnp.where(strict, -_dot(e["d_rhs"], sol, 1, 1), 0.0)
            e["d_intra"] = jnp.where(incl, dintra_ref[c], 0.0)
            d_qk = e["d_intra"] * e["decay"]
            e["dq"] = _dot(d_qk, e["k"])
            e["dk"] = _dot(d_qk, e["q"], 0, 0)

        def third(e):
            d_a = e["d_l"] * e["decay"]
            e["dkb"] = _dot(d_a, e["k"])
            e["dk"] = e["dk"] + _dot(d_a, e["kb"], 0, 0)

        def last(c, e):
            rows, q_c, k_c, beta_c, gcc_c = e["rows"], e["q"], e["k"], e["beta"], e["gcc"]
            v_c = v_ref[rows, :]
            eg = jnp.exp(gcc_c)
            g_last, is_last = _last_row(gcc_c)
            e_rev = jnp.exp(g_last - gcc_c)
            d_rhs_u, d_rhs_w = e["d_rhs"][:, :DN_DV], e["d_rhs"][:, DN_DV:]
            dqkv_ref[2, rows, :] = d_rhs_u * beta_c
            dbeta = jnp.sum(d_rhs_u * v_c, axis=1, keepdims=True)
            dkb = e["dkb"] + d_rhs_w * eg
            dgc = jnp.sum(d_rhs_w * e["kb"] * eg, axis=1, keepdims=True)
            m1 = e["d_l"] * e["lmat"]
            dgc = dgc + jnp.sum(m1, axis=1, keepdims=True)
            dgr = -jnp.sum(m1, axis=0, keepdims=True)
            m2 = e["d_intra"] * e["intra"]
            dgc = dgc + jnp.sum(m2, axis=1, keepdims=True)
            dgr = dgr - jnp.sum(m2, axis=0, keepdims=True)
            dqd = dqd_ref[rows, :]
            dq = e["dq"] + dqd * eg
            dgc = dgc + jnp.sum(dqd * q_c * eg, axis=1, keepdims=True)
            dkd = dkd_ref[rows, :]
            dk = e["dk"] + dkd * e_rev
            tk = jnp.sum(dkd * k_c * e_rev, axis=1, keepdims=True)
            dgc = dgc - tk
            d_last = dgl_ref[c][:, :1] + jnp.sum(tk, axis=0, keepdims=True)
            dgc = dgc + jnp.where(is_last, d_last, 0.0)
            dk = dk + dkb * beta_c
            dbeta = dbeta + jnp.sum(dkb * k_c, axis=1, keepdims=True)
            dqkv_ref[0, rows, :] = dq * DN_SCALE
            dqkv_ref[1, rows, :] = dk
            dbeta_ref[pl.ds(c, 1), :] = _to_row(dbeta)
            dgc_ref[pl.ds(c, 1), :] = _to_row(dgc) + dgr

        for c0 in range(0, DN_CB, DN_GROUP):
            chunks = range(c0, c0 + DN_GROUP)
            env = [first(c) for c in chunks]
            for c, e in zip(chunks, env):
                second(c, e)
            for e in env:
                third(e)
            for c, e in zip(chunks, env):
                last(c, e)

    sd = jax.ShapeDtypeStruct
    f32 = jnp.float32
    return pl.pallas_call(
        body, name="dn_bwd_chunks", grid=(DN_HEADS, n_chunks // DN_CB),
        out_shape=(sd(qkv.shape, f32), sd(beta.shape, f32), sd(gc.shape, f32)),
        in_specs=planes + [rowv, rowv, sq, tok, tok, tok, tok, tok, tok, sq, lane],
        out_specs=(all_planes, rowv, rowv),
        compiler_params=pltpu.CompilerParams(dimension_semantics=("parallel", "parallel")),
    )(qkv, qkv, qkv, beta, gc, t, u, w, du, dw, dqd, dkd, dintra, dgl)


@jax.custom_vjp
def _delta_rule_op(qkv, beta, gc):
    return _delta_rule_fwd(qkv, beta, gc)[0]


def _delta_rule_fwd(qkv, beta, gc):
    u, w, qd, kd, intra, t, cd = _dn_prep(qkv, beta, gc)
    out, vn, st = _dn_scan(u, w, qd, kd, intra, cd)
    return out, (qkv, beta, gc, u, w, qd, kd, intra, t, cd, vn, st)


def _delta_rule_bwd(res, do):
    qkv, beta, gc, u, w, qd, kd, intra, t, cd, vn, st = res
    du, dw, dqd, dkd, dintra, dgl = _dn_bwd_scan(do, w, qd, kd, intra, cd, vn, st)
    return _dn_bwd_chunks(qkv, beta, gc, t, u, w, du, dw, dqd, dkd, dintra, dgl)


_delta_rule_op.defvjp(_delta_rule_fwd, _delta_rule_bwd)


def _gated_delta_rule(qkv, g, beta):
    s, h = g.shape
    n_chunks = s // CHUNK
    gc = jnp.cumsum(g.T.reshape(h, n_chunks, CHUNK), axis=-1)
    return _delta_rule_op(qkv, beta.T.reshape(h, n_chunks, CHUNK), gc)


PRE_ROWS = 256
HALO = 8
PRE_W = DN_QK_W


def _shift_rows(xs, k):
    return pltpu.roll(xs, k, 0)[HALO:]


def _conv_silu(x_ref, halo_ref, w_ref, first_block):
    halo = jnp.where(first_block, 0.0, halo_ref[...])
    xs = jnp.concatenate([halo, x_ref[...]], axis=0)
    taps = [_shift_rows(xs, CONV_WIDTH - 1 - j) for j in range(CONV_WIDTH - 1)] + [x_ref[...]]
    conv = sum(w_ref[pl.ds(j, 1), :] * taps[j] for j in range(CONV_WIDTH))
    return conv, jax.nn.sigmoid(conv), taps


def _pre_specs():
    blk = pl.BlockSpec((PRE_ROWS, PRE_W), lambda j, i: (i, j))
    prev = pl.BlockSpec((HALO, PRE_W), lambda j, i: (jnp.maximum(i * (PRE_ROWS // HALO) - 1, 0), j))
    wts = pl.BlockSpec((CONV_WIDTH, PRE_W), lambda j, i: (0, j))
    plane = pl.BlockSpec((None, PRE_ROWS, PRE_W), lambda j, i: (j, i, 0))
    return blk, prev, wts, plane


def _pre_fwd_call(x, conv_w):
    s = x.shape[0]
    blk, prev, wts, plane = _pre_specs()

    def body(x_ref, halo_ref, w_ref, o_ref):
        conv, sig, _ = _conv_silu(x_ref, halo_ref, w_ref, pl.program_id(1) == 0)
        act = conv * sig
        is_v = pl.program_id(0) == 2
        for h in range(DN_HEADS):
            cols = slice(h * DN_DK, (h + 1) * DN_DK)
            a_h = act[:, cols]
            r = lax.rsqrt(jnp.sum(a_h * a_h, axis=-1, keepdims=True) + NORM_EPS)
            o_ref[:, cols] = a_h * jnp.where(is_v, 1.0, r)

    return pl.pallas_call(
        body, name="pre_fwd", grid=(3, s // PRE_ROWS),
        out_shape=jax.ShapeDtypeStruct((3, s, PRE_W), jnp.float32),
        in_specs=[blk, prev, wts], out_specs=plane,
        compiler_params=pltpu.CompilerParams(dimension_semantics=("parallel", "parallel")),
    )(x, x, conv_w)


def _pre_bwd_act_call(x, conv_w, d_out):
    s = x.shape[0]
    blk, prev, wts, plane = _pre_specs()

    def body(x_ref, halo_ref, w_ref, do_ref, dc_ref):
        conv, sig, _ = _conv_silu(x_ref, halo_ref, w_ref, pl.program_id(1) == 0)
        act = conv * sig
        d_silu = sig * (1.0 + conv * (1.0 - sig))
        is_v = pl.program_id(0) == 2
        for h in range(DN_HEADS):
            cols = slice(h * DN_DK, (h + 1) * DN_DK)
            a_h, do_h = act[:, cols], do_ref[:, cols]
            r = lax.rsqrt(jnp.sum(a_h * a_h, axis=-1, keepdims=True) + NORM_EPS)
            n_h = a_h * r
            d_norm = r * (do_h - n_h * jnp.sum(do_h * n_h, axis=-1, keepdims=True))
            dc_ref[:, cols] = jnp.where(is_v, do_h, d_norm) * d_silu[:, cols]

    return pl.pallas_call(
        body, name="pre_bwd_act", grid=(3, s // PRE_ROWS),
        out_shape=jax.ShapeDtypeStruct(x.shape, jnp.float32),
        in_specs=[blk, prev, wts, plane], out_specs=blk,
        compiler_params=pltpu.CompilerParams(dimension_semantics=("parallel", "parallel")),
    )(x, x, conv_w, d_out)


def _pre_bwd_conv_call(x, conv_w, dc):
    s = x.shape[0]
    n_blocks = s // PRE_ROWS
    blk, prev, wts, _ = _pre_specs()
    nxt = pl.BlockSpec((HALO, PRE_W), lambda j, i: (jnp.minimum((i + 1) * (PRE_ROWS // HALO), s // HALO - 1), j))

    def body(x_ref, halo_ref, w_ref, dc_ref, dcn_ref, dx_ref, dw_ref):
        i = pl.program_id(1)

        @pl.when(i == 0)
        def _():
            dw_ref[...] = jnp.zeros_like(dw_ref)

        dcv = dc_ref[...]
        ahead = jnp.concatenate([dcv, jnp.where(i == n_blocks - 1, 0.0, dcn_ref[...])], axis=0)
        dx = w_ref[pl.ds(CONV_WIDTH - 1, 1), :] * dcv
        for j in range(CONV_WIDTH - 1):
            k = CONV_WIDTH - 1 - j
            dx = dx + w_ref[pl.ds(j, 1), :] * pltpu.roll(ahead, PRE_ROWS + HALO - k, 0)[:PRE_ROWS]
        dx_ref[...] = dx
        halo = jnp.where(i == 0, 0.0, halo_ref[...])
        xs = jnp.concatenate([halo, x_ref[...]], axis=0)
        for j in range(CONV_WIDTH):
            tap = x_ref[...] if j == CONV_WIDTH - 1 else _shift_rows(xs, CONV_WIDTH - 1 - j)
            dw_ref[pl.ds(j, 1), :] += jnp.sum(dcv * tap, axis=0, keepdims=True)

    sd = jax.ShapeDtypeStruct
    return pl.pallas_call(
        body, name="pre_bwd_conv", grid=(3, n_blocks),
        out_shape=(sd(x.shape, jnp.float32), sd(conv_w.shape, jnp.float32)),
        in_specs=[blk, prev, wts, blk, nxt], out_specs=(blk, wts),
        compiler_params=pltpu.CompilerParams(dimension_semantics=("parallel", "arbitrary")),
    )(x, x, conv_w, dc, dc)


@jax.custom_vjp
def _pre_op(x, conv_w):
    return _pre_fwd_call(x, conv_w)


def _pre_op_fwd(x, conv_w):
    return _pre_fwd_call(x, conv_w), (x, conv_w)


def _pre_op_bwd(res, d_out):
    x, conv_w = res
    return _pre_bwd_conv_call(x, conv_w, _pre_bwd_act_call(x, conv_w, d_out))


_pre_op.defvjp(_pre_op_fwd, _pre_op_bwd)


def _project(h, h_lo, w, slot):
    b, s, d = h.shape
    return _linear(h.reshape(b * s, d), h_lo.reshape(b * s, d), w, slot).reshape(b, s, w.shape[1])


def _rope_table(positions, dh):
    inv_freq = ROPE_THETA ** (-jnp.arange(0, dh, 2, dtype=jnp.float32) / dh)
    ang = positions.astype(jnp.float32)[:, None] * inv_freq
    reps = 128 // (dh // 2)
    return jnp.concatenate([jnp.tile(jnp.cos(ang), (1, reps)), jnp.tile(jnp.sin(ang), (1, reps))], axis=-1)


GATE_ROWS = 512


def _gate_terms(o_h, z_h):
    r = lax.rsqrt(jnp.mean(o_h * o_h, axis=-1, keepdims=True) + NORM_EPS)
    sig = jax.nn.sigmoid(z_h)
    return r, o_h * r, sig, z_h * sig


def _gate_fwd_call(o, z, nw):
    tok = pl.BlockSpec((GATE_ROWS, DN_V_W), lambda i: (i, 0))
    vec = pl.BlockSpec((1, DN_DV), lambda i: (0, 0))

    def body(o_ref, z_ref, nw_ref, y_ref):
        for h in range(DN_HEADS):
            cols = pl.ds(h * DN_DV, DN_DV)
            _, n_h, _, g_h = _gate_terms(o_ref[:, cols], z_ref[:, cols])
            y_ref[:, cols] = n_h * nw_ref[...] * g_h

    return pl.pallas_call(
        body, name="gate_fwd", grid=(o.shape[0] // GATE_ROWS,),
        out_shape=jax.ShapeDtypeStruct(o.shape, jnp.float32), in_specs=[tok, tok, vec], out_specs=tok,
        compiler_params=pltpu.CompilerParams(dimension_semantics=("parallel",)),
    )(o, z, nw)


def _gate_bwd_call(o, z, nw, dy):
    tok = pl.BlockSpec((GATE_ROWS, DN_V_W), lambda i: (i, 0))
    vec = pl.BlockSpec((1, DN_DV), lambda i: (0, 0))

    def body(o_ref, z_ref, nw_ref, dy_ref, do_ref, dz_ref, dnw_ref):
        @pl.when(pl.program_id(0) == 0)
        def _():
            dnw_ref[...] = jnp.zeros_like(dnw_ref)

        for h in range(DN_HEADS):
            cols = pl.ds(h * DN_DV, DN_DV)
            z_h, dy_h = z_ref[:, cols], dy_ref[:, cols]
            r, n_h, sig, g_h = _gate_terms(o_ref[:, cols], z_h)
            dz_ref[:, cols] = dy_h * n_h * nw_ref[...] * (sig * (1.0 + z_h * (1.0 - sig)))
            dn = dy_h * nw_ref[...] * g_h
            do_ref[:, cols] = r * (dn - n_h * jnp.mean(dn * n_h, axis=-1, keepdims=True))
            dnw_ref[...] += jnp.sum(dy_h * n_h * g_h, axis=0, keepdims=True)

    sd = jax.ShapeDtypeStruct
    return pl.pallas_call(
        body, name="gate_bwd", grid=(o.shape[0] // GATE_ROWS,),
        out_shape=(sd(o.shape, jnp.float32), sd(o.shape, jnp.float32), sd(nw.shape, jnp.float32)),
        in_specs=[tok, tok, vec, tok], out_specs=(tok, tok, vec),
        compiler_params=pltpu.CompilerParams(dimension_semantics=("arbitrary",)),
    )(o, z, nw, dy)


@jax.custom_vjp
def _gate_op(o, z, nw):
    return _gate_fwd_call(o, z, nw)


def _gate_op_fwd(o, z, nw):
    return _gate_fwd_call(o, z, nw), (o, z, nw)


def _gate_op_bwd(res, dy):
    return _gate_bwd_call(*res, dy)


_gate_op.defvjp(_gate_op_fwd, _gate_op_bwd)


_MASKED = -1e30


def _swa_probs(qs, k_h, sinks, valid):
    ss = [jnp.where(valid, _dot(q_h, k_h, 1, 1) * (SWA_DH ** -0.5), _MASKED) for q_h in qs]
    ms = [jnp.maximum(jnp.max(s, axis=-1, keepdims=True), sink) for s, sink in zip(ss, sinks)]
    ps = [jnp.exp(s - m) for s, m in zip(ss, ms)]
    es = [jnp.exp(sink - m) for sink, m in zip(sinks, ms)]
    invs = [1.0 / (jnp.sum(p, axis=-1, keepdims=True) + e) for p, e in zip(ps, es)]
    return [p * inv for p, inv in zip(ps, invs)], [e * inv for e, inv in zip(es, invs)]


def _swa_valid(n):
    qi = lax.broadcasted_iota(jnp.int32, (WINDOW, 2 * WINDOW), 0)
    kj = lax.broadcasted_iota(jnp.int32, (WINDOW, 2 * WINDOW), 1)
    diff = qi + WINDOW - kj
    return (diff >= 0) & (diff < WINDOW) & ((kj >= WINDOW) | (n > 0))


def _rotate_half(x, transpose=False):
    half = SWA_DH // 2
    lower = lax.broadcasted_iota(jnp.int32, x.shape, 1) % SWA_DH < half
    ahead, behind = pltpu.roll(x, 128 - half, 1), pltpu.roll(x, half, 1)
    return jnp.where(lower, ahead, -behind) if transpose else jnp.where(lower, -ahead, behind)


def _rope(x, table):
    return x * table[:, :128] + _rotate_half(x) * table[:, 128:]


def _unrope(dy, table):
    return dy * table[:, :128] + _rotate_half(dy * table[:, 128:], transpose=True)


def _swa_specs():
    qs = pl.BlockSpec((WINDOW, SWA_Q_W), lambda n: (n, 0))
    first = lambda n: jnp.maximum(n - 1, 0)
    kv = [pl.BlockSpec((WINDOW, SWA_KV_W), lambda n: (first(n), 0)), pl.BlockSpec((WINDOW, SWA_KV_W), lambda n: (n, 0)),
          pl.BlockSpec((WINDOW, SWA_KV_W), lambda n: (first(n), 1)), pl.BlockSpec((WINDOW, SWA_KV_W), lambda n: (n, 1))]
    tables = [pl.BlockSpec((WINDOW, 256), lambda n: (first(n), 0)), pl.BlockSpec((WINDOW, 256), lambda n: (n, 0))]
    cur = pl.BlockSpec((WINDOW, SWA_KV_W), lambda n: (n, 0))
    sk = pl.BlockSpec((SWA_HEADS, 1, 128), lambda n: (0, 0, 0))
    return qs, kv, tables, cur, sk


def _swa_load(q_ref, kp_ref, kc_ref, vp_ref, vc_ref, tp_ref, tc_ref):
    table_kk = jnp.concatenate([tp_ref[...], tc_ref[...]], axis=0)
    kk = _rope(jnp.concatenate([kp_ref[...], kc_ref[...]], axis=0), table_kk)
    vv = jnp.concatenate([vp_ref[...], vc_ref[...]], axis=0)
    q_rot = []
    for b in range(SWA_Q_W // 128):
        pair = _rope(q_ref[:, pl.ds(b * 128, 128)], tc_ref[...])
        q_rot += [pair[:, :SWA_DH], pair[:, SWA_DH:]]
    split = lambda t: [t[:, hkv * SWA_DH:(hkv + 1) * SWA_DH] for hkv in range(SWA_KV_HEADS)]
    return q_rot, split(kk), split(vv), table_kk


def _swa_fwd_call(q, kv, table, sinks):
    qs, kvs, tables, _, sk = _swa_specs()

    def body(q_ref, kp_ref, kc_ref, vp_ref, vc_ref, tp_ref, tc_ref, sink_ref, o_ref):
        valid = _swa_valid(pl.program_id(0))
        q_rot, kk, vv, _ = _swa_load(q_ref, kp_ref, kc_ref, vp_ref, vc_ref, tp_ref, tc_ref)
        for hkv in range(SWA_KV_HEADS):
            heads = range(hkv * SWA_GROUP, (hkv + 1) * SWA_GROUP)
            probs, _ = _swa_probs([q_rot[h] for h in heads], kk[hkv], [sink_ref[h][:, :1] for h in heads], valid)
            outs = [_dot(p, vv[hkv]) for p in probs]
            for h, o in zip(heads, outs):
                o_ref[:, pl.ds(h * SWA_DH, SWA_DH)] = o

    return pl.pallas_call(
        body, name="swa_fwd", grid=(q.shape[0] // WINDOW,),
        out_shape=jax.ShapeDtypeStruct(q.shape, jnp.float32),
        in_specs=[qs] + kvs + tables + [sk], out_specs=qs,
        compiler_params=pltpu.CompilerParams(dimension_semantics=("parallel",)),
    )(q, kv, kv, kv, kv, table, table, sinks)


def _swa_bwd_call(q, kv, table, sinks, do):
    qs, kvs, tables, cur, sk = _swa_specs()

    def body(q_ref, kp_ref, kc_ref, vp_ref, vc_ref, tp_ref, tc_ref, sink_ref, do_ref,
             dq_ref, dkc_ref, dkp_ref, dvc_ref, dvp_ref, ds_ref):
        @pl.when(pl.program_id(0) == 0)
        def _():
            ds_ref[...] = jnp.zeros_like(ds_ref)

        valid = _swa_valid(pl.program_id(0))
        q_rot, kk, vv, table_kk = _swa_load(q_ref, kp_ref, kc_ref, vp_ref, vc_ref, tp_ref, tc_ref)
        lane0 = lax.broadcasted_iota(jnp.int32, (1, 128), 1) == 0
        dq_heads, dk_heads, dv_heads = [], [], []
        for hkv in range(SWA_KV_HEADS):
            k_h, v_h = kk[hkv], vv[hkv]
            heads = range(hkv * SWA_GROUP, (hkv + 1) * SWA_GROUP)
            q_hs = [q_rot[h] for h in heads]
            dos = [do_ref[:, pl.ds(h * SWA_DH, SWA_DH)] for h in heads]
            probs, p_sinks = _swa_probs(q_hs, k_h, [sink_ref[h][:, :1] for h in heads], valid)
            dps = [_dot(do_h, v_h, 1, 1) for do_h in dos]
            rss = [jnp.sum(p * dp, axis=-1, keepdims=True) for p, dp in zip(probs, dps)]
            d_ss = [p * (dp - rs) for p, dp, rs in zip(probs, dps, rss)]
            dq_heads += [_dot(d_s, k_h) * (SWA_DH ** -0.5) for d_s in d_ss]
            dks = [_dot(d_s, q_h, 0, 0) for d_s, q_h in zip(d_ss, q_hs)]
            dvs = [_dot(p, do_h, 0, 0) for p, do_h in zip(probs, dos)]
            for h, p_sink, rs in zip(heads, p_sinks, rss):
                d_sink = -jnp.sum(p_sink * rs, axis=0, keepdims=True)
                ds_ref[h] += jnp.where(lane0, d_sink, 0.0)
            dk_heads.append(sum(dks[1:], dks[0]) * (SWA_DH ** -0.5))
            dv_heads.append(sum(dvs[1:], dvs[0]))
        for b in range(SWA_Q_W // 128):
            pair = jnp.concatenate([dq_heads[2 * b], dq_heads[2 * b + 1]], axis=1)
            dq_ref[:, pl.ds(b * 128, 128)] = _unrope(pair, tc_ref[...])
        dk = _unrope(jnp.concatenate(dk_heads, axis=1), table_kk)
        dv = jnp.concatenate(dv_heads, axis=1)
        dkp_ref[...] = dk[:WINDOW]
        dkc_ref[...] = dk[WINDOW:]
        dvp_ref[...] = dv[:WINDOW]
        dvc_ref[...] = dv[WINDOW:]

    sd = jax.ShapeDtypeStruct
    f32 = jnp.float32
    half = (q.shape[0], SWA_KV_W)
    return pl.pallas_call(
        body, name="swa_bwd", grid=(q.shape[0] // WINDOW,),
        out_shape=(sd(q.shape, f32), sd(half, f32), sd(half, f32), sd(half, f32), sd(half, f32), sd(sinks.shape, f32)),
        in_specs=[qs] + kvs + tables + [sk, qs], out_specs=(qs, cur, cur, cur, cur, sk),
        compiler_params=pltpu.CompilerParams(dimension_semantics=("arbitrary",)),
    )(q, kv, kv, kv, kv, table, table, sinks, do)


@jax.custom_vjp
def _swa_op(q, kv, table, sinks):
    return _swa_fwd_call(q, kv, table, sinks)


def _swa_op_fwd(q, kv, table, sinks):
    return _swa_fwd_call(q, kv, table, sinks), (q, kv, table, sinks)


def _swa_op_bwd(res, do):
    q, kv, table, sinks = res
    dq, dkc, dkp, dvc, dvp, dsinks = _swa_bwd_call(q, kv, table, sinks, do)

    def fold(cur, prev):
        return cur + jnp.concatenate([prev[WINDOW:], jnp.zeros_like(prev[:WINDOW])], axis=0)

    return dq, jnp.concatenate([fold(dkc, dkp), fold(dvc, dvp)], axis=1), jnp.zeros_like(table), dsinks


_swa_op.defvjp(_swa_op_fwd, _swa_op_bwd)


def _swa_sink_attention(q, kv, table, sinks):
    return _swa_op(q, kv, table, jnp.broadcast_to(sinks[:, None, None], (SWA_HEADS, 1, 128)))


MEM_ROWS = 512


def _mem_probs(q_h, k_h):
    s = _dot(q_h, k_h, 1, 1) * (MEM_DH ** -0.5)
    p = jnp.exp(s - jnp.max(s, axis=-1, keepdims=True))
    return p / jnp.sum(p, axis=-1, keepdims=True)


def _mem_fwd_call(qm, kv):
    qs = pl.BlockSpec((MEM_ROWS, MEM_W), lambda i: (i, 0))
    kvs = pl.BlockSpec(kv.shape, lambda i: (0, 0))

    def body(q_ref, kv_ref, o_ref):
        for h in range(MEM_HEADS):
            cols = pl.ds(h * MEM_DH, MEM_DH)
            probs = _mem_probs(q_ref[:, cols], kv_ref[:, cols])
            o_ref[:, cols] = _dot(probs, kv_ref[:, pl.ds(MEM_W + h * MEM_DH, MEM_DH)])

    return pl.pallas_call(
        body, name="mem_fwd", grid=(qm.shape[0] // MEM_ROWS,),
        out_shape=jax.ShapeDtypeStruct(qm.shape, jnp.float32), in_specs=[qs, kvs], out_specs=qs,
        compiler_params=pltpu.CompilerParams(dimension_semantics=("parallel",)),
    )(qm, kv)


def _mem_bwd_call(qm, kv, do):
    qs = pl.BlockSpec((MEM_ROWS, MEM_W), lambda i: (i, 0))
    kvs = pl.BlockSpec(kv.shape, lambda i: (0, 0))

    def body(q_ref, kv_ref, do_ref, dq_ref, dkv_ref):
        @pl.when(pl.program_id(0) == 0)
        def _():
            dkv_ref[...] = jnp.zeros_like(dkv_ref)

        for h in range(MEM_HEADS):
            cols = pl.ds(h * MEM_DH, MEM_DH)
            v_cols = pl.ds(MEM_W + h * MEM_DH, MEM_DH)
            q_h, k_h, do_h = q_ref[:, cols], kv_ref[:, cols], do_ref[:, cols]
            probs = _mem_probs(q_h, k_h)
            dp = _dot(do_h, kv_ref[:, v_cols], 1, 1)
            d_s = probs * (dp - jnp.sum(probs * dp, axis=-1, keepdims=True))
            dq_ref[:, cols] = _dot(d_s, k_h) * (MEM_DH ** -0.5)
            dkv_ref[:, cols] += _dot(d_s, q_h, 0, 0) * (MEM_DH ** -0.5)
            dkv_ref[:, v_cols] += _dot(probs, do_h, 0, 0)

    sd = jax.ShapeDtypeStruct
    return pl.pallas_call(
        body, name="mem_bwd", grid=(qm.shape[0] // MEM_ROWS,),
        out_shape=(sd(qm.shape, jnp.float32), sd(kv.shape, jnp.float32)),
        in_specs=[qs, kvs, qs], out_specs=(qs, kvs),
        compiler_params=pltpu.CompilerParams(dimension_semantics=("arbitrary",)),
    )(qm, kv, do)


@jax.custom_vjp
def _mem_op(qm, kv):
    return _mem_fwd_call(qm, kv)


def _mem_op_fwd(qm, kv):
    return _mem_fwd_call(qm, kv), (qm, kv)


def _mem_op_bwd(res, do):
    return _mem_bwd_call(*res, do)


_mem_op.defvjp(_mem_op_fwd, _mem_op_bwd)


def _memory_attention(qm, kv):
    return _mem_op(qm[0], kv[0])[None]


def _mixer_a(h, h_lo, mem, mem_lo, p, s, layer):
    B, S, _ = h.shape
    proj = _project(h, h_lo, p["a_w_in"][layer], s["a_w_in"][layer])
    c1 = 2 * DN_QK_W + DN_V_W
    qkv = proj[..., :c1]
    z = proj[..., c1:QKVZ_W]
    qm = proj[..., QKVZ_W:QKVZ_W + MEM_W]
    a = proj[..., QKVZ_W + MEM_W:QKVZ_W + MEM_W + DN_HEADS]
    b = proj[..., QKVZ_W + MEM_W + DN_HEADS:QKVZ_W + MEM_W + 2 * DN_HEADS]
    planes = _pre_op(qkv[0], p["a_conv_w"][layer])
    beta = jax.nn.sigmoid(b[0])
    g = -jnp.exp(p["a_A_log"][layer]) * jax.nn.softplus(a[0] + p["a_dt_bias"][layer])
    o = _gate_op(_gated_delta_rule(planes, g, beta), z[0], p["a_norm_w"][layer][None])[None]
    kv = _project(mem, mem_lo, p["mem_w_kv"][layer], s["mem_w_kv"][layer])
    mo = _memory_attention(qm, kv)
    cat = jnp.concatenate([o, mo], axis=-1)
    return _project(cat, _lo(cat), p["w_o"][layer], s["w_o"][layer])


def _mixer_b(h, h_lo, mem, mem_lo, kv_shared, table, p, s, layer):
    j = layer - N_A
    proj = _project(h, h_lo, p["b_w_in"][j], s["b_w_in"][j])
    o = _swa_sink_attention(proj[0, :, :SWA_Q_W], kv_shared, table, p["b_sinks"][j])[None]
    kv = _project(mem, mem_lo, p["mem_w_kv"][layer], s["mem_w_kv"][layer])
    mo = _memory_attention(proj[..., SWA_Q_W:], kv)
    cat = jnp.concatenate([o, mo], axis=-1)
    return _project(cat, _lo(cat), p["w_o"][layer], s["w_o"][layer])


def _forward(p, s, x, mem, positions):
    table = _rope_table(positions[0], SWA_DH)
    h, h_lo, mem_lo = x, _lo(x), _lo(mem)
    kv_shared = None
    for layer in range(DEPTH):
        if layer < N_A:
            mix = _mixer_a(h, h_lo, mem, mem_lo, p, s, layer)
        else:
            mix = _mixer_b(h, h_lo, mem, mem_lo, kv_shared, table, p, s, layer)
        seq = h.shape[1]
        h2, h2_lo = _ln_res(h[0], mix[0], p["ln_g"][layer, 0][None], p["ln_b"][layer, 0][None])
        down = _mlp(h2, h2_lo, p["mlp_w_up"][layer], p["mlp_w_down"][layer], s["mlp_w_up"][layer],
                    s["mlp_w_down"][layer])
        h, h_lo = _ln_res(h2, down, p["ln_g"][layer, 1][None], p["ln_b"][layer, 1][None])
        h, h_lo = h.reshape(1, seq, D_MODEL), h_lo.reshape(1, seq, D_MODEL)
        if layer == N_A - 1:
            kv_shared = _project(h, h_lo, p["w_kv_shared"], s["w_kv_shared"])[0]
    return h


def _loss(diff, s, p, mem, positions, target):
    y = _forward({**p, **diff["small"]}, s, diff["x"], mem, positions)
    return 0.5 * jnp.sum(jnp.mean(jnp.square(y - target), axis=-1))


def _reorder_a_w_in(w):
    pad = jnp.zeros(w.shape[:-1] + (A_IN_PAD - A_IN,), w.dtype)
    return jnp.concatenate([w[..., :QKVZ_W], w[..., QKVZ_W + 2 * DN_HEADS:], w[..., QKVZ_W:QKVZ_W + 2 * DN_HEADS], pad],
                           axis=-1)


def _restore_a_w_in(w):
    return jnp.concatenate([w[..., :QKVZ_W], w[..., QKVZ_W + MEM_W:QKVZ_W + MEM_W + 2 * DN_HEADS],
                            w[..., QKVZ_W:QKVZ_W + MEM_W]], axis=-1)


def kernel(x, mem, positions, a_w_in, a_conv_w, a_A_log, a_dt_bias, a_norm_w, b_w_in, b_sinks, w_kv_shared, mem_w_kv, w_o, mlp_w_up, mlp_w_down, ln_g, ln_b, loss_target, m_a_w_in, m_a_conv_w, m_a_A_log, m_a_dt_bias, m_a_norm_w, m_b_w_in, m_b_sinks, m_w_kv_shared, m_mem_w_kv, m_w_o, m_mlp_w_up, m_mlp_w_down, m_ln_g, m_ln_b, v_a_w_in, v_a_conv_w, v_a_A_log, v_a_dt_bias, v_a_norm_w, v_b_w_in, v_b_sinks, v_w_kv_shared, v_mem_w_kv, v_w_o, v_mlp_w_up, v_mlp_w_down, v_ln_g, v_ln_b):
    w_sh = dict(a_w_in=a_w_in, a_conv_w=a_conv_w, a_A_log=a_A_log, a_dt_bias=a_dt_bias, a_norm_w=a_norm_w,
                b_w_in=b_w_in, b_sinks=b_sinks, w_kv_shared=w_kv_shared, mem_w_kv=mem_w_kv, w_o=w_o,
                mlp_w_up=mlp_w_up, mlp_w_down=mlp_w_down, ln_g=ln_g, ln_b=ln_b)
    m_sh = dict(a_w_in=m_a_w_in, a_conv_w=m_a_conv_w, a_A_log=m_a_A_log, a_dt_bias=m_a_dt_bias, a_norm_w=m_a_norm_w,
                b_w_in=m_b_w_in, b_sinks=m_b_sinks, w_kv_shared=m_w_kv_shared, mem_w_kv=m_mem_w_kv, w_o=m_w_o,
                mlp_w_up=m_mlp_w_up, mlp_w_down=m_mlp_w_down, ln_g=m_ln_g, ln_b=m_ln_b)
    v_sh = dict(a_w_in=v_a_w_in, a_conv_w=v_a_conv_w, a_A_log=v_a_A_log, a_dt_bias=v_a_dt_bias, a_norm_w=v_a_norm_w,
                b_w_in=v_b_w_in, b_sinks=v_b_sinks, w_kv_shared=v_w_kv_shared, mem_w_kv=v_mem_w_kv, w_o=v_w_o,
                mlp_w_up=v_mlp_w_up, mlp_w_down=v_mlp_w_down, ln_g=v_ln_g, ln_b=v_ln_b)
    shard_shapes = {n: w_sh[n].shape for n in WEIGHTS}
    rb, rows = _rows_for(w_sh)

    big, small = _pack(w_sh, rb, jnp.bfloat16)
    gbig, gsmall = _gather_weights(big.reshape(2, rb // 2, FLAT_W), small.reshape(2, SMALL_ROWS // 2, FLAT_W))
    gbig, gsmall = gbig.reshape(N_CHIPS, rb, FLAT_W), gsmall.reshape(N_CHIPS, SMALL_ROWS, FLAT_W)
    pieces = [_unpack(gbig[q], gsmall[q], shard_shapes) for q in range(N_CHIPS)]
    full = {n: jnp.concatenate([pieces[q][n] for q in range(N_CHIPS)], axis=SHARD_AXIS[n]) for n in SHARD_AXIS}
    for n in REPLICATED:
        full[n] = w_sh[n]
    big_w = {n: full[n] for n in BIG}
    big_w["a_w_in"] = _reorder_a_w_in(big_w["a_w_in"])
    small_w = {n: full[n] for n in SMALL}
    slots = {n: jnp.zeros(big_w[n].shape, jnp.float32) for n in BIG}

    loss, (grads, g_slots) = jax.value_and_grad(_loss, argnums=(0, 1))(
        {"x": x, "small": small_w}, slots, big_w, mem, positions, loss_target)
    loss = lax.psum(loss, ("x", "y", "c"))
    g_full = {**g_slots, **grads["small"]}
    g_full["a_w_in"] = _restore_a_w_in(g_full["a_w_in"])

    def shard_of(n, q):
        if n in REPLICATED:
            return g_full[n]
        size = shard_shapes[n][SHARD_AXIS[n]]
        return lax.slice_in_dim(g_full[n], q * size, (q + 1) * size, axis=SHARD_AXIS[n])

    parts = []
    for q in range(N_CHIPS):
        pb, ps = _pack({n: shard_of(n, q) for n in WEIGHTS}, rb, jnp.bfloat16)
        parts.append(jnp.concatenate([pb, ps.astype(jnp.bfloat16)], axis=0).reshape(2, rows // 2, FLAT_W))
    partials = jnp.stack(parts, axis=1)
    half = lax.axis_index("c").astype(jnp.int32).reshape(1)
    chip_partials = _add_pairs(partials, _swap_halves(partials), half)
    g_flat = _join_halves(_sum_chips(_scatter_grads(chip_partials))).reshape(rows, FLAT_W)

    flat = [jnp.concatenate(_pack(d, rb), axis=0) for d in (w_sh, m_sh, v_sh)]
    outs = (g_flat,) + tuple(_adamw(g_flat, *flat))
    g_o, d_o, m_o, v_o = [_unpack(o[:rb], o[rb:], shard_shapes) for o in outs]
    return (loss, grads["x"], *[g_o[n] for n in WEIGHTS], *[d_o[n] for n in WEIGHTS],
            *[m_o[n] for n in WEIGHTS], *[v_o[n] for n in WEIGHTS])
```

```python
import functools
import math

import jax
import jax.numpy as jnp
from jax import lax
from jax.experimental import pallas as pl
from jax.experimental.pallas import tpu as pltpu

D_MODEL = 1024
DEPTH = 4
N_A = DEPTH // 2
N_B = DEPTH - N_A
MEM_HEADS = 4
MEM_DH = D_MODEL // 16
MEM_W = MEM_HEADS * MEM_DH
DN_DK = 128
DN_DV = 128
DN_HEADS = (3 * D_MODEL) // (4 * DN_DV)
DN_QK_W = DN_HEADS * DN_DK
DN_V_W = DN_HEADS * DN_DV
CONV_WIDTH = 4
CHUNK = 64
SWA_DH = 64
SWA_HEADS = (3 * D_MODEL) // (4 * SWA_DH)
SWA_KV_HEADS = 2
SWA_GROUP = SWA_HEADS // SWA_KV_HEADS
SWA_Q_W = SWA_HEADS * SWA_DH
SWA_KV_W = SWA_KV_HEADS * SWA_DH
WINDOW = 128
ROPE_THETA = 10000.0
MLP_HIDDEN = 4 * D_MODEL
LN_EPS = 1e-5
NORM_EPS = 1e-6
DN_ALPHA = (2.0 * DEPTH) ** 0.25
A_IN = 2 * DN_QK_W + 2 * DN_V_W + 2 * DN_HEADS + MEM_W
A_IN_PAD = 3456
QKVZ_W = 2 * DN_QK_W + 2 * DN_V_W

ADAM_LR = 0.001
ADAM_B1 = 0.9
ADAM_B2 = 0.999
ADAM_EPS = 1e-08
ADAM_WD = 0.01
ADAM_STEP = 10

N_CHIPS = 4
FLAT_W = 1024
BIG = ("a_w_in", "b_w_in", "w_kv_shared", "mem_w_kv", "w_o", "mlp_w_up", "mlp_w_down")
SMALL = ("a_conv_w", "ln_g", "ln_b", "a_A_log", "a_dt_bias", "a_norm_w", "b_sinks")
REPLICATED = ("a_A_log", "a_dt_bias", "a_norm_w", "b_sinks")
WEIGHTS = ("a_w_in", "a_conv_w", "a_A_log", "a_dt_bias", "a_norm_w", "b_w_in", "b_sinks", "w_kv_shared",
           "mem_w_kv", "w_o", "mlp_w_up", "mlp_w_down", "ln_g", "ln_b")
SHARD_AXIS = {"a_w_in": 2, "a_conv_w": 2, "b_w_in": 1, "w_kv_shared": 0, "mem_w_kv": 1, "w_o": 1,
              "mlp_w_up": 2, "mlp_w_down": 1, "ln_g": 2, "ln_b": 2}
SMALL_ROWS = 32
ROW_ALIGN = 256

MESH = pl.DeviceIdType.MESH
HBM_SPEC = pl.BlockSpec(memory_space=pltpu.HBM)
VMEM_LIMIT = 48 * 1024 * 1024


def _rows_for(shards):
    n_big = sum(math.prod(shards[n].shape) for n in BIG)
    n_small = sum(math.prod(shards[n].shape) for n in SMALL)
    assert n_small <= SMALL_ROWS * FLAT_W
    total = -(-n_big // FLAT_W) + SMALL_ROWS
    total = -(-total // (2 * ROW_ALIGN)) * (2 * ROW_ALIGN)
    return total - SMALL_ROWS, total


def _pack(shards, rb, dtype_big=jnp.float32):
    big = jnp.concatenate([shards[n].reshape(-1).astype(dtype_big) for n in BIG])
    big = jnp.pad(big, (0, rb * FLAT_W - big.shape[0])).reshape(rb, FLAT_W)
    small = jnp.concatenate([shards[n].reshape(-1).astype(jnp.float32) for n in SMALL])
    small = jnp.pad(small, (0, SMALL_ROWS * FLAT_W - small.shape[0])).reshape(SMALL_ROWS, FLAT_W)
    return big, small


def _unpack(big, small, shapes):
    out = {}
    for flat, names in ((big.reshape(-1), BIG), (small.reshape(-1), SMALL)):
        off = 0
        for n in names:
            size = math.prod(shapes[n])
            out[n] = flat[off:off + size].reshape(shapes[n])
            off += size
    return out


def _other_chips(x, y):
    return [(1 - x, y), (x, 1 - y), (1 - x, 1 - y)]


def _gather_weights(big, small):
    def body(big_ref, small_ref, obig_ref, osmall_ref, send_sems, recv_sems, pass_send_sems, pass_recv_sems, local_sems):
        x, y, c = lax.axis_index("x"), lax.axis_index("y"), lax.axis_index("c")
        me = 2 * x + y
        sibling = (x, y, 1 - c)
        pairs = ((big_ref, obig_ref), (small_ref, osmall_ref))
        local = [pltpu.make_async_copy(src, dst.at[me], local_sems.at[i]) for i, (src, dst) in enumerate(pairs)]
        for cp in local:
            cp.start()
        sends = []
        for j, (px, py) in enumerate(_other_chips(x, y)):
            for i, (src, dst) in enumerate(pairs):
                sends.append(pltpu.make_async_remote_copy(
                    src_ref=src.at[c], dst_ref=dst.at[me, c], send_sem=send_sems.at[2 * j + i],
                    recv_sem=recv_sems.at[2 * j + i], device_id=(px, py, c), device_id_type=MESH))
        for cp in sends:
            cp.start()
        passed = []
        for j, (px, py) in enumerate(_other_chips(x, y)):
            for i, (src, dst) in enumerate(pairs):
                landed = dst.at[2 * px + py, c]
                pltpu.make_async_remote_copy(
                    src_ref=src.at[c], dst_ref=landed, send_sem=send_sems.at[2 * j + i],
                    recv_sem=recv_sems.at[2 * j + i], device_id=(px, py, c), device_id_type=MESH).wait_recv()
                passed.append(pltpu.make_async_remote_copy(
                    src_ref=landed, dst_ref=landed, send_sem=pass_send_sems.at[2 * j + i],
                    recv_sem=pass_recv_sems.at[2 * j + i], device_id=sibling, device_id_type=MESH))
                passed[-1].start()
        for j, (px, py) in enumerate(_other_chips(x, y)):
            for i, (src, dst) in enumerate(pairs):
                other_half = dst.at[2 * px + py, 1 - c]
                pltpu.make_async_remote_copy(
                    src_ref=other_half, dst_ref=other_half, send_sem=pass_send_sems.at[2 * j + i],
                    recv_sem=pass_recv_sems.at[2 * j + i], device_id=sibling, device_id_type=MESH).wait_recv()
        for cp in sends + passed:
            cp.wait_send()
        for cp in local:
            cp.wait()

    dma6 = pltpu.SemaphoreType.DMA((6,))
    return pl.pallas_call(
        body, name="gather_weights",
        out_shape=(jax.ShapeDtypeStruct((N_CHIPS,) + big.shape, big.dtype),
                   jax.ShapeDtypeStruct((N_CHIPS,) + small.shape, small.dtype)),
        in_specs=[HBM_SPEC, HBM_SPEC], out_specs=(HBM_SPEC, HBM_SPEC),
        scratch_shapes=[dma6, dma6, dma6, dma6, pltpu.SemaphoreType.DMA((2,))],
    )(big, small)


def _scatter_grads(g):
    def body(g_ref, o_ref, send_sems, recv_sems, local_sem):
        x, y, c = lax.axis_index("x"), lax.axis_index("y"), lax.axis_index("c")
        me = 2 * x + y
        local = pltpu.make_async_copy(g_ref.at[me], o_ref.at[me], local_sem)
        local.start()
        sends = []
        for j, (px, py) in enumerate(_other_chips(x, y)):
            sends.append(pltpu.make_async_remote_copy(
                src_ref=g_ref.at[2 * px + py], dst_ref=o_ref.at[me], send_sem=send_sems.at[j], recv_sem=recv_sems.at[j],
                device_id=(px, py, c), device_id_type=MESH))
        for cp in sends:
            cp.start()
        for j, (px, py) in enumerate(_other_chips(x, y)):
            pltpu.make_async_remote_copy(
                src_ref=g_ref.at[me], dst_ref=o_ref.at[2 * px + py], send_sem=send_sems.at[j], recv_sem=recv_sems.at[j],
                device_id=(px, py, c), device_id_type=MESH).wait_recv()
        for cp in sends:
            cp.wait_send()
        local.wait()

    return pl.pallas_call(
        body, name="scatter_grads",
        out_shape=jax.ShapeDtypeStruct(g.shape, g.dtype),
        in_specs=[HBM_SPEC], out_specs=HBM_SPEC,
        scratch_shapes=[pltpu.SemaphoreType.DMA((3,)), pltpu.SemaphoreType.DMA((3,)), pltpu.SemaphoreType.DMA],
    )(g)


def _swap_halves(g):
    def body(g_ref, o_ref, send_sem, recv_sem):
        x, y, c = lax.axis_index("x"), lax.axis_index("y"), lax.axis_index("c")
        cp = pltpu.make_async_remote_copy(src_ref=g_ref.at[1 - c], dst_ref=o_ref, send_sem=send_sem, recv_sem=recv_sem,
                                          device_id=(x, y, 1 - c), device_id_type=MESH)
        cp.start()
        cp.wait()

    return pl.pallas_call(
        body, name="swap_halves",
        out_shape=jax.ShapeDtypeStruct(g.shape[1:], g.dtype),
        in_specs=[HBM_SPEC], out_specs=HBM_SPEC,
        scratch_shapes=[pltpu.SemaphoreType.DMA, pltpu.SemaphoreType.DMA],
    )(g)


def _join_halves(v):
    def body(v_ref, o_ref, send_sem, recv_sem, local_sem):
        x, y, c = lax.axis_index("x"), lax.axis_index("y"), lax.axis_index("c")
        local = pltpu.make_async_copy(v_ref, o_ref.at[c], local_sem)
        local.start()
        cp = pltpu.make_async_remote_copy(src_ref=v_ref, dst_ref=o_ref.at[c], send_sem=send_sem, recv_sem=recv_sem,
                                          device_id=(x, y, 1 - c), device_id_type=MESH)
        cp.start()
        cp.wait_send()
        pltpu.make_async_remote_copy(src_ref=v_ref, dst_ref=o_ref.at[1 - c], send_sem=send_sem, recv_sem=recv_sem,
                                     device_id=(x, y, 1 - c), device_id_type=MESH).wait_recv()
        local.wait()

    return pl.pallas_call(
        body, name="join_halves",
        out_shape=jax.ShapeDtypeStruct((2,) + v.shape, v.dtype),
        in_specs=[HBM_SPEC], out_specs=HBM_SPEC,
        scratch_shapes=[pltpu.SemaphoreType.DMA, pltpu.SemaphoreType.DMA, pltpu.SemaphoreType.DMA],
    )(v)


def _add_pairs(g, theirs, half):
    _, n, rows, width = g.shape
    assert rows % ROW_ALIGN == 0, rows

    def body(half_ref, g_ref, t_ref, o_ref):
        o_ref[...] = (g_ref[...].astype(jnp.float32) + t_ref[...].astype(jnp.float32)).astype(o_ref.dtype)

    blk = pl.BlockSpec((None, ROW_ALIGN, width), lambda p, i, h: (p, i, 0))
    grid_spec = pltpu.PrefetchScalarGridSpec(
        num_scalar_prefetch=1, grid=(n, rows // ROW_ALIGN),
        in_specs=[pl.BlockSpec((None, None, ROW_ALIGN, width), lambda p, i, h: (h[0], p, i, 0)), blk], out_specs=blk)
    return pl.pallas_call(
        body, name="add_pairs", grid_spec=grid_spec, out_shape=jax.ShapeDtypeStruct(theirs.shape, g.dtype),
        compiler_params=pltpu.CompilerParams(dimension_semantics=("parallel", "parallel")),
    )(half, g, theirs)


def _sum_chips(parts):
    n, rows, width = parts.shape
    assert rows % ROW_ALIGN == 0, rows

    def body(p_ref, o_ref):
        p = [p_ref[q].astype(jnp.float32) for q in range(n)]
        o_ref[...] = (p[0] + p[1]) + (p[2] + p[3])

    return pl.pallas_call(
        body, name="sum_chips", grid=(rows // ROW_ALIGN,),
        out_shape=jax.ShapeDtypeStruct((rows, width), jnp.float32),
        in_specs=[pl.BlockSpec((n, ROW_ALIGN, width), lambda i: (0, i, 0))],
        out_specs=pl.BlockSpec((ROW_ALIGN, width), lambda i: (i, 0)),
        compiler_params=pltpu.CompilerParams(dimension_semantics=("parallel",), vmem_limit_bytes=VMEM_LIMIT),
    )(parts)


def _adamw(g, w, m, v):
    rows, width = w.shape
    blk = ROW_ALIGN // 2

    def body(g_ref, w_ref, m_ref, v_ref, d_out, m_out, v_out):
        g = g_ref[...]
        m_new = ADAM_B1 * m_ref[...] + (1.0 - ADAM_B1) * g
        v_new = ADAM_B2 * v_ref[...] + (1.0 - ADAM_B2) * jnp.square(g)
        m_hat = m_new / (1.0 - ADAM_B1 ** ADAM_STEP)
        v_hat = v_new / (1.0 - ADAM_B2 ** ADAM_STEP)
        d_out[...] = -ADAM_LR * (m_hat / (jnp.sqrt(v_hat) + ADAM_EPS) + ADAM_WD * w_ref[...])
        m_out[...] = m_new
        v_out[...] = v_new

    spec = pl.BlockSpec((blk, width), lambda i: (i, 0))
    shape = jax.ShapeDtypeStruct((rows, width), jnp.float32)
    return pl.pallas_call(
        body, name="adamw", grid=(rows // blk,),
        out_shape=(shape,) * 3, in_specs=[spec] * 4, out_specs=(spec,) * 3,
        compiler_params=pltpu.CompilerParams(dimension_semantics=("parallel",), vmem_limit_bytes=VMEM_LIMIT),
    )(g, w, m, v)


def _tile(dim, pref):
    if dim <= pref:
        return dim
    for t in range(pref - pref % 128, 0, -128):
        if dim % t == 0:
            return t
    raise ValueError(f"no 128-aligned tile for {dim}")


def _matmul(a, b, *, ta=False, tb=False, name, epilogue=None, extra=None, out_dtype=jnp.float32):
    (k_a, m) = a.shape if ta else a.shape[::-1]
    (k_b, n) = b.shape[::-1] if tb else b.shape
    assert k_a == k_b, (a.shape, b.shape, ta, tb)
    k = k_a
    tk = _tile(k, 1152)
    nk = k // tk
    if ta:
        tm, tn = _tile(m, 1024), _tile(n, 2048 if m <= 1024 else 1024)
    else:
        tm, tn = _tile(m, 2048), _tile(n, 512 if nk == 1 else 1024)
    a_spec = pl.BlockSpec((tk, tm), lambda i, j, l: (l, i)) if ta else pl.BlockSpec((tm, tk), lambda i, j, l: (i, l))
    b_spec = pl.BlockSpec((tn, tk), lambda i, j, l: (j, l)) if tb else pl.BlockSpec((tk, tn), lambda i, j, l: (l, j))
    o_spec = pl.BlockSpec((tm, tn), lambda i, j, l: (i, j))
    dims = (((0 if ta else 1,), (1 if tb else 0,)), ((), ()))
    has_extra = epilogue == "relu2_grad"
    assert has_extra == (extra is not None)

    def body(*refs):
        a_ref, b_ref = refs[:2]
        outs = refs[2 + has_extra:2 + has_extra + (2 if epilogue == "relu2" else 1)]
        l = pl.program_id(2)
        part = lax.dot_general(a_ref[...].astype(jnp.bfloat16), b_ref[...].astype(jnp.bfloat16), dims,
                               preferred_element_type=jnp.float32)

        def finish(acc):
            if epilogue is None:
                outs[0][...] = acc.astype(out_dtype)
            elif epilogue == "relu2":
                outs[0][...] = acc.astype(jnp.bfloat16)
                outs[1][...] = jnp.square(jnp.maximum(acc, 0.0)).astype(jnp.bfloat16)
            else:
                outs[0][...] = (acc * (2.0 * jnp.maximum(refs[2][...].astype(jnp.float32), 0.0))).astype(out_dtype)

        if nk == 1:
            finish(part)
            return
        acc_ref = refs[-1]

        @pl.when(l == 0)
        def _():
            acc_ref[...] = part

        @pl.when((l > 0) & (l < nk - 1))
        def _():
            acc_ref[...] += part

        @pl.when(l == nk - 1)
        def _():
            finish(acc_ref[...] + part)

    if epilogue == "relu2":
        out_shape = (jax.ShapeDtypeStruct((m, n), jnp.bfloat16),) * 2
        out_specs = (o_spec, o_spec)
    else:
        out_shape = jax.ShapeDtypeStruct((m, n), out_dtype)
        out_specs = o_spec
    return pl.pallas_call(
        body, name=name, grid=(m // tm, n // tn, nk), out_shape=out_shape,
        in_specs=[a_spec, b_spec] + ([o_spec] if has_extra else []), out_specs=out_specs,
        scratch_shapes=[pltpu.VMEM((tm, tn), jnp.float32)] if nk > 1 else [],
        compiler_params=pltpu.CompilerParams(dimension_semantics=("parallel", "parallel", "arbitrary"),
                                             vmem_limit_bytes=VMEM_LIMIT),
    )(*((a, b) + ((extra,) if has_extra else ())))


def _lo(x):
    return lax.stop_gradient(x.astype(jnp.bfloat16))


@jax.custom_vjp
def _linear(x, x_lo, w, slot):
    del x, slot
    return _matmul(x_lo, w, name="linear_fwd")


def _linear_fwd(x, x_lo, w, slot):
    del x, slot
    return _matmul(x_lo, w, name="linear_fwd"), (x_lo, w)


def _linear_bwd(res, dy):
    x_lo, w = res
    dy = dy.astype(jnp.bfloat16)
    dx = _matmul(dy, w, tb=True, name="linear_dx")
    dw = _matmul(x_lo, dy, ta=True, name="linear_dw")
    return dx, jnp.zeros_like(x_lo), jnp.zeros_like(w), dw


_linear.defvjp(_linear_fwd, _linear_bwd)


@jax.custom_vjp
def _mlp(h, h_lo, w_up, w_down, slot_up, slot_down):
    return _mlp_fwd(h, h_lo, w_up, w_down, slot_up, slot_down)[0]


def _mlp_fwd(h, h_lo, w_up, w_down, slot_up, slot_down):
    del h, slot_up, slot_down
    up, act = _matmul(h_lo, w_up, name="mlp_up", epilogue="relu2")
    return _matmul(act, w_down, name="mlp_down"), (h_lo, up, act, w_up, w_down)


def _mlp_bwd(res, dy):
    h_lo, up, act, w_up, w_down = res
    dy = dy.astype(jnp.bfloat16)
    d_up = _matmul(dy, w_down, tb=True, name="mlp_d_up", epilogue="relu2_grad", extra=up, out_dtype=jnp.bfloat16)
    dw_down = _matmul(act, dy, ta=True, name="mlp_dw_down")
    dw_up = _matmul(h_lo, d_up, ta=True, name="mlp_dw_up")
    dh = _matmul(d_up, w_up, tb=True, name="mlp_dh")
    return dh, jnp.zeros_like(h_lo), jnp.zeros_like(w_up), jnp.zeros_like(w_down), dw_up, dw_down


_mlp.defvjp(_mlp_fwd, _mlp_bwd)


LN_ROWS = 256


def _ln_call(h, mix, g, b):
    s, d = h.shape
    tok = pl.BlockSpec((LN_ROWS, d), lambda i: (i, 0))
    vec = pl.BlockSpec((1, d), lambda i: (0, 0))
    stat = pl.BlockSpec((LN_ROWS, 1), lambda i: (i, 0))

    def body(h_ref, mix_ref, g_ref, b_ref, y_ref, ylo_ref, xhat_ref, rstd_ref):
        z = DN_ALPHA * h_ref[...] + mix_ref[...]
        mu = jnp.mean(z, axis=-1, keepdims=True)
        zc = z - mu
        rstd = lax.rsqrt(jnp.mean(jnp.square(zc), axis=-1, keepdims=True) + LN_EPS)
        xhat = zc * rstd
        y = xhat * g_ref[...] + b_ref[...]
        y_ref[...] = y
        ylo_ref[...] = y.astype(ylo_ref.dtype)
        xhat_ref[...] = xhat
        rstd_ref[...] = rstd

    sd = jax.ShapeDtypeStruct
    return pl.pallas_call(
        body, name="ln_fwd", grid=(s // LN_ROWS,),
        out_shape=(sd((s, d), jnp.float32), sd((s, d), jnp.bfloat16), sd((s, d), jnp.float32), sd((s, 1), jnp.float32)),
        in_specs=[tok, tok, vec, vec], out_specs=(tok, tok, tok, stat),
        compiler_params=pltpu.CompilerParams(dimension_semantics=("parallel",)),
    )(h, mix, g, b)


def _ln_grad_call(dy, xhat, rstd, g):
    s, d = dy.shape
    tok = pl.BlockSpec((LN_ROWS, d), lambda i: (i, 0))
    vec = pl.BlockSpec((1, d), lambda i: (0, 0))
    stat = pl.BlockSpec((LN_ROWS, 1), lambda i: (i, 0))

    def body(dy_ref, xhat_ref, rstd_ref, g_ref, dz_ref, dg_ref, db_ref):
        @pl.when(pl.program_id(0) == 0)
        def _():
            dg_ref[...] = jnp.zeros_like(dg_ref)
            db_ref[...] = jnp.zeros_like(db_ref)

        dy, xhat = dy_ref[...], xhat_ref[...]
        dyg = dy * g_ref[...]
        m1 = jnp.mean(dyg, axis=-1, keepdims=True)
        m2 = jnp.mean(dyg * xhat, axis=-1, keepdims=True)
        dz_ref[...] = rstd_ref[...] * (dyg - m1 - xhat * m2)
        dg_ref[...] += jnp.sum(dy * xhat, axis=0, keepdims=True)
        db_ref[...] += jnp.sum(dy, axis=0, keepdims=True)

    sd = jax.ShapeDtypeStruct
    return pl.pallas_call(
        body, name="ln_bwd", grid=(s // LN_ROWS,),
        out_shape=(sd((s, d), jnp.float32), sd((1, d), jnp.float32), sd((1, d), jnp.float32)),
        in_specs=[tok, tok, stat, vec], out_specs=(tok, vec, vec),
        compiler_params=pltpu.CompilerParams(dimension_semantics=("arbitrary",)),
    )(dy, xhat, rstd, g)


@jax.custom_vjp
def _ln_res(h, mix, g, b):
    return _ln_call(h, mix, g, b)[:2]


def _ln_res_fwd(h, mix, g, b):
    y, y_lo, xhat, rstd = _ln_call(h, mix, g, b)
    return (y, y_lo), (xhat, rstd, g)


def _ln_res_bwd(res, cts):
    xhat, rstd, g = res
    dz, dg, db = _ln_grad_call(cts[0], xhat, rstd, g)
    return DN_ALPHA * dz, dz, dg, db


_ln_res.defvjp(_ln_res_fwd, _ln_res_bwd)


MXU_DTYPE = jnp.bfloat16
DN_CB = 8
DN_GROUP = 8
DN_SCALE = DN_DK ** -0.5


def _dot(a, b, ca=1, cb=0):
    return lax.dot_general(a.astype(MXU_DTYPE), b.astype(MXU_DTYPE), (((ca,), (cb,)), ((), ())),
                           preferred_element_type=jnp.float32)


def _chunk_masks():
    row = lax.broadcasted_iota(jnp.int32, (CHUNK, CHUNK), 0)
    col = lax.broadcasted_iota(jnp.int32, (CHUNK, CHUNK), 1)
    return row >= col, row > col, row == col


def _to_col(row_vec):
    _, _, eye = _chunk_masks()
    return jnp.sum(jnp.where(eye, jnp.broadcast_to(row_vec, (CHUNK, CHUNK)), 0.0), axis=1, keepdims=True)


def _to_row(col_vec):
    _, _, eye = _chunk_masks()
    return jnp.sum(jnp.where(eye, jnp.broadcast_to(col_vec, (CHUNK, CHUNK)), 0.0), axis=0, keepdims=True)


def _last_row(col_vec):
    last = lax.broadcasted_iota(jnp.int32, (CHUNK, 1), 0) == CHUNK - 1
    return jnp.sum(jnp.where(last, col_vec, 0.0), axis=0, keepdims=True), last


def _chunk_terms(q, k, beta, gcc, gcr):
    incl, strict, _ = _chunk_masks()
    decay = jnp.where(incl, jnp.exp(jnp.minimum(gcc - gcr, 0.0)), 0.0)
    kb = k * beta
    lmat = jnp.where(strict, _dot(kb, k, 1, 1) * decay, 0.0)
    intra = jnp.where(incl, _dot(q, k, 1, 1) * decay, 0.0)
    return decay, kb, lmat, intra


def _dot3(a, b, ca=1, cb=0):
    if MXU_DTYPE == jnp.float32:
        return _dot(a, b, ca, cb)
    a_hi, b_hi = a.astype(MXU_DTYPE), b.astype(MXU_DTYPE)
    a_lo = (a - a_hi.astype(jnp.float32)).astype(MXU_DTYPE)
    b_lo = (b - b_hi.astype(jnp.float32)).astype(MXU_DTYPE)
    return _dot(a_hi, b_hi, ca, cb) + (_dot(a_hi, b_lo, ca, cb) + _dot(a_lo, b_hi, ca, cb))


def _unit_lower_inverse(lmats):
    _, _, eye = _chunk_masks()
    ident = jnp.where(eye, 1.0, 0.0)
    ts = [ident - m for m in lmats]
    ps = [_dot(m, m) for m in lmats]
    for _ in range(4):
        ts = [t + _dot(t, p) for t, p in zip(ts, ps)]
        ps = [_dot(p, p) for p in ps]
    ts = [t + _dot(t, p) for t, p in zip(ts, ps)]
    resids = [(t - ident) + _dot3(m, t) for m, t in zip(lmats, ts)]
    return [t - _dot(t, r) for t, r in zip(ts, resids)]


def _dn_specs(n_chunks):
    tok = pl.BlockSpec((DN_CB * CHUNK, DN_DK), lambda h, n: (n, h))
    rowv = pl.BlockSpec((None, DN_CB, CHUNK), lambda h, n: (h, n, 0))
    sq = pl.BlockSpec((None, DN_CB, CHUNK, CHUNK), lambda h, n: (h, n, 0, 0))
    lane = pl.BlockSpec((None, DN_CB, 1, DN_DV), lambda h, n: (h, n, 0, 0))
    planes = [pl.BlockSpec((None, DN_CB * CHUNK, DN_DK), functools.partial(lambda h, n, p: (p, n, h), p=p))
              for p in range(3)]
    return tok, rowv, sq, lane, planes


def _dn_prep(qkv, beta, gc):
    s = qkv.shape[1]
    n_chunks = s // CHUNK
    tok, rowv, sq, lane, planes = _dn_specs(n_chunks)
    tok_shape = qkv.shape[1:]

    def body(q_ref, k_ref, v_ref, beta_ref, gc_ref, u_ref, w_ref, qd_ref, kd_ref, intra_ref, t_ref, cd_ref):
        for c0 in range(0, DN_CB, DN_GROUP):
            chunks = range(c0, c0 + DN_GROUP)
            rhs, lmats = [], []
            for c in chunks:
                rows = pl.ds(c * CHUNK, CHUNK)
                q_c, k_c, v_c = q_ref[rows, :] * DN_SCALE, k_ref[rows, :], v_ref[rows, :]
                gcr_c = gc_ref[pl.ds(c, 1), :]
                beta_c, gcc_c = _to_col(beta_ref[pl.ds(c, 1), :]), _to_col(gcr_c)
                _, kb, lmat, intra = _chunk_terms(q_c, k_c, beta_c, gcc_c, gcr_c)
                eg = jnp.exp(gcc_c)
                g_last, _ = _last_row(gcc_c)
                qd_ref[rows, :] = (q_c * eg).astype(qd_ref.dtype)
                kd_ref[rows, :] = (k_c * jnp.exp(g_last - gcc_c)).astype(kd_ref.dtype)
                intra_ref[c] = intra.astype(intra_ref.dtype)
                cd_ref[c] = jnp.broadcast_to(jnp.exp(g_last), (1, DN_DV))
                rhs.append(jnp.concatenate([v_c * beta_c, kb * eg], axis=1))
                lmats.append(lmat)
            ts = _unit_lower_inverse(lmats)
            sols = [_dot3(t, r) for t, r in zip(ts, rhs)]
            for c, t, sol in zip(chunks, ts, sols):
                rows = pl.ds(c * CHUNK, CHUNK)
                t_ref[c] = t
                u_ref[rows, :] = sol[:, :DN_DV]
                w_ref[rows, :] = sol[:, DN_DV:].astype(w_ref.dtype)

    f32, mx = jnp.float32, MXU_DTYPE
    sd = jax.ShapeDtypeStruct
    return pl.pallas_call(
        body, name="dn_prep", grid=(DN_HEADS, n_chunks // DN_CB),
        out_shape=(sd(tok_shape, f32), sd(tok_shape, mx), sd(tok_shape, mx), sd(tok_shape, mx),
                   sd((DN_HEADS, n_chunks, CHUNK, CHUNK), mx), sd((DN_HEADS, n_chunks, CHUNK, CHUNK), f32),
                   sd((DN_HEADS, n_chunks, 1, DN_DV), f32)),
        in_specs=planes + [rowv, rowv], out_specs=(tok, tok, tok, tok, sq, sq, lane),
        compiler_params=pltpu.CompilerParams(dimension_semantics=("parallel", "parallel")),
    )(qkv, qkv, qkv, beta, gc)


def _dn_scan(u, w, qd, kd, intra, cd):
    s, width = u.shape
    n_chunks = s // CHUNK
    tok = pl.BlockSpec((CHUNK, width), lambda n: (n, 0))
    sq = pl.BlockSpec((DN_HEADS, None, CHUNK, CHUNK), lambda n: (0, n, 0, 0))
    lane = pl.BlockSpec((DN_HEADS, None, 1, DN_DV), lambda n: (0, n, 0, 0))
    st = pl.BlockSpec((DN_HEADS, None, DN_DK, DN_DV), lambda n: (0, n, 0, 0))

    def body(u_ref, w_ref, qd_ref, kd_ref, intra_ref, cd_ref, o_ref, vn_ref, st_ref, state):
        @pl.when(pl.program_id(0) == 0)
        def _():
            state[...] = jnp.zeros_like(state)

        heads = range(DN_HEADS)
        cols = [pl.ds(h * DN_DK, DN_DK) for h in heads]
        s_f = [state[h] for h in heads]
        s_mx = [s.astype(MXU_DTYPE) for s in s_f]
        for h in heads:
            st_ref[h] = s_mx[h]
        ws = [_dot(w_ref[:, cols[h]], s_mx[h]) for h in heads]
        qs = [_dot(qd_ref[:, cols[h]], s_mx[h]) for h in heads]
        v_new = [(u_ref[:, cols[h]] - ws[h]).astype(MXU_DTYPE) for h in heads]
        inner = [_dot(intra_ref[h], v_new[h]) for h in heads]
        outer = [_dot(kd_ref[:, cols[h]], v_new[h], 0, 0) for h in heads]
        for h in heads:
            vn_ref[:, cols[h]] = v_new[h]
            o_ref[:, cols[h]] = qs[h] + inner[h]
            state[h] = s_f[h] * cd_ref[h] + outer[h]

    sd = jax.ShapeDtypeStruct
    return pl.pallas_call(
        body, name="dn_scan", grid=(n_chunks,),
        out_shape=(sd(u.shape, jnp.float32), sd(u.shape, MXU_DTYPE),
                   sd((DN_HEADS, n_chunks, DN_DK, DN_DV), MXU_DTYPE)),
        in_specs=[tok, tok, tok, tok, sq, lane], out_specs=(tok, tok, st),
        scratch_shapes=[pltpu.VMEM((DN_HEADS, DN_DK, DN_DV), jnp.float32)],
        compiler_params=pltpu.CompilerParams(dimension_semantics=("arbitrary",)),
    )(u, w, qd, kd, intra, cd)


def _dn_bwd_scan(do, w, qd, kd, intra, cd, vn, st):
    s, width = do.shape
    n_chunks = s // CHUNK
    last = n_chunks - 1
    tok = pl.BlockSpec((CHUNK, width), lambda n: (last - n, 0))
    sq = pl.BlockSpec((DN_HEADS, None, CHUNK, CHUNK), lambda n: (0, last - n, 0, 0))
    lane = pl.BlockSpec((DN_HEADS, None, 1, DN_DV), lambda n: (0, last - n, 0, 0))
    stt = pl.BlockSpec((DN_HEADS, None, DN_DK, DN_DV), lambda n: (0, last - n, 0, 0))

    def body(do_ref, w_ref, qd_ref, kd_ref, intra_ref, cd_ref, vn_ref, st_ref,
             du_ref, dw_ref, dqd_ref, dkd_ref, dintra_ref, dgl_ref, dstate):
        @pl.when(pl.program_id(0) == 0)
        def _():
            dstate[...] = jnp.zeros_like(dstate)

        heads = range(DN_HEADS)
        cols = [pl.ds(h * DN_DK, DN_DK) for h in heads]
        ds_f = [dstate[h] for h in heads]
        ds_mx = [d.astype(MXU_DTYPE) for d in ds_f]
        do_h = [do_ref[:, cols[h]].astype(MXU_DTYPE) for h in heads]
        dv_a = [_dot(intra_ref[h], do_h[h], 0, 0) for h in heads]
        dv_b = [_dot(kd_ref[:, cols[h]], ds_mx[h]) for h in heads]
        d_intra = [_dot(do_h[h], vn_ref[:, cols[h]], 1, 1) for h in heads]
        d_qd = [_dot(do_h[h], st_ref[h], 1, 1) for h in heads]
        d_kd = [_dot(vn_ref[:, cols[h]], ds_mx[h], 1, 1) for h in heads]
        ds_q = [_dot(qd_ref[:, cols[h]], do_h[h], 0, 0) for h in heads]
        dv_new = [dv_a[h] + dv_b[h] for h in heads]
        dv_mx = [d.astype(MXU_DTYPE) for d in dv_new]
        d_w = [_dot(dv_mx[h], st_ref[h], 1, 1) for h in heads]
        ds_w = [_dot(w_ref[:, cols[h]], dv_mx[h], 0, 0) for h in heads]
        for h in heads:
            du_ref[:, cols[h]] = dv_new[h]
            dintra_ref[h] = d_intra[h]
            dqd_ref[:, cols[h]] = d_qd[h]
            dkd_ref[:, cols[h]] = d_kd[h]
            dw_ref[:, cols[h]] = -d_w[h]
            cd_h = cd_ref[h]
            dcd = jnp.sum(jnp.sum(st_ref[h].astype(jnp.float32) * ds_f[h], axis=1, keepdims=True), axis=0,
                          keepdims=True)
            dgl_ref[h] = dcd * cd_h
            dstate[h] = ds_q[h] + ds_f[h] * cd_h - ds_w[h]

    sd = jax.ShapeDtypeStruct
    f32 = jnp.float32
    return pl.pallas_call(
        body, name="dn_bwd_scan", grid=(n_chunks,),
        out_shape=(sd(do.shape, f32), sd(do.shape, f32), sd(do.shape, f32), sd(do.shape, f32),
                   sd((DN_HEADS, n_chunks, CHUNK, CHUNK), f32), sd((DN_HEADS, n_chunks, 1, DN_DV), f32)),
        in_specs=[tok, tok, tok, tok, sq, lane, tok, stt], out_specs=(tok, tok, tok, tok, sq, lane),
        scratch_shapes=[pltpu.VMEM((DN_HEADS, DN_DK, DN_DV), f32)],
        compiler_params=pltpu.CompilerParams(dimension_semantics=("arbitrary",)),
    )(do, w, qd, kd, intra, cd, vn, st)


def _dn_bwd_chunks(qkv, beta, gc, t, u, w, du, dw, dqd, dkd, dintra, dgl):
    s = qkv.shape[1]
    n_chunks = s // CHUNK
    tok, rowv, sq, lane, planes = _dn_specs(n_chunks)
    all_planes = pl.BlockSpec((3, DN_CB * CHUNK, DN_DK), lambda h, n: (0, n, h))

    def body(q_ref, k_ref, v_ref, beta_ref, gc_ref, t_ref, u_ref, w_ref, du_ref, dw_ref, dqd_ref, dkd_ref,
             dintra_ref, dgl_ref, dqkv_ref, dbeta_ref, dgc_ref):
        incl, strict, _ = _chunk_masks()

        def first(c):
            rows = pl.ds(c * CHUNK, CHUNK)
            q_c, k_c = q_ref[rows, :] * DN_SCALE, k_ref[rows, :]
            gcr_c = gc_ref[pl.ds(c, 1), :]
            beta_c, gcc_c = _to_col(beta_ref[pl.ds(c, 1), :]), _to_col(gcr_c)
            decay, kb, lmat, intra = _chunk_terms(q_c, k_c, beta_c, gcc_c, gcr_c)
            d_sol = jnp.concatenate([du_ref[rows, :], dw_ref[rows, :]], axis=1)
            d_rhs = _dot3(t_ref[c], d_sol, 0, 0)
            return dict(rows=rows, q=q_c, k=k_c, beta=beta_c, gcc=gcc_c, decay=decay, kb=kb, lmat=lmat, intra=intra,
                        d_rhs=d_rhs)

        def second(c, e):
            sol = jnp.concatenate([u_ref[e["rows"], :], w_ref[e["rows"], :].astype(jnp.float32)], axis=1)
            e["d_l"] = jnp.where(strict, -_dot(e["d_rhs"], sol, 1, 1), 0.0)
            e["d_intra"] = jnp.where(incl, dintra_ref[c], 0.0)
            d_qk = e["d_intra"] * e["decay"]
            e["dq"] = _dot(d_qk, e["k"])
            e["dk"] = _dot(d_qk, e["q"], 0, 0)

        def third(e):
            d_a = e["d_l"] * e["decay"]
            e["dkb"] = _dot(d_a, e["k"])
            e["dk"] = e["dk"] + _dot(d_a, e["kb"], 0, 0)

        def last(c, e):
            rows, q_c, k_c, beta_c, gcc_c = e["rows"], e["q"], e["k"], e["beta"], e["gcc"]
            v_c = v_ref[rows, :]
            eg = jnp.exp(gcc_c)
            g_last, is_last = _last_row(gcc_c)
            e_rev = jnp.exp(g_last - gcc_c)
            d_rhs_u, d_rhs_w = e["d_rhs"][:, :DN_DV], e["d_rhs"][:, DN_DV:]
            dqkv_ref[2, rows, :] = d_rhs_u * beta_c
            dbeta = jnp.sum(d_rhs_u * v_c, axis=1, keepdims=True)
            dkb = e["dkb"] + d_rhs_w * eg
            dgc = jnp.sum(d_rhs_w * e["kb"] * eg, axis=1, keepdims=True)
            m1 = e["d_l"] * e["lmat"]
            dgc = dgc + jnp.sum(m1, axis=1, keepdims=True)
            dgr = -jnp.sum(m1, axis=0, keepdims=True)
            m2 = e["d_intra"] * e["intra"]
            dgc = dgc + jnp.sum(m2, axis=1, keepdims=True)
            dgr = dgr - jnp.sum(m2, axis=0, keepdims=True)
            dqd = dqd_ref[rows, :]
            dq = e["dq"] + dqd * eg
            dgc = dgc + jnp.sum(dqd * q_c * eg, axis=1, keepdims=True)
            dkd = dkd_ref[rows, :]
            dk = e["dk"] + dkd * e_rev
            tk = jnp.sum(dkd * k_c * e_rev, axis=1, keepdims=True)
            dgc = dgc - tk
            d_last = dgl_ref[c][:, :1] + jnp.sum(tk, axis=0, keepdims=True)
            dgc = dgc + jnp.where(is_last, d_last, 0.0)
            dk = dk + dkb * beta_c
            dbeta = dbeta + jnp.sum(dkb * k_c, axis=1, keepdims=True)
            dqkv_ref[0, rows, :] = dq * DN_SCALE
            dqkv_ref[1, rows, :] = dk
            dbeta_ref[pl.ds(c, 1), :] = _to_row(dbeta)
            dgc_ref[pl.ds(c, 1), :] = _to_row(dgc) + dgr

        for c0 in range(0, DN_CB, DN_GROUP):
            chunks = range(c0, c0 + DN_GROUP)
            env = [first(c) for c in chunks]
            for c, e in zip(chunks, env):
                second(c, e)
            for e in env:
                third(e)
            for c, e in zip(chunks, env):
                last(c, e)

    sd = jax.ShapeDtypeStruct
    f32 = jnp.float32
    return pl.pallas_call(
        body, name="dn_bwd_chunks", grid=(DN_HEADS, n_chunks // DN_CB),
        out_shape=(sd(qkv.shape, f32), sd(beta.shape, f32), sd(gc.shape, f32)),
        in_specs=planes + [rowv, rowv, sq, tok, tok, tok, tok, tok, tok, sq, lane],
        out_specs=(all_planes, rowv, rowv),
        compiler_params=pltpu.CompilerParams(dimension_semantics=("parallel", "parallel")),
    )(qkv, qkv, qkv, beta, gc, t, u, w, du, dw, dqd, dkd, dintra, dgl)


@jax.custom_vjp
def _delta_rule_op(qkv, beta, gc):
    return _delta_rule_fwd(qkv, beta, gc)[0]


def _delta_rule_fwd(qkv, beta, gc):
    u, w, qd, kd, intra, t, cd = _dn_prep(qkv, beta, gc)
    out, vn, st = _dn_scan(u, w, qd, kd, intra, cd)
    return out, (qkv, beta, gc, u, w, qd, kd, intra, t, cd, vn, st)


def _delta_rule_bwd(res, do):
    qkv, beta, gc, u, w, qd, kd, intra, t, cd, vn, st = res
    du, dw, dqd, dkd, dintra, dgl = _dn_bwd_scan(do, w, qd, kd, intra, cd, vn, st)
    return _dn_bwd_chunks(qkv, beta, gc, t, u, w, du, dw, dqd, dkd, dintra, dgl)


_delta_rule_op.defvjp(_delta_rule_fwd, _delta_rule_bwd)


def _gated_delta_rule(qkv, g, beta):
    s, h = g.shape
    n_chunks = s // CHUNK
    gc = jnp.cumsum(g.T.reshape(h, n_chunks, CHUNK), axis=-1)
    return _delta_rule_op(qkv, beta.T.reshape(h, n_chunks, CHUNK), gc)


PRE_ROWS = 256
HALO = 8
PRE_W = DN_QK_W


def _shift_rows(xs, k):
    return pltpu.roll(xs, k, 0)[HALO:]


def _conv_silu(x_ref, halo_ref, w_ref, first_block):
    halo = jnp.where(first_block, 0.0, halo_ref[...])
    xs = jnp.concatenate([halo, x_ref[...]], axis=0)
    taps = [_shift_rows(xs, CONV_WIDTH - 1 - j) for j in range(CONV_WIDTH - 1)] + [x_ref[...]]
    conv = sum(w_ref[pl.ds(j, 1), :] * taps[j] for j in range(CONV_WIDTH))
    return conv, jax.nn.sigmoid(conv), taps


def _pre_specs():
    blk = pl.BlockSpec((PRE_ROWS, PRE_W), lambda j, i: (i, j))
    prev = pl.BlockSpec((HALO, PRE_W), lambda j, i: (jnp.maximum(i * (PRE_ROWS // HALO) - 1, 0), j))
    wts = pl.BlockSpec((CONV_WIDTH, PRE_W), lambda j, i: (0, j))
    plane = pl.BlockSpec((None, PRE_ROWS, PRE_W), lambda j, i: (j, i, 0))
    return blk, prev, wts, plane


def _pre_fwd_call(x, conv_w):
    s = x.shape[0]
    blk, prev, wts, plane = _pre_specs()

    def body(x_ref, halo_ref, w_ref, o_ref):
        conv, sig, _ = _conv_silu(x_ref, halo_ref, w_ref, pl.program_id(1) == 0)
        act = conv * sig
        is_v = pl.program_id(0) == 2
        for h in range(DN_HEADS):
            cols = slice(h * DN_DK, (h + 1) * DN_DK)
            a_h = act[:, cols]
            r = lax.rsqrt(jnp.sum(a_h * a_h, axis=-1, keepdims=True) + NORM_EPS)
            o_ref[:, cols] = a_h * jnp.where(is_v, 1.0, r)

    return pl.pallas_call(
        body, name="pre_fwd", grid=(3, s // PRE_ROWS),
        out_shape=jax.ShapeDtypeStruct((3, s, PRE_W), jnp.float32),
        in_specs=[blk, prev, wts], out_specs=plane,
        compiler_params=pltpu.CompilerParams(dimension_semantics=("parallel", "parallel")),
    )(x, x, conv_w)


def _pre_bwd_act_call(x, conv_w, d_out):
    s = x.shape[0]
    blk, prev, wts, plane = _pre_specs()

    def body(x_ref, halo_ref, w_ref, do_ref, dc_ref):
        conv, sig, _ = _conv_silu(x_ref, halo_ref, w_ref, pl.program_id(1) == 0)
        act = conv * sig
        d_silu = sig * (1.0 + conv * (1.0 - sig))
        is_v = pl.program_id(0) == 2
        for h in range(DN_HEADS):
            cols = slice(h * DN_DK, (h + 1) * DN_DK)
            a_h, do_h = act[:, cols], do_ref[:, cols]
            r = lax.rsqrt(jnp.sum(a_h * a_h, axis=-1, keepdims=True) + NORM_EPS)
            n_h = a_h * r
            d_norm = r * (do_h - n_h * jnp.sum(do_h * n_h, axis=-1, keepdims=True))
            dc_ref[:, cols] = jnp.where(is_v, do_h, d_norm) * d_silu[:, cols]

    return pl.pallas_call(
        body, name="pre_bwd_act", grid=(3, s // PRE_ROWS),
        out_shape=jax.ShapeDtypeStruct(x.shape, jnp.float32),
        in_specs=[blk, prev, wts, plane], out_specs=blk,
        compiler_params=pltpu.CompilerParams(dimension_semantics=("parallel", "parallel")),
    )(x, x, conv_w, d_out)


def _pre_bwd_conv_call(x, conv_w, dc):
    s = x.shape[0]
    n_blocks = s // PRE_ROWS
    blk, prev, wts, _ = _pre_specs()
    nxt = pl.BlockSpec((HALO, PRE_W), lambda j, i: (jnp.minimum((i + 1) * (PRE_ROWS // HALO), s // HALO - 1), j))

    def body(x_ref, halo_ref, w_ref, dc_ref, dcn_ref, dx_ref, dw_ref):
        i = pl.program_id(1)

        @pl.when(i == 0)
        def _():
            dw_ref[...] = jnp.zeros_like(dw_ref)

        dcv = dc_ref[...]
        ahead = jnp.concatenate([dcv, jnp.where(i == n_blocks - 1, 0.0, dcn_ref[...])], axis=0)
        dx = w_ref[pl.ds(CONV_WIDTH - 1, 1), :] * dcv
        for j in range(CONV_WIDTH - 1):
            k = CONV_WIDTH - 1 - j
            dx = dx + w_ref[pl.ds(j, 1), :] * pltpu.roll(ahead, PRE_ROWS + HALO - k, 0)[:PRE_ROWS]
        dx_ref[...] = dx
        halo = jnp.where(i == 0, 0.0, halo_ref[...])
        xs = jnp.concatenate([halo, x_ref[...]], axis=0)
        for j in range(CONV_WIDTH):
            tap = x_ref[...] if j == CONV_WIDTH - 1 else _shift_rows(xs, CONV_WIDTH - 1 - j)
            dw_ref[pl.ds(j, 1), :] += jnp.sum(dcv * tap, axis=0, keepdims=True)

    sd = jax.ShapeDtypeStruct
    return pl.pallas_call(
        body, name="pre_bwd_conv", grid=(3, n_blocks),
        out_shape=(sd(x.shape, jnp.float32), sd(conv_w.shape, jnp.float32)),
        in_specs=[blk, prev, wts, blk, nxt], out_specs=(blk, wts),
        compiler_params=pltpu.CompilerParams(dimension_semantics=("parallel", "arbitrary")),
    )(x, x, conv_w, dc, dc)


@jax.custom_vjp
def _pre_op(x, conv_w):
    return _pre_fwd_call(x, conv_w)


def _pre_op_fwd(x, conv_w):
    return _pre_fwd_call(x, conv_w), (x, conv_w)


def _pre_op_bwd(res, d_out):
    x, conv_w = res
    return _pre_bwd_conv_call(x, conv_w, _pre_bwd_act_call(x, conv_w, d_out))


_pre_op.defvjp(_pre_op_fwd, _pre_op_bwd)


def _project(h, h_lo, w, slot):
    b, s, d = h.shape
    return _linear(h.reshape(b * s, d), h_lo.reshape(b * s, d), w, slot).reshape(b, s, w.shape[1])


def _rope_table(positions, dh):
    inv_freq = ROPE_THETA ** (-jnp.arange(0, dh, 2, dtype=jnp.float32) / dh)
    ang = positions.astype(jnp.float32)[:, None] * inv_freq
    reps = 128 // (dh // 2)
    return jnp.concatenate([jnp.tile(jnp.cos(ang), (1, reps)), jnp.tile(jnp.sin(ang), (1, reps))], axis=-1)


GATE_ROWS = 512


def _gate_terms(o_h, z_h):
    r = lax.rsqrt(jnp.mean(o_h * o_h, axis=-1, keepdims=True) + NORM_EPS)
    sig = jax.nn.sigmoid(z_h)
    return r, o_h * r, sig, z_h * sig


def _gate_fwd_call(o, z, nw):
    tok = pl.BlockSpec((GATE_ROWS, DN_V_W), lambda i: (i, 0))
    vec = pl.BlockSpec((1, DN_DV), lambda i: (0, 0))

    def body(o_ref, z_ref, nw_ref, y_ref):
        for h in range(DN_HEADS):
            cols = pl.ds(h * DN_DV, DN_DV)
            _, n_h, _, g_h = _gate_terms(o_ref[:, cols], z_ref[:, cols])
            y_ref[:, cols] = n_h * nw_ref[...] * g_h

    return pl.pallas_call(
        body, name="gate_fwd", grid=(o.shape[0] // GATE_ROWS,),
        out_shape=jax.ShapeDtypeStruct(o.shape, jnp.float32), in_specs=[tok, tok, vec], out_specs=tok,
        compiler_params=pltpu.CompilerParams(dimension_semantics=("parallel",)),
    )(o, z, nw)


def _gate_bwd_call(o, z, nw, dy):
    tok = pl.BlockSpec((GATE_ROWS, DN_V_W), lambda i: (i, 0))
    vec = pl.BlockSpec((1, DN_DV), lambda i: (0, 0))

    def body(o_ref, z_ref, nw_ref, dy_ref, do_ref, dz_ref, dnw_ref):
        @pl.when(pl.program_id(0) == 0)
        def _():
            dnw_ref[...] = jnp.zeros_like(dnw_ref)

        for h in range(DN_HEADS):
            cols = pl.ds(h * DN_DV, DN_DV)
            z_h, dy_h = z_ref[:, cols], dy_ref[:, cols]
            r, n_h, sig, g_h = _gate_terms(o_ref[:, cols], z_h)
            dz_ref[:, cols] = dy_h * n_h * nw_ref[...] * (sig * (1.0 + z_h * (1.0 - sig)))
            dn = dy_h * nw_ref[...] * g_h
            do_ref[:, cols] = r * (dn - n_h * jnp.mean(dn * n_h, axis=-1, keepdims=True))
            dnw_ref[...] += jnp.sum(dy_h * n_h * g_h, axis=0, keepdims=True)

    sd = jax.ShapeDtypeStruct
    return pl.pallas_call(
        body, name="gate_bwd", grid=(o.shape[0] // GATE_ROWS,),
        out_shape=(sd(o.shape, jnp.float32), sd(o.shape, jnp.float32), sd(nw.shape, jnp.float32)),
        in_specs=[tok, tok, vec, tok], out_specs=(tok, tok, vec),
        compiler_params=pltpu.CompilerParams(dimension_semantics=("arbitrary",)),
    )(o, z, nw, dy)


@jax.custom_vjp
def _gate_op(o, z, nw):
    return _gate_fwd_call(o, z, nw)


def _gate_op_fwd(o, z, nw):
    return _gate_fwd_call(o, z, nw), (o, z, nw)


def _gate_op_bwd(res, dy):
    return _gate_bwd_call(*res, dy)


_gate_op.defvjp(_gate_op_fwd, _gate_op_bwd)


_MASKED = -1e30


def _swa_probs(qs, k_h, sinks, valid):
    ss = [jnp.where(valid, _dot(q_h, k_h, 1, 1) * (SWA_DH ** -0.5), _MASKED) for q_h in qs]
    ms = [jnp.maximum(jnp.max(s, axis=-1, keepdims=True), sink) for s, sink in zip(ss, sinks)]
    ps = [jnp.exp(s - m) for s, m in zip(ss, ms)]
    es = [jnp.exp(sink - m) for sink, m in zip(sinks, ms)]
    invs = [1.0 / (jnp.sum(p, axis=-1, keepdims=True) + e) for p, e in zip(ps, es)]
    return [p * inv for p, inv in zip(ps, invs)], [e * inv for e, inv in zip(es, invs)]


def _swa_valid(n):
    qi = lax.broadcasted_iota(jnp.int32, (WINDOW, 2 * WINDOW), 0)
    kj = lax.broadcasted_iota(jnp.int32, (WINDOW, 2 * WINDOW), 1)
    diff = qi + WINDOW - kj
    return (diff >= 0) & (diff < WINDOW) & ((kj >= WINDOW) | (n > 0))


def _rotate_half(x, transpose=False):
    half = SWA_DH // 2
    lower = lax.broadcasted_iota(jnp.int32, x.shape, 1) % SWA_DH < half
    ahead, behind = pltpu.roll(x, 128 - half, 1), pltpu.roll(x, half, 1)
    return jnp.where(lower, ahead, -behind) if transpose else jnp.where(lower, -ahead, behind)


def _rope(x, table):
    return x * table[:, :128] + _rotate_half(x) * table[:, 128:]


def _unrope(dy, table):
    return dy * table[:, :128] + _rotate_half(dy * table[:, 128:], transpose=True)


def _swa_specs():
    qs = pl.BlockSpec((WINDOW, SWA_Q_W), lambda n: (n, 0))
    first = lambda n: jnp.maximum(n - 1, 0)
    kv = [pl.BlockSpec((WINDOW, SWA_KV_W), lambda n: (first(n), 0)), pl.BlockSpec((WINDOW, SWA_KV_W), lambda n: (n, 0)),
          pl.BlockSpec((WINDOW, SWA_KV_W), lambda n: (first(n), 1)), pl.BlockSpec((WINDOW, SWA_KV_W), lambda n: (n, 1))]
    tables = [pl.BlockSpec((WINDOW, 256), lambda n: (first(n), 0)), pl.BlockSpec((WINDOW, 256), lambda n: (n, 0))]
    cur = pl.BlockSpec((WINDOW, SWA_KV_W), lambda n: (n, 0))
    sk = pl.BlockSpec((SWA_HEADS, 1, 128), lambda n: (0, 0, 0))
    return qs, kv, tables, cur, sk


def _swa_load(q_ref, kp_ref, kc_ref, vp_ref, vc_ref, tp_ref, tc_ref):
    table_kk = jnp.concatenate([tp_ref[...], tc_ref[...]], axis=0)
    kk = _rope(jnp.concatenate([kp_ref[...], kc_ref[...]], axis=0), table_kk)
    vv = jnp.concatenate([vp_ref[...], vc_ref[...]], axis=0)
    q_rot = []
    for b in range(SWA_Q_W // 128):
        pair = _rope(q_ref[:, pl.ds(b * 128, 128)], tc_ref[...])
        q_rot += [pair[:, :SWA_DH], pair[:, SWA_DH:]]
    split = lambda t: [t[:, hkv * SWA_DH:(hkv + 1) * SWA_DH] for hkv in range(SWA_KV_HEADS)]
    return q_rot, split(kk), split(vv), table_kk


def _swa_fwd_call(q, kv, table, sinks):
    qs, kvs, tables, _, sk = _swa_specs()

    def body(q_ref, kp_ref, kc_ref, vp_ref, vc_ref, tp_ref, tc_ref, sink_ref, o_ref):
        valid = _swa_valid(pl.program_id(0))
        q_rot, kk, vv, _ = _swa_load(q_ref, kp_ref, kc_ref, vp_ref, vc_ref, tp_ref, tc_ref)
        for hkv in range(SWA_KV_HEADS):
            heads = range(hkv * SWA_GROUP, (hkv + 1) * SWA_GROUP)
            probs, _ = _swa_probs([q_rot[h] for h in heads], kk[hkv], [sink_ref[h][:, :1] for h in heads], valid)
            outs = [_dot(p, vv[hkv]) for p in probs]
            for h, o in zip(heads, outs):
                o_ref[:, pl.ds(h * SWA_DH, SWA_DH)] = o

    return pl.pallas_call(
        body, name="swa_fwd", grid=(q.shape[0] // WINDOW,),
        out_shape=jax.ShapeDtypeStruct(q.shape, jnp.float32),
        in_specs=[qs] + kvs + tables + [sk], out_specs=qs,
        compiler_params=pltpu.CompilerParams(dimension_semantics=("parallel",)),
    )(q, kv, kv, kv, kv, table, table, sinks)


def _swa_bwd_call(q, kv, table, sinks, do):
    qs, kvs, tables, cur, sk = _swa_specs()

    def body(q_ref, kp_ref, kc_ref, vp_ref, vc_ref, tp_ref, tc_ref, sink_ref, do_ref,
             dq_ref, dkc_ref, dkp_ref, dvc_ref, dvp_ref, ds_ref):
        @pl.when(pl.program_id(0) == 0)
        def _():
            ds_ref[...] = jnp.zeros_like(ds_ref)

        valid = _swa_valid(pl.program_id(0))
        q_rot, kk, vv, table_kk = _swa_load(q_ref, kp_ref, kc_ref, vp_ref, vc_ref, tp_ref, tc_ref)
        lane0 = lax.broadcasted_iota(jnp.int32, (1, 128), 1) == 0
        dq_heads, dk_heads, dv_heads = [], [], []
        for hkv in range(SWA_KV_HEADS):
            k_h, v_h = kk[hkv], vv[hkv]
            heads = range(hkv * SWA_GROUP, (hkv + 1) * SWA_GROUP)
            q_hs = [q_rot[h] for h in heads]
            dos = [do_ref[:, pl.ds(h * SWA_DH, SWA_DH)] for h in heads]
            probs, p_sinks = _swa_probs(q_hs, k_h, [sink_ref[h][:, :1] for h in heads], valid)
            dps = [_dot(do_h, v_h, 1, 1) for do_h in dos]
            rss = [jnp.sum(p * dp, axis=-1, keepdims=True) for p, dp in zip(probs, dps)]
            d_ss = [p * (dp - rs) for p, dp, rs in zip(probs, dps, rss)]
            dq_heads += [_dot(d_s, k_h) * (SWA_DH ** -0.5) for d_s in d_ss]
            dks = [_dot(d_s, q_h, 0, 0) for d_s, q_h in zip(d_ss, q_hs)]
            dvs = [_dot(p, do_h, 0, 0) for p, do_h in zip(probs, dos)]
            for h, p_sink, rs in zip(heads, p_sinks, rss):
                d_sink = -jnp.sum(p_sink * rs, axis=0, keepdims=True)
                ds_ref[h] += jnp.where(lane0, d_sink, 0.0)
            dk_heads.append(sum(dks[1:], dks[0]) * (SWA_DH ** -0.5))
            dv_heads.append(sum(dvs[1:], dvs[0]))
        for b in range(SWA_Q_W // 128):
            pair = jnp.concatenate([dq_heads[2 * b], dq_heads[2 * b + 1]], axis=1)
            dq_ref[:, pl.ds(b * 128, 128)] = _unrope(pair, tc_ref[...])
        dk = _unrope(jnp.concatenate(dk_heads, axis=1), table_kk)
        dv = jnp.concatenate(dv_heads, axis=1)
        dkp_ref[...] = dk[:WINDOW]
        dkc_ref[...] = dk[WINDOW:]
        dvp_ref[...] = dv[:WINDOW]
        dvc_ref[...] = dv[WINDOW:]

    sd = jax.ShapeDtypeStruct
    f32 = jnp.float32
    half = (q.shape[0], SWA_KV_W)
    return pl.pallas_call(
        body, name="swa_bwd", grid=(q.shape[0] // WINDOW,),
        out_shape=(sd(q.shape, f32), sd(half, f32), sd(half, f32), sd(half, f32), sd(half, f32), sd(sinks.shape, f32)),
        in_specs=[qs] + kvs + tables + [sk, qs], out_specs=(qs, cur, cur, cur, cur, sk),
        compiler_params=pltpu.CompilerParams(dimension_semantics=("arbitrary",)),
    )(q, kv, kv, kv, kv, table, table, sinks, do)


@jax.custom_vjp
def _swa_op(q, kv, table, sinks):
    return _swa_fwd_call(q, kv, table, sinks)


def _swa_op_fwd(q, kv, table, sinks):
    return _swa_fwd_call(q, kv, table, sinks), (q, kv, table, sinks)


def _swa_op_bwd(res, do):
    q, kv, table, sinks = res
    dq, dkc, dkp, dvc, dvp, dsinks = _swa_bwd_call(q, kv, table, sinks, do)

    def fold(cur, prev):
        return cur + jnp.concatenate([prev[WINDOW:], jnp.zeros_like(prev[:WINDOW])], axis=0)

    return dq, jnp.concatenate([fold(dkc, dkp), fold(dvc, dvp)], axis=1), jnp.zeros_like(table), dsinks


_swa_op.defvjp(_swa_op_fwd, _swa_op_bwd)


def _swa_sink_attention(q, kv, table, sinks):
    return _swa_op(q, kv, table, jnp.broadcast_to(sinks[:, None, None], (SWA_HEADS, 1, 128)))


MEM_ROWS = 512


def _mem_probs(q_h, k_h):
    s = _dot(q_h, k_h, 1, 1) * (MEM_DH ** -0.5)
    p = jnp.exp(s - jnp.max(s, axis=-1, keepdims=True))
    return p / jnp.sum(p, axis=-1, keepdims=True)


def _mem_fwd_call(qm, kv):
    qs = pl.BlockSpec((MEM_ROWS, MEM_W), lambda i: (i, 0))
    kvs = pl.BlockSpec(kv.shape, lambda i: (0, 0))

    def body(q_ref, kv_ref, o_ref):
        for h in range(MEM_HEADS):
            cols = pl.ds(h * MEM_DH, MEM_DH)
            probs = _mem_probs(q_ref[:, cols], kv_ref[:, cols])
            o_ref[:, cols] = _dot(probs, kv_ref[:, pl.ds(MEM_W + h * MEM_DH, MEM_DH)])

    return pl.pallas_call(
        body, name="mem_fwd", grid=(qm.shape[0] // MEM_ROWS,),
        out_shape=jax.ShapeDtypeStruct(qm.shape, jnp.float32), in_specs=[qs, kvs], out_specs=qs,
        compiler_params=pltpu.CompilerParams(dimension_semantics=("parallel",)),
    )(qm, kv)


def _mem_bwd_call(qm, kv, do):
    qs = pl.BlockSpec((MEM_ROWS, MEM_W), lambda i: (i, 0))
    kvs = pl.BlockSpec(kv.shape, lambda i: (0, 0))

    def body(q_ref, kv_ref, do_ref, dq_ref, dkv_ref):
        @pl.when(pl.program_id(0) == 0)
        def _():
            dkv_ref[...] = jnp.zeros_like(dkv_ref)

        for h in range(MEM_HEADS):
            cols = pl.ds(h * MEM_DH, MEM_DH)
            v_cols = pl.ds(MEM_W + h * MEM_DH, MEM_DH)
            q_h, k_h, do_h = q_ref[:, cols], kv_ref[:, cols], do_ref[:, cols]
            probs = _mem_probs(q_h, k_h)
            dp = _dot(do_h, kv_ref[:, v_cols], 1, 1)
            d_s = probs * (dp - jnp.sum(probs * dp, axis=-1, keepdims=True))
            dq_ref[:, cols] = _dot(d_s, k_h) * (MEM_DH ** -0.5)
            dkv_ref[:, cols] += _dot(d_s, q_h, 0, 0) * (MEM_DH ** -0.5)
            dkv_ref[:, v_cols] += _dot(probs, do_h, 0, 0)

    sd = jax.ShapeDtypeStruct
    return pl.pallas_call(
        body, name="mem_bwd", grid=(qm.shape[0] // MEM_ROWS,),
        out_shape=(sd(qm.shape, jnp.float32), sd(kv.shape, jnp.float32)),
        in_specs=[qs, kvs, qs], out_specs=(qs, kvs),
        compiler_params=pltpu.CompilerParams(dimension_semantics=("arbitrary",)),
    )(qm, kv, do)


@jax.custom_vjp
def _mem_op(qm, kv):
    return _mem_fwd_call(qm, kv)


def _mem_op_fwd(qm, kv):
    return _mem_fwd_call(qm, kv), (qm, kv)


def _mem_op_bwd(res, do):
    return _mem_bwd_call(*res, do)


_mem_op.defvjp(_mem_op_fwd, _mem_op_bwd)


def _memory_attention(qm, kv):
    return _mem_op(qm[0], kv[0])[None]


def _mixer_a(h, h_lo, mem, mem_lo, p, s, layer):
    B, S, _ = h.shape
    proj = _project(h, h_lo, p["a_w_in"][layer], s["a_w_in"][layer])
    c1 = 2 * DN_QK_W + DN_V_W
    qkv = proj[..., :c1]
    z = proj[..., c1:QKVZ_W]
    qm = proj[..., QKVZ_W:QKVZ_W + MEM_W]
    a = proj[..., QKVZ_W + MEM_W:QKVZ_W + MEM_W + DN_HEADS]
    b = proj[..., QKVZ_W + MEM_W + DN_HEADS:QKVZ_W + MEM_W + 2 * DN_HEADS]
    planes = _pre_op(qkv[0], p["a_conv_w"][layer])
    beta = jax.nn.sigmoid(b[0])
    g = -jnp.exp(p["a_A_log"][layer]) * jax.nn.softplus(a[0] + p["a_dt_bias"][layer])
    o = _gate_op(_gated_delta_rule(planes, g, beta), z[0], p["a_norm_w"][layer][None])[None]
    kv = _project(mem, mem_lo, p["mem_w_kv"][layer], s["mem_w_kv"][layer])
    mo = _memory_attention(qm, kv)
    cat = jnp.concatenate([o, mo], axis=-1)
    return _project(cat, _lo(cat), p["w_o"][layer], s["w_o"][layer])


def _mixer_b(h, h_lo, mem, mem_lo, kv_shared, table, p, s, layer):
    j = layer - N_A
    proj = _project(h, h_lo, p["b_w_in"][j], s["b_w_in"][j])
    o = _swa_sink_attention(proj[0, :, :SWA_Q_W], kv_shared, table, p["b_sinks"][j])[None]
    kv = _project(mem, mem_lo, p["mem_w_kv"][layer], s["mem_w_kv"][layer])
    mo = _memory_attention(proj[..., SWA_Q_W:], kv)
    cat = jnp.concatenate([o, mo], axis=-1)
    return _project(cat, _lo(cat), p["w_o"][layer], s["w_o"][layer])


def _forward(p, s, x, mem, positions):
    table = _rope_table(positions[0], SWA_DH)
    h, h_lo, mem_lo = x, _lo(x), _lo(mem)
    kv_shared = None
    for layer in range(DEPTH):
        if layer < N_A:
            mix = _mixer_a(h, h_lo, mem, mem_lo, p, s, layer)
        else:
            mix = _mixer_b(h, h_lo, mem, mem_lo, kv_shared, table, p, s, layer)
        seq = h.shape[1]
        h2, h2_lo = _ln_res(h[0], mix[0], p["ln_g"][layer, 0][None], p["ln_b"][layer, 0][None])
        down = _mlp(h2, h2_lo, p["mlp_w_up"][layer], p["mlp_w_down"][layer], s["mlp_w_up"][layer],
                    s["mlp_w_down"][layer])
        h, h_lo = _ln_res(h2, down, p["ln_g"][layer, 1][None], p["ln_b"][layer, 1][None])
        h, h_lo = h.reshape(1, seq, D_MODEL), h_lo.reshape(1, seq, D_MODEL)
        if layer == N_A - 1:
            kv_shared = _project(h, h_lo, p["w_kv_shared"], s["w_kv_shared"])[0]
    return h


def _loss(diff, s, p, mem, positions, target):
    y = _forward({**p, **diff["small"]}, s, diff["x"], mem, positions)
    return 0.5 * jnp.sum(jnp.mean(jnp.square(y - target), axis=-1))


def _reorder_a_w_in(w):
    pad = jnp.zeros(w.shape[:-1] + (A_IN_PAD - A_IN,), w.dtype)
    return jnp.concatenate([w[..., :QKVZ_W], w[..., QKVZ_W + 2 * DN_HEADS:], w[..., QKVZ_W:QKVZ_W + 2 * DN_HEADS], pad],
                           axis=-1)


def _restore_a_w_in(w):
    return jnp.concatenate([w[..., :QKVZ_W], w[..., QKVZ_W + MEM_W:QKVZ_W + MEM_W + 2 * DN_HEADS],
                            w[..., QKVZ_W:QKVZ_W + MEM_W]], axis=-1)


def kernel(x, mem, positions, a_w_in, a_conv_w, a_A_log, a_dt_bias, a_norm_w, b_w_in, b_sinks, w_kv_shared, mem_w_kv, w_o, mlp_w_up, mlp_w_down, ln_g, ln_b, loss_target, m_a_w_in, m_a_conv_w, m_a_A_log, m_a_dt_bias, m_a_norm_w, m_b_w_in, m_b_sinks, m_w_kv_shared, m_mem_w_kv, m_w_o, m_mlp_w_up, m_mlp_w_down, m_ln_g, m_ln_b, v_a_w_in, v_a_conv_w, v_a_A_log, v_a_dt_bias, v_a_norm_w, v_b_w_in, v_b_sinks, v_w_kv_shared, v_mem_w_kv, v_w_o, v_mlp_w_up, v_mlp_w_down, v_ln_g, v_ln_b):
    w_sh = dict(a_w_in=a_w_in, a_conv_w=a_conv_w, a_A_log=a_A_log, a_dt_bias=a_dt_bias, a_norm_w=a_norm_w,
                b_w_in=b_w_in, b_sinks=b_sinks, w_kv_shared=w_kv_shared, mem_w_kv=mem_w_kv, w_o=w_o,
                mlp_w_up=mlp_w_up, mlp_w_down=mlp_w_down, ln_g=ln_g, ln_b=ln_b)
    m_sh = dict(a_w_in=m_a_w_in, a_conv_w=m_a_conv_w, a_A_log=m_a_A_log, a_dt_bias=m_a_dt_bias, a_norm_w=m_a_norm_w,
                b_w_in=m_b_w_in, b_sinks=m_b_sinks, w_kv_shared=m_w_kv_shared, mem_w_kv=m_mem_w_kv, w_o=m_w_o,
                mlp_w_up=m_mlp_w_up, mlp_w_down=m_mlp_w_down, ln_g=m_ln_g, ln_b=m_ln_b)
    v_sh = dict(a_w_in=v_a_w_in, a_conv_w=v_a_conv_w, a_A_log=v_a_A_log, a_dt_bias=v_a_dt_bias, a_norm_w=v_a_norm_w,
                b_w_in=v_b_w_in, b_sinks=v_b_sinks, w_kv_shared=v_w_kv_shared, mem_w_kv=v_mem_w_kv, w_o=v_w_o,
                mlp_w_up=v_mlp_w_up, mlp_w_down=v_mlp_w_down, ln_g=v_ln_g, ln_b=v_ln_b)
    shard_shapes = {n: w_sh[n].shape for n in WEIGHTS}
    rb, rows = _rows_for(w_sh)

    big, small = _pack(w_sh, rb, jnp.bfloat16)
    gbig, gsmall = _gather_weights(big.reshape(2, rb // 2, FLAT_W), small.reshape(2, SMALL_ROWS // 2, FLAT_W))
    gbig, gsmall = gbig.reshape(N_CHIPS, rb, FLAT_W), gsmall.reshape(N_CHIPS, SMALL_ROWS, FLAT_W)
    pieces = [_unpack(gbig[q], gsmall[q], shard_shapes) for q in range(N_CHIPS)]
    full = {n: jnp.concatenate([pieces[q][n] for q in range(N_CHIPS)], axis=SHARD_AXIS[n]) for n in SHARD_AXIS}
    for n in REPLICATED:
        full[n] = w_sh[n]
    big_w = {n: full[n] for n in BIG}
    big_w["a_w_in"] = _reorder_a_w_in(big_w["a_w_in"])
    small_w = {n: full[n] for n in SMALL}
    slots = {n: jnp.zeros(big_w[n].shape, jnp.float32) for n in BIG}

    loss, (grads, g_slots) = jax.value_and_grad(_loss, argnums=(0, 1))(
        {"x": x, "small": small_w}, slots, big_w, mem, positions, loss_target)
    loss = lax.psum(loss, ("x", "y", "c"))
    g_full = {**g_slots, **grads["small"]}
    g_full["a_w_in"] = _restore_a_w_in(g_full["a_w_in"])

    def shard_of(n, q):
        if n in REPLICATED:
            return g_full[n]
        size = shard_shapes[n][SHARD_AXIS[n]]
        return lax.slice_in_dim(g_full[n], q * size, (q + 1) * size, axis=SHARD_AXIS[n])

    parts = []
    for q in range(N_CHIPS):
        pb, ps = _pack({n: shard_of(n, q) for n in WEIGHTS}, rb, jnp.bfloat16)
        parts.append(jnp.concatenate([pb, ps.astype(jnp.bfloat16)], axis=0).reshape(2, rows // 2, FLAT_W))
    partials = jnp.stack(parts, axis=1)
    half = lax.axis_index("c").astype(jnp.int32).reshape(1)
    chip_partials = _add_pairs(partials, _swap_halves(partials), half)
    g_flat = _join_halves(_sum_chips(_scatter_grads(chip_partials))).reshape(rows, FLAT_W)

    flat = [jnp.concatenate(_pack(d, rb), axis=0) for d in (w_sh, m_sh, v_sh)]
    outs = (g_flat,) + tuple(_adamw(g_flat, *flat))
    g_o, d_o, m_o, v_o = [_unpack(o[:rb], o[rb:], shard_shapes) for o in outs]
    return (loss, grads["x"], *[g_o[n] for n in WEIGHTS], *[d_o[n] for n in WEIGHTS],
            *[m_o[n] for n in WEIGHTS], *[v_o[n] for n in WEIGHTS])
```

```python
import functools
import math

import jax
import jax.numpy as jnp
from jax import lax
from jax.experimental import pallas as pl
from jax.experimental.pallas import tpu as pltpu

D_MODEL = 1024
DEPTH = 4
N_A = DEPTH // 2
N_B = DEPTH - N_A
MEM_HEADS = 4
MEM_DH = D_MODEL // 16
MEM_W = MEM_HEADS * MEM_DH
DN_DK = 128
DN_DV = 128
DN_HEADS = (3 * D_MODEL) // (4 * DN_DV)
DN_QK_W = DN_HEADS * DN_DK
DN_V_W = DN_HEADS * DN_DV
CONV_WIDTH = 4
CHUNK = 64
SWA_DH = 64
SWA_HEADS = (3 * D_MODEL) // (4 * SWA_DH)
SWA_KV_HEADS = 2
SWA_GROUP = SWA_HEADS // SWA_KV_HEADS
SWA_Q_W = SWA_HEADS * SWA_DH
SWA_KV_W = SWA_KV_HEADS * SWA_DH
WINDOW = 128
ROPE_THETA = 10000.0
MLP_HIDDEN = 4 * D_MODEL
LN_EPS = 1e-5
NORM_EPS = 1e-6
DN_ALPHA = (2.0 * DEPTH) ** 0.25
A_IN = 2 * DN_QK_W + 2 * DN_V_W + 2 * DN_HEADS + MEM_W
A_IN_PAD = 3456
QKVZ_W = 2 * DN_QK_W + 2 * DN_V_W

ADAM_LR = 0.001
ADAM_B1 = 0.9
ADAM_B2 = 0.999
ADAM_EPS = 1e-08
ADAM_WD = 0.01
ADAM_STEP = 10

N_CHIPS = 4
FLAT_W = 1024
BIG = ("a_w_in", "b_w_in", "w_kv_shared", "mem_w_kv", "w_o", "mlp_w_up", "mlp_w_down")
SMALL = ("a_conv_w", "ln_g", "ln_b", "a_A_log", "a_dt_bias", "a_norm_w", "b_sinks")
REPLICATED = ("a_A_log", "a_dt_bias", "a_norm_w", "b_sinks")
WEIGHTS = ("a_w_in", "a_conv_w", "a_A_log", "a_dt_bias", "a_norm_w", "b_w_in", "b_sinks", "w_kv_shared",
           "mem_w_kv", "w_o", "mlp_w_up", "mlp_w_down", "ln_g", "ln_b")
SHARD_AXIS = {"a_w_in": 2, "a_conv_w": 2, "b_w_in": 1, "w_kv_shared": 0, "mem_w_kv": 1, "w_o": 1,
              "mlp_w_up": 2, "mlp_w_down": 1, "ln_g": 2, "ln_b": 2}
SMALL_ROWS = 32
ROW_ALIGN = 256

MESH = pl.DeviceIdType.MESH
HBM_SPEC = pl.BlockSpec(memory_space=pltpu.HBM)
VMEM_LIMIT = 48 * 1024 * 1024


def _rows_for(shards):
    n_big = sum(math.prod(shards[n].shape) for n in BIG)
    n_small = sum(math.prod(shards[n].shape) for n in SMALL)
    assert n_small <= SMALL_ROWS * FLAT_W
    total = -(-n_big // FLAT_W) + SMALL_ROWS
    total = -(-total // (2 * ROW_ALIGN)) * (2 * ROW_ALIGN)
    return total - SMALL_ROWS, total


def _pack(shards, rb, dtype_big=jnp.float32):
    big = jnp.concatenate([shards[n].reshape(-1).astype(dtype_big) for n in BIG])
    big = jnp.pad(big, (0, rb * FLAT_W - big.shape[0])).reshape(rb, FLAT_W)
    small = jnp.concatenate([shards[n].reshape(-1).astype(jnp.float32) for n in SMALL])
    small = jnp.pad(small, (0, SMALL_ROWS * FLAT_W - small.shape[0])).reshape(SMALL_ROWS, FLAT_W)
    return big, small


def _unpack(big, small, shapes):
    out = {}
    for flat, names in ((big.reshape(-1), BIG), (small.reshape(-1), SMALL)):
        off = 0
        for n in names:
            size = math.prod(shapes[n])
            out[n] = flat[off:off + size].reshape(shapes[n])
            off += size
    return out


def _other_chips(x, y):
    return [(1 - x, y), (x, 1 - y), (1 - x, 1 - y)]


def _gather_weights(big, small):
    def body(big_ref, small_ref, init_big_ref, init_small_ref, obig_ref, osmall_ref,
             send_sems, recv_sems, pass_send_sems, pass_recv_sems):
        del init_big_ref, init_small_ref
        x, y, c = lax.axis_index("x"), lax.axis_index("y"), lax.axis_index("c")
        me = 2 * x + y
        sibling = (x, y, 1 - c)
        pairs = ((big_ref, obig_ref), (small_ref, osmall_ref))
        sends = []
        for j, (px, py) in enumerate(_other_chips(x, y)):
            for i, (src, dst) in enumerate(pairs):
                sends.append(pltpu.make_async_remote_copy(
                    src_ref=src.at[c], dst_ref=dst.at[me, c], send_sem=send_sems.at[2 * j + i],
                    recv_sem=recv_sems.at[2 * j + i], device_id=(px, py, c), device_id_type=MESH))
        for cp in sends:
            cp.start()
        passed = []
        for j, (px, py) in enumerate(_other_chips(x, y)):
            for i, (src, dst) in enumerate(pairs):
                landed = dst.at[2 * px + py, c]
                pltpu.make_async_remote_copy(
                    src_ref=src.at[c], dst_ref=landed, send_sem=send_sems.at[2 * j + i],
                    recv_sem=recv_sems.at[2 * j + i], device_id=(px, py, c), device_id_type=MESH).wait_recv()
                passed.append(pltpu.make_async_remote_copy(
                    src_ref=landed, dst_ref=landed, send_sem=pass_send_sems.at[2 * j + i],
                    recv_sem=pass_recv_sems.at[2 * j + i], device_id=sibling, device_id_type=MESH))
                passed[-1].start()
        for j, (px, py) in enumerate(_other_chips(x, y)):
            for i, (src, dst) in enumerate(pairs):
                other_half = dst.at[2 * px + py, 1 - c]
                pltpu.make_async_remote_copy(
                    src_ref=other_half, dst_ref=other_half, send_sem=pass_send_sems.at[2 * j + i],
                    recv_sem=pass_recv_sems.at[2 * j + i], device_id=sibling, device_id_type=MESH).wait_recv()
        for cp in sends + passed:
            cp.wait_send()

    dma6 = pltpu.SemaphoreType.DMA((6,))
    four = lambda t: jnp.broadcast_to(t[None], (N_CHIPS,) + t.shape)
    return pl.pallas_call(
        body, name="gather_weights",
        out_shape=(jax.ShapeDtypeStruct((N_CHIPS,) + big.shape, big.dtype),
                   jax.ShapeDtypeStruct((N_CHIPS,) + small.shape, small.dtype)),
        in_specs=[HBM_SPEC] * 4, out_specs=(HBM_SPEC, HBM_SPEC), input_output_aliases={2: 0, 3: 1},
        scratch_shapes=[dma6, dma6, dma6, dma6],
    )(big, small, four(big), four(small))


def _scatter_grads(g):
    def body(g_ref, o_ref, send_sems, recv_sems, local_sem):
        x, y, c = lax.axis_index("x"), lax.axis_index("y"), lax.axis_index("c")
        me = 2 * x + y
        local = pltpu.make_async_copy(g_ref.at[me], o_ref.at[me], local_sem)
        local.start()
        sends = []
        for j, (px, py) in enumerate(_other_chips(x, y)):
            sends.append(pltpu.make_async_remote_copy(
                src_ref=g_ref.at[2 * px + py], dst_ref=o_ref.at[me], send_sem=send_sems.at[j], recv_sem=recv_sems.at[j],
                device_id=(px, py, c), device_id_type=MESH))
        for cp in sends:
            cp.start()
        for j, (px, py) in enumerate(_other_chips(x, y)):
            pltpu.make_async_remote_copy(
                src_ref=g_ref.at[me], dst_ref=o_ref.at[2 * px + py], send_sem=send_sems.at[j], recv_sem=recv_sems.at[j],
                device_id=(px, py, c), device_id_type=MESH).wait_recv()
        for cp in sends:
            cp.wait_send()
        local.wait()

    return pl.pallas_call(
        body, name="scatter_grads",
        out_shape=jax.ShapeDtypeStruct(g.shape, g.dtype),
        in_specs=[HBM_SPEC], out_specs=HBM_SPEC,
        scratch_shapes=[pltpu.SemaphoreType.DMA((3,)), pltpu.SemaphoreType.DMA((3,)), pltpu.SemaphoreType.DMA],
    )(g)


def _swap_halves(g):
    def body(g_ref, o_ref, send_sem, recv_sem):
        x, y, c = lax.axis_index("x"), lax.axis_index("y"), lax.axis_index("c")
        cp = pltpu.make_async_remote_copy(src_ref=g_ref.at[1 - c], dst_ref=o_ref, send_sem=send_sem, recv_sem=recv_sem,
                                          device_id=(x, y, 1 - c), device_id_type=MESH)
        cp.start()
        cp.wait()

    return pl.pallas_call(
        body, name="swap_halves",
        out_shape=jax.ShapeDtypeStruct(g.shape[1:], g.dtype),
        in_specs=[HBM_SPEC], out_specs=HBM_SPEC,
        scratch_shapes=[pltpu.SemaphoreType.DMA, pltpu.SemaphoreType.DMA],
    )(g)


def _join_halves(v):
    def body(v_ref, init_ref, o_ref, send_sem, recv_sem):
        del init_ref
        x, y, c = lax.axis_index("x"), lax.axis_index("y"), lax.axis_index("c")
        cp = pltpu.make_async_remote_copy(src_ref=v_ref, dst_ref=o_ref.at[c], send_sem=send_sem, recv_sem=recv_sem,
                                          device_id=(x, y, 1 - c), device_id_type=MESH)
        cp.start()
        cp.wait_send()
        pltpu.make_async_remote_copy(src_ref=v_ref, dst_ref=o_ref.at[1 - c], send_sem=send_sem, recv_sem=recv_sem,
                                     device_id=(x, y, 1 - c), device_id_type=MESH).wait_recv()

    return pl.pallas_call(
        body, name="join_halves",
        out_shape=jax.ShapeDtypeStruct((2,) + v.shape, v.dtype),
        in_specs=[HBM_SPEC, HBM_SPEC], out_specs=HBM_SPEC, input_output_aliases={1: 0},
        scratch_shapes=[pltpu.SemaphoreType.DMA, pltpu.SemaphoreType.DMA],
    )(v, jnp.stack([v, v]))


def _add_pairs(g, theirs, half):
    _, n, rows, width = g.shape
    assert rows % ROW_ALIGN == 0, rows

    def body(half_ref, g_ref, t_ref, o_ref):
        o_ref[...] = (g_ref[...].astype(jnp.float32) + t_ref[...].astype(jnp.float32)).astype(o_ref.dtype)

    blk = pl.BlockSpec((None, ROW_ALIGN, width), lambda p, i, h: (p, i, 0))
    grid_spec = pltpu.PrefetchScalarGridSpec(
        num_scalar_prefetch=1, grid=(n, rows // ROW_ALIGN),
        in_specs=[pl.BlockSpec((None, None, ROW_ALIGN, width), lambda p, i, h: (h[0], p, i, 0)), blk], out_specs=blk)
    return pl.pallas_call(
        body, name="add_pairs", grid_spec=grid_spec, out_shape=jax.ShapeDtypeStruct(theirs.shape, g.dtype),
        compiler_params=pltpu.CompilerParams(dimension_semantics=("parallel", "parallel")),
    )(half, g, theirs)


def _sum_chips(parts):
    n, rows, width = parts.shape
    assert rows % ROW_ALIGN == 0, rows

    def body(p_ref, o_ref):
        p = [p_ref[q].astype(jnp.float32) for q in range(n)]
        o_ref[...] = (p[0] + p[1]) + (p[2] + p[3])

    return pl.pallas_call(
        body, name="sum_chips", grid=(rows // ROW_ALIGN,),
        out_shape=jax.ShapeDtypeStruct((rows, width), jnp.float32),
        in_specs=[pl.BlockSpec((n, ROW_ALIGN, width), lambda i: (0, i, 0))],
        out_specs=pl.BlockSpec((ROW_ALIGN, width), lambda i: (i, 0)),
        compiler_params=pltpu.CompilerParams(dimension_semantics=("parallel",), vmem_limit_bytes=VMEM_LIMIT),
    )(parts)


def _adamw(g, w, m, v):
    rows, width = w.shape
    blk = ROW_ALIGN // 2

    def body(g_ref, w_ref, m_ref, v_ref, d_out, m_out, v_out):
        g = g_ref[...]
        m_new = ADAM_B1 * m_ref[...] + (1.0 - ADAM_B1) * g
        v_new = ADAM_B2 * v_ref[...] + (1.0 - ADAM_B2) * jnp.square(g)
        m_hat = m_new / (1.0 - ADAM_B1 ** ADAM_STEP)
        v_hat = v_new / (1.0 - ADAM_B2 ** ADAM_STEP)
        d_out[...] = -ADAM_LR * (m_hat / (jnp.sqrt(v_hat) + ADAM_EPS) + ADAM_WD * w_ref[...])
        m_out[...] = m_new
        v_out[...] = v_new

    spec = pl.BlockSpec((blk, width), lambda i: (i, 0))
    shape = jax.ShapeDtypeStruct((rows, width), jnp.float32)
    return pl.pallas_call(
        body, name="adamw", grid=(rows // blk,),
        out_shape=(shape,) * 3, in_specs=[spec] * 4, out_specs=(spec,) * 3,
        compiler_params=pltpu.CompilerParams(dimension_semantics=("parallel",), vmem_limit_bytes=VMEM_LIMIT),
    )(g, w, m, v)


def _tile(dim, pref):
    if dim <= pref:
        return dim
    for t in range(pref - pref % 128, 0, -128):
        if dim % t == 0:
            return t
    raise ValueError(f"no 128-aligned tile for {dim}")


def _matmul(a, b, *, ta=False, tb=False, name, epilogue=None, extra=None, out_dtype=jnp.float32):
    (k_a, m) = a.shape if ta else a.shape[::-1]
    (k_b, n) = b.shape[::-1] if tb else b.shape
    assert k_a == k_b, (a.shape, b.shape, ta, tb)
    k = k_a
    tk = _tile(k, 1152)
    nk = k // tk
    if ta:
        tm, tn = _tile(m, 1024), _tile(n, 2048 if m <= 1024 else 1024)
    else:
        tm, tn = _tile(m, 2048), _tile(n, 512 if nk == 1 else 1024)
    a_spec = pl.BlockSpec((tk, tm), lambda i, j, l: (l, i)) if ta else pl.BlockSpec((tm, tk), lambda i, j, l: (i, l))
    b_spec = pl.BlockSpec((tn, tk), lambda i, j, l: (j, l)) if tb else pl.BlockSpec((tk, tn), lambda i, j, l: (l, j))
    o_spec = pl.BlockSpec((tm, tn), lambda i, j, l: (i, j))
    dims = (((0 if ta else 1,), (1 if tb else 0,)), ((), ()))
    has_extra = epilogue == "relu2_grad"
    assert has_extra == (extra is not None)

    def body(*refs):
        a_ref, b_ref = refs[:2]
        outs = refs[2 + has_extra:2 + has_extra + (2 if epilogue == "relu2" else 1)]
        l = pl.program_id(2)
        part = lax.dot_general(a_ref[...].astype(jnp.bfloat16), b_ref[...].astype(jnp.bfloat16), dims,
                               preferred_element_type=jnp.float32)

        def finish(acc):
            if epilogue is None:
                outs[0][...] = acc.astype(out_dtype)
            elif epilogue == "relu2":
                outs[0][...] = acc.astype(jnp.bfloat16)
                outs[1][...] = jnp.square(jnp.maximum(acc, 0.0)).astype(jnp.bfloat16)
            else:
                outs[0][...] = (acc * (2.0 * jnp.maximum(refs[2][...].astype(jnp.float32), 0.0))).astype(out_dtype)

        if nk == 1:
            finish(part)
            return
        acc_ref = refs[-1]

        @pl.when(l == 0)
        def _():
            acc_ref[...] = part

        @pl.when((l > 0) & (l < nk - 1))
        def _():
            acc_ref[...] += part

        @pl.when(l == nk - 1)
        def _():
            finish(acc_ref[...] + part)

    if epilogue == "relu2":
        out_shape = (jax.ShapeDtypeStruct((m, n), jnp.bfloat16),) * 2
        out_specs = (o_spec, o_spec)
    else:
        out_shape = jax.ShapeDtypeStruct((m, n), out_dtype)
        out_specs = o_spec
    return pl.pallas_call(
        body, name=name, grid=(m // tm, n // tn, nk), out_shape=out_shape,
        in_specs=[a_spec, b_spec] + ([o_spec] if has_extra else []), out_specs=out_specs,
        scratch_shapes=[pltpu.VMEM((tm, tn), jnp.float32)] if nk > 1 else [],
        compiler_params=pltpu.CompilerParams(dimension_semantics=("parallel", "parallel", "arbitrary"),
                                             vmem_limit_bytes=VMEM_LIMIT),
    )(*((a, b) + ((extra,) if has_extra else ())))


def _lo(x):
    return lax.stop_gradient(x.astype(jnp.bfloat16))


@jax.custom_vjp
def _linear(x, x_lo, w, slot):
    del x, slot
    return _matmul(x_lo, w, name="linear_fwd")


def _linear_fwd(x, x_lo, w, slot):
    del x, slot
    return _matmul(x_lo, w, name="linear_fwd"), (x_lo, w)


def _linear_bwd(res, dy):
    x_lo, w = res
    dy = dy.astype(jnp.bfloat16)
    dx = _matmul(dy, w, tb=True, name="linear_dx")
    dw = _matmul(x_lo, dy, ta=True, name="linear_dw")
    return dx, jnp.zeros_like(x_lo), jnp.zeros_like(w), dw


_linear.defvjp(_linear_fwd, _linear_bwd)


@jax.custom_vjp
def _mlp(h, h_lo, w_up, w_down, slot_up, slot_down):
    return _mlp_fwd(h, h_lo, w_up, w_down, slot_up, slot_down)[0]


def _mlp_fwd(h, h_lo, w_up, w_down, slot_up, slot_down):
    del h, slot_up, slot_down
    up, act = _matmul(h_lo, w_up, name="mlp_up", epilogue="relu2")
    return _matmul(act, w_down, name="mlp_down"), (h_lo, up, act, w_up, w_down)


def _mlp_bwd(res, dy):
    h_lo, up, act, w_up, w_down = res
    dy = dy.astype(jnp.bfloat16)
    d_up = _matmul(dy, w_down, tb=True, name="mlp_d_up", epilogue="relu2_grad", extra=up, out_dtype=jnp.bfloat16)
    dw_down = _matmul(act, dy, ta=True, name="mlp_dw_down")
    dw_up = _matmul(h_lo, d_up, ta=True, name="mlp_dw_up")
    dh = _matmul(d_up, w_up, tb=True, name="mlp_dh")
    return dh, jnp.zeros_like(h_lo), jnp.zeros_like(w_up), jnp.zeros_like(w_down), dw_up, dw_down


_mlp.defvjp(_mlp_fwd, _mlp_bwd)


LN_ROWS = 256


def _ln_call(h, mix, g, b):
    s, d = h.shape
    tok = pl.BlockSpec((LN_ROWS, d), lambda i: (i, 0))
    vec = pl.BlockSpec((1, d), lambda i: (0, 0))
    stat = pl.BlockSpec((LN_ROWS, 1), lambda i: (i, 0))

    def body(h_ref, mix_ref, g_ref, b_ref, y_ref, ylo_ref, xhat_ref, rstd_ref):
        z = DN_ALPHA * h_ref[...] + mix_ref[...]
        mu = jnp.mean(z, axis=-1, keepdims=True)
        zc = z - mu
        rstd = lax.rsqrt(jnp.mean(jnp.square(zc), axis=-1, keepdims=True) + LN_EPS)
        xhat = zc * rstd
        y = xhat * g_ref[...] + b_ref[...]
        y_ref[...] = y
        ylo_ref[...] = y.astype(ylo_ref.dtype)
        xhat_ref[...] = xhat
        rstd_ref[...] = rstd

    sd = jax.ShapeDtypeStruct
    return pl.pallas_call(
        body, name="ln_fwd", grid=(s // LN_ROWS,),
        out_shape=(sd((s, d), jnp.float32), sd((s, d), jnp.bfloat16), sd((s, d), jnp.float32), sd((s, 1), jnp.float32)),
        in_specs=[tok, tok, vec, vec], out_specs=(tok, tok, tok, stat),
        compiler_params=pltpu.CompilerParams(dimension_semantics=("parallel",)),
    )(h, mix, g, b)


def _ln_grad_call(dy, xhat, rstd, g):
    s, d = dy.shape
    tok = pl.BlockSpec((LN_ROWS, d), lambda i: (i, 0))
    vec = pl.BlockSpec((1, d), lambda i: (0, 0))
    stat = pl.BlockSpec((LN_ROWS, 1), lambda i: (i, 0))

    def body(dy_ref, xhat_ref, rstd_ref, g_ref, dz_ref, dg_ref, db_ref):
        @pl.when(pl.program_id(0) == 0)
        def _():
            dg_ref[...] = jnp.zeros_like(dg_ref)
            db_ref[...] = jnp.zeros_like(db_ref)

        dy, xhat = dy_ref[...], xhat_ref[...]
        dyg = dy * g_ref[...]
        m1 = jnp.mean(dyg, axis=-1, keepdims=True)
        m2 = jnp.mean(dyg * xhat, axis=-1, keepdims=True)
        dz_ref[...] = rstd_ref[...] * (dyg - m1 - xhat * m2)
        dg_ref[...] += jnp.sum(dy * xhat, axis=0, keepdims=True)
        db_ref[...] += jnp.sum(dy, axis=0, keepdims=True)

    sd = jax.ShapeDtypeStruct
    return pl.pallas_call(
        body, name="ln_bwd", grid=(s // LN_ROWS,),
        out_shape=(sd((s, d), jnp.float32), sd((1, d), jnp.float32), sd((1, d), jnp.float32)),
        in_specs=[tok, tok, stat, vec], out_specs=(tok, vec, vec),
        compiler_params=pltpu.CompilerParams(dimension_semantics=("arbitrary",)),
    )(dy, xhat, rstd, g)


@jax.custom_vjp
def _ln_res(h, mix, g, b):
    return _ln_call(h, mix, g, b)[:2]


def _ln_res_fwd(h, mix, g, b):
    y, y_lo, xhat, rstd = _ln_call(h, mix, g, b)
    return (y, y_lo), (xhat, rstd, g)


def _ln_res_bwd(res, cts):
    xhat, rstd, g = res
    dz, dg, db = _ln_grad_call(cts[0], xhat, rstd, g)
    return DN_ALPHA * dz, dz, dg, db


_ln_res.defvjp(_ln_res_fwd, _ln_res_bwd)


MXU_DTYPE = jnp.bfloat16
DN_CB = 8
DN_GROUP = 8
DN_SCALE = DN_DK ** -0.5


def _dot(a, b, ca=1, cb=0):
    return lax.dot_general(a.astype(MXU_DTYPE), b.astype(MXU_DTYPE), (((ca,), (cb,)), ((), ())),
                           preferred_element_type=jnp.float32)


def _chunk_masks():
    row = lax.broadcasted_iota(jnp.int32, (CHUNK, CHUNK), 0)
    col = lax.broadcasted_iota(jnp.int32, (CHUNK, CHUNK), 1)
    return row >= col, row > col, row == col


def _to_col(row_vec):
    _, _, eye = _chunk_masks()
    return jnp.sum(jnp.where(eye, jnp.broadcast_to(row_vec, (CHUNK, CHUNK)), 0.0), axis=1, keepdims=True)


def _to_row(col_vec):
    _, _, eye = _chunk_masks()
    return jnp.sum(jnp.where(eye, jnp.broadcast_to(col_vec, (CHUNK, CHUNK)), 0.0), axis=0, keepdims=True)


def _last_row(col_vec):
    last = lax.broadcasted_iota(jnp.int32, (CHUNK, 1), 0) == CHUNK - 1
    return jnp.sum(jnp.where(last, col_vec, 0.0), axis=0, keepdims=True), last


def _chunk_terms(q, k, beta, gcc, gcr):
    incl, strict, _ = _chunk_masks()
    decay = jnp.where(incl, jnp.exp(jnp.minimum(gcc - gcr, 0.0)), 0.0)
    kb = k * beta
    lmat = jnp.where(strict, _dot(kb, k, 1, 1) * decay, 0.0)
    intra = jnp.where(incl, _dot(q, k, 1, 1) * decay, 0.0)
    return decay, kb, lmat, intra


def _dot3(a, b, ca=1, cb=0):
    if MXU_DTYPE == jnp.float32:
        return _dot(a, b, ca, cb)
    a_hi, b_hi = a.astype(MXU_DTYPE), b.astype(MXU_DTYPE)
    a_lo = (a - a_hi.astype(jnp.float32)).astype(MXU_DTYPE)
    b_lo = (b - b_hi.astype(jnp.float32)).astype(MXU_DTYPE)
    return _dot(a_hi, b_hi, ca, cb) + (_dot(a_hi, b_lo, ca, cb) + _dot(a_lo, b_hi, ca, cb))


def _unit_lower_inverse(lmats):
    _, _, eye = _chunk_masks()
    ident = jnp.where(eye, 1.0, 0.0)
    ts = [ident - m for m in lmats]
    ps = [_dot(m, m) for m in lmats]
    for _ in range(4):
        ts = [t + _dot(t, p) for t, p in zip(ts, ps)]
        ps = [_dot(p, p) for p in ps]
    ts = [t + _dot(t, p) for t, p in zip(ts, ps)]
    resids = [(t - ident) + _dot3(m, t) for m, t in zip(lmats, ts)]
    return [t - _dot(t, r) for t, r in zip(ts, resids)]


def _dn_specs(n_chunks):
    tok = pl.BlockSpec((DN_CB * CHUNK, DN_DK), lambda h, n: (n, h))
    rowv = pl.BlockSpec((None, DN_CB, CHUNK), lambda h, n: (h, n, 0))
    sq = pl.BlockSpec((None, DN_CB, CHUNK, CHUNK), lambda h, n: (h, n, 0, 0))
    lane = pl.BlockSpec((None, DN_CB, 1, DN_DV), lambda h, n: (h, n, 0, 0))
    planes = [pl.BlockSpec((None, DN_CB * CHUNK, DN_DK), functools.partial(lambda h, n, p: (p, n, h), p=p))
              for p in range(3)]
    return tok, rowv, sq, lane, planes


def _dn_prep(qkv, beta, gc):
    s = qkv.shape[1]
    n_chunks = s // CHUNK
    tok, rowv, sq, lane, planes = _dn_specs(n_chunks)
    tok_shape = qkv.shape[1:]

    def body(q_ref, k_ref, v_ref, beta_ref, gc_ref, u_ref, w_ref, qd_ref, kd_ref, intra_ref, t_ref, cd_ref):
        for c0 in range(0, DN_CB, DN_GROUP):
            chunks = range(c0, c0 + DN_GROUP)
            rhs, lmats = [], []
            for c in chunks:
                rows = pl.ds(c * CHUNK, CHUNK)
                q_c, k_c, v_c = q_ref[rows, :] * DN_SCALE, k_ref[rows, :], v_ref[rows, :]
                gcr_c = gc_ref[pl.ds(c, 1), :]
                beta_c, gcc_c = _to_col(beta_ref[pl.ds(c, 1), :]), _to_col(gcr_c)
                _, kb, lmat, intra = _chunk_terms(q_c, k_c, beta_c, gcc_c, gcr_c)
                eg = jnp.exp(gcc_c)
                g_last, _ = _last_row(gcc_c)
                qd_ref[rows, :] = (q_c * eg).astype(qd_ref.dtype)
                kd_ref[rows, :] = (k_c * jnp.exp(g_last - gcc_c)).astype(kd_ref.dtype)
                intra_ref[c] = intra.astype(intra_ref.dtype)
                cd_ref[c] = jnp.broadcast_to(jnp.exp(g_last), (1, DN_DV))
                rhs.append(jnp.concatenate([v_c * beta_c, kb * eg], axis=1))
                lmats.append(lmat)
            ts = _unit_lower_inverse(lmats)
            sols = [_dot3(t, r) for t, r in zip(ts, rhs)]
            for c, t, sol in zip(chunks, ts, sols):
                rows = pl.ds(c * CHUNK, CHUNK)
                t_ref[c] = t
                u_ref[rows, :] = sol[:, :DN_DV]
                w_ref[rows, :] = sol[:, DN_DV:].astype(w_ref.dtype)

    f32, mx = jnp.float32, MXU_DTYPE
    sd = jax.ShapeDtypeStruct
    return pl.pallas_call(
        body, name="dn_prep", grid=(DN_HEADS, n_chunks // DN_CB),
        out_shape=(sd(tok_shape, f32), sd(tok_shape, mx), sd(tok_shape, mx), sd(tok_shape, mx),
                   sd((DN_HEADS, n_chunks, CHUNK, CHUNK), mx), sd((DN_HEADS, n_chunks, CHUNK, CHUNK), f32),
                   sd((DN_HEADS, n_chunks, 1, DN_DV), f32)),
        in_specs=planes + [rowv, rowv], out_specs=(tok, tok, tok, tok, sq, sq, lane),
        compiler_params=pltpu.CompilerParams(dimension_semantics=("parallel", "parallel")),
    )(qkv, qkv, qkv, beta, gc)


def _dn_scan(u, w, qd, kd, intra, cd):
    s, width = u.shape
    n_chunks = s // CHUNK
    tok = pl.BlockSpec((CHUNK, width), lambda n: (n, 0))
    sq = pl.BlockSpec((DN_HEADS, None, CHUNK, CHUNK), lambda n: (0, n, 0, 0))
    lane = pl.BlockSpec((DN_HEADS, None, 1, DN_DV), lambda n: (0, n, 0, 0))
    st = pl.BlockSpec((DN_HEADS, None, DN_DK, DN_DV), lambda n: (0, n, 0, 0))

    def body(u_ref, w_ref, qd_ref, kd_ref, intra_ref, cd_ref, o_ref, vn_ref, st_ref, state):
        @pl.when(pl.program_id(0) == 0)
        def _():
            state[...] = jnp.zeros_like(state)

        heads = range(DN_HEADS)
        cols = [pl.ds(h * DN_DK, DN_DK) for h in heads]
        s_f = [state[h] for h in heads]
        s_mx = [s.astype(MXU_DTYPE) for s in s_f]
        for h in heads:
            st_ref[h] = s_mx[h]
        ws = [_dot(w_ref[:, cols[h]], s_mx[h]) for h in heads]
        qs = [_dot(qd_ref[:, cols[h]], s_mx[h]) for h in heads]
        v_new = [(u_ref[:, cols[h]] - ws[h]).astype(MXU_DTYPE) for h in heads]
        inner = [_dot(intra_ref[h], v_new[h]) for h in heads]
        outer = [_dot(kd_ref[:, cols[h]], v_new[h], 0, 0) for h in heads]
        for h in heads:
            vn_ref[:, cols[h]] = v_new[h]
            o_ref[:, cols[h]] = qs[h] + inner[h]
            state[h] = s_f[h] * cd_ref[h] + outer[h]

    sd = jax.ShapeDtypeStruct
    return pl.pallas_call(
        body, name="dn_scan", grid=(n_chunks,),
        out_shape=(sd(u.shape, jnp.float32), sd(u.shape, MXU_DTYPE),
                   sd((DN_HEADS, n_chunks, DN_DK, DN_DV), MXU_DTYPE)),
        in_specs=[tok, tok, tok, tok, sq, lane], out_specs=(tok, tok, st),
        scratch_shapes=[pltpu.VMEM((DN_HEADS, DN_DK, DN_DV), jnp.float32)],
        compiler_params=pltpu.CompilerParams(dimension_semantics=("arbitrary",)),
    )(u, w, qd, kd, intra, cd)


def _dn_bwd_scan(do, w, qd, kd, intra, cd, vn, st):
    s, width = do.shape
    n_chunks = s // CHUNK
    last = n_chunks - 1
    tok = pl.BlockSpec((CHUNK, width), lambda n: (last - n, 0))
    sq = pl.BlockSpec((DN_HEADS, None, CHUNK, CHUNK), lambda n: (0, last - n, 0, 0))
    lane = pl.BlockSpec((DN_HEADS, None, 1, DN_DV), lambda n: (0, last - n, 0, 0))
    stt = pl.BlockSpec((DN_HEADS, None, DN_DK, DN_DV), lambda n: (0, last - n, 0, 0))

    def body(do_ref, w_ref, qd_ref, kd_ref, intra_ref, cd_ref, vn_ref, st_ref,
             du_ref, dw_ref, dqd_ref, dkd_ref, dintra_ref, dgl_ref, dstate):
        @pl.when(pl.program_id(0) == 0)
        def _():
            dstate[...] = jnp.zeros_like(dstate)

        heads = range(DN_HEADS)
        cols = [pl.ds(h * DN_DK, DN_DK) for h in heads]
        ds_f = [dstate[h] for h in heads]
        ds_mx = [d.astype(MXU_DTYPE) for d in ds_f]
        do_h = [do_ref[:, cols[h]].astype(MXU_DTYPE) for h in heads]
        dv_a = [_dot(intra_ref[h], do_h[h], 0, 0) for h in heads]
        dv_b = [_dot(kd_ref[:, cols[h]], ds_mx[h]) for h in heads]
        d_intra = [_dot(do_h[h], vn_ref[:, cols[h]], 1, 1) for h in heads]
        d_qd = [_dot(do_h[h], st_ref[h], 1, 1) for h in heads]
        d_kd = [_dot(vn_ref[:, cols[h]], ds_mx[h], 1, 1) for h in heads]
        ds_q = [_dot(qd_ref[:, cols[h]], do_h[h], 0, 0) for h in heads]
        dv_new = [dv_a[h] + dv_b[h] for h in heads]
        dv_mx = [d.astype(MXU_DTYPE) for d in dv_new]
        d_w = [_dot(dv_mx[h], st_ref[h], 1, 1) for h in heads]
        ds_w = [_dot(w_ref[:, cols[h]], dv_mx[h], 0, 0) for h in heads]
        for h in heads:
            du_ref[:, cols[h]] = dv_new[h]
            dintra_ref[h] = d_intra[h]
            dqd_ref[:, cols[h]] = d_qd[h]
            dkd_ref[:, cols[h]] = d_kd[h]
            dw_ref[:, cols[h]] = -d_w[h]
            cd_h = cd_ref[h]
            dcd = jnp.sum(jnp.sum(st_ref[h].astype(jnp.float32) * ds_f[h], axis=1, keepdims=True), axis=0,
                          keepdims=True)
            dgl_ref[h] = dcd * cd_h
            dstate[h] = ds_q[h] + ds_f[h] * cd_h - ds_w[h]

    sd = jax.ShapeDtypeStruct
    f32 = jnp.float32
    return pl.pallas_call(
        body, name="dn_bwd_scan", grid=(n_chunks,),
        out_shape=(sd(do.shape, f32), sd(do.shape, f32), sd(do.shape, f32), sd(do.shape, f32),
                   sd((DN_HEADS, n_chunks, CHUNK, CHUNK), f32), sd((DN_HEADS, n_chunks, 1, DN_DV), f32)),
        in_specs=[tok, tok, tok, tok, sq, lane, tok, stt], out_specs=(tok, tok, tok, tok, sq, lane),
        scratch_shapes=[pltpu.VMEM((DN_HEADS, DN_DK, DN_DV), f32)],
        compiler_params=pltpu.CompilerParams(dimension_semantics=("arbitrary",)),
    )(do, w, qd, kd, intra, cd, vn, st)


def _dn_bwd_chunks(qkv, beta, gc, t, u, w, du, dw, dqd, dkd, dintra, dgl):
    s = qkv.shape[1]
    n_chunks = s // CHUNK
    tok, rowv, sq, lane, planes = _dn_specs(n_chunks)
    all_planes = pl.BlockSpec((3, DN_CB * CHUNK, DN_DK), lambda h, n: (0, n, h))

    def body(q_ref, k_ref, v_ref, beta_ref, gc_ref, t_ref, u_ref, w_ref, du_ref, dw_ref, dqd_ref, dkd_ref,
             dintra_ref, dgl_ref, dqkv_ref, dbeta_ref, dgc_ref):
        incl, strict, _ = _chunk_masks()

        def first(c):
            rows = pl.ds(c * CHUNK, CHUNK)
            q_c, k_c = q_ref[rows, :] * DN_SCALE, k_ref[rows, :]
            gcr_c = gc_ref[pl.ds(c, 1), :]
            beta_c, gcc_c = _to_col(beta_ref[pl.ds(c, 1), :]), _to_col(gcr_c)
            decay, kb, lmat, intra = _chunk_terms(q_c, k_c, beta_c, gcc_c, gcr_c)
            d_sol = jnp.concatenate([du_ref[rows, :], dw_ref[rows, :]], axis=1)
            d_rhs = _dot3(t_ref[c], d_sol, 0, 0)
            return dict(rows=rows, q=q_c, k=k_c, beta=beta_c, gcc=gcc_c, decay=decay, kb=kb, lmat=lmat, intra=intra,
                        d_rhs=d_rhs)

        def second(c, e):
            sol = jnp.concatenate([u_ref[e["rows"], :], w_ref[e["rows"], :].astype(jnp.float32)], axis=1)
            e["d_l"] = jnp.where(strict, -_dot(e["d_rhs"], sol, 1, 1), 0.0)
            e["d_intra"] = jnp.where(incl, dintra_ref[c], 0.0)
            d_qk = e["d_intra"] * e["decay"]
            e["dq"] = _dot(d_qk, e["k"])
            e["dk"] = _dot(d_qk, e["q"], 0, 0)

        def third(e):
            d_a = e["d_l"] * e["decay"]
            e["dkb"] = _dot(d_a, e["k"])
            e["dk"] = e["dk"] + _dot(d_a, e["kb"], 0, 0)

        def last(c, e):
            rows, q_c, k_c, beta_c, gcc_c = e["rows"], e["q"], e["k"], e["beta"], e["gcc"]
            v_c = v_ref[rows, :]
            eg = jnp.exp(gcc_c)
            g_last, is_last = _last_row(gcc_c)
            e_rev = jnp.exp(g_last - gcc_c)
            d_rhs_u, d_rhs_w = e["d_rhs"][:, :DN_DV], e["d_rhs"][:, DN_DV:]
            dqkv_ref[2, rows, :] = d_rhs_u * beta_c
            dbeta = jnp.sum(d_rhs_u * v_c, axis=1, keepdims=True)
            dkb = e["dkb"] + d_rhs_w * eg
            dgc = jnp.sum(d_rhs_w * e["kb"] * eg, axis=1, keepdims=True)
            m1 = e["d_l"] * e["lmat"]
            dgc = dgc + jnp.sum(m1, axis=1, keepdims=True)
            dgr = -jnp.sum(m1, axis=0, keepdims=True)
            m2 = e["d_intra"] * e["intra"]
            dgc = dgc + jnp.sum(m2, axis=1, keepdims=True)
            dgr = dgr - jnp.sum(m2, axis=0, keepdims=True)
            dqd = dqd_ref[rows, :]
            dq = e["dq"] + dqd * eg
            dgc = dgc + jnp.sum(dqd * q_c * eg, axis=1, keepdims=True)
            dkd = dkd_ref[rows, :]
            dk = e["dk"] + dkd * e_rev
            tk = jnp.sum(dkd * k_c * e_rev, axis=1, keepdims=True)
            dgc = dgc - tk
            d_last = dgl_ref[c][:, :1] + jnp.sum(tk, axis=0, keepdims=True)
            dgc = dgc + jnp.where(is_last, d_last, 0.0)
            dk = dk + dkb * beta_c
            dbeta = dbeta + jnp.sum(dkb * k_c, axis=1, keepdims=True)
            dqkv_ref[0, rows, :] = dq * DN_SCALE
            dqkv_ref[1, rows, :] = dk
            dbeta_ref[pl.ds(c, 1), :] = _to_row(dbeta)
            dgc_ref[pl.ds(c, 1), :] = _to_row(dgc) + dgr

        for c0 in range(0, DN_CB, DN_GROUP):
            chunks = range(c0, c0 + DN_GROUP)
            env = [first(c) for c in chunks]
            for c, e in zip(chunks, env):
                second(c, e)
            for e in env:
                third(e)
            for c, e in zip(chunks, env):
                last(c, e)

    sd = jax.ShapeDtypeStruct
    f32 = jnp.float32
    return pl.pallas_call(
        body, name="dn_bwd_chunks", grid=(DN_HEADS, n_chunks // DN_CB),
        out_shape=(sd(qkv.shape, f32), sd(beta.shape, f32), sd(gc.shape, f32)),
        in_specs=planes + [rowv, rowv, sq, tok, tok, tok, tok, tok, tok, sq, lane],
        out_specs=(all_planes, rowv, rowv),
        compiler_params=pltpu.CompilerParams(dimension_semantics=("parallel", "parallel")),
    )(qkv, qkv, qkv, beta, gc, t, u, w, du, dw, dqd, dkd, dintra, dgl)


@jax.custom_vjp
def _delta_rule_op(qkv, beta, gc):
    return _delta_rule_fwd(qkv, beta, gc)[0]


def _delta_rule_fwd(qkv, beta, gc):
    u, w, qd, kd, intra, t, cd = _dn_prep(qkv, beta, gc)
    out, vn, st = _dn_scan(u, w, qd, kd, intra, cd)
    return out, (qkv, beta, gc, u, w, qd, kd, intra, t, cd, vn, st)


def _delta_rule_bwd(res, do):
    qkv, beta, gc, u, w, qd, kd, intra, t, cd, vn, st = res
    du, dw, dqd, dkd, dintra, dgl = _dn_bwd_scan(do, w, qd, kd, intra, cd, vn, st)
    return _dn_bwd_chunks(qkv, beta, gc, t, u, w, du, dw, dqd, dkd, dintra, dgl)


_delta_rule_op.defvjp(_delta_rule_fwd, _delta_rule_bwd)


def _gated_delta_rule(qkv, g, beta):
    s, h = g.shape
    n_chunks = s // CHUNK
    gc = jnp.cumsum(g.T.reshape(h, n_chunks, CHUNK), axis=-1)
    return _delta_rule_op(qkv, beta.T.reshape(h, n_chunks, CHUNK), gc)


PRE_ROWS = 256
HALO = 8
PRE_W = DN_QK_W


def _shift_rows(xs, k):
    return pltpu.roll(xs, k, 0)[HALO:]


def _conv_silu(x_ref, halo_ref, w_ref, first_block):
    halo = jnp.where(first_block, 0.0, halo_ref[...])
    xs = jnp.concatenate([halo, x_ref[...]], axis=0)
    taps = [_shift_rows(xs, CONV_WIDTH - 1 - j) for j in range(CONV_WIDTH - 1)] + [x_ref[...]]
    conv = sum(w_ref[pl.ds(j, 1), :] * taps[j] for j in range(CONV_WIDTH))
    return conv, jax.nn.sigmoid(conv), taps


def _pre_specs():
    blk = pl.BlockSpec((PRE_ROWS, PRE_W), lambda j, i: (i, j))
    prev = pl.BlockSpec((HALO, PRE_W), lambda j, i: (jnp.maximum(i * (PRE_ROWS // HALO) - 1, 0), j))
    wts = pl.BlockSpec((CONV_WIDTH, PRE_W), lambda j, i: (0, j))
    plane = pl.BlockSpec((None, PRE_ROWS, PRE_W), lambda j, i: (j, i, 0))
    return blk, prev, wts, plane


def _pre_fwd_call(x, conv_w):
    s = x.shape[0]
    blk, prev, wts, plane = _pre_specs()

    def body(x_ref, halo_ref, w_ref, o_ref):
        conv, sig, _ = _conv_silu(x_ref, halo_ref, w_ref, pl.program_id(1) == 0)
        act = conv * sig
        is_v = pl.program_id(0) == 2
        for h in range(DN_HEADS):
            cols = slice(h * DN_DK, (h + 1) * DN_DK)
            a_h = act[:, cols]
            r = lax.rsqrt(jnp.sum(a_h * a_h, axis=-1, keepdims=True) + NORM_EPS)
            o_ref[:, cols] = a_h * jnp.where(is_v, 1.0, r)

    return pl.pallas_call(
        body, name="pre_fwd", grid=(3, s // PRE_ROWS),
        out_shape=jax.ShapeDtypeStruct((3, s, PRE_W), jnp.float32),
        in_specs=[blk, prev, wts], out_specs=plane,
        compiler_params=pltpu.CompilerParams(dimension_semantics=("parallel", "parallel")),
    )(x, x, conv_w)


def _pre_bwd_act_call(x, conv_w, d_out):
    s = x.shape[0]
    blk, prev, wts, plane = _pre_specs()

    def body(x_ref, halo_ref, w_ref, do_ref, dc_ref):
        conv, sig, _ = _conv_silu(x_ref, halo_ref, w_ref, pl.program_id(1) == 0)
        act = conv * sig
        d_silu = sig * (1.0 + conv * (1.0 - sig))
        is_v = pl.program_id(0) == 2
        for h in range(DN_HEADS):
            cols = slice(h * DN_DK, (h + 1) * DN_DK)
            a_h, do_h = act[:, cols], do_ref[:, cols]
            r = lax.rsqrt(jnp.sum(a_h * a_h, axis=-1, keepdims=True) + NORM_EPS)
            n_h = a_h * r
            d_norm = r * (do_h - n_h * jnp.sum(do_h * n_h, axis=-1, keepdims=True))
            dc_ref[:, cols] = jnp.where(is_v, do_h, d_norm) * d_silu[:, cols]

    return pl.pallas_call(
        body, name="pre_bwd_act", grid=(3, s // PRE_ROWS),
        out_shape=jax.ShapeDtypeStruct(x.shape, jnp.float32),
        in_specs=[blk, prev, wts, plane], out_specs=blk,
        compiler_params=pltpu.CompilerParams(dimension_semantics=("parallel", "parallel")),
    )(x, x, conv_w, d_out)


def _pre_bwd_conv_call(x, conv_w, dc):
    s = x.shape[0]
    n_blocks = s // PRE_ROWS
    blk, prev, wts, _ = _pre_specs()
    nxt = pl.BlockSpec((HALO, PRE_W), lambda j, i: (jnp.minimum((i + 1) * (PRE_ROWS // HALO), s // HALO - 1), j))

    def body(x_ref, halo_ref, w_ref, dc_ref, dcn_ref, dx_ref, dw_ref):
        i = pl.program_id(1)

        @pl.when(i == 0)
        def _():
            dw_ref[...] = jnp.zeros_like(dw_ref)

        dcv = dc_ref[...]
        ahead = jnp.concatenate([dcv, jnp.where(i == n_blocks - 1, 0.0, dcn_ref[...])], axis=0)
        dx = w_ref[pl.ds(CONV_WIDTH - 1, 1), :] * dcv
        for j in range(CONV_WIDTH - 1):
            k = CONV_WIDTH - 1 - j
            dx = dx + w_ref[pl.ds(j, 1), :] * pltpu.roll(ahead, PRE_ROWS + HALO - k, 0)[:PRE_ROWS]
        dx_ref[...] = dx
        halo = jnp.where(i == 0, 0.0, halo_ref[...])
        xs = jnp.concatenate([halo, x_ref[...]], axis=0)
        for j in range(CONV_WIDTH):
            tap = x_ref[...] if j == CONV_WIDTH - 1 else _shift_rows(xs, CONV_WIDTH - 1 - j)
            dw_ref[pl.ds(j, 1), :] += jnp.sum(dcv * tap, axis=0, keepdims=True)

    sd = jax.ShapeDtypeStruct
    return pl.pallas_call(
        body, name="pre_bwd_conv", grid=(3, n_blocks),
        out_shape=(sd(x.shape, jnp.float32), sd(conv_w.shape, jnp.float32)),
        in_specs=[blk, prev, wts, blk, nxt], out_specs=(blk, wts),
        compiler_params=pltpu.CompilerParams(dimension_semantics=("parallel", "arbitrary")),
    )(x, x, conv_w, dc, dc)


@jax.custom_vjp
def _pre_op(x, conv_w):
    return _pre_fwd_call(x, conv_w)


def _pre_op_fwd(x, conv_w):
    return _pre_fwd_call(x, conv_w), (x, conv_w)


def _pre_op_bwd(res, d_out):
    x, conv_w = res
    return _pre_bwd_conv_call(x, conv_w, _pre_bwd_act_call(x, conv_w, d_out))


_pre_op.defvjp(_pre_op_fwd, _pre_op_bwd)


def _project(h, h_lo, w, slot):
    b, s, d = h.shape
    return _linear(h.reshape(b * s, d), h_lo.reshape(b * s, d), w, slot).reshape(b, s, w.shape[1])


def _rope_table(positions, dh):
    inv_freq = ROPE_THETA ** (-jnp.arange(0, dh, 2, dtype=jnp.float32) / dh)
    ang = positions.astype(jnp.float32)[:, None] * inv_freq
    reps = 128 // (dh // 2)
    return jnp.concatenate([jnp.tile(jnp.cos(ang), (1, reps)), jnp.tile(jnp.sin(ang), (1, reps))], axis=-1)


GATE_ROWS = 512


def _gate_terms(o_h, z_h):
    r = lax.rsqrt(jnp.mean(o_h * o_h, axis=-1, keepdims=True) + NORM_EPS)
    sig = jax.nn.sigmoid(z_h)
    return r, o_h * r, sig, z_h * sig


def _gate_fwd_call(o, z, nw):
    tok = pl.BlockSpec((GATE_ROWS, DN_V_W), lambda i: (i, 0))
    vec = pl.BlockSpec((1, DN_DV), lambda i: (0, 0))

    def body(o_ref, z_ref, nw_ref, y_ref):
        for h in range(DN_HEADS):
            cols = pl.ds(h * DN_DV, DN_DV)
            _, n_h, _, g_h = _gate_terms(o_ref[:, cols], z_ref[:, cols])
            y_ref[:, cols] = n_h * nw_ref[...] * g_h

    return pl.pallas_call(
        body, name="gate_fwd", grid=(o.shape[0] // GATE_ROWS,),
        out_shape=jax.ShapeDtypeStruct(o.shape, jnp.float32), in_specs=[tok, tok, vec], out_specs=tok,
        compiler_params=pltpu.CompilerParams(dimension_semantics=("parallel",)),
    )(o, z, nw)


def _gate_bwd_call(o, z, nw, dy):
    tok = pl.BlockSpec((GATE_ROWS, DN_V_W), lambda i: (i, 0))
    vec = pl.BlockSpec((1, DN_DV), lambda i: (0, 0))

    def body(o_ref, z_ref, nw_ref, dy_ref, do_ref, dz_ref, dnw_ref):
        @pl.when(pl.program_id(0) == 0)
        def _():
            dnw_ref[...] = jnp.zeros_like(dnw_ref)

        for h in range(DN_HEADS):
            cols = pl.ds(h * DN_DV, DN_DV)
            z_h, dy_h = z_ref[:, cols], dy_ref[:, cols]
            r, n_h, sig, g_h = _gate_terms(o_ref[:, cols], z_h)
            dz_ref[:, cols] = dy_h * n_h * nw_ref[...] * (sig * (1.0 + z_h * (1.0 - sig)))
            dn = dy_h * nw_ref[...] * g_h
            do_ref[:, cols] = r * (dn - n_h * jnp.mean(dn * n_h, axis=-1, keepdims=True))
            dnw_ref[...] += jnp.sum(dy_h * n_h * g_h, axis=0, keepdims=True)

    sd = jax.ShapeDtypeStruct
    return pl.pallas_call(
        body, name="gate_bwd", grid=(o.shape[0] // GATE_ROWS,),
        out_shape=(sd(o.shape, jnp.float32), sd(o.shape, jnp.float32), sd(nw.shape, jnp.float32)),
        in_specs=[tok, tok, vec, tok], out_specs=(tok, tok, vec),
        compiler_params=pltpu.CompilerParams(dimension_semantics=("arbitrary",)),
    )(o, z, nw, dy)


@jax.custom_vjp
def _gate_op(o, z, nw):
    return _gate_fwd_call(o, z, nw)


def _gate_op_fwd(o, z, nw):
    return _gate_fwd_call(o, z, nw), (o, z, nw)


def _gate_op_bwd(res, dy):
    return _gate_bwd_call(*res, dy)


_gate_op.defvjp(_gate_op_fwd, _gate_op_bwd)


_MASKED = -1e30


def _swa_probs(qs, k_h, sinks, valid):
    ss = [jnp.where(valid, _dot(q_h, k_h, 1, 1) * (SWA_DH ** -0.5), _MASKED) for q_h in qs]
    ms = [jnp.maximum(jnp.max(s, axis=-1, keepdims=True), sink) for s, sink in zip(ss, sinks)]
    ps = [jnp.exp(s - m) for s, m in zip(ss, ms)]
    es = [jnp.exp(sink - m) for sink, m in zip(sinks, ms)]
    invs = [1.0 / (jnp.sum(p, axis=-1, keepdims=True) + e) for p, e in zip(ps, es)]
    return [p * inv for p, inv in zip(ps, invs)], [e * inv for e, inv in zip(es, invs)]


def _swa_valid(n):
    qi = lax.broadcasted_iota(jnp.int32, (WINDOW, 2 * WINDOW), 0)
    kj = lax.broadcasted_iota(jnp.int32, (WINDOW, 2 * WINDOW), 1)
    diff = qi + WINDOW - kj
    return (diff >= 0) & (diff < WINDOW) & ((kj >= WINDOW) | (n > 0))


def _rotate_half(x, transpose=False):
    half = SWA_DH // 2
    lower = lax.broadcasted_iota(jnp.int32, x.shape, 1) % SWA_DH < half
    ahead, behind = pltpu.roll(x, 128 - half, 1), pltpu.roll(x, half, 1)
    return jnp.where(lower, ahead, -behind) if transpose else jnp.where(lower, -ahead, behind)


def _rope(x, table):
    return x * table[:, :128] + _rotate_half(x) * table[:, 128:]


def _unrope(dy, table):
    return dy * table[:, :128] + _rotate_half(dy * table[:, 128:], transpose=True)


def _swa_specs():
    qs = pl.BlockSpec((WINDOW, SWA_Q_W), lambda n: (n, 0))
    first = lambda n: jnp.maximum(n - 1, 0)
    kv = [pl.BlockSpec((WINDOW, SWA_KV_W), lambda n: (first(n), 0)), pl.BlockSpec((WINDOW, SWA_KV_W), lambda n: (n, 0)),
          pl.BlockSpec((WINDOW, SWA_KV_W), lambda n: (first(n), 1)), pl.BlockSpec((WINDOW, SWA_KV_W), lambda n: (n, 1))]
    tables = [pl.BlockSpec((WINDOW, 256), lambda n: (first(n), 0)), pl.BlockSpec((WINDOW, 256), lambda n: (n, 0))]
    cur = pl.BlockSpec((WINDOW, SWA_KV_W), lambda n: (n, 0))
    sk = pl.BlockSpec((SWA_HEADS, 1, 128), lambda n: (0, 0, 0))
    return qs, kv, tables, cur, sk


def _swa_load(q_ref, kp_ref, kc_ref, vp_ref, vc_ref, tp_ref, tc_ref):
    table_kk = jnp.concatenate([tp_ref[...], tc_ref[...]], axis=0)
    kk = _rope(jnp.concatenate([kp_ref[...], kc_ref[...]], axis=0), table_kk)
    vv = jnp.concatenate([vp_ref[...], vc_ref[...]], axis=0)
    q_rot = []
    for b in range(SWA_Q_W // 128):
        pair = _rope(q_ref[:, pl.ds(b * 128, 128)], tc_ref[...])
        q_rot += [pair[:, :SWA_DH], pair[:, SWA_DH:]]
    split = lambda t: [t[:, hkv * SWA_DH:(hkv + 1) * SWA_DH] for hkv in range(SWA_KV_HEADS)]
    return q_rot, split(kk), split(vv), table_kk


def _swa_fwd_call(q, kv, table, sinks):
    qs, kvs, tables, _, sk = _swa_specs()

    def body(q_ref, kp_ref, kc_ref, vp_ref, vc_ref, tp_ref, tc_ref, sink_ref, o_ref):
        valid = _swa_valid(pl.program_id(0))
        q_rot, kk, vv, _ = _swa_load(q_ref, kp_ref, kc_ref, vp_ref, vc_ref, tp_ref, tc_ref)
        for hkv in range(SWA_KV_HEADS):
            heads = range(hkv * SWA_GROUP, (hkv + 1) * SWA_GROUP)
            probs, _ = _swa_probs([q_rot[h] for h in heads], kk[hkv], [sink_ref[h][:, :1] for h in heads], valid)
            outs = [_dot(p, vv[hkv]) for p in probs]
            for h, o in zip(heads, outs):
                o_ref[:, pl.ds(h * SWA_DH, SWA_DH)] = o

    return pl.pallas_call(
        body, name="swa_fwd", grid=(q.shape[0] // WINDOW,),
        out_shape=jax.ShapeDtypeStruct(q.shape, jnp.float32),
        in_specs=[qs] + kvs + tables + [sk], out_specs=qs,
        compiler_params=pltpu.CompilerParams(dimension_semantics=("parallel",)),
    )(q, kv, kv, kv, kv, table, table, sinks)


def _swa_bwd_call(q, kv, table, sinks, do):
    qs, kvs, tables, cur, sk = _swa_specs()

    def body(q_ref, kp_ref, kc_ref, vp_ref, vc_ref, tp_ref, tc_ref, sink_ref, do_ref,
             dq_ref, dkc_ref, dkp_ref, dvc_ref, dvp_ref, ds_ref):
        @pl.when(pl.program_id(0) == 0)
        def _():
            ds_ref[...] = jnp.zeros_like(ds_ref)

        valid = _swa_valid(pl.program_id(0))
        q_rot, kk, vv, table_kk = _swa_load(q_ref, kp_ref, kc_ref, vp_ref, vc_ref, tp_ref, tc_ref)
        lane0 = lax.broadcasted_iota(jnp.int32, (1, 128), 1) == 0
        dq_heads, dk_heads, dv_heads = [], [], []
        for hkv in range(SWA_KV_HEADS):
            k_h, v_h = kk[hkv], vv[hkv]
            heads = range(hkv * SWA_GROUP, (hkv + 1) * SWA_GROUP)
            q_hs = [q_rot[h] for h in heads]
            dos = [do_ref[:, pl.ds(h * SWA_DH, SWA_DH)] for h in heads]
            probs, p_sinks = _swa_probs(q_hs, k_h, [sink_ref[h][:, :1] for h in heads], valid)
            dps = [_dot(do_h, v_h, 1, 1) for do_h in dos]
            rss = [jnp.sum(p * dp, axis=-1, keepdims=True) for p, dp in zip(probs, dps)]
            d_ss = [p * (dp - rs) for p, dp, rs in zip(probs, dps, rss)]
            dq_heads += [_dot(d_s, k_h) * (SWA_DH ** -0.5) for d_s in d_ss]
            dks = [_dot(d_s, q_h, 0, 0) for d_s, q_h in zip(d_ss, q_hs)]
            dvs = [_dot(p, do_h, 0, 0) for p, do_h in zip(probs, dos)]
            for h, p_sink, rs in zip(heads, p_sinks, rss):
                d_sink = -jnp.sum(p_sink * rs, axis=0, keepdims=True)
                ds_ref[h] += jnp.where(lane0, d_sink, 0.0)
            dk_heads.append(sum(dks[1:], dks[0]) * (SWA_DH ** -0.5))
            dv_heads.append(sum(dvs[1:], dvs[0]))
        for b in range(SWA_Q_W // 128):
            pair = jnp.concatenate([dq_heads[2 * b], dq_heads[2 * b + 1]], axis=1)
            dq_ref[:, pl.ds(b * 128, 128)] = _unrope(pair, tc_ref[...])
        dk = _unrope(jnp.concatenate(dk_heads, axis=1), table_kk)
        dv = jnp.concatenate(dv_heads, axis=1)
        dkp_ref[...] = dk[:WINDOW]
        dkc_ref[...] = dk[WINDOW:]
        dvp_ref[...] = dv[:WINDOW]
        dvc_ref[...] = dv[WINDOW:]

    sd = jax.ShapeDtypeStruct
    f32 = jnp.float32
    half = (q.shape[0], SWA_KV_W)
    return pl.pallas_call(
        body, name="swa_bwd", grid=(q.shape[0] // WINDOW,),
        out_shape=(sd(q.shape, f32), sd(half, f32), sd(half, f32), sd(half, f32), sd(half, f32), sd(sinks.shape, f32)),
        in_specs=[qs] + kvs + tables + [sk, qs], out_specs=(qs, cur, cur, cur, cur, sk),
        compiler_params=pltpu.CompilerParams(dimension_semantics=("arbitrary",)),
    )(q, kv, kv, kv, kv, table, table, sinks, do)


@jax.custom_vjp
def _swa_op(q, kv, table, sinks):
    return _swa_fwd_call(q, kv, table, sinks)


def _swa_op_fwd(q, kv, table, sinks):
    return _swa_fwd_call(q, kv, table, sinks), (q, kv, table, sinks)


def _swa_op_bwd(res, do):
    q, kv, table, sinks = res
    dq, dkc, dkp, dvc, dvp, dsinks = _swa_bwd_call(q, kv, table, sinks, do)

    def fold(cur, prev):
        return cur + jnp.concatenate([prev[WINDOW:], jnp.zeros_like(prev[:WINDOW])], axis=0)

    return dq, jnp.concatenate([fold(dkc, dkp), fold(dvc, dvp)], axis=1), jnp.zeros_like(table), dsinks


_swa_op.defvjp(_swa_op_fwd, _swa_op_bwd)


def _swa_sink_attention(q, kv, table, sinks):
    return _swa_op(q, kv, table, jnp.broadcast_to(sinks[:, None, None], (SWA_HEADS, 1, 128)))


MEM_ROWS = 512


def _mem_probs(q_h, k_h):
    s = _dot(q_h, k_h, 1, 1) * (MEM_DH ** -0.5)
    p = jnp.exp(s - jnp.max(s, axis=-1, keepdims=True))
    return p / jnp.sum(p, axis=-1, keepdims=True)


def _mem_fwd_call(qm, kv):
    qs = pl.BlockSpec((MEM_ROWS, MEM_W), lambda i: (i, 0))
    kvs = pl.BlockSpec(kv.shape, lambda i: (0, 0))

    def body(q_ref, kv_ref, o_ref):
        for h in range(MEM_HEADS):
            cols = pl.ds(h * MEM_DH, MEM_DH)
            probs = _mem_probs(q_ref[:, cols], kv_ref[:, cols])
            o_ref[:, cols] = _dot(probs, kv_ref[:, pl.ds(MEM_W + h * MEM_DH, MEM_DH)])

    return pl.pallas_call(
        body, name="mem_fwd", grid=(qm.shape[0] // MEM_ROWS,),
        out_shape=jax.ShapeDtypeStruct(qm.shape, jnp.float32), in_specs=[qs, kvs], out_specs=qs,
        compiler_params=pltpu.CompilerParams(dimension_semantics=("parallel",)),
    )(qm, kv)


def _mem_bwd_call(qm, kv, do):
    qs = pl.BlockSpec((MEM_ROWS, MEM_W), lambda i: (i, 0))
    kvs = pl.BlockSpec(kv.shape, lambda i: (0, 0))

    def body(q_ref, kv_ref, do_ref, dq_ref, dkv_ref):
        @pl.when(pl.program_id(0) == 0)
        def _():
            dkv_ref[...] = jnp.zeros_like(dkv_ref)

        for h in range(MEM_HEADS):
            cols = pl.ds(h * MEM_DH, MEM_DH)
            v_cols = pl.ds(MEM_W + h * MEM_DH, MEM_DH)
            q_h, k_h, do_h = q_ref[:, cols], kv_ref[:, cols], do_ref[:, cols]
            probs = _mem_probs(q_h, k_h)
            dp = _dot(do_h, kv_ref[:, v_cols], 1, 1)
            d_s = probs * (dp - jnp.sum(probs * dp, axis=-1, keepdims=True))
            dq_ref[:, cols] = _dot(d_s, k_h) * (MEM_DH ** -0.5)
            dkv_ref[:, cols] += _dot(d_s, q_h, 0, 0) * (MEM_DH ** -0.5)
            dkv_ref[:, v_cols] += _dot(probs, do_h, 0, 0)

    sd = jax.ShapeDtypeStruct
    return pl.pallas_call(
        body, name="mem_bwd", grid=(qm.shape[0] // MEM_ROWS,),
        out_shape=(sd(qm.shape, jnp.float32), sd(kv.shape, jnp.float32)),
        in_specs=[qs, kvs, qs], out_specs=(qs, kvs),
        compiler_params=pltpu.CompilerParams(dimension_semantics=("arbitrary",)),
    )(qm, kv, do)


@jax.custom_vjp
def _mem_op(qm, kv):
    return _mem_fwd_call(qm, kv)


def _mem_op_fwd(qm, kv):
    return _mem_fwd_call(qm, kv), (qm, kv)


def _mem_op_bwd(res, do):
    return _mem_bwd_call(*res, do)


_mem_op.defvjp(_mem_op_fwd, _mem_op_bwd)


def _memory_attention(qm, kv):
    return _mem_op(qm[0], kv[0])[None]


def _mixer_a(h, h_lo, mem, mem_lo, p, s, layer):
    B, S, _ = h.shape
    proj = _project(h, h_lo, p["a_w_in"][layer], s["a_w_in"][layer])
    c1 = 2 * DN_QK_W + DN_V_W
    qkv = proj[..., :c1]
    z = proj[..., c1:QKVZ_W]
    qm = proj[..., QKVZ_W:QKVZ_W + MEM_W]
    a = proj[..., QKVZ_W + MEM_W:QKVZ_W + MEM_W + DN_HEADS]
    b = proj[..., QKVZ_W + MEM_W + DN_HEADS:QKVZ_W + MEM_W + 2 * DN_HEADS]
    planes = _pre_op(qkv[0], p["a_conv_w"][layer])
    beta = jax.nn.sigmoid(b[0])
    g = -jnp.exp(p["a_A_log"][layer]) * jax.nn.softplus(a[0] + p["a_dt_bias"][layer])
    o = _gate_op(_gated_delta_rule(planes, g, beta), z[0], p["a_norm_w"][layer][None])[None]
    kv = _project(mem, mem_lo, p["mem_w_kv"][layer], s["mem_w_kv"][layer])
    mo = _memory_attention(qm, kv)
    cat = jnp.concatenate([o, mo], axis=-1)
    return _project(cat, _lo(cat), p["w_o"][layer], s["w_o"][layer])


def _mixer_b(h, h_lo, mem, mem_lo, kv_shared, table, p, s, layer):
    j = layer - N_A
    proj = _project(h, h_lo, p["b_w_in"][j], s["b_w_in"][j])
    o = _swa_sink_attention(proj[0, :, :SWA_Q_W], kv_shared, table, p["b_sinks"][j])[None]
    kv = _project(mem, mem_lo, p["mem_w_kv"][layer], s["mem_w_kv"][layer])
    mo = _memory_attention(proj[..., SWA_Q_W:], kv)
    cat = jnp.concatenate([o, mo], axis=-1)
    return _project(cat, _lo(cat), p["w_o"][layer], s["w_o"][layer])


def _forward(p, s, x, mem, positions):
    table = _rope_table(positions[0], SWA_DH)
    h, h_lo, mem_lo = x, _lo(x), _lo(mem)
    kv_shared = None
    for layer in range(DEPTH):
        if layer < N_A:
            mix = _mixer_a(h, h_lo, mem, mem_lo, p, s, layer)
        else:
            mix = _mixer_b(h, h_lo, mem, mem_lo, kv_shared, table, p, s, layer)
        seq = h.shape[1]
        h2, h2_lo = _ln_res(h[0], mix[0], p["ln_g"][layer, 0][None], p["ln_b"][layer, 0][None])
        down = _mlp(h2, h2_lo, p["mlp_w_up"][layer], p["mlp_w_down"][layer], s["mlp_w_up"][layer],
                    s["mlp_w_down"][layer])
        h, h_lo = _ln_res(h2, down, p["ln_g"][layer, 1][None], p["ln_b"][layer, 1][None])
        h, h_lo = h.reshape(1, seq, D_MODEL), h_lo.reshape(1, seq, D_MODEL)
        if layer == N_A - 1:
            kv_shared = _project(h, h_lo, p["w_kv_shared"], s["w_kv_shared"])[0]
    return h


def _loss(diff, s, p, mem, positions, target):
    y = _forward({**p, **diff["small"]}, s, diff["x"], mem, positions)
    return 0.5 * jnp.sum(jnp.mean(jnp.square(y - target), axis=-1))


def _reorder_a_w_in(w):
    pad = jnp.zeros(w.shape[:-1] + (A_IN_PAD - A_IN,), w.dtype)
    return jnp.concatenate([w[..., :QKVZ_W], w[..., QKVZ_W + 2 * DN_HEADS:], w[..., QKVZ_W:QKVZ_W + 2 * DN_HEADS], pad],
                           axis=-1)


def _restore_a_w_in(w):
    return jnp.concatenate([w[..., :QKVZ_W], w[..., QKVZ_W + MEM_W:QKVZ_W + MEM_W + 2 * DN_HEADS],
                            w[..., QKVZ_W:QKVZ_W + MEM_W]], axis=-1)


def kernel(x, mem, positions, a_w_in, a_conv_w, a_A_log, a_dt_bias, a_norm_w, b_w_in, b_sinks, w_kv_shared, mem_w_kv, w_o, mlp_w_up, mlp_w_down, ln_g, ln_b, loss_target, m_a_w_in, m_a_conv_w, m_a_A_log, m_a_dt_bias, m_a_norm_w, m_b_w_in, m_b_sinks, m_w_kv_shared, m_mem_w_kv, m_w_o, m_mlp_w_up, m_mlp_w_down, m_ln_g, m_ln_b, v_a_w_in, v_a_conv_w, v_a_A_log, v_a_dt_bias, v_a_norm_w, v_b_w_in, v_b_sinks, v_w_kv_shared, v_mem_w_kv, v_w_o, v_mlp_w_up, v_mlp_w_down, v_ln_g, v_ln_b):
    w_sh = dict(a_w_in=a_w_in, a_conv_w=a_conv_w, a_A_log=a_A_log, a_dt_bias=a_dt_bias, a_norm_w=a_norm_w,
                b_w_in=b_w_in, b_sinks=b_sinks, w_kv_shared=w_kv_shared, mem_w_kv=mem_w_kv, w_o=w_o,
                mlp_w_up=mlp_w_up, mlp_w_down=mlp_w_down, ln_g=ln_g, ln_b=ln_b)
    m_sh = dict(a_w_in=m_a_w_in, a_conv_w=m_a_conv_w, a_A_log=m_a_A_log, a_dt_bias=m_a_dt_bias, a_norm_w=m_a_norm_w,
                b_w_in=m_b_w_in, b_sinks=m_b_sinks, w_kv_shared=m_w_kv_shared, mem_w_kv=m_mem_w_kv, w_o=m_w_o,
                mlp_w_up=m_mlp_w_up, mlp_w_down=m_mlp_w_down, ln_g=m_ln_g, ln_b=m_ln_b)
    v_sh = dict(a_w_in=v_a_w_in, a_conv_w=v_a_conv_w, a_A_log=v_a_A_log, a_dt_bias=v_a_dt_bias, a_norm_w=v_a_norm_w,
                b_w_in=v_b_w_in, b_sinks=v_b_sinks, w_kv_shared=v_w_kv_shared, mem_w_kv=v_mem_w_kv, w_o=v_w_o,
                mlp_w_up=v_mlp_w_up, mlp_w_down=v_mlp_w_down, ln_g=v_ln_g, ln_b=v_ln_b)
    shard_shapes = {n: w_sh[n].shape for n in WEIGHTS}
    rb, rows = _rows_for(w_sh)

    big, small = _pack(w_sh, rb, jnp.bfloat16)
    gbig, gsmall = _gather_weights(big.reshape(2, rb // 2, FLAT_W), small.reshape(2, SMALL_ROWS // 2, FLAT_W))
    gbig, gsmall = gbig.reshape(N_CHIPS, rb, FLAT_W), gsmall.reshape(N_CHIPS, SMALL_ROWS, FLAT_W)
    pieces = [_unpack(gbig[q], gsmall[q], shard_shapes) for q in range(N_CHIPS)]
    full = {n: jnp.concatenate([pieces[q][n] for q in range(N_CHIPS)], axis=SHARD_AXIS[n]) for n in SHARD_AXIS}
    for n in REPLICATED:
        full[n] = w_sh[n]
    big_w = {n: full[n] for n in BIG}
    big_w["a_w_in"] = _reorder_a_w_in(big_w["a_w_in"])
    small_w = {n: full[n] for n in SMALL}
    slots = {n: jnp.zeros(big_w[n].shape, jnp.float32) for n in BIG}

    loss, (grads, g_slots) = jax.value_and_grad(_loss, argnums=(0, 1))(
        {"x": x, "small": small_w}, slots, big_w, mem, positions, loss_target)
    loss = lax.psum(loss, ("x", "y", "c"))
    g_full = {**g_slots, **grads["small"]}
    g_full["a_w_in"] = _restore_a_w_in(g_full["a_w_in"])

    def shard_of(n, q):
        if n in REPLICATED:
            return g_full[n]
        size = shard_shapes[n][SHARD_AXIS[n]]
        return lax.slice_in_dim(g_full[n], q * size, (q + 1) * size, axis=SHARD_AXIS[n])

    parts = []
    for q in range(N_CHIPS):
        pb, ps = _pack({n: shard_of(n, q) for n in WEIGHTS}, rb, jnp.bfloat16)
        parts.append(jnp.concatenate([pb, ps.astype(jnp.bfloat16)], axis=0).reshape(2, rows // 2, FLAT_W))
    partials = jnp.stack(parts, axis=1)
    half = lax.axis_index("c").astype(jnp.int32).reshape(1)
    chip_partials = _add_pairs(partials, _swap_halves(partials), half)
    g_flat = _join_halves(_sum_chips(_scatter_grads(chip_partials))).reshape(rows, FLAT_W)

    flat = [jnp.concatenate(_pack(d, rb), axis=0) for d in (w_sh, m_sh, v_sh)]
    outs = (g_flat,) + tuple(_adamw(g_flat, *flat))
    g_o, d_o, m_o, v_o = [_unpack(o[:rb], o[rb:], shard_shapes) for o in outs]
    return (loss, grads["x"], *[g_o[n] for n in WEIGHTS], *[d_o[n] for n in WEIGHTS],
            *[m_o[n] for n in WEIGHTS], *[v_o[n] for n in WEIGHTS])
```

```python
import functools
import math

import jax
import jax.numpy as jnp
from jax import lax
from jax.experimental import pallas as pl
from jax.experimental.pallas import tpu as pltpu

D_MODEL = 1024
DEPTH = 4
N_A = DEPTH // 2
MEM_HEADS = 4
MEM_DH = D_MODEL // 16
MEM_W = MEM_HEADS * MEM_DH
DN_DK = 128
DN_DV = 128
DN_HEADS = (3 * D_MODEL) // (4 * DN_DV)
DN_QK_W = DN_HEADS * DN_DK
DN_V_W = DN_HEADS * DN_DV
CONV_WIDTH = 4
CHUNK = 64
SWA_DH = 64
SWA_HEADS = (3 * D_MODEL) // (4 * SWA_DH)
SWA_KV_HEADS = 2
SWA_GROUP = SWA_HEADS // SWA_KV_HEADS
SWA_Q_W = SWA_HEADS * SWA_DH
SWA_KV_W = SWA_KV_HEADS * SWA_DH
WINDOW = 128
ROPE_THETA = 10000.0
LN_EPS = 1e-5
NORM_EPS = 1e-6
DN_ALPHA = (2.0 * DEPTH) ** 0.25
A_IN = 2 * DN_QK_W + 2 * DN_V_W + 2 * DN_HEADS + MEM_W
A_IN_PAD = 3456
QKVZ_W = 2 * DN_QK_W + 2 * DN_V_W

ADAM_LR = 0.001
ADAM_B1 = 0.9
ADAM_B2 = 0.999
ADAM_EPS = 1e-08
ADAM_WD = 0.01
ADAM_STEP = 10

N_CHIPS = 4
FLAT_W = 1024
BIG = ("a_w_in", "b_w_in", "w_kv_shared", "mem_w_kv", "w_o", "mlp_w_up", "mlp_w_down")
SMALL = ("a_conv_w", "ln_g", "ln_b", "a_A_log", "a_dt_bias", "a_norm_w", "b_sinks")
REPLICATED = ("a_A_log", "a_dt_bias", "a_norm_w", "b_sinks")
WEIGHTS = ("a_w_in", "a_conv_w", "a_A_log", "a_dt_bias", "a_norm_w", "b_w_in", "b_sinks", "w_kv_shared",
           "mem_w_kv", "w_o", "mlp_w_up", "mlp_w_down", "ln_g", "ln_b")
SHARD_AXIS = {"a_w_in": 2, "a_conv_w": 2, "b_w_in": 1, "w_kv_shared": 0, "mem_w_kv": 1, "w_o": 1,
              "mlp_w_up": 2, "mlp_w_down": 1, "ln_g": 2, "ln_b": 2}
SMALL_ROWS = 32
ROW_ALIGN = 256

MESH = pl.DeviceIdType.MESH
HBM_SPEC = pl.BlockSpec(memory_space=pltpu.HBM)
VMEM_LIMIT = 48 * 1024 * 1024


def _rows_for(shards):
    n_big = sum(math.prod(shards[n].shape) for n in BIG)
    n_small = sum(math.prod(shards[n].shape) for n in SMALL)
    assert n_small <= SMALL_ROWS * FLAT_W
    total = -(-n_big // FLAT_W) + SMALL_ROWS
    total = -(-total // (2 * ROW_ALIGN)) * (2 * ROW_ALIGN)
    return total - SMALL_ROWS, total


def _pack(shards, rb, dtype_big=jnp.float32):
    big = jnp.concatenate([shards[n].reshape(-1).astype(dtype_big) for n in BIG])
    big = jnp.pad(big, (0, rb * FLAT_W - big.shape[0])).reshape(rb, FLAT_W)
    small = jnp.concatenate([shards[n].reshape(-1).astype(jnp.float32) for n in SMALL])
    small = jnp.pad(small, (0, SMALL_ROWS * FLAT_W - small.shape[0])).reshape(SMALL_ROWS, FLAT_W)
    return big, small


def _unpack(big, small, shapes):
    out = {}
    for flat, names in ((big.reshape(-1), BIG), (small.reshape(-1), SMALL)):
        off = 0
        for n in names:
            size = math.prod(shapes[n])
            out[n] = flat[off:off + size].reshape(shapes[n])
            off += size
    return out


def _other_chips(x, y):
    return [(1 - x, y), (x, 1 - y), (1 - x, 1 - y)]


def _gather_weights(big, small):
    def body(big_ref, small_ref, init_big_ref, init_small_ref, obig_ref, osmall_ref,
             send_sems, recv_sems, pass_send_sems, pass_recv_sems):
        del init_big_ref, init_small_ref
        x, y, c = lax.axis_index("x"), lax.axis_index("y"), lax.axis_index("c")
        me = 2 * x + y
        sibling = (x, y, 1 - c)
        pairs = ((big_ref, obig_ref), (small_ref, osmall_ref))
        sends = []
        for j, (px, py) in enumerate(_other_chips(x, y)):
            for i, (src, dst) in enumerate(pairs):
                sends.append(pltpu.make_async_remote_copy(
                    src_ref=src.at[c], dst_ref=dst.at[me, c], send_sem=send_sems.at[2 * j + i],
                    recv_sem=recv_sems.at[2 * j + i], device_id=(px, py, c), device_id_type=MESH))
        for cp in sends:
            cp.start()
        passed = []
        for j, (px, py) in enumerate(_other_chips(x, y)):
            for i, (src, dst) in enumerate(pairs):
                landed = dst.at[2 * px + py, c]
                pltpu.make_async_remote_copy(
                    src_ref=src.at[c], dst_ref=landed, send_sem=send_sems.at[2 * j + i],
                    recv_sem=recv_sems.at[2 * j + i], device_id=(px, py, c), device_id_type=MESH).wait_recv()
                passed.append(pltpu.make_async_remote_copy(
                    src_ref=landed, dst_ref=landed, send_sem=pass_send_sems.at[2 * j + i],
                    recv_sem=pass_recv_sems.at[2 * j + i], device_id=sibling, device_id_type=MESH))
                passed[-1].start()
        for j, (px, py) in enumerate(_other_chips(x, y)):
            for i, (src, dst) in enumerate(pairs):
                other_half = dst.at[2 * px + py, 1 - c]
                pltpu.make_async_remote_copy(
                    src_ref=other_half, dst_ref=other_half, send_sem=pass_send_sems.at[2 * j + i],
                    recv_sem=pass_recv_sems.at[2 * j + i], device_id=sibling, device_id_type=MESH).wait_recv()
        for cp in sends + passed:
            cp.wait_send()

    dma6 = pltpu.SemaphoreType.DMA((6,))
    four = lambda t: jnp.broadcast_to(t[None], (N_CHIPS,) + t.shape)
    return pl.pallas_call(
        body, name="gather_weights",
        out_shape=(jax.ShapeDtypeStruct((N_CHIPS,) + big.shape, big.dtype),
                   jax.ShapeDtypeStruct((N_CHIPS,) + small.shape, small.dtype)),
        in_specs=[HBM_SPEC] * 4, out_specs=(HBM_SPEC, HBM_SPEC), input_output_aliases={2: 0, 3: 1},
        scratch_shapes=[dma6, dma6, dma6, dma6],
    )(big, small, four(big), four(small))


def _scatter_grads(g):
    def body(g_ref, o_ref, send_sems, recv_sems, local_sem):
        x, y, c = lax.axis_index("x"), lax.axis_index("y"), lax.axis_index("c")
        me = 2 * x + y
        local = pltpu.make_async_copy(g_ref.at[me], o_ref.at[me], local_sem)
        local.start()
        sends = []
        for j, (px, py) in enumerate(_other_chips(x, y)):
            sends.append(pltpu.make_async_remote_copy(
                src_ref=g_ref.at[2 * px + py], dst_ref=o_ref.at[me], send_sem=send_sems.at[j], recv_sem=recv_sems.at[j],
                device_id=(px, py, c), device_id_type=MESH))
        for cp in sends:
            cp.start()
        for j, (px, py) in enumerate(_other_chips(x, y)):
            pltpu.make_async_remote_copy(
                src_ref=g_ref.at[me], dst_ref=o_ref.at[2 * px + py], send_sem=send_sems.at[j], recv_sem=recv_sems.at[j],
                device_id=(px, py, c), device_id_type=MESH).wait_recv()
        for cp in sends:
            cp.wait_send()
        local.wait()

    return pl.pallas_call(
        body, name="scatter_grads",
        out_shape=jax.ShapeDtypeStruct(g.shape, g.dtype),
        in_specs=[HBM_SPEC], out_specs=HBM_SPEC,
        scratch_shapes=[pltpu.SemaphoreType.DMA((3,)), pltpu.SemaphoreType.DMA((3,)), pltpu.SemaphoreType.DMA],
    )(g)


def _swap_halves(g):
    def body(g_ref, o_ref, send_sem, recv_sem):
        x, y, c = lax.axis_index("x"), lax.axis_index("y"), lax.axis_index("c")
        cp = pltpu.make_async_remote_copy(src_ref=g_ref.at[1 - c], dst_ref=o_ref, send_sem=send_sem, recv_sem=recv_sem,
                                          device_id=(x, y, 1 - c), device_id_type=MESH)
        cp.start()
        cp.wait()

    return pl.pallas_call(
        body, name="swap_halves",
        out_shape=jax.ShapeDtypeStruct(g.shape[1:], g.dtype),
        in_specs=[HBM_SPEC], out_specs=HBM_SPEC,
        scratch_shapes=[pltpu.SemaphoreType.DMA, pltpu.SemaphoreType.DMA],
    )(g)


def _join_halves(v):
    def body(v_ref, init_ref, o_ref, send_sem, recv_sem):
        del init_ref
        x, y, c = lax.axis_index("x"), lax.axis_index("y"), lax.axis_index("c")
        cp = pltpu.make_async_remote_copy(src_ref=v_ref, dst_ref=o_ref.at[c], send_sem=send_sem, recv_sem=recv_sem,
                                          device_id=(x, y, 1 - c), device_id_type=MESH)
        cp.start()
        cp.wait_send()
        pltpu.make_async_remote_copy(src_ref=v_ref, dst_ref=o_ref.at[1 - c], send_sem=send_sem, recv_sem=recv_sem,
                                     device_id=(x, y, 1 - c), device_id_type=MESH).wait_recv()

    return pl.pallas_call(
        body, name="join_halves",
        out_shape=jax.ShapeDtypeStruct((2,) + v.shape, v.dtype),
        in_specs=[HBM_SPEC, HBM_SPEC], out_specs=HBM_SPEC, input_output_aliases={1: 0},
        scratch_shapes=[pltpu.SemaphoreType.DMA, pltpu.SemaphoreType.DMA],
    )(v, jnp.stack([v, v]))


def _add_pairs(g, theirs, half):
    _, n, rows, width = g.shape
    assert rows % ROW_ALIGN == 0, rows

    def body(half_ref, g_ref, t_ref, o_ref):
        o_ref[...] = (g_ref[...].astype(jnp.float32) + t_ref[...].astype(jnp.float32)).astype(o_ref.dtype)

    blk = pl.BlockSpec((None, ROW_ALIGN, width), lambda p, i, h: (p, i, 0))
    grid_spec = pltpu.PrefetchScalarGridSpec(
        num_scalar_prefetch=1, grid=(n, rows // ROW_ALIGN),
        in_specs=[pl.BlockSpec((None, None, ROW_ALIGN, width), lambda p, i, h: (h[0], p, i, 0)), blk], out_specs=blk)
    return pl.pallas_call(
        body, name="add_pairs", grid_spec=grid_spec, out_shape=jax.ShapeDtypeStruct(theirs.shape, g.dtype),
        compiler_params=pltpu.CompilerParams(dimension_semantics=("parallel", "parallel")),
    )(half, g, theirs)


def _sum_chips(parts):
    n, rows, width = parts.shape
    assert rows % ROW_ALIGN == 0, rows

    def body(p_ref, o_ref):
        p = [p_ref[q].astype(jnp.float32) for q in range(n)]
        o_ref[...] = (p[0] + p[1]) + (p[2] + p[3])

    return pl.pallas_call(
        body, name="sum_chips", grid=(rows // ROW_ALIGN,),
        out_shape=jax.ShapeDtypeStruct((rows, width), jnp.float32),
        in_specs=[pl.BlockSpec((n, ROW_ALIGN, width), lambda i: (0, i, 0))],
        out_specs=pl.BlockSpec((ROW_ALIGN, width), lambda i: (i, 0)),
        compiler_params=pltpu.CompilerParams(dimension_semantics=("parallel",), vmem_limit_bytes=VMEM_LIMIT),
    )(parts)


def _adamw(g, w, m, v):
    rows, width = w.shape
    blk = ROW_ALIGN // 2

    def body(g_ref, w_ref, m_ref, v_ref, d_out, m_out, v_out):
        g = g_ref[...]
        m_new = ADAM_B1 * m_ref[...] + (1.0 - ADAM_B1) * g
        v_new = ADAM_B2 * v_ref[...] + (1.0 - ADAM_B2) * jnp.square(g)
        m_hat = m_new / (1.0 - ADAM_B1 ** ADAM_STEP)
        v_hat = v_new / (1.0 - ADAM_B2 ** ADAM_STEP)
        d_out[...] = -ADAM_LR * (m_hat / (jnp.sqrt(v_hat) + ADAM_EPS) + ADAM_WD * w_ref[...])
        m_out[...] = m_new
        v_out[...] = v_new

    spec = pl.BlockSpec((blk, width), lambda i: (i, 0))
    shape = jax.ShapeDtypeStruct((rows, width), jnp.float32)
    return pl.pallas_call(
        body, name="adamw", grid=(rows // blk,),
        out_shape=(shape,) * 3, in_specs=[spec] * 4, out_specs=(spec,) * 3,
        compiler_params=pltpu.CompilerParams(dimension_semantics=("parallel",), vmem_limit_bytes=VMEM_LIMIT),
    )(g, w, m, v)


def _tile(dim, pref):
    if dim <= pref:
        return dim
    for t in range(pref - pref % 128, 0, -128):
        if dim % t == 0:
            return t
    raise ValueError(f"no 128-aligned tile for {dim}")


def _matmul(a, b, *, ta=False, tb=False, name, epilogue=None, extra=None, out_dtype=jnp.float32):
    (k_a, m) = a.shape if ta else a.shape[::-1]
    (k_b, n) = b.shape[::-1] if tb else b.shape
    assert k_a == k_b, (a.shape, b.shape, ta, tb)
    k = k_a
    tk = _tile(k, 1152)
    nk = k // tk
    if ta:
        tm, tn = _tile(m, 1024), _tile(n, 2048 if m <= 1024 else 1024)
    else:
        tm, tn = _tile(m, 2048), _tile(n, 512 if nk == 1 else 1024)
    a_spec = pl.BlockSpec((tk, tm), lambda i, j, l: (l, i)) if ta else pl.BlockSpec((tm, tk), lambda i, j, l: (i, l))
    b_spec = pl.BlockSpec((tn, tk), lambda i, j, l: (j, l)) if tb else pl.BlockSpec((tk, tn), lambda i, j, l: (l, j))
    o_spec = pl.BlockSpec((tm, tn), lambda i, j, l: (i, j))
    dims = (((0 if ta else 1,), (1 if tb else 0,)), ((), ()))
    has_extra = epilogue == "relu2_grad"
    assert has_extra == (extra is not None)

    def body(*refs):
        a_ref, b_ref = refs[:2]
        outs = refs[2 + has_extra:2 + has_extra + (2 if epilogue == "relu2" else 1)]
        l = pl.program_id(2)
        part = lax.dot_general(a_ref[...].astype(jnp.bfloat16), b_ref[...].astype(jnp.bfloat16), dims,
                               preferred_element_type=jnp.float32)

        def finish(acc):
            if epilogue is None:
                outs[0][...] = acc.astype(out_dtype)
            elif epilogue == "relu2":
                outs[0][...] = acc.astype(jnp.bfloat16)
                outs[1][...] = jnp.square(jnp.maximum(acc, 0.0)).astype(jnp.bfloat16)
            else:
                outs[0][...] = (acc * (2.0 * jnp.maximum(refs[2][...].astype(jnp.float32), 0.0))).astype(out_dtype)

        if nk == 1:
            finish(part)
            return
        acc_ref = refs[-1]

        @pl.when(l == 0)
        def _():
            acc_ref[...] = part

        @pl.when((l > 0) & (l < nk - 1))
        def _():
            acc_ref[...] += part

        @pl.when(l == nk - 1)
        def _():
            finish(acc_ref[...] + part)

    if epilogue == "relu2":
        out_shape = (jax.ShapeDtypeStruct((m, n), jnp.bfloat16),) * 2
        out_specs = (o_spec, o_spec)
    else:
        out_shape = jax.ShapeDtypeStruct((m, n), out_dtype)
        out_specs = o_spec
    return pl.pallas_call(
        body, name=name, grid=(m // tm, n // tn, nk), out_shape=out_shape,
        in_specs=[a_spec, b_spec] + ([o_spec] if has_extra else []), out_specs=out_specs,
        scratch_shapes=[pltpu.VMEM((tm, tn), jnp.float32)] if nk > 1 else [],
        compiler_params=pltpu.CompilerParams(dimension_semantics=("parallel", "parallel", "arbitrary"),
                                             vmem_limit_bytes=VMEM_LIMIT),
    )(*((a, b) + ((extra,) if has_extra else ())))


def _lo(x):
    return lax.stop_gradient(x.astype(jnp.bfloat16))


@jax.custom_vjp
def _linear(x, x_lo, w, slot):
    del x, slot
    return _matmul(x_lo, w, name="linear_fwd")


def _linear_fwd(x, x_lo, w, slot):
    del x, slot
    return _matmul(x_lo, w, name="linear_fwd"), (x_lo, w)


def _linear_bwd(res, dy):
    x_lo, w = res
    dy = dy.astype(jnp.bfloat16)
    dx = _matmul(dy, w, tb=True, name="linear_dx")
    dw = _matmul(x_lo, dy, ta=True, name="linear_dw")
    return dx, jnp.zeros_like(x_lo), jnp.zeros_like(w), dw


_linear.defvjp(_linear_fwd, _linear_bwd)


@jax.custom_vjp
def _mlp(h, h_lo, w_up, w_down, slot_up, slot_down):
    return _mlp_fwd(h, h_lo, w_up, w_down, slot_up, slot_down)[0]


def _mlp_fwd(h, h_lo, w_up, w_down, slot_up, slot_down):
    del h, slot_up, slot_down
    up, act = _matmul(h_lo, w_up, name="mlp_up", epilogue="relu2")
    return _matmul(act, w_down, name="mlp_down"), (h_lo, up, act, w_up, w_down)


def _mlp_bwd(res, dy):
    h_lo, up, act, w_up, w_down = res
    dy = dy.astype(jnp.bfloat16)
    d_up = _matmul(dy, w_down, tb=True, name="mlp_d_up", epilogue="relu2_grad", extra=up, out_dtype=jnp.bfloat16)
    dw_down = _matmul(act, dy, ta=True, name="mlp_dw_down")
    dw_up = _matmul(h_lo, d_up, ta=True, name="mlp_dw_up")
    dh = _matmul(d_up, w_up, tb=True, name="mlp_dh")
    return dh, jnp.zeros_like(h_lo), jnp.zeros_like(w_up), jnp.zeros_like(w_down), dw_up, dw_down


_mlp.defvjp(_mlp_fwd, _mlp_bwd)


LN_ROWS = 256


def _ln_call(h, mix, g, b):
    s, d = h.shape
    tok = pl.BlockSpec((LN_ROWS, d), lambda i: (i, 0))
    vec = pl.BlockSpec((1, d), lambda i: (0, 0))
    stat = pl.BlockSpec((LN_ROWS, 1), lambda i: (i, 0))

    def body(h_ref, mix_ref, g_ref, b_ref, y_ref, ylo_ref, xhat_ref, rstd_ref):
        z = DN_ALPHA * h_ref[...] + mix_ref[...]
        mu = jnp.mean(z, axis=-1, keepdims=True)
        zc = z - mu
        rstd = lax.rsqrt(jnp.mean(jnp.square(zc), axis=-1, keepdims=True) + LN_EPS)
        xhat = zc * rstd
        y = xhat * g_ref[...] + b_ref[...]
        y_ref[...] = y
        ylo_ref[...] = y.astype(ylo_ref.dtype)
        xhat_ref[...] = xhat
        rstd_ref[...] = rstd

    sd = jax.ShapeDtypeStruct
    return pl.pallas_call(
        body, name="ln_fwd", grid=(s // LN_ROWS,),
        out_shape=(sd((s, d), jnp.float32), sd((s, d), jnp.bfloat16), sd((s, d), jnp.float32), sd((s, 1), jnp.float32)),
        in_specs=[tok, tok, vec, vec], out_specs=(tok, tok, tok, stat),
        compiler_params=pltpu.CompilerParams(dimension_semantics=("parallel",)),
    )(h, mix, g, b)


def _ln_grad_call(dy, xhat, rstd, g):
    s, d = dy.shape
    tok = pl.BlockSpec((LN_ROWS, d), lambda i: (i, 0))
    vec = pl.BlockSpec((1, d), lambda i: (0, 0))
    stat = pl.BlockSpec((LN_ROWS, 1), lambda i: (i, 0))

    def body(dy_ref, xhat_ref, rstd_ref, g_ref, dz_ref, dg_ref, db_ref):
        @pl.when(pl.program_id(0) == 0)
        def _():
            dg_ref[...] = jnp.zeros_like(dg_ref)
            db_ref[...] = jnp.zeros_like(db_ref)

        dy, xhat = dy_ref[...], xhat_ref[...]
        dyg = dy * g_ref[...]
        m1 = jnp.mean(dyg, axis=-1, keepdims=True)
        m2 = jnp.mean(dyg * xhat, axis=-1, keepdims=True)
        dz_ref[...] = rstd_ref[...] * (dyg - m1 - xhat * m2)
        dg_ref[...] += jnp.sum(dy * xhat, axis=0, keepdims=True)
        db_ref[...] += jnp.sum(dy, axis=0, keepdims=True)

    sd = jax.ShapeDtypeStruct
    return pl.pallas_call(
        body, name="ln_bwd", grid=(s // LN_ROWS,),
        out_shape=(sd((s, d), jnp.float32), sd((1, d), jnp.float32), sd((1, d), jnp.float32)),
        in_specs=[tok, tok, stat, vec], out_specs=(tok, vec, vec),
        compiler_params=pltpu.CompilerParams(dimension_semantics=("arbitrary",)),
    )(dy, xhat, rstd, g)


@jax.custom_vjp
def _ln_res(h, mix, g, b):
    return _ln_call(h, mix, g, b)[:2]


def _ln_res_fwd(h, mix, g, b):
    y, y_lo, xhat, rstd = _ln_call(h, mix, g, b)
    return (y, y_lo), (xhat, rstd, g)


def _ln_res_bwd(res, cts):
    xhat, rstd, g = res
    dz, dg, db = _ln_grad_call(cts[0], xhat, rstd, g)
    return DN_ALPHA * dz, dz, dg, db


_ln_res.defvjp(_ln_res_fwd, _ln_res_bwd)


MXU_DTYPE = jnp.bfloat16
DN_CB = 8
DN_GROUP = 8
DN_SCAN_CB = 4
DN_SCALE = DN_DK ** -0.5


def _dot(a, b, ca=1, cb=0):
    return lax.dot_general(a.astype(MXU_DTYPE), b.astype(MXU_DTYPE), (((ca,), (cb,)), ((), ())),
                           preferred_element_type=jnp.float32)


def _chunk_masks():
    row = lax.broadcasted_iota(jnp.int32, (CHUNK, CHUNK), 0)
    col = lax.broadcasted_iota(jnp.int32, (CHUNK, CHUNK), 1)
    return row >= col, row > col, row == col


def _to_col(row_vec):
    _, _, eye = _chunk_masks()
    return jnp.sum(jnp.where(eye, jnp.broadcast_to(row_vec, (CHUNK, CHUNK)), 0.0), axis=1, keepdims=True)


def _to_row(col_vec):
    _, _, eye = _chunk_masks()
    return jnp.sum(jnp.where(eye, jnp.broadcast_to(col_vec, (CHUNK, CHUNK)), 0.0), axis=0, keepdims=True)


def _last_row(col_vec):
    last = lax.broadcasted_iota(jnp.int32, (CHUNK, 1), 0) == CHUNK - 1
    return jnp.sum(jnp.where(last, col_vec, 0.0), axis=0, keepdims=True), last


def _chunk_terms(q, k, beta, gcc, gcr):
    incl, strict, _ = _chunk_masks()
    decay = jnp.where(incl, jnp.exp(jnp.minimum(gcc - gcr, 0.0)), 0.0)
    kb = k * beta
    lmat = jnp.where(strict, _dot(kb, k, 1, 1) * decay, 0.0)
    intra = jnp.where(incl, _dot(q, k, 1, 1) * decay, 0.0)
    return decay, kb, lmat, intra


def _dot3(a, b, ca=1, cb=0):
    if MXU_DTYPE == jnp.float32:
        return _dot(a, b, ca, cb)
    a_hi, b_hi = a.astype(MXU_DTYPE), b.astype(MXU_DTYPE)
    a_lo = (a - a_hi.astype(jnp.float32)).astype(MXU_DTYPE)
    b_lo = (b - b_hi.astype(jnp.float32)).astype(MXU_DTYPE)
    return _dot(a_hi, b_hi, ca, cb) + (_dot(a_hi, b_lo, ca, cb) + _dot(a_lo, b_hi, ca, cb))


def _unit_lower_inverse(lmats):
    _, _, eye = _chunk_masks()
    ident = jnp.where(eye, 1.0, 0.0)
    ts = [ident - m for m in lmats]
    ps = [_dot(m, m) for m in lmats]
    for _ in range(4):
        ts = [t + _dot(t, p) for t, p in zip(ts, ps)]
        ps = [_dot(p, p) for p in ps]
    ts = [t + _dot(t, p) for t, p in zip(ts, ps)]
    resids = [(t - ident) + _dot3(m, t) for m, t in zip(lmats, ts)]
    return [t - _dot(t, r) for t, r in zip(ts, resids)]


def _dn_specs(n_chunks):
    tok = pl.BlockSpec((DN_CB * CHUNK, DN_DK), lambda h, n: (n, h))
    rowv = pl.BlockSpec((None, DN_CB, CHUNK), lambda h, n: (h, n, 0))
    sq = pl.BlockSpec((None, DN_CB, CHUNK, CHUNK), lambda h, n: (h, n, 0, 0))
    lane = pl.BlockSpec((None, DN_CB, 1, DN_DV), lambda h, n: (h, n, 0, 0))
    planes = [pl.BlockSpec((None, DN_CB * CHUNK, DN_DK), functools.partial(lambda h, n, p: (p, n, h), p=p))
              for p in range(3)]
    return tok, rowv, sq, lane, planes


def _dn_prep(qkv, beta, gc):
    s = qkv.shape[1]
    n_chunks = s // CHUNK
    tok, rowv, sq, lane, planes = _dn_specs(n_chunks)
    tok_shape = qkv.shape[1:]

    def body(q_ref, k_ref, v_ref, beta_ref, gc_ref, u_ref, w_ref, qd_ref, kd_ref, intra_ref, t_ref, cd_ref):
        for c0 in range(0, DN_CB, DN_GROUP):
            chunks = range(c0, c0 + DN_GROUP)
            rhs, lmats = [], []
            for c in chunks:
                rows = pl.ds(c * CHUNK, CHUNK)
                q_c, k_c, v_c = q_ref[rows, :] * DN_SCALE, k_ref[rows, :], v_ref[rows, :]
                gcr_c = gc_ref[pl.ds(c, 1), :]
                beta_c, gcc_c = _to_col(beta_ref[pl.ds(c, 1), :]), _to_col(gcr_c)
                _, kb, lmat, intra = _chunk_terms(q_c, k_c, beta_c, gcc_c, gcr_c)
                eg = jnp.exp(gcc_c)
                g_last, _ = _last_row(gcc_c)
                qd_ref[rows, :] = (q_c * eg).astype(qd_ref.dtype)
                kd_ref[rows, :] = (k_c * jnp.exp(g_last - gcc_c)).astype(kd_ref.dtype)
                intra_ref[c] = intra.astype(intra_ref.dtype)
                cd_ref[c] = jnp.broadcast_to(jnp.exp(g_last), (1, DN_DV))
                rhs.append(jnp.concatenate([v_c * beta_c, kb * eg], axis=1))
                lmats.append(lmat)
            ts = _unit_lower_inverse(lmats)
            sols = [_dot3(t, r) for t, r in zip(ts, rhs)]
            for c, t, sol in zip(chunks, ts, sols):
                rows = pl.ds(c * CHUNK, CHUNK)
                t_ref[c] = t
                u_ref[rows, :] = sol[:, :DN_DV]
                w_ref[rows, :] = sol[:, DN_DV:].astype(w_ref.dtype)

    f32, mx = jnp.float32, MXU_DTYPE
    sd = jax.ShapeDtypeStruct
    return pl.pallas_call(
        body, name="dn_prep", grid=(DN_HEADS, n_chunks // DN_CB),
        out_shape=(sd(tok_shape, f32), sd(tok_shape, mx), sd(tok_shape, mx), sd(tok_shape, mx),
                   sd((DN_HEADS, n_chunks, CHUNK, CHUNK), mx), sd((DN_HEADS, n_chunks, CHUNK, CHUNK), f32),
                   sd((DN_HEADS, n_chunks, 1, DN_DV), f32)),
        in_specs=planes + [rowv, rowv], out_specs=(tok, tok, tok, tok, sq, sq, lane),
        compiler_params=pltpu.CompilerParams(dimension_semantics=("parallel", "parallel")),
    )(qkv, qkv, qkv, beta, gc)


def _dn_scan(u, w, qd, kd, intra, cd):
    s, width = u.shape
    n_chunks = s // CHUNK
    cb = DN_SCAN_CB
    tok = pl.BlockSpec((cb * CHUNK, width), lambda n: (n, 0))
    sq = pl.BlockSpec((DN_HEADS, cb, CHUNK, CHUNK), lambda n: (0, n, 0, 0))
    lane = pl.BlockSpec((DN_HEADS, cb, 1, DN_DV), lambda n: (0, n, 0, 0))
    st = pl.BlockSpec((DN_HEADS, cb, DN_DK, DN_DV), lambda n: (0, n, 0, 0))

    def body(u_ref, w_ref, qd_ref, kd_ref, intra_ref, cd_ref, o_ref, vn_ref, st_ref, state):
        @pl.when(pl.program_id(0) == 0)
        def _():
            state[...] = jnp.zeros_like(state)

        heads = range(DN_HEADS)
        cols = [pl.ds(h * DN_DK, DN_DK) for h in heads]
        s_f = [state[h] for h in heads]
        for c in range(cb):
            rows = pl.ds(c * CHUNK, CHUNK)
            s_mx = [s.astype(MXU_DTYPE) for s in s_f]
            for h in heads:
                st_ref[h, c] = s_mx[h]
            ws = [_dot(w_ref[rows, cols[h]], s_mx[h]) for h in heads]
            qs = [_dot(qd_ref[rows, cols[h]], s_mx[h]) for h in heads]
            v_new = [(u_ref[rows, cols[h]] - ws[h]).astype(MXU_DTYPE) for h in heads]
            inner = [_dot(intra_ref[h, c], v_new[h]) for h in heads]
            outer = [_dot(kd_ref[rows, cols[h]], v_new[h], 0, 0) for h in heads]
            for h in heads:
                vn_ref[rows, cols[h]] = v_new[h]
                o_ref[rows, cols[h]] = qs[h] + inner[h]
            s_f = [s_f[h] * cd_ref[h, c] + outer[h] for h in heads]
        for h in heads:
            state[h] = s_f[h]

    sd = jax.ShapeDtypeStruct
    return pl.pallas_call(
        body, name="dn_scan", grid=(n_chunks // cb,),
        out_shape=(sd(u.shape, jnp.float32), sd(u.shape, MXU_DTYPE),
                   sd((DN_HEADS, n_chunks, DN_DK, DN_DV), MXU_DTYPE)),
        in_specs=[tok, tok, tok, tok, sq, lane], out_specs=(tok, tok, st),
        scratch_shapes=[pltpu.VMEM((DN_HEADS, DN_DK, DN_DV), jnp.float32)],
        compiler_params=pltpu.CompilerParams(dimension_semantics=("arbitrary",)),
    )(u, w, qd, kd, intra, cd)


def _dn_bwd_scan(do, w, qd, kd, intra, cd, vn, st):
    s, width = do.shape
    n_chunks = s // CHUNK
    cb = DN_SCAN_CB
    last = n_chunks // cb - 1
    tok = pl.BlockSpec((cb * CHUNK, width), lambda n: (last - n, 0))
    sq = pl.BlockSpec((DN_HEADS, cb, CHUNK, CHUNK), lambda n: (0, last - n, 0, 0))
    lane = pl.BlockSpec((DN_HEADS, cb, 1, DN_DV), lambda n: (0, last - n, 0, 0))
    stt = pl.BlockSpec((DN_HEADS, cb, DN_DK, DN_DV), lambda n: (0, last - n, 0, 0))

    def body(do_ref, w_ref, qd_ref, kd_ref, intra_ref, cd_ref, vn_ref, st_ref,
             du_ref, dw_ref, dqd_ref, dkd_ref, dintra_ref, dgl_ref, dstate):
        @pl.when(pl.program_id(0) == 0)
        def _():
            dstate[...] = jnp.zeros_like(dstate)

        heads = range(DN_HEADS)
        cols = [pl.ds(h * DN_DK, DN_DK) for h in heads]
        ds_f = [dstate[h] for h in heads]
        for c in reversed(range(cb)):
            rows = pl.ds(c * CHUNK, CHUNK)
            ds_mx = [d.astype(MXU_DTYPE) for d in ds_f]
            do_h = [do_ref[rows, cols[h]].astype(MXU_DTYPE) for h in heads]
            dv_a = [_dot(intra_ref[h, c], do_h[h], 0, 0) for h in heads]
            dv_b = [_dot(kd_ref[rows, cols[h]], ds_mx[h]) for h in heads]
            d_intra = [_dot(do_h[h], vn_ref[rows, cols[h]], 1, 1) for h in heads]
            d_qd = [_dot(do_h[h], st_ref[h, c], 1, 1) for h in heads]
            d_kd = [_dot(vn_ref[rows, cols[h]], ds_mx[h], 1, 1) for h in heads]
            ds_q = [_dot(qd_ref[rows, cols[h]], do_h[h], 0, 0) for h in heads]
            dv_new = [dv_a[h] + dv_b[h] for h in heads]
            dv_mx = [d.astype(MXU_DTYPE) for d in dv_new]
            d_w = [_dot(dv_mx[h], st_ref[h, c], 1, 1) for h in heads]
            ds_w = [_dot(w_ref[rows, cols[h]], dv_mx[h], 0, 0) for h in heads]
            ds_next = []
            for h in heads:
                du_ref[rows, cols[h]] = dv_new[h]
                dintra_ref[h, c] = d_intra[h]
                dqd_ref[rows, cols[h]] = d_qd[h]
                dkd_ref[rows, cols[h]] = d_kd[h]
                dw_ref[rows, cols[h]] = -d_w[h]
                cd_h = cd_ref[h, c]
                dcd = jnp.sum(jnp.sum(st_ref[h, c].astype(jnp.float32) * ds_f[h], axis=1, keepdims=True), axis=0,
                              keepdims=True)
                dgl_ref[h, c] = dcd * cd_h
                ds_next.append(ds_q[h] + ds_f[h] * cd_h - ds_w[h])
            ds_f = ds_next
        for h in heads:
            dstate[h] = ds_f[h]

    sd = jax.ShapeDtypeStruct
    f32 = jnp.float32
    return pl.pallas_call(
        body, name="dn_bwd_scan", grid=(n_chunks // cb,),
        out_shape=(sd(do.shape, f32), sd(do.shape, f32), sd(do.shape, f32), sd(do.shape, f32),
                   sd((DN_HEADS, n_chunks, CHUNK, CHUNK), f32), sd((DN_HEADS, n_chunks, 1, DN_DV), f32)),
        in_specs=[tok, tok, tok, tok, sq, lane, tok, stt], out_specs=(tok, tok, tok, tok, sq, lane),
        scratch_shapes=[pltpu.VMEM((DN_HEADS, DN_DK, DN_DV), f32)],
        compiler_params=pltpu.CompilerParams(dimension_semantics=("arbitrary",)),
    )(do, w, qd, kd, intra, cd, vn, st)


def _dn_bwd_chunks(qkv, beta, gc, t, u, w, du, dw, dqd, dkd, dintra, dgl):
    s = qkv.shape[1]
    n_chunks = s // CHUNK
    tok, rowv, sq, lane, planes = _dn_specs(n_chunks)
    all_planes = pl.BlockSpec((3, DN_CB * CHUNK, DN_DK), lambda h, n: (0, n, h))

    def body(q_ref, k_ref, v_ref, beta_ref, gc_ref, t_ref, u_ref, w_ref, du_ref, dw_ref, dqd_ref, dkd_ref,
             dintra_ref, dgl_ref, dqkv_ref, dbeta_ref, dgc_ref):
        incl, strict, _ = _chunk_masks()

        def first(c):
            rows = pl.ds(c * CHUNK, CHUNK)
            q_c, k_c = q_ref[rows, :] * DN_SCALE, k_ref[rows, :]
            gcr_c = gc_ref[pl.ds(c, 1), :]
            beta_c, gcc_c = _to_col(beta_ref[pl.ds(c, 1), :]), _to_col(gcr_c)
            decay, kb, lmat, intra = _chunk_terms(q_c, k_c, beta_c, gcc_c, gcr_c)
            d_sol = jnp.concatenate([du_ref[rows, :], dw_ref[rows, :]], axis=1)
            d_rhs = _dot3(t_ref[c], d_sol, 0, 0)
            return dict(rows=rows, q=q_c, k=k_c, beta=beta_c, gcc=gcc_c, decay=decay, kb=kb, lmat=lmat, intra=intra,
                        d_rhs=d_rhs)

        def second(c, e):
            sol = jnp.concatenate([u_ref[e["rows"], :], w_ref[e["rows"], :].astype(jnp.float32)], axis=1)
            e["d_l"] = jnp.where(strict, -_dot(e["d_rhs"], sol, 1, 1), 0.0)
            e["d_intra"] = jnp.where(incl, dintra_ref[c], 0.0)
            d_qk = e["d_intra"] * e["decay"]
            e["dq"] = _dot(d_qk, e["k"])
            e["dk"] = _dot(d_qk, e["q"], 0, 0)

        def third(e):
            d_a = e["d_l"] * e["decay"]
            e["dkb"] = _dot(d_a, e["k"])
            e["dk"] = e["dk"] + _dot(d_a, e["kb"], 0, 0)

        def last(c, e):
            rows, q_c, k_c, beta_c, gcc_c = e["rows"], e["q"], e["k"], e["beta"], e["gcc"]
            v_c = v_ref[rows, :]
            eg = jnp.exp(gcc_c)
            g_last, is_last = _last_row(gcc_c)
            e_rev = jnp.exp(g_last - gcc_c)
            d_rhs_u, d_rhs_w = e["d_rhs"][:, :DN_DV], e["d_rhs"][:, DN_DV:]
            dqkv_ref[2, rows, :] = d_rhs_u * beta_c
            dbeta = jnp.sum(d_rhs_u * v_c, axis=1, keepdims=True)
            dkb = e["dkb"] + d_rhs_w * eg
            dgc = jnp.sum(d_rhs_w * e["kb"] * eg, axis=1, keepdims=True)
            m1 = e["d_l"] * e["lmat"]
            dgc = dgc + jnp.sum(m1, axis=1, keepdims=True)
            dgr = -jnp.sum(m1, axis=0, keepdims=True)
            m2 = e["d_intra"] * e["intra"]
            dgc = dgc + jnp.sum(m2, axis=1, keepdims=True)
            dgr = dgr - jnp.sum(m2, axis=0, keepdims=True)
            dqd = dqd_ref[rows, :]
            dq = e["dq"] + dqd * eg
            dgc = dgc + jnp.sum(dqd * q_c * eg, axis=1, keepdims=True)
            dkd = dkd_ref[rows, :]
            dk = e["dk"] + dkd * e_rev
            tk = jnp.sum(dkd * k_c * e_rev, axis=1, keepdims=True)
            dgc = dgc - tk
            d_last = dgl_ref[c][:, :1] + jnp.sum(tk, axis=0, keepdims=True)
            dgc = dgc + jnp.where(is_last, d_last, 0.0)
            dk = dk + dkb * beta_c
            dbeta = dbeta + jnp.sum(dkb * k_c, axis=1, keepdims=True)
            dqkv_ref[0, rows, :] = dq * DN_SCALE
            dqkv_ref[1, rows, :] = dk
            dbeta_ref[pl.ds(c, 1), :] = _to_row(dbeta)
            dgc_ref[pl.ds(c, 1), :] = _to_row(dgc) + dgr

        for c0 in range(0, DN_CB, DN_GROUP):
            chunks = range(c0, c0 + DN_GROUP)
            env = [first(c) for c in chunks]
            for c, e in zip(chunks, env):
                second(c, e)
            for e in env:
                third(e)
            for c, e in zip(chunks, env):
                last(c, e)

    sd = jax.ShapeDtypeStruct
    f32 = jnp.float32
    return pl.pallas_call(
        body, name="dn_bwd_chunks", grid=(DN_HEADS, n_chunks // DN_CB),
        out_shape=(sd(qkv.shape, f32), sd(beta.shape, f32), sd(gc.shape, f32)),
        in_specs=planes + [rowv, rowv, sq, tok, tok, tok, tok, tok, tok, sq, lane],
        out_specs=(all_planes, rowv, rowv),
        compiler_params=pltpu.CompilerParams(dimension_semantics=("parallel", "parallel")),
    )(qkv, qkv, qkv, beta, gc, t, u, w, du, dw, dqd, dkd, dintra, dgl)


@jax.custom_vjp
def _delta_rule_op(qkv, beta, gc):
    return _delta_rule_fwd(qkv, beta, gc)[0]


def _delta_rule_fwd(qkv, beta, gc):
    u, w, qd, kd, intra, t, cd = _dn_prep(qkv, beta, gc)
    out, vn, st = _dn_scan(u, w, qd, kd, intra, cd)
    return out, (qkv, beta, gc, u, w, qd, kd, intra, t, cd, vn, st)


def _delta_rule_bwd(res, do):
    qkv, beta, gc, u, w, qd, kd, intra, t, cd, vn, st = res
    du, dw, dqd, dkd, dintra, dgl = _dn_bwd_scan(do, w, qd, kd, intra, cd, vn, st)
    return _dn_bwd_chunks(qkv, beta, gc, t, u, w, du, dw, dqd, dkd, dintra, dgl)


_delta_rule_op.defvjp(_delta_rule_fwd, _delta_rule_bwd)


def _gated_delta_rule(qkv, g, beta):
    s, h = g.shape
    n_chunks = s // CHUNK
    gc = jnp.cumsum(g.T.reshape(h, n_chunks, CHUNK), axis=-1)
    return _delta_rule_op(qkv, beta.T.reshape(h, n_chunks, CHUNK), gc)


PRE_ROWS = 512
HALO = 8
PRE_W = DN_QK_W


def _shift_rows(xs, k):
    return pltpu.roll(xs, k, 0)[HALO:]


def _conv_silu(x_ref, halo_ref, w_ref, first_block):
    halo = jnp.where(first_block, 0.0, halo_ref[...])
    xs = jnp.concatenate([halo, x_ref[...]], axis=0)
    taps = [_shift_rows(xs, CONV_WIDTH - 1 - j) for j in range(CONV_WIDTH - 1)] + [x_ref[...]]
    conv = sum(w_ref[pl.ds(j, 1), :] * taps[j] for j in range(CONV_WIDTH))
    return conv, jax.nn.sigmoid(conv), taps


def _pre_specs():
    blk = pl.BlockSpec((PRE_ROWS, PRE_W), lambda j, i: (i, j))
    prev = pl.BlockSpec((HALO, PRE_W), lambda j, i: (jnp.maximum(i * (PRE_ROWS // HALO) - 1, 0), j))
    wts = pl.BlockSpec((CONV_WIDTH, PRE_W), lambda j, i: (0, j))
    plane = pl.BlockSpec((None, PRE_ROWS, PRE_W), lambda j, i: (j, i, 0))
    return blk, prev, wts, plane


def _pre_fwd_call(x, conv_w):
    s = x.shape[0]
    blk, prev, wts, plane = _pre_specs()

    def body(x_ref, halo_ref, w_ref, o_ref):
        conv, sig, _ = _conv_silu(x_ref, halo_ref, w_ref, pl.program_id(1) == 0)
        act = conv * sig
        is_v = pl.program_id(0) == 2
        for h in range(DN_HEADS):
            cols = slice(h * DN_DK, (h + 1) * DN_DK)
            a_h = act[:, cols]
            r = lax.rsqrt(jnp.sum(a_h * a_h, axis=-1, keepdims=True) + NORM_EPS)
            o_ref[:, cols] = a_h * jnp.where(is_v, 1.0, r)

    return pl.pallas_call(
        body, name="pre_fwd", grid=(3, s // PRE_ROWS),
        out_shape=jax.ShapeDtypeStruct((3, s, PRE_W), jnp.float32),
        in_specs=[blk, prev, wts], out_specs=plane,
        compiler_params=pltpu.CompilerParams(dimension_semantics=("parallel", "parallel")),
    )(x, x, conv_w)


def _pre_bwd_act_call(x, conv_w, d_out):
    s = x.shape[0]
    blk, prev, wts, plane = _pre_specs()

    def body(x_ref, halo_ref, w_ref, do_ref, dc_ref):
        conv, sig, _ = _conv_silu(x_ref, halo_ref, w_ref, pl.program_id(1) == 0)
        act = conv * sig
        d_silu = sig * (1.0 + conv * (1.0 - sig))
        is_v = pl.program_id(0) == 2
        for h in range(DN_HEADS):
            cols = slice(h * DN_DK, (h + 1) * DN_DK)
            a_h, do_h = act[:, cols], do_ref[:, cols]
            r = lax.rsqrt(jnp.sum(a_h * a_h, axis=-1, keepdims=True) + NORM_EPS)
            n_h = a_h * r
            d_norm = r * (do_h - n_h * jnp.sum(do_h * n_h, axis=-1, keepdims=True))
            dc_ref[:, cols] = jnp.where(is_v, do_h, d_norm) * d_silu[:, cols]

    return pl.pallas_call(
        body, name="pre_bwd_act", grid=(3, s // PRE_ROWS),
        out_shape=jax.ShapeDtypeStruct(x.shape, jnp.float32),
        in_specs=[blk, prev, wts, plane], out_specs=blk,
        compiler_params=pltpu.CompilerParams(dimension_semantics=("parallel", "parallel")),
    )(x, x, conv_w, d_out)


def _pre_bwd_conv_call(x, conv_w, dc):
    s = x.shape[0]
    n_blocks = s // PRE_ROWS
    blk, prev, wts, _ = _pre_specs()
    nxt = pl.BlockSpec((HALO, PRE_W), lambda j, i: (jnp.minimum((i + 1) * (PRE_ROWS // HALO), s // HALO - 1), j))

    def body(x_ref, halo_ref, w_ref, dc_ref, dcn_ref, dx_ref, dw_ref):
        i = pl.program_id(1)

        @pl.when(i == 0)
        def _():
            dw_ref[...] = jnp.zeros_like(dw_ref)

        dcv = dc_ref[...]
        ahead = jnp.concatenate([dcv, jnp.where(i == n_blocks - 1, 0.0, dcn_ref[...])], axis=0)
        dx = w_ref[pl.ds(CONV_WIDTH - 1, 1), :] * dcv
        for j in range(CONV_WIDTH - 1):
            k = CONV_WIDTH - 1 - j
            dx = dx + w_ref[pl.ds(j, 1), :] * pltpu.roll(ahead, PRE_ROWS + HALO - k, 0)[:PRE_ROWS]
        dx_ref[...] = dx
        halo = jnp.where(i == 0, 0.0, halo_ref[...])
        xs = jnp.concatenate([halo, x_ref[...]], axis=0)
        for j in range(CONV_WIDTH):
            tap = x_ref[...] if j == CONV_WIDTH - 1 else _shift_rows(xs, CONV_WIDTH - 1 - j)
            dw_ref[pl.ds(j, 1), :] += jnp.sum(dcv * tap, axis=0, keepdims=True)

    sd = jax.ShapeDtypeStruct
    return pl.pallas_call(
        body, name="pre_bwd_conv", grid=(3, n_blocks),
        out_shape=(sd(x.shape, jnp.float32), sd(conv_w.shape, jnp.float32)),
        in_specs=[blk, prev, wts, blk, nxt], out_specs=(blk, wts),
        compiler_params=pltpu.CompilerParams(dimension_semantics=("parallel", "arbitrary")),
    )(x, x, conv_w, dc, dc)


@jax.custom_vjp
def _pre_op(x, conv_w):
    return _pre_fwd_call(x, conv_w)


def _pre_op_fwd(x, conv_w):
    return _pre_fwd_call(x, conv_w), (x, conv_w)


def _pre_op_bwd(res, d_out):
    x, conv_w = res
    return _pre_bwd_conv_call(x, conv_w, _pre_bwd_act_call(x, conv_w, d_out))


_pre_op.defvjp(_pre_op_fwd, _pre_op_bwd)


def _project(h, h_lo, w, slot):
    b, s, d = h.shape
    return _linear(h.reshape(b * s, d), h_lo.reshape(b * s, d), w, slot).reshape(b, s, w.shape[1])


def _rope_table(positions, dh):
    inv_freq = ROPE_THETA ** (-jnp.arange(0, dh, 2, dtype=jnp.float32) / dh)
    ang = positions.astype(jnp.float32)[:, None] * inv_freq
    reps = 128 // (dh // 2)
    return jnp.concatenate([jnp.tile(jnp.cos(ang), (1, reps)), jnp.tile(jnp.sin(ang), (1, reps))], axis=-1)


GATE_ROWS = 512


def _gate_terms(o_h, z_h):
    r = lax.rsqrt(jnp.mean(o_h * o_h, axis=-1, keepdims=True) + NORM_EPS)
    sig = jax.nn.sigmoid(z_h)
    return r, o_h * r, sig, z_h * sig


def _gate_fwd_call(o, z, nw):
    tok = pl.BlockSpec((GATE_ROWS, DN_V_W), lambda i: (i, 0))
    vec = pl.BlockSpec((1, DN_DV), lambda i: (0, 0))

    def body(o_ref, z_ref, nw_ref, y_ref):
        for h in range(DN_HEADS):
            cols = pl.ds(h * DN_DV, DN_DV)
            _, n_h, _, g_h = _gate_terms(o_ref[:, cols], z_ref[:, cols])
            y_ref[:, cols] = n_h * nw_ref[...] * g_h

    return pl.pallas_call(
        body, name="gate_fwd", grid=(o.shape[0] // GATE_ROWS,),
        out_shape=jax.ShapeDtypeStruct(o.shape, jnp.float32), in_specs=[tok, tok, vec], out_specs=tok,
        compiler_params=pltpu.CompilerParams(dimension_semantics=("parallel",)),
    )(o, z, nw)


def _gate_bwd_call(o, z, nw, dy):
    tok = pl.BlockSpec((GATE_ROWS, DN_V_W), lambda i: (i, 0))
    vec = pl.BlockSpec((1, DN_DV), lambda i: (0, 0))

    def body(o_ref, z_ref, nw_ref, dy_ref, do_ref, dz_ref, dnw_ref):
        @pl.when(pl.program_id(0) == 0)
        def _():
            dnw_ref[...] = jnp.zeros_like(dnw_ref)

        for h in range(DN_HEADS):
            cols = pl.ds(h * DN_DV, DN_DV)
            z_h, dy_h = z_ref[:, cols], dy_ref[:, cols]
            r, n_h, sig, g_h = _gate_terms(o_ref[:, cols], z_h)
            dz_ref[:, cols] = dy_h * n_h * nw_ref[...] * (sig * (1.0 + z_h * (1.0 - sig)))
            dn = dy_h * nw_ref[...] * g_h
            do_ref[:, cols] = r * (dn - n_h * jnp.mean(dn * n_h, axis=-1, keepdims=True))
            dnw_ref[...] += jnp.sum(dy_h * n_h * g_h, axis=0, keepdims=True)

    sd = jax.ShapeDtypeStruct
    return pl.pallas_call(
        body, name="gate_bwd", grid=(o.shape[0] // GATE_ROWS,),
        out_shape=(sd(o.shape, jnp.float32), sd(o.shape, jnp.float32), sd(nw.shape, jnp.float32)),
        in_specs=[tok, tok, vec, tok], out_specs=(tok, tok, vec),
        compiler_params=pltpu.CompilerParams(dimension_semantics=("arbitrary",)),
    )(o, z, nw, dy)


@jax.custom_vjp
def _gate_op(o, z, nw):
    return _gate_fwd_call(o, z, nw)


def _gate_op_fwd(o, z, nw):
    return _gate_fwd_call(o, z, nw), (o, z, nw)


def _gate_op_bwd(res, dy):
    return _gate_bwd_call(*res, dy)


_gate_op.defvjp(_gate_op_fwd, _gate_op_bwd)


_MASKED = -1e30


def _swa_probs(qs, k_h, sinks, valid):
    ss = [jnp.where(valid, _dot(q_h, k_h, 1, 1) * (SWA_DH ** -0.5), _MASKED) for q_h in qs]
    ms = [jnp.maximum(jnp.max(s, axis=-1, keepdims=True), sink) for s, sink in zip(ss, sinks)]
    ps = [jnp.exp(s - m) for s, m in zip(ss, ms)]
    es = [jnp.exp(sink - m) for sink, m in zip(sinks, ms)]
    invs = [1.0 / (jnp.sum(p, axis=-1, keepdims=True) + e) for p, e in zip(ps, es)]
    return [p * inv for p, inv in zip(ps, invs)], [e * inv for e, inv in zip(es, invs)]


def _swa_valid(n):
    qi = lax.broadcasted_iota(jnp.int32, (WINDOW, 2 * WINDOW), 0)
    kj = lax.broadcasted_iota(jnp.int32, (WINDOW, 2 * WINDOW), 1)
    diff = qi + WINDOW - kj
    return (diff >= 0) & (diff < WINDOW) & ((kj >= WINDOW) | (n > 0))


def _rotate_half(x, transpose=False):
    half = SWA_DH // 2
    lower = lax.broadcasted_iota(jnp.int32, x.shape, 1) % SWA_DH < half
    ahead, behind = pltpu.roll(x, 128 - half, 1), pltpu.roll(x, half, 1)
    return jnp.where(lower, ahead, -behind) if transpose else jnp.where(lower, -ahead, behind)


def _rope(x, table):
    return x * table[:, :128] + _rotate_half(x) * table[:, 128:]


def _unrope(dy, table):
    return dy * table[:, :128] + _rotate_half(dy * table[:, 128:], transpose=True)


def _swa_specs():
    qs = pl.BlockSpec((WINDOW, SWA_Q_W), lambda n: (n, 0))
    first = lambda n: jnp.maximum(n - 1, 0)
    kv = [pl.BlockSpec((WINDOW, SWA_KV_W), lambda n: (first(n), 0)), pl.BlockSpec((WINDOW, SWA_KV_W), lambda n: (n, 0)),
          pl.BlockSpec((WINDOW, SWA_KV_W), lambda n: (first(n), 1)), pl.BlockSpec((WINDOW, SWA_KV_W), lambda n: (n, 1))]
    tables = [pl.BlockSpec((WINDOW, 256), lambda n: (first(n), 0)), pl.BlockSpec((WINDOW, 256), lambda n: (n, 0))]
    cur = pl.BlockSpec((WINDOW, SWA_KV_W), lambda n: (n, 0))
    sk = pl.BlockSpec((SWA_HEADS, 1, 128), lambda n: (0, 0, 0))
    return qs, kv, tables, cur, sk


def _swa_load(q_ref, kp_ref, kc_ref, vp_ref, vc_ref, tp_ref, tc_ref):
    table_kk = jnp.concatenate([tp_ref[...], tc_ref[...]], axis=0)
    kk = _rope(jnp.concatenate([kp_ref[...], kc_ref[...]], axis=0), table_kk)
    vv = jnp.concatenate([vp_ref[...], vc_ref[...]], axis=0)
    q_rot = []
    for b in range(SWA_Q_W // 128):
        pair = _rope(q_ref[:, pl.ds(b * 128, 128)], tc_ref[...])
        q_rot += [pair[:, :SWA_DH], pair[:, SWA_DH:]]
    split = lambda t: [t[:, hkv * SWA_DH:(hkv + 1) * SWA_DH] for hkv in range(SWA_KV_HEADS)]
    return q_rot, split(kk), split(vv), table_kk


def _swa_fwd_call(q, kv, table, sinks):
    qs, kvs, tables, _, sk = _swa_specs()

    def body(q_ref, kp_ref, kc_ref, vp_ref, vc_ref, tp_ref, tc_ref, sink_ref, o_ref):
        valid = _swa_valid(pl.program_id(0))
        q_rot, kk, vv, _ = _swa_load(q_ref, kp_ref, kc_ref, vp_ref, vc_ref, tp_ref, tc_ref)
        for hkv in range(SWA_KV_HEADS):
            heads = range(hkv * SWA_GROUP, (hkv + 1) * SWA_GROUP)
            probs, _ = _swa_probs([q_rot[h] for h in heads], kk[hkv], [sink_ref[h][:, :1] for h in heads], valid)
            outs = [_dot(p, vv[hkv]) for p in probs]
            for h, o in zip(heads, outs):
                o_ref[:, pl.ds(h * SWA_DH, SWA_DH)] = o

    return pl.pallas_call(
        body, name="swa_fwd", grid=(q.shape[0] // WINDOW,),
        out_shape=jax.ShapeDtypeStruct(q.shape, jnp.float32),
        in_specs=[qs] + kvs + tables + [sk], out_specs=qs,
        compiler_params=pltpu.CompilerParams(dimension_semantics=("parallel",)),
    )(q, kv, kv, kv, kv, table, table, sinks)


def _swa_bwd_call(q, kv, table, sinks, do):
    qs, kvs, tables, cur, sk = _swa_specs()

    def body(q_ref, kp_ref, kc_ref, vp_ref, vc_ref, tp_ref, tc_ref, sink_ref, do_ref,
             dq_ref, dkc_ref, dkp_ref, dvc_ref, dvp_ref, ds_ref):
        @pl.when(pl.program_id(0) == 0)
        def _():
            ds_ref[...] = jnp.zeros_like(ds_ref)

        valid = _swa_valid(pl.program_id(0))
        q_rot, kk, vv, table_kk = _swa_load(q_ref, kp_ref, kc_ref, vp_ref, vc_ref, tp_ref, tc_ref)
        lane0 = lax.broadcasted_iota(jnp.int32, (1, 128), 1) == 0
        dq_heads, dk_heads, dv_heads = [], [], []
        for hkv in range(SWA_KV_HEADS):
            k_h, v_h = kk[hkv], vv[hkv]
            heads = range(hkv * SWA_GROUP, (hkv + 1) * SWA_GROUP)
            q_hs = [q_rot[h] for h in heads]
            dos = [do_ref[:, pl.ds(h * SWA_DH, SWA_DH)] for h in heads]
            probs, p_sinks = _swa_probs(q_hs, k_h, [sink_ref[h][:, :1] for h in heads], valid)
            dps = [_dot(do_h, v_h, 1, 1) for do_h in dos]
            rss = [jnp.sum(p * dp, axis=-1, keepdims=True) for p, dp in zip(probs, dps)]
            d_ss = [p * (dp - rs) for p, dp, rs in zip(probs, dps, rss)]
            dq_heads += [_dot(d_s, k_h) * (SWA_DH ** -0.5) for d_s in d_ss]
            dks = [_dot(d_s, q_h, 0, 0) for d_s, q_h in zip(d_ss, q_hs)]
            dvs = [_dot(p, do_h, 0, 0) for p, do_h in zip(probs, dos)]
            for h, p_sink, rs in zip(heads, p_sinks, rss):
                d_sink = -jnp.sum(p_sink * rs, axis=0, keepdims=True)
                ds_ref[h] += jnp.where(lane0, d_sink, 0.0)
            dk_heads.append(sum(dks[1:], dks[0]) * (SWA_DH ** -0.5))
            dv_heads.append(sum(dvs[1:], dvs[0]))
        for b in range(SWA_Q_W // 128):
            pair = jnp.concatenate([dq_heads[2 * b], dq_heads[2 * b + 1]], axis=1)
            dq_ref[:, pl.ds(b * 128, 128)] = _unrope(pair, tc_ref[...])
        dk = _unrope(jnp.concatenate(dk_heads, axis=1), table_kk)
        dv = jnp.concatenate(dv_heads, axis=1)
        dkp_ref[...] = dk[:WINDOW]
        dkc_ref[...] = dk[WINDOW:]
        dvp_ref[...] = dv[:WINDOW]
        dvc_ref[...] = dv[WINDOW:]

    sd = jax.ShapeDtypeStruct
    f32 = jnp.float32
    half = (q.shape[0], SWA_KV_W)
    return pl.pallas_call(
        body, name="swa_bwd", grid=(q.shape[0] // WINDOW,),
        out_shape=(sd(q.shape, f32), sd(half, f32), sd(half, f32), sd(half, f32), sd(half, f32), sd(sinks.shape, f32)),
        in_specs=[qs] + kvs + tables + [sk, qs], out_specs=(qs, cur, cur, cur, cur, sk),
        compiler_params=pltpu.CompilerParams(dimension_semantics=("arbitrary",)),
    )(q, kv, kv, kv, kv, table, table, sinks, do)


@jax.custom_vjp
def _swa_op(q, kv, table, sinks):
    return _swa_fwd_call(q, kv, table, sinks)


def _swa_op_fwd(q, kv, table, sinks):
    return _swa_fwd_call(q, kv, table, sinks), (q, kv, table, sinks)


def _swa_op_bwd(res, do):
    q, kv, table, sinks = res
    dq, dkc, dkp, dvc, dvp, dsinks = _swa_bwd_call(q, kv, table, sinks, do)

    def fold(cur, prev):
        return cur + jnp.concatenate([prev[WINDOW:], jnp.zeros_like(prev[:WINDOW])], axis=0)

    return dq, jnp.concatenate([fold(dkc, dkp), fold(dvc, dvp)], axis=1), jnp.zeros_like(table), dsinks


_swa_op.defvjp(_swa_op_fwd, _swa_op_bwd)


def _swa_sink_attention(q, kv, table, sinks):
    return _swa_op(q, kv, table, jnp.broadcast_to(sinks[:, None, None], (SWA_HEADS, 1, 128)))


MEM_ROWS = 512


def _mem_probs(q_h, k_h):
    s = _dot(q_h, k_h, 1, 1) * (MEM_DH ** -0.5)
    p = jnp.exp(s - jnp.max(s, axis=-1, keepdims=True))
    return p / jnp.sum(p, axis=-1, keepdims=True)


def _mem_fwd_call(qm, kv):
    qs = pl.BlockSpec((MEM_ROWS, MEM_W), lambda i: (i, 0))
    kvs = pl.BlockSpec(kv.shape, lambda i: (0, 0))

    def body(q_ref, kv_ref, o_ref):
        for h in range(MEM_HEADS):
            cols = pl.ds(h * MEM_DH, MEM_DH)
            probs = _mem_probs(q_ref[:, cols], kv_ref[:, cols])
            o_ref[:, cols] = _dot(probs, kv_ref[:, pl.ds(MEM_W + h * MEM_DH, MEM_DH)])

    return pl.pallas_call(
        body, name="mem_fwd", grid=(qm.shape[0] // MEM_ROWS,),
        out_shape=jax.ShapeDtypeStruct(qm.shape, jnp.float32), in_specs=[qs, kvs], out_specs=qs,
        compiler_params=pltpu.CompilerParams(dimension_semantics=("parallel",)),
    )(qm, kv)


def _mem_bwd_call(qm, kv, do):
    qs = pl.BlockSpec((MEM_ROWS, MEM_W), lambda i: (i, 0))
    kvs = pl.BlockSpec(kv.shape, lambda i: (0, 0))

    def body(q_ref, kv_ref, do_ref, dq_ref, dkv_ref):
        @pl.when(pl.program_id(0) == 0)
        def _():
            dkv_ref[...] = jnp.zeros_like(dkv_ref)

        for h in range(MEM_HEADS):
            cols = pl.ds(h * MEM_DH, MEM_DH)
            v_cols = pl.ds(MEM_W + h * MEM_DH, MEM_DH)
            q_h, k_h, do_h = q_ref[:, cols], kv_ref[:, cols], do_ref[:, cols]
            probs = _mem_probs(q_h, k_h)
            dp = _dot(do_h, kv_ref[:, v_cols], 1, 1)
            d_s = probs * (dp - jnp.sum(probs * dp, axis=-1, keepdims=True))
            dq_ref[:, cols] = _dot(d_s, k_h) * (MEM_DH ** -0.5)
            dkv_ref[:, cols] += _dot(d_s, q_h, 0, 0) * (MEM_DH ** -0.5)
            dkv_ref[:, v_cols] += _dot(probs, do_h, 0, 0)

    sd = jax.ShapeDtypeStruct
    return pl.pallas_call(
        body, name="mem_bwd", grid=(qm.shape[0] // MEM_ROWS,),
        out_shape=(sd(qm.shape, jnp.float32), sd(kv.shape, jnp.float32)),
        in_specs=[qs, kvs, qs], out_specs=(qs, kvs),
        compiler_params=pltpu.CompilerParams(dimension_semantics=("arbitrary",)),
    )(qm, kv, do)


@jax.custom_vjp
def _mem_op(qm, kv):
    return _mem_fwd_call(qm, kv)


def _mem_op_fwd(qm, kv):
    return _mem_fwd_call(qm, kv), (qm, kv)


def _mem_op_bwd(res, do):
    return _mem_bwd_call(*res, do)


_mem_op.defvjp(_mem_op_fwd, _mem_op_bwd)


def _memory_attention(qm, kv):
    return _mem_op(qm[0], kv[0])[None]


def _mixer_a(h, h_lo, mem, mem_lo, p, s, layer):
    B, S, _ = h.shape
    proj = _project(h, h_lo, p["a_w_in"][layer], s["a_w_in"][layer])
    c1 = 2 * DN_QK_W + DN_V_W
    qkv = proj[..., :c1]
    z = proj[..., c1:QKVZ_W]
    qm = proj[..., QKVZ_W:QKVZ_W + MEM_W]
    a = proj[..., QKVZ_W + MEM_W:QKVZ_W + MEM_W + DN_HEADS]
    b = proj[..., QKVZ_W + MEM_W + DN_HEADS:QKVZ_W + MEM_W + 2 * DN_HEADS]
    planes = _pre_op(qkv[0], p["a_conv_w"][layer])
    beta = jax.nn.sigmoid(b[0])
    g = -jnp.exp(p["a_A_log"][layer]) * jax.nn.softplus(a[0] + p["a_dt_bias"][layer])
    o = _gate_op(_gated_delta_rule(planes, g, beta), z[0], p["a_norm_w"][layer][None])[None]
    kv = _project(mem, mem_lo, p["mem_w_kv"][layer], s["mem_w_kv"][layer])
    mo = _memory_attention(qm, kv)
    cat = jnp.concatenate([o, mo], axis=-1)
    return _project(cat, _lo(cat), p["w_o"][layer], s["w_o"][layer])


def _mixer_b(h, h_lo, mem, mem_lo, kv_shared, table, p, s, layer):
    j = layer - N_A
    proj = _project(h, h_lo, p["b_w_in"][j], s["b_w_in"][j])
    o = _swa_sink_attention(proj[0, :, :SWA_Q_W], kv_shared, table, p["b_sinks"][j])[None]
    kv = _project(mem, mem_lo, p["mem_w_kv"][layer], s["mem_w_kv"][layer])
    mo = _memory_attention(proj[..., SWA_Q_W:], kv)
    cat = jnp.concatenate([o, mo], axis=-1)
    return _project(cat, _lo(cat), p["w_o"][layer], s["w_o"][layer])


def _forward(p, s, x, mem, positions):
    table = _rope_table(positions[0], SWA_DH)
    h, h_lo, mem_lo = x, _lo(x), _lo(mem)
    kv_shared = None
    for layer in range(DEPTH):
        if layer < N_A:
            mix = _mixer_a(h, h_lo, mem, mem_lo, p, s, layer)
        else:
            mix = _mixer_b(h, h_lo, mem, mem_lo, kv_shared, table, p, s, layer)
        seq = h.shape[1]
        h2, h2_lo = _ln_res(h[0], mix[0], p["ln_g"][layer, 0][None], p["ln_b"][layer, 0][None])
        down = _mlp(h2, h2_lo, p["mlp_w_up"][layer], p["mlp_w_down"][layer], s["mlp_w_up"][layer],
                    s["mlp_w_down"][layer])
        h, h_lo = _ln_res(h2, down, p["ln_g"][layer, 1][None], p["ln_b"][layer, 1][None])
        h, h_lo = h.reshape(1, seq, D_MODEL), h_lo.reshape(1, seq, D_MODEL)
        if layer == N_A - 1:
            kv_shared = _project(h, h_lo, p["w_kv_shared"], s["w_kv_shared"])[0]
    return h


def _loss(diff, s, p, mem, positions, target):
    y = _forward({**p, **diff["small"]}, s, diff["x"], mem, positions)
    return 0.5 * jnp.sum(jnp.mean(jnp.square(y - target), axis=-1))


def _reorder_a_w_in(w):
    pad = jnp.zeros(w.shape[:-1] + (A_IN_PAD - A_IN,), w.dtype)
    return jnp.concatenate([w[..., :QKVZ_W], w[..., QKVZ_W + 2 * DN_HEADS:], w[..., QKVZ_W:QKVZ_W + 2 * DN_HEADS], pad],
                           axis=-1)


def _restore_a_w_in(w):
    return jnp.concatenate([w[..., :QKVZ_W], w[..., QKVZ_W + MEM_W:QKVZ_W + MEM_W + 2 * DN_HEADS],
                            w[..., QKVZ_W:QKVZ_W + MEM_W]], axis=-1)


def kernel(x, mem, positions, a_w_in, a_conv_w, a_A_log, a_dt_bias, a_norm_w, b_w_in, b_sinks, w_kv_shared, mem_w_kv, w_o, mlp_w_up, mlp_w_down, ln_g, ln_b, loss_target, m_a_w_in, m_a_conv_w, m_a_A_log, m_a_dt_bias, m_a_norm_w, m_b_w_in, m_b_sinks, m_w_kv_shared, m_mem_w_kv, m_w_o, m_mlp_w_up, m_mlp_w_down, m_ln_g, m_ln_b, v_a_w_in, v_a_conv_w, v_a_A_log, v_a_dt_bias, v_a_norm_w, v_b_w_in, v_b_sinks, v_w_kv_shared, v_mem_w_kv, v_w_o, v_mlp_w_up, v_mlp_w_down, v_ln_g, v_ln_b):
    w_sh = dict(a_w_in=a_w_in, a_conv_w=a_conv_w, a_A_log=a_A_log, a_dt_bias=a_dt_bias, a_norm_w=a_norm_w,
                b_w_in=b_w_in, b_sinks=b_sinks, w_kv_shared=w_kv_shared, mem_w_kv=mem_w_kv, w_o=w_o,
                mlp_w_up=mlp_w_up, mlp_w_down=mlp_w_down, ln_g=ln_g, ln_b=ln_b)
    m_sh = dict(a_w_in=m_a_w_in, a_conv_w=m_a_conv_w, a_A_log=m_a_A_log, a_dt_bias=m_a_dt_bias, a_norm_w=m_a_norm_w,
                b_w_in=m_b_w_in, b_sinks=m_b_sinks, w_kv_shared=m_w_kv_shared, mem_w_kv=m_mem_w_kv, w_o=m_w_o,
                mlp_w_up=m_mlp_w_up, mlp_w_down=m_mlp_w_down, ln_g=m_ln_g, ln_b=m_ln_b)
    v_sh = dict(a_w_in=v_a_w_in, a_conv_w=v_a_conv_w, a_A_log=v_a_A_log, a_dt_bias=v_a_dt_bias, a_norm_w=v_a_norm_w,
                b_w_in=v_b_w_in, b_sinks=v_b_sinks, w_kv_shared=v_w_kv_shared, mem_w_kv=v_mem_w_kv, w_o=v_w_o,
                mlp_w_up=v_mlp_w_up, mlp_w_down=v_mlp_w_down, ln_g=v_ln_g, ln_b=v_ln_b)
    shard_shapes = {n: w_sh[n].shape for n in WEIGHTS}
    rb, rows = _rows_for(w_sh)

    big, small = _pack(w_sh, rb, jnp.bfloat16)
    gbig, gsmall = _gather_weights(big.reshape(2, rb // 2, FLAT_W), small.reshape(2, SMALL_ROWS // 2, FLAT_W))
    gbig, gsmall = gbig.reshape(N_CHIPS, rb, FLAT_W), gsmall.reshape(N_CHIPS, SMALL_ROWS, FLAT_W)
    pieces = [_unpack(gbig[q], gsmall[q], shard_shapes) for q in range(N_CHIPS)]
    full = {n: jnp.concatenate([pieces[q][n] for q in range(N_CHIPS)], axis=SHARD_AXIS[n]) for n in SHARD_AXIS}
    for n in REPLICATED:
        full[n] = w_sh[n]
    big_w = {n: full[n] for n in BIG}
    big_w["a_w_in"] = _reorder_a_w_in(big_w["a_w_in"])
    small_w = {n: full[n] for n in SMALL}
    slots = {n: jnp.zeros(big_w[n].shape, jnp.float32) for n in BIG}

    loss, (grads, g_slots) = jax.value_and_grad(_loss, argnums=(0, 1))(
        {"x": x, "small": small_w}, slots, big_w, mem, positions, loss_target)
    loss = lax.psum(loss, ("x", "y", "c"))
    g_full = {**g_slots, **grads["small"]}
    g_full["a_w_in"] = _restore_a_w_in(g_full["a_w_in"])

    def shard_of(n, q):
        if n in REPLICATED:
            return g_full[n]
        size = shard_shapes[n][SHARD_AXIS[n]]
        return lax.slice_in_dim(g_full[n], q * size, (q + 1) * size, axis=SHARD_AXIS[n])

    parts = []
    for q in range(N_CHIPS):
        pb, ps = _pack({n: shard_of(n, q) for n in WEIGHTS}, rb, jnp.bfloat16)
        parts.append(jnp.concatenate([pb, ps.astype(jnp.bfloat16)], axis=0).reshape(2, rows // 2, FLAT_W))
    partials = jnp.stack(parts, axis=1)
    half = lax.axis_index("c").astype(jnp.int32).reshape(1)
    chip_partials = _add_pairs(partials, _swap_halves(partials), half)
    g_flat = _join_halves(_sum_chips(_scatter_grads(chip_partials))).reshape(rows, FLAT_W)

    flat = [jnp.concatenate(_pack(d, rb), axis=0) for d in (w_sh, m_sh, v_sh)]
    outs = (g_flat,) + tuple(_adamw(g_flat, *flat))
    g_o, d_o, m_o, v_o = [_unpack(o[:rb], o[rb:], shard_shapes) for o in outs]
    return (loss, grads["x"], *[g_o[n] for n in WEIGHTS], *[d_o[n] for n in WEIGHTS],
            *[m_o[n] for n in WEIGHTS], *[v_o[n] for n in WEIGHTS])
```

```python
import functools
import math

import jax
import jax.numpy as jnp
from jax import lax
from jax.experimental import pallas as pl
from jax.experimental.pallas import tpu as pltpu

D_MODEL = 1024
DEPTH = 4
N_A = DEPTH // 2
MEM_HEADS = 4
MEM_DH = D_MODEL // 16
MEM_W = MEM_HEADS * MEM_DH
DN_DK = 128
DN_DV = 128
DN_HEADS = (3 * D_MODEL) // (4 * DN_DV)
DN_QK_W = DN_HEADS * DN_DK
DN_V_W = DN_HEADS * DN_DV
CONV_WIDTH = 4
CHUNK = 64
SWA_DH = 64
SWA_HEADS = (3 * D_MODEL) // (4 * SWA_DH)
SWA_KV_HEADS = 2
SWA_GROUP = SWA_HEADS // SWA_KV_HEADS
SWA_Q_W = SWA_HEADS * SWA_DH
SWA_KV_W = SWA_KV_HEADS * SWA_DH
WINDOW = 128
ROPE_THETA = 10000.0
LN_EPS = 1e-5
NORM_EPS = 1e-6
DN_ALPHA = (2.0 * DEPTH) ** 0.25
A_IN = 2 * DN_QK_W + 2 * DN_V_W + 2 * DN_HEADS + MEM_W
A_IN_PAD = 3456
QKVZ_W = 2 * DN_QK_W + 2 * DN_V_W

ADAM_LR = 0.001
ADAM_B1 = 0.9
ADAM_B2 = 0.999
ADAM_EPS = 1e-08
ADAM_WD = 0.01
ADAM_STEP = 10

N_CHIPS = 4
FLAT_W = 1024
BIG = ("a_w_in", "b_w_in", "w_kv_shared", "mem_w_kv", "w_o", "mlp_w_up", "mlp_w_down")
SMALL = ("a_conv_w", "ln_g", "ln_b", "a_A_log", "a_dt_bias", "a_norm_w", "b_sinks")
REPLICATED = ("a_A_log", "a_dt_bias", "a_norm_w", "b_sinks")
WEIGHTS = ("a_w_in", "a_conv_w", "a_A_log", "a_dt_bias", "a_norm_w", "b_w_in", "b_sinks", "w_kv_shared",
           "mem_w_kv", "w_o", "mlp_w_up", "mlp_w_down", "ln_g", "ln_b")
SHARD_AXIS = {"a_w_in": 2, "a_conv_w": 2, "b_w_in": 1, "w_kv_shared": 0, "mem_w_kv": 1, "w_o": 1,
              "mlp_w_up": 2, "mlp_w_down": 1, "ln_g": 2, "ln_b": 2}
SMALL_ROWS = 32
ROW_ALIGN = 256

MESH = pl.DeviceIdType.MESH
HBM_SPEC = pl.BlockSpec(memory_space=pltpu.HBM)
VMEM_LIMIT = 48 * 1024 * 1024


def _rows_for(shards):
    n_big = sum(math.prod(shards[n].shape) for n in BIG)
    n_small = sum(math.prod(shards[n].shape) for n in SMALL)
    assert n_small <= SMALL_ROWS * FLAT_W
    total = -(-n_big // FLAT_W) + SMALL_ROWS
    total = -(-total // (2 * ROW_ALIGN)) * (2 * ROW_ALIGN)
    return total - SMALL_ROWS, total


def _pack(shards, rb, dtype_big=jnp.float32):
    big = jnp.concatenate([shards[n].reshape(-1).astype(dtype_big) for n in BIG])
    big = jnp.pad(big, (0, rb * FLAT_W - big.shape[0])).reshape(rb, FLAT_W)
    small = jnp.concatenate([shards[n].reshape(-1).astype(jnp.float32) for n in SMALL])
    small = jnp.pad(small, (0, SMALL_ROWS * FLAT_W - small.shape[0])).reshape(SMALL_ROWS, FLAT_W)
    return big, small


def _unpack(big, small, shapes):
    out = {}
    for flat, names in ((big.reshape(-1), BIG), (small.reshape(-1), SMALL)):
        off = 0
        for n in names:
            size = math.prod(shapes[n])
            out[n] = flat[off:off + size].reshape(shapes[n])
            off += size
    return out


def _other_chips(x, y):
    return [(1 - x, y), (x, 1 - y), (1 - x, 1 - y)]


def _gather_weights(big, small):
    def body(big_ref, small_ref, init_big_ref, init_small_ref, obig_ref, osmall_ref,
             send_sems, recv_sems, pass_send_sems, pass_recv_sems):
        del init_big_ref, init_small_ref
        x, y, c = lax.axis_index("x"), lax.axis_index("y"), lax.axis_index("c")
        me = 2 * x + y
        sibling = (x, y, 1 - c)
        pairs = ((big_ref, obig_ref), (small_ref, osmall_ref))
        sends = []
        for j, (px, py) in enumerate(_other_chips(x, y)):
            for i, (src, dst) in enumerate(pairs):
                sends.append(pltpu.make_async_remote_copy(
                    src_ref=src.at[c], dst_ref=dst.at[me, c], send_sem=send_sems.at[2 * j + i],
                    recv_sem=recv_sems.at[2 * j + i], device_id=(px, py, c), device_id_type=MESH))
        for cp in sends:
            cp.start()
        passed = []
        for j, (px, py) in enumerate(_other_chips(x, y)):
            for i, (src, dst) in enumerate(pairs):
                landed = dst.at[2 * px + py, c]
                pltpu.make_async_remote_copy(
                    src_ref=src.at[c], dst_ref=landed, send_sem=send_sems.at[2 * j + i],
                    recv_sem=recv_sems.at[2 * j + i], device_id=(px, py, c), device_id_type=MESH).wait_recv()
                passed.append(pltpu.make_async_remote_copy(
                    src_ref=landed, dst_ref=landed, send_sem=pass_send_sems.at[2 * j + i],
                    recv_sem=pass_recv_sems.at[2 * j + i], device_id=sibling, device_id_type=MESH))
                passed[-1].start()
        for j, (px, py) in enumerate(_other_chips(x, y)):
            for i, (src, dst) in enumerate(pairs):
                other_half = dst.at[2 * px + py, 1 - c]
                pltpu.make_async_remote_copy(
                    src_ref=other_half, dst_ref=other_half, send_sem=pass_send_sems.at[2 * j + i],
                    recv_sem=pass_recv_sems.at[2 * j + i], device_id=sibling, device_id_type=MESH).wait_recv()
        for cp in sends + passed:
            cp.wait_send()

    dma6 = pltpu.SemaphoreType.DMA((6,))
    four = lambda t: jnp.broadcast_to(t[None], (N_CHIPS,) + t.shape)
    return pl.pallas_call(
        body, name="gather_weights",
        out_shape=(jax.ShapeDtypeStruct((N_CHIPS,) + big.shape, big.dtype),
                   jax.ShapeDtypeStruct((N_CHIPS,) + small.shape, small.dtype)),
        in_specs=[HBM_SPEC] * 4, out_specs=(HBM_SPEC, HBM_SPEC), input_output_aliases={2: 0, 3: 1},
        scratch_shapes=[dma6, dma6, dma6, dma6],
    )(big, small, four(big), four(small))


def _scatter_grads(g):
    def body(g_ref, o_ref, send_sems, recv_sems, local_sem):
        x, y, c = lax.axis_index("x"), lax.axis_index("y"), lax.axis_index("c")
        me = 2 * x + y
        local = pltpu.make_async_copy(g_ref.at[me], o_ref.at[me], local_sem)
        local.start()
        sends = []
        for j, (px, py) in enumerate(_other_chips(x, y)):
            sends.append(pltpu.make_async_remote_copy(
                src_ref=g_ref.at[2 * px + py], dst_ref=o_ref.at[me], send_sem=send_sems.at[j], recv_sem=recv_sems.at[j],
                device_id=(px, py, c), device_id_type=MESH))
        for cp in sends:
            cp.start()
        for j, (px, py) in enumerate(_other_chips(x, y)):
            pltpu.make_async_remote_copy(
                src_ref=g_ref.at[me], dst_ref=o_ref.at[2 * px + py], send_sem=send_sems.at[j], recv_sem=recv_sems.at[j],
                device_id=(px, py, c), device_id_type=MESH).wait_recv()
        for cp in sends:
            cp.wait_send()
        local.wait()

    return pl.pallas_call(
        body, name="scatter_grads",
        out_shape=jax.ShapeDtypeStruct(g.shape, g.dtype),
        in_specs=[HBM_SPEC], out_specs=HBM_SPEC,
        scratch_shapes=[pltpu.SemaphoreType.DMA((3,)), pltpu.SemaphoreType.DMA((3,)), pltpu.SemaphoreType.DMA],
    )(g)


def _swap_halves(g):
    def body(g_ref, o_ref, send_sem, recv_sem):
        x, y, c = lax.axis_index("x"), lax.axis_index("y"), lax.axis_index("c")
        cp = pltpu.make_async_remote_copy(src_ref=g_ref.at[1 - c], dst_ref=o_ref, send_sem=send_sem, recv_sem=recv_sem,
                                          device_id=(x, y, 1 - c), device_id_type=MESH)
        cp.start()
        cp.wait()

    return pl.pallas_call(
        body, name="swap_halves",
        out_shape=jax.ShapeDtypeStruct(g.shape[1:], g.dtype),
        in_specs=[HBM_SPEC], out_specs=HBM_SPEC,
        scratch_shapes=[pltpu.SemaphoreType.DMA, pltpu.SemaphoreType.DMA],
    )(g)


def _join_halves(v):
    def body(v_ref, init_ref, o_ref, send_sem, recv_sem):
        del init_ref
        x, y, c = lax.axis_index("x"), lax.axis_index("y"), lax.axis_index("c")
        cp = pltpu.make_async_remote_copy(src_ref=v_ref, dst_ref=o_ref.at[c], send_sem=send_sem, recv_sem=recv_sem,
                                          device_id=(x, y, 1 - c), device_id_type=MESH)
        cp.start()
        cp.wait_send()
        pltpu.make_async_remote_copy(src_ref=v_ref, dst_ref=o_ref.at[1 - c], send_sem=send_sem, recv_sem=recv_sem,
                                     device_id=(x, y, 1 - c), device_id_type=MESH).wait_recv()

    return pl.pallas_call(
        body, name="join_halves",
        out_shape=jax.ShapeDtypeStruct((2,) + v.shape, v.dtype),
        in_specs=[HBM_SPEC, HBM_SPEC], out_specs=HBM_SPEC, input_output_aliases={1: 0},
        scratch_shapes=[pltpu.SemaphoreType.DMA, pltpu.SemaphoreType.DMA],
    )(v, jnp.stack([v, v]))


def _add_pairs(g, theirs, half):
    _, n, rows, width = g.shape
    assert rows % ROW_ALIGN == 0, rows

    def body(half_ref, g_ref, t_ref, o_ref):
        o_ref[...] = (g_ref[...].astype(jnp.float32) + t_ref[...].astype(jnp.float32)).astype(o_ref.dtype)

    blk = pl.BlockSpec((None, ROW_ALIGN, width), lambda p, i, h: (p, i, 0))
    grid_spec = pltpu.PrefetchScalarGridSpec(
        num_scalar_prefetch=1, grid=(n, rows // ROW_ALIGN),
        in_specs=[pl.BlockSpec((None, None, ROW_ALIGN, width), lambda p, i, h: (h[0], p, i, 0)), blk], out_specs=blk)
    return pl.pallas_call(
        body, name="add_pairs", grid_spec=grid_spec, out_shape=jax.ShapeDtypeStruct(theirs.shape, g.dtype),
        compiler_params=pltpu.CompilerParams(dimension_semantics=("parallel", "parallel")),
    )(half, g, theirs)


def _sum_chips(parts):
    n, rows, width = parts.shape
    assert rows % ROW_ALIGN == 0, rows

    def body(p_ref, o_ref):
        p = [p_ref[q].astype(jnp.float32) for q in range(n)]
        o_ref[...] = (p[0] + p[1]) + (p[2] + p[3])

    return pl.pallas_call(
        body, name="sum_chips", grid=(rows // ROW_ALIGN,),
        out_shape=jax.ShapeDtypeStruct((rows, width), jnp.float32),
        in_specs=[pl.BlockSpec((n, ROW_ALIGN, width), lambda i: (0, i, 0))],
        out_specs=pl.BlockSpec((ROW_ALIGN, width), lambda i: (i, 0)),
        compiler_params=pltpu.CompilerParams(dimension_semantics=("parallel",), vmem_limit_bytes=VMEM_LIMIT),
    )(parts)


def _adamw(g, w, m, v):
    rows, width = w.shape
    blk = ROW_ALIGN // 2

    def body(g_ref, w_ref, m_ref, v_ref, d_out, m_out, v_out):
        g = g_ref[...]
        m_new = ADAM_B1 * m_ref[...] + (1.0 - ADAM_B1) * g
        v_new = ADAM_B2 * v_ref[...] + (1.0 - ADAM_B2) * jnp.square(g)
        m_hat = m_new / (1.0 - ADAM_B1 ** ADAM_STEP)
        v_hat = v_new / (1.0 - ADAM_B2 ** ADAM_STEP)
        d_out[...] = -ADAM_LR * (m_hat / (jnp.sqrt(v_hat) + ADAM_EPS) + ADAM_WD * w_ref[...])
        m_out[...] = m_new
        v_out[...] = v_new

    spec = pl.BlockSpec((blk, width), lambda i: (i, 0))
    shape = jax.ShapeDtypeStruct((rows, width), jnp.float32)
    return pl.pallas_call(
        body, name="adamw", grid=(rows // blk,),
        out_shape=(shape,) * 3, in_specs=[spec] * 4, out_specs=(spec,) * 3,
        compiler_params=pltpu.CompilerParams(dimension_semantics=("parallel",), vmem_limit_bytes=VMEM_LIMIT),
    )(g, w, m, v)


def _tile(dim, pref):
    if dim <= pref:
        return dim
    for t in range(pref - pref % 128, 0, -128):
        if dim % t == 0:
            return t
    raise ValueError(f"no 128-aligned tile for {dim}")


def _matmul(a, b, *, ta=False, tb=False, name, epilogue=None, extra=None, out_dtype=jnp.float32):
    (k_a, m) = a.shape if ta else a.shape[::-1]
    (k_b, n) = b.shape[::-1] if tb else b.shape
    assert k_a == k_b, (a.shape, b.shape, ta, tb)
    k = k_a
    tk = _tile(k, 1152)
    nk = k // tk
    if ta:
        tm, tn = _tile(m, 1024), _tile(n, 2048 if m <= 1024 else 1024)
    else:
        tm, tn = _tile(m, 2048), _tile(n, 512 if nk == 1 else 1024)
    a_spec = pl.BlockSpec((tk, tm), lambda i, j, l: (l, i)) if ta else pl.BlockSpec((tm, tk), lambda i, j, l: (i, l))
    b_spec = pl.BlockSpec((tn, tk), lambda i, j, l: (j, l)) if tb else pl.BlockSpec((tk, tn), lambda i, j, l: (l, j))
    o_spec = pl.BlockSpec((tm, tn), lambda i, j, l: (i, j))
    dims = (((0 if ta else 1,), (1 if tb else 0,)), ((), ()))
    has_extra = epilogue == "relu2_grad"
    assert has_extra == (extra is not None)

    def body(*refs):
        a_ref, b_ref = refs[:2]
        outs = refs[2 + has_extra:2 + has_extra + (2 if epilogue == "relu2" else 1)]
        l = pl.program_id(2)
        part = lax.dot_general(a_ref[...].astype(jnp.bfloat16), b_ref[...].astype(jnp.bfloat16), dims,
                               preferred_element_type=jnp.float32)

        def finish(acc):
            if epilogue is None:
                outs[0][...] = acc.astype(out_dtype)
            elif epilogue == "relu2":
                outs[0][...] = acc.astype(jnp.bfloat16)
                outs[1][...] = jnp.square(jnp.maximum(acc, 0.0)).astype(jnp.bfloat16)
            else:
                outs[0][...] = (acc * (2.0 * jnp.maximum(refs[2][...].astype(jnp.float32), 0.0))).astype(out_dtype)

        if nk == 1:
            finish(part)
            return
        acc_ref = refs[-1]

        @pl.when(l == 0)
        def _():
            acc_ref[...] = part

        @pl.when((l > 0) & (l < nk - 1))
        def _():
            acc_ref[...] += part

        @pl.when(l == nk - 1)
        def _():
            finish(acc_ref[...] + part)

    if epilogue == "relu2":
        out_shape = (jax.ShapeDtypeStruct((m, n), jnp.bfloat16),) * 2
        out_specs = (o_spec, o_spec)
    else:
        out_shape = jax.ShapeDtypeStruct((m, n), out_dtype)
        out_specs = o_spec
    return pl.pallas_call(
        body, name=name, grid=(m // tm, n // tn, nk), out_shape=out_shape,
        in_specs=[a_spec, b_spec] + ([o_spec] if has_extra else []), out_specs=out_specs,
        scratch_shapes=[pltpu.VMEM((tm, tn), jnp.float32)] if nk > 1 else [],
        compiler_params=pltpu.CompilerParams(dimension_semantics=("parallel", "parallel", "arbitrary"),
                                             vmem_limit_bytes=VMEM_LIMIT),
    )(*((a, b) + ((extra,) if has_extra else ())))


def _lo(x):
    return lax.stop_gradient(x.astype(jnp.bfloat16))


@jax.custom_vjp
def _linear(x, x_lo, w, slot):
    del x, slot
    return _matmul(x_lo, w, name="linear_fwd")


def _linear_fwd(x, x_lo, w, slot):
    del x, slot
    return _matmul(x_lo, w, name="linear_fwd"), (x_lo, w)


def _linear_bwd(res, dy):
    x_lo, w = res
    dy = dy.astype(jnp.bfloat16)
    dx = _matmul(dy, w, tb=True, name="linear_dx")
    dw = _matmul(x_lo, dy, ta=True, name="linear_dw")
    return dx, jnp.zeros_like(x_lo), jnp.zeros_like(w), dw


_linear.defvjp(_linear_fwd, _linear_bwd)


@jax.custom_vjp
def _mlp(h, h_lo, w_up, w_down, slot_up, slot_down):
    return _mlp_fwd(h, h_lo, w_up, w_down, slot_up, slot_down)[0]


def _mlp_fwd(h, h_lo, w_up, w_down, slot_up, slot_down):
    del h, slot_up, slot_down
    up, act = _matmul(h_lo, w_up, name="mlp_up", epilogue="relu2")
    return _matmul(act, w_down, name="mlp_down"), (h_lo, up, act, w_up, w_down)


def _mlp_bwd(res, dy):
    h_lo, up, act, w_up, w_down = res
    dy = dy.astype(jnp.bfloat16)
    d_up = _matmul(dy, w_down, tb=True, name="mlp_d_up", epilogue="relu2_grad", extra=up, out_dtype=jnp.bfloat16)
    dw_down = _matmul(act, dy, ta=True, name="mlp_dw_down")
    dw_up = _matmul(h_lo, d_up, ta=True, name="mlp_dw_up")
    dh = _matmul(d_up, w_up, tb=True, name="mlp_dh")
    return dh, jnp.zeros_like(h_lo), jnp.zeros_like(w_up), jnp.zeros_like(w_down), dw_up, dw_down


_mlp.defvjp(_mlp_fwd, _mlp_bwd)


LN_ROWS = 256


def _ln_call(h, mix, g, b):
    s, d = h.shape
    tok = pl.BlockSpec((LN_ROWS, d), lambda i: (i, 0))
    vec = pl.BlockSpec((1, d), lambda i: (0, 0))
    stat = pl.BlockSpec((LN_ROWS, 1), lambda i: (i, 0))

    def body(h_ref, mix_ref, g_ref, b_ref, y_ref, ylo_ref, xhat_ref, rstd_ref):
        z = DN_ALPHA * h_ref[...] + mix_ref[...]
        mu = jnp.mean(z, axis=-1, keepdims=True)
        zc = z - mu
        rstd = lax.rsqrt(jnp.mean(jnp.square(zc), axis=-1, keepdims=True) + LN_EPS)
        xhat = zc * rstd
        y = xhat * g_ref[...] + b_ref[...]
        y_ref[...] = y
        ylo_ref[...] = y.astype(ylo_ref.dtype)
        xhat_ref[...] = xhat
        rstd_ref[...] = rstd

    sd = jax.ShapeDtypeStruct
    return pl.pallas_call(
        body, name="ln_fwd", grid=(s // LN_ROWS,),
        out_shape=(sd((s, d), jnp.float32), sd((s, d), jnp.bfloat16), sd((s, d), jnp.float32), sd((s, 1), jnp.float32)),
        in_specs=[tok, tok, vec, vec], out_specs=(tok, tok, tok, stat),
        compiler_params=pltpu.CompilerParams(dimension_semantics=("parallel",)),
    )(h, mix, g, b)


def _ln_grad_call(dy, xhat, rstd, g):
    s, d = dy.shape
    tok = pl.BlockSpec((LN_ROWS, d), lambda i: (i, 0))
    vec = pl.BlockSpec((1, d), lambda i: (0, 0))
    stat = pl.BlockSpec((LN_ROWS, 1), lambda i: (i, 0))

    def body(dy_ref, xhat_ref, rstd_ref, g_ref, dz_ref, dg_ref, db_ref):
        @pl.when(pl.program_id(0) == 0)
        def _():
            dg_ref[...] = jnp.zeros_like(dg_ref)
            db_ref[...] = jnp.zeros_like(db_ref)

        dy, xhat = dy_ref[...], xhat_ref[...]
        dyg = dy * g_ref[...]
        m1 = jnp.mean(dyg, axis=-1, keepdims=True)
        m2 = jnp.mean(dyg * xhat, axis=-1, keepdims=True)
        dz_ref[...] = rstd_ref[...] * (dyg - m1 - xhat * m2)
        dg_ref[...] += jnp.sum(dy * xhat, axis=0, keepdims=True)
        db_ref[...] += jnp.sum(dy, axis=0, keepdims=True)

    sd = jax.ShapeDtypeStruct
    return pl.pallas_call(
        body, name="ln_bwd", grid=(s // LN_ROWS,),
        out_shape=(sd((s, d), jnp.float32), sd((1, d), jnp.float32), sd((1, d), jnp.float32)),
        in_specs=[tok, tok, stat, vec], out_specs=(tok, vec, vec),
        compiler_params=pltpu.CompilerParams(dimension_semantics=("arbitrary",)),
    )(dy, xhat, rstd, g)


@jax.custom_vjp
def _ln_res(h, mix, g, b):
    return _ln_call(h, mix, g, b)[:2]


def _ln_res_fwd(h, mix, g, b):
    y, y_lo, xhat, rstd = _ln_call(h, mix, g, b)
    return (y, y_lo), (xhat, rstd, g)


def _ln_res_bwd(res, cts):
    xhat, rstd, g = res
    dz, dg, db = _ln_grad_call(cts[0], xhat, rstd, g)
    return DN_ALPHA * dz, dz, dg, db


_ln_res.defvjp(_ln_res_fwd, _ln_res_bwd)


MXU_DTYPE = jnp.bfloat16
DN_CB = 16
DN_GROUP = 8
DN_SCAN_CB = 4
DN_SCALE = DN_DK ** -0.5


def _dot(a, b, ca=1, cb=0):
    return lax.dot_general(a.astype(MXU_DTYPE), b.astype(MXU_DTYPE), (((ca,), (cb,)), ((), ())),
                           preferred_element_type=jnp.float32)


def _chunk_masks():
    row = lax.broadcasted_iota(jnp.int32, (CHUNK, CHUNK), 0)
    col = lax.broadcasted_iota(jnp.int32, (CHUNK, CHUNK), 1)
    return row >= col, row > col, row == col


def _to_col(row_vec):
    _, _, eye = _chunk_masks()
    return jnp.sum(jnp.where(eye, jnp.broadcast_to(row_vec, (CHUNK, CHUNK)), 0.0), axis=1, keepdims=True)


def _to_row(col_vec):
    _, _, eye = _chunk_masks()
    return jnp.sum(jnp.where(eye, jnp.broadcast_to(col_vec, (CHUNK, CHUNK)), 0.0), axis=0, keepdims=True)


def _last_row(col_vec):
    last = lax.broadcasted_iota(jnp.int32, (CHUNK, 1), 0) == CHUNK - 1
    return jnp.sum(jnp.where(last, col_vec, 0.0), axis=0, keepdims=True), last


def _chunk_terms(q, k, beta, gcc, gcr):
    incl, strict, _ = _chunk_masks()
    decay = jnp.where(incl, jnp.exp(jnp.minimum(gcc - gcr, 0.0)), 0.0)
    kb = k * beta
    lmat = jnp.where(strict, _dot(kb, k, 1, 1) * decay, 0.0)
    intra = jnp.where(incl, _dot(q, k, 1, 1) * decay, 0.0)
    return decay, kb, lmat, intra


def _dot3(a, b, ca=1, cb=0):
    if MXU_DTYPE == jnp.float32:
        return _dot(a, b, ca, cb)
    a_hi, b_hi = a.astype(MXU_DTYPE), b.astype(MXU_DTYPE)
    a_lo = (a - a_hi.astype(jnp.float32)).astype(MXU_DTYPE)
    b_lo = (b - b_hi.astype(jnp.float32)).astype(MXU_DTYPE)
    return _dot(a_hi, b_hi, ca, cb) + (_dot(a_hi, b_lo, ca, cb) + _dot(a_lo, b_hi, ca, cb))


def _unit_lower_inverse(lmats):
    _, _, eye = _chunk_masks()
    ident = jnp.where(eye, 1.0, 0.0)
    ts = [ident - m for m in lmats]
    ps = [_dot(m, m) for m in lmats]
    for _ in range(4):
        ts = [t + _dot(t, p) for t, p in zip(ts, ps)]
        ps = [_dot(p, p) for p in ps]
    ts = [t + _dot(t, p) for t, p in zip(ts, ps)]
    resids = [(t - ident) + _dot3(m, t) for m, t in zip(lmats, ts)]
    return [t - _dot(t, r) for t, r in zip(ts, resids)]


def _dn_specs(n_chunks):
    tok = pl.BlockSpec((DN_CB * CHUNK, DN_DK), lambda h, n: (n, h))
    rowv = pl.BlockSpec((None, DN_CB, CHUNK), lambda h, n: (h, n, 0))
    sq = pl.BlockSpec((None, DN_CB, CHUNK, CHUNK), lambda h, n: (h, n, 0, 0))
    lane = pl.BlockSpec((None, DN_CB, 1, DN_DV), lambda h, n: (h, n, 0, 0))
    planes = [pl.BlockSpec((None, DN_CB * CHUNK, DN_DK), functools.partial(lambda h, n, p: (p, n, h), p=p))
              for p in range(3)]
    return tok, rowv, sq, lane, planes


def _dn_prep(qkv, beta, gc):
    s = qkv.shape[1]
    n_chunks = s // CHUNK
    tok, rowv, sq, lane, planes = _dn_specs(n_chunks)
    tok_shape = qkv.shape[1:]

    def body(q_ref, k_ref, v_ref, beta_ref, gc_ref, u_ref, w_ref, qd_ref, kd_ref, intra_ref, t_ref, cd_ref):
        for c0 in range(0, DN_CB, DN_GROUP):
            chunks = range(c0, c0 + DN_GROUP)
            rhs, lmats = [], []
            for c in chunks:
                rows = pl.ds(c * CHUNK, CHUNK)
                q_c, k_c, v_c = q_ref[rows, :] * DN_SCALE, k_ref[rows, :], v_ref[rows, :]
                gcr_c = gc_ref[pl.ds(c, 1), :]
                beta_c, gcc_c = _to_col(beta_ref[pl.ds(c, 1), :]), _to_col(gcr_c)
                _, kb, lmat, intra = _chunk_terms(q_c, k_c, beta_c, gcc_c, gcr_c)
                eg = jnp.exp(gcc_c)
                g_last, _ = _last_row(gcc_c)
                qd_ref[rows, :] = (q_c * eg).astype(qd_ref.dtype)
                kd_ref[rows, :] = (k_c * jnp.exp(g_last - gcc_c)).astype(kd_ref.dtype)
                intra_ref[c] = intra.astype(intra_ref.dtype)
                cd_ref[c] = jnp.broadcast_to(jnp.exp(g_last), (1, DN_DV))
                rhs.append(jnp.concatenate([v_c * beta_c, kb * eg], axis=1))
                lmats.append(lmat)
            ts = _unit_lower_inverse(lmats)
            sols = [_dot3(t, r) for t, r in zip(ts, rhs)]
            for c, t, sol in zip(chunks, ts, sols):
                rows = pl.ds(c * CHUNK, CHUNK)
                t_ref[c] = t
                u_ref[rows, :] = sol[:, :DN_DV]
                w_ref[rows, :] = sol[:, DN_DV:].astype(w_ref.dtype)

    f32, mx = jnp.float32, MXU_DTYPE
    sd = jax.ShapeDtypeStruct
    return pl.pallas_call(
        body, name="dn_prep", grid=(DN_HEADS, n_chunks // DN_CB),
        out_shape=(sd(tok_shape, f32), sd(tok_shape, mx), sd(tok_shape, mx), sd(tok_shape, mx),
                   sd((DN_HEADS, n_chunks, CHUNK, CHUNK), mx), sd((DN_HEADS, n_chunks, CHUNK, CHUNK), f32),
                   sd((DN_HEADS, n_chunks, 1, DN_DV), f32)),
        in_specs=planes + [rowv, rowv], out_specs=(tok, tok, tok, tok, sq, sq, lane),
        compiler_params=pltpu.CompilerParams(dimension_semantics=("parallel", "parallel")),
    )(qkv, qkv, qkv, beta, gc)


def _dn_scan(u, w, qd, kd, intra, cd):
    s, width = u.shape
    n_chunks = s // CHUNK
    cb = DN_SCAN_CB
    tok = pl.BlockSpec((cb * CHUNK, width), lambda n: (n, 0))
    sq = pl.BlockSpec((DN_HEADS, cb, CHUNK, CHUNK), lambda n: (0, n, 0, 0))
    lane = pl.BlockSpec((DN_HEADS, cb, 1, DN_DV), lambda n: (0, n, 0, 0))
    st = pl.BlockSpec((DN_HEADS, cb, DN_DK, DN_DV), lambda n: (0, n, 0, 0))

    def body(u_ref, w_ref, qd_ref, kd_ref, intra_ref, cd_ref, o_ref, vn_ref, st_ref, state):
        @pl.when(pl.program_id(0) == 0)
        def _():
            state[...] = jnp.zeros_like(state)

        heads = range(DN_HEADS)
        cols = [pl.ds(h * DN_DK, DN_DK) for h in heads]
        s_f = [state[h] for h in heads]
        for c in range(cb):
            rows = pl.ds(c * CHUNK, CHUNK)
            s_mx = [s.astype(MXU_DTYPE) for s in s_f]
            for h in heads:
                st_ref[h, c] = s_mx[h]
            ws = [_dot(w_ref[rows, cols[h]], s_mx[h]) for h in heads]
            qs = [_dot(qd_ref[rows, cols[h]], s_mx[h]) for h in heads]
            v_new = [(u_ref[rows, cols[h]] - ws[h]).astype(MXU_DTYPE) for h in heads]
            inner = [_dot(intra_ref[h, c], v_new[h]) for h in heads]
            outer = [_dot(kd_ref[rows, cols[h]], v_new[h], 0, 0) for h in heads]
            for h in heads:
                vn_ref[rows, cols[h]] = v_new[h]
                o_ref[rows, cols[h]] = qs[h] + inner[h]
            s_f = [s_f[h] * cd_ref[h, c] + outer[h] for h in heads]
        for h in heads:
            state[h] = s_f[h]

    sd = jax.ShapeDtypeStruct
    return pl.pallas_call(
        body, name="dn_scan", grid=(n_chunks // cb,),
        out_shape=(sd(u.shape, jnp.float32), sd(u.shape, MXU_DTYPE),
                   sd((DN_HEADS, n_chunks, DN_DK, DN_DV), MXU_DTYPE)),
        in_specs=[tok, tok, tok, tok, sq, lane], out_specs=(tok, tok, st),
        scratch_shapes=[pltpu.VMEM((DN_HEADS, DN_DK, DN_DV), jnp.float32)],
        compiler_params=pltpu.CompilerParams(dimension_semantics=("arbitrary",)),
    )(u, w, qd, kd, intra, cd)


def _dn_bwd_scan(do, w, qd, kd, intra, cd, vn, st):
    s, width = do.shape
    n_chunks = s // CHUNK
    cb = DN_SCAN_CB
    last = n_chunks // cb - 1
    tok = pl.BlockSpec((cb * CHUNK, width), lambda n: (last - n, 0))
    sq = pl.BlockSpec((DN_HEADS, cb, CHUNK, CHUNK), lambda n: (0, last - n, 0, 0))
    lane = pl.BlockSpec((DN_HEADS, cb, 1, DN_DV), lambda n: (0, last - n, 0, 0))
    stt = pl.BlockSpec((DN_HEADS, cb, DN_DK, DN_DV), lambda n: (0, last - n, 0, 0))

    def body(do_ref, w_ref, qd_ref, kd_ref, intra_ref, cd_ref, vn_ref, st_ref,
             du_ref, dw_ref, dqd_ref, dkd_ref, dintra_ref, dgl_ref, dstate):
        @pl.when(pl.program_id(0) == 0)
        def _():
            dstate[...] = jnp.zeros_like(dstate)

        heads = range(DN_HEADS)
        cols = [pl.ds(h * DN_DK, DN_DK) for h in heads]
        ds_f = [dstate[h] for h in heads]
        for c in reversed(range(cb)):
            rows = pl.ds(c * CHUNK, CHUNK)
            ds_mx = [d.astype(MXU_DTYPE) for d in ds_f]
            do_h = [do_ref[rows, cols[h]].astype(MXU_DTYPE) for h in heads]
            dv_a = [_dot(intra_ref[h, c], do_h[h], 0, 0) for h in heads]
            dv_b = [_dot(kd_ref[rows, cols[h]], ds_mx[h]) for h in heads]
            d_intra = [_dot(do_h[h], vn_ref[rows, cols[h]], 1, 1) for h in heads]
            d_qd = [_dot(do_h[h], st_ref[h, c], 1, 1) for h in heads]
            d_kd = [_dot(vn_ref[rows, cols[h]], ds_mx[h], 1, 1) for h in heads]
            ds_q = [_dot(qd_ref[rows, cols[h]], do_h[h], 0, 0) for h in heads]
            dv_new = [dv_a[h] + dv_b[h] for h in heads]
            dv_mx = [d.astype(MXU_DTYPE) for d in dv_new]
            d_w = [_dot(dv_mx[h], st_ref[h, c], 1, 1) for h in heads]
            ds_w = [_dot(w_ref[rows, cols[h]], dv_mx[h], 0, 0) for h in heads]
            ds_next = []
            for h in heads:
                du_ref[rows, cols[h]] = dv_new[h]
                dintra_ref[h, c] = d_intra[h]
                dqd_ref[rows, cols[h]] = d_qd[h]
                dkd_ref[rows, cols[h]] = d_kd[h]
                dw_ref[rows, cols[h]] = -d_w[h]
                cd_h = cd_ref[h, c]
                dcd = jnp.sum(jnp.sum(st_ref[h, c].astype(jnp.float32) * ds_f[h], axis=1, keepdims=True), axis=0,
                              keepdims=True)
                dgl_ref[h, c] = dcd * cd_h
                ds_next.append(ds_q[h] + ds_f[h] * cd_h - ds_w[h])
            ds_f = ds_next
        for h in heads:
            dstate[h] = ds_f[h]

    sd = jax.ShapeDtypeStruct
    f32 = jnp.float32
    return pl.pallas_call(
        body, name="dn_bwd_scan", grid=(n_chunks // cb,),
        out_shape=(sd(do.shape, f32), sd(do.shape, f32), sd(do.shape, f32), sd(do.shape, f32),
                   sd((DN_HEADS, n_chunks, CHUNK, CHUNK), f32), sd((DN_HEADS, n_chunks, 1, DN_DV), f32)),
        in_specs=[tok, tok, tok, tok, sq, lane, tok, stt], out_specs=(tok, tok, tok, tok, sq, lane),
        scratch_shapes=[pltpu.VMEM((DN_HEADS, DN_DK, DN_DV), f32)],
        compiler_params=pltpu.CompilerParams(dimension_semantics=("arbitrary",)),
    )(do, w, qd, kd, intra, cd, vn, st)


def _dn_bwd_chunks(qkv, beta, gc, t, u, w, du, dw, dqd, dkd, dintra, dgl):
    s = qkv.shape[1]
    n_chunks = s // CHUNK
    tok, rowv, sq, lane, planes = _dn_specs(n_chunks)
    all_planes = pl.BlockSpec((3, DN_CB * CHUNK, DN_DK), lambda h, n: (0, n, h))

    def body(q_ref, k_ref, v_ref, beta_ref, gc_ref, t_ref, u_ref, w_ref, du_ref, dw_ref, dqd_ref, dkd_ref,
             dintra_ref, dgl_ref, dqkv_ref, dbeta_ref, dgc_ref):
        incl, strict, _ = _chunk_masks()

        def first(c):
            rows = pl.ds(c * CHUNK, CHUNK)
            q_c, k_c = q_ref[rows, :] * DN_SCALE, k_ref[rows, :]
            gcr_c = gc_ref[pl.ds(c, 1), :]
            beta_c, gcc_c = _to_col(beta_ref[pl.ds(c, 1), :]), _to_col(gcr_c)
            decay, kb, lmat, intra = _chunk_terms(q_c, k_c, beta_c, gcc_c, gcr_c)
            d_sol = jnp.concatenate([du_ref[rows, :], dw_ref[rows, :]], axis=1)
            d_rhs = _dot3(t_ref[c], d_sol, 0, 0)
            return dict(rows=rows, q=q_c, k=k_c, beta=beta_c, gcc=gcc_c, decay=decay, kb=kb, lmat=lmat, intra=intra,
                        d_rhs=d_rhs)

        def second(c, e):
            sol = jnp.concatenate([u_ref[e["rows"], :], w_ref[e["rows"], :].astype(jnp.float32)], axis=1)
            e["d_l"] = jnp.where(strict, -_dot(e["d_rhs"], sol, 1, 1), 0.0)
            e["d_intra"] = jnp.where(incl, dintra_ref[c], 0.0)
            d_qk = e["d_intra"] * e["decay"]
            e["dq"] = _dot(d_qk, e["k"])
            e["dk"] = _dot(d_qk, e["q"], 0, 0)

        def third(e):
            d_a = e["d_l"] * e["decay"]
            e["dkb"] = _dot(d_a, e["k"])
            e["dk"] = e["dk"] + _dot(d_a, e["kb"], 0, 0)

        def last(c, e):
            rows, q_c, k_c, beta_c, gcc_c = e["rows"], e["q"], e["k"], e["beta"], e["gcc"]
            v_c = v_ref[rows, :]
            eg = jnp.exp(gcc_c)
            g_last, is_last = _last_row(gcc_c)
            e_rev = jnp.exp(g_last - gcc_c)
            d_rhs_u, d_rhs_w = e["d_rhs"][:, :DN_DV], e["d_rhs"][:, DN_DV:]
            dqkv_ref[2, rows, :] = d_rhs_u * beta_c
            dbeta = jnp.sum(d_rhs_u * v_c, axis=1, keepdims=True)
            dkb = e["dkb"] + d_rhs_w * eg
            dgc = jnp.sum(d_rhs_w * e["kb"] * eg, axis=1, keepdims=True)
            m1 = e["d_l"] * e["lmat"]
            dgc = dgc + jnp.sum(m1, axis=1, keepdims=True)
            dgr = -jnp.sum(m1, axis=0, keepdims=True)
            m2 = e["d_intra"] * e["intra"]
            dgc = dgc + jnp.sum(m2, axis=1, keepdims=True)
            dgr = dgr - jnp.sum(m2, axis=0, keepdims=True)
            dqd = dqd_ref[rows, :]
            dq = e["dq"] + dqd * eg
            dgc = dgc + jnp.sum(dqd * q_c * eg, axis=1, keepdims=True)
            dkd = dkd_ref[rows, :]
            dk = e["dk"] + dkd * e_rev
            tk = jnp.sum(dkd * k_c * e_rev, axis=1, keepdims=True)
            dgc = dgc - tk
            d_last = dgl_ref[c][:, :1] + jnp.sum(tk, axis=0, keepdims=True)
            dgc = dgc + jnp.where(is_last, d_last, 0.0)
            dk = dk + dkb * beta_c
            dbeta = dbeta + jnp.sum(dkb * k_c, axis=1, keepdims=True)
            dqkv_ref[0, rows, :] = dq * DN_SCALE
            dqkv_ref[1, rows, :] = dk
            dbeta_ref[pl.ds(c, 1), :] = _to_row(dbeta)
            dgc_ref[pl.ds(c, 1), :] = _to_row(dgc) + dgr

        for c0 in range(0, DN_CB, DN_GROUP):
            chunks = range(c0, c0 + DN_GROUP)
            env = [first(c) for c in chunks]
            for c, e in zip(chunks, env):
                second(c, e)
            for e in env:
                third(e)
            for c, e in zip(chunks, env):
                last(c, e)

    sd = jax.ShapeDtypeStruct
    f32 = jnp.float32
    return pl.pallas_call(
        body, name="dn_bwd_chunks", grid=(DN_HEADS, n_chunks // DN_CB),
        out_shape=(sd(qkv.shape, f32), sd(beta.shape, f32), sd(gc.shape, f32)),
        in_specs=planes + [rowv, rowv, sq, tok, tok, tok, tok, tok, tok, sq, lane],
        out_specs=(all_planes, rowv, rowv),
        compiler_params=pltpu.CompilerParams(dimension_semantics=("parallel", "parallel")),
    )(qkv, qkv, qkv, beta, gc, t, u, w, du, dw, dqd, dkd, dintra, dgl)


@jax.custom_vjp
def _delta_rule_op(qkv, beta, gc):
    return _delta_rule_fwd(qkv, beta, gc)[0]


def _delta_rule_fwd(qkv, beta, gc):
    u, w, qd, kd, intra, t, cd = _dn_prep(qkv, beta, gc)
    out, vn, st = _dn_scan(u, w, qd, kd, intra, cd)
    return out, (qkv, beta, gc, u, w, qd, kd, intra, t, cd, vn, st)


def _delta_rule_bwd(res, do):
    qkv, beta, gc, u, w, qd, kd, intra, t, cd, vn, st = res
    du, dw, dqd, dkd, dintra, dgl = _dn_bwd_scan(do, w, qd, kd, intra, cd, vn, st)
    return _dn_bwd_chunks(qkv, beta, gc, t, u, w, du, dw, dqd, dkd, dintra, dgl)


_delta_rule_op.defvjp(_delta_rule_fwd, _delta_rule_bwd)


def _gated_delta_rule(qkv, g, beta):
    s, h = g.shape
    n_chunks = s // CHUNK
    gc = jnp.cumsum(g.T.reshape(h, n_chunks, CHUNK), axis=-1)
    return _delta_rule_op(qkv, beta.T.reshape(h, n_chunks, CHUNK), gc)


PRE_ROWS = 512
HALO = 8
PRE_W = DN_QK_W


def _shift_rows(xs, k):
    return pltpu.roll(xs, k, 0)[HALO:]


def _conv_silu(x_ref, halo_ref, w_ref, first_block):
    halo = jnp.where(first_block, 0.0, halo_ref[...])
    xs = jnp.concatenate([halo, x_ref[...]], axis=0)
    taps = [_shift_rows(xs, CONV_WIDTH - 1 - j) for j in range(CONV_WIDTH - 1)] + [x_ref[...]]
    conv = sum(w_ref[pl.ds(j, 1), :] * taps[j] for j in range(CONV_WIDTH))
    return conv, jax.nn.sigmoid(conv), taps


def _pre_specs():
    blk = pl.BlockSpec((PRE_ROWS, PRE_W), lambda j, i: (i, j))
    prev = pl.BlockSpec((HALO, PRE_W), lambda j, i: (jnp.maximum(i * (PRE_ROWS // HALO) - 1, 0), j))
    wts = pl.BlockSpec((CONV_WIDTH, PRE_W), lambda j, i: (0, j))
    plane = pl.BlockSpec((None, PRE_ROWS, PRE_W), lambda j, i: (j, i, 0))
    return blk, prev, wts, plane


def _pre_fwd_call(x, conv_w):
    s = x.shape[0]
    blk, prev, wts, plane = _pre_specs()

    def body(x_ref, halo_ref, w_ref, o_ref):
        conv, sig, _ = _conv_silu(x_ref, halo_ref, w_ref, pl.program_id(1) == 0)
        act = conv * sig
        is_v = pl.program_id(0) == 2
        for h in range(DN_HEADS):
            cols = slice(h * DN_DK, (h + 1) * DN_DK)
            a_h = act[:, cols]
            r = lax.rsqrt(jnp.sum(a_h * a_h, axis=-1, keepdims=True) + NORM_EPS)
            o_ref[:, cols] = a_h * jnp.where(is_v, 1.0, r)

    return pl.pallas_call(
        body, name="pre_fwd", grid=(3, s // PRE_ROWS),
        out_shape=jax.ShapeDtypeStruct((3, s, PRE_W), jnp.float32),
        in_specs=[blk, prev, wts], out_specs=plane,
        compiler_params=pltpu.CompilerParams(dimension_semantics=("parallel", "parallel")),
    )(x, x, conv_w)


def _pre_bwd_act_call(x, conv_w, d_out):
    s = x.shape[0]
    blk, prev, wts, plane = _pre_specs()

    def body(x_ref, halo_ref, w_ref, do_ref, dc_ref):
        conv, sig, _ = _conv_silu(x_ref, halo_ref, w_ref, pl.program_id(1) == 0)
        act = conv * sig
        d_silu = sig * (1.0 + conv * (1.0 - sig))
        is_v = pl.program_id(0) == 2
        for h in range(DN_HEADS):
            cols = slice(h * DN_DK, (h + 1) * DN_DK)
            a_h, do_h = act[:, cols], do_ref[:, cols]
            r = lax.rsqrt(jnp.sum(a_h * a_h, axis=-1, keepdims=True) + NORM_EPS)
            n_h = a_h * r
            d_norm = r * (do_h - n_h * jnp.sum(do_h * n_h, axis=-1, keepdims=True))
            dc_ref[:, cols] = jnp.where(is_v, do_h, d_norm) * d_silu[:, cols]

    return pl.pallas_call(
        body, name="pre_bwd_act", grid=(3, s // PRE_ROWS),
        out_shape=jax.ShapeDtypeStruct(x.shape, jnp.float32),
        in_specs=[blk, prev, wts, plane], out_specs=blk,
        compiler_params=pltpu.CompilerParams(dimension_semantics=("parallel", "parallel")),
    )(x, x, conv_w, d_out)


def _pre_bwd_conv_call(x, conv_w, dc):
    s = x.shape[0]
    n_blocks = s // PRE_ROWS
    blk, prev, wts, _ = _pre_specs()
    nxt = pl.BlockSpec((HALO, PRE_W), lambda j, i: (jnp.minimum((i + 1) * (PRE_ROWS // HALO), s // HALO - 1), j))

    def body(x_ref, halo_ref, w_ref, dc_ref, dcn_ref, dx_ref, dw_ref):
        i = pl.program_id(1)

        @pl.when(i == 0)
        def _():
            dw_ref[...] = jnp.zeros_like(dw_ref)

        dcv = dc_ref[...]
        ahead = jnp.concatenate([dcv, jnp.where(i == n_blocks - 1, 0.0, dcn_ref[...])], axis=0)
        dx = w_ref[pl.ds(CONV_WIDTH - 1, 1), :] * dcv
        for j in range(CONV_WIDTH - 1):
            k = CONV_WIDTH - 1 - j
            dx = dx + w_ref[pl.ds(j, 1), :] * pltpu.roll(ahead, PRE_ROWS + HALO - k, 0)[:PRE_ROWS]
        dx_ref[...] = dx
        halo = jnp.where(i == 0, 0.0, halo_ref[...])
        xs = jnp.concatenate([halo, x_ref[...]], axis=0)
        for j in range(CONV_WIDTH):
            tap = x_ref[...] if j == CONV_WIDTH - 1 else _shift_rows(xs, CONV_WIDTH - 1 - j)
            dw_ref[pl.ds(j, 1), :] += jnp.sum(dcv * tap, axis=0, keepdims=True)

    sd = jax.ShapeDtypeStruct
    return pl.pallas_call(
        body, name="pre_bwd_conv", grid=(3, n_blocks),
        out_shape=(sd(x.shape, jnp.float32), sd(conv_w.shape, jnp.float32)),
        in_specs=[blk, prev, wts, blk, nxt], out_specs=(blk, wts),
        compiler_params=pltpu.CompilerParams(dimension_semantics=("parallel", "arbitrary")),
    )(x, x, conv_w, dc, dc)


@jax.custom_vjp
def _pre_op(x, conv_w):
    return _pre_fwd_call(x, conv_w)


def _pre_op_fwd(x, conv_w):
    return _pre_fwd_call(x, conv_w), (x, conv_w)


def _pre_op_bwd(res, d_out):
    x, conv_w = res
    return _pre_bwd_conv_call(x, conv_w, _pre_bwd_act_call(x, conv_w, d_out))


_pre_op.defvjp(_pre_op_fwd, _pre_op_bwd)


def _project(h, h_lo, w, slot):
    b, s, d = h.shape
    return _linear(h.reshape(b * s, d), h_lo.reshape(b * s, d), w, slot).reshape(b, s, w.shape[1])


def _rope_table(positions, dh):
    inv_freq = ROPE_THETA ** (-jnp.arange(0, dh, 2, dtype=jnp.float32) / dh)
    ang = positions.astype(jnp.float32)[:, None] * inv_freq
    reps = 128 // (dh // 2)
    return jnp.concatenate([jnp.tile(jnp.cos(ang), (1, reps)), jnp.tile(jnp.sin(ang), (1, reps))], axis=-1)


GATE_ROWS = 512


def _gate_terms(o_h, z_h):
    r = lax.rsqrt(jnp.mean(o_h * o_h, axis=-1, keepdims=True) + NORM_EPS)
    sig = jax.nn.sigmoid(z_h)
    return r, o_h * r, sig, z_h * sig


def _gate_fwd_call(o, z, nw):
    tok = pl.BlockSpec((GATE_ROWS, DN_V_W), lambda i: (i, 0))
    vec = pl.BlockSpec((1, DN_DV), lambda i: (0, 0))

    def body(o_ref, z_ref, nw_ref, y_ref):
        for h in range(DN_HEADS):
            cols = pl.ds(h * DN_DV, DN_DV)
            _, n_h, _, g_h = _gate_terms(o_ref[:, cols], z_ref[:, cols])
            y_ref[:, cols] = n_h * nw_ref[...] * g_h

    return pl.pallas_call(
        body, name="gate_fwd", grid=(o.shape[0] // GATE_ROWS,),
        out_shape=jax.ShapeDtypeStruct(o.shape, jnp.float32), in_specs=[tok, tok, vec], out_specs=tok,
        compiler_params=pltpu.CompilerParams(dimension_semantics=("parallel",)),
    )(o, z, nw)


def _gate_bwd_call(o, z, nw, dy):
    tok = pl.BlockSpec((GATE_ROWS, DN_V_W), lambda i: (i, 0))
    vec = pl.BlockSpec((1, DN_DV), lambda i: (0, 0))

    def body(o_ref, z_ref, nw_ref, dy_ref, do_ref, dz_ref, dnw_ref):
        @pl.when(pl.program_id(0) == 0)
        def _():
            dnw_ref[...] = jnp.zeros_like(dnw_ref)

        for h in range(DN_HEADS):
            cols = pl.ds(h * DN_DV, DN_DV)
            z_h, dy_h = z_ref[:, cols], dy_ref[:, cols]
            r, n_h, sig, g_h = _gate_terms(o_ref[:, cols], z_h)
            dz_ref[:, cols] = dy_h * n_h * nw_ref[...] * (sig * (1.0 + z_h * (1.0 - sig)))
            dn = dy_h * nw_ref[...] * g_h
            do_ref[:, cols] = r * (dn - n_h * jnp.mean(dn * n_h, axis=-1, keepdims=True))
            dnw_ref[...] += jnp.sum(dy_h * n_h * g_h, axis=0, keepdims=True)

    sd = jax.ShapeDtypeStruct
    return pl.pallas_call(
        body, name="gate_bwd", grid=(o.shape[0] // GATE_ROWS,),
        out_shape=(sd(o.shape, jnp.float32), sd(o.shape, jnp.float32), sd(nw.shape, jnp.float32)),
        in_specs=[tok, tok, vec, tok], out_specs=(tok, tok, vec),
        compiler_params=pltpu.CompilerParams(dimension_semantics=("arbitrary",)),
    )(o, z, nw, dy)


@jax.custom_vjp
def _gate_op(o, z, nw):
    return _gate_fwd_call(o, z, nw)


def _gate_op_fwd(o, z, nw):
    return _gate_fwd_call(o, z, nw), (o, z, nw)


def _gate_op_bwd(res, dy):
    return _gate_bwd_call(*res, dy)


_gate_op.defvjp(_gate_op_fwd, _gate_op_bwd)


_MASKED = -1e30


def _swa_probs(qs, k_h, sinks, valid):
    ss = [jnp.where(valid, _dot(q_h, k_h, 1, 1) * (SWA_DH ** -0.5), _MASKED) for q_h in qs]
    ms = [jnp.maximum(jnp.max(s, axis=-1, keepdims=True), sink) for s, sink in zip(ss, sinks)]
    ps = [jnp.exp(s - m) for s, m in zip(ss, ms)]
    es = [jnp.exp(sink - m) for sink, m in zip(sinks, ms)]
    invs = [1.0 / (jnp.sum(p, axis=-1, keepdims=True) + e) for p, e in zip(ps, es)]
    return [p * inv for p, inv in zip(ps, invs)], [e * inv for e, inv in zip(es, invs)]


def _swa_valid(n):
    qi = lax.broadcasted_iota(jnp.int32, (WINDOW, 2 * WINDOW), 0)
    kj = lax.broadcasted_iota(jnp.int32, (WINDOW, 2 * WINDOW), 1)
    diff = qi + WINDOW - kj
    return (diff >= 0) & (diff < WINDOW) & ((kj >= WINDOW) | (n > 0))


def _rotate_half(x, transpose=False):
    half = SWA_DH // 2
    lower = lax.broadcasted_iota(jnp.int32, x.shape, 1) % SWA_DH < half
    ahead, behind = pltpu.roll(x, 128 - half, 1), pltpu.roll(x, half, 1)
    return jnp.where(lower, ahead, -behind) if transpose else jnp.where(lower, -ahead, behind)


def _rope(x, table):
    return x * table[:, :128] + _rotate_half(x) * table[:, 128:]


def _unrope(dy, table):
    return dy * table[:, :128] + _rotate_half(dy * table[:, 128:], transpose=True)


def _swa_specs():
    qs = pl.BlockSpec((WINDOW, SWA_Q_W), lambda n: (n, 0))
    first = lambda n: jnp.maximum(n - 1, 0)
    kv = [pl.BlockSpec((WINDOW, SWA_KV_W), lambda n: (first(n), 0)), pl.BlockSpec((WINDOW, SWA_KV_W), lambda n: (n, 0)),
          pl.BlockSpec((WINDOW, SWA_KV_W), lambda n: (first(n), 1)), pl.BlockSpec((WINDOW, SWA_KV_W), lambda n: (n, 1))]
    tables = [pl.BlockSpec((WINDOW, 256), lambda n: (first(n), 0)), pl.BlockSpec((WINDOW, 256), lambda n: (n, 0))]
    cur = pl.BlockSpec((WINDOW, SWA_KV_W), lambda n: (n, 0))
    sk = pl.BlockSpec((SWA_HEADS, 1, 128), lambda n: (0, 0, 0))
    return qs, kv, tables, cur, sk


def _swa_load(q_ref, kp_ref, kc_ref, vp_ref, vc_ref, tp_ref, tc_ref):
    table_kk = jnp.concatenate([tp_ref[...], tc_ref[...]], axis=0)
    kk = _rope(jnp.concatenate([kp_ref[...], kc_ref[...]], axis=0), table_kk)
    vv = jnp.concatenate([vp_ref[...], vc_ref[...]], axis=0)
    q_rot = []
    for b in range(SWA_Q_W // 128):
        pair = _rope(q_ref[:, pl.ds(b * 128, 128)], tc_ref[...])
        q_rot += [pair[:, :SWA_DH], pair[:, SWA_DH:]]
    split = lambda t: [t[:, hkv * SWA_DH:(hkv + 1) * SWA_DH] for hkv in range(SWA_KV_HEADS)]
    return q_rot, split(kk), split(vv), table_kk


def _swa_fwd_call(q, kv, table, sinks):
    qs, kvs, tables, _, sk = _swa_specs()

    def body(q_ref, kp_ref, kc_ref, vp_ref, vc_ref, tp_ref, tc_ref, sink_ref, o_ref):
        valid = _swa_valid(pl.program_id(0))
        q_rot, kk, vv, _ = _swa_load(q_ref, kp_ref, kc_ref, vp_ref, vc_ref, tp_ref, tc_ref)
        for hkv in range(SWA_KV_HEADS):
            heads = range(hkv * SWA_GROUP, (hkv + 1) * SWA_GROUP)
            probs, _ = _swa_probs([q_rot[h] for h in heads], kk[hkv], [sink_ref[h][:, :1] for h in heads], valid)
            outs = [_dot(p, vv[hkv]) for p in probs]
            for h, o in zip(heads, outs):
                o_ref[:, pl.ds(h * SWA_DH, SWA_DH)] = o

    return pl.pallas_call(
        body, name="swa_fwd", grid=(q.shape[0] // WINDOW,),
        out_shape=jax.ShapeDtypeStruct(q.shape, jnp.float32),
        in_specs=[qs] + kvs + tables + [sk], out_specs=qs,
        compiler_params=pltpu.CompilerParams(dimension_semantics=("parallel",)),
    )(q, kv, kv, kv, kv, table, table, sinks)


def _swa_bwd_call(q, kv, table, sinks, do):
    qs, kvs, tables, cur, sk = _swa_specs()

    def body(q_ref, kp_ref, kc_ref, vp_ref, vc_ref, tp_ref, tc_ref, sink_ref, do_ref,
             dq_ref, dkc_ref, dkp_ref, dvc_ref, dvp_ref, ds_ref):
        @pl.when(pl.program_id(0) == 0)
        def _():
            ds_ref[...] = jnp.zeros_like(ds_ref)

        valid = _swa_valid(pl.program_id(0))
        q_rot, kk, vv, table_kk = _swa_load(q_ref, kp_ref, kc_ref, vp_ref, vc_ref, tp_ref, tc_ref)
        lane0 = lax.broadcasted_iota(jnp.int32, (1, 128), 1) == 0
        dq_heads, dk_heads, dv_heads = [], [], []
        for hkv in range(SWA_KV_HEADS):
            k_h, v_h = kk[hkv], vv[hkv]
            heads = range(hkv * SWA_GROUP, (hkv + 1) * SWA_GROUP)
            q_hs = [q_rot[h] for h in heads]
            dos = [do_ref[:, pl.ds(h * SWA_DH, SWA_DH)] for h in heads]
            probs, p_sinks = _swa_probs(q_hs, k_h, [sink_ref[h][:, :1] for h in heads], valid)
            dps = [_dot(do_h, v_h, 1, 1) for do_h in dos]
            rss = [jnp.sum(p * dp, axis=-1, keepdims=True) for p, dp in zip(probs, dps)]
            d_ss = [p * (dp - rs) for p, dp, rs in zip(probs, dps, rss)]
            dq_heads += [_dot(d_s, k_h) * (SWA_DH ** -0.5) for d_s in d_ss]
            dks = [_dot(d_s, q_h, 0, 0) for d_s, q_h in zip(d_ss, q_hs)]
            dvs = [_dot(p, do_h, 0, 0) for p, do_h in zip(probs, dos)]
            for h, p_sink, rs in zip(heads, p_sinks, rss):
                d_sink = -jnp.sum(p_sink * rs, axis=0, keepdims=True)
                ds_ref[h] += jnp.where(lane0, d_sink, 0.0)
            dk_heads.append(sum(dks[1:], dks[0]) * (SWA_DH ** -0.5))
            dv_heads.append(sum(dvs[1:], dvs[0]))
        for b in range(SWA_Q_W // 128):
            pair = jnp.concatenate([dq_heads[2 * b], dq_heads[2 * b + 1]], axis=1)
            dq_ref[:, pl.ds(b * 128, 128)] = _unrope(pair, tc_ref[...])
        dk = _unrope(jnp.concatenate(dk_heads, axis=1), table_kk)
        dv = jnp.concatenate(dv_heads, axis=1)
        dkp_ref[...] = dk[:WINDOW]
        dkc_ref[...] = dk[WINDOW:]
        dvp_ref[...] = dv[:WINDOW]
        dvc_ref[...] = dv[WINDOW:]

    sd = jax.ShapeDtypeStruct
    f32 = jnp.float32
    half = (q.shape[0], SWA_KV_W)
    return pl.pallas_call(
        body, name="swa_bwd", grid=(q.shape[0] // WINDOW,),
        out_shape=(sd(q.shape, f32), sd(half, f32), sd(half, f32), sd(half, f32), sd(half, f32), sd(sinks.shape, f32)),
        in_specs=[qs] + kvs + tables + [sk, qs], out_specs=(qs, cur, cur, cur, cur, sk),
        compiler_params=pltpu.CompilerParams(dimension_semantics=("arbitrary",)),
    )(q, kv, kv, kv, kv, table, table, sinks, do)


@jax.custom_vjp
def _swa_op(q, kv, table, sinks):
    return _swa_fwd_call(q, kv, table, sinks)


def _swa_op_fwd(q, kv, table, sinks):
    return _swa_fwd_call(q, kv, table, sinks), (q, kv, table, sinks)


def _swa_op_bwd(res, do):
    q, kv, table, sinks = res
    dq, dkc, dkp, dvc, dvp, dsinks = _swa_bwd_call(q, kv, table, sinks, do)

    def fold(cur, prev):
        return cur + jnp.concatenate([prev[WINDOW:], jnp.zeros_like(prev[:WINDOW])], axis=0)

    return dq, jnp.concatenate([fold(dkc, dkp), fold(dvc, dvp)], axis=1), jnp.zeros_like(table), dsinks


_swa_op.defvjp(_swa_op_fwd, _swa_op_bwd)


def _swa_sink_attention(q, kv, table, sinks):
    return _swa_op(q, kv, table, jnp.broadcast_to(sinks[:, None, None], (SWA_HEADS, 1, 128)))


MEM_ROWS = 512


def _mem_probs(q_h, k_h):
    s = _dot(q_h, k_h, 1, 1) * (MEM_DH ** -0.5)
    p = jnp.exp(s - jnp.max(s, axis=-1, keepdims=True))
    return p / jnp.sum(p, axis=-1, keepdims=True)


def _mem_fwd_call(qm, kv):
    qs = pl.BlockSpec((MEM_ROWS, MEM_W), lambda i: (i, 0))
    kvs = pl.BlockSpec(kv.shape, lambda i: (0, 0))

    def body(q_ref, kv_ref, o_ref):
        for h in range(MEM_HEADS):
            cols = pl.ds(h * MEM_DH, MEM_DH)
            probs = _mem_probs(q_ref[:, cols], kv_ref[:, cols])
            o_ref[:, cols] = _dot(probs, kv_ref[:, pl.ds(MEM_W + h * MEM_DH, MEM_DH)])

    return pl.pallas_call(
        body, name="mem_fwd", grid=(qm.shape[0] // MEM_ROWS,),
        out_shape=jax.ShapeDtypeStruct(qm.shape, jnp.float32), in_specs=[qs, kvs], out_specs=qs,
        compiler_params=pltpu.CompilerParams(dimension_semantics=("parallel",)),
    )(qm, kv)


def _mem_bwd_call(qm, kv, do):
    qs = pl.BlockSpec((MEM_ROWS, MEM_W), lambda i: (i, 0))
    kvs = pl.BlockSpec(kv.shape, lambda i: (0, 0))

    def body(q_ref, kv_ref, do_ref, dq_ref, dkv_ref):
        @pl.when(pl.program_id(0) == 0)
        def _():
            dkv_ref[...] = jnp.zeros_like(dkv_ref)

        for h in range(MEM_HEADS):
            cols = pl.ds(h * MEM_DH, MEM_DH)
            v_cols = pl.ds(MEM_W + h * MEM_DH, MEM_DH)
            q_h, k_h, do_h = q_ref[:, cols], kv_ref[:, cols], do_ref[:, cols]
            probs = _mem_probs(q_h, k_h)
            dp = _dot(do_h, kv_ref[:, v_cols], 1, 1)
            d_s = probs * (dp - jnp.sum(probs * dp, axis=-1, keepdims=True))
            dq_ref[:, cols] = _dot(d_s, k_h) * (MEM_DH ** -0.5)
            dkv_ref[:, cols] += _dot(d_s, q_h, 0, 0) * (MEM_DH ** -0.5)
            dkv_ref[:, v_cols] += _dot(probs, do_h, 0, 0)

    sd = jax.ShapeDtypeStruct
    return pl.pallas_call(
        body, name="mem_bwd", grid=(qm.shape[0] // MEM_ROWS,),
        out_shape=(sd(qm.shape, jnp.float32), sd(kv.shape, jnp.float32)),
        in_specs=[qs, kvs, qs], out_specs=(qs, kvs),
        compiler_params=pltpu.CompilerParams(dimension_semantics=("arbitrary",)),
    )(qm, kv, do)


@jax.custom_vjp
def _mem_op(qm, kv):
    return _mem_fwd_call(qm, kv)


def _mem_op_fwd(qm, kv):
    return _mem_fwd_call(qm, kv), (qm, kv)


def _mem_op_bwd(res, do):
    return _mem_bwd_call(*res, do)


_mem_op.defvjp(_mem_op_fwd, _mem_op_bwd)


def _memory_attention(qm, kv):
    return _mem_op(qm[0], kv[0])[None]


def _mixer_a(h, h_lo, mem, mem_lo, p, s, layer):
    B, S, _ = h.shape
    proj = _project(h, h_lo, p["a_w_in"][layer], s["a_w_in"][layer])
    c1 = 2 * DN_QK_W + DN_V_W
    qkv = proj[..., :c1]
    z = proj[..., c1:QKVZ_W]
    qm = proj[..., QKVZ_W:QKVZ_W + MEM_W]
    a = proj[..., QKVZ_W + MEM_W:QKVZ_W + MEM_W + DN_HEADS]
    b = proj[..., QKVZ_W + MEM_W + DN_HEADS:QKVZ_W + MEM_W + 2 * DN_HEADS]
    planes = _pre_op(qkv[0], p["a_conv_w"][layer])
    beta = jax.nn.sigmoid(b[0])
    g = -jnp.exp(p["a_A_log"][layer]) * jax.nn.softplus(a[0] + p["a_dt_bias"][layer])
    o = _gate_op(_gated_delta_rule(planes, g, beta), z[0], p["a_norm_w"][layer][None])[None]
    kv = _project(mem, mem_lo, p["mem_w_kv"][layer], s["mem_w_kv"][layer])
    mo = _memory_attention(qm, kv)
    cat = jnp.concatenate([o, mo], axis=-1)
    return _project(cat, _lo(cat), p["w_o"][layer], s["w_o"][layer])


def _mixer_b(h, h_lo, mem, mem_lo, kv_shared, table, p, s, layer):
    j = layer - N_A
    proj = _project(h, h_lo, p["b_w_in"][j], s["b_w_in"][j])
    o = _swa_sink_attention(proj[0, :, :SWA_Q_W], kv_shared, table, p["b_sinks"][j])[None]
    kv = _project(mem, mem_lo, p["mem_w_kv"][layer], s["mem_w_kv"][layer])
    mo = _memory_attention(proj[..., SWA_Q_W:], kv)
    cat = jnp.concatenate([o, mo], axis=-1)
    return _project(cat, _lo(cat), p["w_o"][layer], s["w_o"][layer])


def _forward(p, s, x, mem, positions):
    table = _rope_table(positions[0], SWA_DH)
    h, h_lo, mem_lo = x, _lo(x), _lo(mem)
    kv_shared = None
    for layer in range(DEPTH):
        if layer < N_A:
            mix = _mixer_a(h, h_lo, mem, mem_lo, p, s, layer)
        else:
            mix = _mixer_b(h, h_lo, mem, mem_lo, kv_shared, table, p, s, layer)
        seq = h.shape[1]
        h2, h2_lo = _ln_res(h[0], mix[0], p["ln_g"][layer, 0][None], p["ln_b"][layer, 0][None])
        down = _mlp(h2, h2_lo, p["mlp_w_up"][layer], p["mlp_w_down"][layer], s["mlp_w_up"][layer],
                    s["mlp_w_down"][layer])
        h, h_lo = _ln_res(h2, down, p["ln_g"][layer, 1][None], p["ln_b"][layer, 1][None])
        h, h_lo = h.reshape(1, seq, D_MODEL), h_lo.reshape(1, seq, D_MODEL)
        if layer == N_A - 1:
            kv_shared = _project(h, h_lo, p["w_kv_shared"], s["w_kv_shared"])[0]
    return h


def _loss(diff, s, p, mem, positions, target):
    y = _forward({**p, **diff["small"]}, s, diff["x"], mem, positions)
    return 0.5 * jnp.sum(jnp.mean(jnp.square(y - target), axis=-1))


def _reorder_a_w_in(w):
    pad = jnp.zeros(w.shape[:-1] + (A_IN_PAD - A_IN,), w.dtype)
    return jnp.concatenate([w[..., :QKVZ_W], w[..., QKVZ_W + 2 * DN_HEADS:], w[..., QKVZ_W:QKVZ_W + 2 * DN_HEADS], pad],
                           axis=-1)


def _restore_a_w_in(w):
    return jnp.concatenate([w[..., :QKVZ_W], w[..., QKVZ_W + MEM_W:QKVZ_W + MEM_W + 2 * DN_HEADS],
                            w[..., QKVZ_W:QKVZ_W + MEM_W]], axis=-1)


def kernel(x, mem, positions, a_w_in, a_conv_w, a_A_log, a_dt_bias, a_norm_w, b_w_in, b_sinks, w_kv_shared, mem_w_kv, w_o, mlp_w_up, mlp_w_down, ln_g, ln_b, loss_target, m_a_w_in, m_a_conv_w, m_a_A_log, m_a_dt_bias, m_a_norm_w, m_b_w_in, m_b_sinks, m_w_kv_shared, m_mem_w_kv, m_w_o, m_mlp_w_up, m_mlp_w_down, m_ln_g, m_ln_b, v_a_w_in, v_a_conv_w, v_a_A_log, v_a_dt_bias, v_a_norm_w, v_b_w_in, v_b_sinks, v_w_kv_shared, v_mem_w_kv, v_w_o, v_mlp_w_up, v_mlp_w_down, v_ln_g, v_ln_b):
    w_sh = dict(a_w_in=a_w_in, a_conv_w=a_conv_w, a_A_log=a_A_log, a_dt_bias=a_dt_bias, a_norm_w=a_norm_w,
                b_w_in=b_w_in, b_sinks=b_sinks, w_kv_shared=w_kv_shared, mem_w_kv=mem_w_kv, w_o=w_o,
                mlp_w_up=mlp_w_up, mlp_w_down=mlp_w_down, ln_g=ln_g, ln_b=ln_b)
    m_sh = dict(a_w_in=m_a_w_in, a_conv_w=m_a_conv_w, a_A_log=m_a_A_log, a_dt_bias=m_a_dt_bias, a_norm_w=m_a_norm_w,
                b_w_in=m_b_w_in, b_sinks=m_b_sinks, w_kv_shared=m_w_kv_shared, mem_w_kv=m_mem_w_kv, w_o=m_w_o,
                mlp_w_up=m_mlp_w_up, mlp_w_down=m_mlp_w_down, ln_g=m_ln_g, ln_b=m_ln_b)
    v_sh = dict(a_w_in=v_a_w_in, a_conv_w=v_a_conv_w, a_A_log=v_a_A_log, a_dt_bias=v_a_dt_bias, a_norm_w=v_a_norm_w,
                b_w_in=v_b_w_in, b_sinks=v_b_sinks, w_kv_shared=v_w_kv_shared, mem_w_kv=v_mem_w_kv, w_o=v_w_o,
                mlp_w_up=v_mlp_w_up, mlp_w_down=v_mlp_w_down, ln_g=v_ln_g, ln_b=v_ln_b)
    shard_shapes = {n: w_sh[n].shape for n in WEIGHTS}
    rb, rows = _rows_for(w_sh)

    big, small = _pack(w_sh, rb, jnp.bfloat16)
    gbig, gsmall = _gather_weights(big.reshape(2, rb // 2, FLAT_W), small.reshape(2, SMALL_ROWS // 2, FLAT_W))
    gbig, gsmall = gbig.reshape(N_CHIPS, rb, FLAT_W), gsmall.reshape(N_CHIPS, SMALL_ROWS, FLAT_W)
    pieces = [_unpack(gbig[q], gsmall[q], shard_shapes) for q in range(N_CHIPS)]
    full = {n: jnp.concatenate([pieces[q][n] for q in range(N_CHIPS)], axis=SHARD_AXIS[n]) for n in SHARD_AXIS}
    for n in REPLICATED:
        full[n] = w_sh[n]
    big_w = {n: full[n] for n in BIG}
    big_w["a_w_in"] = _reorder_a_w_in(big_w["a_w_in"])
    small_w = {n: full[n] for n in SMALL}
    slots = {n: jnp.zeros(big_w[n].shape, jnp.float32) for n in BIG}

    loss, (grads, g_slots) = jax.value_and_grad(_loss, argnums=(0, 1))(
        {"x": x, "small": small_w}, slots, big_w, mem, positions, loss_target)
    loss = lax.psum(loss, ("x", "y", "c"))
    g_full = {**g_slots, **grads["small"]}
    g_full["a_w_in"] = _restore_a_w_in(g_full["a_w_in"])

    def shard_of(n, q):
        if n in REPLICATED:
            return g_full[n]
        size = shard_shapes[n][SHARD_AXIS[n]]
        return lax.slice_in_dim(g_full[n], q * size, (q + 1) * size, axis=SHARD_AXIS[n])

    parts = []
    for q in range(N_CHIPS):
        pb, ps = _pack({n: shard_of(n, q) for n in WEIGHTS}, rb, jnp.bfloat16)
        parts.append(jnp.concatenate([pb, ps.astype(jnp.bfloat16)], axis=0).reshape(2, rows // 2, FLAT_W))
    partials = jnp.stack(parts, axis=1)
    half = lax.axis_index("c").astype(jnp.int32).reshape(1)
    chip_partials = _add_pairs(partials, _swap_halves(partials), half)
    g_flat = _join_halves(_sum_chips(_scatter_grads(chip_partials))).reshape(rows, FLAT_W)

    odd = "a_w_in"
    blank = jnp.zeros(shard_shapes[odd], jnp.float32)
    flat = [jnp.concatenate(_pack({**d, odd: blank}, rb), axis=0) for d in (w_sh, m_sh, v_sh)]
    outs = (g_flat,) + tuple(_adamw(g_flat, *flat))
    g_o, d_o, m_o, v_o = [_unpack(o[:rb], o[rb:], shard_shapes) for o in outs]
    as_rows = lambda t: t.reshape(-1, t.shape[-1])
    updated = _adamw(as_rows(g_o[odd]), as_rows(w_sh[odd]), as_rows(m_sh[odd]), as_rows(v_sh[odd]))
    d_o[odd], m_o[odd], v_o[odd] = [t.reshape(shard_shapes[odd]) for t in updated]
    return (loss, grads["x"], *[g_o[n] for n in WEIGHTS], *[d_o[n] for n in WEIGHTS],
            *[m_o[n] for n in WEIGHTS], *[v_o[n] for n in WEIGHTS])
```

```python
import functools
import math

import jax
import jax.numpy as jnp
from jax import lax
from jax.experimental import pallas as pl
from jax.experimental.pallas import tpu as pltpu

D_MODEL = 1024
DEPTH = 4
N_A = DEPTH // 2
MEM_HEADS = 4
MEM_DH = D_MODEL // 16
MEM_W = MEM_HEADS * MEM_DH
DN_DK = 128
DN_DV = 128
DN_HEADS = (3 * D_MODEL) // (4 * DN_DV)
DN_QK_W = DN_HEADS * DN_DK
DN_V_W = DN_HEADS * DN_DV
CONV_WIDTH = 4
CHUNK = 64
SWA_DH = 64
SWA_HEADS = (3 * D_MODEL) // (4 * SWA_DH)
SWA_KV_HEADS = 2
SWA_GROUP = SWA_HEADS // SWA_KV_HEADS
SWA_Q_W = SWA_HEADS * SWA_DH
SWA_KV_W = SWA_KV_HEADS * SWA_DH
WINDOW = 128
ROPE_THETA = 10000.0
LN_EPS = 1e-5
NORM_EPS = 1e-6
DN_ALPHA = (2.0 * DEPTH) ** 0.25
A_IN = 2 * DN_QK_W + 2 * DN_V_W + 2 * DN_HEADS + MEM_W
A_IN_PAD = 3456
QKVZ_W = 2 * DN_QK_W + 2 * DN_V_W

ADAM_LR = 0.001
ADAM_B1 = 0.9
ADAM_B2 = 0.999
ADAM_EPS = 1e-08
ADAM_WD = 0.01
ADAM_STEP = 10

N_CHIPS = 4
FLAT_W = 1024
BIG = ("a_w_in", "b_w_in", "w_kv_shared", "mem_w_kv", "w_o", "mlp_w_up", "mlp_w_down")
SMALL = ("a_conv_w", "ln_g", "ln_b", "a_A_log", "a_dt_bias", "a_norm_w", "b_sinks")
REPLICATED = ("a_A_log", "a_dt_bias", "a_norm_w", "b_sinks")
WEIGHTS = ("a_w_in", "a_conv_w", "a_A_log", "a_dt_bias", "a_norm_w", "b_w_in", "b_sinks", "w_kv_shared",
           "mem_w_kv", "w_o", "mlp_w_up", "mlp_w_down", "ln_g", "ln_b")
SHARD_AXIS = {"a_w_in": 2, "a_conv_w": 2, "b_w_in": 1, "w_kv_shared": 0, "mem_w_kv": 1, "w_o": 1,
              "mlp_w_up": 2, "mlp_w_down": 1, "ln_g": 2, "ln_b": 2}
SMALL_ROWS = 32
ROW_ALIGN = 256

MESH = pl.DeviceIdType.MESH
HBM_SPEC = pl.BlockSpec(memory_space=pltpu.HBM)
VMEM_LIMIT = 48 * 1024 * 1024


def _rows_for(shards):
    n_big = sum(math.prod(shards[n].shape) for n in BIG)
    n_small = sum(math.prod(shards[n].shape) for n in SMALL)
    assert n_small <= SMALL_ROWS * FLAT_W
    total = -(-n_big // FLAT_W) + SMALL_ROWS
    total = -(-total // (2 * ROW_ALIGN)) * (2 * ROW_ALIGN)
    return total - SMALL_ROWS, total


def _pack(shards, rb, dtype_big=jnp.float32):
    big = jnp.concatenate([shards[n].reshape(-1).astype(dtype_big) for n in BIG])
    big = jnp.pad(big, (0, rb * FLAT_W - big.shape[0])).reshape(rb, FLAT_W)
    small = jnp.concatenate([shards[n].reshape(-1).astype(jnp.float32) for n in SMALL])
    small = jnp.pad(small, (0, SMALL_ROWS * FLAT_W - small.shape[0])).reshape(SMALL_ROWS, FLAT_W)
    return big, small


def _unpack(big, small, shapes):
    out = {}
    for flat, names in ((big.reshape(-1), BIG), (small.reshape(-1), SMALL)):
        off = 0
        for n in names:
            size = math.prod(shapes[n])
            out[n] = flat[off:off + size].reshape(shapes[n])
            off += size
    return out


def _other_chips(x, y):
    return [(1 - x, y), (x, 1 - y), (1 - x, 1 - y)]


def _gather_weights(big, small):
    def body(big_ref, small_ref, init_big_ref, init_small_ref, obig_ref, osmall_ref,
             send_sems, recv_sems, pass_send_sems, pass_recv_sems):
        del init_big_ref, init_small_ref
        x, y, c = lax.axis_index("x"), lax.axis_index("y"), lax.axis_index("c")
        me = 2 * x + y
        sibling = (x, y, 1 - c)
        pairs = ((big_ref, obig_ref), (small_ref, osmall_ref))
        sends = []
        for j, (px, py) in enumerate(_other_chips(x, y)):
            for i, (src, dst) in enumerate(pairs):
                sends.append(pltpu.make_async_remote_copy(
                    src_ref=src.at[c], dst_ref=dst.at[me, c], send_sem=send_sems.at[2 * j + i],
                    recv_sem=recv_sems.at[2 * j + i], device_id=(px, py, c), device_id_type=MESH))
        for cp in sends:
            cp.start()
        passed = []
        for j, (px, py) in enumerate(_other_chips(x, y)):
            for i, (src, dst) in enumerate(pairs):
                landed = dst.at[2 * px + py, c]
                pltpu.make_async_remote_copy(
                    src_ref=src.at[c], dst_ref=landed, send_sem=send_sems.at[2 * j + i],
                    recv_sem=recv_sems.at[2 * j + i], device_id=(px, py, c), device_id_type=MESH).wait_recv()
                passed.append(pltpu.make_async_remote_copy(
                    src_ref=landed, dst_ref=landed, send_sem=pass_send_sems.at[2 * j + i],
                    recv_sem=pass_recv_sems.at[2 * j + i], device_id=sibling, device_id_type=MESH))
                passed[-1].start()
        for j, (px, py) in enumerate(_other_chips(x, y)):
            for i, (src, dst) in enumerate(pairs):
                other_half = dst.at[2 * px + py, 1 - c]
                pltpu.make_async_remote_copy(
                    src_ref=other_half, dst_ref=other_half, send_sem=pass_send_sems.at[2 * j + i],
                    recv_sem=pass_recv_sems.at[2 * j + i], device_id=sibling, device_id_type=MESH).wait_recv()
        for cp in sends + passed:
            cp.wait_send()

    dma6 = pltpu.SemaphoreType.DMA((6,))
    four = lambda t: jnp.broadcast_to(t[None], (N_CHIPS,) + t.shape)
    return pl.pallas_call(
        body, name="gather_weights",
        out_shape=(jax.ShapeDtypeStruct((N_CHIPS,) + big.shape, big.dtype),
                   jax.ShapeDtypeStruct((N_CHIPS,) + small.shape, small.dtype)),
        in_specs=[HBM_SPEC] * 4, out_specs=(HBM_SPEC, HBM_SPEC), input_output_aliases={2: 0, 3: 1},
        scratch_shapes=[dma6, dma6, dma6, dma6],
    )(big, small, four(big), four(small))


def _scatter_grads(g):
    def body(g_ref, o_ref, send_sems, recv_sems, local_sem):
        x, y, c = lax.axis_index("x"), lax.axis_index("y"), lax.axis_index("c")
        me = 2 * x + y
        local = pltpu.make_async_copy(g_ref.at[me], o_ref.at[me], local_sem)
        local.start()
        sends = []
        for j, (px, py) in enumerate(_other_chips(x, y)):
            sends.append(pltpu.make_async_remote_copy(
                src_ref=g_ref.at[2 * px + py], dst_ref=o_ref.at[me], send_sem=send_sems.at[j], recv_sem=recv_sems.at[j],
                device_id=(px, py, c), device_id_type=MESH))
        for cp in sends:
            cp.start()
        for j, (px, py) in enumerate(_other_chips(x, y)):
            pltpu.make_async_remote_copy(
                src_ref=g_ref.at[me], dst_ref=o_ref.at[2 * px + py], send_sem=send_sems.at[j], recv_sem=recv_sems.at[j],
                device_id=(px, py, c), device_id_type=MESH).wait_recv()
        for cp in sends:
            cp.wait_send()
        local.wait()

    return pl.pallas_call(
        body, name="scatter_grads",
        out_shape=jax.ShapeDtypeStruct(g.shape, g.dtype),
        in_specs=[HBM_SPEC], out_specs=HBM_SPEC,
        scratch_shapes=[pltpu.SemaphoreType.DMA((3,)), pltpu.SemaphoreType.DMA((3,)), pltpu.SemaphoreType.DMA],
    )(g)


def _swap_halves(g):
    def body(g_ref, o_ref, send_sem, recv_sem):
        x, y, c = lax.axis_index("x"), lax.axis_index("y"), lax.axis_index("c")
        cp = pltpu.make_async_remote_copy(src_ref=g_ref.at[1 - c], dst_ref=o_ref, send_sem=send_sem, recv_sem=recv_sem,
                                          device_id=(x, y, 1 - c), device_id_type=MESH)
        cp.start()
        cp.wait()

    return pl.pallas_call(
        body, name="swap_halves",
        out_shape=jax.ShapeDtypeStruct(g.shape[1:], g.dtype),
        in_specs=[HBM_SPEC], out_specs=HBM_SPEC,
        scratch_shapes=[pltpu.SemaphoreType.DMA, pltpu.SemaphoreType.DMA],
    )(g)


def _join_halves(v):
    def body(v_ref, init_ref, o_ref, send_sem, recv_sem):
        del init_ref
        x, y, c = lax.axis_index("x"), lax.axis_index("y"), lax.axis_index("c")
        cp = pltpu.make_async_remote_copy(src_ref=v_ref, dst_ref=o_ref.at[c], send_sem=send_sem, recv_sem=recv_sem,
                                          device_id=(x, y, 1 - c), device_id_type=MESH)
        cp.start()
        cp.wait_send()
        pltpu.make_async_remote_copy(src_ref=v_ref, dst_ref=o_ref.at[1 - c], send_sem=send_sem, recv_sem=recv_sem,
                                     device_id=(x, y, 1 - c), device_id_type=MESH).wait_recv()

    return pl.pallas_call(
        body, name="join_halves",
        out_shape=jax.ShapeDtypeStruct((2,) + v.shape, v.dtype),
        in_specs=[HBM_SPEC, HBM_SPEC], out_specs=HBM_SPEC, input_output_aliases={1: 0},
        scratch_shapes=[pltpu.SemaphoreType.DMA, pltpu.SemaphoreType.DMA],
    )(v, jnp.stack([v, v]))


def _add_pairs(g, theirs, half):
    _, n, rows, width = g.shape
    assert rows % ROW_ALIGN == 0, rows

    def body(half_ref, g_ref, t_ref, o_ref):
        o_ref[...] = (g_ref[...].astype(jnp.float32) + t_ref[...].astype(jnp.float32)).astype(o_ref.dtype)

    blk = pl.BlockSpec((None, ROW_ALIGN, width), lambda p, i, h: (p, i, 0))
    grid_spec = pltpu.PrefetchScalarGridSpec(
        num_scalar_prefetch=1, grid=(n, rows // ROW_ALIGN),
        in_specs=[pl.BlockSpec((None, None, ROW_ALIGN, width), lambda p, i, h: (h[0], p, i, 0)), blk], out_specs=blk)
    return pl.pallas_call(
        body, name="add_pairs", grid_spec=grid_spec, out_shape=jax.ShapeDtypeStruct(theirs.shape, g.dtype),
        compiler_params=pltpu.CompilerParams(dimension_semantics=("parallel", "parallel")),
    )(half, g, theirs)


def _sum_chips(parts):
    n, rows, width = parts.shape
    assert rows % ROW_ALIGN == 0, rows

    def body(p_ref, o_ref):
        p = [p_ref[q].astype(jnp.float32) for q in range(n)]
        o_ref[...] = (p[0] + p[1]) + (p[2] + p[3])

    return pl.pallas_call(
        body, name="sum_chips", grid=(rows // ROW_ALIGN,),
        out_shape=jax.ShapeDtypeStruct((rows, width), jnp.float32),
        in_specs=[pl.BlockSpec((n, ROW_ALIGN, width), lambda i: (0, i, 0))],
        out_specs=pl.BlockSpec((ROW_ALIGN, width), lambda i: (i, 0)),
        compiler_params=pltpu.CompilerParams(dimension_semantics=("parallel",), vmem_limit_bytes=VMEM_LIMIT),
    )(parts)


def _adamw(g, w, m, v):
    rows, width = w.shape
    blk = ROW_ALIGN // 2

    def body(g_ref, w_ref, m_ref, v_ref, d_out, m_out, v_out):
        g = g_ref[...]
        m_new = ADAM_B1 * m_ref[...] + (1.0 - ADAM_B1) * g
        v_new = ADAM_B2 * v_ref[...] + (1.0 - ADAM_B2) * jnp.square(g)
        m_hat = m_new / (1.0 - ADAM_B1 ** ADAM_STEP)
        v_hat = v_new / (1.0 - ADAM_B2 ** ADAM_STEP)
        d_out[...] = -ADAM_LR * (m_hat / (jnp.sqrt(v_hat) + ADAM_EPS) + ADAM_WD * w_ref[...])
        m_out[...] = m_new
        v_out[...] = v_new

    spec = pl.BlockSpec((blk, width), lambda i: (i, 0))
    shape = jax.ShapeDtypeStruct((rows, width), jnp.float32)
    return pl.pallas_call(
        body, name="adamw", grid=(rows // blk,),
        out_shape=(shape,) * 3, in_specs=[spec] * 4, out_specs=(spec,) * 3,
        compiler_params=pltpu.CompilerParams(dimension_semantics=("parallel",), vmem_limit_bytes=VMEM_LIMIT),
    )(g, w, m, v)


def _tile(dim, pref):
    if dim <= pref:
        return dim
    for t in range(pref - pref % 128, 0, -128):
        if dim % t == 0:
            return t
    raise ValueError(f"no 128-aligned tile for {dim}")


def _matmul(a, b, *, ta=False, tb=False, name, epilogue=None, extra=None, out_dtype=jnp.float32):
    (k_a, m) = a.shape if ta else a.shape[::-1]
    (k_b, n) = b.shape[::-1] if tb else b.shape
    assert k_a == k_b, (a.shape, b.shape, ta, tb)
    k = k_a
    tk = _tile(k, 1152)
    nk = k // tk
    if ta:
        tm, tn = _tile(m, 1024), _tile(n, 2048 if m <= 1024 else 1024)
    else:
        tm, tn = _tile(m, 2048), _tile(n, 1152)
    a_spec = pl.BlockSpec((tk, tm), lambda i, j, l: (l, i)) if ta else pl.BlockSpec((tm, tk), lambda i, j, l: (i, l))
    b_spec = pl.BlockSpec((tn, tk), lambda i, j, l: (j, l)) if tb else pl.BlockSpec((tk, tn), lambda i, j, l: (l, j))
    o_spec = pl.BlockSpec((tm, tn), lambda i, j, l: (i, j))
    dims = (((0 if ta else 1,), (1 if tb else 0,)), ((), ()))
    has_extra = epilogue == "relu2_grad"
    assert has_extra == (extra is not None)

    def body(*refs):
        a_ref, b_ref = refs[:2]
        outs = refs[2 + has_extra:2 + has_extra + (2 if epilogue == "relu2" else 1)]
        l = pl.program_id(2)
        part = lax.dot_general(a_ref[...].astype(jnp.bfloat16), b_ref[...].astype(jnp.bfloat16), dims,
                               preferred_element_type=jnp.float32)

        def finish(acc):
            if epilogue is None:
                outs[0][...] = acc.astype(out_dtype)
            elif epilogue == "relu2":
                outs[0][...] = acc.astype(jnp.bfloat16)
                outs[1][...] = jnp.square(jnp.maximum(acc, 0.0)).astype(jnp.bfloat16)
            else:
                outs[0][...] = (acc * (2.0 * jnp.maximum(refs[2][...].astype(jnp.float32), 0.0))).astype(out_dtype)

        if nk == 1:
            finish(part)
            return
        acc_ref = refs[-1]

        @pl.when(l == 0)
        def _():
            acc_ref[...] = part

        @pl.when((l > 0) & (l < nk - 1))
        def _():
            acc_ref[...] += part

        @pl.when(l == nk - 1)
        def _():
            finish(acc_ref[...] + part)

    if epilogue == "relu2":
        out_shape = (jax.ShapeDtypeStruct((m, n), jnp.bfloat16),) * 2
        out_specs = (o_spec, o_spec)
    else:
        out_shape = jax.ShapeDtypeStruct((m, n), out_dtype)
        out_specs = o_spec
    return pl.pallas_call(
        body, name=name, grid=(m // tm, n // tn, nk), out_shape=out_shape,
        in_specs=[a_spec, b_spec] + ([o_spec] if has_extra else []), out_specs=out_specs,
        scratch_shapes=[pltpu.VMEM((tm, tn), jnp.float32)] if nk > 1 else [],
        compiler_params=pltpu.CompilerParams(dimension_semantics=("parallel", "parallel", "arbitrary"),
                                             vmem_limit_bytes=VMEM_LIMIT),
    )(*((a, b) + ((extra,) if has_extra else ())))


def _lo(x):
    return lax.stop_gradient(x.astype(jnp.bfloat16))


@jax.custom_vjp
def _linear(x, x_lo, w, slot):
    del x, slot
    return _matmul(x_lo, w, name="linear_fwd")


def _linear_fwd(x, x_lo, w, slot):
    del x, slot
    return _matmul(x_lo, w, name="linear_fwd"), (x_lo, w)


def _linear_bwd(res, dy):
    x_lo, w = res
    dy = dy.astype(jnp.bfloat16)
    dx = _matmul(dy, w, tb=True, name="linear_dx")
    dw = _matmul(x_lo, dy, ta=True, name="linear_dw")
    return dx, jnp.zeros_like(x_lo), jnp.zeros_like(w), dw


_linear.defvjp(_linear_fwd, _linear_bwd)


@jax.custom_vjp
def _mlp(h, h_lo, w_up, w_down, slot_up, slot_down):
    return _mlp_fwd(h, h_lo, w_up, w_down, slot_up, slot_down)[0]


def _mlp_fwd(h, h_lo, w_up, w_down, slot_up, slot_down):
    del h, slot_up, slot_down
    up, act = _matmul(h_lo, w_up, name="mlp_up", epilogue="relu2")
    return _matmul(act, w_down, name="mlp_down"), (h_lo, up, act, w_up, w_down)


def _mlp_bwd(res, dy):
    h_lo, up, act, w_up, w_down = res
    dy = dy.astype(jnp.bfloat16)
    d_up = _matmul(dy, w_down, tb=True, name="mlp_d_up", epilogue="relu2_grad", extra=up, out_dtype=jnp.bfloat16)
    dw_down = _matmul(act, dy, ta=True, name="mlp_dw_down")
    dw_up = _matmul(h_lo, d_up, ta=True, name="mlp_dw_up")
    dh = _matmul(d_up, w_up, tb=True, name="mlp_dh")
    return dh, jnp.zeros_like(h_lo), jnp.zeros_like(w_up), jnp.zeros_like(w_down), dw_up, dw_down


_mlp.defvjp(_mlp_fwd, _mlp_bwd)


LN_ROWS = 256


def _ln_call(h, mix, g, b):
    s, d = h.shape
    tok = pl.BlockSpec((LN_ROWS, d), lambda i: (i, 0))
    vec = pl.BlockSpec((1, d), lambda i: (0, 0))
    stat = pl.BlockSpec((LN_ROWS, 1), lambda i: (i, 0))

    def body(h_ref, mix_ref, g_ref, b_ref, y_ref, ylo_ref, xhat_ref, rstd_ref):
        z = DN_ALPHA * h_ref[...] + mix_ref[...]
        mu = jnp.mean(z, axis=-1, keepdims=True)
        zc = z - mu
        rstd = lax.rsqrt(jnp.mean(jnp.square(zc), axis=-1, keepdims=True) + LN_EPS)
        xhat = zc * rstd
        y = xhat * g_ref[...] + b_ref[...]
        y_ref[...] = y
        ylo_ref[...] = y.astype(ylo_ref.dtype)
        xhat_ref[...] = xhat
        rstd_ref[...] = rstd

    sd = jax.ShapeDtypeStruct
    return pl.pallas_call(
        body, name="ln_fwd", grid=(s // LN_ROWS,),
        out_shape=(sd((s, d), jnp.float32), sd((s, d), jnp.bfloat16), sd((s, d), jnp.float32), sd((s, 1), jnp.float32)),
        in_specs=[tok, tok, vec, vec], out_specs=(tok, tok, tok, stat),
        compiler_params=pltpu.CompilerParams(dimension_semantics=("parallel",)),
    )(h, mix, g, b)


def _ln_grad_call(dy, xhat, rstd, g):
    s, d = dy.shape
    tok = pl.BlockSpec((LN_ROWS, d), lambda i: (i, 0))
    vec = pl.BlockSpec((1, d), lambda i: (0, 0))
    stat = pl.BlockSpec((LN_ROWS, 1), lambda i: (i, 0))

    def body(dy_ref, xhat_ref, rstd_ref, g_ref, dz_ref, dg_ref, db_ref):
        @pl.when(pl.program_id(0) == 0)
        def _():
            dg_ref[...] = jnp.zeros_like(dg_ref)
            db_ref[...] = jnp.zeros_like(db_ref)

        dy, xhat = dy_ref[...], xhat_ref[...]
        dyg = dy * g_ref[...]
        m1 = jnp.mean(dyg, axis=-1, keepdims=True)
        m2 = jnp.mean(dyg * xhat, axis=-1, keepdims=True)
        dz_ref[...] = rstd_ref[...] * (dyg - m1 - xhat * m2)
        dg_ref[...] += jnp.sum(dy * xhat, axis=0, keepdims=True)
        db_ref[...] += jnp.sum(dy, axis=0, keepdims=True)

    sd = jax.ShapeDtypeStruct
    return pl.pallas_call(
        body, name="ln_bwd", grid=(s // LN_ROWS,),
        out_shape=(sd((s, d), jnp.float32), sd((1, d), jnp.float32), sd((1, d), jnp.float32)),
        in_specs=[tok, tok, stat, vec], out_specs=(tok, vec, vec),
        compiler_params=pltpu.CompilerParams(dimension_semantics=("arbitrary",)),
    )(dy, xhat, rstd, g)


@jax.custom_vjp
def _ln_res(h, mix, g, b):
    return _ln_call(h, mix, g, b)[:2]


def _ln_res_fwd(h, mix, g, b):
    y, y_lo, xhat, rstd = _ln_call(h, mix, g, b)
    return (y, y_lo), (xhat, rstd, g)


def _ln_res_bwd(res, cts):
    xhat, rstd, g = res
    dz, dg, db = _ln_grad_call(cts[0], xhat, rstd, g)
    return DN_ALPHA * dz, dz, dg, db


_ln_res.defvjp(_ln_res_fwd, _ln_res_bwd)


MXU_DTYPE = jnp.bfloat16
DN_CB = 16
DN_GROUP = 8
DN_SCAN_CB = 4
DN_SCALE = DN_DK ** -0.5


def _dot(a, b, ca=1, cb=0):
    return lax.dot_general(a.astype(MXU_DTYPE), b.astype(MXU_DTYPE), (((ca,), (cb,)), ((), ())),
                           preferred_element_type=jnp.float32)


def _chunk_masks():
    row = lax.broadcasted_iota(jnp.int32, (CHUNK, CHUNK), 0)
    col = lax.broadcasted_iota(jnp.int32, (CHUNK, CHUNK), 1)
    return row >= col, row > col, row == col


def _to_col(row_vec):
    _, _, eye = _chunk_masks()
    return jnp.sum(jnp.where(eye, jnp.broadcast_to(row_vec, (CHUNK, CHUNK)), 0.0), axis=1, keepdims=True)


def _to_row(col_vec):
    _, _, eye = _chunk_masks()
    return jnp.sum(jnp.where(eye, jnp.broadcast_to(col_vec, (CHUNK, CHUNK)), 0.0), axis=0, keepdims=True)


def _last_row(col_vec):
    last = lax.broadcasted_iota(jnp.int32, (CHUNK, 1), 0) == CHUNK - 1
    return jnp.sum(jnp.where(last, col_vec, 0.0), axis=0, keepdims=True), last


def _chunk_terms(q, k, beta, gcc, gcr):
    incl, strict, _ = _chunk_masks()
    decay = jnp.where(incl, jnp.exp(jnp.minimum(gcc - gcr, 0.0)), 0.0)
    kb = k * beta
    lmat = jnp.where(strict, _dot(kb, k, 1, 1) * decay, 0.0)
    intra = jnp.where(incl, _dot(q, k, 1, 1) * decay, 0.0)
    return decay, kb, lmat, intra


def _dot3(a, b, ca=1, cb=0):
    if MXU_DTYPE == jnp.float32:
        return _dot(a, b, ca, cb)
    a_hi, b_hi = a.astype(MXU_DTYPE), b.astype(MXU_DTYPE)
    a_lo = (a - a_hi.astype(jnp.float32)).astype(MXU_DTYPE)
    b_lo = (b - b_hi.astype(jnp.float32)).astype(MXU_DTYPE)
    return _dot(a_hi, b_hi, ca, cb) + (_dot(a_hi, b_lo, ca, cb) + _dot(a_lo, b_hi, ca, cb))


def _unit_lower_inverse(lmats):
    _, _, eye = _chunk_masks()
    ident = jnp.where(eye, 1.0, 0.0)
    ts = [ident - m for m in lmats]
    ps = [_dot(m, m) for m in lmats]
    for _ in range(4):
        ts = [t + _dot(t, p) for t, p in zip(ts, ps)]
        ps = [_dot(p, p) for p in ps]
    ts = [t + _dot(t, p) for t, p in zip(ts, ps)]
    resids = [(t - ident) + _dot3(m, t) for m, t in zip(lmats, ts)]
    return [t - _dot(t, r) for t, r in zip(ts, resids)]


def _dn_specs(n_chunks):
    tok = pl.BlockSpec((DN_CB * CHUNK, DN_DK), lambda h, n: (n, h))
    rowv = pl.BlockSpec((None, DN_CB, CHUNK), lambda h, n: (h, n, 0))
    sq = pl.BlockSpec((None, DN_CB, CHUNK, CHUNK), lambda h, n: (h, n, 0, 0))
    lane = pl.BlockSpec((None, DN_CB, 1, DN_DV), lambda h, n: (h, n, 0, 0))
    planes = [pl.BlockSpec((None, DN_CB * CHUNK, DN_DK), functools.partial(lambda h, n, p: (p, n, h), p=p))
              for p in range(3)]
    return tok, rowv, sq, lane, planes


def _dn_prep(qkv, beta, gc):
    s = qkv.shape[1]
    n_chunks = s // CHUNK
    tok, rowv, sq, lane, planes = _dn_specs(n_chunks)
    tok_shape = qkv.shape[1:]

    def body(q_ref, k_ref, v_ref, beta_ref, gc_ref, u_ref, w_ref, qd_ref, kd_ref, intra_ref, t_ref, cd_ref):
        for c0 in range(0, DN_CB, DN_GROUP):
            chunks = range(c0, c0 + DN_GROUP)
            rhs, lmats = [], []
            for c in chunks:
                rows = pl.ds(c * CHUNK, CHUNK)
                q_c, k_c, v_c = q_ref[rows, :] * DN_SCALE, k_ref[rows, :], v_ref[rows, :]
                gcr_c = gc_ref[pl.ds(c, 1), :]
                beta_c, gcc_c = _to_col(beta_ref[pl.ds(c, 1), :]), _to_col(gcr_c)
                _, kb, lmat, intra = _chunk_terms(q_c, k_c, beta_c, gcc_c, gcr_c)
                eg = jnp.exp(gcc_c)
                g_last, _ = _last_row(gcc_c)
                qd_ref[rows, :] = (q_c * eg).astype(qd_ref.dtype)
                kd_ref[rows, :] = (k_c * jnp.exp(g_last - gcc_c)).astype(kd_ref.dtype)
                intra_ref[c] = intra.astype(intra_ref.dtype)
                cd_ref[c] = jnp.broadcast_to(jnp.exp(g_last), (1, DN_DV))
                rhs.append(jnp.concatenate([v_c * beta_c, kb * eg], axis=1))
                lmats.append(lmat)
            ts = _unit_lower_inverse(lmats)
            sols = [_dot3(t, r) for t, r in zip(ts, rhs)]
            for c, t, sol in zip(chunks, ts, sols):
                rows = pl.ds(c * CHUNK, CHUNK)
                t_ref[c] = t
                u_ref[rows, :] = sol[:, :DN_DV]
                w_ref[rows, :] = sol[:, DN_DV:].astype(w_ref.dtype)

    f32, mx = jnp.float32, MXU_DTYPE
    sd = jax.ShapeDtypeStruct
    return pl.pallas_call(
        body, name="dn_prep", grid=(DN_HEADS, n_chunks // DN_CB),
        out_shape=(sd(tok_shape, f32), sd(tok_shape, mx), sd(tok_shape, mx), sd(tok_shape, mx),
                   sd((DN_HEADS, n_chunks, CHUNK, CHUNK), mx), sd((DN_HEADS, n_chunks, CHUNK, CHUNK), f32),
                   sd((DN_HEADS, n_chunks, 1, DN_DV), f32)),
        in_specs=planes + [rowv, rowv], out_specs=(tok, tok, tok, tok, sq, sq, lane),
        compiler_params=pltpu.CompilerParams(dimension_semantics=("parallel", "parallel")),
    )(qkv, qkv, qkv, beta, gc)


def _dn_scan(u, w, qd, kd, intra, cd):
    s, width = u.shape
    n_chunks = s // CHUNK
    cb = DN_SCAN_CB
    tok = pl.BlockSpec((cb * CHUNK, width), lambda n: (n, 0))
    sq = pl.BlockSpec((DN_HEADS, cb, CHUNK, CHUNK), lambda n: (0, n, 0, 0))
    lane = pl.BlockSpec((DN_HEADS, cb, 1, DN_DV), lambda n: (0, n, 0, 0))
    st = pl.BlockSpec((DN_HEADS, cb, DN_DK, DN_DV), lambda n: (0, n, 0, 0))

    def body(u_ref, w_ref, qd_ref, kd_ref, intra_ref, cd_ref, o_ref, vn_ref, st_ref, state):
        @pl.when(pl.program_id(0) == 0)
        def _():
            state[...] = jnp.zeros_like(state)

        heads = range(DN_HEADS)
        cols = [pl.ds(h * DN_DK, DN_DK) for h in heads]
        s_f = [state[h] for h in heads]
        for c in range(cb):
            rows = pl.ds(c * CHUNK, CHUNK)
            s_mx = [s.astype(MXU_DTYPE) for s in s_f]
            for h in heads:
                st_ref[h, c] = s_mx[h]
            ws = [_dot(w_ref[rows, cols[h]], s_mx[h]) for h in heads]
            qs = [_dot(qd_ref[rows, cols[h]], s_mx[h]) for h in heads]
            v_new = [(u_ref[rows, cols[h]] - ws[h]).astype(MXU_DTYPE) for h in heads]
            inner = [_dot(intra_ref[h, c], v_new[h]) for h in heads]
            outer = [_dot(kd_ref[rows, cols[h]], v_new[h], 0, 0) for h in heads]
            for h in heads:
                vn_ref[rows, cols[h]] = v_new[h]
                o_ref[rows, cols[h]] = qs[h] + inner[h]
            s_f = [s_f[h] * cd_ref[h, c] + outer[h] for h in heads]
        for h in heads:
            state[h] = s_f[h]

    sd = jax.ShapeDtypeStruct
    return pl.pallas_call(
        body, name="dn_scan", grid=(n_chunks // cb,),
        out_shape=(sd(u.shape, jnp.float32), sd(u.shape, MXU_DTYPE),
                   sd((DN_HEADS, n_chunks, DN_DK, DN_DV), MXU_DTYPE)),
        in_specs=[tok, tok, tok, tok, sq, lane], out_specs=(tok, tok, st),
        scratch_shapes=[pltpu.VMEM((DN_HEADS, DN_DK, DN_DV), jnp.float32)],
        compiler_params=pltpu.CompilerParams(dimension_semantics=("arbitrary",)),
    )(u, w, qd, kd, intra, cd)


def _dn_bwd_scan(do, w, qd, kd, intra, cd, vn, st):
    s, width = do.shape
    n_chunks = s // CHUNK
    cb = DN_SCAN_CB
    last = n_chunks // cb - 1
    tok = pl.BlockSpec((cb * CHUNK, width), lambda n: (last - n, 0))
    sq = pl.BlockSpec((DN_HEADS, cb, CHUNK, CHUNK), lambda n: (0, last - n, 0, 0))
    lane = pl.BlockSpec((DN_HEADS, cb, 1, DN_DV), lambda n: (0, last - n, 0, 0))
    stt = pl.BlockSpec((DN_HEADS, cb, DN_DK, DN_DV), lambda n: (0, last - n, 0, 0))

    def body(do_ref, w_ref, qd_ref, kd_ref, intra_ref, cd_ref, vn_ref, st_ref,
             du_ref, dw_ref, dqd_ref, dkd_ref, dintra_ref, dgl_ref, dstate):
        @pl.when(pl.program_id(0) == 0)
        def _():
            dstate[...] = jnp.zeros_like(dstate)

        heads = range(DN_HEADS)
        cols = [pl.ds(h * DN_DK, DN_DK) for h in heads]
        ds_f = [dstate[h] for h in heads]
        for c in reversed(range(cb)):
            rows = pl.ds(c * CHUNK, CHUNK)
            ds_mx = [d.astype(MXU_DTYPE) for d in ds_f]
            do_h = [do_ref[rows, cols[h]].astype(MXU_DTYPE) for h in heads]
            dv_a = [_dot(intra_ref[h, c], do_h[h], 0, 0) for h in heads]
            dv_b = [_dot(kd_ref[rows, cols[h]], ds_mx[h]) for h in heads]
            d_intra = [_dot(do_h[h], vn_ref[rows, cols[h]], 1, 1) for h in heads]
            d_qd = [_dot(do_h[h], st_ref[h, c], 1, 1) for h in heads]
            d_kd = [_dot(vn_ref[rows, cols[h]], ds_mx[h], 1, 1) for h in heads]
            ds_q = [_dot(qd_ref[rows, cols[h]], do_h[h], 0, 0) for h in heads]
            dv_new = [dv_a[h] + dv_b[h] for h in heads]
            dv_mx = [d.astype(MXU_DTYPE) for d in dv_new]
            d_w = [_dot(dv_mx[h], st_ref[h, c], 1, 1) for h in heads]
            ds_w = [_dot(w_ref[rows, cols[h]], dv_mx[h], 0, 0) for h in heads]
            ds_next = []
            for h in heads:
                du_ref[rows, cols[h]] = dv_new[h]
                dintra_ref[h, c] = d_intra[h]
                dqd_ref[rows, cols[h]] = d_qd[h]
                dkd_ref[rows, cols[h]] = d_kd[h]
                dw_ref[rows, cols[h]] = -d_w[h]
                cd_h = cd_ref[h, c]
                dcd = jnp.sum(jnp.sum(st_ref[h, c].astype(jnp.float32) * ds_f[h], axis=1, keepdims=True), axis=0,
                              keepdims=True)
                dgl_ref[h, c] = dcd * cd_h
                ds_next.append(ds_q[h] + ds_f[h] * cd_h - ds_w[h])
            ds_f = ds_next
        for h in heads:
            dstate[h] = ds_f[h]

    sd = jax.ShapeDtypeStruct
    f32 = jnp.float32
    return pl.pallas_call(
        body, name="dn_bwd_scan", grid=(n_chunks // cb,),
        out_shape=(sd(do.shape, f32), sd(do.shape, f32), sd(do.shape, f32), sd(do.shape, f32),
                   sd((DN_HEADS, n_chunks, CHUNK, CHUNK), f32), sd((DN_HEADS, n_chunks, 1, DN_DV), f32)),
        in_specs=[tok, tok, tok, tok, sq, lane, tok, stt], out_specs=(tok, tok, tok, tok, sq, lane),
        scratch_shapes=[pltpu.VMEM((DN_HEADS, DN_DK, DN_DV), f32)],
        compiler_params=pltpu.CompilerParams(dimension_semantics=("arbitrary",)),
    )(do, w, qd, kd, intra, cd, vn, st)


def _dn_bwd_chunks(qkv, beta, gc, t, u, w, du, dw, dqd, dkd, dintra, dgl):
    s = qkv.shape[1]
    n_chunks = s // CHUNK
    tok, rowv, sq, lane, planes = _dn_specs(n_chunks)
    all_planes = pl.BlockSpec((3, DN_CB * CHUNK, DN_DK), lambda h, n: (0, n, h))

    def body(q_ref, k_ref, v_ref, beta_ref, gc_ref, t_ref, u_ref, w_ref, du_ref, dw_ref, dqd_ref, dkd_ref,
             dintra_ref, dgl_ref, dqkv_ref, dbeta_ref, dgc_ref):
        incl, strict, _ = _chunk_masks()

        def first(c):
            rows = pl.ds(c * CHUNK, CHUNK)
            q_c, k_c = q_ref[rows, :] * DN_SCALE, k_ref[rows, :]
            gcr_c = gc_ref[pl.ds(c, 1), :]
            beta_c, gcc_c = _to_col(beta_ref[pl.ds(c, 1), :]), _to_col(gcr_c)
            decay, kb, lmat, intra = _chunk_terms(q_c, k_c, beta_c, gcc_c, gcr_c)
            d_sol = jnp.concatenate([du_ref[rows, :], dw_ref[rows, :]], axis=1)
            d_rhs = _dot3(t_ref[c], d_sol, 0, 0)
            return dict(rows=rows, q=q_c, k=k_c, beta=beta_c, gcc=gcc_c, decay=decay, kb=kb, lmat=lmat, intra=intra,
                        d_rhs=d_rhs)

        def second(c, e):
            sol = jnp.concatenate([u_ref[e["rows"], :], w_ref[e["rows"], :].astype(jnp.float32)], axis=1)
            e["d_l"] = jnp.where(strict, -_dot(e["d_rhs"], sol, 1, 1), 0.0)
            e["d_intra"] = jnp.where(incl, dintra_ref[c], 0.0)
            d_qk = e["d_intra"] * e["decay"]
            e["dq"] = _dot(d_qk, e["k"])
            e["dk"] = _dot(d_qk, e["q"], 0, 0)

        def third(e):
            d_a = e["d_l"] * e["decay"]
            e["dkb"] = _dot(d_a, e["k"])
            e["dk"] = e["dk"] + _dot(d_a, e["kb"], 0, 0)

        def last(c, e):
            rows, q_c, k_c, beta_c, gcc_c = e["rows"], e["q"], e["k"], e["beta"], e["gcc"]
            v_c = v_ref[rows, :]
            eg = jnp.exp(gcc_c)
            g_last, is_last = _last_row(gcc_c)
            e_rev = jnp.exp(g_last - gcc_c)
            d_rhs_u, d_rhs_w = e["d_rhs"][:, :DN_DV], e["d_rhs"][:, DN_DV:]
            dqkv_ref[2, rows, :] = d_rhs_u * beta_c
            dbeta = jnp.sum(d_rhs_u * v_c, axis=1, keepdims=True)
            dkb = e["dkb"] + d_rhs_w * eg
            dgc = jnp.sum(d_rhs_w * e["kb"] * eg, axis=1, keepdims=True)
            m1 = e["d_l"] * e["lmat"]
            dgc = dgc + jnp.sum(m1, axis=1, keepdims=True)
            dgr = -jnp.sum(m1, axis=0, keepdims=True)
            m2 = e["d_intra"] * e["intra"]
            dgc = dgc + jnp.sum(m2, axis=1, keepdims=True)
            dgr = dgr - jnp.sum(m2, axis=0, keepdims=True)
            dqd = dqd_ref[rows, :]
            dq = e["dq"] + dqd * eg
            dgc = dgc + jnp.sum(dqd * q_c * eg, axis=1, keepdims=True)
            dkd = dkd_ref[rows, :]
            dk = e["dk"] + dkd * e_rev
            tk = jnp.sum(dkd * k_c * e_rev, axis=1, keepdims=True)
            dgc = dgc - tk
            d_last = dgl_ref[c][:, :1] + jnp.sum(tk, axis=0, keepdims=True)
            dgc = dgc + jnp.where(is_last, d_last, 0.0)
            dk = dk + dkb * beta_c
            dbeta = dbeta + jnp.sum(dkb * k_c, axis=1, keepdims=True)
            dqkv_ref[0, rows, :] = dq * DN_SCALE
            dqkv_ref[1, rows, :] = dk
            dbeta_ref[pl.ds(c, 1), :] = _to_row(dbeta)
            dgc_ref[pl.ds(c, 1), :] = _to_row(dgc) + dgr

        for c0 in range(0, DN_CB, DN_GROUP):
            chunks = range(c0, c0 + DN_GROUP)
            env = [first(c) for c in chunks]
            for c, e in zip(chunks, env):
                second(c, e)
            for e in env:
                third(e)
            for c, e in zip(chunks, env):
                last(c, e)

    sd = jax.ShapeDtypeStruct
    f32 = jnp.float32
    return pl.pallas_call(
        body, name="dn_bwd_chunks", grid=(DN_HEADS, n_chunks // DN_CB),
        out_shape=(sd(qkv.shape, f32), sd(beta.shape, f32), sd(gc.shape, f32)),
        in_specs=planes + [rowv, rowv, sq, tok, tok, tok, tok, tok, tok, sq, lane],
        out_specs=(all_planes, rowv, rowv),
        compiler_params=pltpu.CompilerParams(dimension_semantics=("parallel", "parallel")),
    )(qkv, qkv, qkv, beta, gc, t, u, w, du, dw, dqd, dkd, dintra, dgl)


@jax.custom_vjp
def _delta_rule_op(qkv, beta, gc):
    return _delta_rule_fwd(qkv, beta, gc)[0]


def _delta_rule_fwd(qkv, beta, gc):
    u, w, qd, kd, intra, t, cd = _dn_prep(qkv, beta, gc)
    out, vn, st = _dn_scan(u, w, qd, kd, intra, cd)
    return out, (qkv, beta, gc, u, w, qd, kd, intra, t, cd, vn, st)


def _delta_rule_bwd(res, do):
    qkv, beta, gc, u, w, qd, kd, intra, t, cd, vn, st = res
    du, dw, dqd, dkd, dintra, dgl = _dn_bwd_scan(do, w, qd, kd, intra, cd, vn, st)
    return _dn_bwd_chunks(qkv, beta, gc, t, u, w, du, dw, dqd, dkd, dintra, dgl)


_delta_rule_op.defvjp(_delta_rule_fwd, _delta_rule_bwd)


def _gated_delta_rule(qkv, g, beta):
    s, h = g.shape
    n_chunks = s // CHUNK
    gc = jnp.cumsum(g.T.reshape(h, n_chunks, CHUNK), axis=-1)
    return _delta_rule_op(qkv, beta.T.reshape(h, n_chunks, CHUNK), gc)


PRE_ROWS = 512
HALO = 8
PRE_W = DN_QK_W


def _shift_rows(xs, k):
    return pltpu.roll(xs, k, 0)[HALO:]


def _conv_silu(x_ref, halo_ref, w_ref, first_block):
    halo = jnp.where(first_block, 0.0, halo_ref[...])
    xs = jnp.concatenate([halo, x_ref[...]], axis=0)
    taps = [_shift_rows(xs, CONV_WIDTH - 1 - j) for j in range(CONV_WIDTH - 1)] + [x_ref[...]]
    conv = sum(w_ref[pl.ds(j, 1), :] * taps[j] for j in range(CONV_WIDTH))
    return conv, jax.nn.sigmoid(conv), taps


def _pre_specs():
    blk = pl.BlockSpec((PRE_ROWS, PRE_W), lambda j, i: (i, j))
    prev = pl.BlockSpec((HALO, PRE_W), lambda j, i: (jnp.maximum(i * (PRE_ROWS // HALO) - 1, 0), j))
    wts = pl.BlockSpec((CONV_WIDTH, PRE_W), lambda j, i: (0, j))
    plane = pl.BlockSpec((None, PRE_ROWS, PRE_W), lambda j, i: (j, i, 0))
    return blk, prev, wts, plane


def _pre_fwd_call(x, conv_w):
    s = x.shape[0]
    blk, prev, wts, plane = _pre_specs()

    def body(x_ref, halo_ref, w_ref, o_ref):
        conv, sig, _ = _conv_silu(x_ref, halo_ref, w_ref, pl.program_id(1) == 0)
        act = conv * sig
        is_v = pl.program_id(0) == 2
        for h in range(DN_HEADS):
            cols = slice(h * DN_DK, (h + 1) * DN_DK)
            a_h = act[:, cols]
            r = lax.rsqrt(jnp.sum(a_h * a_h, axis=-1, keepdims=True) + NORM_EPS)
            o_ref[:, cols] = a_h * jnp.where(is_v, 1.0, r)

    return pl.pallas_call(
        body, name="pre_fwd", grid=(3, s // PRE_ROWS),
        out_shape=jax.ShapeDtypeStruct((3, s, PRE_W), jnp.float32),
        in_specs=[blk, prev, wts], out_specs=plane,
        compiler_params=pltpu.CompilerParams(dimension_semantics=("parallel", "parallel")),
    )(x, x, conv_w)


def _pre_bwd_act_call(x, conv_w, d_out):
    s = x.shape[0]
    blk, prev, wts, plane = _pre_specs()

    def body(x_ref, halo_ref, w_ref, do_ref, dc_ref):
        conv, sig, _ = _conv_silu(x_ref, halo_ref, w_ref, pl.program_id(1) == 0)
        act = conv * sig
        d_silu = sig * (1.0 + conv * (1.0 - sig))
        is_v = pl.program_id(0) == 2
        for h in range(DN_HEADS):
            cols = slice(h * DN_DK, (h + 1) * DN_DK)
            a_h, do_h = act[:, cols], do_ref[:, cols]
            r = lax.rsqrt(jnp.sum(a_h * a_h, axis=-1, keepdims=True) + NORM_EPS)
            n_h = a_h * r
            d_norm = r * (do_h - n_h * jnp.sum(do_h * n_h, axis=-1, keepdims=True))
            dc_ref[:, cols] = jnp.where(is_v, do_h, d_norm) * d_silu[:, cols]

    return pl.pallas_call(
        body, name="pre_bwd_act", grid=(3, s // PRE_ROWS),
        out_shape=jax.ShapeDtypeStruct(x.shape, jnp.float32),
        in_specs=[blk, prev, wts, plane], out_specs=blk,
        compiler_params=pltpu.CompilerParams(dimension_semantics=("parallel", "parallel")),
    )(x, x, conv_w, d_out)


def _pre_bwd_conv_call(x, conv_w, dc):
    s = x.shape[0]
    n_blocks = s // PRE_ROWS
    blk, prev, wts, _ = _pre_specs()
    nxt = pl.BlockSpec((HALO, PRE_W), lambda j, i: (jnp.minimum((i + 1) * (PRE_ROWS // HALO), s // HALO - 1), j))

    def body(x_ref, halo_ref, w_ref, dc_ref, dcn_ref, dx_ref, dw_ref):
        i = pl.program_id(1)

        @pl.when(i == 0)
        def _():
            dw_ref[...] = jnp.zeros_like(dw_ref)

        dcv = dc_ref[...]
        ahead = jnp.concatenate([dcv, jnp.where(i == n_blocks - 1, 0.0, dcn_ref[...])], axis=0)
        dx = w_ref[pl.ds(CONV_WIDTH - 1, 1), :] * dcv
        for j in range(CONV_WIDTH - 1):
            k = CONV_WIDTH - 1 - j
            dx = dx + w_ref[pl.ds(j, 1), :] * pltpu.roll(ahead, PRE_ROWS + HALO - k, 0)[:PRE_ROWS]
        dx_ref[...] = dx
        halo = jnp.where(i == 0, 0.0, halo_ref[...])
        xs = jnp.concatenate([halo, x_ref[...]], axis=0)
        for j in range(CONV_WIDTH):
            tap = x_ref[...] if j == CONV_WIDTH - 1 else _shift_rows(xs, CONV_WIDTH - 1 - j)
            dw_ref[pl.ds(j, 1), :] += jnp.sum(dcv * tap, axis=0, keepdims=True)

    sd = jax.ShapeDtypeStruct
    return pl.pallas_call(
        body, name="pre_bwd_conv", grid=(3, n_blocks),
        out_shape=(sd(x.shape, jnp.float32), sd(conv_w.shape, jnp.float32)),
        in_specs=[blk, prev, wts, blk, nxt], out_specs=(blk, wts),
        compiler_params=pltpu.CompilerParams(dimension_semantics=("parallel", "arbitrary")),
    )(x, x, conv_w, dc, dc)


@jax.custom_vjp
def _pre_op(x, conv_w):
    return _pre_fwd_call(x, conv_w)


def _pre_op_fwd(x, conv_w):
    return _pre_fwd_call(x, conv_w), (x, conv_w)


def _pre_op_bwd(res, d_out):
    x, conv_w = res
    return _pre_bwd_conv_call(x, conv_w, _pre_bwd_act_call(x, conv_w, d_out))


_pre_op.defvjp(_pre_op_fwd, _pre_op_bwd)


def _project(h, h_lo, w, slot):
    b, s, d = h.shape
    return _linear(h.reshape(b * s, d), h_lo.reshape(b * s, d), w, slot).reshape(b, s, w.shape[1])


def _rope_table(positions, dh):
    inv_freq = ROPE_THETA ** (-jnp.arange(0, dh, 2, dtype=jnp.float32) / dh)
    ang = positions.astype(jnp.float32)[:, None] * inv_freq
    reps = 128 // (dh // 2)
    return jnp.concatenate([jnp.tile(jnp.cos(ang), (1, reps)), jnp.tile(jnp.sin(ang), (1, reps))], axis=-1)


GATE_ROWS = 512


def _gate_terms(o_h, z_h):
    r = lax.rsqrt(jnp.mean(o_h * o_h, axis=-1, keepdims=True) + NORM_EPS)
    sig = jax.nn.sigmoid(z_h)
    return r, o_h * r, sig, z_h * sig


def _gate_fwd_call(o, z, nw):
    tok = pl.BlockSpec((GATE_ROWS, DN_V_W), lambda i: (i, 0))
    vec = pl.BlockSpec((1, DN_DV), lambda i: (0, 0))

    def body(o_ref, z_ref, nw_ref, y_ref):
        for h in range(DN_HEADS):
            cols = pl.ds(h * DN_DV, DN_DV)
            _, n_h, _, g_h = _gate_terms(o_ref[:, cols], z_ref[:, cols])
            y_ref[:, cols] = n_h * nw_ref[...] * g_h

    return pl.pallas_call(
        body, name="gate_fwd", grid=(o.shape[0] // GATE_ROWS,),
        out_shape=jax.ShapeDtypeStruct(o.shape, jnp.float32), in_specs=[tok, tok, vec], out_specs=tok,
        compiler_params=pltpu.CompilerParams(dimension_semantics=("parallel",)),
    )(o, z, nw)


def _gate_bwd_call(o, z, nw, dy):
    tok = pl.BlockSpec((GATE_ROWS, DN_V_W), lambda i: (i, 0))
    vec = pl.BlockSpec((1, DN_DV), lambda i: (0, 0))

    def body(o_ref, z_ref, nw_ref, dy_ref, do_ref, dz_ref, dnw_ref):
        @pl.when(pl.program_id(0) == 0)
        def _():
            dnw_ref[...] = jnp.zeros_like(dnw_ref)

        for h in range(DN_HEADS):
            cols = pl.ds(h * DN_DV, DN_DV)
            z_h, dy_h = z_ref[:, cols], dy_ref[:, cols]
            r, n_h, sig, g_h = _gate_terms(o_ref[:, cols], z_h)
            dz_ref[:, cols] = dy_h * n_h * nw_ref[...] * (sig * (1.0 + z_h * (1.0 - sig)))
            dn = dy_h * nw_ref[...] * g_h
            do_ref[:, cols] = r * (dn - n_h * jnp.mean(dn * n_h, axis=-1, keepdims=True))
            dnw_ref[...] += jnp.sum(dy_h * n_h * g_h, axis=0, keepdims=True)

    sd = jax.ShapeDtypeStruct
    return pl.pallas_call(
        body, name="gate_bwd", grid=(o.shape[0] // GATE_ROWS,),
        out_shape=(sd(o.shape, jnp.float32), sd(o.shape, jnp.float32), sd(nw.shape, jnp.float32)),
        in_specs=[tok, tok, vec, tok], out_specs=(tok, tok, vec),
        compiler_params=pltpu.CompilerParams(dimension_semantics=("arbitrary",)),
    )(o, z, nw, dy)


@jax.custom_vjp
def _gate_op(o, z, nw):
    return _gate_fwd_call(o, z, nw)


def _gate_op_fwd(o, z, nw):
    return _gate_fwd_call(o, z, nw), (o, z, nw)


def _gate_op_bwd(res, dy):
    return _gate_bwd_call(*res, dy)


_gate_op.defvjp(_gate_op_fwd, _gate_op_bwd)


_MASKED = -1e30


def _swa_probs(qs, k_h, sinks, valid):
    ss = [jnp.where(valid, _dot(q_h, k_h, 1, 1) * (SWA_DH ** -0.5), _MASKED) for q_h in qs]
    ms = [jnp.maximum(jnp.max(s, axis=-1, keepdims=True), sink) for s, sink in zip(ss, sinks)]
    ps = [jnp.exp(s - m) for s, m in zip(ss, ms)]
    es = [jnp.exp(sink - m) for sink, m in zip(sinks, ms)]
    invs = [1.0 / (jnp.sum(p, axis=-1, keepdims=True) + e) for p, e in zip(ps, es)]
    return [p * inv for p, inv in zip(ps, invs)], [e * inv for e, inv in zip(es, invs)]


def _swa_valid(n):
    qi = lax.broadcasted_iota(jnp.int32, (WINDOW, 2 * WINDOW), 0)
    kj = lax.broadcasted_iota(jnp.int32, (WINDOW, 2 * WINDOW), 1)
    diff = qi + WINDOW - kj
    return (diff >= 0) & (diff < WINDOW) & ((kj >= WINDOW) | (n > 0))


def _rotate_half(x, transpose=False):
    half = SWA_DH // 2
    lower = lax.broadcasted_iota(jnp.int32, x.shape, 1) % SWA_DH < half
    ahead, behind = pltpu.roll(x, 128 - half, 1), pltpu.roll(x, half, 1)
    return jnp.where(lower, ahead, -behind) if transpose else jnp.where(lower, -ahead, behind)


def _rope(x, table):
    return x * table[:, :128] + _rotate_half(x) * table[:, 128:]


def _unrope(dy, table):
    return dy * table[:, :128] + _rotate_half(dy * table[:, 128:], transpose=True)


def _swa_specs():
    qs = pl.BlockSpec((WINDOW, SWA_Q_W), lambda n: (n, 0))
    first = lambda n: jnp.maximum(n - 1, 0)
    kv = [pl.BlockSpec((WINDOW, SWA_KV_W), lambda n: (first(n), 0)), pl.BlockSpec((WINDOW, SWA_KV_W), lambda n: (n, 0)),
          pl.BlockSpec((WINDOW, SWA_KV_W), lambda n: (first(n), 1)), pl.BlockSpec((WINDOW, SWA_KV_W), lambda n: (n, 1))]
    tables = [pl.BlockSpec((WINDOW, 256), lambda n: (first(n), 0)), pl.BlockSpec((WINDOW, 256), lambda n: (n, 0))]
    cur = pl.BlockSpec((WINDOW, SWA_KV_W), lambda n: (n, 0))
    sk = pl.BlockSpec((SWA_HEADS, 1, 128), lambda n: (0, 0, 0))
    return qs, kv, tables, cur, sk


def _swa_load(q_ref, kp_ref, kc_ref, vp_ref, vc_ref, tp_ref, tc_ref):
    table_kk = jnp.concatenate([tp_ref[...], tc_ref[...]], axis=0)
    kk = _rope(jnp.concatenate([kp_ref[...], kc_ref[...]], axis=0), table_kk)
    vv = jnp.concatenate([vp_ref[...], vc_ref[...]], axis=0)
    q_rot = []
    for b in range(SWA_Q_W // 128):
        pair = _rope(q_ref[:, pl.ds(b * 128, 128)], tc_ref[...])
        q_rot += [pair[:, :SWA_DH], pair[:, SWA_DH:]]
    split = lambda t: [t[:, hkv * SWA_DH:(hkv + 1) * SWA_DH] for hkv in range(SWA_KV_HEADS)]
    return q_rot, split(kk), split(vv), table_kk


def _swa_fwd_call(q, kv, table, sinks):
    qs, kvs, tables, _, sk = _swa_specs()

    def body(q_ref, kp_ref, kc_ref, vp_ref, vc_ref, tp_ref, tc_ref, sink_ref, o_ref):
        valid = _swa_valid(pl.program_id(0))
        q_rot, kk, vv, _ = _swa_load(q_ref, kp_ref, kc_ref, vp_ref, vc_ref, tp_ref, tc_ref)
        for hkv in range(SWA_KV_HEADS):
            heads = range(hkv * SWA_GROUP, (hkv + 1) * SWA_GROUP)
            probs, _ = _swa_probs([q_rot[h] for h in heads], kk[hkv], [sink_ref[h][:, :1] for h in heads], valid)
            outs = [_dot(p, vv[hkv]) for p in probs]
            for h, o in zip(heads, outs):
                o_ref[:, pl.ds(h * SWA_DH, SWA_DH)] = o

    return pl.pallas_call(
        body, name="swa_fwd", grid=(q.shape[0] // WINDOW,),
        out_shape=jax.ShapeDtypeStruct(q.shape, jnp.float32),
        in_specs=[qs] + kvs + tables + [sk], out_specs=qs,
        compiler_params=pltpu.CompilerParams(dimension_semantics=("parallel",)),
    )(q, kv, kv, kv, kv, table, table, sinks)


def _swa_bwd_call(q, kv, table, sinks, do):
    qs, kvs, tables, cur, sk = _swa_specs()

    def body(q_ref, kp_ref, kc_ref, vp_ref, vc_ref, tp_ref, tc_ref, sink_ref, do_ref,
             dq_ref, dkc_ref, dkp_ref, dvc_ref, dvp_ref, ds_ref):
        @pl.when(pl.program_id(0) == 0)
        def _():
            ds_ref[...] = jnp.zeros_like(ds_ref)

        valid = _swa_valid(pl.program_id(0))
        q_rot, kk, vv, table_kk = _swa_load(q_ref, kp_ref, kc_ref, vp_ref, vc_ref, tp_ref, tc_ref)
        lane0 = lax.broadcasted_iota(jnp.int32, (1, 128), 1) == 0
        dq_heads, dk_heads, dv_heads = [], [], []
        for hkv in range(SWA_KV_HEADS):
            k_h, v_h = kk[hkv], vv[hkv]
            heads = range(hkv * SWA_GROUP, (hkv + 1) * SWA_GROUP)
            q_hs = [q_rot[h] for h in heads]
            dos = [do_ref[:, pl.ds(h * SWA_DH, SWA_DH)] for h in heads]
            probs, p_sinks = _swa_probs(q_hs, k_h, [sink_ref[h][:, :1] for h in heads], valid)
            dps = [_dot(do_h, v_h, 1, 1) for do_h in dos]
            rss = [jnp.sum(p * dp, axis=-1, keepdims=True) for p, dp in zip(probs, dps)]
            d_ss = [p * (dp - rs) for p, dp, rs in zip(probs, dps, rss)]
            dq_heads += [_dot(d_s, k_h) * (SWA_DH ** -0.5) for d_s in d_ss]
            dks = [_dot(d_s, q_h, 0, 0) for d_s, q_h in zip(d_ss, q_hs)]
            dvs = [_dot(p, do_h, 0, 0) for p, do_h in zip(probs, dos)]
            for h, p_sink, rs in zip(heads, p_sinks, rss):
                d_sink = -jnp.sum(p_sink * rs, axis=0, keepdims=True)
                ds_ref[h] += jnp.where(lane0, d_sink, 0.0)
            dk_heads.append(sum(dks[1:], dks[0]) * (SWA_DH ** -0.5))
            dv_heads.append(sum(dvs[1:], dvs[0]))
        for b in range(SWA_Q_W // 128):
            pair = jnp.concatenate([dq_heads[2 * b], dq_heads[2 * b + 1]], axis=1)
            dq_ref[:, pl.ds(b * 128, 128)] = _unrope(pair, tc_ref[...])
        dk = _unrope(jnp.concatenate(dk_heads, axis=1), table_kk)
        dv = jnp.concatenate(dv_heads, axis=1)
        dkp_ref[...] = dk[:WINDOW]
        dkc_ref[...] = dk[WINDOW:]
        dvp_ref[...] = dv[:WINDOW]
        dvc_ref[...] = dv[WINDOW:]

    sd = jax.ShapeDtypeStruct
    f32 = jnp.float32
    half = (q.shape[0], SWA_KV_W)
    return pl.pallas_call(
        body, name="swa_bwd", grid=(q.shape[0] // WINDOW,),
        out_shape=(sd(q.shape, f32), sd(half, f32), sd(half, f32), sd(half, f32), sd(half, f32), sd(sinks.shape, f32)),
        in_specs=[qs] + kvs + tables + [sk, qs], out_specs=(qs, cur, cur, cur, cur, sk),
        compiler_params=pltpu.CompilerParams(dimension_semantics=("arbitrary",)),
    )(q, kv, kv, kv, kv, table, table, sinks, do)


@jax.custom_vjp
def _swa_op(q, kv, table, sinks):
    return _swa_fwd_call(q, kv, table, sinks)


def _swa_op_fwd(q, kv, table, sinks):
    return _swa_fwd_call(q, kv, table, sinks), (q, kv, table, sinks)


def _swa_op_bwd(res, do):
    q, kv, table, sinks = res
    dq, dkc, dkp, dvc, dvp, dsinks = _swa_bwd_call(q, kv, table, sinks, do)

    def fold(cur, prev):
        return cur + jnp.concatenate([prev[WINDOW:], jnp.zeros_like(prev[:WINDOW])], axis=0)

    return dq, jnp.concatenate([fold(dkc, dkp), fold(dvc, dvp)], axis=1), jnp.zeros_like(table), dsinks


_swa_op.defvjp(_swa_op_fwd, _swa_op_bwd)


def _swa_sink_attention(q, kv, table, sinks):
    return _swa_op(q, kv, table, jnp.broadcast_to(sinks[:, None, None], (SWA_HEADS, 1, 128)))


MEM_ROWS = 512


def _mem_probs(q_h, k_h):
    s = _dot(q_h, k_h, 1, 1) * (MEM_DH ** -0.5)
    p = jnp.exp(s - jnp.max(s, axis=-1, keepdims=True))
    return p / jnp.sum(p, axis=-1, keepdims=True)


def _mem_fwd_call(qm, kv):
    qs = pl.BlockSpec((MEM_ROWS, MEM_W), lambda i: (i, 0))
    kvs = pl.BlockSpec(kv.shape, lambda i: (0, 0))

    def body(q_ref, kv_ref, o_ref):
        for h in range(MEM_HEADS):
            cols = pl.ds(h * MEM_DH, MEM_DH)
            probs = _mem_probs(q_ref[:, cols], kv_ref[:, cols])
            o_ref[:, cols] = _dot(probs, kv_ref[:, pl.ds(MEM_W + h * MEM_DH, MEM_DH)])

    return pl.pallas_call(
        body, name="mem_fwd", grid=(qm.shape[0] // MEM_ROWS,),
        out_shape=jax.ShapeDtypeStruct(qm.shape, jnp.float32), in_specs=[qs, kvs], out_specs=qs,
        compiler_params=pltpu.CompilerParams(dimension_semantics=("parallel",)),
    )(qm, kv)


def _mem_bwd_call(qm, kv, do):
    qs = pl.BlockSpec((MEM_ROWS, MEM_W), lambda i: (i, 0))
    kvs = pl.BlockSpec(kv.shape, lambda i: (0, 0))

    def body(q_ref, kv_ref, do_ref, dq_ref, dkv_ref):
        @pl.when(pl.program_id(0) == 0)
        def _():
            dkv_ref[...] = jnp.zeros_like(dkv_ref)

        for h in range(MEM_HEADS):
            cols = pl.ds(h * MEM_DH, MEM_DH)
            v_cols = pl.ds(MEM_W + h * MEM_DH, MEM_DH)
            q_h, k_h, do_h = q_ref[:, cols], kv_ref[:, cols], do_ref[:, cols]
            probs = _mem_probs(q_h, k_h)
            dp = _dot(do_h, kv_ref[:, v_cols], 1, 1)
            d_s = probs * (dp - jnp.sum(probs * dp, axis=-1, keepdims=True))
            dq_ref[:, cols] = _dot(d_s, k_h) * (MEM_DH ** -0.5)
            dkv_ref[:, cols] += _dot(d_s, q_h, 0, 0) * (MEM_DH ** -0.5)
            dkv_ref[:, v_cols] += _dot(probs, do_h, 0, 0)

    sd = jax.ShapeDtypeStruct
    return pl.pallas_call(
        body, name="mem_bwd", grid=(qm.shape[0] // MEM_ROWS,),
        out_shape=(sd(qm.shape, jnp.float32), sd(kv.shape, jnp.float32)),
        in_specs=[qs, kvs, qs], out_specs=(qs, kvs),
        compiler_params=pltpu.CompilerParams(dimension_semantics=("arbitrary",)),
    )(qm, kv, do)


@jax.custom_vjp
def _mem_op(qm, kv):
    return _mem_fwd_call(qm, kv)


def _mem_op_fwd(qm, kv):
    return _mem_fwd_call(qm, kv), (qm, kv)


def _mem_op_bwd(res, do):
    return _mem_bwd_call(*res, do)


_mem_op.defvjp(_mem_op_fwd, _mem_op_bwd)


def _memory_attention(qm, kv):
    return _mem_op(qm[0], kv[0])[None]


def _mixer_a(h, h_lo, mem, mem_lo, p, s, layer):
    B, S, _ = h.shape
    proj = _project(h, h_lo, p["a_w_in"][layer], s["a_w_in"][layer])
    c1 = 2 * DN_QK_W + DN_V_W
    qkv = proj[..., :c1]
    z = proj[..., c1:QKVZ_W]
    qm = proj[..., QKVZ_W:QKVZ_W + MEM_W]
    a = proj[..., QKVZ_W + MEM_W:QKVZ_W + MEM_W + DN_HEADS]
    b = proj[..., QKVZ_W + MEM_W + DN_HEADS:QKVZ_W + MEM_W + 2 * DN_HEADS]
    planes = _pre_op(qkv[0], p["a_conv_w"][layer])
    beta = jax.nn.sigmoid(b[0])
    g = -jnp.exp(p["a_A_log"][layer]) * jax.nn.softplus(a[0] + p["a_dt_bias"][layer])
    o = _gate_op(_gated_delta_rule(planes, g, beta), z[0], p["a_norm_w"][layer][None])[None]
    kv = _project(mem, mem_lo, p["mem_w_kv"][layer], s["mem_w_kv"][layer])
    mo = _memory_attention(qm, kv)
    cat = jnp.concatenate([o, mo], axis=-1)
    return _project(cat, _lo(cat), p["w_o"][layer], s["w_o"][layer])


def _mixer_b(h, h_lo, mem, mem_lo, kv_shared, table, p, s, layer):
    j = layer - N_A
    proj = _project(h, h_lo, p["b_w_in"][j], s["b_w_in"][j])
    o = _swa_sink_attention(proj[0, :, :SWA_Q_W], kv_shared, table, p["b_sinks"][j])[None]
    kv = _project(mem, mem_lo, p["mem_w_kv"][layer], s["mem_w_kv"][layer])
    mo = _memory_attention(proj[..., SWA_Q_W:], kv)
    cat = jnp.concatenate([o, mo], axis=-1)
    return _project(cat, _lo(cat), p["w_o"][layer], s["w_o"][layer])


def _forward(p, s, x, mem, positions):
    table = _rope_table(positions[0], SWA_DH)
    h, h_lo, mem_lo = x, _lo(x), _lo(mem)
    kv_shared = None
    for layer in range(DEPTH):
        if layer < N_A:
            mix = _mixer_a(h, h_lo, mem, mem_lo, p, s, layer)
        else:
            mix = _mixer_b(h, h_lo, mem, mem_lo, kv_shared, table, p, s, layer)
        seq = h.shape[1]
        h2, h2_lo = _ln_res(h[0], mix[0], p["ln_g"][layer, 0][None], p["ln_b"][layer, 0][None])
        down = _mlp(h2, h2_lo, p["mlp_w_up"][layer], p["mlp_w_down"][layer], s["mlp_w_up"][layer],
                    s["mlp_w_down"][layer])
        h, h_lo = _ln_res(h2, down, p["ln_g"][layer, 1][None], p["ln_b"][layer, 1][None])
        h, h_lo = h.reshape(1, seq, D_MODEL), h_lo.reshape(1, seq, D_MODEL)
        if layer == N_A - 1:
            kv_shared = _project(h, h_lo, p["w_kv_shared"], s["w_kv_shared"])[0]
    return h


def _loss(diff, s, p, mem, positions, target):
    y = _forward({**p, **diff["small"]}, s, diff["x"], mem, positions)
    return 0.5 * jnp.sum(jnp.mean(jnp.square(y - target), axis=-1))


def _reorder_a_w_in(w):
    pad = jnp.zeros(w.shape[:-1] + (A_IN_PAD - A_IN,), w.dtype)
    return jnp.concatenate([w[..., :QKVZ_W], w[..., QKVZ_W + 2 * DN_HEADS:], w[..., QKVZ_W:QKVZ_W + 2 * DN_HEADS], pad],
                           axis=-1)


def _restore_a_w_in(w):
    return jnp.concatenate([w[..., :QKVZ_W], w[..., QKVZ_W + MEM_W:QKVZ_W + MEM_W + 2 * DN_HEADS],
                            w[..., QKVZ_W:QKVZ_W + MEM_W]], axis=-1)


def kernel(x, mem, positions, a_w_in, a_conv_w, a_A_log, a_dt_bias, a_norm_w, b_w_in, b_sinks, w_kv_shared, mem_w_kv, w_o, mlp_w_up, mlp_w_down, ln_g, ln_b, loss_target, m_a_w_in, m_a_conv_w, m_a_A_log, m_a_dt_bias, m_a_norm_w, m_b_w_in, m_b_sinks, m_w_kv_shared, m_mem_w_kv, m_w_o, m_mlp_w_up, m_mlp_w_down, m_ln_g, m_ln_b, v_a_w_in, v_a_conv_w, v_a_A_log, v_a_dt_bias, v_a_norm_w, v_b_w_in, v_b_sinks, v_w_kv_shared, v_mem_w_kv, v_w_o, v_mlp_w_up, v_mlp_w_down, v_ln_g, v_ln_b):
    w_sh = dict(a_w_in=a_w_in, a_conv_w=a_conv_w, a_A_log=a_A_log, a_dt_bias=a_dt_bias, a_norm_w=a_norm_w,
                b_w_in=b_w_in, b_sinks=b_sinks, w_kv_shared=w_kv_shared, mem_w_kv=mem_w_kv, w_o=w_o,
                mlp_w_up=mlp_w_up, mlp_w_down=mlp_w_down, ln_g=ln_g, ln_b=ln_b)
    m_sh = dict(a_w_in=m_a_w_in, a_conv_w=m_a_conv_w, a_A_log=m_a_A_log, a_dt_bias=m_a_dt_bias, a_norm_w=m_a_norm_w,
                b_w_in=m_b_w_in, b_sinks=m_b_sinks, w_kv_shared=m_w_kv_shared, mem_w_kv=m_mem_w_kv, w_o=m_w_o,
                mlp_w_up=m_mlp_w_up, mlp_w_down=m_mlp_w_down, ln_g=m_ln_g, ln_b=m_ln_b)
    v_sh = dict(a_w_in=v_a_w_in, a_conv_w=v_a_conv_w, a_A_log=v_a_A_log, a_dt_bias=v_a_dt_bias, a_norm_w=v_a_norm_w,
                b_w_in=v_b_w_in, b_sinks=v_b_sinks, w_kv_shared=v_w_kv_shared, mem_w_kv=v_mem_w_kv, w_o=v_w_o,
                mlp_w_up=v_mlp_w_up, mlp_w_down=v_mlp_w_down, ln_g=v_ln_g, ln_b=v_ln_b)
    shard_shapes = {n: w_sh[n].shape for n in WEIGHTS}
    rb, rows = _rows_for(w_sh)

    big, small = _pack(w_sh, rb, jnp.bfloat16)
    gbig, gsmall = _gather_weights(big.reshape(2, rb // 2, FLAT_W), small.reshape(2, SMALL_ROWS // 2, FLAT_W))
    gbig, gsmall = gbig.reshape(N_CHIPS, rb, FLAT_W), gsmall.reshape(N_CHIPS, SMALL_ROWS, FLAT_W)
    pieces = [_unpack(gbig[q], gsmall[q], shard_shapes) for q in range(N_CHIPS)]
    full = {n: jnp.concatenate([pieces[q][n] for q in range(N_CHIPS)], axis=SHARD_AXIS[n]) for n in SHARD_AXIS}
    for n in REPLICATED:
        full[n] = w_sh[n]
    big_w = {n: full[n] for n in BIG}
    big_w["a_w_in"] = _reorder_a_w_in(big_w["a_w_in"])
    small_w = {n: full[n] for n in SMALL}
    slots = {n: jnp.zeros(big_w[n].shape, jnp.float32) for n in BIG}

    loss, (grads, g_slots) = jax.value_and_grad(_loss, argnums=(0, 1))(
        {"x": x, "small": small_w}, slots, big_w, mem, positions, loss_target)
    loss = lax.psum(loss, ("x", "y", "c"))
    g_full = {**g_slots, **grads["small"]}
    g_full["a_w_in"] = _restore_a_w_in(g_full["a_w_in"])

    def shard_of(n, q):
        if n in REPLICATED:
            return g_full[n]
        size = shard_shapes[n][SHARD_AXIS[n]]
        return lax.slice_in_dim(g_full[n], q * size, (q + 1) * size, axis=SHARD_AXIS[n])

    parts = []
    for q in range(N_CHIPS):
        pb, ps = _pack({n: shard_of(n, q) for n in WEIGHTS}, rb, jnp.bfloat16)
        parts.append(jnp.concatenate([pb, ps.astype(jnp.bfloat16)], axis=0).reshape(2, rows // 2, FLAT_W))
    partials = jnp.stack(parts, axis=1)
    half = lax.axis_index("c").astype(jnp.int32).reshape(1)
    chip_partials = _add_pairs(partials, _swap_halves(partials), half)
    g_flat = _join_halves(_sum_chips(_scatter_grads(chip_partials))).reshape(rows, FLAT_W)

    odd = "a_w_in"
    blank = jnp.zeros(shard_shapes[odd], jnp.float32)
    flat = [jnp.concatenate(_pack({**d, odd: blank}, rb), axis=0) for d in (w_sh, m_sh, v_sh)]
    outs = (g_flat,) + tuple(_adamw(g_flat, *flat))
    g_o, d_o, m_o, v_o = [_unpack(o[:rb], o[rb:], shard_shapes) for o in outs]
    as_rows = lambda t: t.reshape(-1, t.shape[-1])
    updated = _adamw(as_rows(g_o[odd]), as_rows(w_sh[odd]), as_rows(m_sh[odd]), as_rows(v_sh[odd]))
    d_o[odd], m_o[odd], v_o[odd] = [t.reshape(shard_shapes[odd]) for t in updated]
    return (loss, grads["x"], *[g_o[n] for n in WEIGHTS], *[d_o[n] for n in WEIGHTS],
            *[m_o[n] for n in WEIGHTS], *[v_o[n] for n in WEIGHTS])
```

```python
import functools
import math

import jax
import jax.numpy as jnp
from jax import lax
from jax.experimental import pallas as pl
from jax.experimental.pallas import tpu as pltpu

D_MODEL = 1024
DEPTH = 4
N_A = DEPTH // 2
MEM_HEADS = 4
MEM_DH = D_MODEL // 16
MEM_W = MEM_HEADS * MEM_DH
DN_DK = 128
DN_DV = 128
DN_HEADS = (3 * D_MODEL) // (4 * DN_DV)
DN_QK_W = DN_HEADS * DN_DK
DN_V_W = DN_HEADS * DN_DV
CONV_WIDTH = 4
CHUNK = 64
SWA_DH = 64
SWA_HEADS = (3 * D_MODEL) // (4 * SWA_DH)
SWA_KV_HEADS = 2
SWA_GROUP = SWA_HEADS // SWA_KV_HEADS
SWA_Q_W = SWA_HEADS * SWA_DH
SWA_KV_W = SWA_KV_HEADS * SWA_DH
WINDOW = 128
ROPE_THETA = 10000.0
LN_EPS = 1e-5
NORM_EPS = 1e-6
DN_ALPHA = (2.0 * DEPTH) ** 0.25
A_IN = 2 * DN_QK_W + 2 * DN_V_W + 2 * DN_HEADS + MEM_W
A_IN_PAD = 3456
QKVZ_W = 2 * DN_QK_W + 2 * DN_V_W

ADAM_LR = 0.001
ADAM_B1 = 0.9
ADAM_B2 = 0.999
ADAM_EPS = 1e-08
ADAM_WD = 0.01
ADAM_STEP = 10

N_CHIPS = 4
FLAT_W = 1024
BIG = ("a_w_in", "b_w_in", "w_kv_shared", "mem_w_kv", "w_o", "mlp_w_up", "mlp_w_down")
SMALL = ("a_conv_w", "ln_g", "ln_b", "a_A_log", "a_dt_bias", "a_norm_w", "b_sinks")
REPLICATED = ("a_A_log", "a_dt_bias", "a_norm_w", "b_sinks")
WEIGHTS = ("a_w_in", "a_conv_w", "a_A_log", "a_dt_bias", "a_norm_w", "b_w_in", "b_sinks", "w_kv_shared",
           "mem_w_kv", "w_o", "mlp_w_up", "mlp_w_down", "ln_g", "ln_b")
SHARD_AXIS = {"a_w_in": 2, "a_conv_w": 2, "b_w_in": 1, "w_kv_shared": 0, "mem_w_kv": 1, "w_o": 1,
              "mlp_w_up": 2, "mlp_w_down": 1, "ln_g": 2, "ln_b": 2}
SMALL_ROWS = 32
ROW_ALIGN = 256

MESH = pl.DeviceIdType.MESH
HBM_SPEC = pl.BlockSpec(memory_space=pltpu.HBM)
VMEM_LIMIT = 48 * 1024 * 1024


def _rows_for(shards):
    n_big = sum(math.prod(shards[n].shape) for n in BIG)
    n_small = sum(math.prod(shards[n].shape) for n in SMALL)
    assert n_small <= SMALL_ROWS * FLAT_W
    total = -(-n_big // FLAT_W) + SMALL_ROWS
    total = -(-total // (2 * ROW_ALIGN)) * (2 * ROW_ALIGN)
    return total - SMALL_ROWS, total


def _pack(shards, rb, dtype_big=jnp.float32):
    big = jnp.concatenate([shards[n].reshape(-1).astype(dtype_big) for n in BIG])
    big = jnp.pad(big, (0, rb * FLAT_W - big.shape[0])).reshape(rb, FLAT_W)
    small = jnp.concatenate([shards[n].reshape(-1).astype(jnp.float32) for n in SMALL])
    small = jnp.pad(small, (0, SMALL_ROWS * FLAT_W - small.shape[0])).reshape(SMALL_ROWS, FLAT_W)
    return big, small


def _unpack(big, small, shapes):
    out = {}
    for flat, names in ((big.reshape(-1), BIG), (small.reshape(-1), SMALL)):
        off = 0
        for n in names:
            size = math.prod(shapes[n])
            out[n] = flat[off:off + size].reshape(shapes[n])
            off += size
    return out


def _other_chips(x, y):
    return [(1 - x, y), (x, 1 - y), (1 - x, 1 - y)]


def _gather_weights(big, small):
    def body(big_ref, small_ref, init_big_ref, init_small_ref, obig_ref, osmall_ref,
             send_sems, recv_sems, pass_send_sems, pass_recv_sems):
        del init_big_ref, init_small_ref
        x, y, c = lax.axis_index("x"), lax.axis_index("y"), lax.axis_index("c")
        me = 2 * x + y
        sibling = (x, y, 1 - c)
        pairs = ((big_ref, obig_ref), (small_ref, osmall_ref))
        sends = []
        for j, (px, py) in enumerate(_other_chips(x, y)):
            for i, (src, dst) in enumerate(pairs):
                sends.append(pltpu.make_async_remote_copy(
                    src_ref=src.at[c], dst_ref=dst.at[me, c], send_sem=send_sems.at[2 * j + i],
                    recv_sem=recv_sems.at[2 * j + i], device_id=(px, py, c), device_id_type=MESH))
        for cp in sends:
            cp.start()
        passed = []
        for j, (px, py) in enumerate(_other_chips(x, y)):
            for i, (src, dst) in enumerate(pairs):
                landed = dst.at[2 * px + py, c]
                pltpu.make_async_remote_copy(
                    src_ref=src.at[c], dst_ref=landed, send_sem=send_sems.at[2 * j + i],
                    recv_sem=recv_sems.at[2 * j + i], device_id=(px, py, c), device_id_type=MESH).wait_recv()
                passed.append(pltpu.make_async_remote_copy(
                    src_ref=landed, dst_ref=landed, send_sem=pass_send_sems.at[2 * j + i],
                    recv_sem=pass_recv_sems.at[2 * j + i], device_id=sibling, device_id_type=MESH))
                passed[-1].start()
        for j, (px, py) in enumerate(_other_chips(x, y)):
            for i, (src, dst) in enumerate(pairs):
                other_half = dst.at[2 * px + py, 1 - c]
                pltpu.make_async_remote_copy(
                    src_ref=other_half, dst_ref=other_half, send_sem=pass_send_sems.at[2 * j + i],
                    recv_sem=pass_recv_sems.at[2 * j + i], device_id=sibling, device_id_type=MESH).wait_recv()
        for cp in sends + passed:
            cp.wait_send()

    dma6 = pltpu.SemaphoreType.DMA((6,))
    four = lambda t: jnp.broadcast_to(t[None], (N_CHIPS,) + t.shape)
    return pl.pallas_call(
        body, name="gather_weights",
        out_shape=(jax.ShapeDtypeStruct((N_CHIPS,) + big.shape, big.dtype),
                   jax.ShapeDtypeStruct((N_CHIPS,) + small.shape, small.dtype)),
        in_specs=[HBM_SPEC] * 4, out_specs=(HBM_SPEC, HBM_SPEC), input_output_aliases={2: 0, 3: 1},
        scratch_shapes=[dma6, dma6, dma6, dma6],
    )(big, small, four(big), four(small))


def _scatter_grads(g):
    def body(g_ref, o_ref, send_sems, recv_sems, local_sem):
        x, y, c = lax.axis_index("x"), lax.axis_index("y"), lax.axis_index("c")
        me = 2 * x + y
        local = pltpu.make_async_copy(g_ref.at[me], o_ref.at[me], local_sem)
        local.start()
        sends = []
        for j, (px, py) in enumerate(_other_chips(x, y)):
            sends.append(pltpu.make_async_remote_copy(
                src_ref=g_ref.at[2 * px + py], dst_ref=o_ref.at[me], send_sem=send_sems.at[j], recv_sem=recv_sems.at[j],
                device_id=(px, py, c), device_id_type=MESH))
        for cp in sends:
            cp.start()
        for j, (px, py) in enumerate(_other_chips(x, y)):
            pltpu.make_async_remote_copy(
                src_ref=g_ref.at[me], dst_ref=o_ref.at[2 * px + py], send_sem=send_sems.at[j], recv_sem=recv_sems.at[j],
                device_id=(px, py, c), device_id_type=MESH).wait_recv()
        for cp in sends:
            cp.wait_send()
        local.wait()

    return pl.pallas_call(
        body, name="scatter_grads",
        out_shape=jax.ShapeDtypeStruct(g.shape, g.dtype),
        in_specs=[HBM_SPEC], out_specs=HBM_SPEC,
        scratch_shapes=[pltpu.SemaphoreType.DMA((3,)), pltpu.SemaphoreType.DMA((3,)), pltpu.SemaphoreType.DMA],
    )(g)


def _swap_halves(g):
    def body(g_ref, o_ref, send_sem, recv_sem):
        x, y, c = lax.axis_index("x"), lax.axis_index("y"), lax.axis_index("c")
        cp = pltpu.make_async_remote_copy(src_ref=g_ref.at[1 - c], dst_ref=o_ref, send_sem=send_sem, recv_sem=recv_sem,
                                          device_id=(x, y, 1 - c), device_id_type=MESH)
        cp.start()
        cp.wait()

    return pl.pallas_call(
        body, name="swap_halves",
        out_shape=jax.ShapeDtypeStruct(g.shape[1:], g.dtype),
        in_specs=[HBM_SPEC], out_specs=HBM_SPEC,
        scratch_shapes=[pltpu.SemaphoreType.DMA, pltpu.SemaphoreType.DMA],
    )(g)


def _join_halves(v):
    def body(v_ref, init_ref, o_ref, send_sem, recv_sem):
        del init_ref
        x, y, c = lax.axis_index("x"), lax.axis_index("y"), lax.axis_index("c")
        cp = pltpu.make_async_remote_copy(src_ref=v_ref, dst_ref=o_ref.at[c], send_sem=send_sem, recv_sem=recv_sem,
                                          device_id=(x, y, 1 - c), device_id_type=MESH)
        cp.start()
        cp.wait_send()
        pltpu.make_async_remote_copy(src_ref=v_ref, dst_ref=o_ref.at[1 - c], send_sem=send_sem, recv_sem=recv_sem,
                                     device_id=(x, y, 1 - c), device_id_type=MESH).wait_recv()

    return pl.pallas_call(
        body, name="join_halves",
        out_shape=jax.ShapeDtypeStruct((2,) + v.shape, v.dtype),
        in_specs=[HBM_SPEC, HBM_SPEC], out_specs=HBM_SPEC, input_output_aliases={1: 0},
        scratch_shapes=[pltpu.SemaphoreType.DMA, pltpu.SemaphoreType.DMA],
    )(v, jnp.stack([v, v]))


def _add_pairs(g, theirs, half):
    _, n, rows, width = g.shape
    assert rows % ROW_ALIGN == 0, rows

    def body(half_ref, g_ref, t_ref, o_ref):
        o_ref[...] = (g_ref[...].astype(jnp.float32) + t_ref[...].astype(jnp.float32)).astype(o_ref.dtype)

    blk = pl.BlockSpec((None, ROW_ALIGN, width), lambda p, i, h: (p, i, 0))
    grid_spec = pltpu.PrefetchScalarGridSpec(
        num_scalar_prefetch=1, grid=(n, rows // ROW_ALIGN),
        in_specs=[pl.BlockSpec((None, None, ROW_ALIGN, width), lambda p, i, h: (h[0], p, i, 0)), blk], out_specs=blk)
    return pl.pallas_call(
        body, name="add_pairs", grid_spec=grid_spec, out_shape=jax.ShapeDtypeStruct(theirs.shape, g.dtype),
        compiler_params=pltpu.CompilerParams(dimension_semantics=("parallel", "parallel")),
    )(half, g, theirs)


def _sum_chips(parts):
    n, rows, width = parts.shape
    assert rows % ROW_ALIGN == 0, rows

    def body(p_ref, o_ref):
        p = [p_ref[q].astype(jnp.float32) for q in range(n)]
        o_ref[...] = (p[0] + p[1]) + (p[2] + p[3])

    return pl.pallas_call(
        body, name="sum_chips", grid=(rows // ROW_ALIGN,),
        out_shape=jax.ShapeDtypeStruct((rows, width), jnp.float32),
        in_specs=[pl.BlockSpec((n, ROW_ALIGN, width), lambda i: (0, i, 0))],
        out_specs=pl.BlockSpec((ROW_ALIGN, width), lambda i: (i, 0)),
        compiler_params=pltpu.CompilerParams(dimension_semantics=("parallel",), vmem_limit_bytes=VMEM_LIMIT),
    )(parts)


def _adamw(g, w, m, v):
    rows, width = w.shape
    blk = ROW_ALIGN // 2

    def body(g_ref, w_ref, m_ref, v_ref, d_out, m_out, v_out):
        g = g_ref[...]
        m_new = ADAM_B1 * m_ref[...] + (1.0 - ADAM_B1) * g
        v_new = ADAM_B2 * v_ref[...] + (1.0 - ADAM_B2) * jnp.square(g)
        m_hat = m_new / (1.0 - ADAM_B1 ** ADAM_STEP)
        v_hat = v_new / (1.0 - ADAM_B2 ** ADAM_STEP)
        d_out[...] = -ADAM_LR * (m_hat / (jnp.sqrt(v_hat) + ADAM_EPS) + ADAM_WD * w_ref[...])
        m_out[...] = m_new
        v_out[...] = v_new

    spec = pl.BlockSpec((blk, width), lambda i: (i, 0))
    shape = jax.ShapeDtypeStruct((rows, width), jnp.float32)
    return pl.pallas_call(
        body, name="adamw", grid=(rows // blk,),
        out_shape=(shape,) * 3, in_specs=[spec] * 4, out_specs=(spec,) * 3,
        compiler_params=pltpu.CompilerParams(dimension_semantics=("parallel",), vmem_limit_bytes=VMEM_LIMIT),
    )(g, w, m, v)


def _tile(dim, pref):
    if dim <= pref:
        return dim
    for t in range(pref - pref % 128, 0, -128):
        if dim % t == 0:
            return t
    raise ValueError(f"no 128-aligned tile for {dim}")


def _matmul(a, b, *, ta=False, tb=False, name, epilogue=None, extra=None, out_dtype=jnp.float32):
    (k_a, m) = a.shape if ta else a.shape[::-1]
    (k_b, n) = b.shape[::-1] if tb else b.shape
    assert k_a == k_b, (a.shape, b.shape, ta, tb)
    k = k_a
    tk = _tile(k, 1152)
    nk = k // tk
    if ta:
        tm, tn = _tile(m, 1024), _tile(n, 2048 if m <= 1024 else 1024)
    else:
        tm, tn = _tile(m, 2048), _tile(n, 1152)
    a_spec = pl.BlockSpec((tk, tm), lambda i, j, l: (l, i)) if ta else pl.BlockSpec((tm, tk), lambda i, j, l: (i, l))
    b_spec = pl.BlockSpec((tn, tk), lambda i, j, l: (j, l)) if tb else pl.BlockSpec((tk, tn), lambda i, j, l: (l, j))
    o_spec = pl.BlockSpec((tm, tn), lambda i, j, l: (i, j))
    dims = (((0 if ta else 1,), (1 if tb else 0,)), ((), ()))
    has_extra = epilogue == "relu2_grad"
    assert has_extra == (extra is not None)

    def body(*refs):
        a_ref, b_ref = refs[:2]
        outs = refs[2 + has_extra:2 + has_extra + (2 if epilogue == "relu2" else 1)]
        l = pl.program_id(2)
        part = lax.dot_general(a_ref[...].astype(jnp.bfloat16), b_ref[...].astype(jnp.bfloat16), dims,
                               preferred_element_type=jnp.float32)

        def finish(acc):
            if epilogue is None:
                outs[0][...] = acc.astype(out_dtype)
            elif epilogue == "relu2":
                outs[0][...] = acc.astype(jnp.bfloat16)
                outs[1][...] = jnp.square(jnp.maximum(acc, 0.0)).astype(jnp.bfloat16)
            else:
                outs[0][...] = (acc * (2.0 * jnp.maximum(refs[2][...].astype(jnp.float32), 0.0))).astype(out_dtype)

        if nk == 1:
            finish(part)
            return
        acc_ref = refs[-1]

        @pl.when(l == 0)
        def _():
            acc_ref[...] = part

        @pl.when((l > 0) & (l < nk - 1))
        def _():
            acc_ref[...] += part

        @pl.when(l == nk - 1)
        def _():
            finish(acc_ref[...] + part)

    if epilogue == "relu2":
        out_shape = (jax.ShapeDtypeStruct((m, n), jnp.bfloat16),) * 2
        out_specs = (o_spec, o_spec)
    else:
        out_shape = jax.ShapeDtypeStruct((m, n), out_dtype)
        out_specs = o_spec
    return pl.pallas_call(
        body, name=name, grid=(m // tm, n // tn, nk), out_shape=out_shape,
        in_specs=[a_spec, b_spec] + ([o_spec] if has_extra else []), out_specs=out_specs,
        scratch_shapes=[pltpu.VMEM((tm, tn), jnp.float32)] if nk > 1 else [],
        compiler_params=pltpu.CompilerParams(dimension_semantics=("parallel", "parallel", "arbitrary"),
                                             vmem_limit_bytes=VMEM_LIMIT),
    )(*((a, b) + ((extra,) if has_extra else ())))


def _lo(x):
    return lax.stop_gradient(x.astype(jnp.bfloat16))


def _concat_lo(a, b):
    s, wa = a.shape
    wb = b.shape[1]
    rows = 512

    def body(a_ref, b_ref, o_ref):
        o_ref[:, :wa] = a_ref[...].astype(o_ref.dtype)
        o_ref[:, wa:] = b_ref[...].astype(o_ref.dtype)

    a, b = lax.stop_gradient(a), lax.stop_gradient(b)
    return pl.pallas_call(
        body, name="concat_lo", grid=(s // rows,),
        out_shape=jax.ShapeDtypeStruct((s, wa + wb), jnp.bfloat16),
        in_specs=[pl.BlockSpec((rows, wa), lambda i: (i, 0)), pl.BlockSpec((rows, wb), lambda i: (i, 0))],
        out_specs=pl.BlockSpec((rows, wa + wb), lambda i: (i, 0)),
        compiler_params=pltpu.CompilerParams(dimension_semantics=("parallel",)),
    )(a, b)


@jax.custom_vjp
def _linear(x, x_lo, w, slot):
    del x, slot
    return _matmul(x_lo, w, name="linear_fwd")


def _linear_fwd(x, x_lo, w, slot):
    del x, slot
    return _matmul(x_lo, w, name="linear_fwd"), (x_lo, w)


def _linear_bwd(res, dy):
    x_lo, w = res
    dy = dy.astype(jnp.bfloat16)
    dx = _matmul(dy, w, tb=True, name="linear_dx")
    dw = _matmul(x_lo, dy, ta=True, name="linear_dw")
    return dx, jnp.zeros_like(x_lo), jnp.zeros_like(w), dw


_linear.defvjp(_linear_fwd, _linear_bwd)


@jax.custom_vjp
def _mlp(h, h_lo, w_up, w_down, slot_up, slot_down):
    return _mlp_fwd(h, h_lo, w_up, w_down, slot_up, slot_down)[0]


def _mlp_fwd(h, h_lo, w_up, w_down, slot_up, slot_down):
    del h, slot_up, slot_down
    up, act = _matmul(h_lo, w_up, name="mlp_up", epilogue="relu2")
    return _matmul(act, w_down, name="mlp_down"), (h_lo, up, act, w_up, w_down)


def _mlp_bwd(res, dy):
    h_lo, up, act, w_up, w_down = res
    dy = dy.astype(jnp.bfloat16)
    d_up = _matmul(dy, w_down, tb=True, name="mlp_d_up", epilogue="relu2_grad", extra=up, out_dtype=jnp.bfloat16)
    dw_down = _matmul(act, dy, ta=True, name="mlp_dw_down")
    dw_up = _matmul(h_lo, d_up, ta=True, name="mlp_dw_up")
    dh = _matmul(d_up, w_up, tb=True, name="mlp_dh")
    return dh, jnp.zeros_like(h_lo), jnp.zeros_like(w_up), jnp.zeros_like(w_down), dw_up, dw_down


_mlp.defvjp(_mlp_fwd, _mlp_bwd)


LN_ROWS = 256


def _ln_call(h, mix, g, b):
    s, d = h.shape
    tok = pl.BlockSpec((LN_ROWS, d), lambda i: (i, 0))
    vec = pl.BlockSpec((1, d), lambda i: (0, 0))
    stat = pl.BlockSpec((LN_ROWS, 1), lambda i: (i, 0))

    def body(h_ref, mix_ref, g_ref, b_ref, y_ref, ylo_ref, xhat_ref, rstd_ref):
        z = DN_ALPHA * h_ref[...] + mix_ref[...]
        mu = jnp.mean(z, axis=-1, keepdims=True)
        zc = z - mu
        rstd = lax.rsqrt(jnp.mean(jnp.square(zc), axis=-1, keepdims=True) + LN_EPS)
        xhat = zc * rstd
        y = xhat * g_ref[...] + b_ref[...]
        y_ref[...] = y
        ylo_ref[...] = y.astype(ylo_ref.dtype)
        xhat_ref[...] = xhat
        rstd_ref[...] = rstd

    sd = jax.ShapeDtypeStruct
    return pl.pallas_call(
        body, name="ln_fwd", grid=(s // LN_ROWS,),
        out_shape=(sd((s, d), jnp.float32), sd((s, d), jnp.bfloat16), sd((s, d), jnp.float32), sd((s, 1), jnp.float32)),
        in_specs=[tok, tok, vec, vec], out_specs=(tok, tok, tok, stat),
        compiler_params=pltpu.CompilerParams(dimension_semantics=("parallel",)),
    )(h, mix, g, b)


def _ln_grad_call(dy, xhat, rstd, g):
    s, d = dy.shape
    tok = pl.BlockSpec((LN_ROWS, d), lambda i: (i, 0))
    vec = pl.BlockSpec((1, d), lambda i: (0, 0))
    stat = pl.BlockSpec((LN_ROWS, 1), lambda i: (i, 0))

    def body(dy_ref, xhat_ref, rstd_ref, g_ref, dz_ref, dg_ref, db_ref):
        @pl.when(pl.program_id(0) == 0)
        def _():
            dg_ref[...] = jnp.zeros_like(dg_ref)
            db_ref[...] = jnp.zeros_like(db_ref)

        dy, xhat = dy_ref[...], xhat_ref[...]
        dyg = dy * g_ref[...]
        m1 = jnp.mean(dyg, axis=-1, keepdims=True)
        m2 = jnp.mean(dyg * xhat, axis=-1, keepdims=True)
        dz_ref[...] = rstd_ref[...] * (dyg - m1 - xhat * m2)
        dg_ref[...] += jnp.sum(dy * xhat, axis=0, keepdims=True)
        db_ref[...] += jnp.sum(dy, axis=0, keepdims=True)

    sd = jax.ShapeDtypeStruct
    return pl.pallas_call(
        body, name="ln_bwd", grid=(s // LN_ROWS,),
        out_shape=(sd((s, d), jnp.float32), sd((1, d), jnp.float32), sd((1, d), jnp.float32)),
        in_specs=[tok, tok, stat, vec], out_specs=(tok, vec, vec),
        compiler_params=pltpu.CompilerParams(dimension_semantics=("arbitrary",)),
    )(dy, xhat, rstd, g)


@jax.custom_vjp
def _ln_res(h, mix, g, b):
    return _ln_call(h, mix, g, b)[:2]


def _ln_res_fwd(h, mix, g, b):
    y, y_lo, xhat, rstd = _ln_call(h, mix, g, b)
    return (y, y_lo), (xhat, rstd, g)


def _ln_res_bwd(res, cts):
    xhat, rstd, g = res
    dz, dg, db = _ln_grad_call(cts[0], xhat, rstd, g)
    return DN_ALPHA * dz, dz, dg, db


_ln_res.defvjp(_ln_res_fwd, _ln_res_bwd)


MXU_DTYPE = jnp.bfloat16
DN_CB = 16
DN_GROUP = 8
DN_SCAN_CB = 4
DN_SCALE = DN_DK ** -0.5


def _dot(a, b, ca=1, cb=0):
    return lax.dot_general(a.astype(MXU_DTYPE), b.astype(MXU_DTYPE), (((ca,), (cb,)), ((), ())),
                           preferred_element_type=jnp.float32)


def _chunk_masks():
    row = lax.broadcasted_iota(jnp.int32, (CHUNK, CHUNK), 0)
    col = lax.broadcasted_iota(jnp.int32, (CHUNK, CHUNK), 1)
    return row >= col, row > col, row == col


def _to_col(row_vec):
    _, _, eye = _chunk_masks()
    return jnp.sum(jnp.where(eye, jnp.broadcast_to(row_vec, (CHUNK, CHUNK)), 0.0), axis=1, keepdims=True)


def _to_row(col_vec):
    _, _, eye = _chunk_masks()
    return jnp.sum(jnp.where(eye, jnp.broadcast_to(col_vec, (CHUNK, CHUNK)), 0.0), axis=0, keepdims=True)


def _last_row(col_vec):
    last = lax.broadcasted_iota(jnp.int32, (CHUNK, 1), 0) == CHUNK - 1
    return jnp.sum(jnp.where(last, col_vec, 0.0), axis=0, keepdims=True), last


def _chunk_terms(q, k, beta, gcc, gcr):
    incl, strict, _ = _chunk_masks()
    decay = jnp.where(incl, jnp.exp(jnp.minimum(gcc - gcr, 0.0)), 0.0)
    kb = k * beta
    lmat = jnp.where(strict, _dot(kb, k, 1, 1) * decay, 0.0)
    intra = jnp.where(incl, _dot(q, k, 1, 1) * decay, 0.0)
    return decay, kb, lmat, intra


def _dot3(a, b, ca=1, cb=0):
    if MXU_DTYPE == jnp.float32:
        return _dot(a, b, ca, cb)
    a_hi, b_hi = a.astype(MXU_DTYPE), b.astype(MXU_DTYPE)
    a_lo = (a - a_hi.astype(jnp.float32)).astype(MXU_DTYPE)
    b_lo = (b - b_hi.astype(jnp.float32)).astype(MXU_DTYPE)
    return _dot(a_hi, b_hi, ca, cb) + (_dot(a_hi, b_lo, ca, cb) + _dot(a_lo, b_hi, ca, cb))


def _unit_lower_inverse(lmats):
    _, _, eye = _chunk_masks()
    ident = jnp.where(eye, 1.0, 0.0)
    ts = [ident - m for m in lmats]
    ps = [_dot(m, m) for m in lmats]
    for _ in range(4):
        ts = [t + _dot(t, p) for t, p in zip(ts, ps)]
        ps = [_dot(p, p) for p in ps]
    ts = [t + _dot(t, p) for t, p in zip(ts, ps)]
    resids = [(t - ident) + _dot3(m, t) for m, t in zip(lmats, ts)]
    return [t - _dot(t, r) for t, r in zip(ts, resids)]


def _dn_specs(n_chunks):
    tok = pl.BlockSpec((DN_CB * CHUNK, DN_DK), lambda h, n: (n, h))
    rowv = pl.BlockSpec((None, DN_CB, CHUNK), lambda h, n: (h, n, 0))
    sq = pl.BlockSpec((None, DN_CB, CHUNK, CHUNK), lambda h, n: (h, n, 0, 0))
    lane = pl.BlockSpec((None, DN_CB, 1, DN_DV), lambda h, n: (h, n, 0, 0))
    planes = [pl.BlockSpec((None, DN_CB * CHUNK, DN_DK), functools.partial(lambda h, n, p: (p, n, h), p=p))
              for p in range(3)]
    return tok, rowv, sq, lane, planes


def _dn_prep(qkv, beta, gc):
    s = qkv.shape[1]
    n_chunks = s // CHUNK
    tok, rowv, sq, lane, planes = _dn_specs(n_chunks)
    tok_shape = qkv.shape[1:]

    def body(q_ref, k_ref, v_ref, beta_ref, gc_ref, u_ref, w_ref, qd_ref, kd_ref, intra_ref, t_ref, cd_ref):
        for c0 in range(0, DN_CB, DN_GROUP):
            chunks = range(c0, c0 + DN_GROUP)
            rhs, lmats = [], []
            for c in chunks:
                rows = pl.ds(c * CHUNK, CHUNK)
                q_c, k_c, v_c = q_ref[rows, :] * DN_SCALE, k_ref[rows, :], v_ref[rows, :]
                gcr_c = gc_ref[pl.ds(c, 1), :]
                beta_c, gcc_c = _to_col(beta_ref[pl.ds(c, 1), :]), _to_col(gcr_c)
                _, kb, lmat, intra = _chunk_terms(q_c, k_c, beta_c, gcc_c, gcr_c)
                eg = jnp.exp(gcc_c)
                g_last, _ = _last_row(gcc_c)
                qd_ref[rows, :] = (q_c * eg).astype(qd_ref.dtype)
                kd_ref[rows, :] = (k_c * jnp.exp(g_last - gcc_c)).astype(kd_ref.dtype)
                intra_ref[c] = intra.astype(intra_ref.dtype)
                cd_ref[c] = jnp.broadcast_to(jnp.exp(g_last), (1, DN_DV))
                rhs.append(jnp.concatenate([v_c * beta_c, kb * eg], axis=1))
                lmats.append(lmat)
            ts = _unit_lower_inverse(lmats)
            sols = [_dot3(t, r) for t, r in zip(ts, rhs)]
            for c, t, sol in zip(chunks, ts, sols):
                rows = pl.ds(c * CHUNK, CHUNK)
                t_ref[c] = t
                u_ref[rows, :] = sol[:, :DN_DV]
                w_ref[rows, :] = sol[:, DN_DV:].astype(w_ref.dtype)

    f32, mx = jnp.float32, MXU_DTYPE
    sd = jax.ShapeDtypeStruct
    return pl.pallas_call(
        body, name="dn_prep", grid=(DN_HEADS, n_chunks // DN_CB),
        out_shape=(sd(tok_shape, f32), sd(tok_shape, mx), sd(tok_shape, mx), sd(tok_shape, mx),
                   sd((DN_HEADS, n_chunks, CHUNK, CHUNK), mx), sd((DN_HEADS, n_chunks, CHUNK, CHUNK), f32),
                   sd((DN_HEADS, n_chunks, 1, DN_DV), f32)),
        in_specs=planes + [rowv, rowv], out_specs=(tok, tok, tok, tok, sq, sq, lane),
        compiler_params=pltpu.CompilerParams(dimension_semantics=("parallel", "parallel")),
    )(qkv, qkv, qkv, beta, gc)


def _dn_scan(u, w, qd, kd, intra, cd):
    s, width = u.shape
    n_chunks = s // CHUNK
    cb = DN_SCAN_CB
    tok = pl.BlockSpec((cb * CHUNK, width), lambda n: (n, 0))
    sq = pl.BlockSpec((DN_HEADS, cb, CHUNK, CHUNK), lambda n: (0, n, 0, 0))
    lane = pl.BlockSpec((DN_HEADS, cb, 1, DN_DV), lambda n: (0, n, 0, 0))
    st = pl.BlockSpec((DN_HEADS, cb, DN_DK, DN_DV), lambda n: (0, n, 0, 0))

    def body(u_ref, w_ref, qd_ref, kd_ref, intra_ref, cd_ref, o_ref, vn_ref, st_ref, state):
        @pl.when(pl.program_id(0) == 0)
        def _():
            state[...] = jnp.zeros_like(state)

        heads = range(DN_HEADS)
        cols = [pl.ds(h * DN_DK, DN_DK) for h in heads]
        s_f = [state[h] for h in heads]
        for c in range(cb):
            rows = pl.ds(c * CHUNK, CHUNK)
            s_mx = [s.astype(MXU_DTYPE) for s in s_f]
            for h in heads:
                st_ref[h, c] = s_mx[h]
            ws = [_dot(w_ref[rows, cols[h]], s_mx[h]) for h in heads]
            qs = [_dot(qd_ref[rows, cols[h]], s_mx[h]) for h in heads]
            v_new = [(u_ref[rows, cols[h]] - ws[h]).astype(MXU_DTYPE) for h in heads]
            inner = [_dot(intra_ref[h, c], v_new[h]) for h in heads]
            outer = [_dot(kd_ref[rows, cols[h]], v_new[h], 0, 0) for h in heads]
            for h in heads:
                vn_ref[rows, cols[h]] = v_new[h]
                o_ref[rows, cols[h]] = qs[h] + inner[h]
            s_f = [s_f[h] * cd_ref[h, c] + outer[h] for h in heads]
        for h in heads:
            state[h] = s_f[h]

    sd = jax.ShapeDtypeStruct
    return pl.pallas_call(
        body, name="dn_scan", grid=(n_chunks // cb,),
        out_shape=(sd(u.shape, jnp.float32), sd(u.shape, MXU_DTYPE),
                   sd((DN_HEADS, n_chunks, DN_DK, DN_DV), MXU_DTYPE)),
        in_specs=[tok, tok, tok, tok, sq, lane], out_specs=(tok, tok, st),
        scratch_shapes=[pltpu.VMEM((DN_HEADS, DN_DK, DN_DV), jnp.float32)],
        compiler_params=pltpu.CompilerParams(dimension_semantics=("arbitrary",)),
    )(u, w, qd, kd, intra, cd)


def _dn_bwd_scan(do, w, qd, kd, intra, cd, vn, st):
    s, width = do.shape
    n_chunks = s // CHUNK
    cb = DN_SCAN_CB
    last = n_chunks // cb - 1
    tok = pl.BlockSpec((cb * CHUNK, width), lambda n: (last - n, 0))
    sq = pl.BlockSpec((DN_HEADS, cb, CHUNK, CHUNK), lambda n: (0, last - n, 0, 0))
    lane = pl.BlockSpec((DN_HEADS, cb, 1, DN_DV), lambda n: (0, last - n, 0, 0))
    stt = pl.BlockSpec((DN_HEADS, cb, DN_DK, DN_DV), lambda n: (0, last - n, 0, 0))

    def body(do_ref, w_ref, qd_ref, kd_ref, intra_ref, cd_ref, vn_ref, st_ref,
             du_ref, dw_ref, dqd_ref, dkd_ref, dintra_ref, dgl_ref, dstate):
        @pl.when(pl.program_id(0) == 0)
        def _():
            dstate[...] = jnp.zeros_like(dstate)

        heads = range(DN_HEADS)
        cols = [pl.ds(h * DN_DK, DN_DK) for h in heads]
        ds_f = [dstate[h] for h in heads]
        for c in reversed(range(cb)):
            rows = pl.ds(c * CHUNK, CHUNK)
            ds_mx = [d.astype(MXU_DTYPE) for d in ds_f]
            do_h = [do_ref[rows, cols[h]].astype(MXU_DTYPE) for h in heads]
            dv_a = [_dot(intra_ref[h, c], do_h[h], 0, 0) for h in heads]
            dv_b = [_dot(kd_ref[rows, cols[h]], ds_mx[h]) for h in heads]
            d_intra = [_dot(do_h[h], vn_ref[rows, cols[h]], 1, 1) for h in heads]
            d_qd = [_dot(do_h[h], st_ref[h, c], 1, 1) for h in heads]
            d_kd = [_dot(vn_ref[rows, cols[h]], ds_mx[h], 1, 1) for h in heads]
            ds_q = [_dot(qd_ref[rows, cols[h]], do_h[h], 0, 0) for h in heads]
            dv_new = [dv_a[h] + dv_b[h] for h in heads]
            dv_mx = [d.astype(MXU_DTYPE) for d in dv_new]
            d_w = [_dot(dv_mx[h], st_ref[h, c], 1, 1) for h in heads]
            ds_w = [_dot(w_ref[rows, cols[h]], dv_mx[h], 0, 0) for h in heads]
            ds_next = []
            for h in heads:
                du_ref[rows, cols[h]] = dv_new[h]
                dintra_ref[h, c] = d_intra[h]
                dqd_ref[rows, cols[h]] = d_qd[h]
                dkd_ref[rows, cols[h]] = d_kd[h]
                dw_ref[rows, cols[h]] = -d_w[h]
                cd_h = cd_ref[h, c]
                dcd = jnp.sum(jnp.sum(st_ref[h, c].astype(jnp.float32) * ds_f[h], axis=1, keepdims=True), axis=0,
                              keepdims=True)
                dgl_ref[h, c] = dcd * cd_h
                ds_next.append(ds_q[h] + ds_f[h] * cd_h - ds_w[h])
            ds_f = ds_next
        for h in heads:
            dstate[h] = ds_f[h]

    sd = jax.ShapeDtypeStruct
    f32 = jnp.float32
    return pl.pallas_call(
        body, name="dn_bwd_scan", grid=(n_chunks // cb,),
        out_shape=(sd(do.shape, f32), sd(do.shape, f32), sd(do.shape, f32), sd(do.shape, f32),
                   sd((DN_HEADS, n_chunks, CHUNK, CHUNK), f32), sd((DN_HEADS, n_chunks, 1, DN_DV), f32)),
        in_specs=[tok, tok, tok, tok, sq, lane, tok, stt], out_specs=(tok, tok, tok, tok, sq, lane),
        scratch_shapes=[pltpu.VMEM((DN_HEADS, DN_DK, DN_DV), f32)],
        compiler_params=pltpu.CompilerParams(dimension_semantics=("arbitrary",)),
    )(do, w, qd, kd, intra, cd, vn, st)


def _dn_bwd_chunks(qkv, beta, gc, t, u, w, du, dw, dqd, dkd, dintra, dgl):
    s = qkv.shape[1]
    n_chunks = s // CHUNK
    tok, rowv, sq, lane, planes = _dn_specs(n_chunks)
    all_planes = pl.BlockSpec((3, DN_CB * CHUNK, DN_DK), lambda h, n: (0, n, h))

    def body(q_ref, k_ref, v_ref, beta_ref, gc_ref, t_ref, u_ref, w_ref, du_ref, dw_ref, dqd_ref, dkd_ref,
             dintra_ref, dgl_ref, dqkv_ref, dbeta_ref, dgc_ref):
        incl, strict, _ = _chunk_masks()

        def first(c):
            rows = pl.ds(c * CHUNK, CHUNK)
            q_c, k_c = q_ref[rows, :] * DN_SCALE, k_ref[rows, :]
            gcr_c = gc_ref[pl.ds(c, 1), :]
            beta_c, gcc_c = _to_col(beta_ref[pl.ds(c, 1), :]), _to_col(gcr_c)
            decay, kb, lmat, intra = _chunk_terms(q_c, k_c, beta_c, gcc_c, gcr_c)
            d_sol = jnp.concatenate([du_ref[rows, :], dw_ref[rows, :]], axis=1)
            d_rhs = _dot3(t_ref[c], d_sol, 0, 0)
            return dict(rows=rows, q=q_c, k=k_c, beta=beta_c, gcc=gcc_c, decay=decay, kb=kb, lmat=lmat, intra=intra,
                        d_rhs=d_rhs)

        def second(c, e):
            sol = jnp.concatenate([u_ref[e["rows"], :], w_ref[e["rows"], :].astype(jnp.float32)], axis=1)
            e["d_l"] = jnp.where(strict, -_dot(e["d_rhs"], sol, 1, 1), 0.0)
            e["d_intra"] = jnp.where(incl, dintra_ref[c], 0.0)
            d_qk = e["d_intra"] * e["decay"]
            e["dq"] = _dot(d_qk, e["k"])
            e["dk"] = _dot(d_qk, e["q"], 0, 0)

        def third(e):
            d_a = e["d_l"] * e["decay"]
            e["dkb"] = _dot(d_a, e["k"])
            e["dk"] = e["dk"] + _dot(d_a, e["kb"], 0, 0)

        def last(c, e):
            rows, q_c, k_c, beta_c, gcc_c = e["rows"], e["q"], e["k"], e["beta"], e["gcc"]
            v_c = v_ref[rows, :]
            eg = jnp.exp(gcc_c)
            g_last, is_last = _last_row(gcc_c)
            e_rev = jnp.exp(g_last - gcc_c)
            d_rhs_u, d_rhs_w = e["d_rhs"][:, :DN_DV], e["d_rhs"][:, DN_DV:]
            dqkv_ref[2, rows, :] = d_rhs_u * beta_c
            dbeta = jnp.sum(d_rhs_u * v_c, axis=1, keepdims=True)
            dkb = e["dkb"] + d_rhs_w * eg
            dgc = jnp.sum(d_rhs_w * e["kb"] * eg, axis=1, keepdims=True)
            m1 = e["d_l"] * e["lmat"]
            dgc = dgc + jnp.sum(m1, axis=1, keepdims=True)
            dgr = -jnp.sum(m1, axis=0, keepdims=True)
            m2 = e["d_intra"] * e["intra"]
            dgc = dgc + jnp.sum(m2, axis=1, keepdims=True)
            dgr = dgr - jnp.sum(m2, axis=0, keepdims=True)
            dqd = dqd_ref[rows, :]
            dq = e["dq"] + dqd * eg
            dgc = dgc + jnp.sum(dqd * q_c * eg, axis=1, keepdims=True)
            dkd = dkd_ref[rows, :]
            dk = e["dk"] + dkd * e_rev
            tk = jnp.sum(dkd * k_c * e_rev, axis=1, keepdims=True)
            dgc = dgc - tk
            d_last = dgl_ref[c][:, :1] + jnp.sum(tk, axis=0, keepdims=True)
            dgc = dgc + jnp.where(is_last, d_last, 0.0)
            dk = dk + dkb * beta_c
            dbeta = dbeta + jnp.sum(dkb * k_c, axis=1, keepdims=True)
            dqkv_ref[0, rows, :] = dq * DN_SCALE
            dqkv_ref[1, rows, :] = dk
            dbeta_ref[pl.ds(c, 1), :] = _to_row(dbeta)
            dgc_ref[pl.ds(c, 1), :] = _to_row(dgc) + dgr

        for c0 in range(0, DN_CB, DN_GROUP):
            chunks = range(c0, c0 + DN_GROUP)
            env = [first(c) for c in chunks]
            for c, e in zip(chunks, env):
                second(c, e)
            for e in env:
                third(e)
            for c, e in zip(chunks, env):
                last(c, e)

    sd = jax.ShapeDtypeStruct
    f32 = jnp.float32
    return pl.pallas_call(
        body, name="dn_bwd_chunks", grid=(DN_HEADS, n_chunks // DN_CB),
        out_shape=(sd(qkv.shape, f32), sd(beta.shape, f32), sd(gc.shape, f32)),
        in_specs=planes + [rowv, rowv, sq, tok, tok, tok, tok, tok, tok, sq, lane],
        out_specs=(all_planes, rowv, rowv),
        compiler_params=pltpu.CompilerParams(dimension_semantics=("parallel", "parallel")),
    )(qkv, qkv, qkv, beta, gc, t, u, w, du, dw, dqd, dkd, dintra, dgl)


@jax.custom_vjp
def _delta_rule_op(qkv, beta, gc):
    return _delta_rule_fwd(qkv, beta, gc)[0]


def _delta_rule_fwd(qkv, beta, gc):
    u, w, qd, kd, intra, t, cd = _dn_prep(qkv, beta, gc)
    out, vn, st = _dn_scan(u, w, qd, kd, intra, cd)
    return out, (qkv, beta, gc, u, w, qd, kd, intra, t, cd, vn, st)


def _delta_rule_bwd(res, do):
    qkv, beta, gc, u, w, qd, kd, intra, t, cd, vn, st = res
    du, dw, dqd, dkd, dintra, dgl = _dn_bwd_scan(do, w, qd, kd, intra, cd, vn, st)
    return _dn_bwd_chunks(qkv, beta, gc, t, u, w, du, dw, dqd, dkd, dintra, dgl)


_delta_rule_op.defvjp(_delta_rule_fwd, _delta_rule_bwd)


def _gated_delta_rule(qkv, g, beta):
    s, h = g.shape
    n_chunks = s // CHUNK
    gc = jnp.cumsum(g.T.reshape(h, n_chunks, CHUNK), axis=-1)
    return _delta_rule_op(qkv, beta.T.reshape(h, n_chunks, CHUNK), gc)


PRE_ROWS = 512
HALO = 8
PRE_W = DN_QK_W


def _shift_rows(xs, k):
    return pltpu.roll(xs, k, 0)[HALO:]


def _conv_silu(x_ref, halo_ref, w_ref, first_block):
    halo = jnp.where(first_block, 0.0, halo_ref[...])
    xs = jnp.concatenate([halo, x_ref[...]], axis=0)
    taps = [_shift_rows(xs, CONV_WIDTH - 1 - j) for j in range(CONV_WIDTH - 1)] + [x_ref[...]]
    conv = sum(w_ref[pl.ds(j, 1), :] * taps[j] for j in range(CONV_WIDTH))
    return conv, jax.nn.sigmoid(conv), taps


def _pre_specs():
    blk = pl.BlockSpec((PRE_ROWS, PRE_W), lambda j, i: (i, j))
    prev = pl.BlockSpec((HALO, PRE_W), lambda j, i: (jnp.maximum(i * (PRE_ROWS // HALO) - 1, 0), j))
    wts = pl.BlockSpec((CONV_WIDTH, PRE_W), lambda j, i: (0, j))
    plane = pl.BlockSpec((None, PRE_ROWS, PRE_W), lambda j, i: (j, i, 0))
    return blk, prev, wts, plane


def _pre_fwd_call(x, conv_w):
    s = x.shape[0]
    blk, prev, wts, plane = _pre_specs()

    def body(x_ref, halo_ref, w_ref, o_ref):
        conv, sig, _ = _conv_silu(x_ref, halo_ref, w_ref, pl.program_id(1) == 0)
        act = conv * sig
        is_v = pl.program_id(0) == 2
        for h in range(DN_HEADS):
            cols = slice(h * DN_DK, (h + 1) * DN_DK)
            a_h = act[:, cols]
            r = lax.rsqrt(jnp.sum(a_h * a_h, axis=-1, keepdims=True) + NORM_EPS)
            o_ref[:, cols] = a_h * jnp.where(is_v, 1.0, r)

    return pl.pallas_call(
        body, name="pre_fwd", grid=(3, s // PRE_ROWS),
        out_shape=jax.ShapeDtypeStruct((3, s, PRE_W), jnp.float32),
        in_specs=[blk, prev, wts], out_specs=plane,
        compiler_params=pltpu.CompilerParams(dimension_semantics=("parallel", "parallel")),
    )(x, x, conv_w)


def _pre_bwd_act_call(x, conv_w, d_out):
    s = x.shape[0]
    blk, prev, wts, plane = _pre_specs()

    def body(x_ref, halo_ref, w_ref, do_ref, dc_ref):
        conv, sig, _ = _conv_silu(x_ref, halo_ref, w_ref, pl.program_id(1) == 0)
        act = conv * sig
        d_silu = sig * (1.0 + conv * (1.0 - sig))
        is_v = pl.program_id(0) == 2
        for h in range(DN_HEADS):
            cols = slice(h * DN_DK, (h + 1) * DN_DK)
            a_h, do_h = act[:, cols], do_ref[:, cols]
            r = lax.rsqrt(jnp.sum(a_h * a_h, axis=-1, keepdims=True) + NORM_EPS)
            n_h = a_h * r
            d_norm = r * (do_h - n_h * jnp.sum(do_h * n_h, axis=-1, keepdims=True))
            dc_ref[:, cols] = jnp.where(is_v, do_h, d_norm) * d_silu[:, cols]

    return pl.pallas_call(
        body, name="pre_bwd_act", grid=(3, s // PRE_ROWS),
        out_shape=jax.ShapeDtypeStruct(x.shape, jnp.float32),
        in_specs=[blk, prev, wts, plane], out_specs=blk,
        compiler_params=pltpu.CompilerParams(dimension_semantics=("parallel", "parallel")),
    )(x, x, conv_w, d_out)


def _pre_bwd_conv_call(x, conv_w, dc):
    s = x.shape[0]
    n_blocks = s // PRE_ROWS
    blk, prev, wts, _ = _pre_specs()
    nxt = pl.BlockSpec((HALO, PRE_W), lambda j, i: (jnp.minimum((i + 1) * (PRE_ROWS // HALO), s // HALO - 1), j))

    def body(x_ref, halo_ref, w_ref, dc_ref, dcn_ref, dx_ref, dw_ref):
        i = pl.program_id(1)

        @pl.when(i == 0)
        def _():
            dw_ref[...] = jnp.zeros_like(dw_ref)

        dcv = dc_ref[...]
        ahead = jnp.concatenate([dcv, jnp.where(i == n_blocks - 1, 0.0, dcn_ref[...])], axis=0)
        dx = w_ref[pl.ds(CONV_WIDTH - 1, 1), :] * dcv
        for j in range(CONV_WIDTH - 1):
            k = CONV_WIDTH - 1 - j
            dx = dx + w_ref[pl.ds(j, 1), :] * pltpu.roll(ahead, PRE_ROWS + HALO - k, 0)[:PRE_ROWS]
        dx_ref[...] = dx
        halo = jnp.where(i == 0, 0.0, halo_ref[...])
        xs = jnp.concatenate([halo, x_ref[...]], axis=0)
        for j in range(CONV_WIDTH):
            tap = x_ref[...] if j == CONV_WIDTH - 1 else _shift_rows(xs, CONV_WIDTH - 1 - j)
            dw_ref[pl.ds(j, 1), :] += jnp.sum(dcv * tap, axis=0, keepdims=True)

    sd = jax.ShapeDtypeStruct
    return pl.pallas_call(
        body, name="pre_bwd_conv", grid=(3, n_blocks),
        out_shape=(sd(x.shape, jnp.float32), sd(conv_w.shape, jnp.float32)),
        in_specs=[blk, prev, wts, blk, nxt], out_specs=(blk, wts),
        compiler_params=pltpu.CompilerParams(dimension_semantics=("parallel", "arbitrary")),
    )(x, x, conv_w, dc, dc)


@jax.custom_vjp
def _pre_op(x, conv_w):
    return _pre_fwd_call(x, conv_w)


def _pre_op_fwd(x, conv_w):
    return _pre_fwd_call(x, conv_w), (x, conv_w)


def _pre_op_bwd(res, d_out):
    x, conv_w = res
    return _pre_bwd_conv_call(x, conv_w, _pre_bwd_act_call(x, conv_w, d_out))


_pre_op.defvjp(_pre_op_fwd, _pre_op_bwd)


def _project(h, h_lo, w, slot):
    b, s, d = h.shape
    return _linear(h.reshape(b * s, d), h_lo.reshape(b * s, d), w, slot).reshape(b, s, w.shape[1])


def _rope_table(positions, dh):
    inv_freq = ROPE_THETA ** (-jnp.arange(0, dh, 2, dtype=jnp.float32) / dh)
    ang = positions.astype(jnp.float32)[:, None] * inv_freq
    reps = 128 // (dh // 2)
    return jnp.concatenate([jnp.tile(jnp.cos(ang), (1, reps)), jnp.tile(jnp.sin(ang), (1, reps))], axis=-1)


GATE_ROWS = 512


def _gate_terms(o_h, z_h):
    r = lax.rsqrt(jnp.mean(o_h * o_h, axis=-1, keepdims=True) + NORM_EPS)
    sig = jax.nn.sigmoid(z_h)
    return r, o_h * r, sig, z_h * sig


def _gate_fwd_call(o, z, nw):
    tok = pl.BlockSpec((GATE_ROWS, DN_V_W), lambda i: (i, 0))
    vec = pl.BlockSpec((1, DN_DV), lambda i: (0, 0))

    def body(o_ref, z_ref, nw_ref, y_ref):
        for h in range(DN_HEADS):
            cols = pl.ds(h * DN_DV, DN_DV)
            _, n_h, _, g_h = _gate_terms(o_ref[:, cols], z_ref[:, cols])
            y_ref[:, cols] = n_h * nw_ref[...] * g_h

    return pl.pallas_call(
        body, name="gate_fwd", grid=(o.shape[0] // GATE_ROWS,),
        out_shape=jax.ShapeDtypeStruct(o.shape, jnp.float32), in_specs=[tok, tok, vec], out_specs=tok,
        compiler_params=pltpu.CompilerParams(dimension_semantics=("parallel",)),
    )(o, z, nw)


def _gate_bwd_call(o, z, nw, dy):
    tok = pl.BlockSpec((GATE_ROWS, DN_V_W), lambda i: (i, 0))
    vec = pl.BlockSpec((1, DN_DV), lambda i: (0, 0))

    def body(o_ref, z_ref, nw_ref, dy_ref, do_ref, dz_ref, dnw_ref):
        @pl.when(pl.program_id(0) == 0)
        def _():
            dnw_ref[...] = jnp.zeros_like(dnw_ref)

        for h in range(DN_HEADS):
            cols = pl.ds(h * DN_DV, DN_DV)
            z_h, dy_h = z_ref[:, cols], dy_ref[:, cols]
            r, n_h, sig, g_h = _gate_terms(o_ref[:, cols], z_h)
            dz_ref[:, cols] = dy_h * n_h * nw_ref[...] * (sig * (1.0 + z_h * (1.0 - sig)))
            dn = dy_h * nw_ref[...] * g_h
            do_ref[:, cols] = r * (dn - n_h * jnp.mean(dn * n_h, axis=-1, keepdims=True))
            dnw_ref[...] += jnp.sum(dy_h * n_h * g_h, axis=0, keepdims=True)

    sd = jax.ShapeDtypeStruct
    return pl.pallas_call(
        body, name="gate_bwd", grid=(o.shape[0] // GATE_ROWS,),
        out_shape=(sd(o.shape, jnp.float32), sd(o.shape, jnp.float32), sd(nw.shape, jnp.float32)),
        in_specs=[tok, tok, vec, tok], out_specs=(tok, tok, vec),
        compiler_params=pltpu.CompilerParams(dimension_semantics=("arbitrary",)),
    )(o, z, nw, dy)


@jax.custom_vjp
def _gate_op(o, z, nw):
    return _gate_fwd_call(o, z, nw)


def _gate_op_fwd(o, z, nw):
    return _gate_fwd_call(o, z, nw), (o, z, nw)


def _gate_op_bwd(res, dy):
    return _gate_bwd_call(*res, dy)


_gate_op.defvjp(_gate_op_fwd, _gate_op_bwd)


_MASKED = -1e30


def _swa_probs(qs, k_h, sinks, valid):
    ss = [jnp.where(valid, _dot(q_h, k_h, 1, 1) * (SWA_DH ** -0.5), _MASKED) for q_h in qs]
    ms = [jnp.maximum(jnp.max(s, axis=-1, keepdims=True), sink) for s, sink in zip(ss, sinks)]
    ps = [jnp.exp(s - m) for s, m in zip(ss, ms)]
    es = [jnp.exp(sink - m) for sink, m in zip(sinks, ms)]
    invs = [1.0 / (jnp.sum(p, axis=-1, keepdims=True) + e) for p, e in zip(ps, es)]
    return [p * inv for p, inv in zip(ps, invs)], [e * inv for e, inv in zip(es, invs)]


def _swa_valid(n):
    qi = lax.broadcasted_iota(jnp.int32, (WINDOW, 2 * WINDOW), 0)
    kj = lax.broadcasted_iota(jnp.int32, (WINDOW, 2 * WINDOW), 1)
    diff = qi + WINDOW - kj
    return (diff >= 0) & (diff < WINDOW) & ((kj >= WINDOW) | (n > 0))


def _rotate_half(x, transpose=False):
    half = SWA_DH // 2
    lower = lax.broadcasted_iota(jnp.int32, x.shape, 1) % SWA_DH < half
    ahead, behind = pltpu.roll(x, 128 - half, 1), pltpu.roll(x, half, 1)
    return jnp.where(lower, ahead, -behind) if transpose else jnp.where(lower, -ahead, behind)


def _rope(x, table):
    return x * table[:, :128] + _rotate_half(x) * table[:, 128:]


def _unrope(dy, table):
    return dy * table[:, :128] + _rotate_half(dy * table[:, 128:], transpose=True)


def _swa_specs():
    qs = pl.BlockSpec((WINDOW, SWA_Q_W), lambda n: (n, 0))
    first = lambda n: jnp.maximum(n - 1, 0)
    kv = [pl.BlockSpec((WINDOW, SWA_KV_W), lambda n: (first(n), 0)), pl.BlockSpec((WINDOW, SWA_KV_W), lambda n: (n, 0)),
          pl.BlockSpec((WINDOW, SWA_KV_W), lambda n: (first(n), 1)), pl.BlockSpec((WINDOW, SWA_KV_W), lambda n: (n, 1))]
    tables = [pl.BlockSpec((WINDOW, 256), lambda n: (first(n), 0)), pl.BlockSpec((WINDOW, 256), lambda n: (n, 0))]
    cur = pl.BlockSpec((WINDOW, SWA_KV_W), lambda n: (n, 0))
    sk = pl.BlockSpec((SWA_HEADS, 1, 128), lambda n: (0, 0, 0))
    return qs, kv, tables, cur, sk


def _swa_load(q_ref, kp_ref, kc_ref, vp_ref, vc_ref, tp_ref, tc_ref):
    table_kk = jnp.concatenate([tp_ref[...], tc_ref[...]], axis=0)
    kk = _rope(jnp.concatenate([kp_ref[...], kc_ref[...]], axis=0), table_kk)
    vv = jnp.concatenate([vp_ref[...], vc_ref[...]], axis=0)
    q_rot = []
    for b in range(SWA_Q_W // 128):
        pair = _rope(q_ref[:, pl.ds(b * 128, 128)], tc_ref[...])
        q_rot += [pair[:, :SWA_DH], pair[:, SWA_DH:]]
    split = lambda t: [t[:, hkv * SWA_DH:(hkv + 1) * SWA_DH] for hkv in range(SWA_KV_HEADS)]
    return q_rot, split(kk), split(vv), table_kk


def _swa_fwd_call(q, kv, table, sinks):
    qs, kvs, tables, _, sk = _swa_specs()

    def body(q_ref, kp_ref, kc_ref, vp_ref, vc_ref, tp_ref, tc_ref, sink_ref, o_ref):
        valid = _swa_valid(pl.program_id(0))
        q_rot, kk, vv, _ = _swa_load(q_ref, kp_ref, kc_ref, vp_ref, vc_ref, tp_ref, tc_ref)
        for hkv in range(SWA_KV_HEADS):
            heads = range(hkv * SWA_GROUP, (hkv + 1) * SWA_GROUP)
            probs, _ = _swa_probs([q_rot[h] for h in heads], kk[hkv], [sink_ref[h][:, :1] for h in heads], valid)
            outs = [_dot(p, vv[hkv]) for p in probs]
            for h, o in zip(heads, outs):
                o_ref[:, pl.ds(h * SWA_DH, SWA_DH)] = o

    return pl.pallas_call(
        body, name="swa_fwd", grid=(q.shape[0] // WINDOW,),
        out_shape=jax.ShapeDtypeStruct(q.shape, jnp.float32),
        in_specs=[qs] + kvs + tables + [sk], out_specs=qs,
        compiler_params=pltpu.CompilerParams(dimension_semantics=("parallel",)),
    )(q, kv, kv, kv, kv, table, table, sinks)


def _swa_bwd_call(q, kv, table, sinks, do):
    qs, kvs, tables, cur, sk = _swa_specs()

    def body(q_ref, kp_ref, kc_ref, vp_ref, vc_ref, tp_ref, tc_ref, sink_ref, do_ref,
             dq_ref, dkc_ref, dkp_ref, dvc_ref, dvp_ref, ds_ref):
        @pl.when(pl.program_id(0) == 0)
        def _():
            ds_ref[...] = jnp.zeros_like(ds_ref)

        valid = _swa_valid(pl.program_id(0))
        q_rot, kk, vv, table_kk = _swa_load(q_ref, kp_ref, kc_ref, vp_ref, vc_ref, tp_ref, tc_ref)
        lane0 = lax.broadcasted_iota(jnp.int32, (1, 128), 1) == 0
        dq_heads, dk_heads, dv_heads = [], [], []
        for hkv in range(SWA_KV_HEADS):
            k_h, v_h = kk[hkv], vv[hkv]
            heads = range(hkv * SWA_GROUP, (hkv + 1) * SWA_GROUP)
            q_hs = [q_rot[h] for h in heads]
            dos = [do_ref[:, pl.ds(h * SWA_DH, SWA_DH)] for h in heads]
            probs, p_sinks = _swa_probs(q_hs, k_h, [sink_ref[h][:, :1] for h in heads], valid)
            dps = [_dot(do_h, v_h, 1, 1) for do_h in dos]
            rss = [jnp.sum(p * dp, axis=-1, keepdims=True) for p, dp in zip(probs, dps)]
            d_ss = [p * (dp - rs) for p, dp, rs in zip(probs, dps, rss)]
            dq_heads += [_dot(d_s, k_h) * (SWA_DH ** -0.5) for d_s in d_ss]
            dks = [_dot(d_s, q_h, 0, 0) for d_s, q_h in zip(d_ss, q_hs)]
            dvs = [_dot(p, do_h, 0, 0) for p, do_h in zip(probs, dos)]
            for h, p_sink, rs in zip(heads, p_sinks, rss):
                d_sink = -jnp.sum(p_sink * rs, axis=0, keepdims=True)
                ds_ref[h] += jnp.where(lane0, d_sink, 0.0)
            dk_heads.append(sum(dks[1:], dks[0]) * (SWA_DH ** -0.5))
            dv_heads.append(sum(dvs[1:], dvs[0]))
        for b in range(SWA_Q_W // 128):
            pair = jnp.concatenate([dq_heads[2 * b], dq_heads[2 * b + 1]], axis=1)
            dq_ref[:, pl.ds(b * 128, 128)] = _unrope(pair, tc_ref[...])
        dk = _unrope(jnp.concatenate(dk_heads, axis=1), table_kk)
        dv = jnp.concatenate(dv_heads, axis=1)
        dkp_ref[...] = dk[:WINDOW]
        dkc_ref[...] = dk[WINDOW:]
        dvp_ref[...] = dv[:WINDOW]
        dvc_ref[...] = dv[WINDOW:]

    sd = jax.ShapeDtypeStruct
    f32 = jnp.float32
    half = (q.shape[0], SWA_KV_W)
    return pl.pallas_call(
        body, name="swa_bwd", grid=(q.shape[0] // WINDOW,),
        out_shape=(sd(q.shape, f32), sd(half, f32), sd(half, f32), sd(half, f32), sd(half, f32), sd(sinks.shape, f32)),
        in_specs=[qs] + kvs + tables + [sk, qs], out_specs=(qs, cur, cur, cur, cur, sk),
        compiler_params=pltpu.CompilerParams(dimension_semantics=("arbitrary",)),
    )(q, kv, kv, kv, kv, table, table, sinks, do)


@jax.custom_vjp
def _swa_op(q, kv, table, sinks):
    return _swa_fwd_call(q, kv, table, sinks)


def _swa_op_fwd(q, kv, table, sinks):
    return _swa_fwd_call(q, kv, table, sinks), (q, kv, table, sinks)


def _swa_op_bwd(res, do):
    q, kv, table, sinks = res
    dq, dkc, dkp, dvc, dvp, dsinks = _swa_bwd_call(q, kv, table, sinks, do)

    def fold(cur, prev):
        return cur + jnp.concatenate([prev[WINDOW:], jnp.zeros_like(prev[:WINDOW])], axis=0)

    return dq, jnp.concatenate([fold(dkc, dkp), fold(dvc, dvp)], axis=1), jnp.zeros_like(table), dsinks


_swa_op.defvjp(_swa_op_fwd, _swa_op_bwd)


def _swa_sink_attention(q, kv, table, sinks):
    return _swa_op(q, kv, table, jnp.broadcast_to(sinks[:, None, None], (SWA_HEADS, 1, 128)))


MEM_ROWS = 512


def _mem_probs(q_h, k_h):
    s = _dot(q_h, k_h, 1, 1) * (MEM_DH ** -0.5)
    p = jnp.exp(s - jnp.max(s, axis=-1, keepdims=True))
    return p / jnp.sum(p, axis=-1, keepdims=True)


def _mem_fwd_call(qm, kv):
    qs = pl.BlockSpec((MEM_ROWS, MEM_W), lambda i: (i, 0))
    kvs = pl.BlockSpec(kv.shape, lambda i: (0, 0))

    def body(q_ref, kv_ref, o_ref):
        for h in range(MEM_HEADS):
            cols = pl.ds(h * MEM_DH, MEM_DH)
            probs = _mem_probs(q_ref[:, cols], kv_ref[:, cols])
            o_ref[:, cols] = _dot(probs, kv_ref[:, pl.ds(MEM_W + h * MEM_DH, MEM_DH)])

    return pl.pallas_call(
        body, name="mem_fwd", grid=(qm.shape[0] // MEM_ROWS,),
        out_shape=jax.ShapeDtypeStruct(qm.shape, jnp.float32), in_specs=[qs, kvs], out_specs=qs,
        compiler_params=pltpu.CompilerParams(dimension_semantics=("parallel",)),
    )(qm, kv)


def _mem_bwd_call(qm, kv, do):
    qs = pl.BlockSpec((MEM_ROWS, MEM_W), lambda i: (i, 0))
    kvs = pl.BlockSpec(kv.shape, lambda i: (0, 0))

    def body(q_ref, kv_ref, do_ref, dq_ref, dkv_ref):
        @pl.when(pl.program_id(0) == 0)
        def _():
            dkv_ref[...] = jnp.zeros_like(dkv_ref)

        for h in range(MEM_HEADS):
            cols = pl.ds(h * MEM_DH, MEM_DH)
            v_cols = pl.ds(MEM_W + h * MEM_DH, MEM_DH)
            q_h, k_h, do_h = q_ref[:, cols], kv_ref[:, cols], do_ref[:, cols]
            probs = _mem_probs(q_h, k_h)
            dp = _dot(do_h, kv_ref[:, v_cols], 1, 1)
            d_s = probs * (dp - jnp.sum(probs * dp, axis=-1, keepdims=True))
            dq_ref[:, cols] = _dot(d_s, k_h) * (MEM_DH ** -0.5)
            dkv_ref[:, cols] += _dot(d_s, q_h, 0, 0) * (MEM_DH ** -0.5)
            dkv_ref[:, v_cols] += _dot(probs, do_h, 0, 0)

    sd = jax.ShapeDtypeStruct
    return pl.pallas_call(
        body, name="mem_bwd", grid=(qm.shape[0] // MEM_ROWS,),
        out_shape=(sd(qm.shape, jnp.float32), sd(kv.shape, jnp.float32)),
        in_specs=[qs, kvs, qs], out_specs=(qs, kvs),
        compiler_params=pltpu.CompilerParams(dimension_semantics=("arbitrary",)),
    )(qm, kv, do)


@jax.custom_vjp
def _mem_op(qm, kv):
    return _mem_fwd_call(qm, kv)


def _mem_op_fwd(qm, kv):
    return _mem_fwd_call(qm, kv), (qm, kv)


def _mem_op_bwd(res, do):
    return _mem_bwd_call(*res, do)


_mem_op.defvjp(_mem_op_fwd, _mem_op_bwd)


def _memory_attention(qm, kv):
    return _mem_op(qm[0], kv[0])[None]


def _mixer_a(h, h_lo, mem, mem_lo, p, s, layer):
    B, S, _ = h.shape
    proj = _project(h, h_lo, p["a_w_in"][layer], s["a_w_in"][layer])
    c1 = 2 * DN_QK_W + DN_V_W
    qkv = proj[..., :c1]
    z = proj[..., c1:QKVZ_W]
    qm = proj[..., QKVZ_W:QKVZ_W + MEM_W]
    a = proj[..., QKVZ_W + MEM_W:QKVZ_W + MEM_W + DN_HEADS]
    b = proj[..., QKVZ_W + MEM_W + DN_HEADS:QKVZ_W + MEM_W + 2 * DN_HEADS]
    planes = _pre_op(qkv[0], p["a_conv_w"][layer])
    beta = jax.nn.sigmoid(b[0])
    g = -jnp.exp(p["a_A_log"][layer]) * jax.nn.softplus(a[0] + p["a_dt_bias"][layer])
    o = _gate_op(_gated_delta_rule(planes, g, beta), z[0], p["a_norm_w"][layer][None])[None]
    kv = _project(mem, mem_lo, p["mem_w_kv"][layer], s["mem_w_kv"][layer])
    mo = _memory_attention(qm, kv)
    cat = jnp.concatenate([o, mo], axis=-1)
    return _project(cat, _concat_lo(o[0], mo[0])[None], p["w_o"][layer], s["w_o"][layer])


def _mixer_b(h, h_lo, mem, mem_lo, kv_shared, table, p, s, layer):
    j = layer - N_A
    proj = _project(h, h_lo, p["b_w_in"][j], s["b_w_in"][j])
    o = _swa_sink_attention(proj[0, :, :SWA_Q_W], kv_shared, table, p["b_sinks"][j])[None]
    kv = _project(mem, mem_lo, p["mem_w_kv"][layer], s["mem_w_kv"][layer])
    mo = _memory_attention(proj[..., SWA_Q_W:], kv)
    cat = jnp.concatenate([o, mo], axis=-1)
    return _project(cat, _concat_lo(o[0], mo[0])[None], p["w_o"][layer], s["w_o"][layer])


def _forward(p, s, x, mem, positions):
    table = _rope_table(positions[0], SWA_DH)
    h, h_lo, mem_lo = x, _lo(x), _lo(mem)
    kv_shared = None
    for layer in range(DEPTH):
        if layer < N_A:
            mix = _mixer_a(h, h_lo, mem, mem_lo, p, s, layer)
        else:
            mix = _mixer_b(h, h_lo, mem, mem_lo, kv_shared, table, p, s, layer)
        seq = h.shape[1]
        h2, h2_lo = _ln_res(h[0], mix[0], p["ln_g"][layer, 0][None], p["ln_b"][layer, 0][None])
        down = _mlp(h2, h2_lo, p["mlp_w_up"][layer], p["mlp_w_down"][layer], s["mlp_w_up"][layer],
                    s["mlp_w_down"][layer])
        h, h_lo = _ln_res(h2, down, p["ln_g"][layer, 1][None], p["ln_b"][layer, 1][None])
        h, h_lo = h.reshape(1, seq, D_MODEL), h_lo.reshape(1, seq, D_MODEL)
        if layer == N_A - 1:
            kv_shared = _project(h, h_lo, p["w_kv_shared"], s["w_kv_shared"])[0]
    return h


def _loss(diff, s, p, mem, positions, target):
    y = _forward({**p, **diff["small"]}, s, diff["x"], mem, positions)
    return 0.5 * jnp.sum(jnp.mean(jnp.square(y - target), axis=-1))


def _reorder_a_w_in(w):
    pad = jnp.zeros(w.shape[:-1] + (A_IN_PAD - A_IN,), w.dtype)
    return jnp.concatenate([w[..., :QKVZ_W], w[..., QKVZ_W + 2 * DN_HEADS:], w[..., QKVZ_W:QKVZ_W + 2 * DN_HEADS], pad],
                           axis=-1)


def _restore_a_w_in(w):
    return jnp.concatenate([w[..., :QKVZ_W], w[..., QKVZ_W + MEM_W:QKVZ_W + MEM_W + 2 * DN_HEADS],
                            w[..., QKVZ_W:QKVZ_W + MEM_W]], axis=-1)


def kernel(x, mem, positions, a_w_in, a_conv_w, a_A_log, a_dt_bias, a_norm_w, b_w_in, b_sinks, w_kv_shared, mem_w_kv, w_o, mlp_w_up, mlp_w_down, ln_g, ln_b, loss_target, m_a_w_in, m_a_conv_w, m_a_A_log, m_a_dt_bias, m_a_norm_w, m_b_w_in, m_b_sinks, m_w_kv_shared, m_mem_w_kv, m_w_o, m_mlp_w_up, m_mlp_w_down, m_ln_g, m_ln_b, v_a_w_in, v_a_conv_w, v_a_A_log, v_a_dt_bias, v_a_norm_w, v_b_w_in, v_b_sinks, v_w_kv_shared, v_mem_w_kv, v_w_o, v_mlp_w_up, v_mlp_w_down, v_ln_g, v_ln_b):
    w_sh = dict(a_w_in=a_w_in, a_conv_w=a_conv_w, a_A_log=a_A_log, a_dt_bias=a_dt_bias, a_norm_w=a_norm_w,
                b_w_in=b_w_in, b_sinks=b_sinks, w_kv_shared=w_kv_shared, mem_w_kv=mem_w_kv, w_o=w_o,
                mlp_w_up=mlp_w_up, mlp_w_down=mlp_w_down, ln_g=ln_g, ln_b=ln_b)
    m_sh = dict(a_w_in=m_a_w_in, a_conv_w=m_a_conv_w, a_A_log=m_a_A_log, a_dt_bias=m_a_dt_bias, a_norm_w=m_a_norm_w,
                b_w_in=m_b_w_in, b_sinks=m_b_sinks, w_kv_shared=m_w_kv_shared, mem_w_kv=m_mem_w_kv, w_o=m_w_o,
                mlp_w_up=m_mlp_w_up, mlp_w_down=m_mlp_w_down, ln_g=m_ln_g, ln_b=m_ln_b)
    v_sh = dict(a_w_in=v_a_w_in, a_conv_w=v_a_conv_w, a_A_log=v_a_A_log, a_dt_bias=v_a_dt_bias, a_norm_w=v_a_norm_w,
                b_w_in=v_b_w_in, b_sinks=v_b_sinks, w_kv_shared=v_w_kv_shared, mem_w_kv=v_mem_w_kv, w_o=v_w_o,
                mlp_w_up=v_mlp_w_up, mlp_w_down=v_mlp_w_down, ln_g=v_ln_g, ln_b=v_ln_b)
    shard_shapes = {n: w_sh[n].shape for n in WEIGHTS}
    rb, rows = _rows_for(w_sh)

    big, small = _pack(w_sh, rb, jnp.bfloat16)
    gbig, gsmall = _gather_weights(big.reshape(2, rb // 2, FLAT_W), small.reshape(2, SMALL_ROWS // 2, FLAT_W))
    gbig, gsmall = gbig.reshape(N_CHIPS, rb, FLAT_W), gsmall.reshape(N_CHIPS, SMALL_ROWS, FLAT_W)
    pieces = [_unpack(gbig[q], gsmall[q], shard_shapes) for q in range(N_CHIPS)]
    full = {n: jnp.concatenate([pieces[q][n] for q in range(N_CHIPS)], axis=SHARD_AXIS[n]) for n in SHARD_AXIS}
    for n in REPLICATED:
        full[n] = w_sh[n]
    big_w = {n: full[n] for n in BIG}
    big_w["a_w_in"] = _reorder_a_w_in(big_w["a_w_in"])
    small_w = {n: full[n] for n in SMALL}
    slots = {n: jnp.zeros(big_w[n].shape, jnp.float32) for n in BIG}

    loss, (grads, g_slots) = jax.value_and_grad(_loss, argnums=(0, 1))(
        {"x": x, "small": small_w}, slots, big_w, mem, positions, loss_target)
    loss = lax.psum(loss, ("x", "y", "c"))
    g_full = {**g_slots, **grads["small"]}
    g_full["a_w_in"] = _restore_a_w_in(g_full["a_w_in"])

    def shard_of(n, q):
        if n in REPLICATED:
            return g_full[n]
        size = shard_shapes[n][SHARD_AXIS[n]]
        return lax.slice_in_dim(g_full[n], q * size, (q + 1) * size, axis=SHARD_AXIS[n])

    parts = []
    for q in range(N_CHIPS):
        pb, ps = _pack({n: shard_of(n, q) for n in WEIGHTS}, rb, jnp.bfloat16)
        parts.append(jnp.concatenate([pb, ps.astype(jnp.bfloat16)], axis=0).reshape(2, rows // 2, FLAT_W))
    partials = jnp.stack(parts, axis=1)
    half = lax.axis_index("c").astype(jnp.int32).reshape(1)
    chip_partials = _add_pairs(partials, _swap_halves(partials), half)
    g_flat = _join_halves(_sum_chips(_scatter_grads(chip_partials))).reshape(rows, FLAT_W)

    odd = "a_w_in"
    blank = jnp.zeros(shard_shapes[odd], jnp.float32)
    flat = [jnp.concatenate(_pack({**d, odd: blank}, rb), axis=0) for d in (w_sh, m_sh, v_sh)]
    outs = (g_flat,) + tuple(_adamw(g_flat, *flat))
    g_o, d_o, m_o, v_o = [_unpack(o[:rb], o[rb:], shard_shapes) for o in outs]
    as_rows = lambda t: t.reshape(-1, t.shape[-1])
    updated = _adamw(as_rows(g_o[odd]), as_rows(w_sh[odd]), as_rows(m_sh[odd]), as_rows(v_sh[odd]))
    d_o[odd], m_o[odd], v_o[odd] = [t.reshape(shard_shapes[odd]) for t in updated]
    return (loss, grads["x"], *[g_o[n] for n in WEIGHTS], *[d_o[n] for n in WEIGHTS],
            *[m_o[n] for n in WEIGHTS], *[v_o[n] for n in WEIGHTS])
```

```python
import functools
import math

import jax
import jax.numpy as jnp
from jax import lax
from jax.experimental import pallas as pl
from jax.experimental.pallas import tpu as pltpu

D_MODEL = 1024
DEPTH = 4
N_A = DEPTH // 2
MEM_HEADS = 4
MEM_DH = D_MODEL // 16
MEM_W = MEM_HEADS * MEM_DH
DN_DK = 128
DN_DV = 128
DN_HEADS = (3 * D_MODEL) // (4 * DN_DV)
DN_QK_W = DN_HEADS * DN_DK
DN_V_W = DN_HEADS * DN_DV
CONV_WIDTH = 4
CHUNK = 64
SWA_DH = 64
SWA_HEADS = (3 * D_MODEL) // (4 * SWA_DH)
SWA_KV_HEADS = 2
SWA_GROUP = SWA_HEADS // SWA_KV_HEADS
SWA_Q_W = SWA_HEADS * SWA_DH
SWA_KV_W = SWA_KV_HEADS * SWA_DH
WINDOW = 128
ROPE_THETA = 10000.0
LN_EPS = 1e-5
NORM_EPS = 1e-6
DN_ALPHA = (2.0 * DEPTH) ** 0.25
A_IN = 2 * DN_QK_W + 2 * DN_V_W + 2 * DN_HEADS + MEM_W
A_IN_PAD = 3456
QKVZ_W = 2 * DN_QK_W + 2 * DN_V_W

ADAM_LR = 0.001
ADAM_B1 = 0.9
ADAM_B2 = 0.999
ADAM_EPS = 1e-08
ADAM_WD = 0.01
ADAM_STEP = 10

N_CHIPS = 4
FLAT_W = 1024
BIG = ("a_w_in", "b_w_in", "w_kv_shared", "mem_w_kv", "w_o", "mlp_w_up", "mlp_w_down")
SMALL = ("a_conv_w", "ln_g", "ln_b", "a_A_log", "a_dt_bias", "a_norm_w", "b_sinks")
REPLICATED = ("a_A_log", "a_dt_bias", "a_norm_w", "b_sinks")
WEIGHTS = ("a_w_in", "a_conv_w", "a_A_log", "a_dt_bias", "a_norm_w", "b_w_in", "b_sinks", "w_kv_shared",
           "mem_w_kv", "w_o", "mlp_w_up", "mlp_w_down", "ln_g", "ln_b")
SHARD_AXIS = {"a_w_in": 2, "a_conv_w": 2, "b_w_in": 1, "w_kv_shared": 0, "mem_w_kv": 1, "w_o": 1,
              "mlp_w_up": 2, "mlp_w_down": 1, "ln_g": 2, "ln_b": 2}
SMALL_ROWS = 32
ROW_ALIGN = 256

MESH = pl.DeviceIdType.MESH
HBM_SPEC = pl.BlockSpec(memory_space=pltpu.HBM)
VMEM_LIMIT = 48 * 1024 * 1024


def _rows_for(shards):
    n_big = sum(math.prod(shards[n].shape) for n in BIG)
    n_small = sum(math.prod(shards[n].shape) for n in SMALL)
    assert n_small <= SMALL_ROWS * FLAT_W
    total = -(-n_big // FLAT_W) + SMALL_ROWS
    total = -(-total // (2 * ROW_ALIGN)) * (2 * ROW_ALIGN)
    return total - SMALL_ROWS, total


def _pack(shards, rb, dtype_big=jnp.float32):
    big = jnp.concatenate([shards[n].reshape(-1).astype(dtype_big) for n in BIG])
    big = jnp.pad(big, (0, rb * FLAT_W - big.shape[0])).reshape(rb, FLAT_W)
    small = jnp.concatenate([shards[n].reshape(-1).astype(jnp.float32) for n in SMALL])
    small = jnp.pad(small, (0, SMALL_ROWS * FLAT_W - small.shape[0])).reshape(SMALL_ROWS, FLAT_W)
    return big, small


def _unpack(big, small, shapes):
    out = {}
    for flat, names in ((big.reshape(-1), BIG), (small.reshape(-1), SMALL)):
        off = 0
        for n in names:
            size = math.prod(shapes[n])
            out[n] = flat[off:off + size].reshape(shapes[n])
            off += size
    return out


def _other_chips(x, y):
    return [(1 - x, y), (x, 1 - y), (1 - x, 1 - y)]


def _gather_weights(big, small):
    def body(big_ref, small_ref, init_big_ref, init_small_ref, obig_ref, osmall_ref,
             send_sems, recv_sems, pass_send_sems, pass_recv_sems):
        del init_big_ref, init_small_ref
        x, y, c = lax.axis_index("x"), lax.axis_index("y"), lax.axis_index("c")
        me = 2 * x + y
        sibling = (x, y, 1 - c)
        pairs = ((big_ref, obig_ref), (small_ref, osmall_ref))
        sends = []
        for j, (px, py) in enumerate(_other_chips(x, y)):
            for i, (src, dst) in enumerate(pairs):
                sends.append(pltpu.make_async_remote_copy(
                    src_ref=src.at[c], dst_ref=dst.at[me, c], send_sem=send_sems.at[2 * j + i],
                    recv_sem=recv_sems.at[2 * j + i], device_id=(px, py, c), device_id_type=MESH))
        for cp in sends:
            cp.start()
        passed = []
        for j, (px, py) in enumerate(_other_chips(x, y)):
            for i, (src, dst) in enumerate(pairs):
                landed = dst.at[2 * px + py, c]
                pltpu.make_async_remote_copy(
                    src_ref=src.at[c], dst_ref=landed, send_sem=send_sems.at[2 * j + i],
                    recv_sem=recv_sems.at[2 * j + i], device_id=(px, py, c), device_id_type=MESH).wait_recv()
                passed.append(pltpu.make_async_remote_copy(
                    src_ref=landed, dst_ref=landed, send_sem=pass_send_sems.at[2 * j + i],
                    recv_sem=pass_recv_sems.at[2 * j + i], device_id=sibling, device_id_type=MESH))
                passed[-1].start()
        for j, (px, py) in enumerate(_other_chips(x, y)):
            for i, (src, dst) in enumerate(pairs):
                other_half = dst.at[2 * px + py, 1 - c]
                pltpu.make_async_remote_copy(
                    src_ref=other_half, dst_ref=other_half, send_sem=pass_send_sems.at[2 * j + i],
                    recv_sem=pass_recv_sems.at[2 * j + i], device_id=sibling, device_id_type=MESH).wait_recv()
        for cp in sends + passed:
            cp.wait_send()

    dma6 = pltpu.SemaphoreType.DMA((6,))
    four = lambda t: jnp.broadcast_to(t[None], (N_CHIPS,) + t.shape)
    return pl.pallas_call(
        body, name="gather_weights",
        out_shape=(jax.ShapeDtypeStruct((N_CHIPS,) + big.shape, big.dtype),
                   jax.ShapeDtypeStruct((N_CHIPS,) + small.shape, small.dtype)),
        in_specs=[HBM_SPEC] * 4, out_specs=(HBM_SPEC, HBM_SPEC), input_output_aliases={2: 0, 3: 1},
        scratch_shapes=[dma6, dma6, dma6, dma6],
    )(big, small, four(big), four(small))


def _scatter_grads(g):
    def body(g_ref, o_ref, send_sems, recv_sems, local_sem):
        x, y, c = lax.axis_index("x"), lax.axis_index("y"), lax.axis_index("c")
        me = 2 * x + y
        local = pltpu.make_async_copy(g_ref.at[me], o_ref.at[me], local_sem)
        local.start()
        sends = []
        for j, (px, py) in enumerate(_other_chips(x, y)):
            sends.append(pltpu.make_async_remote_copy(
                src_ref=g_ref.at[2 * px + py], dst_ref=o_ref.at[me], send_sem=send_sems.at[j], recv_sem=recv_sems.at[j],
                device_id=(px, py, c), device_id_type=MESH))
        for cp in sends:
            cp.start()
        for j, (px, py) in enumerate(_other_chips(x, y)):
            pltpu.make_async_remote_copy(
                src_ref=g_ref.at[me], dst_ref=o_ref.at[2 * px + py], send_sem=send_sems.at[j], recv_sem=recv_sems.at[j],
                device_id=(px, py, c), device_id_type=MESH).wait_recv()
        for cp in sends:
            cp.wait_send()
        local.wait()

    return pl.pallas_call(
        body, name="scatter_grads",
        out_shape=jax.ShapeDtypeStruct(g.shape, g.dtype),
        in_specs=[HBM_SPEC], out_specs=HBM_SPEC,
        scratch_shapes=[pltpu.SemaphoreType.DMA((3,)), pltpu.SemaphoreType.DMA((3,)), pltpu.SemaphoreType.DMA],
    )(g)


def _swap_halves(g):
    def body(g_ref, o_ref, send_sem, recv_sem):
        x, y, c = lax.axis_index("x"), lax.axis_index("y"), lax.axis_index("c")
        cp = pltpu.make_async_remote_copy(src_ref=g_ref.at[1 - c], dst_ref=o_ref, send_sem=send_sem, recv_sem=recv_sem,
                                          device_id=(x, y, 1 - c), device_id_type=MESH)
        cp.start()
        cp.wait()

    return pl.pallas_call(
        body, name="swap_halves",
        out_shape=jax.ShapeDtypeStruct(g.shape[1:], g.dtype),
        in_specs=[HBM_SPEC], out_specs=HBM_SPEC,
        scratch_shapes=[pltpu.SemaphoreType.DMA, pltpu.SemaphoreType.DMA],
    )(g)


def _join_halves(v):
    def body(v_ref, init_ref, o_ref, send_sem, recv_sem):
        del init_ref
        x, y, c = lax.axis_index("x"), lax.axis_index("y"), lax.axis_index("c")
        cp = pltpu.make_async_remote_copy(src_ref=v_ref, dst_ref=o_ref.at[c], send_sem=send_sem, recv_sem=recv_sem,
                                          device_id=(x, y, 1 - c), device_id_type=MESH)
        cp.start()
        cp.wait_send()
        pltpu.make_async_remote_copy(src_ref=v_ref, dst_ref=o_ref.at[1 - c], send_sem=send_sem, recv_sem=recv_sem,
                                     device_id=(x, y, 1 - c), device_id_type=MESH).wait_recv()

    return pl.pallas_call(
        body, name="join_halves",
        out_shape=jax.ShapeDtypeStruct((2,) + v.shape, v.dtype),
        in_specs=[HBM_SPEC, HBM_SPEC], out_specs=HBM_SPEC, input_output_aliases={1: 0},
        scratch_shapes=[pltpu.SemaphoreType.DMA, pltpu.SemaphoreType.DMA],
    )(v, jnp.stack([v, v]))


def _add_pairs(g, theirs, half):
    _, n, rows, width = g.shape
    assert rows % ROW_ALIGN == 0, rows

    def body(half_ref, g_ref, t_ref, o_ref):
        o_ref[...] = (g_ref[...].astype(jnp.float32) + t_ref[...].astype(jnp.float32)).astype(o_ref.dtype)

    blk = pl.BlockSpec((None, ROW_ALIGN, width), lambda p, i, h: (p, i, 0))
    grid_spec = pltpu.PrefetchScalarGridSpec(
        num_scalar_prefetch=1, grid=(n, rows // ROW_ALIGN),
        in_specs=[pl.BlockSpec((None, None, ROW_ALIGN, width), lambda p, i, h: (h[0], p, i, 0)), blk], out_specs=blk)
    return pl.pallas_call(
        body, name="add_pairs", grid_spec=grid_spec, out_shape=jax.ShapeDtypeStruct(theirs.shape, g.dtype),
        compiler_params=pltpu.CompilerParams(dimension_semantics=("parallel", "parallel")),
    )(half, g, theirs)


def _sum_chips(parts):
    n, rows, width = parts.shape
    assert rows % ROW_ALIGN == 0, rows

    def body(p_ref, o_ref):
        p = [p_ref[q].astype(jnp.float32) for q in range(n)]
        o_ref[...] = (p[0] + p[1]) + (p[2] + p[3])

    return pl.pallas_call(
        body, name="sum_chips", grid=(rows // ROW_ALIGN,),
        out_shape=jax.ShapeDtypeStruct((rows, width), jnp.float32),
        in_specs=[pl.BlockSpec((n, ROW_ALIGN, width), lambda i: (0, i, 0))],
        out_specs=pl.BlockSpec((ROW_ALIGN, width), lambda i: (i, 0)),
        compiler_params=pltpu.CompilerParams(dimension_semantics=("parallel",), vmem_limit_bytes=VMEM_LIMIT),
    )(parts)


def _adamw(g, w, m, v):
    rows, width = w.shape
    blk = ROW_ALIGN // 2

    def body(g_ref, w_ref, m_ref, v_ref, d_out, m_out, v_out):
        g = g_ref[...]
        m_new = ADAM_B1 * m_ref[...] + (1.0 - ADAM_B1) * g
        v_new = ADAM_B2 * v_ref[...] + (1.0 - ADAM_B2) * jnp.square(g)
        m_hat = m_new / (1.0 - ADAM_B1 ** ADAM_STEP)
        v_hat = v_new / (1.0 - ADAM_B2 ** ADAM_STEP)
        d_out[...] = -ADAM_LR * (m_hat / (jnp.sqrt(v_hat) + ADAM_EPS) + ADAM_WD * w_ref[...])
        m_out[...] = m_new
        v_out[...] = v_new

    spec = pl.BlockSpec((blk, width), lambda i: (i, 0))
    shape = jax.ShapeDtypeStruct((rows, width), jnp.float32)
    return pl.pallas_call(
        body, name="adamw", grid=(rows // blk,),
        out_shape=(shape,) * 3, in_specs=[spec] * 4, out_specs=(spec,) * 3,
        compiler_params=pltpu.CompilerParams(dimension_semantics=("parallel",), vmem_limit_bytes=VMEM_LIMIT),
    )(g, w, m, v)


def _tile(dim, pref):
    if dim <= pref:
        return dim
    for t in range(pref - pref % 128, 0, -128):
        if dim % t == 0:
            return t
    raise ValueError(f"no 128-aligned tile for {dim}")


def _matmul(a, b, *, ta=False, tb=False, name, epilogue=None, extra=None, out_dtype=jnp.float32):
    (k_a, m) = a.shape if ta else a.shape[::-1]
    (k_b, n) = b.shape[::-1] if tb else b.shape
    assert k_a == k_b, (a.shape, b.shape, ta, tb)
    k = k_a
    tk = _tile(k, 1152)
    nk = k // tk
    if ta:
        tm, tn = _tile(m, 1024), _tile(n, 2048 if m <= 1024 else 1024)
    else:
        tm, tn = _tile(m, 2048), _tile(n, 1152)
    a_spec = pl.BlockSpec((tk, tm), lambda i, j, l: (l, i)) if ta else pl.BlockSpec((tm, tk), lambda i, j, l: (i, l))
    b_spec = pl.BlockSpec((tn, tk), lambda i, j, l: (j, l)) if tb else pl.BlockSpec((tk, tn), lambda i, j, l: (l, j))
    o_spec = pl.BlockSpec((tm, tn), lambda i, j, l: (i, j))
    dims = (((0 if ta else 1,), (1 if tb else 0,)), ((), ()))
    has_extra = epilogue == "relu2_grad"
    assert has_extra == (extra is not None)

    def body(*refs):
        a_ref, b_ref = refs[:2]
        outs = refs[2 + has_extra:2 + has_extra + (2 if epilogue == "relu2" else 1)]
        l = pl.program_id(2)
        part = lax.dot_general(a_ref[...].astype(jnp.bfloat16), b_ref[...].astype(jnp.bfloat16), dims,
                               preferred_element_type=jnp.float32)

        def finish(acc):
            if epilogue is None:
                outs[0][...] = acc.astype(out_dtype)
            elif epilogue == "relu2":
                outs[0][...] = acc.astype(jnp.bfloat16)
                outs[1][...] = jnp.square(jnp.maximum(acc, 0.0)).astype(jnp.bfloat16)
            else:
                outs[0][...] = (acc * (2.0 * jnp.maximum(refs[2][...].astype(jnp.float32), 0.0))).astype(out_dtype)

        if nk == 1:
            finish(part)
            return
        acc_ref = refs[-1]

        @pl.when(l == 0)
        def _():
            acc_ref[...] = part

        @pl.when((l > 0) & (l < nk - 1))
        def _():
            acc_ref[...] += part

        @pl.when(l == nk - 1)
        def _():
            finish(acc_ref[...] + part)

    if epilogue == "relu2":
        out_shape = (jax.ShapeDtypeStruct((m, n), jnp.bfloat16),) * 2
        out_specs = (o_spec, o_spec)
    else:
        out_shape = jax.ShapeDtypeStruct((m, n), out_dtype)
        out_specs = o_spec
    return pl.pallas_call(
        body, name=name, grid=(m // tm, n // tn, nk), out_shape=out_shape,
        in_specs=[a_spec, b_spec] + ([o_spec] if has_extra else []), out_specs=out_specs,
        scratch_shapes=[pltpu.VMEM((tm, tn), jnp.float32)] if nk > 1 else [],
        compiler_params=pltpu.CompilerParams(dimension_semantics=("parallel", "parallel", "arbitrary"),
                                             vmem_limit_bytes=VMEM_LIMIT),
    )(*((a, b) + ((extra,) if has_extra else ())))


def _lo(x):
    return lax.stop_gradient(x.astype(jnp.bfloat16))


@jax.custom_vjp
def _linear(x, x_lo, w, slot):
    del x, slot
    return _matmul(x_lo, w, name="linear_fwd")


def _linear_fwd(x, x_lo, w, slot):
    del x, slot
    return _matmul(x_lo, w, name="linear_fwd"), (x_lo, w)


def _linear_bwd(res, dy):
    x_lo, w = res
    dy = dy.astype(jnp.bfloat16)
    dx = _matmul(dy, w, tb=True, name="linear_dx")
    dw = _matmul(x_lo, dy, ta=True, name="linear_dw")
    return dx, jnp.zeros_like(x_lo), jnp.zeros_like(w), dw


_linear.defvjp(_linear_fwd, _linear_bwd)


@jax.custom_vjp
def _mlp(h, h_lo, w_up, w_down, slot_up, slot_down):
    return _mlp_fwd(h, h_lo, w_up, w_down, slot_up, slot_down)[0]


def _mlp_fwd(h, h_lo, w_up, w_down, slot_up, slot_down):
    del h, slot_up, slot_down
    up, act = _matmul(h_lo, w_up, name="mlp_up", epilogue="relu2")
    return _matmul(act, w_down, name="mlp_down"), (h_lo, up, act, w_up, w_down)


def _mlp_bwd(res, dy):
    h_lo, up, act, w_up, w_down = res
    dy = dy.astype(jnp.bfloat16)
    d_up = _matmul(dy, w_down, tb=True, name="mlp_d_up", epilogue="relu2_grad", extra=up, out_dtype=jnp.bfloat16)
    dw_down = _matmul(act, dy, ta=True, name="mlp_dw_down")
    dw_up = _matmul(h_lo, d_up, ta=True, name="mlp_dw_up")
    dh = _matmul(d_up, w_up, tb=True, name="mlp_dh")
    return dh, jnp.zeros_like(h_lo), jnp.zeros_like(w_up), jnp.zeros_like(w_down), dw_up, dw_down


_mlp.defvjp(_mlp_fwd, _mlp_bwd)


LN_ROWS = 256


def _ln_call(h, mix, g, b):
    s, d = h.shape
    tok = pl.BlockSpec((LN_ROWS, d), lambda i: (i, 0))
    vec = pl.BlockSpec((1, d), lambda i: (0, 0))
    stat = pl.BlockSpec((LN_ROWS, 1), lambda i: (i, 0))

    def body(h_ref, mix_ref, g_ref, b_ref, y_ref, ylo_ref, xhat_ref, rstd_ref):
        z = DN_ALPHA * h_ref[...] + mix_ref[...]
        mu = jnp.mean(z, axis=-1, keepdims=True)
        zc = z - mu
        rstd = lax.rsqrt(jnp.mean(jnp.square(zc), axis=-1, keepdims=True) + LN_EPS)
        xhat = zc * rstd
        y = xhat * g_ref[...] + b_ref[...]
        y_ref[...] = y
        ylo_ref[...] = y.astype(ylo_ref.dtype)
        xhat_ref[...] = xhat
        rstd_ref[...] = rstd

    sd = jax.ShapeDtypeStruct
    return pl.pallas_call(
        body, name="ln_fwd", grid=(s // LN_ROWS,),
        out_shape=(sd((s, d), jnp.float32), sd((s, d), jnp.bfloat16), sd((s, d), jnp.float32), sd((s, 1), jnp.float32)),
        in_specs=[tok, tok, vec, vec], out_specs=(tok, tok, tok, stat),
        compiler_params=pltpu.CompilerParams(dimension_semantics=("parallel",)),
    )(h, mix, g, b)


def _ln_grad_call(dy, xhat, rstd, g):
    s, d = dy.shape
    tok = pl.BlockSpec((LN_ROWS, d), lambda i: (i, 0))
    vec = pl.BlockSpec((1, d), lambda i: (0, 0))
    stat = pl.BlockSpec((LN_ROWS, 1), lambda i: (i, 0))

    def body(dy_ref, xhat_ref, rstd_ref, g_ref, dz_ref, dg_ref, db_ref):
        @pl.when(pl.program_id(0) == 0)
        def _():
            dg_ref[...] = jnp.zeros_like(dg_ref)
            db_ref[...] = jnp.zeros_like(db_ref)

        dy, xhat = dy_ref[...], xhat_ref[...]
        dyg = dy * g_ref[...]
        m1 = jnp.mean(dyg, axis=-1, keepdims=True)
        m2 = jnp.mean(dyg * xhat, axis=-1, keepdims=True)
        dz_ref[...] = rstd_ref[...] * (dyg - m1 - xhat * m2)
        dg_ref[...] += jnp.sum(dy * xhat, axis=0, keepdims=True)
        db_ref[...] += jnp.sum(dy, axis=0, keepdims=True)

    sd = jax.ShapeDtypeStruct
    return pl.pallas_call(
        body, name="ln_bwd", grid=(s // LN_ROWS,),
        out_shape=(sd((s, d), jnp.float32), sd((1, d), jnp.float32), sd((1, d), jnp.float32)),
        in_specs=[tok, tok, stat, vec], out_specs=(tok, vec, vec),
        compiler_params=pltpu.CompilerParams(dimension_semantics=("arbitrary",)),
    )(dy, xhat, rstd, g)


@jax.custom_vjp
def _ln_res(h, mix, g, b):
    return _ln_call(h, mix, g, b)[:2]


def _ln_res_fwd(h, mix, g, b):
    y, y_lo, xhat, rstd = _ln_call(h, mix, g, b)
    return (y, y_lo), (xhat, rstd, g)


def _ln_res_bwd(res, cts):
    xhat, rstd, g = res
    dz, dg, db = _ln_grad_call(cts[0], xhat, rstd, g)
    return DN_ALPHA * dz, dz, dg, db


_ln_res.defvjp(_ln_res_fwd, _ln_res_bwd)


MXU_DTYPE = jnp.bfloat16
DN_CB = 16
DN_GROUP = 8
DN_SCAN_CB = 4
DN_SCALE = DN_DK ** -0.5


def _dot(a, b, ca=1, cb=0):
    return lax.dot_general(a.astype(MXU_DTYPE), b.astype(MXU_DTYPE), (((ca,), (cb,)), ((), ())),
                           preferred_element_type=jnp.float32)


def _chunk_masks():
    row = lax.broadcasted_iota(jnp.int32, (CHUNK, CHUNK), 0)
    col = lax.broadcasted_iota(jnp.int32, (CHUNK, CHUNK), 1)
    return row >= col, row > col, row == col


def _to_col(row_vec):
    _, _, eye = _chunk_masks()
    return jnp.sum(jnp.where(eye, jnp.broadcast_to(row_vec, (CHUNK, CHUNK)), 0.0), axis=1, keepdims=True)


def _to_row(col_vec):
    _, _, eye = _chunk_masks()
    return jnp.sum(jnp.where(eye, jnp.broadcast_to(col_vec, (CHUNK, CHUNK)), 0.0), axis=0, keepdims=True)


def _last_row(col_vec):
    last = lax.broadcasted_iota(jnp.int32, (CHUNK, 1), 0) == CHUNK - 1
    return jnp.sum(jnp.where(last, col_vec, 0.0), axis=0, keepdims=True), last


def _chunk_terms(q, k, beta, gcc, gcr):
    incl, strict, _ = _chunk_masks()
    decay = jnp.where(incl, jnp.exp(jnp.minimum(gcc - gcr, 0.0)), 0.0)
    kb = k * beta
    lmat = jnp.where(strict, _dot(kb, k, 1, 1) * decay, 0.0)
    intra = jnp.where(incl, _dot(q, k, 1, 1) * decay, 0.0)
    return decay, kb, lmat, intra


def _dot3(a, b, ca=1, cb=0):
    if MXU_DTYPE == jnp.float32:
        return _dot(a, b, ca, cb)
    a_hi, b_hi = a.astype(MXU_DTYPE), b.astype(MXU_DTYPE)
    a_lo = (a - a_hi.astype(jnp.float32)).astype(MXU_DTYPE)
    b_lo = (b - b_hi.astype(jnp.float32)).astype(MXU_DTYPE)
    return _dot(a_hi, b_hi, ca, cb) + (_dot(a_hi, b_lo, ca, cb) + _dot(a_lo, b_hi, ca, cb))


def _unit_lower_inverse(lmats):
    _, _, eye = _chunk_masks()
    ident = jnp.where(eye, 1.0, 0.0)
    ts = [ident - m for m in lmats]
    ps = [_dot(m, m) for m in lmats]
    for _ in range(4):
        ts = [t + _dot(t, p) for t, p in zip(ts, ps)]
        ps = [_dot(p, p) for p in ps]
    ts = [t + _dot(t, p) for t, p in zip(ts, ps)]
    resids = [(t - ident) + _dot3(m, t) for m, t in zip(lmats, ts)]
    return [t - _dot(t, r) for t, r in zip(ts, resids)]


def _dn_specs(n_chunks):
    tok = pl.BlockSpec((DN_CB * CHUNK, DN_DK), lambda h, n: (n, h))
    rowv = pl.BlockSpec((None, DN_CB, CHUNK), lambda h, n: (h, n, 0))
    sq = pl.BlockSpec((None, DN_CB, CHUNK, CHUNK), lambda h, n: (h, n, 0, 0))
    lane = pl.BlockSpec((None, DN_CB, 1, DN_DV), lambda h, n: (h, n, 0, 0))
    planes = [pl.BlockSpec((None, DN_CB * CHUNK, DN_DK), functools.partial(lambda h, n, p: (p, n, h), p=p))
              for p in range(3)]
    return tok, rowv, sq, lane, planes


def _dn_prep(qkv, beta, gc):
    s = qkv.shape[1]
    n_chunks = s // CHUNK
    tok, rowv, sq, lane, planes = _dn_specs(n_chunks)
    tok_shape = qkv.shape[1:]

    def body(q_ref, k_ref, v_ref, beta_ref, gc_ref, u_ref, w_ref, qd_ref, kd_ref, intra_ref, t_ref, cd_ref):
        for c0 in range(0, DN_CB, DN_GROUP):
            chunks = range(c0, c0 + DN_GROUP)
            rhs, lmats = [], []
            for c in chunks:
                rows = pl.ds(c * CHUNK, CHUNK)
                q_c, k_c, v_c = q_ref[rows, :] * DN_SCALE, k_ref[rows, :], v_ref[rows, :]
                gcr_c = gc_ref[pl.ds(c, 1), :]
                beta_c, gcc_c = _to_col(beta_ref[pl.ds(c, 1), :]), _to_col(gcr_c)
                _, kb, lmat, intra = _chunk_terms(q_c, k_c, beta_c, gcc_c, gcr_c)
                eg = jnp.exp(gcc_c)
                g_last, _ = _last_row(gcc_c)
                qd_ref[rows, :] = (q_c * eg).astype(qd_ref.dtype)
                kd_ref[rows, :] = (k_c * jnp.exp(g_last - gcc_c)).astype(kd_ref.dtype)
                intra_ref[c] = intra.astype(intra_ref.dtype)
                cd_ref[c] = jnp.broadcast_to(jnp.exp(g_last), (1, DN_DV))
                rhs.append(jnp.concatenate([v_c * beta_c, kb * eg], axis=1))
                lmats.append(lmat)
            ts = _unit_lower_inverse(lmats)
            sols = [_dot3(t, r) for t, r in zip(ts, rhs)]
            for c, t, sol in zip(chunks, ts, sols):
                rows = pl.ds(c * CHUNK, CHUNK)
                t_ref[c] = t
                u_ref[rows, :] = sol[:, :DN_DV]
                w_ref[rows, :] = sol[:, DN_DV:].astype(w_ref.dtype)

    f32, mx = jnp.float32, MXU_DTYPE
    sd = jax.ShapeDtypeStruct
    return pl.pallas_call(
        body, name="dn_prep", grid=(DN_HEADS, n_chunks // DN_CB),
        out_shape=(sd(tok_shape, f32), sd(tok_shape, mx), sd(tok_shape, mx), sd(tok_shape, mx),
                   sd((DN_HEADS, n_chunks, CHUNK, CHUNK), mx), sd((DN_HEADS, n_chunks, CHUNK, CHUNK), f32),
                   sd((DN_HEADS, n_chunks, 1, DN_DV), f32)),
        in_specs=planes + [rowv, rowv], out_specs=(tok, tok, tok, tok, sq, sq, lane),
        compiler_params=pltpu.CompilerParams(dimension_semantics=("parallel", "parallel")),
    )(qkv, qkv, qkv, beta, gc)


def _dn_scan(u, w, qd, kd, intra, cd):
    s, width = u.shape
    n_chunks = s // CHUNK
    cb = DN_SCAN_CB
    tok = pl.BlockSpec((cb * CHUNK, width), lambda n: (n, 0))
    sq = pl.BlockSpec((DN_HEADS, cb, CHUNK, CHUNK), lambda n: (0, n, 0, 0))
    lane = pl.BlockSpec((DN_HEADS, cb, 1, DN_DV), lambda n: (0, n, 0, 0))
    st = pl.BlockSpec((DN_HEADS, cb, DN_DK, DN_DV), lambda n: (0, n, 0, 0))

    def body(u_ref, w_ref, qd_ref, kd_ref, intra_ref, cd_ref, o_ref, vn_ref, st_ref, state):
        @pl.when(pl.program_id(0) == 0)
        def _():
            state[...] = jnp.zeros_like(state)

        heads = range(DN_HEADS)
        cols = [pl.ds(h * DN_DK, DN_DK) for h in heads]
        s_f = [state[h] for h in heads]
        for c in range(cb):
            rows = pl.ds(c * CHUNK, CHUNK)
            s_mx = [s.astype(MXU_DTYPE) for s in s_f]
            for h in heads:
                st_ref[h, c] = s_mx[h]
            ws = [_dot(w_ref[rows, cols[h]], s_mx[h]) for h in heads]
            qs = [_dot(qd_ref[rows, cols[h]], s_mx[h]) for h in heads]
            v_new = [(u_ref[rows, cols[h]] - ws[h]).astype(MXU_DTYPE) for h in heads]
            inner = [_dot(intra_ref[h, c], v_new[h]) for h in heads]
            outer = [_dot(kd_ref[rows, cols[h]], v_new[h], 0, 0) for h in heads]
            for h in heads:
                vn_ref[rows, cols[h]] = v_new[h]
                o_ref[rows, cols[h]] = qs[h] + inner[h]
            s_f = [s_f[h] * cd_ref[h, c] + outer[h] for h in heads]
        for h in heads:
            state[h] = s_f[h]

    sd = jax.ShapeDtypeStruct
    return pl.pallas_call(
        body, name="dn_scan", grid=(n_chunks // cb,),
        out_shape=(sd(u.shape, jnp.float32), sd(u.shape, MXU_DTYPE),
                   sd((DN_HEADS, n_chunks, DN_DK, DN_DV), MXU_DTYPE)),
        in_specs=[tok, tok, tok, tok, sq, lane], out_specs=(tok, tok, st),
        scratch_shapes=[pltpu.VMEM((DN_HEADS, DN_DK, DN_DV), jnp.float32)],
        compiler_params=pltpu.CompilerParams(dimension_semantics=("arbitrary",)),
    )(u, w, qd, kd, intra, cd)


def _dn_bwd_scan(do, w, qd, kd, intra, cd, vn, st):
    s, width = do.shape
    n_chunks = s // CHUNK
    cb = DN_SCAN_CB
    last = n_chunks // cb - 1
    tok = pl.BlockSpec((cb * CHUNK, width), lambda n: (last - n, 0))
    sq = pl.BlockSpec((DN_HEADS, cb, CHUNK, CHUNK), lambda n: (0, last - n, 0, 0))
    lane = pl.BlockSpec((DN_HEADS, cb, 1, DN_DV), lambda n: (0, last - n, 0, 0))
    stt = pl.BlockSpec((DN_HEADS, cb, DN_DK, DN_DV), lambda n: (0, last - n, 0, 0))

    def body(do_ref, w_ref, qd_ref, kd_ref, intra_ref, cd_ref, vn_ref, st_ref,
             du_ref, dw_ref, dqd_ref, dkd_ref, dintra_ref, dgl_ref, dstate):
        @pl.when(pl.program_id(0) == 0)
        def _():
            dstate[...] = jnp.zeros_like(dstate)

        heads = range(DN_HEADS)
        cols = [pl.ds(h * DN_DK, DN_DK) for h in heads]
        ds_f = [dstate[h] for h in heads]
        for c in reversed(range(cb)):
            rows = pl.ds(c * CHUNK, CHUNK)
            ds_mx = [d.astype(MXU_DTYPE) for d in ds_f]
            do_h = [do_ref[rows, cols[h]].astype(MXU_DTYPE) for h in heads]
            dv_a = [_dot(intra_ref[h, c], do_h[h], 0, 0) for h in heads]
            dv_b = [_dot(kd_ref[rows, cols[h]], ds_mx[h]) for h in heads]
            d_intra = [_dot(do_h[h], vn_ref[rows, cols[h]], 1, 1) for h in heads]
            d_qd = [_dot(do_h[h], st_ref[h, c], 1, 1) for h in heads]
            d_kd = [_dot(vn_ref[rows, cols[h]], ds_mx[h], 1, 1) for h in heads]
            ds_q = [_dot(qd_ref[rows, cols[h]], do_h[h], 0, 0) for h in heads]
            dv_new = [dv_a[h] + dv_b[h] for h in heads]
            dv_mx = [d.astype(MXU_DTYPE) for d in dv_new]
            d_w = [_dot(dv_mx[h], st_ref[h, c], 1, 1) for h in heads]
            ds_w = [_dot(w_ref[rows, cols[h]], dv_mx[h], 0, 0) for h in heads]
            ds_next = []
            for h in heads:
                du_ref[rows, cols[h]] = dv_new[h]
                dintra_ref[h, c] = d_intra[h]
                dqd_ref[rows, cols[h]] = d_qd[h]
                dkd_ref[rows, cols[h]] = d_kd[h]
                dw_ref[rows, cols[h]] = -d_w[h]
                cd_h = cd_ref[h, c]
                dcd = jnp.sum(jnp.sum(st_ref[h, c].astype(jnp.float32) * ds_f[h], axis=1, keepdims=True), axis=0,
                              keepdims=True)
                dgl_ref[h, c] = dcd * cd_h
                ds_next.append(ds_q[h] + ds_f[h] * cd_h - ds_w[h])
            ds_f = ds_next
        for h in heads:
            dstate[h] = ds_f[h]

    sd = jax.ShapeDtypeStruct
    f32 = jnp.float32
    return pl.pallas_call(
        body, name="dn_bwd_scan", grid=(n_chunks // cb,),
        out_shape=(sd(do.shape, f32), sd(do.shape, f32), sd(do.shape, f32), sd(do.shape, f32),
                   sd((DN_HEADS, n_chunks, CHUNK, CHUNK), f32), sd((DN_HEADS, n_chunks, 1, DN_DV), f32)),
        in_specs=[tok, tok, tok, tok, sq, lane, tok, stt], out_specs=(tok, tok, tok, tok, sq, lane),
        scratch_shapes=[pltpu.VMEM((DN_HEADS, DN_DK, DN_DV), f32)],
        compiler_params=pltpu.CompilerParams(dimension_semantics=("arbitrary",)),
    )(do, w, qd, kd, intra, cd, vn, st)


def _dn_bwd_chunks(qkv, beta, gc, t, u, w, du, dw, dqd, dkd, dintra, dgl):
    s = qkv.shape[1]
    n_chunks = s // CHUNK
    tok, rowv, sq, lane, planes = _dn_specs(n_chunks)
    all_planes = pl.BlockSpec((3, DN_CB * CHUNK, DN_DK), lambda h, n: (0, n, h))

    def body(q_ref, k_ref, v_ref, beta_ref, gc_ref, t_ref, u_ref, w_ref, du_ref, dw_ref, dqd_ref, dkd_ref,
             dintra_ref, dgl_ref, dqkv_ref, dbeta_ref, dgc_ref):
        incl, strict, _ = _chunk_masks()

        def first(c):
            rows = pl.ds(c * CHUNK, CHUNK)
            q_c, k_c = q_ref[rows, :] * DN_SCALE, k_ref[rows, :]
            gcr_c = gc_ref[pl.ds(c, 1), :]
            beta_c, gcc_c = _to_col(beta_ref[pl.ds(c, 1), :]), _to_col(gcr_c)
            decay, kb, lmat, intra = _chunk_terms(q_c, k_c, beta_c, gcc_c, gcr_c)
            d_sol = jnp.concatenate([du_ref[rows, :], dw_ref[rows, :]], axis=1)
            d_rhs = _dot3(t_ref[c], d_sol, 0, 0)
            return dict(rows=rows, q=q_c, k=k_c, beta=beta_c, gcc=gcc_c, decay=decay, kb=kb, lmat=lmat, intra=intra,
                        d_rhs=d_rhs)

        def second(c, e):
            sol = jnp.concatenate([u_ref[e["rows"], :], w_ref[e["rows"], :].astype(jnp.float32)], axis=1)
            e["d_l"] = jnp.where(strict, -_dot(e["d_rhs"], sol, 1, 1), 0.0)
            e["d_intra"] = jnp.where(incl, dintra_ref[c], 0.0)
            d_qk = e["d_intra"] * e["decay"]
            e["dq"] = _dot(d_qk, e["k"])
            e["dk"] = _dot(d_qk, e["q"], 0, 0)

        def third(e):
            d_a = e["d_l"] * e["decay"]
            e["dkb"] = _dot(d_a, e["k"])
            e["dk"] = e["dk"] + _dot(d_a, e["kb"], 0, 0)

        def last(c, e):
            rows, q_c, k_c, beta_c, gcc_c = e["rows"], e["q"], e["k"], e["beta"], e["gcc"]
            v_c = v_ref[rows, :]
            eg = jnp.exp(gcc_c)
            g_last, is_last = _last_row(gcc_c)
            e_rev = jnp.exp(g_last - gcc_c)
            d_rhs_u, d_rhs_w = e["d_rhs"][:, :DN_DV], e["d_rhs"][:, DN_DV:]
            dqkv_ref[2, rows, :] = d_rhs_u * beta_c
            dbeta = jnp.sum(d_rhs_u * v_c, axis=1, keepdims=True)
            dkb = e["dkb"] + d_rhs_w * eg
            dgc = jnp.sum(d_rhs_w * e["kb"] * eg, axis=1, keepdims=True)
            m1 = e["d_l"] * e["lmat"]
            dgc = dgc + jnp.sum(m1, axis=1, keepdims=True)
            dgr = -jnp.sum(m1, axis=0, keepdims=True)
            m2 = e["d_intra"] * e["intra"]
            dgc = dgc + jnp.sum(m2, axis=1, keepdims=True)
            dgr = dgr - jnp.sum(m2, axis=0, keepdims=True)
            dqd = dqd_ref[rows, :]
            dq = e["dq"] + dqd * eg
            dgc = dgc + jnp.sum(dqd * q_c * eg, axis=1, keepdims=True)
            dkd = dkd_ref[rows, :]
            dk = e["dk"] + dkd * e_rev
            tk = jnp.sum(dkd * k_c * e_rev, axis=1, keepdims=True)
            dgc = dgc - tk
            d_last = dgl_ref[c][:, :1] + jnp.sum(tk, axis=0, keepdims=True)
            dgc = dgc + jnp.where(is_last, d_last, 0.0)
            dk = dk + dkb * beta_c
            dbeta = dbeta + jnp.sum(dkb * k_c, axis=1, keepdims=True)
            dqkv_ref[0, rows, :] = dq * DN_SCALE
            dqkv_ref[1, rows, :] = dk
            dbeta_ref[pl.ds(c, 1), :] = _to_row(dbeta)
            dgc_ref[pl.ds(c, 1), :] = _to_row(dgc) + dgr

        for c0 in range(0, DN_CB, DN_GROUP):
            chunks = range(c0, c0 + DN_GROUP)
            env = [first(c) for c in chunks]
            for c, e in zip(chunks, env):
                second(c, e)
            for e in env:
                third(e)
            for c, e in zip(chunks, env):
                last(c, e)

    sd = jax.ShapeDtypeStruct
    f32 = jnp.float32
    return pl.pallas_call(
        body, name="dn_bwd_chunks", grid=(DN_HEADS, n_chunks // DN_CB),
        out_shape=(sd(qkv.shape, f32), sd(beta.shape, f32), sd(gc.shape, f32)),
        in_specs=planes + [rowv, rowv, sq, tok, tok, tok, tok, tok, tok, sq, lane],
        out_specs=(all_planes, rowv, rowv),
        compiler_params=pltpu.CompilerParams(dimension_semantics=("parallel", "parallel")),
    )(qkv, qkv, qkv, beta, gc, t, u, w, du, dw, dqd, dkd, dintra, dgl)


@jax.custom_vjp
def _delta_rule_op(qkv, beta, gc):
    return _delta_rule_fwd(qkv, beta, gc)[0]


def _delta_rule_fwd(qkv, beta, gc):
    u, w, qd, kd, intra, t, cd = _dn_prep(qkv, beta, gc)
    out, vn, st = _dn_scan(u, w, qd, kd, intra, cd)
    return out, (qkv, beta, gc, u, w, qd, kd, intra, t, cd, vn, st)


def _delta_rule_bwd(res, do):
    qkv, beta, gc, u, w, qd, kd, intra, t, cd, vn, st = res
    du, dw, dqd, dkd, dintra, dgl = _dn_bwd_scan(do, w, qd, kd, intra, cd, vn, st)
    return _dn_bwd_chunks(qkv, beta, gc, t, u, w, du, dw, dqd, dkd, dintra, dgl)


_delta_rule_op.defvjp(_delta_rule_fwd, _delta_rule_bwd)


def _gated_delta_rule(qkv, g, beta):
    s, h = g.shape
    n_chunks = s // CHUNK
    gc = jnp.cumsum(g.T.reshape(h, n_chunks, CHUNK), axis=-1)
    return _delta_rule_op(qkv, beta.T.reshape(h, n_chunks, CHUNK), gc)


PRE_ROWS = 1024
HALO = 8
PRE_W = DN_QK_W


def _shift_rows(xs, k):
    return pltpu.roll(xs, k, 0)[HALO:]


def _conv_silu(x_ref, halo_ref, w_ref, first_block):
    halo = jnp.where(first_block, 0.0, halo_ref[...])
    xs = jnp.concatenate([halo, x_ref[...]], axis=0)
    taps = [_shift_rows(xs, CONV_WIDTH - 1 - j) for j in range(CONV_WIDTH - 1)] + [x_ref[...]]
    conv = sum(w_ref[pl.ds(j, 1), :] * taps[j] for j in range(CONV_WIDTH))
    return conv, jax.nn.sigmoid(conv), taps


def _pre_specs():
    blk = pl.BlockSpec((PRE_ROWS, PRE_W), lambda j, i: (i, j))
    prev = pl.BlockSpec((HALO, PRE_W), lambda j, i: (jnp.maximum(i * (PRE_ROWS // HALO) - 1, 0), j))
    wts = pl.BlockSpec((CONV_WIDTH, PRE_W), lambda j, i: (0, j))
    plane = pl.BlockSpec((None, PRE_ROWS, PRE_W), lambda j, i: (j, i, 0))
    return blk, prev, wts, plane


def _pre_fwd_call(x, conv_w):
    s = x.shape[0]
    blk, prev, wts, plane = _pre_specs()

    def body(x_ref, halo_ref, w_ref, o_ref):
        conv, sig, _ = _conv_silu(x_ref, halo_ref, w_ref, pl.program_id(1) == 0)
        act = conv * sig
        is_v = pl.program_id(0) == 2
        for h in range(DN_HEADS):
            cols = slice(h * DN_DK, (h + 1) * DN_DK)
            a_h = act[:, cols]
            r = lax.rsqrt(jnp.sum(a_h * a_h, axis=-1, keepdims=True) + NORM_EPS)
            o_ref[:, cols] = a_h * jnp.where(is_v, 1.0, r)

    return pl.pallas_call(
        body, name="pre_fwd", grid=(3, s // PRE_ROWS),
        out_shape=jax.ShapeDtypeStruct((3, s, PRE_W), jnp.float32),
        in_specs=[blk, prev, wts], out_specs=plane,
        compiler_params=pltpu.CompilerParams(dimension_semantics=("parallel", "parallel")),
    )(x, x, conv_w)


def _pre_bwd_act_call(x, conv_w, d_out):
    s = x.shape[0]
    blk, prev, wts, plane = _pre_specs()

    def body(x_ref, halo_ref, w_ref, do_ref, dc_ref):
        conv, sig, _ = _conv_silu(x_ref, halo_ref, w_ref, pl.program_id(1) == 0)
        act = conv * sig
        d_silu = sig * (1.0 + conv * (1.0 - sig))
        is_v = pl.program_id(0) == 2
        for h in range(DN_HEADS):
            cols = slice(h * DN_DK, (h + 1) * DN_DK)
            a_h, do_h = act[:, cols], do_ref[:, cols]
            r = lax.rsqrt(jnp.sum(a_h * a_h, axis=-1, keepdims=True) + NORM_EPS)
            n_h = a_h * r
            d_norm = r * (do_h - n_h * jnp.sum(do_h * n_h, axis=-1, keepdims=True))
            dc_ref[:, cols] = jnp.where(is_v, do_h, d_norm) * d_silu[:, cols]

    return pl.pallas_call(
        body, name="pre_bwd_act", grid=(3, s // PRE_ROWS),
        out_shape=jax.ShapeDtypeStruct(x.shape, jnp.float32),
        in_specs=[blk, prev, wts, plane], out_specs=blk,
        compiler_params=pltpu.CompilerParams(dimension_semantics=("parallel", "parallel")),
    )(x, x, conv_w, d_out)


def _pre_bwd_conv_call(x, conv_w, dc):
    s = x.shape[0]
    n_blocks = s // PRE_ROWS
    blk, prev, wts, _ = _pre_specs()
    nxt = pl.BlockSpec((HALO, PRE_W), lambda j, i: (jnp.minimum((i + 1) * (PRE_ROWS // HALO), s // HALO - 1), j))

    def body(x_ref, halo_ref, w_ref, dc_ref, dcn_ref, dx_ref, dw_ref):
        i = pl.program_id(1)

        @pl.when(i == 0)
        def _():
            dw_ref[...] = jnp.zeros_like(dw_ref)

        dcv = dc_ref[...]
        ahead = jnp.concatenate([dcv, jnp.where(i == n_blocks - 1, 0.0, dcn_ref[...])], axis=0)
        dx = w_ref[pl.ds(CONV_WIDTH - 1, 1), :] * dcv
        for j in range(CONV_WIDTH - 1):
            k = CONV_WIDTH - 1 - j
            dx = dx + w_ref[pl.ds(j, 1), :] * pltpu.roll(ahead, PRE_ROWS + HALO - k, 0)[:PRE_ROWS]
        dx_ref[...] = dx
        halo = jnp.where(i == 0, 0.0, halo_ref[...])
        xs = jnp.concatenate([halo, x_ref[...]], axis=0)
        for j in range(CONV_WIDTH):
            tap = x_ref[...] if j == CONV_WIDTH - 1 else _shift_rows(xs, CONV_WIDTH - 1 - j)
            dw_ref[pl.ds(j, 1), :] += jnp.sum(dcv * tap, axis=0, keepdims=True)

    sd = jax.ShapeDtypeStruct
    return pl.pallas_call(
        body, name="pre_bwd_conv", grid=(3, n_blocks),
        out_shape=(sd(x.shape, jnp.float32), sd(conv_w.shape, jnp.float32)),
        in_specs=[blk, prev, wts, blk, nxt], out_specs=(blk, wts),
        compiler_params=pltpu.CompilerParams(dimension_semantics=("parallel", "arbitrary")),
    )(x, x, conv_w, dc, dc)


@jax.custom_vjp
def _pre_op(x, conv_w):
    return _pre_fwd_call(x, conv_w)


def _pre_op_fwd(x, conv_w):
    return _pre_fwd_call(x, conv_w), (x, conv_w)


def _pre_op_bwd(res, d_out):
    x, conv_w = res
    return _pre_bwd_conv_call(x, conv_w, _pre_bwd_act_call(x, conv_w, d_out))


_pre_op.defvjp(_pre_op_fwd, _pre_op_bwd)


def _project(h, h_lo, w, slot):
    b, s, d = h.shape
    return _linear(h.reshape(b * s, d), h_lo.reshape(b * s, d), w, slot).reshape(b, s, w.shape[1])


def _rope_table(positions, dh):
    inv_freq = ROPE_THETA ** (-jnp.arange(0, dh, 2, dtype=jnp.float32) / dh)
    ang = positions.astype(jnp.float32)[:, None] * inv_freq
    reps = 128 // (dh // 2)
    return jnp.concatenate([jnp.tile(jnp.cos(ang), (1, reps)), jnp.tile(jnp.sin(ang), (1, reps))], axis=-1)


GATE_ROWS = 1024


def _gate_terms(o_h, z_h):
    r = lax.rsqrt(jnp.mean(o_h * o_h, axis=-1, keepdims=True) + NORM_EPS)
    sig = jax.nn.sigmoid(z_h)
    return r, o_h * r, sig, z_h * sig


def _gate_fwd_call(o, z, nw):
    tok = pl.BlockSpec((GATE_ROWS, DN_V_W), lambda i: (i, 0))
    vec = pl.BlockSpec((1, DN_DV), lambda i: (0, 0))

    def body(o_ref, z_ref, nw_ref, y_ref):
        for h in range(DN_HEADS):
            cols = pl.ds(h * DN_DV, DN_DV)
            _, n_h, _, g_h = _gate_terms(o_ref[:, cols], z_ref[:, cols])
            y_ref[:, cols] = n_h * nw_ref[...] * g_h

    return pl.pallas_call(
        body, name="gate_fwd", grid=(o.shape[0] // GATE_ROWS,),
        out_shape=jax.ShapeDtypeStruct(o.shape, jnp.float32), in_specs=[tok, tok, vec], out_specs=tok,
        compiler_params=pltpu.CompilerParams(dimension_semantics=("parallel",)),
    )(o, z, nw)


def _gate_bwd_call(o, z, nw, dy):
    tok = pl.BlockSpec((GATE_ROWS, DN_V_W), lambda i: (i, 0))
    vec = pl.BlockSpec((1, DN_DV), lambda i: (0, 0))

    def body(o_ref, z_ref, nw_ref, dy_ref, do_ref, dz_ref, dnw_ref):
        @pl.when(pl.program_id(0) == 0)
        def _():
            dnw_ref[...] = jnp.zeros_like(dnw_ref)

        for h in range(DN_HEADS):
            cols = pl.ds(h * DN_DV, DN_DV)
            z_h, dy_h = z_ref[:, cols], dy_ref[:, cols]
            r, n_h, sig, g_h = _gate_terms(o_ref[:, cols], z_h)
            dz_ref[:, cols] = dy_h * n_h * nw_ref[...] * (sig * (1.0 + z_h * (1.0 - sig)))
            dn = dy_h * nw_ref[...] * g_h
            do_ref[:, cols] = r * (dn - n_h * jnp.mean(dn * n_h, axis=-1, keepdims=True))
            dnw_ref[...] += jnp.sum(dy_h * n_h * g_h, axis=0, keepdims=True)

    sd = jax.ShapeDtypeStruct
    return pl.pallas_call(
        body, name="gate_bwd", grid=(o.shape[0] // GATE_ROWS,),
        out_shape=(sd(o.shape, jnp.float32), sd(o.shape, jnp.float32), sd(nw.shape, jnp.float32)),
        in_specs=[tok, tok, vec, tok], out_specs=(tok, tok, vec),
        compiler_params=pltpu.CompilerParams(dimension_semantics=("arbitrary",)),
    )(o, z, nw, dy)


@jax.custom_vjp
def _gate_op(o, z, nw):
    return _gate_fwd_call(o, z, nw)


def _gate_op_fwd(o, z, nw):
    return _gate_fwd_call(o, z, nw), (o, z, nw)


def _gate_op_bwd(res, dy):
    return _gate_bwd_call(*res, dy)


_gate_op.defvjp(_gate_op_fwd, _gate_op_bwd)


_MASKED = -1e30


def _swa_probs(qs, k_h, sinks, valid):
    ss = [jnp.where(valid, _dot(q_h, k_h, 1, 1) * (SWA_DH ** -0.5), _MASKED) for q_h in qs]
    ms = [jnp.maximum(jnp.max(s, axis=-1, keepdims=True), sink) for s, sink in zip(ss, sinks)]
    ps = [jnp.exp(s - m) for s, m in zip(ss, ms)]
    es = [jnp.exp(sink - m) for sink, m in zip(sinks, ms)]
    invs = [1.0 / (jnp.sum(p, axis=-1, keepdims=True) + e) for p, e in zip(ps, es)]
    return [p * inv for p, inv in zip(ps, invs)], [e * inv for e, inv in zip(es, invs)]


def _swa_valid(n):
    qi = lax.broadcasted_iota(jnp.int32, (WINDOW, 2 * WINDOW), 0)
    kj = lax.broadcasted_iota(jnp.int32, (WINDOW, 2 * WINDOW), 1)
    diff = qi + WINDOW - kj
    return (diff >= 0) & (diff < WINDOW) & ((kj >= WINDOW) | (n > 0))


def _rotate_half(x, transpose=False):
    half = SWA_DH // 2
    lower = lax.broadcasted_iota(jnp.int32, x.shape, 1) % SWA_DH < half
    ahead, behind = pltpu.roll(x, 128 - half, 1), pltpu.roll(x, half, 1)
    return jnp.where(lower, ahead, -behind) if transpose else jnp.where(lower, -ahead, behind)


def _rope(x, table):
    return x * table[:, :128] + _rotate_half(x) * table[:, 128:]


def _unrope(dy, table):
    return dy * table[:, :128] + _rotate_half(dy * table[:, 128:], transpose=True)


def _swa_specs():
    qs = pl.BlockSpec((WINDOW, SWA_Q_W), lambda n: (n, 0))
    first = lambda n: jnp.maximum(n - 1, 0)
    kv = [pl.BlockSpec((WINDOW, SWA_KV_W), lambda n: (first(n), 0)), pl.BlockSpec((WINDOW, SWA_KV_W), lambda n: (n, 0)),
          pl.BlockSpec((WINDOW, SWA_KV_W), lambda n: (first(n), 1)), pl.BlockSpec((WINDOW, SWA_KV_W), lambda n: (n, 1))]
    tables = [pl.BlockSpec((WINDOW, 256), lambda n: (first(n), 0)), pl.BlockSpec((WINDOW, 256), lambda n: (n, 0))]
    cur = pl.BlockSpec((WINDOW, SWA_KV_W), lambda n: (n, 0))
    sk = pl.BlockSpec((SWA_HEADS, 1, 128), lambda n: (0, 0, 0))
    return qs, kv, tables, cur, sk


def _swa_load(q_ref, kp_ref, kc_ref, vp_ref, vc_ref, tp_ref, tc_ref):
    table_kk = jnp.concatenate([tp_ref[...], tc_ref[...]], axis=0)
    kk = _rope(jnp.concatenate([kp_ref[...], kc_ref[...]], axis=0), table_kk)
    vv = jnp.concatenate([vp_ref[...], vc_ref[...]], axis=0)
    q_rot = []
    for b in range(SWA_Q_W // 128):
        pair = _rope(q_ref[:, pl.ds(b * 128, 128)], tc_ref[...])
        q_rot += [pair[:, :SWA_DH], pair[:, SWA_DH:]]
    split = lambda t: [t[:, hkv * SWA_DH:(hkv + 1) * SWA_DH] for hkv in range(SWA_KV_HEADS)]
    return q_rot, split(kk), split(vv), table_kk


def _swa_fwd_call(q, kv, table, sinks):
    qs, kvs, tables, _, sk = _swa_specs()

    def body(q_ref, kp_ref, kc_ref, vp_ref, vc_ref, tp_ref, tc_ref, sink_ref, o_ref):
        valid = _swa_valid(pl.program_id(0))
        q_rot, kk, vv, _ = _swa_load(q_ref, kp_ref, kc_ref, vp_ref, vc_ref, tp_ref, tc_ref)
        for hkv in range(SWA_KV_HEADS):
            heads = range(hkv * SWA_GROUP, (hkv + 1) * SWA_GROUP)
            probs, _ = _swa_probs([q_rot[h] for h in heads], kk[hkv], [sink_ref[h][:, :1] for h in heads], valid)
            outs = [_dot(p, vv[hkv]) for p in probs]
            for h, o in zip(heads, outs):
                o_ref[:, pl.ds(h * SWA_DH, SWA_DH)] = o

    return pl.pallas_call(
        body, name="swa_fwd", grid=(q.shape[0] // WINDOW,),
        out_shape=jax.ShapeDtypeStruct(q.shape, jnp.float32),
        in_specs=[qs] + kvs + tables + [sk], out_specs=qs,
        compiler_params=pltpu.CompilerParams(dimension_semantics=("parallel",)),
    )(q, kv, kv, kv, kv, table, table, sinks)


def _swa_bwd_call(q, kv, table, sinks, do):
    qs, kvs, tables, cur, sk = _swa_specs()

    def body(q_ref, kp_ref, kc_ref, vp_ref, vc_ref, tp_ref, tc_ref, sink_ref, do_ref,
             dq_ref, dkc_ref, dkp_ref, dvc_ref, dvp_ref, ds_ref):
        @pl.when(pl.program_id(0) == 0)
        def _():
            ds_ref[...] = jnp.zeros_like(ds_ref)

        valid = _swa_valid(pl.program_id(0))
        q_rot, kk, vv, table_kk = _swa_load(q_ref, kp_ref, kc_ref, vp_ref, vc_ref, tp_ref, tc_ref)
        lane0 = lax.broadcasted_iota(jnp.int32, (1, 128), 1) == 0
        dq_heads, dk_heads, dv_heads = [], [], []
        for hkv in range(SWA_KV_HEADS):
            k_h, v_h = kk[hkv], vv[hkv]
            heads = range(hkv * SWA_GROUP, (hkv + 1) * SWA_GROUP)
            q_hs = [q_rot[h] for h in heads]
            dos = [do_ref[:, pl.ds(h * SWA_DH, SWA_DH)] for h in heads]
            probs, p_sinks = _swa_probs(q_hs, k_h, [sink_ref[h][:, :1] for h in heads], valid)
            dps = [_dot(do_h, v_h, 1, 1) for do_h in dos]
            rss = [jnp.sum(p * dp, axis=-1, keepdims=True) for p, dp in zip(probs, dps)]
            d_ss = [p * (dp - rs) for p, dp, rs in zip(probs, dps, rss)]
            dq_heads += [_dot(d_s, k_h) * (SWA_DH ** -0.5) for d_s in d_ss]
            dks = [_dot(d_s, q_h, 0, 0) for d_s, q_h in zip(d_ss, q_hs)]
            dvs = [_dot(p, do_h, 0, 0) for p, do_h in zip(probs, dos)]
            for h, p_sink, rs in zip(heads, p_sinks, rss):
                d_sink = -jnp.sum(p_sink * rs, axis=0, keepdims=True)
                ds_ref[h] += jnp.where(lane0, d_sink, 0.0)
            dk_heads.append(sum(dks[1:], dks[0]) * (SWA_DH ** -0.5))
            dv_heads.append(sum(dvs[1:], dvs[0]))
        for b in range(SWA_Q_W // 128):
            pair = jnp.concatenate([dq_heads[2 * b], dq_heads[2 * b + 1]], axis=1)
            dq_ref[:, pl.ds(b * 128, 128)] = _unrope(pair, tc_ref[...])
        dk = _unrope(jnp.concatenate(dk_heads, axis=1), table_kk)
        dv = jnp.concatenate(dv_heads, axis=1)
        dkp_ref[...] = dk[:WINDOW]
        dkc_ref[...] = dk[WINDOW:]
        dvp_ref[...] = dv[:WINDOW]
        dvc_ref[...] = dv[WINDOW:]

    sd = jax.ShapeDtypeStruct
    f32 = jnp.float32
    half = (q.shape[0], SWA_KV_W)
    return pl.pallas_call(
        body, name="swa_bwd", grid=(q.shape[0] // WINDOW,),
        out_shape=(sd(q.shape, f32), sd(half, f32), sd(half, f32), sd(half, f32), sd(half, f32), sd(sinks.shape, f32)),
        in_specs=[qs] + kvs + tables + [sk, qs], out_specs=(qs, cur, cur, cur, cur, sk),
        compiler_params=pltpu.CompilerParams(dimension_semantics=("arbitrary",)),
    )(q, kv, kv, kv, kv, table, table, sinks, do)


@jax.custom_vjp
def _swa_op(q, kv, table, sinks):
    return _swa_fwd_call(q, kv, table, sinks)


def _swa_op_fwd(q, kv, table, sinks):
    return _swa_fwd_call(q, kv, table, sinks), (q, kv, table, sinks)


def _swa_op_bwd(res, do):
    q, kv, table, sinks = res
    dq, dkc, dkp, dvc, dvp, dsinks = _swa_bwd_call(q, kv, table, sinks, do)

    def fold(cur, prev):
        return cur + jnp.concatenate([prev[WINDOW:], jnp.zeros_like(prev[:WINDOW])], axis=0)

    return dq, jnp.concatenate([fold(dkc, dkp), fold(dvc, dvp)], axis=1), jnp.zeros_like(table), dsinks


_swa_op.defvjp(_swa_op_fwd, _swa_op_bwd)


def _swa_sink_attention(q, kv, table, sinks):
    return _swa_op(q, kv, table, jnp.broadcast_to(sinks[:, None, None], (SWA_HEADS, 1, 128)))


MEM_ROWS = 1024


def _mem_probs(q_h, k_h):
    s = _dot(q_h, k_h, 1, 1) * (MEM_DH ** -0.5)
    p = jnp.exp(s - jnp.max(s, axis=-1, keepdims=True))
    return p / jnp.sum(p, axis=-1, keepdims=True)


def _mem_fwd_call(qm, kv):
    qs = pl.BlockSpec((MEM_ROWS, MEM_W), lambda i: (i, 0))
    kvs = pl.BlockSpec(kv.shape, lambda i: (0, 0))

    def body(q_ref, kv_ref, o_ref):
        for h in range(MEM_HEADS):
            cols = pl.ds(h * MEM_DH, MEM_DH)
            probs = _mem_probs(q_ref[:, cols], kv_ref[:, cols])
            o_ref[:, cols] = _dot(probs, kv_ref[:, pl.ds(MEM_W + h * MEM_DH, MEM_DH)])

    return pl.pallas_call(
        body, name="mem_fwd", grid=(qm.shape[0] // MEM_ROWS,),
        out_shape=jax.ShapeDtypeStruct(qm.shape, jnp.float32), in_specs=[qs, kvs], out_specs=qs,
        compiler_params=pltpu.CompilerParams(dimension_semantics=("parallel",)),
    )(qm, kv)


def _mem_bwd_call(qm, kv, do):
    qs = pl.BlockSpec((MEM_ROWS, MEM_W), lambda i: (i, 0))
    kvs = pl.BlockSpec(kv.shape, lambda i: (0, 0))

    def body(q_ref, kv_ref, do_ref, dq_ref, dkv_ref):
        @pl.when(pl.program_id(0) == 0)
        def _():
            dkv_ref[...] = jnp.zeros_like(dkv_ref)

        for h in range(MEM_HEADS):
            cols = pl.ds(h * MEM_DH, MEM_DH)
            v_cols = pl.ds(MEM_W + h * MEM_DH, MEM_DH)
            q_h, k_h, do_h = q_ref[:, cols], kv_ref[:, cols], do_ref[:, cols]
            probs = _mem_probs(q_h, k_h)
            dp = _dot(do_h, kv_ref[:, v_cols], 1, 1)
            d_s = probs * (dp - jnp.sum(probs * dp, axis=-1, keepdims=True))
            dq_ref[:, cols] = _dot(d_s, k_h) * (MEM_DH ** -0.5)
            dkv_ref[:, cols] += _dot(d_s, q_h, 0, 0) * (MEM_DH ** -0.5)
            dkv_ref[:, v_cols] += _dot(probs, do_h, 0, 0)

    sd = jax.ShapeDtypeStruct
    return pl.pallas_call(
        body, name="mem_bwd", grid=(qm.shape[0] // MEM_ROWS,),
        out_shape=(sd(qm.shape, jnp.float32), sd(kv.shape, jnp.float32)),
        in_specs=[qs, kvs, qs], out_specs=(qs, kvs),
        compiler_params=pltpu.CompilerParams(dimension_semantics=("arbitrary",)),
    )(qm, kv, do)


@jax.custom_vjp
def _mem_op(qm, kv):
    return _mem_fwd_call(qm, kv)


def _mem_op_fwd(qm, kv):
    return _mem_fwd_call(qm, kv), (qm, kv)


def _mem_op_bwd(res, do):
    return _mem_bwd_call(*res, do)


_mem_op.defvjp(_mem_op_fwd, _mem_op_bwd)


def _memory_attention(qm, kv):
    return _mem_op(qm[0], kv[0])[None]


def _mixer_a(h, h_lo, mem, mem_lo, p, s, layer):
    B, S, _ = h.shape
    proj = _project(h, h_lo, p["a_w_in"][layer], s["a_w_in"][layer])
    c1 = 2 * DN_QK_W + DN_V_W
    qkv = proj[..., :c1]
    z = proj[..., c1:QKVZ_W]
    qm = proj[..., QKVZ_W:QKVZ_W + MEM_W]
    a = proj[..., QKVZ_W + MEM_W:QKVZ_W + MEM_W + DN_HEADS]
    b = proj[..., QKVZ_W + MEM_W + DN_HEADS:QKVZ_W + MEM_W + 2 * DN_HEADS]
    planes = _pre_op(qkv[0], p["a_conv_w"][layer])
    beta = jax.nn.sigmoid(b[0])
    g = -jnp.exp(p["a_A_log"][layer]) * jax.nn.softplus(a[0] + p["a_dt_bias"][layer])
    o = _gate_op(_gated_delta_rule(planes, g, beta), z[0], p["a_norm_w"][layer][None])[None]
    kv = _project(mem, mem_lo, p["mem_w_kv"][layer], s["mem_w_kv"][layer])
    mo = _memory_attention(qm, kv)
    cat = jnp.concatenate([o, mo], axis=-1)
    return _project(cat, _lo(cat), p["w_o"][layer], s["w_o"][layer])


def _mixer_b(h, h_lo, mem, mem_lo, kv_shared, table, p, s, layer):
    j = layer - N_A
    proj = _project(h, h_lo, p["b_w_in"][j], s["b_w_in"][j])
    o = _swa_sink_attention(proj[0, :, :SWA_Q_W], kv_shared, table, p["b_sinks"][j])[None]
    kv = _project(mem, mem_lo, p["mem_w_kv"][layer], s["mem_w_kv"][layer])
    mo = _memory_attention(proj[..., SWA_Q_W:], kv)
    cat = jnp.concatenate([o, mo], axis=-1)
    return _project(cat, _lo(cat), p["w_o"][layer], s["w_o"][layer])


def _forward(p, s, x, mem, positions):
    table = _rope_table(positions[0], SWA_DH)
    h, h_lo, mem_lo = x, _lo(x), _lo(mem)
    kv_shared = None
    for layer in range(DEPTH):
        if layer < N_A:
            mix = _mixer_a(h, h_lo, mem, mem_lo, p, s, layer)
        else:
            mix = _mixer_b(h, h_lo, mem, mem_lo, kv_shared, table, p, s, layer)
        seq = h.shape[1]
        h2, h2_lo = _ln_res(h[0], mix[0], p["ln_g"][layer, 0][None], p["ln_b"][layer, 0][None])
        down = _mlp(h2, h2_lo, p["mlp_w_up"][layer], p["mlp_w_down"][layer], s["mlp_w_up"][layer],
                    s["mlp_w_down"][layer])
        h, h_lo = _ln_res(h2, down, p["ln_g"][layer, 1][None], p["ln_b"][layer, 1][None])
        h, h_lo = h.reshape(1, seq, D_MODEL), h_lo.reshape(1, seq, D_MODEL)
        if layer == N_A - 1:
            kv_shared = _project(h, h_lo, p["w_kv_shared"], s["w_kv_shared"])[0]
    return h


def _loss(diff, s, p, mem, positions, target):
    y = _forward({**p, **diff["small"]}, s, diff["x"], mem, positions)
    return 0.5 * jnp.sum(jnp.mean(jnp.square(y - target), axis=-1))


def _reorder_a_w_in(w):
    pad = jnp.zeros(w.shape[:-1] + (A_IN_PAD - A_IN,), w.dtype)
    return jnp.concatenate([w[..., :QKVZ_W], w[..., QKVZ_W + 2 * DN_HEADS:], w[..., QKVZ_W:QKVZ_W + 2 * DN_HEADS], pad],
                           axis=-1)


def _restore_a_w_in(w):
    return jnp.concatenate([w[..., :QKVZ_W], w[..., QKVZ_W + MEM_W:QKVZ_W + MEM_W + 2 * DN_HEADS],
                            w[..., QKVZ_W:QKVZ_W + MEM_W]], axis=-1)


def kernel(x, mem, positions, a_w_in, a_conv_w, a_A_log, a_dt_bias, a_norm_w, b_w_in, b_sinks, w_kv_shared, mem_w_kv, w_o, mlp_w_up, mlp_w_down, ln_g, ln_b, loss_target, m_a_w_in, m_a_conv_w, m_a_A_log, m_a_dt_bias, m_a_norm_w, m_b_w_in, m_b_sinks, m_w_kv_shared, m_mem_w_kv, m_w_o, m_mlp_w_up, m_mlp_w_down, m_ln_g, m_ln_b, v_a_w_in, v_a_conv_w, v_a_A_log, v_a_dt_bias, v_a_norm_w, v_b_w_in, v_b_sinks, v_w_kv_shared, v_mem_w_kv, v_w_o, v_mlp_w_up, v_mlp_w_down, v_ln_g, v_ln_b):
    w_sh = dict(a_w_in=a_w_in, a_conv_w=a_conv_w, a_A_log=a_A_log, a_dt_bias=a_dt_bias, a_norm_w=a_norm_w,
                b_w_in=b_w_in, b_sinks=b_sinks, w_kv_shared=w_kv_shared, mem_w_kv=mem_w_kv, w_o=w_o,
                mlp_w_up=mlp_w_up, mlp_w_down=mlp_w_down, ln_g=ln_g, ln_b=ln_b)
    m_sh = dict(a_w_in=m_a_w_in, a_conv_w=m_a_conv_w, a_A_log=m_a_A_log, a_dt_bias=m_a_dt_bias, a_norm_w=m_a_norm_w,
                b_w_in=m_b_w_in, b_sinks=m_b_sinks, w_kv_shared=m_w_kv_shared, mem_w_kv=m_mem_w_kv, w_o=m_w_o,
                mlp_w_up=m_mlp_w_up, mlp_w_down=m_mlp_w_down, ln_g=m_ln_g, ln_b=m_ln_b)
    v_sh = dict(a_w_in=v_a_w_in, a_conv_w=v_a_conv_w, a_A_log=v_a_A_log, a_dt_bias=v_a_dt_bias, a_norm_w=v_a_norm_w,
                b_w_in=v_b_w_in, b_sinks=v_b_sinks, w_kv_shared=v_w_kv_shared, mem_w_kv=v_mem_w_kv, w_o=v_w_o,
                mlp_w_up=v_mlp_w_up, mlp_w_down=v_mlp_w_down, ln_g=v_ln_g, ln_b=v_ln_b)
    shard_shapes = {n: w_sh[n].shape for n in WEIGHTS}
    rb, rows = _rows_for(w_sh)

    big, small = _pack(w_sh, rb, jnp.bfloat16)
    gbig, gsmall = _gather_weights(big.reshape(2, rb // 2, FLAT_W), small.reshape(2, SMALL_ROWS // 2, FLAT_W))
    gbig, gsmall = gbig.reshape(N_CHIPS, rb, FLAT_W), gsmall.reshape(N_CHIPS, SMALL_ROWS, FLAT_W)
    pieces = [_unpack(gbig[q], gsmall[q], shard_shapes) for q in range(N_CHIPS)]
    full = {n: jnp.concatenate([pieces[q][n] for q in range(N_CHIPS)], axis=SHARD_AXIS[n]) for n in SHARD_AXIS}
    for n in REPLICATED:
        full[n] = w_sh[n]
    big_w = {n: full[n] for n in BIG}
    big_w["a_w_in"] = _reorder_a_w_in(big_w["a_w_in"])
    small_w = {n: full[n] for n in SMALL}
    slots = {n: jnp.zeros(big_w[n].shape, jnp.float32) for n in BIG}

    loss, (grads, g_slots) = jax.value_and_grad(_loss, argnums=(0, 1))(
        {"x": x, "small": small_w}, slots, big_w, mem, positions, loss_target)
    loss = lax.psum(loss, ("x", "y", "c"))
    g_full = {**g_slots, **grads["small"]}
    g_full["a_w_in"] = _restore_a_w_in(g_full["a_w_in"])

    def shard_of(n, q):
        if n in REPLICATED:
            return g_full[n]
        size = shard_shapes[n][SHARD_AXIS[n]]
        return lax.slice_in_dim(g_full[n], q * size, (q + 1) * size, axis=SHARD_AXIS[n])

    parts = []
    for q in range(N_CHIPS):
        pb, ps = _pack({n: shard_of(n, q) for n in WEIGHTS}, rb, jnp.bfloat16)
        parts.append(jnp.concatenate([pb, ps.astype(jnp.bfloat16)], axis=0).reshape(2, rows // 2, FLAT_W))
    partials = jnp.stack(parts, axis=1)
    half = lax.axis_index("c").astype(jnp.int32).reshape(1)
    chip_partials = _add_pairs(partials, _swap_halves(partials), half)
    g_flat = _join_halves(_sum_chips(_scatter_grads(chip_partials))).reshape(rows, FLAT_W)

    odd = "a_w_in"
    blank = jnp.zeros(shard_shapes[odd], jnp.float32)
    flat = [jnp.concatenate(_pack({**d, odd: blank}, rb), axis=0) for d in (w_sh, m_sh, v_sh)]
    outs = (g_flat,) + tuple(_adamw(g_flat, *flat))
    g_o, d_o, m_o, v_o = [_unpack(o[:rb], o[rb:], shard_shapes) for o in outs]
    as_rows = lambda t: t.reshape(-1, t.shape[-1])
    updated = _adamw(as_rows(g_o[odd]), as_rows(w_sh[odd]), as_rows(m_sh[odd]), as_rows(v_sh[odd]))
    d_o[odd], m_o[odd], v_o[odd] = [t.reshape(shard_shapes[odd]) for t in updated]
    return (loss, grads["x"], *[g_o[n] for n in WEIGHTS], *[d_o[n] for n in WEIGHTS],
            *[m_o[n] for n in WEIGHTS], *[v_o[n] for n in WEIGHTS])
```

```python
import functools
import math

import jax
import jax.numpy as jnp
from jax import lax
from jax.experimental import pallas as pl
from jax.experimental.pallas import tpu as pltpu

D_MODEL = 1024
DEPTH = 4
N_A = DEPTH // 2
MEM_HEADS = 4
MEM_DH = D_MODEL // 16
MEM_W = MEM_HEADS * MEM_DH
DN_DK = 128
DN_DV = 128
DN_HEADS = (3 * D_MODEL) // (4 * DN_DV)
DN_QK_W = DN_HEADS * DN_DK
DN_V_W = DN_HEADS * DN_DV
CONV_WIDTH = 4
CHUNK = 64
SWA_DH = 64
SWA_HEADS = (3 * D_MODEL) // (4 * SWA_DH)
SWA_KV_HEADS = 2
SWA_GROUP = SWA_HEADS // SWA_KV_HEADS
SWA_Q_W = SWA_HEADS * SWA_DH
SWA_KV_W = SWA_KV_HEADS * SWA_DH
WINDOW = 128
ROPE_THETA = 10000.0
LN_EPS = 1e-5
NORM_EPS = 1e-6
DN_ALPHA = (2.0 * DEPTH) ** 0.25
A_IN = 2 * DN_QK_W + 2 * DN_V_W + 2 * DN_HEADS + MEM_W
A_IN_PAD = 3456
QKVZ_W = 2 * DN_QK_W + 2 * DN_V_W

ADAM_LR = 0.001
ADAM_B1 = 0.9
ADAM_B2 = 0.999
ADAM_EPS = 1e-08
ADAM_WD = 0.01
ADAM_STEP = 10

N_CHIPS = 4
FLAT_W = 1024
BIG = ("a_w_in", "b_w_in", "w_kv_shared", "mem_w_kv", "w_o", "mlp_w_up", "mlp_w_down")
SMALL = ("a_conv_w", "ln_g", "ln_b", "a_A_log", "a_dt_bias", "a_norm_w", "b_sinks")
REPLICATED = ("a_A_log", "a_dt_bias", "a_norm_w", "b_sinks")
WEIGHTS = ("a_w_in", "a_conv_w", "a_A_log", "a_dt_bias", "a_norm_w", "b_w_in", "b_sinks", "w_kv_shared",
           "mem_w_kv", "w_o", "mlp_w_up", "mlp_w_down", "ln_g", "ln_b")
SHARD_AXIS = {"a_w_in": 2, "a_conv_w": 2, "b_w_in": 1, "w_kv_shared": 0, "mem_w_kv": 1, "w_o": 1,
              "mlp_w_up": 2, "mlp_w_down": 1, "ln_g": 2, "ln_b": 2}
SMALL_ROWS = 32
ROW_ALIGN = 256

MESH = pl.DeviceIdType.MESH
HBM_SPEC = pl.BlockSpec(memory_space=pltpu.HBM)
VMEM_LIMIT = 48 * 1024 * 1024


def _rows_for(shards):
    n_big = sum(math.prod(shards[n].shape) for n in BIG)
    n_small = sum(math.prod(shards[n].shape) for n in SMALL)
    assert n_small <= SMALL_ROWS * FLAT_W
    total = -(-n_big // FLAT_W) + SMALL_ROWS
    total = -(-total // (2 * ROW_ALIGN)) * (2 * ROW_ALIGN)
    return total - SMALL_ROWS, total


def _pack(shards, rb, dtype_big=jnp.float32):
    big = jnp.concatenate([shards[n].reshape(-1).astype(dtype_big) for n in BIG])
    big = jnp.pad(big, (0, rb * FLAT_W - big.shape[0])).reshape(rb, FLAT_W)
    small = jnp.concatenate([shards[n].reshape(-1).astype(jnp.float32) for n in SMALL])
    small = jnp.pad(small, (0, SMALL_ROWS * FLAT_W - small.shape[0])).reshape(SMALL_ROWS, FLAT_W)
    return big, small


def _unpack(big, small, shapes):
    out = {}
    for flat, names in ((big.reshape(-1), BIG), (small.reshape(-1), SMALL)):
        off = 0
        for n in names:
            size = math.prod(shapes[n])
            out[n] = flat[off:off + size].reshape(shapes[n])
            off += size
    return out


def _other_chips(x, y):
    return [(1 - x, y), (x, 1 - y), (1 - x, 1 - y)]


def _gather_weights(big, small):
    def body(big_ref, small_ref, init_big_ref, init_small_ref, obig_ref, osmall_ref,
             send_sems, recv_sems, pass_send_sems, pass_recv_sems):
        del init_big_ref, init_small_ref
        x, y, c = lax.axis_index("x"), lax.axis_index("y"), lax.axis_index("c")
        me = 2 * x + y
        sibling = (x, y, 1 - c)
        pairs = ((big_ref, obig_ref), (small_ref, osmall_ref))
        sends = []
        for j, (px, py) in enumerate(_other_chips(x, y)):
            for i, (src, dst) in enumerate(pairs):
                sends.append(pltpu.make_async_remote_copy(
                    src_ref=src.at[c], dst_ref=dst.at[me, c], send_sem=send_sems.at[2 * j + i],
                    recv_sem=recv_sems.at[2 * j + i], device_id=(px, py, c), device_id_type=MESH))
        for cp in sends:
            cp.start()
        passed = []
        for j, (px, py) in enumerate(_other_chips(x, y)):
            for i, (src, dst) in enumerate(pairs):
                landed = dst.at[2 * px + py, c]
                pltpu.make_async_remote_copy(
                    src_ref=src.at[c], dst_ref=landed, send_sem=send_sems.at[2 * j + i],
                    recv_sem=recv_sems.at[2 * j + i], device_id=(px, py, c), device_id_type=MESH).wait_recv()
                passed.append(pltpu.make_async_remote_copy(
                    src_ref=landed, dst_ref=landed, send_sem=pass_send_sems.at[2 * j + i],
                    recv_sem=pass_recv_sems.at[2 * j + i], device_id=sibling, device_id_type=MESH))
                passed[-1].start()
        for j, (px, py) in enumerate(_other_chips(x, y)):
            for i, (src, dst) in enumerate(pairs):
                other_half = dst.at[2 * px + py, 1 - c]
                pltpu.make_async_remote_copy(
                    src_ref=other_half, dst_ref=other_half, send_sem=pass_send_sems.at[2 * j + i],
                    recv_sem=pass_recv_sems.at[2 * j + i], device_id=sibling, device_id_type=MESH).wait_recv()
        for cp in sends + passed:
            cp.wait_send()

    dma6 = pltpu.SemaphoreType.DMA((6,))
    four = lambda t: jnp.broadcast_to(t[None], (N_CHIPS,) + t.shape)
    return pl.pallas_call(
        body, name="gather_weights",
        out_shape=(jax.ShapeDtypeStruct((N_CHIPS,) + big.shape, big.dtype),
                   jax.ShapeDtypeStruct((N_CHIPS,) + small.shape, small.dtype)),
        in_specs=[HBM_SPEC] * 4, out_specs=(HBM_SPEC, HBM_SPEC), input_output_aliases={2: 0, 3: 1},
        scratch_shapes=[dma6, dma6, dma6, dma6],
    )(big, small, four(big), four(small))


def _scatter_grads(g):
    def body(g_ref, o_ref, send_sems, recv_sems, local_sem):
        x, y, c = lax.axis_index("x"), lax.axis_index("y"), lax.axis_index("c")
        me = 2 * x + y
        local = pltpu.make_async_copy(g_ref.at[me], o_ref.at[me], local_sem)
        local.start()
        sends = []
        for j, (px, py) in enumerate(_other_chips(x, y)):
            sends.append(pltpu.make_async_remote_copy(
                src_ref=g_ref.at[2 * px + py], dst_ref=o_ref.at[me], send_sem=send_sems.at[j], recv_sem=recv_sems.at[j],
                device_id=(px, py, c), device_id_type=MESH))
        for cp in sends:
            cp.start()
        for j, (px, py) in enumerate(_other_chips(x, y)):
            pltpu.make_async_remote_copy(
                src_ref=g_ref.at[me], dst_ref=o_ref.at[2 * px + py], send_sem=send_sems.at[j], recv_sem=recv_sems.at[j],
                device_id=(px, py, c), device_id_type=MESH).wait_recv()
        for cp in sends:
            cp.wait_send()
        local.wait()

    return pl.pallas_call(
        body, name="scatter_grads",
        out_shape=jax.ShapeDtypeStruct(g.shape, g.dtype),
        in_specs=[HBM_SPEC], out_specs=HBM_SPEC,
        scratch_shapes=[pltpu.SemaphoreType.DMA((3,)), pltpu.SemaphoreType.DMA((3,)), pltpu.SemaphoreType.DMA],
    )(g)


def _swap_halves(g):
    def body(g_ref, o_ref, send_sem, recv_sem):
        x, y, c = lax.axis_index("x"), lax.axis_index("y"), lax.axis_index("c")
        cp = pltpu.make_async_remote_copy(src_ref=g_ref.at[1 - c], dst_ref=o_ref, send_sem=send_sem, recv_sem=recv_sem,
                                          device_id=(x, y, 1 - c), device_id_type=MESH)
        cp.start()
        cp.wait()

    return pl.pallas_call(
        body, name="swap_halves",
        out_shape=jax.ShapeDtypeStruct(g.shape[1:], g.dtype),
        in_specs=[HBM_SPEC], out_specs=HBM_SPEC,
        scratch_shapes=[pltpu.SemaphoreType.DMA, pltpu.SemaphoreType.DMA],
    )(g)


def _join_halves(v):
    def body(v_ref, init_ref, o_ref, send_sem, recv_sem):
        del init_ref
        x, y, c = lax.axis_index("x"), lax.axis_index("y"), lax.axis_index("c")
        cp = pltpu.make_async_remote_copy(src_ref=v_ref, dst_ref=o_ref.at[c], send_sem=send_sem, recv_sem=recv_sem,
                                          device_id=(x, y, 1 - c), device_id_type=MESH)
        cp.start()
        cp.wait_send()
        pltpu.make_async_remote_copy(src_ref=v_ref, dst_ref=o_ref.at[1 - c], send_sem=send_sem, recv_sem=recv_sem,
                                     device_id=(x, y, 1 - c), device_id_type=MESH).wait_recv()

    return pl.pallas_call(
        body, name="join_halves",
        out_shape=jax.ShapeDtypeStruct((2,) + v.shape, v.dtype),
        in_specs=[HBM_SPEC, HBM_SPEC], out_specs=HBM_SPEC, input_output_aliases={1: 0},
        scratch_shapes=[pltpu.SemaphoreType.DMA, pltpu.SemaphoreType.DMA],
    )(v, jnp.stack([v, v]))


def _add_pairs(g, theirs, half):
    _, n, rows, width = g.shape
    assert rows % ROW_ALIGN == 0, rows

    def body(half_ref, g_ref, t_ref, o_ref):
        o_ref[...] = (g_ref[...].astype(jnp.float32) + t_ref[...].astype(jnp.float32)).astype(o_ref.dtype)

    blk = pl.BlockSpec((None, ROW_ALIGN, width), lambda p, i, h: (p, i, 0))
    grid_spec = pltpu.PrefetchScalarGridSpec(
        num_scalar_prefetch=1, grid=(n, rows // ROW_ALIGN),
        in_specs=[pl.BlockSpec((None, None, ROW_ALIGN, width), lambda p, i, h: (h[0], p, i, 0)), blk], out_specs=blk)
    return pl.pallas_call(
        body, name="add_pairs", grid_spec=grid_spec, out_shape=jax.ShapeDtypeStruct(theirs.shape, g.dtype),
        compiler_params=pltpu.CompilerParams(dimension_semantics=("parallel", "parallel")),
    )(half, g, theirs)


def _sum_chips(parts):
    n, rows, width = parts.shape
    assert rows % ROW_ALIGN == 0, rows

    def body(p_ref, o_ref):
        p = [p_ref[q].astype(jnp.float32) for q in range(n)]
        o_ref[...] = (p[0] + p[1]) + (p[2] + p[3])

    return pl.pallas_call(
        body, name="sum_chips", grid=(rows // ROW_ALIGN,),
        out_shape=jax.ShapeDtypeStruct((rows, width), jnp.float32),
        in_specs=[pl.BlockSpec((n, ROW_ALIGN, width), lambda i: (0, i, 0))],
        out_specs=pl.BlockSpec((ROW_ALIGN, width), lambda i: (i, 0)),
        compiler_params=pltpu.CompilerParams(dimension_semantics=("parallel",), vmem_limit_bytes=VMEM_LIMIT),
    )(parts)


def _adamw(g, w, m, v):
    rows, width = w.shape
    blk = ROW_ALIGN // 2

    def body(g_ref, w_ref, m_ref, v_ref, d_out, m_out, v_out):
        g = g_ref[...]
        m_new = ADAM_B1 * m_ref[...] + (1.0 - ADAM_B1) * g
        v_new = ADAM_B2 * v_ref[...] + (1.0 - ADAM_B2) * jnp.square(g)
        m_hat = m_new / (1.0 - ADAM_B1 ** ADAM_STEP)
        v_hat = v_new / (1.0 - ADAM_B2 ** ADAM_STEP)
        d_out[...] = -ADAM_LR * (m_hat / (jnp.sqrt(v_hat) + ADAM_EPS) + ADAM_WD * w_ref[...])
        m_out[...] = m_new
        v_out[...] = v_new

    spec = pl.BlockSpec((blk, width), lambda i: (i, 0))
    shape = jax.ShapeDtypeStruct((rows, width), jnp.float32)
    return pl.pallas_call(
        body, name="adamw", grid=(rows // blk,),
        out_shape=(shape,) * 3, in_specs=[spec] * 4, out_specs=(spec,) * 3,
        compiler_params=pltpu.CompilerParams(dimension_semantics=("parallel",), vmem_limit_bytes=VMEM_LIMIT),
    )(g, w, m, v)


def _tile(dim, pref):
    if dim <= pref:
        return dim
    for t in range(pref - pref % 128, 0, -128):
        if dim % t == 0:
            return t
    raise ValueError(f"no 128-aligned tile for {dim}")


def _matmul(a, b, *, ta=False, tb=False, name, epilogue=None, extra=None, out_dtype=jnp.float32):
    (k_a, m) = a.shape if ta else a.shape[::-1]
    (k_b, n) = b.shape[::-1] if tb else b.shape
    assert k_a == k_b, (a.shape, b.shape, ta, tb)
    k = k_a
    tk = _tile(k, 1152)
    nk = k // tk
    if ta:
        tm, tn = _tile(m, 1024), _tile(n, 2048 if m <= 1024 else 1024)
    else:
        tm, tn = _tile(m, 2048), _tile(n, 1152)
    a_spec = pl.BlockSpec((tk, tm), lambda i, j, l: (l, i)) if ta else pl.BlockSpec((tm, tk), lambda i, j, l: (i, l))
    b_spec = pl.BlockSpec((tn, tk), lambda i, j, l: (j, l)) if tb else pl.BlockSpec((tk, tn), lambda i, j, l: (l, j))
    o_spec = pl.BlockSpec((tm, tn), lambda i, j, l: (i, j))
    dims = (((0 if ta else 1,), (1 if tb else 0,)), ((), ()))
    has_extra = epilogue == "relu2_grad"
    assert has_extra == (extra is not None)

    def body(*refs):
        a_ref, b_ref = refs[:2]
        outs = refs[2 + has_extra:2 + has_extra + (2 if epilogue == "relu2" else 1)]
        l = pl.program_id(2)
        part = lax.dot_general(a_ref[...].astype(jnp.bfloat16), b_ref[...].astype(jnp.bfloat16), dims,
                               preferred_element_type=jnp.float32)

        def finish(acc):
            if epilogue is None:
                outs[0][...] = acc.astype(out_dtype)
            elif epilogue == "relu2":
                outs[0][...] = acc.astype(jnp.bfloat16)
                outs[1][...] = jnp.square(jnp.maximum(acc, 0.0)).astype(jnp.bfloat16)
            else:
                outs[0][...] = (acc * (2.0 * jnp.maximum(refs[2][...].astype(jnp.float32), 0.0))).astype(out_dtype)

        if nk == 1:
            finish(part)
            return
        acc_ref = refs[-1]

        @pl.when(l == 0)
        def _():
            acc_ref[...] = part

        @pl.when((l > 0) & (l < nk - 1))
        def _():
            acc_ref[...] += part

        @pl.when(l == nk - 1)
        def _():
            finish(acc_ref[...] + part)

    if epilogue == "relu2":
        out_shape = (jax.ShapeDtypeStruct((m, n), jnp.bfloat16),) * 2
        out_specs = (o_spec, o_spec)
    else:
        out_shape = jax.ShapeDtypeStruct((m, n), out_dtype)
        out_specs = o_spec
    return pl.pallas_call(
        body, name=name, grid=(m // tm, n // tn, nk), out_shape=out_shape,
        in_specs=[a_spec, b_spec] + ([o_spec] if has_extra else []), out_specs=out_specs,
        scratch_shapes=[pltpu.VMEM((tm, tn), jnp.float32)] if nk > 1 else [],
        compiler_params=pltpu.CompilerParams(dimension_semantics=("parallel", "parallel", "arbitrary"),
                                             vmem_limit_bytes=VMEM_LIMIT),
    )(*((a, b) + ((extra,) if has_extra else ())))


def _lo(x):
    return lax.stop_gradient(x.astype(jnp.bfloat16))


@jax.custom_vjp
def _linear(x, x_lo, w, slot):
    del x, slot
    return _matmul(x_lo, w, name="linear_fwd")


def _linear_fwd(x, x_lo, w, slot):
    del x, slot
    return _matmul(x_lo, w, name="linear_fwd"), (x_lo, w)


def _linear_bwd(res, dy):
    x_lo, w = res
    dy = dy.astype(jnp.bfloat16)
    dx = _matmul(dy, w, tb=True, name="linear_dx")
    dw = _matmul(x_lo, dy, ta=True, name="linear_dw")
    return dx, jnp.zeros_like(x_lo), jnp.zeros_like(w), dw


_linear.defvjp(_linear_fwd, _linear_bwd)


@jax.custom_vjp
def _mlp(h, h_lo, w_up, w_down, slot_up, slot_down):
    return _mlp_fwd(h, h_lo, w_up, w_down, slot_up, slot_down)[0]


def _mlp_fwd(h, h_lo, w_up, w_down, slot_up, slot_down):
    del h, slot_up, slot_down
    up, act = _matmul(h_lo, w_up, name="mlp_up", epilogue="relu2")
    return _matmul(act, w_down, name="mlp_down"), (h_lo, up, act, w_up, w_down)


def _mlp_bwd(res, dy):
    h_lo, up, act, w_up, w_down = res
    dy = dy.astype(jnp.bfloat16)
    d_up = _matmul(dy, w_down, tb=True, name="mlp_d_up", epilogue="relu2_grad", extra=up, out_dtype=jnp.bfloat16)
    dw_down = _matmul(act, dy, ta=True, name="mlp_dw_down")
    dw_up = _matmul(h_lo, d_up, ta=True, name="mlp_dw_up")
    dh = _matmul(d_up, w_up, tb=True, name="mlp_dh")
    return dh, jnp.zeros_like(h_lo), jnp.zeros_like(w_up), jnp.zeros_like(w_down), dw_up, dw_down


_mlp.defvjp(_mlp_fwd, _mlp_bwd)


LN_ROWS = 512


def _ln_call(h, mix, g, b):
    s, d = h.shape
    tok = pl.BlockSpec((LN_ROWS, d), lambda i: (i, 0))
    vec = pl.BlockSpec((1, d), lambda i: (0, 0))
    stat = pl.BlockSpec((LN_ROWS, 1), lambda i: (i, 0))

    def body(h_ref, mix_ref, g_ref, b_ref, y_ref, ylo_ref, xhat_ref, rstd_ref):
        z = DN_ALPHA * h_ref[...] + mix_ref[...]
        mu = jnp.mean(z, axis=-1, keepdims=True)
        zc = z - mu
        rstd = lax.rsqrt(jnp.mean(jnp.square(zc), axis=-1, keepdims=True) + LN_EPS)
        xhat = zc * rstd
        y = xhat * g_ref[...] + b_ref[...]
        y_ref[...] = y
        ylo_ref[...] = y.astype(ylo_ref.dtype)
        xhat_ref[...] = xhat
        rstd_ref[...] = rstd

    sd = jax.ShapeDtypeStruct
    return pl.pallas_call(
        body, name="ln_fwd", grid=(s // LN_ROWS,),
        out_shape=(sd((s, d), jnp.float32), sd((s, d), jnp.bfloat16), sd((s, d), jnp.float32), sd((s, 1), jnp.float32)),
        in_specs=[tok, tok, vec, vec], out_specs=(tok, tok, tok, stat),
        compiler_params=pltpu.CompilerParams(dimension_semantics=("parallel",)),
    )(h, mix, g, b)


def _ln_grad_call(dy, xhat, rstd, g):
    s, d = dy.shape
    tok = pl.BlockSpec((LN_ROWS, d), lambda i: (i, 0))
    vec = pl.BlockSpec((1, d), lambda i: (0, 0))
    stat = pl.BlockSpec((LN_ROWS, 1), lambda i: (i, 0))

    def body(dy_ref, xhat_ref, rstd_ref, g_ref, dz_ref, dg_ref, db_ref):
        @pl.when(pl.program_id(0) == 0)
        def _():
            dg_ref[...] = jnp.zeros_like(dg_ref)
            db_ref[...] = jnp.zeros_like(db_ref)

        dy, xhat = dy_ref[...], xhat_ref[...]
        dyg = dy * g_ref[...]
        m1 = jnp.mean(dyg, axis=-1, keepdims=True)
        m2 = jnp.mean(dyg * xhat, axis=-1, keepdims=True)
        dz_ref[...] = rstd_ref[...] * (dyg - m1 - xhat * m2)
        dg_ref[...] += jnp.sum(dy * xhat, axis=0, keepdims=True)
        db_ref[...] += jnp.sum(dy, axis=0, keepdims=True)

    sd = jax.ShapeDtypeStruct
    return pl.pallas_call(
        body, name="ln_bwd", grid=(s // LN_ROWS,),
        out_shape=(sd((s, d), jnp.float32), sd((1, d), jnp.float32), sd((1, d), jnp.float32)),
        in_specs=[tok, tok, stat, vec], out_specs=(tok, vec, vec),
        compiler_params=pltpu.CompilerParams(dimension_semantics=("arbitrary",)),
    )(dy, xhat, rstd, g)


@jax.custom_vjp
def _ln_res(h, mix, g, b):
    return _ln_call(h, mix, g, b)[:2]


def _ln_res_fwd(h, mix, g, b):
    y, y_lo, xhat, rstd = _ln_call(h, mix, g, b)
    return (y, y_lo), (xhat, rstd, g)


def _ln_res_bwd(res, cts):
    xhat, rstd, g = res
    dz, dg, db = _ln_grad_call(cts[0], xhat, rstd, g)
    return DN_ALPHA * dz, dz, dg, db


_ln_res.defvjp(_ln_res_fwd, _ln_res_bwd)


MXU_DTYPE = jnp.bfloat16
DN_CB = 16
DN_GROUP = 8
DN_SCAN_CB = 4
DN_SCALE = DN_DK ** -0.5


def _dot(a, b, ca=1, cb=0):
    return lax.dot_general(a.astype(MXU_DTYPE), b.astype(MXU_DTYPE), (((ca,), (cb,)), ((), ())),
                           preferred_element_type=jnp.float32)


def _chunk_masks():
    row = lax.broadcasted_iota(jnp.int32, (CHUNK, CHUNK), 0)
    col = lax.broadcasted_iota(jnp.int32, (CHUNK, CHUNK), 1)
    return row >= col, row > col, row == col


def _to_col(row_vec):
    _, _, eye = _chunk_masks()
    return jnp.sum(jnp.where(eye, jnp.broadcast_to(row_vec, (CHUNK, CHUNK)), 0.0), axis=1, keepdims=True)


def _to_row(col_vec):
    _, _, eye = _chunk_masks()
    return jnp.sum(jnp.where(eye, jnp.broadcast_to(col_vec, (CHUNK, CHUNK)), 0.0), axis=0, keepdims=True)


def _last_row(col_vec):
    last = lax.broadcasted_iota(jnp.int32, (CHUNK, 1), 0) == CHUNK - 1
    return jnp.sum(jnp.where(last, col_vec, 0.0), axis=0, keepdims=True), last


def _chunk_terms(q, k, beta, gcc, gcr):
    incl, strict, _ = _chunk_masks()
    decay = jnp.where(incl, jnp.exp(jnp.minimum(gcc - gcr, 0.0)), 0.0)
    kb = k * beta
    lmat = jnp.where(strict, _dot(kb, k, 1, 1) * decay, 0.0)
    intra = jnp.where(incl, _dot(q, k, 1, 1) * decay, 0.0)
    return decay, kb, lmat, intra


def _dot3(a, b, ca=1, cb=0):
    if MXU_DTYPE == jnp.float32:
        return _dot(a, b, ca, cb)
    a_hi, b_hi = a.astype(MXU_DTYPE), b.astype(MXU_DTYPE)
    a_lo = (a - a_hi.astype(jnp.float32)).astype(MXU_DTYPE)
    b_lo = (b - b_hi.astype(jnp.float32)).astype(MXU_DTYPE)
    return _dot(a_hi, b_hi, ca, cb) + (_dot(a_hi, b_lo, ca, cb) + _dot(a_lo, b_hi, ca, cb))


def _unit_lower_inverse(lmats):
    _, _, eye = _chunk_masks()
    ident = jnp.where(eye, 1.0, 0.0)
    ts = [ident - m for m in lmats]
    ps = [_dot(m, m) for m in lmats]
    for _ in range(4):
        ts = [t + _dot(t, p) for t, p in zip(ts, ps)]
        ps = [_dot(p, p) for p in ps]
    ts = [t + _dot(t, p) for t, p in zip(ts, ps)]
    resids = [(t - ident) + _dot3(m, t) for m, t in zip(lmats, ts)]
    return [t - _dot(t, r) for t, r in zip(ts, resids)]


def _dn_specs(n_chunks):
    tok = pl.BlockSpec((DN_CB * CHUNK, DN_DK), lambda h, n: (n, h))
    rowv = pl.BlockSpec((None, DN_CB, CHUNK), lambda h, n: (h, n, 0))
    sq = pl.BlockSpec((None, DN_CB, CHUNK, CHUNK), lambda h, n: (h, n, 0, 0))
    lane = pl.BlockSpec((None, DN_CB, 1, DN_DV), lambda h, n: (h, n, 0, 0))
    planes = [pl.BlockSpec((None, DN_CB * CHUNK, DN_DK), functools.partial(lambda h, n, p: (p, n, h), p=p))
              for p in range(3)]
    return tok, rowv, sq, lane, planes


def _dn_prep(qkv, beta, gc):
    s = qkv.shape[1]
    n_chunks = s // CHUNK
    tok, rowv, sq, lane, planes = _dn_specs(n_chunks)
    tok_shape = qkv.shape[1:]

    def body(q_ref, k_ref, v_ref, beta_ref, gc_ref, u_ref, w_ref, qd_ref, kd_ref, intra_ref, t_ref, cd_ref):
        for c0 in range(0, DN_CB, DN_GROUP):
            chunks = range(c0, c0 + DN_GROUP)
            rhs, lmats = [], []
            for c in chunks:
                rows = pl.ds(c * CHUNK, CHUNK)
                q_c, k_c, v_c = q_ref[rows, :] * DN_SCALE, k_ref[rows, :], v_ref[rows, :]
                gcr_c = gc_ref[pl.ds(c, 1), :]
                beta_c, gcc_c = _to_col(beta_ref[pl.ds(c, 1), :]), _to_col(gcr_c)
                _, kb, lmat, intra = _chunk_terms(q_c, k_c, beta_c, gcc_c, gcr_c)
                eg = jnp.exp(gcc_c)
                g_last, _ = _last_row(gcc_c)
                qd_ref[rows, :] = (q_c * eg).astype(qd_ref.dtype)
                kd_ref[rows, :] = (k_c * jnp.exp(g_last - gcc_c)).astype(kd_ref.dtype)
                intra_ref[c] = intra.astype(intra_ref.dtype)
                cd_ref[c] = jnp.broadcast_to(jnp.exp(g_last), (1, DN_DV))
                rhs.append(jnp.concatenate([v_c * beta_c, kb * eg], axis=1))
                lmats.append(lmat)
            ts = _unit_lower_inverse(lmats)
            sols = [_dot3(t, r) for t, r in zip(ts, rhs)]
            for c, t, sol in zip(chunks, ts, sols):
                rows = pl.ds(c * CHUNK, CHUNK)
                t_ref[c] = t
                u_ref[rows, :] = sol[:, :DN_DV]
                w_ref[rows, :] = sol[:, DN_DV:].astype(w_ref.dtype)

    f32, mx = jnp.float32, MXU_DTYPE
    sd = jax.ShapeDtypeStruct
    return pl.pallas_call(
        body, name="dn_prep", grid=(DN_HEADS, n_chunks // DN_CB),
        out_shape=(sd(tok_shape, f32), sd(tok_shape, mx), sd(tok_shape, mx), sd(tok_shape, mx),
                   sd((DN_HEADS, n_chunks, CHUNK, CHUNK), mx), sd((DN_HEADS, n_chunks, CHUNK, CHUNK), f32),
                   sd((DN_HEADS, n_chunks, 1, DN_DV), f32)),
        in_specs=planes + [rowv, rowv], out_specs=(tok, tok, tok, tok, sq, sq, lane),
        compiler_params=pltpu.CompilerParams(dimension_semantics=("parallel", "parallel")),
    )(qkv, qkv, qkv, beta, gc)


def _dn_scan(u, w, qd, kd, intra, cd):
    s, width = u.shape
    n_chunks = s // CHUNK
    cb = DN_SCAN_CB
    tok = pl.BlockSpec((cb * CHUNK, width), lambda n: (n, 0))
    sq = pl.BlockSpec((DN_HEADS, cb, CHUNK, CHUNK), lambda n: (0, n, 0, 0))
    lane = pl.BlockSpec((DN_HEADS, cb, 1, DN_DV), lambda n: (0, n, 0, 0))
    st = pl.BlockSpec((DN_HEADS, cb, DN_DK, DN_DV), lambda n: (0, n, 0, 0))

    def body(u_ref, w_ref, qd_ref, kd_ref, intra_ref, cd_ref, o_ref, vn_ref, st_ref, state):
        @pl.when(pl.program_id(0) == 0)
        def _():
            state[...] = jnp.zeros_like(state)

        heads = range(DN_HEADS)
        cols = [pl.ds(h * DN_DK, DN_DK) for h in heads]
        s_f = [state[h] for h in heads]
        for c in range(cb):
            rows = pl.ds(c * CHUNK, CHUNK)
            s_mx = [s.astype(MXU_DTYPE) for s in s_f]
            for h in heads:
                st_ref[h, c] = s_mx[h]
            ws = [_dot(w_ref[rows, cols[h]], s_mx[h]) for h in heads]
            qs = [_dot(qd_ref[rows, cols[h]], s_mx[h]) for h in heads]
            v_new = [(u_ref[rows, cols[h]] - ws[h]).astype(MXU_DTYPE) for h in heads]
            inner = [_dot(intra_ref[h, c], v_new[h]) for h in heads]
            outer = [_dot(kd_ref[rows, cols[h]], v_new[h], 0, 0) for h in heads]
            for h in heads:
                vn_ref[rows, cols[h]] = v_new[h]
                o_ref[rows, cols[h]] = qs[h] + inner[h]
            s_f = [s_f[h] * cd_ref[h, c] + outer[h] for h in heads]
        for h in heads:
            state[h] = s_f[h]

    sd = jax.ShapeDtypeStruct
    return pl.pallas_call(
        body, name="dn_scan", grid=(n_chunks // cb,),
        out_shape=(sd(u.shape, jnp.float32), sd(u.shape, MXU_DTYPE),
                   sd((DN_HEADS, n_chunks, DN_DK, DN_DV), MXU_DTYPE)),
        in_specs=[tok, tok, tok, tok, sq, lane], out_specs=(tok, tok, st),
        scratch_shapes=[pltpu.VMEM((DN_HEADS, DN_DK, DN_DV), jnp.float32)],
        compiler_params=pltpu.CompilerParams(dimension_semantics=("arbitrary",)),
    )(u, w, qd, kd, intra, cd)


def _dn_bwd_scan(do, w, qd, kd, intra, cd, vn, st):
    s, width = do.shape
    n_chunks = s // CHUNK
    cb = DN_SCAN_CB
    last = n_chunks // cb - 1
    tok = pl.BlockSpec((cb * CHUNK, width), lambda n: (last - n, 0))
    sq = pl.BlockSpec((DN_HEADS, cb, CHUNK, CHUNK), lambda n: (0, last - n, 0, 0))
    lane = pl.BlockSpec((DN_HEADS, cb, 1, DN_DV), lambda n: (0, last - n, 0, 0))
    stt = pl.BlockSpec((DN_HEADS, cb, DN_DK, DN_DV), lambda n: (0, last - n, 0, 0))

    def body(do_ref, w_ref, qd_ref, kd_ref, intra_ref, cd_ref, vn_ref, st_ref,
             du_ref, dw_ref, dqd_ref, dkd_ref, dintra_ref, dgl_ref, dstate):
        @pl.when(pl.program_id(0) == 0)
        def _():
            dstate[...] = jnp.zeros_like(dstate)

        heads = range(DN_HEADS)
        cols = [pl.ds(h * DN_DK, DN_DK) for h in heads]
        ds_f = [dstate[h] for h in heads]
        for c in reversed(range(cb)):
            rows = pl.ds(c * CHUNK, CHUNK)
            ds_mx = [d.astype(MXU_DTYPE) for d in ds_f]
            do_h = [do_ref[rows, cols[h]].astype(MXU_DTYPE) for h in heads]
            dv_a = [_dot(intra_ref[h, c], do_h[h], 0, 0) for h in heads]
            dv_b = [_dot(kd_ref[rows, cols[h]], ds_mx[h]) for h in heads]
            d_intra = [_dot(do_h[h], vn_ref[rows, cols[h]], 1, 1) for h in heads]
            d_qd = [_dot(do_h[h], st_ref[h, c], 1, 1) for h in heads]
            d_kd = [_dot(vn_ref[rows, cols[h]], ds_mx[h], 1, 1) for h in heads]
            ds_q = [_dot(qd_ref[rows, cols[h]], do_h[h], 0, 0) for h in heads]
            dv_new = [dv_a[h] + dv_b[h] for h in heads]
            dv_mx = [d.astype(MXU_DTYPE) for d in dv_new]
            d_w = [_dot(dv_mx[h], st_ref[h, c], 1, 1) for h in heads]
            ds_w = [_dot(w_ref[rows, cols[h]], dv_mx[h], 0, 0) for h in heads]
            ds_next = []
            for h in heads:
                du_ref[rows, cols[h]] = dv_new[h]
                dintra_ref[h, c] = d_intra[h]
                dqd_ref[rows, cols[h]] = d_qd[h]
                dkd_ref[rows, cols[h]] = d_kd[h]
                dw_ref[rows, cols[h]] = -d_w[h]
                cd_h = cd_ref[h, c]
                dcd = jnp.sum(jnp.sum(st_ref[h, c].astype(jnp.float32) * ds_f[h], axis=1, keepdims=True), axis=0,
                              keepdims=True)
                dgl_ref[h, c] = dcd * cd_h
                ds_next.append(ds_q[h] + ds_f[h] * cd_h - ds_w[h])
            ds_f = ds_next
        for h in heads:
            dstate[h] = ds_f[h]

    sd = jax.ShapeDtypeStruct
    f32 = jnp.float32
    return pl.pallas_call(
        body, name="dn_bwd_scan", grid=(n_chunks // cb,),
        out_shape=(sd(do.shape, f32), sd(do.shape, f32), sd(do.shape, f32), sd(do.shape, f32),
                   sd((DN_HEADS, n_chunks, CHUNK, CHUNK), f32), sd((DN_HEADS, n_chunks, 1, DN_DV), f32)),
        in_specs=[tok, tok, tok, tok, sq, lane, tok, stt], out_specs=(tok, tok, tok, tok, sq, lane),
        scratch_shapes=[pltpu.VMEM((DN_HEADS, DN_DK, DN_DV), f32)],
        compiler_params=pltpu.CompilerParams(dimension_semantics=("arbitrary",)),
    )(do, w, qd, kd, intra, cd, vn, st)


def _dn_bwd_chunks(qkv, beta, gc, t, u, w, du, dw, dqd, dkd, dintra, dgl):
    s = qkv.shape[1]
    n_chunks = s // CHUNK
    tok, rowv, sq, lane, planes = _dn_specs(n_chunks)
    all_planes = pl.BlockSpec((3, DN_CB * CHUNK, DN_DK), lambda h, n: (0, n, h))

    def body(q_ref, k_ref, v_ref, beta_ref, gc_ref, t_ref, u_ref, w_ref, du_ref, dw_ref, dqd_ref, dkd_ref,
             dintra_ref, dgl_ref, dqkv_ref, dbeta_ref, dgc_ref):
        incl, strict, _ = _chunk_masks()

        def first(c):
            rows = pl.ds(c * CHUNK, CHUNK)
            q_c, k_c = q_ref[rows, :] * DN_SCALE, k_ref[rows, :]
            gcr_c = gc_ref[pl.ds(c, 1), :]
            beta_c, gcc_c = _to_col(beta_ref[pl.ds(c, 1), :]), _to_col(gcr_c)
            decay, kb, lmat, intra = _chunk_terms(q_c, k_c, beta_c, gcc_c, gcr_c)
            d_sol = jnp.concatenate([du_ref[rows, :], dw_ref[rows, :]], axis=1)
            d_rhs = _dot3(t_ref[c], d_sol, 0, 0)
            return dict(rows=rows, q=q_c, k=k_c, beta=beta_c, gcc=gcc_c, decay=decay, kb=kb, lmat=lmat, intra=intra,
                        d_rhs=d_rhs)

        def second(c, e):
            sol = jnp.concatenate([u_ref[e["rows"], :], w_ref[e["rows"], :].astype(jnp.float32)], axis=1)
            e["d_l"] = jnp.where(strict, -_dot(e["d_rhs"], sol, 1, 1), 0.0)
            e["d_intra"] = jnp.where(incl, dintra_ref[c], 0.0)
            d_qk = e["d_intra"] * e["decay"]
            e["dq"] = _dot(d_qk, e["k"])
            e["dk"] = _dot(d_qk, e["q"], 0, 0)

        def third(e):
            d_a = e["d_l"] * e["decay"]
            e["dkb"] = _dot(d_a, e["k"])
            e["dk"] = e["dk"] + _dot(d_a, e["kb"], 0, 0)

        def last(c, e):
            rows, q_c, k_c, beta_c, gcc_c = e["rows"], e["q"], e["k"], e["beta"], e["gcc"]
            v_c = v_ref[rows, :]
            eg = jnp.exp(gcc_c)
            g_last, is_last = _last_row(gcc_c)
            e_rev = jnp.exp(g_last - gcc_c)
            d_rhs_u, d_rhs_w = e["d_rhs"][:, :DN_DV], e["d_rhs"][:, DN_DV:]
            dqkv_ref[2, rows, :] = d_rhs_u * beta_c
            dbeta = jnp.sum(d_rhs_u * v_c, axis=1, keepdims=True)
            dkb = e["dkb"] + d_rhs_w * eg
            dgc = jnp.sum(d_rhs_w * e["kb"] * eg, axis=1, keepdims=True)
            m1 = e["d_l"] * e["lmat"]
            dgc = dgc + jnp.sum(m1, axis=1, keepdims=True)
            dgr = -jnp.sum(m1, axis=0, keepdims=True)
            m2 = e["d_intra"] * e["intra"]
            dgc = dgc + jnp.sum(m2, axis=1, keepdims=True)
            dgr = dgr - jnp.sum(m2, axis=0, keepdims=True)
            dqd = dqd_ref[rows, :]
            dq = e["dq"] + dqd * eg
            dgc = dgc + jnp.sum(dqd * q_c * eg, axis=1, keepdims=True)
            dkd = dkd_ref[rows, :]
            dk = e["dk"] + dkd * e_rev
            tk = jnp.sum(dkd * k_c * e_rev, axis=1, keepdims=True)
            dgc = dgc - tk
            d_last = dgl_ref[c][:, :1] + jnp.sum(tk, axis=0, keepdims=True)
            dgc = dgc + jnp.where(is_last, d_last, 0.0)
            dk = dk + dkb * beta_c
            dbeta = dbeta + jnp.sum(dkb * k_c, axis=1, keepdims=True)
            dqkv_ref[0, rows, :] = dq * DN_SCALE
            dqkv_ref[1, rows, :] = dk
            dbeta_ref[pl.ds(c, 1), :] = _to_row(dbeta)
            dgc_ref[pl.ds(c, 1), :] = _to_row(dgc) + dgr

        for c0 in range(0, DN_CB, DN_GROUP):
            chunks = range(c0, c0 + DN_GROUP)
            env = [first(c) for c in chunks]
            for c, e in zip(chunks, env):
                second(c, e)
            for e in env:
                third(e)
            for c, e in zip(chunks, env):
                last(c, e)

    sd = jax.ShapeDtypeStruct
    f32 = jnp.float32
    return pl.pallas_call(
        body, name="dn_bwd_chunks", grid=(DN_HEADS, n_chunks // DN_CB),
        out_shape=(sd(qkv.shape, f32), sd(beta.shape, f32), sd(gc.shape, f32)),
        in_specs=planes + [rowv, rowv, sq, tok, tok, tok, tok, tok, tok, sq, lane],
        out_specs=(all_planes, rowv, rowv),
        compiler_params=pltpu.CompilerParams(dimension_semantics=("parallel", "parallel")),
    )(qkv, qkv, qkv, beta, gc, t, u, w, du, dw, dqd, dkd, dintra, dgl)


@jax.custom_vjp
def _delta_rule_op(qkv, beta, gc):
    return _delta_rule_fwd(qkv, beta, gc)[0]


def _delta_rule_fwd(qkv, beta, gc):
    u, w, qd, kd, intra, t, cd = _dn_prep(qkv, beta, gc)
    out, vn, st = _dn_scan(u, w, qd, kd, intra, cd)
    return out, (qkv, beta, gc, u, w, qd, kd, intra, t, cd, vn, st)


def _delta_rule_bwd(res, do):
    qkv, beta, gc, u, w, qd, kd, intra, t, cd, vn, st = res
    du, dw, dqd, dkd, dintra, dgl = _dn_bwd_scan(do, w, qd, kd, intra, cd, vn, st)
    return _dn_bwd_chunks(qkv, beta, gc, t, u, w, du, dw, dqd, dkd, dintra, dgl)


_delta_rule_op.defvjp(_delta_rule_fwd, _delta_rule_bwd)


def _gated_delta_rule(qkv, g, beta):
    s, h = g.shape
    n_chunks = s // CHUNK
    gc = jnp.cumsum(g.T.reshape(h, n_chunks, CHUNK), axis=-1)
    return _delta_rule_op(qkv, beta.T.reshape(h, n_chunks, CHUNK), gc)


PRE_ROWS = 1024
HALO = 8
PRE_W = DN_QK_W


def _shift_rows(xs, k):
    return pltpu.roll(xs, k, 0)[HALO:]


def _conv_silu(x_ref, halo_ref, w_ref, first_block):
    halo = jnp.where(first_block, 0.0, halo_ref[...])
    xs = jnp.concatenate([halo, x_ref[...]], axis=0)
    taps = [_shift_rows(xs, CONV_WIDTH - 1 - j) for j in range(CONV_WIDTH - 1)] + [x_ref[...]]
    conv = sum(w_ref[pl.ds(j, 1), :] * taps[j] for j in range(CONV_WIDTH))
    return conv, jax.nn.sigmoid(conv), taps


def _pre_specs():
    blk = pl.BlockSpec((PRE_ROWS, PRE_W), lambda j, i: (i, j))
    prev = pl.BlockSpec((HALO, PRE_W), lambda j, i: (jnp.maximum(i * (PRE_ROWS // HALO) - 1, 0), j))
    wts = pl.BlockSpec((CONV_WIDTH, PRE_W), lambda j, i: (0, j))
    plane = pl.BlockSpec((None, PRE_ROWS, PRE_W), lambda j, i: (j, i, 0))
    return blk, prev, wts, plane


def _pre_fwd_call(x, conv_w):
    s = x.shape[0]
    blk, prev, wts, plane = _pre_specs()

    def body(x_ref, halo_ref, w_ref, o_ref):
        conv, sig, _ = _conv_silu(x_ref, halo_ref, w_ref, pl.program_id(1) == 0)
        act = conv * sig
        is_v = pl.program_id(0) == 2
        for h in range(DN_HEADS):
            cols = slice(h * DN_DK, (h + 1) * DN_DK)
            a_h = act[:, cols]
            r = lax.rsqrt(jnp.sum(a_h * a_h, axis=-1, keepdims=True) + NORM_EPS)
            o_ref[:, cols] = a_h * jnp.where(is_v, 1.0, r)

    return pl.pallas_call(
        body, name="pre_fwd", grid=(3, s // PRE_ROWS),
        out_shape=jax.ShapeDtypeStruct((3, s, PRE_W), jnp.float32),
        in_specs=[blk, prev, wts], out_specs=plane,
        compiler_params=pltpu.CompilerParams(dimension_semantics=("parallel", "parallel")),
    )(x, x, conv_w)


def _pre_bwd_act_call(x, conv_w, d_out):
    s = x.shape[0]
    blk, prev, wts, plane = _pre_specs()

    def body(x_ref, halo_ref, w_ref, do_ref, dc_ref):
        conv, sig, _ = _conv_silu(x_ref, halo_ref, w_ref, pl.program_id(1) == 0)
        act = conv * sig
        d_silu = sig * (1.0 + conv * (1.0 - sig))
        is_v = pl.program_id(0) == 2
        for h in range(DN_HEADS):
            cols = slice(h * DN_DK, (h + 1) * DN_DK)
            a_h, do_h = act[:, cols], do_ref[:, cols]
            r = lax.rsqrt(jnp.sum(a_h * a_h, axis=-1, keepdims=True) + NORM_EPS)
            n_h = a_h * r
            d_norm = r * (do_h - n_h * jnp.sum(do_h * n_h, axis=-1, keepdims=True))
            dc_ref[:, cols] = jnp.where(is_v, do_h, d_norm) * d_silu[:, cols]

    return pl.pallas_call(
        body, name="pre_bwd_act", grid=(3, s // PRE_ROWS),
        out_shape=jax.ShapeDtypeStruct(x.shape, jnp.float32),
        in_specs=[blk, prev, wts, plane], out_specs=blk,
        compiler_params=pltpu.CompilerParams(dimension_semantics=("parallel", "parallel")),
    )(x, x, conv_w, d_out)


def _pre_bwd_conv_call(x, conv_w, dc):
    s = x.shape[0]
    n_blocks = s // PRE_ROWS
    blk, prev, wts, _ = _pre_specs()
    nxt = pl.BlockSpec((HALO, PRE_W), lambda j, i: (jnp.minimum((i + 1) * (PRE_ROWS // HALO), s // HALO - 1), j))

    def body(x_ref, halo_ref, w_ref, dc_ref, dcn_ref, dx_ref, dw_ref):
        i = pl.program_id(1)

        @pl.when(i == 0)
        def _():
            dw_ref[...] = jnp.zeros_like(dw_ref)

        dcv = dc_ref[...]
        ahead = jnp.concatenate([dcv, jnp.where(i == n_blocks - 1, 0.0, dcn_ref[...])], axis=0)
        dx = w_ref[pl.ds(CONV_WIDTH - 1, 1), :] * dcv
        for j in range(CONV_WIDTH - 1):
            k = CONV_WIDTH - 1 - j
            dx = dx + w_ref[pl.ds(j, 1), :] * pltpu.roll(ahead, PRE_ROWS + HALO - k, 0)[:PRE_ROWS]
        dx_ref[...] = dx
        halo = jnp.where(i == 0, 0.0, halo_ref[...])
        xs = jnp.concatenate([halo, x_ref[...]], axis=0)
        for j in range(CONV_WIDTH):
            tap = x_ref[...] if j == CONV_WIDTH - 1 else _shift_rows(xs, CONV_WIDTH - 1 - j)
            dw_ref[pl.ds(j, 1), :] += jnp.sum(dcv * tap, axis=0, keepdims=True)

    sd = jax.ShapeDtypeStruct
    return pl.pallas_call(
        body, name="pre_bwd_conv", grid=(3, n_blocks),
        out_shape=(sd(x.shape, jnp.float32), sd(conv_w.shape, jnp.float32)),
        in_specs=[blk, prev, wts, blk, nxt], out_specs=(blk, wts),
        compiler_params=pltpu.CompilerParams(dimension_semantics=("parallel", "arbitrary")),
    )(x, x, conv_w, dc, dc)


@jax.custom_vjp
def _pre_op(x, conv_w):
    return _pre_fwd_call(x, conv_w)


def _pre_op_fwd(x, conv_w):
    return _pre_fwd_call(x, conv_w), (x, conv_w)


def _pre_op_bwd(res, d_out):
    x, conv_w = res
    return _pre_bwd_conv_call(x, conv_w, _pre_bwd_act_call(x, conv_w, d_out))


_pre_op.defvjp(_pre_op_fwd, _pre_op_bwd)


def _project(h, h_lo, w, slot):
    b, s, d = h.shape
    return _linear(h.reshape(b * s, d), h_lo.reshape(b * s, d), w, slot).reshape(b, s, w.shape[1])


def _rope_table(positions, dh):
    inv_freq = ROPE_THETA ** (-jnp.arange(0, dh, 2, dtype=jnp.float32) / dh)
    ang = positions.astype(jnp.float32)[:, None] * inv_freq
    reps = 128 // (dh // 2)
    return jnp.concatenate([jnp.tile(jnp.cos(ang), (1, reps)), jnp.tile(jnp.sin(ang), (1, reps))], axis=-1)


GATE_ROWS = 1024


def _gate_terms(o_h, z_h):
    r = lax.rsqrt(jnp.mean(o_h * o_h, axis=-1, keepdims=True) + NORM_EPS)
    sig = jax.nn.sigmoid(z_h)
    return r, o_h * r, sig, z_h * sig


def _gate_fwd_call(o, z, nw):
    tok = pl.BlockSpec((GATE_ROWS, DN_V_W), lambda i: (i, 0))
    vec = pl.BlockSpec((1, DN_DV), lambda i: (0, 0))

    def body(o_ref, z_ref, nw_ref, y_ref):
        for h in range(DN_HEADS):
            cols = pl.ds(h * DN_DV, DN_DV)
            _, n_h, _, g_h = _gate_terms(o_ref[:, cols], z_ref[:, cols])
            y_ref[:, cols] = n_h * nw_ref[...] * g_h

    return pl.pallas_call(
        body, name="gate_fwd", grid=(o.shape[0] // GATE_ROWS,),
        out_shape=jax.ShapeDtypeStruct(o.shape, jnp.float32), in_specs=[tok, tok, vec], out_specs=tok,
        compiler_params=pltpu.CompilerParams(dimension_semantics=("parallel",)),
    )(o, z, nw)


def _gate_bwd_call(o, z, nw, dy):
    tok = pl.BlockSpec((GATE_ROWS, DN_V_W), lambda i: (i, 0))
    vec = pl.BlockSpec((1, DN_DV), lambda i: (0, 0))

    def body(o_ref, z_ref, nw_ref, dy_ref, do_ref, dz_ref, dnw_ref):
        @pl.when(pl.program_id(0) == 0)
        def _():
            dnw_ref[...] = jnp.zeros_like(dnw_ref)

        for h in range(DN_HEADS):
            cols = pl.ds(h * DN_DV, DN_DV)
            z_h, dy_h = z_ref[:, cols], dy_ref[:, cols]
            r, n_h, sig, g_h = _gate_terms(o_ref[:, cols], z_h)
            dz_ref[:, cols] = dy_h * n_h * nw_ref[...] * (sig * (1.0 + z_h * (1.0 - sig)))
            dn = dy_h * nw_ref[...] * g_h
            do_ref[:, cols] = r * (dn - n_h * jnp.mean(dn * n_h, axis=-1, keepdims=True))
            dnw_ref[...] += jnp.sum(dy_h * n_h * g_h, axis=0, keepdims=True)

    sd = jax.ShapeDtypeStruct
    return pl.pallas_call(
        body, name="gate_bwd", grid=(o.shape[0] // GATE_ROWS,),
        out_shape=(sd(o.shape, jnp.float32), sd(o.shape, jnp.float32), sd(nw.shape, jnp.float32)),
        in_specs=[tok, tok, vec, tok], out_specs=(tok, tok, vec),
        compiler_params=pltpu.CompilerParams(dimension_semantics=("arbitrary",)),
    )(o, z, nw, dy)


@jax.custom_vjp
def _gate_op(o, z, nw):
    return _gate_fwd_call(o, z, nw)


def _gate_op_fwd(o, z, nw):
    return _gate_fwd_call(o, z, nw), (o, z, nw)


def _gate_op_bwd(res, dy):
    return _gate_bwd_call(*res, dy)


_gate_op.defvjp(_gate_op_fwd, _gate_op_bwd)


_MASKED = -1e30


def _swa_probs(qs, k_h, sinks, valid):
    ss = [jnp.where(valid, _dot(q_h, k_h, 1, 1) * (SWA_DH ** -0.5), _MASKED) for q_h in qs]
    ms = [jnp.maximum(jnp.max(s, axis=-1, keepdims=True), sink) for s, sink in zip(ss, sinks)]
    ps = [jnp.exp(s - m) for s, m in zip(ss, ms)]
    es = [jnp.exp(sink - m) for sink, m in zip(sinks, ms)]
    invs = [1.0 / (jnp.sum(p, axis=-1, keepdims=True) + e) for p, e in zip(ps, es)]
    return [p * inv for p, inv in zip(ps, invs)], [e * inv for e, inv in zip(es, invs)]


def _swa_valid(n):
    qi = lax.broadcasted_iota(jnp.int32, (WINDOW, 2 * WINDOW), 0)
    kj = lax.broadcasted_iota(jnp.int32, (WINDOW, 2 * WINDOW), 1)
    diff = qi + WINDOW - kj
    return (diff >= 0) & (diff < WINDOW) & ((kj >= WINDOW) | (n > 0))


def _rotate_half(x, transpose=False):
    half = SWA_DH // 2
    lower = lax.broadcasted_iota(jnp.int32, x.shape, 1) % SWA_DH < half
    ahead, behind = pltpu.roll(x, 128 - half, 1), pltpu.roll(x, half, 1)
    return jnp.where(lower, ahead, -behind) if transpose else jnp.where(lower, -ahead, behind)


def _rope(x, table):
    return x * table[:, :128] + _rotate_half(x) * table[:, 128:]


def _unrope(dy, table):
    return dy * table[:, :128] + _rotate_half(dy * table[:, 128:], transpose=True)


def _swa_specs():
    qs = pl.BlockSpec((WINDOW, SWA_Q_W), lambda n: (n, 0))
    first = lambda n: jnp.maximum(n - 1, 0)
    kv = [pl.BlockSpec((WINDOW, SWA_KV_W), lambda n: (first(n), 0)), pl.BlockSpec((WINDOW, SWA_KV_W), lambda n: (n, 0)),
          pl.BlockSpec((WINDOW, SWA_KV_W), lambda n: (first(n), 1)), pl.BlockSpec((WINDOW, SWA_KV_W), lambda n: (n, 1))]
    tables = [pl.BlockSpec((WINDOW, 256), lambda n: (first(n), 0)), pl.BlockSpec((WINDOW, 256), lambda n: (n, 0))]
    cur = pl.BlockSpec((WINDOW, SWA_KV_W), lambda n: (n, 0))
    sk = pl.BlockSpec((SWA_HEADS, 1, 128), lambda n: (0, 0, 0))
    return qs, kv, tables, cur, sk


def _swa_load(q_ref, kp_ref, kc_ref, vp_ref, vc_ref, tp_ref, tc_ref):
    table_kk = jnp.concatenate([tp_ref[...], tc_ref[...]], axis=0)
    kk = _rope(jnp.concatenate([kp_ref[...], kc_ref[...]], axis=0), table_kk)
    vv = jnp.concatenate([vp_ref[...], vc_ref[...]], axis=0)
    q_rot = []
    for b in range(SWA_Q_W // 128):
        pair = _rope(q_ref[:, pl.ds(b * 128, 128)], tc_ref[...])
        q_rot += [pair[:, :SWA_DH], pair[:, SWA_DH:]]
    split = lambda t: [t[:, hkv * SWA_DH:(hkv + 1) * SWA_DH] for hkv in range(SWA_KV_HEADS)]
    return q_rot, split(kk), split(vv), table_kk


def _swa_fwd_call(q, kv, table, sinks):
    qs, kvs, tables, _, sk = _swa_specs()

    def body(q_ref, kp_ref, kc_ref, vp_ref, vc_ref, tp_ref, tc_ref, sink_ref, o_ref):
        valid = _swa_valid(pl.program_id(0))
        q_rot, kk, vv, _ = _swa_load(q_ref, kp_ref, kc_ref, vp_ref, vc_ref, tp_ref, tc_ref)
        for hkv in range(SWA_KV_HEADS):
            heads = range(hkv * SWA_GROUP, (hkv + 1) * SWA_GROUP)
            probs, _ = _swa_probs([q_rot[h] for h in heads], kk[hkv], [sink_ref[h][:, :1] for h in heads], valid)
            outs = [_dot(p, vv[hkv]) for p in probs]
            for h, o in zip(heads, outs):
                o_ref[:, pl.ds(h * SWA_DH, SWA_DH)] = o

    return pl.pallas_call(
        body, name="swa_fwd", grid=(q.shape[0] // WINDOW,),
        out_shape=jax.ShapeDtypeStruct(q.shape, jnp.float32),
        in_specs=[qs] + kvs + tables + [sk], out_specs=qs,
        compiler_params=pltpu.CompilerParams(dimension_semantics=("parallel",)),
    )(q, kv, kv, kv, kv, table, table, sinks)


def _swa_bwd_call(q, kv, table, sinks, do):
    qs, kvs, tables, cur, sk = _swa_specs()

    def body(q_ref, kp_ref, kc_ref, vp_ref, vc_ref, tp_ref, tc_ref, sink_ref, do_ref,
             dq_ref, dkc_ref, dkp_ref, dvc_ref, dvp_ref, ds_ref):
        @pl.when(pl.program_id(0) == 0)
        def _():
            ds_ref[...] = jnp.zeros_like(ds_ref)

        valid = _swa_valid(pl.program_id(0))
        q_rot, kk, vv, table_kk = _swa_load(q_ref, kp_ref, kc_ref, vp_ref, vc_ref, tp_ref, tc_ref)
        lane0 = lax.broadcasted_iota(jnp.int32, (1, 128), 1) == 0
        dq_heads, dk_heads, dv_heads = [], [], []
        for hkv in range(SWA_KV_HEADS):
            k_h, v_h = kk[hkv], vv[hkv]
            heads = range(hkv * SWA_GROUP, (hkv + 1) * SWA_GROUP)
            q_hs = [q_rot[h] for h in heads]
            dos = [do_ref[:, pl.ds(h * SWA_DH, SWA_DH)] for h in heads]
            probs, p_sinks = _swa_probs(q_hs, k_h, [sink_ref[h][:, :1] for h in heads], valid)
            dps = [_dot(do_h, v_h, 1, 1) for do_h in dos]
            rss = [jnp.sum(p * dp, axis=-1, keepdims=True) for p, dp in zip(probs, dps)]
            d_ss = [p * (dp - rs) for p, dp, rs in zip(probs, dps, rss)]
            dq_heads += [_dot(d_s, k_h) * (SWA_DH ** -0.5) for d_s in d_ss]
            dks = [_dot(d_s, q_h, 0, 0) for d_s, q_h in zip(d_ss, q_hs)]
            dvs = [_dot(p, do_h, 0, 0) for p, do_h in zip(probs, dos)]
            for h, p_sink, rs in zip(heads, p_sinks, rss):
                d_sink = -jnp.sum(p_sink * rs, axis=0, keepdims=True)
                ds_ref[h] += jnp.where(lane0, d_sink, 0.0)
            dk_heads.append(sum(dks[1:], dks[0]) * (SWA_DH ** -0.5))
            dv_heads.append(sum(dvs[1:], dvs[0]))
        for b in range(SWA_Q_W // 128):
            pair = jnp.concatenate([dq_heads[2 * b], dq_heads[2 * b + 1]], axis=1)
            dq_ref[:, pl.ds(b * 128, 128)] = _unrope(pair, tc_ref[...])
        dk = _unrope(jnp.concatenate(dk_heads, axis=1), table_kk)
        dv = jnp.concatenate(dv_heads, axis=1)
        dkp_ref[...] = dk[:WINDOW]
        dkc_ref[...] = dk[WINDOW:]
        dvp_ref[...] = dv[:WINDOW]
        dvc_ref[...] = dv[WINDOW:]

    sd = jax.ShapeDtypeStruct
    f32 = jnp.float32
    half = (q.shape[0], SWA_KV_W)
    return pl.pallas_call(
        body, name="swa_bwd", grid=(q.shape[0] // WINDOW,),
        out_shape=(sd(q.shape, f32), sd(half, f32), sd(half, f32), sd(half, f32), sd(half, f32), sd(sinks.shape, f32)),
        in_specs=[qs] + kvs + tables + [sk, qs], out_specs=(qs, cur, cur, cur, cur, sk),
        compiler_params=pltpu.CompilerParams(dimension_semantics=("arbitrary",)),
    )(q, kv, kv, kv, kv, table, table, sinks, do)


@jax.custom_vjp
def _swa_op(q, kv, table, sinks):
    return _swa_fwd_call(q, kv, table, sinks)


def _swa_op_fwd(q, kv, table, sinks):
    return _swa_fwd_call(q, kv, table, sinks), (q, kv, table, sinks)


def _swa_op_bwd(res, do):
    q, kv, table, sinks = res
    dq, dkc, dkp, dvc, dvp, dsinks = _swa_bwd_call(q, kv, table, sinks, do)

    def fold(cur, prev):
        return cur + jnp.concatenate([prev[WINDOW:], jnp.zeros_like(prev[:WINDOW])], axis=0)

    return dq, jnp.concatenate([fold(dkc, dkp), fold(dvc, dvp)], axis=1), jnp.zeros_like(table), dsinks


_swa_op.defvjp(_swa_op_fwd, _swa_op_bwd)


def _swa_sink_attention(q, kv, table, sinks):
    return _swa_op(q, kv, table, jnp.broadcast_to(sinks[:, None, None], (SWA_HEADS, 1, 128)))


MEM_ROWS = 1024


def _mem_probs(q_h, k_h):
    s = _dot(q_h, k_h, 1, 1) * (MEM_DH ** -0.5)
    p = jnp.exp(s - jnp.max(s, axis=-1, keepdims=True))
    return p / jnp.sum(p, axis=-1, keepdims=True)


def _mem_fwd_call(qm, kv):
    qs = pl.BlockSpec((MEM_ROWS, MEM_W), lambda i: (i, 0))
    kvs = pl.BlockSpec(kv.shape, lambda i: (0, 0))

    def body(q_ref, kv_ref, o_ref):
        for h in range(MEM_HEADS):
            cols = pl.ds(h * MEM_DH, MEM_DH)
            probs = _mem_probs(q_ref[:, cols], kv_ref[:, cols])
            o_ref[:, cols] = _dot(probs, kv_ref[:, pl.ds(MEM_W + h * MEM_DH, MEM_DH)])

    return pl.pallas_call(
        body, name="mem_fwd", grid=(qm.shape[0] // MEM_ROWS,),
        out_shape=jax.ShapeDtypeStruct(qm.shape, jnp.float32), in_specs=[qs, kvs], out_specs=qs,
        compiler_params=pltpu.CompilerParams(dimension_semantics=("parallel",)),
    )(qm, kv)


def _mem_bwd_call(qm, kv, do):
    qs = pl.BlockSpec((MEM_ROWS, MEM_W), lambda i: (i, 0))
    kvs = pl.BlockSpec(kv.shape, lambda i: (0, 0))

    def body(q_ref, kv_ref, do_ref, dq_ref, dkv_ref):
        @pl.when(pl.program_id(0) == 0)
        def _():
            dkv_ref[...] = jnp.zeros_like(dkv_ref)

        for h in range(MEM_HEADS):
            cols = pl.ds(h * MEM_DH, MEM_DH)
            v_cols = pl.ds(MEM_W + h * MEM_DH, MEM_DH)
            q_h, k_h, do_h = q_ref[:, cols], kv_ref[:, cols], do_ref[:, cols]
            probs = _mem_probs(q_h, k_h)
            dp = _dot(do_h, kv_ref[:, v_cols], 1, 1)
            d_s = probs * (dp - jnp.sum(probs * dp, axis=-1, keepdims=True))
            dq_ref[:, cols] = _dot(d_s, k_h) * (MEM_DH ** -0.5)
            dkv_ref[:, cols] += _dot(d_s, q_h, 0, 0) * (MEM_DH ** -0.5)
            dkv_ref[:, v_cols] += _dot(probs, do_h, 0, 0)

    sd = jax.ShapeDtypeStruct
    return pl.pallas_call(
        body, name="mem_bwd", grid=(qm.shape[0] // MEM_ROWS,),
        out_shape=(sd(qm.shape, jnp.float32), sd(kv.shape, jnp.float32)),
        in_specs=[qs, kvs, qs], out_specs=(qs, kvs),
        compiler_params=pltpu.CompilerParams(dimension_semantics=("arbitrary",)),
    )(qm, kv, do)


@jax.custom_vjp
def _mem_op(qm, kv):
    return _mem_fwd_call(qm, kv)


def _mem_op_fwd(qm, kv):
    return _mem_fwd_call(qm, kv), (qm, kv)


def _mem_op_bwd(res, do):
    return _mem_bwd_call(*res, do)


_mem_op.defvjp(_mem_op_fwd, _mem_op_bwd)


def _memory_attention(qm, kv):
    return _mem_op(qm[0], kv[0])[None]


def _mixer_a(h, h_lo, mem, mem_lo, p, s, layer):
    B, S, _ = h.shape
    proj = _project(h, h_lo, p["a_w_in"][layer], s["a_w_in"][layer])
    c1 = 2 * DN_QK_W + DN_V_W
    qkv = proj[..., :c1]
    z = proj[..., c1:QKVZ_W]
    qm = proj[..., QKVZ_W:QKVZ_W + MEM_W]
    a = proj[..., QKVZ_W + MEM_W:QKVZ_W + MEM_W + DN_HEADS]
    b = proj[..., QKVZ_W + MEM_W + DN_HEADS:QKVZ_W + MEM_W + 2 * DN_HEADS]
    planes = _pre_op(qkv[0], p["a_conv_w"][layer])
    beta = jax.nn.sigmoid(b[0])
    g = -jnp.exp(p["a_A_log"][layer]) * jax.nn.softplus(a[0] + p["a_dt_bias"][layer])
    o = _gate_op(_gated_delta_rule(planes, g, beta), z[0], p["a_norm_w"][layer][None])[None]
    kv = _project(mem, mem_lo, p["mem_w_kv"][layer], s["mem_w_kv"][layer])
    mo = _memory_attention(qm, kv)
    cat = jnp.concatenate([o, mo], axis=-1)
    return _project(cat, _lo(cat), p["w_o"][layer], s["w_o"][layer])


def _mixer_b(h, h_lo, mem, mem_lo, kv_shared, table, p, s, layer):
    j = layer - N_A
    proj = _project(h, h_lo, p["b_w_in"][j], s["b_w_in"][j])
    o = _swa_sink_attention(proj[0, :, :SWA_Q_W], kv_shared, table, p["b_sinks"][j])[None]
    kv = _project(mem, mem_lo, p["mem_w_kv"][layer], s["mem_w_kv"][layer])
    mo = _memory_attention(proj[..., SWA_Q_W:], kv)
    cat = jnp.concatenate([o, mo], axis=-1)
    return _project(cat, _lo(cat), p["w_o"][layer], s["w_o"][layer])


def _forward(p, s, x, mem, positions):
    table = _rope_table(positions[0], SWA_DH)
    h, h_lo, mem_lo = x, _lo(x), _lo(mem)
    kv_shared = None
    for layer in range(DEPTH):
        if layer < N_A:
            mix = _mixer_a(h, h_lo, mem, mem_lo, p, s, layer)
        else:
            mix = _mixer_b(h, h_lo, mem, mem_lo, kv_shared, table, p, s, layer)
        seq = h.shape[1]
        h2, h2_lo = _ln_res(h[0], mix[0], p["ln_g"][layer, 0][None], p["ln_b"][layer, 0][None])
        down = _mlp(h2, h2_lo, p["mlp_w_up"][layer], p["mlp_w_down"][layer], s["mlp_w_up"][layer],
                    s["mlp_w_down"][layer])
        h, h_lo = _ln_res(h2, down, p["ln_g"][layer, 1][None], p["ln_b"][layer, 1][None])
        h, h_lo = h.reshape(1, seq, D_MODEL), h_lo.reshape(1, seq, D_MODEL)
        if layer == N_A - 1:
            kv_shared = _project(h, h_lo, p["w_kv_shared"], s["w_kv_shared"])[0]
    return h


def _loss(diff, s, p, mem, positions, target):
    y = _forward({**p, **diff["small"]}, s, diff["x"], mem, positions)
    return 0.5 * jnp.sum(jnp.mean(jnp.square(y - target), axis=-1))


def _reorder_a_w_in(w):
    pad = jnp.zeros(w.shape[:-1] + (A_IN_PAD - A_IN,), w.dtype)
    return jnp.concatenate([w[..., :QKVZ_W], w[..., QKVZ_W + 2 * DN_HEADS:], w[..., QKVZ_W:QKVZ_W + 2 * DN_HEADS], pad],
                           axis=-1)


def _restore_a_w_in(w):
    return jnp.concatenate([w[..., :QKVZ_W], w[..., QKVZ_W + MEM_W:QKVZ_W + MEM_W + 2 * DN_HEADS],
                            w[..., QKVZ_W:QKVZ_W + MEM_W]], axis=-1)


def kernel(x, mem, positions, a_w_in, a_conv_w, a_A_log, a_dt_bias, a_norm_w, b_w_in, b_sinks, w_kv_shared, mem_w_kv, w_o, mlp_w_up, mlp_w_down, ln_g, ln_b, loss_target, m_a_w_in, m_a_conv_w, m_a_A_log, m_a_dt_bias, m_a_norm_w, m_b_w_in, m_b_sinks, m_w_kv_shared, m_mem_w_kv, m_w_o, m_mlp_w_up, m_mlp_w_down, m_ln_g, m_ln_b, v_a_w_in, v_a_conv_w, v_a_A_log, v_a_dt_bias, v_a_norm_w, v_b_w_in, v_b_sinks, v_w_kv_shared, v_mem_w_kv, v_w_o, v_mlp_w_up, v_mlp_w_down, v_ln_g, v_ln_b):
    w_sh = dict(a_w_in=a_w_in, a_conv_w=a_conv_w, a_A_log=a_A_log, a_dt_bias=a_dt_bias, a_norm_w=a_norm_w,
                b_w_in=b_w_in, b_sinks=b_sinks, w_kv_shared=w_kv_shared, mem_w_kv=mem_w_kv, w_o=w_o,
                mlp_w_up=mlp_w_up, mlp_w_down=mlp_w_down, ln_g=ln_g, ln_b=ln_b)
    m_sh = dict(a_w_in=m_a_w_in, a_conv_w=m_a_conv_w, a_A_log=m_a_A_log, a_dt_bias=m_a_dt_bias, a_norm_w=m_a_norm_w,
                b_w_in=m_b_w_in, b_sinks=m_b_sinks, w_kv_shared=m_w_kv_shared, mem_w_kv=m_mem_w_kv, w_o=m_w_o,
                mlp_w_up=m_mlp_w_up, mlp_w_down=m_mlp_w_down, ln_g=m_ln_g, ln_b=m_ln_b)
    v_sh = dict(a_w_in=v_a_w_in, a_conv_w=v_a_conv_w, a_A_log=v_a_A_log, a_dt_bias=v_a_dt_bias, a_norm_w=v_a_norm_w,
                b_w_in=v_b_w_in, b_sinks=v_b_sinks, w_kv_shared=v_w_kv_shared, mem_w_kv=v_mem_w_kv, w_o=v_w_o,
                mlp_w_up=v_mlp_w_up, mlp_w_down=v_mlp_w_down, ln_g=v_ln_g, ln_b=v_ln_b)
    shard_shapes = {n: w_sh[n].shape for n in WEIGHTS}
    rb, rows = _rows_for(w_sh)

    big, small = _pack(w_sh, rb, jnp.bfloat16)
    gbig, gsmall = _gather_weights(big.reshape(2, rb // 2, FLAT_W), small.reshape(2, SMALL_ROWS // 2, FLAT_W))
    gbig, gsmall = gbig.reshape(N_CHIPS, rb, FLAT_W), gsmall.reshape(N_CHIPS, SMALL_ROWS, FLAT_W)
    pieces = [_unpack(gbig[q], gsmall[q], shard_shapes) for q in range(N_CHIPS)]
    full = {n: jnp.concatenate([pieces[q][n] for q in range(N_CHIPS)], axis=SHARD_AXIS[n]) for n in SHARD_AXIS}
    for n in REPLICATED:
        full[n] = w_sh[n]
    big_w = {n: full[n] for n in BIG}
    big_w["a_w_in"] = _reorder_a_w_in(big_w["a_w_in"])
    small_w = {n: full[n] for n in SMALL}
    slots = {n: jnp.zeros(big_w[n].shape, jnp.float32) for n in BIG}

    loss, (grads, g_slots) = jax.value_and_grad(_loss, argnums=(0, 1))(
        {"x": x, "small": small_w}, slots, big_w, mem, positions, loss_target)
    loss = lax.psum(loss, ("x", "y", "c"))
    g_full = {**g_slots, **grads["small"]}
    g_full["a_w_in"] = _restore_a_w_in(g_full["a_w_in"])

    def shard_of(n, q):
        if n in REPLICATED:
            return g_full[n]
        size = shard_shapes[n][SHARD_AXIS[n]]
        return lax.slice_in_dim(g_full[n], q * size, (q + 1) * size, axis=SHARD_AXIS[n])

    parts = []
    for q in range(N_CHIPS):
        pb, ps = _pack({n: shard_of(n, q) for n in WEIGHTS}, rb, jnp.bfloat16)
        parts.append(jnp.concatenate([pb, ps.astype(jnp.bfloat16)], axis=0).reshape(2, rows // 2, FLAT_W))
    partials = jnp.stack(parts, axis=1)
    half = lax.axis_index("c").astype(jnp.int32).reshape(1)
    chip_partials = _add_pairs(partials, _swap_halves(partials), half)
    g_flat = _join_halves(_sum_chips(_scatter_grads(chip_partials))).reshape(rows, FLAT_W)

    odd = "a_w_in"
    blank = jnp.zeros(shard_shapes[odd], jnp.float32)
    flat = [jnp.concatenate(_pack({**d, odd: blank}, rb), axis=0) for d in (w_sh, m_sh, v_sh)]
    outs = (g_flat,) + tuple(_adamw(g_flat, *flat))
    g_o, d_o, m_o, v_o = [_unpack(o[:rb], o[rb:], shard_shapes) for o in outs]
    as_rows = lambda t: t.reshape(-1, t.shape[-1])
    updated = _adamw(as_rows(g_o[odd]), as_rows(w_sh[odd]), as_rows(m_sh[odd]), as_rows(v_sh[odd]))
    d_o[odd], m_o[odd], v_o[odd] = [t.reshape(shard_shapes[odd]) for t in updated]
    return (loss, grads["x"], *[g_o[n] for n in WEIGHTS], *[d_o[n] for n in WEIGHTS],
            *[m_o[n] for n in WEIGHTS], *[v_o[n] for n in WEIGHTS])
```

```python
import functools
import math

import jax
import jax.numpy as jnp
from jax import lax
from jax.experimental import pallas as pl
from jax.experimental.pallas import tpu as pltpu

D_MODEL = 1024
DEPTH = 4
N_A = DEPTH // 2
MEM_HEADS = 4
MEM_DH = D_MODEL // 16
MEM_W = MEM_HEADS * MEM_DH
DN_DK = 128
DN_DV = 128
DN_HEADS = (3 * D_MODEL) // (4 * DN_DV)
DN_QK_W = DN_HEADS * DN_DK
DN_V_W = DN_HEADS * DN_DV
CONV_WIDTH = 4
CHUNK = 64
SWA_DH = 64
SWA_HEADS = (3 * D_MODEL) // (4 * SWA_DH)
SWA_KV_HEADS = 2
SWA_GROUP = SWA_HEADS // SWA_KV_HEADS
SWA_Q_W = SWA_HEADS * SWA_DH
SWA_KV_W = SWA_KV_HEADS * SWA_DH
WINDOW = 128
ROPE_THETA = 10000.0
LN_EPS = 1e-5
NORM_EPS = 1e-6
DN_ALPHA = (2.0 * DEPTH) ** 0.25
A_IN = 2 * DN_QK_W + 2 * DN_V_W + 2 * DN_HEADS + MEM_W
A_IN_PAD = 3456
QKVZ_W = 2 * DN_QK_W + 2 * DN_V_W

ADAM_LR = 0.001
ADAM_B1 = 0.9
ADAM_B2 = 0.999
ADAM_EPS = 1e-08
ADAM_WD = 0.01
ADAM_STEP = 10

N_CHIPS = 4
FLAT_W = 1024
BIG = ("a_w_in", "b_w_in", "w_kv_shared", "mem_w_kv", "w_o", "mlp_w_up", "mlp_w_down")
SMALL = ("a_conv_w", "ln_g", "ln_b", "a_A_log", "a_dt_bias", "a_norm_w", "b_sinks")
REPLICATED = ("a_A_log", "a_dt_bias", "a_norm_w", "b_sinks")
WEIGHTS = ("a_w_in", "a_conv_w", "a_A_log", "a_dt_bias", "a_norm_w", "b_w_in", "b_sinks", "w_kv_shared",
           "mem_w_kv", "w_o", "mlp_w_up", "mlp_w_down", "ln_g", "ln_b")
SHARD_AXIS = {"a_w_in": 2, "a_conv_w": 2, "b_w_in": 1, "w_kv_shared": 0, "mem_w_kv": 1, "w_o": 1,
              "mlp_w_up": 2, "mlp_w_down": 1, "ln_g": 2, "ln_b": 2}
SMALL_ROWS = 32
ROW_ALIGN = 256

MESH = pl.DeviceIdType.MESH
HBM_SPEC = pl.BlockSpec(memory_space=pltpu.HBM)
VMEM_LIMIT = 48 * 1024 * 1024


def _rows_for(shards):
    n_big = sum(math.prod(shards[n].shape) for n in BIG)
    n_small = sum(math.prod(shards[n].shape) for n in SMALL)
    assert n_small <= SMALL_ROWS * FLAT_W
    total = -(-n_big // FLAT_W) + SMALL_ROWS
    total = -(-total // (2 * ROW_ALIGN)) * (2 * ROW_ALIGN)
    return total - SMALL_ROWS, total


def _pack(shards, rb, dtype_big=jnp.float32):
    big = jnp.concatenate([shards[n].reshape(-1).astype(dtype_big) for n in BIG])
    big = jnp.pad(big, (0, rb * FLAT_W - big.shape[0])).reshape(rb, FLAT_W)
    small = jnp.concatenate([shards[n].reshape(-1).astype(jnp.float32) for n in SMALL])
    small = jnp.pad(small, (0, SMALL_ROWS * FLAT_W - small.shape[0])).reshape(SMALL_ROWS, FLAT_W)
    return big, small


def _unpack(big, small, shapes):
    out = {}
    for flat, names in ((big.reshape(-1), BIG), (small.reshape(-1), SMALL)):
        off = 0
        for n in names:
            size = math.prod(shapes[n])
            out[n] = flat[off:off + size].reshape(shapes[n])
            off += size
    return out


def _other_chips(x, y):
    return [(1 - x, y), (x, 1 - y), (1 - x, 1 - y)]


def _gather_weights(big, small):
    def body(big_ref, small_ref, init_big_ref, init_small_ref, obig_ref, osmall_ref,
             send_sems, recv_sems, pass_send_sems, pass_recv_sems):
        del init_big_ref, init_small_ref
        x, y, c = lax.axis_index("x"), lax.axis_index("y"), lax.axis_index("c")
        me = 2 * x + y
        sibling = (x, y, 1 - c)
        pairs = ((big_ref, obig_ref), (small_ref, osmall_ref))
        sends = []
        for j, (px, py) in enumerate(_other_chips(x, y)):
            for i, (src, dst) in enumerate(pairs):
                sends.append(pltpu.make_async_remote_copy(
                    src_ref=src.at[c], dst_ref=dst.at[me, c], send_sem=send_sems.at[2 * j + i],
                    recv_sem=recv_sems.at[2 * j + i], device_id=(px, py, c), device_id_type=MESH))
        for cp in sends:
            cp.start()
        passed = []
        for j, (px, py) in enumerate(_other_chips(x, y)):
            for i, (src, dst) in enumerate(pairs):
                landed = dst.at[2 * px + py, c]
                pltpu.make_async_remote_copy(
                    src_ref=src.at[c], dst_ref=landed, send_sem=send_sems.at[2 * j + i],
                    recv_sem=recv_sems.at[2 * j + i], device_id=(px, py, c), device_id_type=MESH).wait_recv()
                passed.append(pltpu.make_async_remote_copy(
                    src_ref=landed, dst_ref=landed, send_sem=pass_send_sems.at[2 * j + i],
                    recv_sem=pass_recv_sems.at[2 * j + i], device_id=sibling, device_id_type=MESH))
                passed[-1].start()
        for j, (px, py) in enumerate(_other_chips(x, y)):
            for i, (src, dst) in enumerate(pairs):
                other_half = dst.at[2 * px + py, 1 - c]
                pltpu.make_async_remote_copy(
                    src_ref=other_half, dst_ref=other_half, send_sem=pass_send_sems.at[2 * j + i],
                    recv_sem=pass_recv_sems.at[2 * j + i], device_id=sibling, device_id_type=MESH).wait_recv()
        for cp in sends + passed:
            cp.wait_send()

    dma6 = pltpu.SemaphoreType.DMA((6,))
    four = lambda t: jnp.broadcast_to(t[None], (N_CHIPS,) + t.shape)
    return pl.pallas_call(
        body, name="gather_weights",
        out_shape=(jax.ShapeDtypeStruct((N_CHIPS,) + big.shape, big.dtype),
                   jax.ShapeDtypeStruct((N_CHIPS,) + small.shape, small.dtype)),
        in_specs=[HBM_SPEC] * 4, out_specs=(HBM_SPEC, HBM_SPEC), input_output_aliases={2: 0, 3: 1},
        scratch_shapes=[dma6, dma6, dma6, dma6],
    )(big, small, four(big), four(small))


def _scatter_grads(g):
    def body(g_ref, o_ref, send_sems, recv_sems, local_sem):
        x, y, c = lax.axis_index("x"), lax.axis_index("y"), lax.axis_index("c")
        me = 2 * x + y
        local = pltpu.make_async_copy(g_ref.at[me], o_ref.at[me], local_sem)
        local.start()
        sends = []
        for j, (px, py) in enumerate(_other_chips(x, y)):
            sends.append(pltpu.make_async_remote_copy(
                src_ref=g_ref.at[2 * px + py], dst_ref=o_ref.at[me], send_sem=send_sems.at[j], recv_sem=recv_sems.at[j],
                device_id=(px, py, c), device_id_type=MESH))
        for cp in sends:
            cp.start()
        for j, (px, py) in enumerate(_other_chips(x, y)):
            pltpu.make_async_remote_copy(
                src_ref=g_ref.at[me], dst_ref=o_ref.at[2 * px + py], send_sem=send_sems.at[j], recv_sem=recv_sems.at[j],
                device_id=(px, py, c), device_id_type=MESH).wait_recv()
        for cp in sends:
            cp.wait_send()
        local.wait()

    return pl.pallas_call(
        body, name="scatter_grads",
        out_shape=jax.ShapeDtypeStruct(g.shape, g.dtype),
        in_specs=[HBM_SPEC], out_specs=HBM_SPEC,
        scratch_shapes=[pltpu.SemaphoreType.DMA((3,)), pltpu.SemaphoreType.DMA((3,)), pltpu.SemaphoreType.DMA],
    )(g)


def _swap_halves(g):
    def body(g_ref, o_ref, send_sem, recv_sem):
        x, y, c = lax.axis_index("x"), lax.axis_index("y"), lax.axis_index("c")
        cp = pltpu.make_async_remote_copy(src_ref=g_ref.at[1 - c], dst_ref=o_ref, send_sem=send_sem, recv_sem=recv_sem,
                                          device_id=(x, y, 1 - c), device_id_type=MESH)
        cp.start()
        cp.wait()

    return pl.pallas_call(
        body, name="swap_halves",
        out_shape=jax.ShapeDtypeStruct(g.shape[1:], g.dtype),
        in_specs=[HBM_SPEC], out_specs=HBM_SPEC,
        scratch_shapes=[pltpu.SemaphoreType.DMA, pltpu.SemaphoreType.DMA],
    )(g)


def _join_halves(v):
    def body(v_ref, init_ref, o_ref, send_sem, recv_sem):
        del init_ref
        x, y, c = lax.axis_index("x"), lax.axis_index("y"), lax.axis_index("c")
        cp = pltpu.make_async_remote_copy(src_ref=v_ref, dst_ref=o_ref.at[c], send_sem=send_sem, recv_sem=recv_sem,
                                          device_id=(x, y, 1 - c), device_id_type=MESH)
        cp.start()
        cp.wait_send()
        pltpu.make_async_remote_copy(src_ref=v_ref, dst_ref=o_ref.at[1 - c], send_sem=send_sem, recv_sem=recv_sem,
                                     device_id=(x, y, 1 - c), device_id_type=MESH).wait_recv()

    return pl.pallas_call(
        body, name="join_halves",
        out_shape=jax.ShapeDtypeStruct((2,) + v.shape, v.dtype),
        in_specs=[HBM_SPEC, HBM_SPEC], out_specs=HBM_SPEC, input_output_aliases={1: 0},
        scratch_shapes=[pltpu.SemaphoreType.DMA, pltpu.SemaphoreType.DMA],
    )(v, jnp.stack([v, v]))


def _add_pairs(g, theirs, half):
    _, n, rows, width = g.shape
    assert rows % ROW_ALIGN == 0, rows

    def body(half_ref, g_ref, t_ref, o_ref):
        o_ref[...] = (g_ref[...].astype(jnp.float32) + t_ref[...].astype(jnp.float32)).astype(o_ref.dtype)

    blk = pl.BlockSpec((None, ROW_ALIGN, width), lambda p, i, h: (p, i, 0))
    grid_spec = pltpu.PrefetchScalarGridSpec(
        num_scalar_prefetch=1, grid=(n, rows // ROW_ALIGN),
        in_specs=[pl.BlockSpec((None, None, ROW_ALIGN, width), lambda p, i, h: (h[0], p, i, 0)), blk], out_specs=blk)
    return pl.pallas_call(
        body, name="add_pairs", grid_spec=grid_spec, out_shape=jax.ShapeDtypeStruct(theirs.shape, g.dtype),
        compiler_params=pltpu.CompilerParams(dimension_semantics=("parallel", "parallel")),
    )(half, g, theirs)


def _sum_chips(parts):
    n, rows, width = parts.shape
    assert rows % ROW_ALIGN == 0, rows

    def body(p_ref, o_ref):
        p = [p_ref[q].astype(jnp.float32) for q in range(n)]
        o_ref[...] = (p[0] + p[1]) + (p[2] + p[3])

    return pl.pallas_call(
        body, name="sum_chips", grid=(rows // ROW_ALIGN,),
        out_shape=jax.ShapeDtypeStruct((rows, width), jnp.float32),
        in_specs=[pl.BlockSpec((n, ROW_ALIGN, width), lambda i: (0, i, 0))],
        out_specs=pl.BlockSpec((ROW_ALIGN, width), lambda i: (i, 0)),
        compiler_params=pltpu.CompilerParams(dimension_semantics=("parallel",), vmem_limit_bytes=VMEM_LIMIT),
    )(parts)


def _adamw(g, w, m, v):
    rows, width = w.shape
    blk = ROW_ALIGN // 2

    def body(g_ref, w_ref, m_ref, v_ref, d_out, m_out, v_out):
        g = g_ref[...]
        m_new = ADAM_B1 * m_ref[...] + (1.0 - ADAM_B1) * g
        v_new = ADAM_B2 * v_ref[...] + (1.0 - ADAM_B2) * jnp.square(g)
        m_hat = m_new / (1.0 - ADAM_B1 ** ADAM_STEP)
        v_hat = v_new / (1.0 - ADAM_B2 ** ADAM_STEP)
        d_out[...] = -ADAM_LR * (m_hat / (jnp.sqrt(v_hat) + ADAM_EPS) + ADAM_WD * w_ref[...])
        m_out[...] = m_new
        v_out[...] = v_new

    spec = pl.BlockSpec((blk, width), lambda i: (i, 0))
    shape = jax.ShapeDtypeStruct((rows, width), jnp.float32)
    return pl.pallas_call(
        body, name="adamw", grid=(rows // blk,),
        out_shape=(shape,) * 3, in_specs=[spec] * 4, out_specs=(spec,) * 3,
        compiler_params=pltpu.CompilerParams(dimension_semantics=("parallel",), vmem_limit_bytes=VMEM_LIMIT),
    )(g, w, m, v)


def _tile(dim, pref):
    if dim <= pref:
        return dim
    for t in range(pref - pref % 128, 0, -128):
        if dim % t == 0:
            return t
    raise ValueError(f"no 128-aligned tile for {dim}")


def _matmul(a, b, *, ta=False, tb=False, name, epilogue=None, extra=None, out_dtype=jnp.float32):
    (k_a, m) = a.shape if ta else a.shape[::-1]
    (k_b, n) = b.shape[::-1] if tb else b.shape
    assert k_a == k_b, (a.shape, b.shape, ta, tb)
    k = k_a
    tk = _tile(k, 1152)
    nk = k // tk
    if ta:
        tm, tn = _tile(m, 1024), _tile(n, 2048 if m <= 1024 else 1024)
    else:
        tm, tn = _tile(m, 2048), _tile(n, 1152)
    a_spec = pl.BlockSpec((tk, tm), lambda i, j, l: (l, i)) if ta else pl.BlockSpec((tm, tk), lambda i, j, l: (i, l))
    b_spec = pl.BlockSpec((tn, tk), lambda i, j, l: (j, l)) if tb else pl.BlockSpec((tk, tn), lambda i, j, l: (l, j))
    o_spec = pl.BlockSpec((tm, tn), lambda i, j, l: (i, j))
    dims = (((0 if ta else 1,), (1 if tb else 0,)), ((), ()))
    has_extra = epilogue == "relu2_grad"
    assert has_extra == (extra is not None)

    def body(*refs):
        a_ref, b_ref = refs[:2]
        outs = refs[2 + has_extra:2 + has_extra + (2 if epilogue == "relu2" else 1)]
        l = pl.program_id(2)
        part = lax.dot_general(a_ref[...].astype(jnp.bfloat16), b_ref[...].astype(jnp.bfloat16), dims,
                               preferred_element_type=jnp.float32)

        def finish(acc):
            if epilogue is None:
                outs[0][...] = acc.astype(out_dtype)
            elif epilogue == "relu2":
                outs[0][...] = acc.astype(jnp.bfloat16)
                outs[1][...] = jnp.square(jnp.maximum(acc, 0.0)).astype(jnp.bfloat16)
            else:
                outs[0][...] = (acc * (2.0 * jnp.maximum(refs[2][...].astype(jnp.float32), 0.0))).astype(out_dtype)

        if nk == 1:
            finish(part)
            return
        acc_ref = refs[-1]

        @pl.when(l == 0)
        def _():
            acc_ref[...] = part

        @pl.when((l > 0) & (l < nk - 1))
        def _():
            acc_ref[...] += part

        @pl.when(l == nk - 1)
        def _():
            finish(acc_ref[...] + part)

    if epilogue == "relu2":
        out_shape = (jax.ShapeDtypeStruct((m, n), jnp.bfloat16),) * 2
        out_specs = (o_spec, o_spec)
    else:
        out_shape = jax.ShapeDtypeStruct((m, n), out_dtype)
        out_specs = o_spec
    return pl.pallas_call(
        body, name=name, grid=(m // tm, n // tn, nk), out_shape=out_shape,
        in_specs=[a_spec, b_spec] + ([o_spec] if has_extra else []), out_specs=out_specs,
        scratch_shapes=[pltpu.VMEM((tm, tn), jnp.float32)] if nk > 1 else [],
        compiler_params=pltpu.CompilerParams(dimension_semantics=("parallel", "parallel", "arbitrary"),
                                             vmem_limit_bytes=VMEM_LIMIT),
    )(*((a, b) + ((extra,) if has_extra else ())))


def _lo(x):
    return lax.stop_gradient(x.astype(jnp.bfloat16))


@jax.custom_vjp
def _linear(x, x_lo, w, slot):
    del x, slot
    return _matmul(x_lo, w, name="linear_fwd")


def _linear_fwd(x, x_lo, w, slot):
    del x, slot
    return _matmul(x_lo, w, name="linear_fwd"), (x_lo, w)


def _linear_bwd(res, dy):
    x_lo, w = res
    dy = dy.astype(jnp.bfloat16)
    dx = _matmul(dy, w, tb=True, name="linear_dx")
    dw = _matmul(x_lo, dy, ta=True, name="linear_dw")
    return dx, jnp.zeros_like(x_lo), jnp.zeros_like(w), dw


_linear.defvjp(_linear_fwd, _linear_bwd)


@jax.custom_vjp
def _mlp(h, h_lo, w_up, w_down, slot_up, slot_down):
    return _mlp_fwd(h, h_lo, w_up, w_down, slot_up, slot_down)[0]


def _mlp_fwd(h, h_lo, w_up, w_down, slot_up, slot_down):
    del h, slot_up, slot_down
    up, act = _matmul(h_lo, w_up, name="mlp_up", epilogue="relu2")
    return _matmul(act, w_down, name="mlp_down"), (h_lo, up, act, w_up, w_down)


def _mlp_bwd(res, dy):
    h_lo, up, act, w_up, w_down = res
    dy = dy.astype(jnp.bfloat16)
    d_up = _matmul(dy, w_down, tb=True, name="mlp_d_up", epilogue="relu2_grad", extra=up, out_dtype=jnp.bfloat16)
    dw_down = _matmul(act, dy, ta=True, name="mlp_dw_down")
    dw_up = _matmul(h_lo, d_up, ta=True, name="mlp_dw_up")
    dh = _matmul(d_up, w_up, tb=True, name="mlp_dh")
    return dh, jnp.zeros_like(h_lo), jnp.zeros_like(w_up), jnp.zeros_like(w_down), dw_up, dw_down


_mlp.defvjp(_mlp_fwd, _mlp_bwd)


LN_ROWS = 1024


def _ln_call(h, mix, g, b):
    s, d = h.shape
    tok = pl.BlockSpec((LN_ROWS, d), lambda i: (i, 0))
    vec = pl.BlockSpec((1, d), lambda i: (0, 0))
    stat = pl.BlockSpec((LN_ROWS, 1), lambda i: (i, 0))

    def body(h_ref, mix_ref, g_ref, b_ref, y_ref, ylo_ref, xhat_ref, rstd_ref):
        z = DN_ALPHA * h_ref[...] + mix_ref[...]
        mu = jnp.mean(z, axis=-1, keepdims=True)
        zc = z - mu
        rstd = lax.rsqrt(jnp.mean(jnp.square(zc), axis=-1, keepdims=True) + LN_EPS)
        xhat = zc * rstd
        y = xhat * g_ref[...] + b_ref[...]
        y_ref[...] = y
        ylo_ref[...] = y.astype(ylo_ref.dtype)
        xhat_ref[...] = xhat
        rstd_ref[...] = rstd

    sd = jax.ShapeDtypeStruct
    return pl.pallas_call(
        body, name="ln_fwd", grid=(s // LN_ROWS,),
        out_shape=(sd((s, d), jnp.float32), sd((s, d), jnp.bfloat16), sd((s, d), jnp.float32), sd((s, 1), jnp.float32)),
        in_specs=[tok, tok, vec, vec], out_specs=(tok, tok, tok, stat),
        compiler_params=pltpu.CompilerParams(dimension_semantics=("parallel",), vmem_limit_bytes=VMEM_LIMIT),
    )(h, mix, g, b)


def _ln_grad_call(dy, xhat, rstd, g):
    s, d = dy.shape
    tok = pl.BlockSpec((LN_ROWS, d), lambda i: (i, 0))
    vec = pl.BlockSpec((1, d), lambda i: (0, 0))
    stat = pl.BlockSpec((LN_ROWS, 1), lambda i: (i, 0))

    def body(dy_ref, xhat_ref, rstd_ref, g_ref, dz_ref, dg_ref, db_ref):
        @pl.when(pl.program_id(0) == 0)
        def _():
            dg_ref[...] = jnp.zeros_like(dg_ref)
            db_ref[...] = jnp.zeros_like(db_ref)

        dy, xhat = dy_ref[...], xhat_ref[...]
        dyg = dy * g_ref[...]
        m1 = jnp.mean(dyg, axis=-1, keepdims=True)
        m2 = jnp.mean(dyg * xhat, axis=-1, keepdims=True)
        dz_ref[...] = rstd_ref[...] * (dyg - m1 - xhat * m2)
        dg_ref[...] += jnp.sum(dy * xhat, axis=0, keepdims=True)
        db_ref[...] += jnp.sum(dy, axis=0, keepdims=True)

    sd = jax.ShapeDtypeStruct
    return pl.pallas_call(
        body, name="ln_bwd", grid=(s // LN_ROWS,),
        out_shape=(sd((s, d), jnp.float32), sd((1, d), jnp.float32), sd((1, d), jnp.float32)),
        in_specs=[tok, tok, stat, vec], out_specs=(tok, vec, vec),
        compiler_params=pltpu.CompilerParams(dimension_semantics=("arbitrary",), vmem_limit_bytes=VMEM_LIMIT),
    )(dy, xhat, rstd, g)


@jax.custom_vjp
def _ln_res(h, mix, g, b):
    return _ln_call(h, mix, g, b)[:2]


def _ln_res_fwd(h, mix, g, b):
    y, y_lo, xhat, rstd = _ln_call(h, mix, g, b)
    return (y, y_lo), (xhat, rstd, g)


def _ln_res_bwd(res, cts):
    xhat, rstd, g = res
    dz, dg, db = _ln_grad_call(cts[0], xhat, rstd, g)
    return DN_ALPHA * dz, dz, dg, db


_ln_res.defvjp(_ln_res_fwd, _ln_res_bwd)


MXU_DTYPE = jnp.bfloat16
DN_CB = 16
DN_GROUP = 8
DN_SCAN_CB = 4
DN_SCALE = DN_DK ** -0.5


def _dot(a, b, ca=1, cb=0):
    return lax.dot_general(a.astype(MXU_DTYPE), b.astype(MXU_DTYPE), (((ca,), (cb,)), ((), ())),
                           preferred_element_type=jnp.float32)


def _chunk_masks():
    row = lax.broadcasted_iota(jnp.int32, (CHUNK, CHUNK), 0)
    col = lax.broadcasted_iota(jnp.int32, (CHUNK, CHUNK), 1)
    return row >= col, row > col, row == col


def _to_col(row_vec):
    _, _, eye = _chunk_masks()
    return jnp.sum(jnp.where(eye, jnp.broadcast_to(row_vec, (CHUNK, CHUNK)), 0.0), axis=1, keepdims=True)


def _to_row(col_vec):
    _, _, eye = _chunk_masks()
    return jnp.sum(jnp.where(eye, jnp.broadcast_to(col_vec, (CHUNK, CHUNK)), 0.0), axis=0, keepdims=True)


def _last_row(col_vec):
    last = lax.broadcasted_iota(jnp.int32, (CHUNK, 1), 0) == CHUNK - 1
    return jnp.sum(jnp.where(last, col_vec, 0.0), axis=0, keepdims=True), last


def _chunk_terms(q, k, beta, gcc, gcr):
    incl, strict, _ = _chunk_masks()
    decay = jnp.where(incl, jnp.exp(jnp.minimum(gcc - gcr, 0.0)), 0.0)
    kb = k * beta
    lmat = jnp.where(strict, _dot(kb, k, 1, 1) * decay, 0.0)
    intra = jnp.where(incl, _dot(q, k, 1, 1) * decay, 0.0)
    return decay, kb, lmat, intra


def _dot3(a, b, ca=1, cb=0):
    if MXU_DTYPE == jnp.float32:
        return _dot(a, b, ca, cb)
    a_hi, b_hi = a.astype(MXU_DTYPE), b.astype(MXU_DTYPE)
    a_lo = (a - a_hi.astype(jnp.float32)).astype(MXU_DTYPE)
    b_lo = (b - b_hi.astype(jnp.float32)).astype(MXU_DTYPE)
    return _dot(a_hi, b_hi, ca, cb) + (_dot(a_hi, b_lo, ca, cb) + _dot(a_lo, b_hi, ca, cb))


def _unit_lower_inverse(lmats):
    _, _, eye = _chunk_masks()
    ident = jnp.where(eye, 1.0, 0.0)
    ts = [ident - m for m in lmats]
    ps = [_dot(m, m) for m in lmats]
    for _ in range(4):
        ts = [t + _dot(t, p) for t, p in zip(ts, ps)]
        ps = [_dot(p, p) for p in ps]
    ts = [t + _dot(t, p) for t, p in zip(ts, ps)]
    resids = [(t - ident) + _dot3(m, t) for m, t in zip(lmats, ts)]
    return [t - _dot(t, r) for t, r in zip(ts, resids)]


def _dn_specs(n_chunks):
    tok = pl.BlockSpec((DN_CB * CHUNK, DN_DK), lambda h, n: (n, h))
    rowv = pl.BlockSpec((None, DN_CB, CHUNK), lambda h, n: (h, n, 0))
    sq = pl.BlockSpec((None, DN_CB, CHUNK, CHUNK), lambda h, n: (h, n, 0, 0))
    lane = pl.BlockSpec((None, DN_CB, 1, DN_DV), lambda h, n: (h, n, 0, 0))
    planes = [pl.BlockSpec((None, DN_CB * CHUNK, DN_DK), functools.partial(lambda h, n, p: (p, n, h), p=p))
              for p in range(3)]
    return tok, rowv, sq, lane, planes


def _dn_prep(qkv, beta, gc):
    s = qkv.shape[1]
    n_chunks = s // CHUNK
    tok, rowv, sq, lane, planes = _dn_specs(n_chunks)
    tok_shape = qkv.shape[1:]

    def body(q_ref, k_ref, v_ref, beta_ref, gc_ref, u_ref, w_ref, qd_ref, kd_ref, intra_ref, t_ref, cd_ref):
        for c0 in range(0, DN_CB, DN_GROUP):
            chunks = range(c0, c0 + DN_GROUP)
            rhs, lmats = [], []
            for c in chunks:
                rows = pl.ds(c * CHUNK, CHUNK)
                q_c, k_c, v_c = q_ref[rows, :] * DN_SCALE, k_ref[rows, :], v_ref[rows, :]
                gcr_c = gc_ref[pl.ds(c, 1), :]
                beta_c, gcc_c = _to_col(beta_ref[pl.ds(c, 1), :]), _to_col(gcr_c)
                _, kb, lmat, intra = _chunk_terms(q_c, k_c, beta_c, gcc_c, gcr_c)
                eg = jnp.exp(gcc_c)
                g_last, _ = _last_row(gcc_c)
                qd_ref[rows, :] = (q_c * eg).astype(qd_ref.dtype)
                kd_ref[rows, :] = (k_c * jnp.exp(g_last - gcc_c)).astype(kd_ref.dtype)
                intra_ref[c] = intra.astype(intra_ref.dtype)
                cd_ref[c] = jnp.broadcast_to(jnp.exp(g_last), (1, DN_DV))
                rhs.append(jnp.concatenate([v_c * beta_c, kb * eg], axis=1))
                lmats.append(lmat)
            ts = _unit_lower_inverse(lmats)
            sols = [_dot3(t, r) for t, r in zip(ts, rhs)]
            for c, t, sol in zip(chunks, ts, sols):
                rows = pl.ds(c * CHUNK, CHUNK)
                t_ref[c] = t
                u_ref[rows, :] = sol[:, :DN_DV]
                w_ref[rows, :] = sol[:, DN_DV:].astype(w_ref.dtype)

    f32, mx = jnp.float32, MXU_DTYPE
    sd = jax.ShapeDtypeStruct
    return pl.pallas_call(
        body, name="dn_prep", grid=(DN_HEADS, n_chunks // DN_CB),
        out_shape=(sd(tok_shape, f32), sd(tok_shape, mx), sd(tok_shape, mx), sd(tok_shape, mx),
                   sd((DN_HEADS, n_chunks, CHUNK, CHUNK), mx), sd((DN_HEADS, n_chunks, CHUNK, CHUNK), f32),
                   sd((DN_HEADS, n_chunks, 1, DN_DV), f32)),
        in_specs=planes + [rowv, rowv], out_specs=(tok, tok, tok, tok, sq, sq, lane),
        compiler_params=pltpu.CompilerParams(dimension_semantics=("parallel", "parallel")),
    )(qkv, qkv, qkv, beta, gc)


def _dn_scan(u, w, qd, kd, intra, cd):
    s, width = u.shape
    n_chunks = s // CHUNK
    cb = DN_SCAN_CB
    tok = pl.BlockSpec((cb * CHUNK, width), lambda n: (n, 0))
    sq = pl.BlockSpec((DN_HEADS, cb, CHUNK, CHUNK), lambda n: (0, n, 0, 0))
    lane = pl.BlockSpec((DN_HEADS, cb, 1, DN_DV), lambda n: (0, n, 0, 0))
    st = pl.BlockSpec((DN_HEADS, cb, DN_DK, DN_DV), lambda n: (0, n, 0, 0))

    def body(u_ref, w_ref, qd_ref, kd_ref, intra_ref, cd_ref, o_ref, vn_ref, st_ref, state):
        @pl.when(pl.program_id(0) == 0)
        def _():
            state[...] = jnp.zeros_like(state)

        heads = range(DN_HEADS)
        cols = [pl.ds(h * DN_DK, DN_DK) for h in heads]
        s_f = [state[h] for h in heads]
        for c in range(cb):
            rows = pl.ds(c * CHUNK, CHUNK)
            s_mx = [s.astype(MXU_DTYPE) for s in s_f]
            for h in heads:
                st_ref[h, c] = s_mx[h]
            ws = [_dot(w_ref[rows, cols[h]], s_mx[h]) for h in heads]
            qs = [_dot(qd_ref[rows, cols[h]], s_mx[h]) for h in heads]
            v_new = [(u_ref[rows, cols[h]] - ws[h]).astype(MXU_DTYPE) for h in heads]
            inner = [_dot(intra_ref[h, c], v_new[h]) for h in heads]
            outer = [_dot(kd_ref[rows, cols[h]], v_new[h], 0, 0) for h in heads]
            for h in heads:
                vn_ref[rows, cols[h]] = v_new[h]
                o_ref[rows, cols[h]] = qs[h] + inner[h]
            s_f = [s_f[h] * cd_ref[h, c] + outer[h] for h in heads]
        for h in heads:
            state[h] = s_f[h]

    sd = jax.ShapeDtypeStruct
    return pl.pallas_call(
        body, name="dn_scan", grid=(n_chunks // cb,),
        out_shape=(sd(u.shape, jnp.float32), sd(u.shape, MXU_DTYPE),
                   sd((DN_HEADS, n_chunks, DN_DK, DN_DV), MXU_DTYPE)),
        in_specs=[tok, tok, tok, tok, sq, lane], out_specs=(tok, tok, st),
        scratch_shapes=[pltpu.VMEM((DN_HEADS, DN_DK, DN_DV), jnp.float32)],
        compiler_params=pltpu.CompilerParams(dimension_semantics=("arbitrary",)),
    )(u, w, qd, kd, intra, cd)


def _dn_bwd_scan(do, w, qd, kd, intra, cd, vn, st):
    s, width = do.shape
    n_chunks = s // CHUNK
    cb = DN_SCAN_CB
    last = n_chunks // cb - 1
    tok = pl.BlockSpec((cb * CHUNK, width), lambda n: (last - n, 0))
    sq = pl.BlockSpec((DN_HEADS, cb, CHUNK, CHUNK), lambda n: (0, last - n, 0, 0))
    lane = pl.BlockSpec((DN_HEADS, cb, 1, DN_DV), lambda n: (0, last - n, 0, 0))
    stt = pl.BlockSpec((DN_HEADS, cb, DN_DK, DN_DV), lambda n: (0, last - n, 0, 0))

    def body(do_ref, w_ref, qd_ref, kd_ref, intra_ref, cd_ref, vn_ref, st_ref,
             du_ref, dw_ref, dqd_ref, dkd_ref, dintra_ref, dgl_ref, dstate):
        @pl.when(pl.program_id(0) == 0)
        def _():
            dstate[...] = jnp.zeros_like(dstate)

        heads = range(DN_HEADS)
        cols = [pl.ds(h * DN_DK, DN_DK) for h in heads]
        ds_f = [dstate[h] for h in heads]
        for c in reversed(range(cb)):
            rows = pl.ds(c * CHUNK, CHUNK)
            ds_mx = [d.astype(MXU_DTYPE) for d in ds_f]
            do_h = [do_ref[rows, cols[h]].astype(MXU_DTYPE) for h in heads]
            dv_a = [_dot(intra_ref[h, c], do_h[h], 0, 0) for h in heads]
            dv_b = [_dot(kd_ref[rows, cols[h]], ds_mx[h]) for h in heads]
            d_intra = [_dot(do_h[h], vn_ref[rows, cols[h]], 1, 1) for h in heads]
            d_qd = [_dot(do_h[h], st_ref[h, c], 1, 1) for h in heads]
            d_kd = [_dot(vn_ref[rows, cols[h]], ds_mx[h], 1, 1) for h in heads]
            ds_q = [_dot(qd_ref[rows, cols[h]], do_h[h], 0, 0) for h in heads]
            dv_new = [dv_a[h] + dv_b[h] for h in heads]
            dv_mx = [d.astype(MXU_DTYPE) for d in dv_new]
            d_w = [_dot(dv_mx[h], st_ref[h, c], 1, 1) for h in heads]
            ds_w = [_dot(w_ref[rows, cols[h]], dv_mx[h], 0, 0) for h in heads]
            ds_next = []
            for h in heads:
                du_ref[rows, cols[h]] = dv_new[h]
                dintra_ref[h, c] = d_intra[h]
                dqd_ref[rows, cols[h]] = d_qd[h]
                dkd_ref[rows, cols[h]] = d_kd[h]
                dw_ref[rows, cols[h]] = -d_w[h]
                cd_h = cd_ref[h, c]
                dcd = jnp.sum(jnp.sum(st_ref[h, c].astype(jnp.float32) * ds_f[h], axis=1, keepdims=True), axis=0,
                              keepdims=True)
                dgl_ref[h, c] = dcd * cd_h
                ds_next.append(ds_q[h] + ds_f[h] * cd_h - ds_w[h])
            ds_f = ds_next
        for h in heads:
            dstate[h] = ds_f[h]

    sd = jax.ShapeDtypeStruct
    f32 = jnp.float32
    return pl.pallas_call(
        body, name="dn_bwd_scan", grid=(n_chunks // cb,),
        out_shape=(sd(do.shape, f32), sd(do.shape, f32), sd(do.shape, f32), sd(do.shape, f32),
                   sd((DN_HEADS, n_chunks, CHUNK, CHUNK), f32), sd((DN_HEADS, n_chunks, 1, DN_DV), f32)),
        in_specs=[tok, tok, tok, tok, sq, lane, tok, stt], out_specs=(tok, tok, tok, tok, sq, lane),
        scratch_shapes=[pltpu.VMEM((DN_HEADS, DN_DK, DN_DV), f32)],
        compiler_params=pltpu.CompilerParams(dimension_semantics=("arbitrary",)),
    )(do, w, qd, kd, intra, cd, vn, st)


def _dn_bwd_chunks(qkv, beta, gc, t, u, w, du, dw, dqd, dkd, dintra, dgl):
    s = qkv.shape[1]
    n_chunks = s // CHUNK
    tok, rowv, sq, lane, planes = _dn_specs(n_chunks)
    all_planes = pl.BlockSpec((3, DN_CB * CHUNK, DN_DK), lambda h, n: (0, n, h))

    def body(q_ref, k_ref, v_ref, beta_ref, gc_ref, t_ref, u_ref, w_ref, du_ref, dw_ref, dqd_ref, dkd_ref,
             dintra_ref, dgl_ref, dqkv_ref, dbeta_ref, dgc_ref):
        incl, strict, _ = _chunk_masks()

        def first(c):
            rows = pl.ds(c * CHUNK, CHUNK)
            q_c, k_c = q_ref[rows, :] * DN_SCALE, k_ref[rows, :]
            gcr_c = gc_ref[pl.ds(c, 1), :]
            beta_c, gcc_c = _to_col(beta_ref[pl.ds(c, 1), :]), _to_col(gcr_c)
            decay, kb, lmat, intra = _chunk_terms(q_c, k_c, beta_c, gcc_c, gcr_c)
            d_sol = jnp.concatenate([du_ref[rows, :], dw_ref[rows, :]], axis=1)
            d_rhs = _dot3(t_ref[c], d_sol, 0, 0)
            return dict(rows=rows, q=q_c, k=k_c, beta=beta_c, gcc=gcc_c, decay=decay, kb=kb, lmat=lmat, intra=intra,
                        d_rhs=d_rhs)

        def second(c, e):
            sol = jnp.concatenate([u_ref[e["rows"], :], w_ref[e["rows"], :].astype(jnp.float32)], axis=1)
            e["d_l"] = jnp.where(strict, -_dot(e["d_rhs"], sol, 1, 1), 0.0)
            e["d_intra"] = jnp.where(incl, dintra_ref[c], 0.0)
            d_qk = e["d_intra"] * e["decay"]
            e["dq"] = _dot(d_qk, e["k"])
            e["dk"] = _dot(d_qk, e["q"], 0, 0)

        def third(e):
            d_a = e["d_l"] * e["decay"]
            e["dkb"] = _dot(d_a, e["k"])
            e["dk"] = e["dk"] + _dot(d_a, e["kb"], 0, 0)

        def last(c, e):
            rows, q_c, k_c, beta_c, gcc_c = e["rows"], e["q"], e["k"], e["beta"], e["gcc"]
            v_c = v_ref[rows, :]
            eg = jnp.exp(gcc_c)
            g_last, is_last = _last_row(gcc_c)
            e_rev = jnp.exp(g_last - gcc_c)
            d_rhs_u, d_rhs_w = e["d_rhs"][:, :DN_DV], e["d_rhs"][:, DN_DV:]
            dqkv_ref[2, rows, :] = d_rhs_u * beta_c
            dbeta = jnp.sum(d_rhs_u * v_c, axis=1, keepdims=True)
            dkb = e["dkb"] + d_rhs_w * eg
            dgc = jnp.sum(d_rhs_w * e["kb"] * eg, axis=1, keepdims=True)
            m1 = e["d_l"] * e["lmat"]
            dgc = dgc + jnp.sum(m1, axis=1, keepdims=True)
            dgr = -jnp.sum(m1, axis=0, keepdims=True)
            m2 = e["d_intra"] * e["intra"]
            dgc = dgc + jnp.sum(m2, axis=1, keepdims=True)
            dgr = dgr - jnp.sum(m2, axis=0, keepdims=True)
            dqd = dqd_ref[rows, :]
            dq = e["dq"] + dqd * eg
            dgc = dgc + jnp.sum(dqd * q_c * eg, axis=1, keepdims=True)
            dkd = dkd_ref[rows, :]
            dk = e["dk"] + dkd * e_rev
            tk = jnp.sum(dkd * k_c * e_rev, axis=1, keepdims=True)
            dgc = dgc - tk
            d_last = dgl_ref[c][:, :1] + jnp.sum(tk, axis=0, keepdims=True)
            dgc = dgc + jnp.where(is_last, d_last, 0.0)
            dk = dk + dkb * beta_c
            dbeta = dbeta + jnp.sum(dkb * k_c, axis=1, keepdims=True)
            dqkv_ref[0, rows, :] = dq * DN_SCALE
            dqkv_ref[1, rows, :] = dk
            dbeta_ref[pl.ds(c, 1), :] = _to_row(dbeta)
            dgc_ref[pl.ds(c, 1), :] = _to_row(dgc) + dgr

        for c0 in range(0, DN_CB, DN_GROUP):
            chunks = range(c0, c0 + DN_GROUP)
            env = [first(c) for c in chunks]
            for c, e in zip(chunks, env):
                second(c, e)
            for e in env:
                third(e)
            for c, e in zip(chunks, env):
                last(c, e)

    sd = jax.ShapeDtypeStruct
    f32 = jnp.float32
    return pl.pallas_call(
        body, name="dn_bwd_chunks", grid=(DN_HEADS, n_chunks // DN_CB),
        out_shape=(sd(qkv.shape, f32), sd(beta.shape, f32), sd(gc.shape, f32)),
        in_specs=planes + [rowv, rowv, sq, tok, tok, tok, tok, tok, tok, sq, lane],
        out_specs=(all_planes, rowv, rowv),
        compiler_params=pltpu.CompilerParams(dimension_semantics=("parallel", "parallel")),
    )(qkv, qkv, qkv, beta, gc, t, u, w, du, dw, dqd, dkd, dintra, dgl)


@jax.custom_vjp
def _delta_rule_op(qkv, beta, gc):
    return _delta_rule_fwd(qkv, beta, gc)[0]


def _delta_rule_fwd(qkv, beta, gc):
    u, w, qd, kd, intra, t, cd = _dn_prep(qkv, beta, gc)
    out, vn, st = _dn_scan(u, w, qd, kd, intra, cd)
    return out, (qkv, beta, gc, u, w, qd, kd, intra, t, cd, vn, st)


def _delta_rule_bwd(res, do):
    qkv, beta, gc, u, w, qd, kd, intra, t, cd, vn, st = res
    du, dw, dqd, dkd, dintra, dgl = _dn_bwd_scan(do, w, qd, kd, intra, cd, vn, st)
    return _dn_bwd_chunks(qkv, beta, gc, t, u, w, du, dw, dqd, dkd, dintra, dgl)


_delta_rule_op.defvjp(_delta_rule_fwd, _delta_rule_bwd)


def _gated_delta_rule(qkv, g, beta):
    s, h = g.shape
    n_chunks = s // CHUNK
    gc = jnp.cumsum(g.T.reshape(h, n_chunks, CHUNK), axis=-1)
    return _delta_rule_op(qkv, beta.T.reshape(h, n_chunks, CHUNK), gc)


PRE_ROWS = 1024
HALO = 8
PRE_W = DN_QK_W


def _shift_rows(xs, k):
    return pltpu.roll(xs, k, 0)[HALO:]


def _conv_silu(x_ref, halo_ref, w_ref, first_block):
    halo = jnp.where(first_block, 0.0, halo_ref[...])
    xs = jnp.concatenate([halo, x_ref[...]], axis=0)
    taps = [_shift_rows(xs, CONV_WIDTH - 1 - j) for j in range(CONV_WIDTH - 1)] + [x_ref[...]]
    conv = sum(w_ref[pl.ds(j, 1), :] * taps[j] for j in range(CONV_WIDTH))
    return conv, jax.nn.sigmoid(conv), taps


def _pre_specs():
    blk = pl.BlockSpec((PRE_ROWS, PRE_W), lambda j, i: (i, j))
    prev = pl.BlockSpec((HALO, PRE_W), lambda j, i: (jnp.maximum(i * (PRE_ROWS // HALO) - 1, 0), j))
    wts = pl.BlockSpec((CONV_WIDTH, PRE_W), lambda j, i: (0, j))
    plane = pl.BlockSpec((None, PRE_ROWS, PRE_W), lambda j, i: (j, i, 0))
    return blk, prev, wts, plane


def _pre_fwd_call(x, conv_w):
    s = x.shape[0]
    blk, prev, wts, plane = _pre_specs()

    def body(x_ref, halo_ref, w_ref, o_ref):
        conv, sig, _ = _conv_silu(x_ref, halo_ref, w_ref, pl.program_id(1) == 0)
        act = conv * sig
        is_v = pl.program_id(0) == 2
        for h in range(DN_HEADS):
            cols = slice(h * DN_DK, (h + 1) * DN_DK)
            a_h = act[:, cols]
            r = lax.rsqrt(jnp.sum(a_h * a_h, axis=-1, keepdims=True) + NORM_EPS)
            o_ref[:, cols] = a_h * jnp.where(is_v, 1.0, r)

    return pl.pallas_call(
        body, name="pre_fwd", grid=(3, s // PRE_ROWS),
        out_shape=jax.ShapeDtypeStruct((3, s, PRE_W), jnp.float32),
        in_specs=[blk, prev, wts], out_specs=plane,
        compiler_params=pltpu.CompilerParams(dimension_semantics=("parallel", "parallel")),
    )(x, x, conv_w)


def _pre_bwd_act_call(x, conv_w, d_out):
    s = x.shape[0]
    blk, prev, wts, plane = _pre_specs()

    def body(x_ref, halo_ref, w_ref, do_ref, dc_ref):
        conv, sig, _ = _conv_silu(x_ref, halo_ref, w_ref, pl.program_id(1) == 0)
        act = conv * sig
        d_silu = sig * (1.0 + conv * (1.0 - sig))
        is_v = pl.program_id(0) == 2
        for h in range(DN_HEADS):
            cols = slice(h * DN_DK, (h + 1) * DN_DK)
            a_h, do_h = act[:, cols], do_ref[:, cols]
            r = lax.rsqrt(jnp.sum(a_h * a_h, axis=-1, keepdims=True) + NORM_EPS)
            n_h = a_h * r
            d_norm = r * (do_h - n_h * jnp.sum(do_h * n_h, axis=-1, keepdims=True))
            dc_ref[:, cols] = jnp.where(is_v, do_h, d_norm) * d_silu[:, cols]

    return pl.pallas_call(
        body, name="pre_bwd_act", grid=(3, s // PRE_ROWS),
        out_shape=jax.ShapeDtypeStruct(x.shape, jnp.float32),
        in_specs=[blk, prev, wts, plane], out_specs=blk,
        compiler_params=pltpu.CompilerParams(dimension_semantics=("parallel", "parallel")),
    )(x, x, conv_w, d_out)


def _pre_bwd_conv_call(x, conv_w, dc):
    s = x.shape[0]
    n_blocks = s // PRE_ROWS
    blk, prev, wts, _ = _pre_specs()
    nxt = pl.BlockSpec((HALO, PRE_W), lambda j, i: (jnp.minimum((i + 1) * (PRE_ROWS // HALO), s // HALO - 1), j))

    def body(x_ref, halo_ref, w_ref, dc_ref, dcn_ref, dx_ref, dw_ref):
        i = pl.program_id(1)

        @pl.when(i == 0)
        def _():
            dw_ref[...] = jnp.zeros_like(dw_ref)

        dcv = dc_ref[...]
        ahead = jnp.concatenate([dcv, jnp.where(i == n_blocks - 1, 0.0, dcn_ref[...])], axis=0)
        dx = w_ref[pl.ds(CONV_WIDTH - 1, 1), :] * dcv
        for j in range(CONV_WIDTH - 1):
            k = CONV_WIDTH - 1 - j
            dx = dx + w_ref[pl.ds(j, 1), :] * pltpu.roll(ahead, PRE_ROWS + HALO - k, 0)[:PRE_ROWS]
        dx_ref[...] = dx
        halo = jnp.where(i == 0, 0.0, halo_ref[...])
        xs = jnp.concatenate([halo, x_ref[...]], axis=0)
        for j in range(CONV_WIDTH):
            tap = x_ref[...] if j == CONV_WIDTH - 1 else _shift_rows(xs, CONV_WIDTH - 1 - j)
            dw_ref[pl.ds(j, 1), :] += jnp.sum(dcv * tap, axis=0, keepdims=True)

    sd = jax.ShapeDtypeStruct
    return pl.pallas_call(
        body, name="pre_bwd_conv", grid=(3, n_blocks),
        out_shape=(sd(x.shape, jnp.float32), sd(conv_w.shape, jnp.float32)),
        in_specs=[blk, prev, wts, blk, nxt], out_specs=(blk, wts),
        compiler_params=pltpu.CompilerParams(dimension_semantics=("parallel", "arbitrary")),
    )(x, x, conv_w, dc, dc)


@jax.custom_vjp
def _pre_op(x, conv_w):
    return _pre_fwd_call(x, conv_w)


def _pre_op_fwd(x, conv_w):
    return _pre_fwd_call(x, conv_w), (x, conv_w)


def _pre_op_bwd(res, d_out):
    x, conv_w = res
    return _pre_bwd_conv_call(x, conv_w, _pre_bwd_act_call(x, conv_w, d_out))


_pre_op.defvjp(_pre_op_fwd, _pre_op_bwd)


def _project(h, h_lo, w, slot):
    b, s, d = h.shape
    return _linear(h.reshape(b * s, d), h_lo.reshape(b * s, d), w, slot).reshape(b, s, w.shape[1])


def _rope_table(positions, dh):
    inv_freq = ROPE_THETA ** (-jnp.arange(0, dh, 2, dtype=jnp.float32) / dh)
    ang = positions.astype(jnp.float32)[:, None] * inv_freq
    reps = 128 // (dh // 2)
    return jnp.concatenate([jnp.tile(jnp.cos(ang), (1, reps)), jnp.tile(jnp.sin(ang), (1, reps))], axis=-1)


GATE_ROWS = 1024


def _gate_terms(o_h, z_h):
    r = lax.rsqrt(jnp.mean(o_h * o_h, axis=-1, keepdims=True) + NORM_EPS)
    sig = jax.nn.sigmoid(z_h)
    return r, o_h * r, sig, z_h * sig


def _gate_fwd_call(o, z, nw):
    tok = pl.BlockSpec((GATE_ROWS, DN_V_W), lambda i: (i, 0))
    vec = pl.BlockSpec((1, DN_DV), lambda i: (0, 0))

    def body(o_ref, z_ref, nw_ref, y_ref):
        for h in range(DN_HEADS):
            cols = pl.ds(h * DN_DV, DN_DV)
            _, n_h, _, g_h = _gate_terms(o_ref[:, cols], z_ref[:, cols])
            y_ref[:, cols] = n_h * nw_ref[...] * g_h

    return pl.pallas_call(
        body, name="gate_fwd", grid=(o.shape[0] // GATE_ROWS,),
        out_shape=jax.ShapeDtypeStruct(o.shape, jnp.float32), in_specs=[tok, tok, vec], out_specs=tok,
        compiler_params=pltpu.CompilerParams(dimension_semantics=("parallel",)),
    )(o, z, nw)


def _gate_bwd_call(o, z, nw, dy):
    tok = pl.BlockSpec((GATE_ROWS, DN_V_W), lambda i: (i, 0))
    vec = pl.BlockSpec((1, DN_DV), lambda i: (0, 0))

    def body(o_ref, z_ref, nw_ref, dy_ref, do_ref, dz_ref, dnw_ref):
        @pl.when(pl.program_id(0) == 0)
        def _():
            dnw_ref[...] = jnp.zeros_like(dnw_ref)

        for h in range(DN_HEADS):
            cols = pl.ds(h * DN_DV, DN_DV)
            z_h, dy_h = z_ref[:, cols], dy_ref[:, cols]
            r, n_h, sig, g_h = _gate_terms(o_ref[:, cols], z_h)
            dz_ref[:, cols] = dy_h * n_h * nw_ref[...] * (sig * (1.0 + z_h * (1.0 - sig)))
            dn = dy_h * nw_ref[...] * g_h
            do_ref[:, cols] = r * (dn - n_h * jnp.mean(dn * n_h, axis=-1, keepdims=True))
            dnw_ref[...] += jnp.sum(dy_h * n_h * g_h, axis=0, keepdims=True)

    sd = jax.ShapeDtypeStruct
    return pl.pallas_call(
        body, name="gate_bwd", grid=(o.shape[0] // GATE_ROWS,),
        out_shape=(sd(o.shape, jnp.float32), sd(o.shape, jnp.float32), sd(nw.shape, jnp.float32)),
        in_specs=[tok, tok, vec, tok], out_specs=(tok, tok, vec),
        compiler_params=pltpu.CompilerParams(dimension_semantics=("arbitrary",)),
    )(o, z, nw, dy)


@jax.custom_vjp
def _gate_op(o, z, nw):
    return _gate_fwd_call(o, z, nw)


def _gate_op_fwd(o, z, nw):
    return _gate_fwd_call(o, z, nw), (o, z, nw)


def _gate_op_bwd(res, dy):
    return _gate_bwd_call(*res, dy)


_gate_op.defvjp(_gate_op_fwd, _gate_op_bwd)


_MASKED = -1e30


def _swa_probs(qs, k_h, sinks, valid):
    ss = [jnp.where(valid, _dot(q_h, k_h, 1, 1) * (SWA_DH ** -0.5), _MASKED) for q_h in qs]
    ms = [jnp.maximum(jnp.max(s, axis=-1, keepdims=True), sink) for s, sink in zip(ss, sinks)]
    ps = [jnp.exp(s - m) for s, m in zip(ss, ms)]
    es = [jnp.exp(sink - m) for sink, m in zip(sinks, ms)]
    invs = [1.0 / (jnp.sum(p, axis=-1, keepdims=True) + e) for p, e in zip(ps, es)]
    return [p * inv for p, inv in zip(ps, invs)], [e * inv for e, inv in zip(es, invs)]


def _swa_valid(n):
    qi = lax.broadcasted_iota(jnp.int32, (WINDOW, 2 * WINDOW), 0)
    kj = lax.broadcasted_iota(jnp.int32, (WINDOW, 2 * WINDOW), 1)
    diff = qi + WINDOW - kj
    return (diff >= 0) & (diff < WINDOW) & ((kj >= WINDOW) | (n > 0))


def _rotate_half(x, transpose=False):
    half = SWA_DH // 2
    lower = lax.broadcasted_iota(jnp.int32, x.shape, 1) % SWA_DH < half
    ahead, behind = pltpu.roll(x, 128 - half, 1), pltpu.roll(x, half, 1)
    return jnp.where(lower, ahead, -behind) if transpose else jnp.where(lower, -ahead, behind)


def _rope(x, table):
    return x * table[:, :128] + _rotate_half(x) * table[:, 128:]


def _unrope(dy, table):
    return dy * table[:, :128] + _rotate_half(dy * table[:, 128:], transpose=True)


def _swa_specs():
    qs = pl.BlockSpec((WINDOW, SWA_Q_W), lambda n: (n, 0))
    first = lambda n: jnp.maximum(n - 1, 0)
    kv = [pl.BlockSpec((WINDOW, SWA_KV_W), lambda n: (first(n), 0)), pl.BlockSpec((WINDOW, SWA_KV_W), lambda n: (n, 0)),
          pl.BlockSpec((WINDOW, SWA_KV_W), lambda n: (first(n), 1)), pl.BlockSpec((WINDOW, SWA_KV_W), lambda n: (n, 1))]
    tables = [pl.BlockSpec((WINDOW, 256), lambda n: (first(n), 0)), pl.BlockSpec((WINDOW, 256), lambda n: (n, 0))]
    cur = pl.BlockSpec((WINDOW, SWA_KV_W), lambda n: (n, 0))
    sk = pl.BlockSpec((SWA_HEADS, 1, 128), lambda n: (0, 0, 0))
    return qs, kv, tables, cur, sk


def _swa_load(q_ref, kp_ref, kc_ref, vp_ref, vc_ref, tp_ref, tc_ref):
    table_kk = jnp.concatenate([tp_ref[...], tc_ref[...]], axis=0)
    kk = _rope(jnp.concatenate([kp_ref[...], kc_ref[...]], axis=0), table_kk)
    vv = jnp.concatenate([vp_ref[...], vc_ref[...]], axis=0)
    q_rot = []
    for b in range(SWA_Q_W // 128):
        pair = _rope(q_ref[:, pl.ds(b * 128, 128)], tc_ref[...])
        q_rot += [pair[:, :SWA_DH], pair[:, SWA_DH:]]
    split = lambda t: [t[:, hkv * SWA_DH:(hkv + 1) * SWA_DH] for hkv in range(SWA_KV_HEADS)]
    return q_rot, split(kk), split(vv), table_kk


def _swa_fwd_call(q, kv, table, sinks):
    qs, kvs, tables, _, sk = _swa_specs()

    def body(q_ref, kp_ref, kc_ref, vp_ref, vc_ref, tp_ref, tc_ref, sink_ref, o_ref):
        valid = _swa_valid(pl.program_id(0))
        q_rot, kk, vv, _ = _swa_load(q_ref, kp_ref, kc_ref, vp_ref, vc_ref, tp_ref, tc_ref)
        for hkv in range(SWA_KV_HEADS):
            heads = range(hkv * SWA_GROUP, (hkv + 1) * SWA_GROUP)
            probs, _ = _swa_probs([q_rot[h] for h in heads], kk[hkv], [sink_ref[h][:, :1] for h in heads], valid)
            outs = [_dot(p, vv[hkv]) for p in probs]
            for h, o in zip(heads, outs):
                o_ref[:, pl.ds(h * SWA_DH, SWA_DH)] = o

    return pl.pallas_call(
        body, name="swa_fwd", grid=(q.shape[0] // WINDOW,),
        out_shape=jax.ShapeDtypeStruct(q.shape, jnp.float32),
        in_specs=[qs] + kvs + tables + [sk], out_specs=qs,
        compiler_params=pltpu.CompilerParams(dimension_semantics=("parallel",)),
    )(q, kv, kv, kv, kv, table, table, sinks)


def _swa_bwd_call(q, kv, table, sinks, do):
    qs, kvs, tables, cur, sk = _swa_specs()

    def body(q_ref, kp_ref, kc_ref, vp_ref, vc_ref, tp_ref, tc_ref, sink_ref, do_ref,
             dq_ref, dkc_ref, dkp_ref, dvc_ref, dvp_ref, ds_ref):
        @pl.when(pl.program_id(0) == 0)
        def _():
            ds_ref[...] = jnp.zeros_like(ds_ref)

        valid = _swa_valid(pl.program_id(0))
        q_rot, kk, vv, table_kk = _swa_load(q_ref, kp_ref, kc_ref, vp_ref, vc_ref, tp_ref, tc_ref)
        lane0 = lax.broadcasted_iota(jnp.int32, (1, 128), 1) == 0
        dq_heads, dk_heads, dv_heads = [], [], []
        for hkv in range(SWA_KV_HEADS):
            k_h, v_h = kk[hkv], vv[hkv]
            heads = range(hkv * SWA_GROUP, (hkv + 1) * SWA_GROUP)
            q_hs = [q_rot[h] for h in heads]
            dos = [do_ref[:, pl.ds(h * SWA_DH, SWA_DH)] for h in heads]
            probs, p_sinks = _swa_probs(q_hs, k_h, [sink_ref[h][:, :1] for h in heads], valid)
            dps = [_dot(do_h, v_h, 1, 1) for do_h in dos]
            rss = [jnp.sum(p * dp, axis=-1, keepdims=True) for p, dp in zip(probs, dps)]
            d_ss = [p * (dp - rs) for p, dp, rs in zip(probs, dps, rss)]
            dq_heads += [_dot(d_s, k_h) * (SWA_DH ** -0.5) for d_s in d_ss]
            dks = [_dot(d_s, q_h, 0, 0) for d_s, q_h in zip(d_ss, q_hs)]
            dvs = [_dot(p, do_h, 0, 0) for p, do_h in zip(probs, dos)]
            for h, p_sink, rs in zip(heads, p_sinks, rss):
                d_sink = -jnp.sum(p_sink * rs, axis=0, keepdims=True)
                ds_ref[h] += jnp.where(lane0, d_sink, 0.0)
            dk_heads.append(sum(dks[1:], dks[0]) * (SWA_DH ** -0.5))
            dv_heads.append(sum(dvs[1:], dvs[0]))
        for b in range(SWA_Q_W // 128):
            pair = jnp.concatenate([dq_heads[2 * b], dq_heads[2 * b + 1]], axis=1)
            dq_ref[:, pl.ds(b * 128, 128)] = _unrope(pair, tc_ref[...])
        dk = _unrope(jnp.concatenate(dk_heads, axis=1), table_kk)
        dv = jnp.concatenate(dv_heads, axis=1)
        dkp_ref[...] = dk[:WINDOW]
        dkc_ref[...] = dk[WINDOW:]
        dvp_ref[...] = dv[:WINDOW]
        dvc_ref[...] = dv[WINDOW:]

    sd = jax.ShapeDtypeStruct
    f32 = jnp.float32
    half = (q.shape[0], SWA_KV_W)
    return pl.pallas_call(
        body, name="swa_bwd", grid=(q.shape[0] // WINDOW,),
        out_shape=(sd(q.shape, f32), sd(half, f32), sd(half, f32), sd(half, f32), sd(half, f32), sd(sinks.shape, f32)),
        in_specs=[qs] + kvs + tables + [sk, qs], out_specs=(qs, cur, cur, cur, cur, sk),
        compiler_params=pltpu.CompilerParams(dimension_semantics=("arbitrary",)),
    )(q, kv, kv, kv, kv, table, table, sinks, do)


@jax.custom_vjp
def _swa_op(q, kv, table, sinks):
    return _swa_fwd_call(q, kv, table, sinks)


def _swa_op_fwd(q, kv, table, sinks):
    return _swa_fwd_call(q, kv, table, sinks), (q, kv, table, sinks)


def _swa_op_bwd(res, do):
    q, kv, table, sinks = res
    dq, dkc, dkp, dvc, dvp, dsinks = _swa_bwd_call(q, kv, table, sinks, do)

    def fold(cur, prev):
        return cur + jnp.concatenate([prev[WINDOW:], jnp.zeros_like(prev[:WINDOW])], axis=0)

    return dq, jnp.concatenate([fold(dkc, dkp), fold(dvc, dvp)], axis=1), jnp.zeros_like(table), dsinks


_swa_op.defvjp(_swa_op_fwd, _swa_op_bwd)


def _swa_sink_attention(q, kv, table, sinks):
    return _swa_op(q, kv, table, jnp.broadcast_to(sinks[:, None, None], (SWA_HEADS, 1, 128)))


MEM_ROWS = 1024


def _mem_probs(q_h, k_h):
    s = _dot(q_h, k_h, 1, 1) * (MEM_DH ** -0.5)
    p = jnp.exp(s - jnp.max(s, axis=-1, keepdims=True))
    return p / jnp.sum(p, axis=-1, keepdims=True)


def _mem_fwd_call(qm, kv):
    qs = pl.BlockSpec((MEM_ROWS, MEM_W), lambda i: (i, 0))
    kvs = pl.BlockSpec(kv.shape, lambda i: (0, 0))

    def body(q_ref, kv_ref, o_ref):
        for h in range(MEM_HEADS):
            cols = pl.ds(h * MEM_DH, MEM_DH)
            probs = _mem_probs(q_ref[:, cols], kv_ref[:, cols])
            o_ref[:, cols] = _dot(probs, kv_ref[:, pl.ds(MEM_W + h * MEM_DH, MEM_DH)])

    return pl.pallas_call(
        body, name="mem_fwd", grid=(qm.shape[0] // MEM_ROWS,),
        out_shape=jax.ShapeDtypeStruct(qm.shape, jnp.float32), in_specs=[qs, kvs], out_specs=qs,
        compiler_params=pltpu.CompilerParams(dimension_semantics=("parallel",)),
    )(qm, kv)


def _mem_bwd_call(qm, kv, do):
    qs = pl.BlockSpec((MEM_ROWS, MEM_W), lambda i: (i, 0))
    kvs = pl.BlockSpec(kv.shape, lambda i: (0, 0))

    def body(q_ref, kv_ref, do_ref, dq_ref, dkv_ref):
        @pl.when(pl.program_id(0) == 0)
        def _():
            dkv_ref[...] = jnp.zeros_like(dkv_ref)

        for h in range(MEM_HEADS):
            cols = pl.ds(h * MEM_DH, MEM_DH)
            v_cols = pl.ds(MEM_W + h * MEM_DH, MEM_DH)
            q_h, k_h, do_h = q_ref[:, cols], kv_ref[:, cols], do_ref[:, cols]
            probs = _mem_probs(q_h, k_h)
            dp = _dot(do_h, kv_ref[:, v_cols], 1, 1)
            d_s = probs * (dp - jnp.sum(probs * dp, axis=-1, keepdims=True))
            dq_ref[:, cols] = _dot(d_s, k_h) * (MEM_DH ** -0.5)
            dkv_ref[:, cols] += _dot(d_s, q_h, 0, 0) * (MEM_DH ** -0.5)
            dkv_ref[:, v_cols] += _dot(probs, do_h, 0, 0)

    sd = jax.ShapeDtypeStruct
    return pl.pallas_call(
        body, name="mem_bwd", grid=(qm.shape[0] // MEM_ROWS,),
        out_shape=(sd(qm.shape, jnp.float32), sd(kv.shape, jnp.float32)),
        in_specs=[qs, kvs, qs], out_specs=(qs, kvs),
        compiler_params=pltpu.CompilerParams(dimension_semantics=("arbitrary",)),
    )(qm, kv, do)


@jax.custom_vjp
def _mem_op(qm, kv):
    return _mem_fwd_call(qm, kv)


def _mem_op_fwd(qm, kv):
    return _mem_fwd_call(qm, kv), (qm, kv)


def _mem_op_bwd(res, do):
    return _mem_bwd_call(*res, do)


_mem_op.defvjp(_mem_op_fwd, _mem_op_bwd)


def _memory_attention(qm, kv):
    return _mem_op(qm[0], kv[0])[None]


def _mixer_a(h, h_lo, mem, mem_lo, p, s, layer):
    B, S, _ = h.shape
    proj = _project(h, h_lo, p["a_w_in"][layer], s["a_w_in"][layer])
    c1 = 2 * DN_QK_W + DN_V_W
    qkv = proj[..., :c1]
    z = proj[..., c1:QKVZ_W]
    qm = proj[..., QKVZ_W:QKVZ_W + MEM_W]
    a = proj[..., QKVZ_W + MEM_W:QKVZ_W + MEM_W + DN_HEADS]
    b = proj[..., QKVZ_W + MEM_W + DN_HEADS:QKVZ_W + MEM_W + 2 * DN_HEADS]
    planes = _pre_op(qkv[0], p["a_conv_w"][layer])
    beta = jax.nn.sigmoid(b[0])
    g = -jnp.exp(p["a_A_log"][layer]) * jax.nn.softplus(a[0] + p["a_dt_bias"][layer])
    o = _gate_op(_gated_delta_rule(planes, g, beta), z[0], p["a_norm_w"][layer][None])[None]
    kv = _project(mem, mem_lo, p["mem_w_kv"][layer], s["mem_w_kv"][layer])
    mo = _memory_attention(qm, kv)
    cat = jnp.concatenate([o, mo], axis=-1)
    return _project(cat, _lo(cat), p["w_o"][layer], s["w_o"][layer])


def _mixer_b(h, h_lo, mem, mem_lo, kv_shared, table, p, s, layer):
    j = layer - N_A
    proj = _project(h, h_lo, p["b_w_in"][j], s["b_w_in"][j])
    o = _swa_sink_attention(proj[0, :, :SWA_Q_W], kv_shared, table, p["b_sinks"][j])[None]
    kv = _project(mem, mem_lo, p["mem_w_kv"][layer], s["mem_w_kv"][layer])
    mo = _memory_attention(proj[..., SWA_Q_W:], kv)
    cat = jnp.concatenate([o, mo], axis=-1)
    return _project(cat, _lo(cat), p["w_o"][layer], s["w_o"][layer])


def _forward(p, s, x, mem, positions):
    table = _rope_table(positions[0], SWA_DH)
    h, h_lo, mem_lo = x, _lo(x), _lo(mem)
    kv_shared = None
    for layer in range(DEPTH):
        if layer < N_A:
            mix = _mixer_a(h, h_lo, mem, mem_lo, p, s, layer)
        else:
            mix = _mixer_b(h, h_lo, mem, mem_lo, kv_shared, table, p, s, layer)
        seq = h.shape[1]
        h2, h2_lo = _ln_res(h[0], mix[0], p["ln_g"][layer, 0][None], p["ln_b"][layer, 0][None])
        down = _mlp(h2, h2_lo, p["mlp_w_up"][layer], p["mlp_w_down"][layer], s["mlp_w_up"][layer],
                    s["mlp_w_down"][layer])
        h, h_lo = _ln_res(h2, down, p["ln_g"][layer, 1][None], p["ln_b"][layer, 1][None])
        h, h_lo = h.reshape(1, seq, D_MODEL), h_lo.reshape(1, seq, D_MODEL)
        if layer == N_A - 1:
            kv_shared = _project(h, h_lo, p["w_kv_shared"], s["w_kv_shared"])[0]
    return h


def _loss(diff, s, p, mem, positions, target):
    y = _forward({**p, **diff["small"]}, s, diff["x"], mem, positions)
    return 0.5 * jnp.sum(jnp.mean(jnp.square(y - target), axis=-1))


def _reorder_a_w_in(w):
    pad = jnp.zeros(w.shape[:-1] + (A_IN_PAD - A_IN,), w.dtype)
    return jnp.concatenate([w[..., :QKVZ_W], w[..., QKVZ_W + 2 * DN_HEADS:], w[..., QKVZ_W:QKVZ_W + 2 * DN_HEADS], pad],
                           axis=-1)


def _restore_a_w_in(w):
    return jnp.concatenate([w[..., :QKVZ_W], w[..., QKVZ_W + MEM_W:QKVZ_W + MEM_W + 2 * DN_HEADS],
                            w[..., QKVZ_W:QKVZ_W + MEM_W]], axis=-1)


def kernel(x, mem, positions, a_w_in, a_conv_w, a_A_log, a_dt_bias, a_norm_w, b_w_in, b_sinks, w_kv_shared, mem_w_kv, w_o, mlp_w_up, mlp_w_down, ln_g, ln_b, loss_target, m_a_w_in, m_a_conv_w, m_a_A_log, m_a_dt_bias, m_a_norm_w, m_b_w_in, m_b_sinks, m_w_kv_shared, m_mem_w_kv, m_w_o, m_mlp_w_up, m_mlp_w_down, m_ln_g, m_ln_b, v_a_w_in, v_a_conv_w, v_a_A_log, v_a_dt_bias, v_a_norm_w, v_b_w_in, v_b_sinks, v_w_kv_shared, v_mem_w_kv, v_w_o, v_mlp_w_up, v_mlp_w_down, v_ln_g, v_ln_b):
    w_sh = dict(a_w_in=a_w_in, a_conv_w=a_conv_w, a_A_log=a_A_log, a_dt_bias=a_dt_bias, a_norm_w=a_norm_w,
                b_w_in=b_w_in, b_sinks=b_sinks, w_kv_shared=w_kv_shared, mem_w_kv=mem_w_kv, w_o=w_o,
                mlp_w_up=mlp_w_up, mlp_w_down=mlp_w_down, ln_g=ln_g, ln_b=ln_b)
    m_sh = dict(a_w_in=m_a_w_in, a_conv_w=m_a_conv_w, a_A_log=m_a_A_log, a_dt_bias=m_a_dt_bias, a_norm_w=m_a_norm_w,
                b_w_in=m_b_w_in, b_sinks=m_b_sinks, w_kv_shared=m_w_kv_shared, mem_w_kv=m_mem_w_kv, w_o=m_w_o,
                mlp_w_up=m_mlp_w_up, mlp_w_down=m_mlp_w_down, ln_g=m_ln_g, ln_b=m_ln_b)
    v_sh = dict(a_w_in=v_a_w_in, a_conv_w=v_a_conv_w, a_A_log=v_a_A_log, a_dt_bias=v_a_dt_bias, a_norm_w=v_a_norm_w,
                b_w_in=v_b_w_in, b_sinks=v_b_sinks, w_kv_shared=v_w_kv_shared, mem_w_kv=v_mem_w_kv, w_o=v_w_o,
                mlp_w_up=v_mlp_w_up, mlp_w_down=v_mlp_w_down, ln_g=v_ln_g, ln_b=v_ln_b)
    shard_shapes = {n: w_sh[n].shape for n in WEIGHTS}
    rb, rows = _rows_for(w_sh)

    big, small = _pack(w_sh, rb, jnp.bfloat16)
    gbig, gsmall = _gather_weights(big.reshape(2, rb // 2, FLAT_W), small.reshape(2, SMALL_ROWS // 2, FLAT_W))
    gbig, gsmall = gbig.reshape(N_CHIPS, rb, FLAT_W), gsmall.reshape(N_CHIPS, SMALL_ROWS, FLAT_W)
    pieces = [_unpack(gbig[q], gsmall[q], shard_shapes) for q in range(N_CHIPS)]
    full = {n: jnp.concatenate([pieces[q][n] for q in range(N_CHIPS)], axis=SHARD_AXIS[n]) for n in SHARD_AXIS}
    for n in REPLICATED:
        full[n] = w_sh[n]
    big_w = {n: full[n] for n in BIG}
    big_w["a_w_in"] = _reorder_a_w_in(big_w["a_w_in"])
    small_w = {n: full[n] for n in SMALL}
    slots = {n: jnp.zeros(big_w[n].shape, jnp.float32) for n in BIG}

    loss, (grads, g_slots) = jax.value_and_grad(_loss, argnums=(0, 1))(
        {"x": x, "small": small_w}, slots, big_w, mem, positions, loss_target)
    loss = lax.psum(loss, ("x", "y", "c"))
    g_full = {**g_slots, **grads["small"]}
    g_full["a_w_in"] = _restore_a_w_in(g_full["a_w_in"])

    def shard_of(n, q):
        if n in REPLICATED:
            return g_full[n]
        size = shard_shapes[n][SHARD_AXIS[n]]
        return lax.slice_in_dim(g_full[n], q * size, (q + 1) * size, axis=SHARD_AXIS[n])

    parts = []
    for q in range(N_CHIPS):
        pb, ps = _pack({n: shard_of(n, q) for n in WEIGHTS}, rb, jnp.bfloat16)
        parts.append(jnp.concatenate([pb, ps.astype(jnp.bfloat16)], axis=0).reshape(2, rows // 2, FLAT_W))
    partials = jnp.stack(parts, axis=1)
    half = lax.axis_index("c").astype(jnp.int32).reshape(1)
    chip_partials = _add_pairs(partials, _swap_halves(partials), half)
    g_flat = _join_halves(_sum_chips(_scatter_grads(chip_partials))).reshape(rows, FLAT_W)

    odd = "a_w_in"
    blank = jnp.zeros(shard_shapes[odd], jnp.float32)
    flat = [jnp.concatenate(_pack({**d, odd: blank}, rb), axis=0) for d in (w_sh, m_sh, v_sh)]
    outs = (g_flat,) + tuple(_adamw(g_flat, *flat))
    g_o, d_o, m_o, v_o = [_unpack(o[:rb], o[rb:], shard_shapes) for o in outs]
    as_rows = lambda t: t.reshape(-1, t.shape[-1])
    updated = _adamw(as_rows(g_o[odd]), as_rows(w_sh[odd]), as_rows(m_sh[odd]), as_rows(v_sh[odd]))
    d_o[odd], m_o[odd], v_o[odd] = [t.reshape(shard_shapes[odd]) for t in updated]
    return (loss, grads["x"], *[g_o[n] for n in WEIGHTS], *[d_o[n] for n in WEIGHTS],
            *[m_o[n] for n in WEIGHTS], *[v_o[n] for n in WEIGHTS])
```
